```python
import math
import jax
import jax.numpy as jnp
from jax import lax
import numpy as np

D_MODEL = 1024
BATCH = 4
SEQ = 8192
DEPTH = 2

NSA_HEADS = 8
NSA_KV_GROUPS = 2
NSA_HEAD_DIM = 64
NSA_CMP_LEN = 32
NSA_CMP_STRIDE = 16
NSA_SEL_LEN = 64
NSA_TOP_N = 16
NSA_WINDOW = 512
NSA_PHI_HIDDEN = 128
NSA_QBLOCK = 128
FORCE_BONUS = 1.0e4
POOL_GROUPS = 4
POOL_GROUP_DIM = 128
POOL_WINDOWS = (2, 4, 8, 16)
DN_HEADS = 4
DN_HEAD_DIM = 128
DN_CONV = 4
DN_CHUNK = 64
D_FF = 4 * D_MODEL
N_BRANCH = 3
EPS = 1e-6
NEG = -1e30

NSA_Q_W = NSA_HEADS * NSA_HEAD_DIM
NSA_KV_W = NSA_KV_GROUPS * NSA_HEAD_DIM
POOL_W = POOL_GROUPS * POOL_GROUP_DIM
DN_W = DN_HEADS * DN_HEAD_DIM
IN_SIZES = (NSA_Q_W, NSA_KV_W, NSA_KV_W, NSA_KV_W, NSA_KV_W, NSA_KV_W, NSA_KV_W,
            3 * NSA_HEADS, POOL_W, 3 * DN_W, DN_W, DN_HEADS, DN_HEADS, N_BRANCH * D_MODEL)
N_IN = sum(IN_SIZES)

kernel_name = "hybrid_nsa_pool_deltanet_block"


def rmsnorm(x, g):
    xf = x.astype(jnp.float32)
    y = xf * lax.rsqrt(jnp.mean(xf * xf, axis=-1, keepdims=True) + EPS)
    return (y * g.astype(jnp.float32)).astype(x.dtype)


def l2norm(t):
    return t * lax.rsqrt(jnp.sum(t * t, axis=-1, keepdims=True) + EPS)


def nsa_compress(k, pos, w1, w2):
    B, S, G, dh = k.shape
    nc = (S - NSA_CMP_LEN) // NSA_CMP_STRIDE + 1
    idx = jnp.arange(nc)[:, None] * NSA_CMP_STRIDE + jnp.arange(NSA_CMP_LEN)[None, :]
    blk = k[:, idx] + pos[None, None, :, None, :]
    blk = blk.transpose(0, 1, 3, 2, 4).reshape(B, nc, G, NSA_CMP_LEN * dh)
    return jax.nn.silu(blk @ w1) @ w2


def nsa_attention(q, kc, vc, ks, vs, kw, vw, gates, slopes):
    B, S, H, dh = q.shape
    G = NSA_KV_GROUPS
    R = H // G
    Q = NSA_QBLOCK
    nc = kc.shape[1]
    nsel = S // NSA_SEL_LEN
    top_n = min(NSA_TOP_N, nsel)
    f32 = jnp.float32
    q = q.reshape(B, S, G, R, dh) * (dh ** -0.5)
    gates = gates.reshape(B, S, G, R, 3)
    m = slopes.reshape(G, R)[None, :, :, None, None]
    c_start = jnp.arange(nc) * NSA_CMP_STRIDE
    c_end = c_start + NSA_CMP_LEN - 1
    s_start = jnp.arange(nsel) * NSA_SEL_LEN
    blk_id = jnp.arange(nsel)
    overlap = (jnp.clip(jnp.minimum(s_start[:, None] + NSA_SEL_LEN, c_start[None, :] + NSA_CMP_LEN)
                        - jnp.maximum(s_start[:, None], c_start[None, :]), 0)
               / NSA_CMP_STRIDE).astype(f32)
    ks_t = ks.transpose(0, 2, 1, 3)
    vs_t = vs.transpose(0, 2, 1, 3)
    pad = ((0, 0), (NSA_WINDOW, 0), (0, 0), (0, 0))
    kw_p = jnp.pad(kw, pad)
    vw_p = jnp.pad(vw, pad)
    gather = jax.vmap(jax.vmap(lambda a, i: a[i]))

    def block(qb):
        t0 = qb * Q
        t = t0 + jnp.arange(Q)
        qq = lax.dynamic_slice_in_dim(q, t0, Q, axis=1)
        gb = lax.dynamic_slice_in_dim(gates, t0, Q, axis=1)
        vis = c_end[None, :] <= t[:, None]
        dist = (t[:, None] - c_end[None, :]).astype(f32)
        lg = jnp.einsum('bqgrd,bngd->bgrqn', qq, kc).astype(f32) - m * dist
        p_c = jnp.where(vis, jax.nn.softmax(jnp.where(vis, lg, NEG), axis=-1), 0.0)
        o_c = jnp.einsum('bgrqn,bngd->bqgrd', p_c.astype(vc.dtype), vc)
        imp = jnp.einsum('bgrqn,jn->bgqj', p_c, overlap)
        valid = s_start[None, :] <= t[:, None]
        forced = (blk_id[None, :] == 0) | (blk_id[None, :] == (t // NSA_SEL_LEN)[:, None])
        score = jnp.where(valid, imp + FORCE_BONUS * forced, NEG)
        _, idx = lax.top_k(score, top_n)
        pos = (idx[..., None] * NSA_SEL_LEN + jnp.arange(NSA_SEL_LEN)).reshape(B, G, Q * top_n * NSA_SEL_LEN)
        k_sel = gather(ks_t, pos).reshape(B, G, Q, top_n * NSA_SEL_LEN, dh)
        v_sel = gather(vs_t, pos).reshape(B, G, Q, top_n * NSA_SEL_LEN, dh)
        pos = pos.reshape(B, G, Q, top_n * NSA_SEL_LEN)
        dsel = t[None, None, :, None] - pos
        lg = jnp.einsum('bqgrd,bgqkd->bgrqk', qq, k_sel).astype(f32) - m * dsel[:, :, None].astype(f32)
        p_s = jax.nn.softmax(jnp.where((dsel >= 0)[:, :, None], lg, NEG), axis=-1)
        o_s = jnp.einsum('bgrqk,bgqkd->bqgrd', p_s.astype(v_sel.dtype), v_sel)
        kwin = lax.dynamic_slice_in_dim(kw_p, t0, NSA_WINDOW + Q, axis=1)
        vwin = lax.dynamic_slice_in_dim(vw_p, t0, NSA_WINDOW + Q, axis=1)
        s = t0 - NSA_WINDOW + jnp.arange(NSA_WINDOW + Q)
        dw = t[:, None] - s[None, :]
        okw = (dw >= 0) & (dw < NSA_WINDOW) & (s[None, :] >= 0)
        lg = jnp.einsum('bqgrd,bkgd->bgrqk', qq, kwin).astype(f32) - m * dw.astype(f32)
        p_w = jax.nn.softmax(jnp.where(okw, lg, NEG), axis=-1)
        o_w = jnp.einsum('bgrqk,bkgd->bqgrd', p_w.astype(vwin.dtype), vwin)
        o = gb[..., 0:1] * o_c + gb[..., 1:2] * o_s + gb[..., 2:3] * o_w
        return o.reshape(B, Q, H * dh)

    out = lax.map(block, jnp.arange(S // Q))
    return out.transpose(1, 0, 2, 3).reshape(B, S, H * dh)


def multiscale_pool(p, w, scale):
    B, S, _ = p.shape
    f32 = jnp.float32
    pg = p.reshape(B, S, POOL_GROUPS, POOL_GROUP_DIM).astype(f32)
    cs = jnp.pad(jnp.cumsum(pg, axis=1), ((0, 0), (1, 0), (0, 0), (0, 0)))
    win = jnp.array(POOL_WINDOWS)
    t = jnp.arange(S)
    lo = jnp.maximum(t[:, None] + 1 - win[None, :], 0)
    cnt = jnp.minimum(t[:, None] + 1, win[None, :]).astype(f32)
    lower = cs[:, lo, jnp.arange(POOL_GROUPS)[None, :]]
    y = ((cs[:, 1:] - lower) / cnt[None, :, :, None] - pg).astype(p.dtype)
    y = jnp.einsum('bsgc,gcd->bsgd', y, w).reshape(B, S, POOL_W)
    return y * scale


def causal_dwconv(x, w):
    K, C = w.shape
    return lax.conv_general_dilated(x, w[:, None, :], window_strides=(1,), padding=[(K - 1, 0)],
                                    dimension_numbers=('NWC', 'WIO', 'NWC'), feature_group_count=C)


def gated_deltanet(qkv, z, beta_logit, a_logit, conv_w, A_log, dt_bias, norm_g):
    B, S, _ = qkv.shape
    H, dk, C = DN_HEADS, DN_HEAD_DIM, DN_CHUNK
    nC = S // C
    f32 = jnp.float32
    qkv = jax.nn.silu(causal_dwconv(qkv, conv_w)).astype(f32)
    q, k, v = jnp.split(qkv, 3, axis=-1)
    q = l2norm(q.reshape(B, S, H, dk)) * (dk ** -0.5)
    k = l2norm(k.reshape(B, S, H, dk))
    v = v.reshape(B, S, H, dk)
    beta = jax.nn.sigmoid(beta_logit.astype(f32))
    g = -jnp.exp(A_log.astype(f32)) * jax.nn.softplus(a_logit.astype(f32) + dt_bias.astype(f32))

    def chunks(a):
        return jnp.moveaxis(a.reshape(B, nC, C, H, *a.shape[3:]), 3, 1)

    q, k, v, beta, g = chunks(q), chunks(k), chunks(v), chunks(beta), chunks(g)
    gc = jnp.cumsum(g, axis=-1)
    tril = jnp.tril(jnp.ones((C, C), bool))
    strict = jnp.tril(jnp.ones((C, C), bool), -1)
    diff = gc[..., :, None] - gc[..., None, :]
    Lmask = jnp.where(tril, jnp.exp(jnp.where(tril, diff, 0.0)), 0.0)
    kb = k * beta[..., None]
    vb = v * beta[..., None]
    N = jnp.where(strict, jnp.einsum('bhnid,bhnjd->bhnij', kb, k) * Lmask, 0.0)
    A = N + jnp.eye(C, dtype=f32)
    rhs = jnp.concatenate([vb, kb * jnp.exp(gc)[..., None]], axis=-1)
    sol = lax.linalg.triangular_solve(A, rhs, left_side=True, lower=True)
    u, w = sol[..., :dk], sol[..., dk:]
    Aqk = jnp.einsum('bhnid,bhnjd->bhnij', q, k) * Lmask
    glast = gc[..., -1]
    q_e = q * jnp.exp(gc)[..., None]
    k_e = k * jnp.exp(glast[..., None] - gc)[..., None]

    def step(state, xs):
        q_c, k_c, u_c, w_c, a_c, d_c = xs
        v_new = u_c - w_c @ state
        o = q_c @ state + a_c @ v_new
        state = state * d_c[..., None, None] + jnp.swapaxes(k_c, -1, -2) @ v_new
        return state, o

    xs = tuple(jnp.moveaxis(a, 2, 0) for a in (q_e, k_e, u, w, Aqk, jnp.exp(glast)))
    _, o = lax.scan(step, jnp.zeros((B, H, dk, dk), f32), xs)
    o = o.transpose(1, 0, 3, 2, 4).reshape(B, S, H, dk)
    o = rmsnorm(o, norm_g) * jax.nn.silu(z.reshape(B, S, H, dk).astype(f32))
    return o.reshape(B, S, H * dk).astype(z.dtype)


def hybrid_mixer(h, w_in, phi_k1, phi_k2, phi_v1, phi_v2, pos_k, pos_v, pool_w, pool_scale,
                 dn_conv_w, dn_A_log, dn_dt_bias, dn_norm_g, w_bn, w_bp, w_bd, w_out, slopes):
    B, S, _ = h.shape
    proj = h @ w_in
    points = np.cumsum(IN_SIZES)[:-1].tolist()
    (nq, nkc, nvc, nks, nvs, nkw, nvw, ngate, pin, dqkv, dz, dbeta, da, mg) = jnp.split(proj, points, axis=-1)
    G = NSA_KV_GROUPS
    rs = lambda a, n: a.reshape(B, S, n, -1)
    kc = nsa_compress(rs(nkc, G), pos_k, phi_k1, phi_k2)
    vc = nsa_compress(rs(nvc, G), pos_v, phi_v1, phi_v2)
    y_a = nsa_attention(rs(nq, NSA_HEADS), kc, vc, rs(nks, G), rs(nvs, G), rs(nkw, G), rs(nvw, G),
                        jax.nn.sigmoid(ngate).reshape(B, S, NSA_HEADS, 3), slopes)
    y_b = multiscale_pool(pin, pool_w, pool_scale)
    y_c = gated_deltanet(dqkv, dz, dbeta, da, dn_conv_w, dn_A_log, dn_dt_bias, dn_norm_g)
    g_a, g_b, g_c = jnp.split(jax.nn.sigmoid(mg), 3, axis=-1)
    merged = g_a * (y_a @ w_bn) + g_b * (y_b @ w_bp) + g_c * (y_c @ w_bd)
    return merged @ w_out


def setup_inputs(seed: int = 0) -> dict:
    key = jax.random.key(seed)
    ks = jax.random.split(key, 32)
    f32 = jnp.float32
    L = DEPTH
    nrm = lambda k, shape, s: jax.random.normal(k, shape, f32) * s
    dt = jnp.exp(jax.random.uniform(ks[20], (L, DN_HEADS), f32, math.log(1e-3), math.log(1e-1)))
    return {
        "x": nrm(ks[0], (BATCH, SEQ, D_MODEL), 1.0),
        "c": nrm(ks[1], (BATCH, D_MODEL), 1.0),
        "ada_w": nrm(ks[2], (L, D_MODEL, 6 * D_MODEL), 0.25 * D_MODEL ** -0.5),
        "ada_b": nrm(ks[3], (L, 6 * D_MODEL), 0.01),
        "norm1_g": 1.0 + nrm(ks[4], (L, D_MODEL), 0.02),
        "norm2_g": 1.0 + nrm(ks[5], (L, D_MODEL), 0.02),
        "w_in": nrm(ks[6], (L, D_MODEL, N_IN), D_MODEL ** -0.5),
        "phi_k1": nrm(ks[7], (L, NSA_CMP_LEN * NSA_HEAD_DIM, NSA_PHI_HIDDEN), (NSA_CMP_LEN * NSA_HEAD_DIM) ** -0.5),
        "phi_k2": nrm(ks[8], (L, NSA_PHI_HIDDEN, NSA_HEAD_DIM), NSA_PHI_HIDDEN ** -0.5),
        "phi_v1": nrm(ks[9], (L, NSA_CMP_LEN * NSA_HEAD_DIM, NSA_PHI_HIDDEN), (NSA_CMP_LEN * NSA_HEAD_DIM) ** -0.5),
        "phi_v2": nrm(ks[10], (L, NSA_PHI_HIDDEN, NSA_HEAD_DIM), NSA_PHI_HIDDEN ** -0.5),
        "pos_k": nrm(ks[11], (L, NSA_CMP_LEN, NSA_HEAD_DIM), 0.1),
        "pos_v": nrm(ks[12], (L, NSA_CMP_LEN, NSA_HEAD_DIM), 0.1),
        "pool_w": nrm(ks[13], (L, POOL_GROUPS, POOL_GROUP_DIM, POOL_GROUP_DIM), POOL_GROUP_DIM ** -0.5),
        "pool_scale": 1.0 + nrm(ks[14], (L, POOL_W), 0.02),
        "dn_conv_w": nrm(ks[15], (L, DN_CONV, 3 * DN_W), DN_CONV ** -0.5),
        "dn_A_log": jnp.log(jax.random.uniform(ks[16], (L, DN_HEADS), f32, 1.0, 16.0)),
        "dn_dt_bias": dt + jnp.log(-jnp.expm1(-dt)),
        "dn_norm_g": 1.0 + nrm(ks[17], (L, DN_HEAD_DIM), 0.02),
        "w_branch_nsa": nrm(ks[18], (L, NSA_Q_W, D_MODEL), NSA_Q_W ** -0.5),
        "w_branch_pool": nrm(ks[19], (L, POOL_W, D_MODEL), POOL_W ** -0.5),
        "w_branch_dn": nrm(ks[21], (L, DN_W, D_MODEL), DN_W ** -0.5),
        "w_out": nrm(ks[22], (L, D_MODEL, D_MODEL), D_MODEL ** -0.5),
        "mlp_w1": nrm(ks[23], (L, D_MODEL, D_FF), D_MODEL ** -0.5),
        "mlp_w2": nrm(ks[24], (L, D_FF, D_MODEL), D_FF ** -0.5),
        "final_g": 1.0 + nrm(ks[25], (D_MODEL,), 0.02),
    }


def reference(x, c, ada_w, ada_b, norm1_g, norm2_g, w_in, phi_k1, phi_k2, phi_v1, phi_v2, pos_k, pos_v,
              pool_w, pool_scale, dn_conv_w, dn_A_log, dn_dt_bias, dn_norm_g, w_branch_nsa, w_branch_pool,
              w_branch_dn, w_out, mlp_w1, mlp_w2, final_g):
    slopes = 2.0 ** (-(8.0 / NSA_HEADS) * (jnp.arange(NSA_HEADS, dtype=jnp.float32) + 1.0))
    cond = jax.nn.silu(c)
    for l in range(DEPTH):
        mod = (cond @ ada_w[l] + ada_b[l])[:, None, :]
        sh1, sc1, gt1, sh2, sc2, gt2 = jnp.split(mod, 6, axis=-1)
        h = rmsnorm(x, norm1_g[l]) * (1.0 + sc1) + sh1
        y = hybrid_mixer(h, w_in[l], phi_k1[l], phi_k2[l], phi_v1[l], phi_v2[l], pos_k[l], pos_v[l],
                         pool_w[l], pool_scale[l], dn_conv_w[l], dn_A_log[l], dn_dt_bias[l], dn_norm_g[l],
                         w_branch_nsa[l], w_branch_pool[l], w_branch_dn[l], w_out[l], slopes)
        x = x + gt1 * y
        h = rmsnorm(x, norm2_g[l]) * (1.0 + sc2) + sh2
        x = x + gt2 * (jnp.square(jax.nn.relu(h @ mlp_w1[l])) @ mlp_w2[l])
    return rmsnorm(x, final_g)
```

```python
import functools
import math

import jax
import jax.numpy as jnp
from jax import lax
from jax.experimental import pallas as pl
from jax.experimental.pallas import tpu as pltpu

F32 = jnp.float32
BF16 = jnp.bfloat16
HIGHEST = lax.Precision.HIGHEST

D_MODEL = 1024
DEPTH = 2
NSA_HEADS = 8
NSA_GROUPS = 2
NSA_REP = NSA_HEADS // NSA_GROUPS
NSA_DH = 64
CMP_LEN = 32
CMP_STRIDE = 16
SEL_LEN = 64
SEL_SHIFT = 6
TOP_N = 16
WINDOW = 512
FORCE_BONUS = 1.0e4
POOL_WINDOWS = (2, 4, 8, 16)
POOL_GROUP_DIM = 128
POOL_W = len(POOL_WINDOWS) * POOL_GROUP_DIM
DN_HEADS = 4
DN_DK = 128
DN_W = DN_HEADS * DN_DK
DN_CONV = 4
D_FF = 4 * D_MODEL
EPS = 1e-6
NEG = -1e30
NSA_Q_W = NSA_HEADS * NSA_DH
NSA_KV_W = NSA_GROUPS * NSA_DH
IN_SIZES = (NSA_Q_W, NSA_KV_W, NSA_KV_W, NSA_KV_W, NSA_KV_W, NSA_KV_W, NSA_KV_W,
            3 * NSA_HEADS, POOL_W, 3 * DN_W, DN_W, DN_HEADS, DN_HEADS, 3 * D_MODEL)

LANES = 128
SUBLANES = 8
VMEM_LIMIT_BYTES = 48 * 1024 * 1024

Q_TILE = 128
KV_CHUNK = 512
DN_CHUNK = 128
PROJ_TM = 1024
MLP_TM = 1024
MLP_TF = 1024
MERGE_TM = 512
POOL_TS = 1024
POOL_HALO = 16
DN_HALO = 8

COL_Q = 0
COL_POOL = 512
COL_Z = 1024
COL_QKV = 1536
COL_MG = 3072
COL_KC = 6144
COL_VC = 6272
COL_GATE = 6400
N_MAIN = 6656
N_KV = 1024
LANE_BETA = 16
LANE_DECAY = 20
PROJ_TN = 1664


def _cparams(sem):
    return pltpu.CompilerParams(dimension_semantics=sem, vmem_limit_bytes=VMEM_LIMIT_BYTES)


def _silu(v):
    return v * jax.nn.sigmoid(v)


def _mm(a, b):
    return jnp.dot(a.astype(BF16), b.astype(BF16), preferred_element_type=F32)


def _mm_nt(a, b):
    return lax.dot_general(a.astype(BF16), b.astype(BF16), (((1,), (1,)), ((), ())), preferred_element_type=F32)


def _mm_tn(a, b):
    return lax.dot_general(a.astype(BF16), b.astype(BF16), (((0,), (0,)), ((), ())), preferred_element_type=F32)


def _mod_kernel(c_ref, w_ref, b_ref, o_ref):
    cond = _silu(c_ref[...])
    o_ref[...] = jnp.dot(cond, w_ref[...], preferred_element_type=F32, precision=HIGHEST) + b_ref[...]


def _mod_call(c_pad, w, b):
    d, n = w.shape
    tn = 1536
    return pl.pallas_call(
        _mod_kernel,
        grid=(n // tn,),
        in_specs=[pl.BlockSpec((SUBLANES, d), lambda j: (0, 0)),
                  pl.BlockSpec((d, tn), lambda j: (0, j)),
                  pl.BlockSpec((1, tn), lambda j: (0, j))],
        out_specs=pl.BlockSpec((SUBLANES, tn), lambda j: (0, j)),
        out_shape=jax.ShapeDtypeStruct((SUBLANES, n), F32),
        compiler_params=_cparams(("parallel",)),
        name="mod",
    )(c_pad, w, b.reshape(1, n))


def _norm_mod(x, g, sc, sh):
    ms = jnp.mean(x * x, axis=-1, keepdims=True)
    return (x * lax.rsqrt(ms + EPS) * g) * (1.0 + sc) + sh


def _proj_kernel(x_ref, g_ref, sc_ref, sh_ref, w_ref, o_ref, h_scr):
    @pl.when(pl.program_id(1) == 0)
    def _():
        h_scr[...] = _norm_mod(x_ref[...], g_ref[...], sc_ref[0], sh_ref[0]).astype(BF16)

    o_ref[...] = jnp.dot(h_scr[...], w_ref[...], preferred_element_type=F32).astype(o_ref.dtype)


def _proj_call(x2, g, sc, sh, w, seq, out_dtype, tn, name):
    t, d = x2.shape
    n = w.shape[1]
    tm = min(PROJ_TM, seq)
    per_b = seq // tm
    return pl.pallas_call(
        _proj_kernel,
        grid=(t // tm, n // tn),
        in_specs=[pl.BlockSpec((tm, d), lambda i, j: (i, 0)),
                  pl.BlockSpec((1, d), lambda i, j: (0, 0)),
                  pl.BlockSpec((1, 1, d), lambda i, j: (i // per_b, 0, 0)),
                  pl.BlockSpec((1, 1, d), lambda i, j: (i // per_b, 0, 0)),
                  pl.BlockSpec((d, tn), lambda i, j: (0, j))],
        out_specs=pl.BlockSpec((tm, tn), lambda i, j: (i, j)),
        out_shape=jax.ShapeDtypeStruct((t, n), out_dtype),
        scratch_shapes=[pltpu.VMEM((tm, d), BF16)],
        compiler_params=_cparams(("parallel", "arbitrary")),
        name=name,
    )(x2, g.reshape(1, d), sc, sh, w)


def _compress_kernel(rows_ref, pos_ref, w1_ref, w2_ref, o_ref):
    rows = rows_ref[0, 0, 0]
    half = rows.shape[1]
    nr = rows.shape[0]
    w1 = w1_ref[0]
    top = jnp.dot(rows, w1[:half], preferred_element_type=F32)
    bot = jnp.dot(rows, w1[half:], preferred_element_type=F32)
    posb = jnp.dot(pos_ref[0], w1, preferred_element_type=F32)[0:1]
    hid = top + pltpu.roll(bot, nr - 1, 0) + posb
    o_ref[0, 0, 0] = _mm(_silu(hid), w2_ref[0]).astype(o_ref.dtype)


def _compress_call(rows, pos, w1, w2):
    two, b, g, nr, wdt = rows.shape
    hid = w1.shape[2]
    return pl.pallas_call(
        _compress_kernel,
        grid=(two, b, g),
        in_specs=[pl.BlockSpec((1, 1, 1, nr, wdt), lambda a, i, j: (a, i, j, 0, 0)),
                  pl.BlockSpec((1, SUBLANES, 2 * wdt), lambda a, i, j: (a, 0, 0)),
                  pl.BlockSpec((1, 2 * wdt, hid), lambda a, i, j: (a, 0, 0)),
                  pl.BlockSpec((1, hid, LANES), lambda a, i, j: (a, 0, 0))],
        out_specs=pl.BlockSpec((1, 1, 1, nr, LANES), lambda a, i, j: (a, i, j, 0, 0)),
        out_shape=jax.ShapeDtypeStruct((two, b, g, nr, LANES), BF16),
        compiler_params=_cparams(("parallel", "parallel", "parallel")),
        name="compress",
    )(rows, pos, w1, w2)


def _stack_heads(q):
    qa = q[:, :LANES]
    qb = q[:, LANES:]
    lo = lax.broadcasted_iota(jnp.int32, qa.shape, 1) < NSA_DH
    z = jnp.zeros_like(qa)
    return jnp.concatenate([jnp.where(lo, qa, z), jnp.where(lo, z, qa),
                            jnp.where(lo, qb, z), jnp.where(lo, z, qb)], axis=0)


def _unstack_heads(o, gates):
    nq = o.shape[0] // NSA_REP
    lo = lax.broadcasted_iota(jnp.int32, (nq, LANES), 1) < NSA_DH
    a = jnp.where(lo, gates[0] * o[0:nq], gates[1] * o[nq:2 * nq])
    b = jnp.where(lo, gates[2] * o[2 * nq:3 * nq], gates[3] * o[3 * nq:])
    return jnp.concatenate([a, b], axis=1)


def _gate_cols(gates_ref, branch):
    sg = jax.nn.sigmoid(gates_ref[...])
    return [sg[:, 3 * r + branch:3 * r + branch + 1] for r in range(NSA_REP)]


def _select_kernel(slopes_ref, q_ref, kc_ref, vc_ref, gates_ref, ov_ref, sel_ref, act_ref, yc_ref):
    g = pl.program_id(1)
    t0 = pl.program_id(2) * Q_TILE
    nq = Q_TILE
    qs = _stack_heads(q_ref[...] * (NSA_DH ** -0.5)).astype(BF16)
    kc = kc_ref[0, 0]
    vc = vc_ref[0, 0]
    ncp = kc.shape[0]
    s = _mm_nt(qs, kc)
    c_end = lax.broadcasted_iota(jnp.int32, (nq, ncp), 1) * CMP_STRIDE + (CMP_LEN - 1)
    tq = t0 + lax.broadcasted_iota(jnp.int32, (nq, ncp), 0)
    vis = c_end <= tq
    rel = (c_end[0:1] - t0).astype(F32)
    psum = jnp.zeros((nq, ncp), F32)
    ps = []
    for r in range(NSA_REP):
        m_r = slopes_ref[g * NSA_REP + r]
        sr = jnp.where(vis, s[r * nq:(r + 1) * nq] + m_r * rel, NEG)
        mx = jnp.max(sr, axis=-1, keepdims=True)
        e = jnp.where(vis, jnp.exp(sr - mx), 0.0)
        den = jnp.sum(e, axis=-1, keepdims=True)
        p = e * (1.0 / jnp.where(den > 0.0, den, 1.0))
        psum = psum + p
        ps.append(p.astype(BF16))
    o_c = jnp.dot(jnp.concatenate(ps, axis=0), vc, preferred_element_type=F32)
    yc_ref[...] = _unstack_heads(o_c, _gate_cols(gates_ref, 0))

    p_hi = psum.astype(BF16)
    p_lo = (psum - p_hi.astype(F32)).astype(BF16)
    ov = ov_ref[...]
    imp_t = _mm_nt(ov, p_hi) + _mm_nt(ov, p_lo)
    nb = ov.shape[0]
    jb = lax.broadcasted_iota(jnp.int32, (nb, nq), 0)
    tql = t0 + lax.broadcasted_iota(jnp.int32, (nb, nq), 1)
    valid = jb * SEL_LEN <= tql
    forced = (jb == 0) | (jb == jnp.right_shift(tql, SEL_SHIFT))
    score = jnp.where(valid, imp_t + FORCE_BONUS * forced.astype(F32), NEG)
    sel = jnp.zeros((nb, nq), jnp.bool_)
    for _ in range(min(TOP_N, nb)):
        mx = jnp.max(score, axis=0, keepdims=True)
        idx = jnp.min(jnp.where(score == mx, jb, nb), axis=0, keepdims=True)
        hit = jb == idx
        sel = sel | hit
        score = jnp.where(hit, -jnp.inf, score)
    sel_t = jnp.where(sel & valid, 1.0, 0.0)
    sel_q = sel_t.T
    sel_ref[0, 0] = sel_q.astype(sel_ref.dtype)
    act_ref[0, 0, 0] = jnp.max(sel_q, axis=0, keepdims=True)


def _select_call(slopes, main, kcmp, vcmp, ov, bsz, seq):
    nqt = seq // Q_TILE
    nb, ncp = ov.shape
    grid = (bsz, NSA_GROUPS, nqt)
    row = lambda b, g, q: b * nqt + q
    return pl.pallas_call(
        _select_kernel,
        grid=grid,
        in_specs=[pl.BlockSpec(memory_space=pltpu.SMEM),
                  pl.BlockSpec((Q_TILE, 256), lambda b, g, q: (row(b, g, q), COL_Q // 256 + g)),
                  pl.BlockSpec((1, 1, ncp, LANES), lambda b, g, q: (b, g, 0, 0)),
                  pl.BlockSpec((1, 1, ncp, LANES), lambda b, g, q: (b, g, 0, 0)),
                  pl.BlockSpec((Q_TILE, LANES), lambda b, g, q: (row(b, g, q), COL_GATE // LANES + g)),
                  pl.BlockSpec((nb, ncp), lambda b, g, q: (0, 0))],
        out_specs=[pl.BlockSpec((1, 1, Q_TILE, nb), lambda b, g, q: (b, g, q, 0)),
                   pl.BlockSpec((1, 1, 1, 1, nb), lambda b, g, q: (b, g, q, 0, 0)),
                   pl.BlockSpec((Q_TILE, 256), lambda b, g, q: (row(b, g, q), g))],
        out_shape=[jax.ShapeDtypeStruct((bsz, NSA_GROUPS, seq, nb), BF16),
                   jax.ShapeDtypeStruct((bsz, NSA_GROUPS, nqt, 1, nb), F32),
                   jax.ShapeDtypeStruct((bsz * seq, NSA_Q_W), F32)],
        compiler_params=_cparams(("parallel", "parallel", "parallel")),
        name="select",
    )(slopes, main, kcmp, vcmp, main, ov)


def _attn_kernel(flags_ref, slopes_ref, q_ref, sel_ref, ks_ref, vs_ref, kw_ref, vw_ref, gates_ref, yc_ref, e_ref,
                 o_ref, m_scr, l_scr, acc_scr, *, chunk):
    b = pl.program_id(0)
    g = pl.program_id(1)
    qt = pl.program_id(2)
    nqt = pl.num_programs(2)
    t0 = qt * Q_TILE
    nq = Q_TILE
    seq = ks_ref.shape[0]
    nch_total = seq // chunk
    qs = _stack_heads(q_ref[...] * (NSA_DH ** -0.5)).astype(BF16)
    sel = sel_ref[0, 0]
    slopes = [slopes_ref[g * NSA_REP + r] for r in range(NSA_REP)]

    m_scr[...] = jnp.full(m_scr.shape, NEG, F32)
    l_scr[...] = jnp.zeros(l_scr.shape, F32)
    acc_scr[...] = jnp.zeros(acc_scr.shape, F32)
    flag_base = ((b * NSA_GROUPS + g) * nqt + qt) * nch_total

    def masked_logits(s, mask, pos_rel):
        return jnp.concatenate(
            [jnp.where(mask, s[r * nq:(r + 1) * nq] + slopes[r] * pos_rel, NEG) for r in range(NSA_REP)], axis=0)

    def body(c, carry):
        @pl.when(flags_ref[flag_base + c] != 0)
        def _():
            start = pl.multiple_of(c * chunk, chunk)
            kc = ks_ref[pl.ds(start, chunk), :]
            vc = vs_ref[pl.ds(start, chunk), :]
            s = _mm_nt(qs, kc)
            pos = start + lax.broadcasted_iota(jnp.int32, (nq, chunk), 1)
            tq = t0 + lax.broadcasted_iota(jnp.int32, (nq, chunk), 0)
            picked = jnp.dot(sel, e_ref[c], preferred_element_type=F32) > 0.5
            mask = picked & (pos <= tq)
            sm = masked_logits(s, mask, (pos[0:1] - t0).astype(F32))
            mask4 = jnp.concatenate([mask] * NSA_REP, axis=0)
            m_prev = m_scr[...]
            m_new = jnp.maximum(m_prev, jnp.max(sm, axis=-1, keepdims=True))
            alpha = jnp.exp(m_prev - m_new)
            p = jnp.where(mask4, jnp.exp(sm - m_new), 0.0)
            l_scr[...] = alpha * l_scr[...] + jnp.sum(p, axis=-1, keepdims=True)
            acc_scr[...] = alpha * acc_scr[...] + jnp.dot(p.astype(BF16), vc, preferred_element_type=F32)
            m_scr[...] = m_new
        return carry

    lax.fori_loop(0, (t0 + nq + chunk - 1) // chunk, body, 0)
    l = l_scr[...]
    o_s = acc_scr[...] * (1.0 / jnp.where(l > 0.0, l, 1.0))

    nw = WINDOW + nq
    base = pl.multiple_of(jnp.maximum(t0 - WINDOW, 0), nq)
    kw = kw_ref[pl.ds(base, nw), :]
    vw = vw_ref[pl.ds(base, nw), :]
    s = _mm_nt(qs, kw)
    pos = base + lax.broadcasted_iota(jnp.int32, (nq, nw), 1)
    tq = t0 + lax.broadcasted_iota(jnp.int32, (nq, nw), 0)
    mask = (pos <= tq) & (pos > tq - WINDOW)
    sm = masked_logits(s, mask, (pos[0:1] - t0).astype(F32))
    mask4 = jnp.concatenate([mask] * NSA_REP, axis=0)
    mx = jnp.max(sm, axis=-1, keepdims=True)
    p = jnp.where(mask4, jnp.exp(sm - mx), 0.0)
    den = jnp.sum(p, axis=-1, keepdims=True)
    o_w = jnp.dot(p.astype(BF16), vw, preferred_element_type=F32) * (1.0 / den)

    o_ref[...] = (yc_ref[...] + _unstack_heads(o_s, _gate_cols(gates_ref, 1))
                  + _unstack_heads(o_w, _gate_cols(gates_ref, 2)))


def _attn_call(flags, slopes, main, sel, kv, yc, expand, bsz, seq):
    nqt = seq // Q_TILE
    nb = sel.shape[-1]
    chunk = expand.shape[-1]
    row = lambda b, g, q: b * nqt + q
    grid_spec = pltpu.PrefetchScalarGridSpec(
        num_scalar_prefetch=1,
        grid=(bsz, NSA_GROUPS, nqt),
        in_specs=[pl.BlockSpec(memory_space=pltpu.SMEM),
                  pl.BlockSpec((Q_TILE, 256), lambda b, g, q, f: (row(b, g, q), COL_Q // 256 + g)),
                  pl.BlockSpec((1, 1, Q_TILE, nb), lambda b, g, q, f: (b, g, q, 0)),
                  pl.BlockSpec((seq, LANES), lambda b, g, q, f: (b, g)),
                  pl.BlockSpec((seq, LANES), lambda b, g, q, f: (b, 2 + g)),
                  pl.BlockSpec((seq, LANES), lambda b, g, q, f: (b, 4 + g)),
                  pl.BlockSpec((seq, LANES), lambda b, g, q, f: (b, 6 + g)),
                  pl.BlockSpec((Q_TILE, LANES), lambda b, g, q, f: (row(b, g, q), COL_GATE // LANES + g)),
                  pl.BlockSpec((Q_TILE, 256), lambda b, g, q, f: (row(b, g, q), g)),
                  pl.BlockSpec(expand.shape, lambda b, g, q, f: (0, 0, 0))],
        out_specs=pl.BlockSpec((Q_TILE, 256), lambda b, g, q, f: (row(b, g, q), g)),
        scratch_shapes=[pltpu.VMEM((NSA_REP * Q_TILE, 1), F32),
                        pltpu.VMEM((NSA_REP * Q_TILE, 1), F32),
                        pltpu.VMEM((NSA_REP * Q_TILE, LANES), F32)],
    )
    return pl.pallas_call(
        functools.partial(_attn_kernel, chunk=chunk),
        grid_spec=grid_spec,
        out_shape=jax.ShapeDtypeStruct((bsz * seq, NSA_Q_W), F32),
        compiler_params=_cparams(("parallel", "parallel", "parallel")),
        name="attn",
    )(flags, slopes, main, sel, kv, kv, kv, kv, main, yc, expand)


def _pool_kernel(x_ref, halo_ref, w_ref, sc_ref, o_ref):
    s = pl.program_id(1)
    ts = x_ref.shape[0]
    x = x_ref[...]
    halo = jnp.where(s > 0, halo_ref[...], 0.0)
    t = s * ts + lax.broadcasted_iota(jnp.int32, (ts, 1), 0)
    for gi, win in enumerate(POOL_WINDOWS):
        lanes = slice(gi * POOL_GROUP_DIM, (gi + 1) * POOL_GROUP_DIM)
        ext = jnp.concatenate([halo[:, lanes], x[:, lanes]], axis=0)
        acc = ext
        span = 1
        while span < win:
            acc = acc + pltpu.roll(acc, span, 0)
            span *= 2
        cnt = jnp.minimum(t + 1, win).astype(F32)
        y = acc[POOL_HALO:] / cnt - x[:, lanes]
        o_ref[:, lanes] = _mm(y, w_ref[gi]) * sc_ref[:, lanes]


def _pool_call(main, w, scale, bsz, seq):
    ts = min(POOL_TS, seq)
    per_b = seq // ts
    halo_per_tile = ts // POOL_HALO
    return pl.pallas_call(
        _pool_kernel,
        grid=(bsz, per_b),
        in_specs=[pl.BlockSpec((ts, POOL_W), lambda b, s: (b * per_b + s, COL_POOL // POOL_W)),
                  pl.BlockSpec((POOL_HALO, POOL_W),
                               lambda b, s: (jnp.maximum((b * per_b + s) * halo_per_tile - 1, 0), COL_POOL // POOL_W)),
                  pl.BlockSpec(w.shape, lambda b, s: (0, 0, 0)),
                  pl.BlockSpec((1, POOL_W), lambda b, s: (0, 0))],
        out_specs=pl.BlockSpec((ts, POOL_W), lambda b, s: (b * per_b + s, 0)),
        out_shape=jax.ShapeDtypeStruct((bsz * seq, POOL_W), F32),
        compiler_params=_cparams(("parallel", "parallel")),
        name="pool",
    )(main, main, w, scale.reshape(1, POOL_W))


def _unit_lower_inverse(nm, ri, ci):
    c = nm.shape[0]
    eye = jnp.where(ri == ci, 1.0, 0.0)
    n8 = jnp.where(jnp.right_shift(ri, 3) == jnp.right_shift(ci, 3), nm, 0.0)
    p = eye - n8
    m = _mm(n8, n8)
    p = p + _mm(p, m)
    m = _mm(m, m)
    p = p + _mm(p, m)
    shift = 3
    while (1 << shift) < c:
        rb = jnp.right_shift(ri, shift)
        cb = jnp.right_shift(ci, shift)
        low = jnp.where(((rb & 1) == 1) & (cb == rb - 1), nm, 0.0)
        p = p - _mm(_mm(p, low), p)
        shift += 1
    return p


def _dn_kernel(qkv_ref, halo_ref, z_ref, gba_ref, cw_ref, alog_ref, dtb_ref, ng_ref, o_ref, st_scr):
    s = pl.program_id(1)
    c = qkv_ref.shape[0]

    @pl.when(s == 0)
    def _():
        st_scr[...] = jnp.zeros(st_scr.shape, F32)

    halo = jnp.where(s > 0, halo_ref[...], 0.0)
    ext = jnp.concatenate([halo, qkv_ref[...]], axis=0)
    cw = cw_ref[...]
    conv = cw[DN_CONV - 1:DN_CONV] * ext
    for tap in range(1, DN_CONV):
        conv = conv + cw[DN_CONV - 1 - tap:DN_CONV - tap] * pltpu.roll(ext, tap, 0)
    act = _silu(conv[DN_HALO:])

    gba = gba_ref[...]
    beta_all = jax.nn.sigmoid(gba)
    xs = gba + dtb_ref[...]
    softplus = jnp.maximum(xs, 0.0) + jnp.log(1.0 + jnp.exp(-jnp.abs(xs)))
    g_all = -jnp.exp(alog_ref[...]) * softplus

    ri = lax.broadcasted_iota(jnp.int32, (c, c), 0)
    ci = lax.broadcasted_iota(jnp.int32, (c, c), 1)
    tril = ri >= ci
    strict = ri > ci
    tl = jnp.where(tril, 1.0, 0.0)
    ng = ng_ref[...]
    for h in range(DN_HEADS):
        q = act[:, h * DN_DK:(h + 1) * DN_DK]
        k = act[:, DN_W + h * DN_DK:DN_W + (h + 1) * DN_DK]
        v = act[:, 2 * DN_W + h * DN_DK:2 * DN_W + (h + 1) * DN_DK]
        qn = q * lax.rsqrt(jnp.sum(q * q, axis=-1, keepdims=True) + EPS) * (DN_DK ** -0.5)
        kn = k * lax.rsqrt(jnp.sum(k * k, axis=-1, keepdims=True) + EPS)
        beta = beta_all[:, LANE_BETA + h:LANE_BETA + h + 1]
        gcol = g_all[:, LANE_DECAY + h:LANE_DECAY + h + 1]
        rhs = jnp.concatenate([jnp.where(strict, gcol, 0.0), jnp.broadcast_to(gcol, (c, DN_DK))], axis=1)
        cum = jnp.dot(tl, rhs, preferred_element_type=F32, precision=HIGHEST)
        diff = cum[:, :c]
        gcb = cum[:, c:]
        lmask = jnp.where(tril, jnp.exp(jnp.where(tril, diff, 0.0)), 0.0)
        kb = kn * beta
        vb = v * beta
        nm = jnp.where(strict, _mm_nt(kb, kn) * lmask, 0.0)
        tinv = _unit_lower_inverse(nm, ri, ci)
        egc = jnp.exp(gcb)
        sol = _mm(tinv, jnp.concatenate([vb, kb * egc], axis=1))
        u = sol[:, :DN_DK]
        w = sol[:, DN_DK:]
        aqk = _mm_nt(qn, kn) * lmask
        state = st_scr[h]
        glast = gcb[c - 1:c, :]
        v_new = u - _mm(w, state)
        o = _mm(qn * egc, state) + _mm(aqk, v_new)
        st_scr[h] = state * jnp.exp(glast) + _mm_tn(kn * jnp.exp(glast - gcb), v_new)
        on = o * lax.rsqrt(jnp.mean(o * o, axis=-1, keepdims=True) + EPS) * ng
        o_ref[:, h * DN_DK:(h + 1) * DN_DK] = on * _silu(z_ref[:, h * DN_DK:(h + 1) * DN_DK])


def _dn_call(main, conv_w, alog_row, dtb_row, norm_g, bsz, seq):
    c = DN_CHUNK
    per_b = seq // c
    halo_per_chunk = c // DN_HALO
    w3 = 3 * DN_W
    return pl.pallas_call(
        _dn_kernel,
        grid=(bsz, per_b),
        in_specs=[pl.BlockSpec((c, w3), lambda b, s: (b * per_b + s, COL_QKV // w3)),
                  pl.BlockSpec((DN_HALO, w3),
                               lambda b, s: (jnp.maximum((b * per_b + s) * halo_per_chunk - 1, 0), COL_QKV // w3)),
                  pl.BlockSpec((c, DN_W), lambda b, s: (b * per_b + s, COL_Z // DN_W)),
                  pl.BlockSpec((c, LANES), lambda b, s: (b * per_b + s, COL_GATE // LANES)),
                  pl.BlockSpec((SUBLANES, w3), lambda b, s: (0, 0)),
                  pl.BlockSpec((1, LANES), lambda b, s: (0, 0)),
                  pl.BlockSpec((1, LANES), lambda b, s: (0, 0)),
                  pl.BlockSpec((1, DN_DK), lambda b, s: (0, 0))],
        out_specs=pl.BlockSpec((c, DN_W), lambda b, s: (b * per_b + s, 0)),
        out_shape=jax.ShapeDtypeStruct((bsz * seq, DN_W), F32),
        scratch_shapes=[pltpu.VMEM((DN_HEADS, DN_DK, DN_DK), F32)],
        compiler_params=_cparams(("parallel", "arbitrary")),
        name="dn",
    )(main, main, main, main, conv_w, alog_row, dtb_row, norm_g.reshape(1, DN_DK))


def _merge_kernel(ya_ref, yb_ref, yc_ref, mg_ref, x_ref, gt_ref, wa_ref, wb_ref, wc_ref, wo_ref, o_ref):
    d = x_ref.shape[1]
    gates = jax.nn.sigmoid(mg_ref[...])
    merged = (gates[:, :d] * jnp.dot(ya_ref[...].astype(BF16), wa_ref[...], preferred_element_type=F32)
              + gates[:, d:2 * d] * jnp.dot(yb_ref[...].astype(BF16), wb_ref[...], preferred_element_type=F32)
              + gates[:, 2 * d:] * jnp.dot(yc_ref[...].astype(BF16), wc_ref[...], preferred_element_type=F32))
    y = jnp.dot(merged.astype(BF16), wo_ref[...], preferred_element_type=F32)
    o_ref[...] = x_ref[...] + gt_ref[0] * y


def _merge_call(ya, yb, yc, main, x2, gt, wa, wb, wc, wo, seq):
    t, d = x2.shape
    tm = min(MERGE_TM, seq)
    per_b = seq // tm
    full = lambda a: pl.BlockSpec(a.shape, lambda i: (0, 0))
    return pl.pallas_call(
        _merge_kernel,
        grid=(t // tm,),
        in_specs=[pl.BlockSpec((tm, ya.shape[1]), lambda i: (i, 0)),
                  pl.BlockSpec((tm, yb.shape[1]), lambda i: (i, 0)),
                  pl.BlockSpec((tm, yc.shape[1]), lambda i: (i, 0)),
                  pl.BlockSpec((tm, 3 * d), lambda i: (i, COL_MG // (3 * d))),
                  pl.BlockSpec((tm, d), lambda i: (i, 0)),
                  pl.BlockSpec((1, 1, d), lambda i: (i // per_b, 0, 0)),
                  full(wa), full(wb), full(wc), full(wo)],
        out_specs=pl.BlockSpec((tm, d), lambda i: (i, 0)),
        out_shape=jax.ShapeDtypeStruct((t, d), F32),
        compiler_params=_cparams(("parallel",)),
        name="merge",
    )(ya, yb, yc, main, x2, gt, wa, wb, wc, wo)


def _mlp_kernel(x_ref, g_ref, sc_ref, sh_ref, gt_ref, w1_ref, w2_ref, fg_ref, o_ref, h_scr, acc_scr, *, final):
    j = pl.program_id(1)

    @pl.when(j == 0)
    def _():
        h_scr[...] = _norm_mod(x_ref[...], g_ref[...], sc_ref[0], sh_ref[0]).astype(BF16)
        acc_scr[...] = jnp.zeros(acc_scr.shape, F32)

    a = jnp.maximum(jnp.dot(h_scr[...], w1_ref[...], preferred_element_type=F32), 0.0)
    acc_scr[...] += jnp.dot((a * a).astype(BF16), w2_ref[...], preferred_element_type=F32)

    @pl.when(j == pl.num_programs(1) - 1)
    def _():
        y = x_ref[...] + gt_ref[0] * acc_scr[...]
        if final:
            ms = jnp.mean(y * y, axis=-1, keepdims=True)
            y = y * lax.rsqrt(ms + EPS) * fg_ref[...]
        o_ref[...] = y


def _mlp_call(x2, g, sc, sh, gt, w1, w2, fg, seq, final):
    t, d = x2.shape
    f = w1.shape[1]
    tm = min(MLP_TM, seq)
    tf = MLP_TF
    per_b = seq // tm
    mod = lambda: pl.BlockSpec((1, 1, d), lambda i, j: (i // per_b, 0, 0))
    return pl.pallas_call(
        functools.partial(_mlp_kernel, final=final),
        grid=(t // tm, f // tf),
        in_specs=[pl.BlockSpec((tm, d), lambda i, j: (i, 0)),
                  pl.BlockSpec((1, d), lambda i, j: (0, 0)),
                  mod(), mod(), mod(),
                  pl.BlockSpec((d, tf), lambda i, j: (0, j)),
                  pl.BlockSpec((tf, d), lambda i, j: (j, 0)),
                  pl.BlockSpec((1, d), lambda i, j: (0, 0))],
        out_specs=pl.BlockSpec((tm, d), lambda i, j: (i, 0)),
        out_shape=jax.ShapeDtypeStruct((t, d), F32),
        scratch_shapes=[pltpu.VMEM((tm, d), BF16), pltpu.VMEM((tm, d), F32)],
        compiler_params=_cparams(("parallel", "arbitrary")),
        name="mlp",
    )(x2, g.reshape(1, d), sc, sh, gt, w1, w2, fg.reshape(1, d))


def _split_w_in(w_in):
    offs = [0]
    for sz in IN_SIZES:
        offs.append(offs[-1] + sz)
    return [w_in[:, offs[i]:offs[i + 1]] for i in range(len(IN_SIZES))]


def _layout_w_in(w_in):
    nq, nkc, nvc, nks, nvs, nkw, nvw, ngate, pin, dqkv, dz, dbeta, da, mg = _split_w_in(w_in)
    d = w_in.shape[0]
    per_g = 3 * NSA_REP
    zeros = lambda n: jnp.zeros((d, n), w_in.dtype)
    gate0 = jnp.concatenate([ngate[:, :per_g], zeros(LANE_BETA - per_g), dbeta, da,
                             zeros(LANES - LANE_DECAY - DN_HEADS)], axis=1)
    gate1 = jnp.concatenate([ngate[:, per_g:], zeros(LANES - per_g)], axis=1)
    main = jnp.concatenate([nq, pin, dz, dqkv, mg, nkc, nvc, gate0, gate1], axis=1)

    def dup(w):
        return jnp.concatenate([w[:, :NSA_DH], w[:, :NSA_DH], w[:, NSA_DH:], w[:, NSA_DH:]], axis=1)

    kv = jnp.concatenate([dup(nks), dup(nvs), dup(nkw), dup(nvw)], axis=1)
    return main.astype(BF16), kv.astype(BF16)


def _nsa_constants(seq, chunk):
    nb = seq // SEL_LEN
    ncp = seq // CMP_STRIDE
    j = jnp.arange(nb)[:, None] * SEL_LEN
    i = jnp.arange(ncp)[None, :] * CMP_STRIDE
    overlap = jnp.clip(jnp.minimum(j + SEL_LEN, i + CMP_LEN) - jnp.maximum(j, i), 0) // CMP_STRIDE
    key_block = (jnp.arange(seq) // SEL_LEN).reshape(seq // chunk, 1, chunk)
    expand = (jnp.arange(nb)[None, :, None] == key_block)
    return overlap.astype(BF16), expand.astype(BF16)


def kernel(x, c, ada_w, ada_b, norm1_g, norm2_g, w_in, phi_k1, phi_k2, phi_v1, phi_v2, pos_k, pos_v, pool_w, pool_scale, dn_conv_w, dn_A_log, dn_dt_bias, dn_norm_g, w_branch_nsa, w_branch_pool, w_branch_dn, w_out, mlp_w1, mlp_w2, final_g):
    bsz, seq, d = x.shape
    assert d == D_MODEL and seq % max(PROJ_TM, MLP_TM, POOL_TS) == 0 and seq % KV_CHUNK == 0
    assert DN_CHUNK == DN_DK == LANES
    t = bsz * seq
    depth = ada_w.shape[0]
    chunk = KV_CHUNK
    nqt = seq // Q_TILE
    nb = seq // SEL_LEN
    blocks_per_chunk = chunk // SEL_LEN

    slopes = 2.0 ** (-(8.0 / NSA_HEADS) * (jnp.arange(NSA_HEADS, dtype=F32) + 1.0))
    overlap, expand = _nsa_constants(seq, chunk)
    c_pad = jnp.zeros((SUBLANES, d), F32).at[:bsz].set(c)
    x2 = x.reshape(t, d)

    for l in range(depth):
        mod = _mod_call(c_pad, ada_w[l], ada_b[l])[:bsz]
        sh1, sc1, gt1, sh2, sc2, gt2 = [m.reshape(bsz, 1, d) for m in jnp.split(mod, 6, axis=-1)]

        w_main, w_kv = _layout_w_in(w_in[l])
        main = _proj_call(x2, norm1_g[l], sc1, sh1, w_main, seq, F32, PROJ_TN, "proj_main")
        kv = _proj_call(x2, norm1_g[l], sc1, sh1, w_kv, seq, BF16, N_KV, "proj_kv")

        def rows16(col):
            a = main[:, col:col + NSA_KV_W].reshape(bsz, seq // CMP_STRIDE, CMP_STRIDE, NSA_GROUPS, NSA_DH)
            return a.transpose(0, 3, 1, 2, 4).reshape(bsz, NSA_GROUPS, seq // CMP_STRIDE, CMP_STRIDE * NSA_DH)

        rows = jnp.stack([rows16(COL_KC), rows16(COL_VC)]).astype(BF16)
        pos = jnp.zeros((2, SUBLANES, CMP_LEN * NSA_DH), F32).at[:, 0].set(
            jnp.stack([pos_k[l].reshape(-1), pos_v[l].reshape(-1)])).astype(BF16)
        w1 = jnp.stack([phi_k1[l], phi_v1[l]]).astype(BF16)
        w2 = jnp.stack([phi_k2[l], phi_v2[l]])
        w2 = jnp.concatenate([w2, w2], axis=-1).astype(BF16)
        cmp_kv = _compress_call(rows, pos, w1, w2)

        sel, act, yc = _select_call(slopes, main, cmp_kv[0], cmp_kv[1], overlap, bsz, seq)
        flags = (act.reshape(bsz, NSA_GROUPS, nqt, nb // blocks_per_chunk, blocks_per_chunk).max(axis=-1) > 0.0)
        flags = flags.astype(jnp.int32).reshape(-1)
        y_a = _attn_call(flags, slopes, main, sel, kv, yc, expand, bsz, seq)

        y_b = _pool_call(main, pool_w[l].astype(BF16), pool_scale[l], bsz, seq)

        conv_w = jnp.zeros((SUBLANES, 3 * DN_W), F32).at[:DN_CONV].set(dn_conv_w[l])
        lane_row = lambda v: jnp.zeros((1, LANES), F32).at[0, LANE_DECAY:LANE_DECAY + DN_HEADS].set(v)
        y_c = _dn_call(main, conv_w, lane_row(dn_A_log[l]), lane_row(dn_dt_bias[l]), dn_norm_g[l], bsz, seq)

        x2 = _merge_call(y_a, y_b, y_c, main, x2, gt1,
                         w_branch_nsa[l].astype(BF16), w_branch_pool[l].astype(BF16),
                         w_branch_dn[l].astype(BF16), w_out[l].astype(BF16), seq)
        x2 = _mlp_call(x2, norm2_g[l], sc2, sh2, gt2, mlp_w1[l].astype(BF16), mlp_w2[l].astype(BF16),
                       final_g, seq, final=(l == depth - 1))
    return x2.reshape(bsz, seq, d)
```

```python
import functools

import jax
import jax.numpy as jnp
from jax import lax
from jax.experimental import pallas as pl
from jax.experimental.pallas import tpu as pltpu

F32 = jnp.float32
BF16 = jnp.bfloat16
HIGHEST = lax.Precision.HIGHEST

D_MODEL = 1024
NSA_HEADS = 8
NSA_GROUPS = 2
NSA_REP = NSA_HEADS // NSA_GROUPS
NSA_DH = 64
CMP_LEN = 32
CMP_STRIDE = 16
SEL_LEN = 64
SEL_SHIFT = 6
TOP_N = 16
WINDOW = 512
FORCE_BONUS = 1.0e4
POOL_WINDOWS = (2, 4, 8, 16)
POOL_GROUP_DIM = 128
POOL_W = len(POOL_WINDOWS) * POOL_GROUP_DIM
DN_HEADS = 4
DN_DK = 128
DN_W = DN_HEADS * DN_DK
DN_CONV = 4
EPS = 1e-6
NEG = -1e30
NSA_Q_W = NSA_HEADS * NSA_DH
NSA_KV_W = NSA_GROUPS * NSA_DH
IN_SIZES = (NSA_Q_W, NSA_KV_W, NSA_KV_W, NSA_KV_W, NSA_KV_W, NSA_KV_W, NSA_KV_W,
            3 * NSA_HEADS, POOL_W, 3 * DN_W, DN_W, DN_HEADS, DN_HEADS, 3 * D_MODEL)

LANES = 128
LANE_SHIFT = 7
SUBLANES = 8
VMEM_LIMIT_BYTES = 48 * 1024 * 1024

Q_TILE = 128
KV_CHUNK = 512
DN_CHUNK = 128
DN_BATCH = 4
PROJ_TM = 1024
MLP_TM = 1024
MLP_TF = 1024
MERGE_TM = 512
POOL_TS = 1024
POOL_HALO = 16
DN_HALO = 8

COL_Q = 0
COL_POOL = 512
COL_Z = 1024
COL_QKV = 1536
COL_MG = 3072
COL_GATE = 6144
N_MAIN = 6400
PROJ_TN = 1280
LANE_BETA = 16
LANE_DECAY = 20
KVB_KS = 0
KVB_VS = 2
KVB_KW = 4
KVB_VW = 6
KVB_KC = 8
KVB_VC = 9
N_KV = 10 * LANES
LANE_POS_HI = NSA_DH
LANE_POS_LO = NSA_DH + 1


def _cparams(sem):
    return pltpu.CompilerParams(dimension_semantics=sem, vmem_limit_bytes=VMEM_LIMIT_BYTES)


def _silu(v):
    return v * jax.nn.sigmoid(v)


def _mm(a, b):
    return jnp.dot(a.astype(BF16), b.astype(BF16), preferred_element_type=F32)


def _mm_nt(a, b):
    return lax.dot_general(a.astype(BF16), b.astype(BF16), (((1,), (1,)), ((), ())), preferred_element_type=F32)


def _mm_tn(a, b):
    return lax.dot_general(a.astype(BF16), b.astype(BF16), (((0,), (0,)), ((), ())), preferred_element_type=F32)


def _mod_kernel(c_ref, w_ref, b_ref, o_ref):
    cond = _silu(c_ref[...])
    o_ref[...] = jnp.dot(cond, w_ref[...], preferred_element_type=F32, precision=HIGHEST) + b_ref[...]


def _mod_call(c_pad, w, b):
    d, n = w.shape
    tn = 1536
    return pl.pallas_call(
        _mod_kernel,
        grid=(n // tn,),
        in_specs=[pl.BlockSpec((SUBLANES, d), lambda j: (0, 0)),
                  pl.BlockSpec((d, tn), lambda j: (0, j)),
                  pl.BlockSpec((1, tn), lambda j: (0, j))],
        out_specs=pl.BlockSpec((SUBLANES, tn), lambda j: (0, j)),
        out_shape=jax.ShapeDtypeStruct((SUBLANES, n), F32),
        compiler_params=_cparams(("parallel",)),
        name="mod",
    )(c_pad, w, b.reshape(1, n))


def _norm_mod(x, g, sc, sh):
    ms = jnp.mean(x * x, axis=-1, keepdims=True)
    return (x * lax.rsqrt(ms + EPS) * g) * (1.0 + sc) + sh


def _proj_kernel(x_ref, g_ref, sc_ref, sh_ref, w_ref, o_ref, h_scr):
    @pl.when(pl.program_id(1) == 0)
    def _():
        h_scr[...] = _norm_mod(x_ref[...], g_ref[...], sc_ref[0], sh_ref[0]).astype(BF16)

    o_ref[...] = jnp.dot(h_scr[...], w_ref[...], preferred_element_type=F32)


def _proj_kv_kernel(x_ref, g_ref, sc_ref, sh_ref, w_ref, o_ref, *, tiles_per_seq):
    h = _norm_mod(x_ref[...], g_ref[...], sc_ref[0], sh_ref[0]).astype(BF16)
    acc = jnp.dot(h, w_ref[...], preferred_element_type=F32)
    tm, n = acc.shape
    col = lax.broadcasted_iota(jnp.int32, (1, n), 1)
    blk = jnp.right_shift(col, LANE_SHIFT)
    lane = col & (LANES - 1)
    is_key = ((blk >= KVB_KS) & (blk < KVB_VS)) | ((blk >= KVB_KW) & (blk < KVB_VW))
    is_val = ((blk >= KVB_VS) & (blk < KVB_KW)) | ((blk >= KVB_VW) & (blk < KVB_KC))
    ones_row = jnp.where(is_val & (lane >= NSA_DH), 1.0, 0.0)
    hi_row = jnp.where(is_key & (lane == LANE_POS_HI), 1.0, 0.0)
    lo_row = jnp.where(is_key & (lane == LANE_POS_LO), 1.0, 0.0)
    pos = (pl.program_id(0) % tiles_per_seq) * tm + lax.broadcasted_iota(jnp.int32, (tm, 1), 0)
    pos_hi = jnp.right_shift(pos, LANE_SHIFT).astype(F32)
    pos_lo = (pos & (LANES - 1)).astype(F32)
    o_ref[...] = (acc + ones_row + pos_hi * hi_row + pos_lo * lo_row).astype(o_ref.dtype)


def _proj_specs(tm, d, tn, per_b):
    return [pl.BlockSpec((tm, d), lambda i, j: (i, 0)),
            pl.BlockSpec((1, d), lambda i, j: (0, 0)),
            pl.BlockSpec((1, 1, d), lambda i, j: (i // per_b, 0, 0)),
            pl.BlockSpec((1, 1, d), lambda i, j: (i // per_b, 0, 0)),
            pl.BlockSpec((d, tn), lambda i, j: (0, j))]


def _proj_call(x2, g, sc, sh, w, seq):
    t, d = x2.shape
    n = w.shape[1]
    tm = min(PROJ_TM, seq)
    tn = PROJ_TN
    return pl.pallas_call(
        _proj_kernel,
        grid=(t // tm, n // tn),
        in_specs=_proj_specs(tm, d, tn, seq // tm),
        out_specs=pl.BlockSpec((tm, tn), lambda i, j: (i, j)),
        out_shape=jax.ShapeDtypeStruct((t, n), F32),
        scratch_shapes=[pltpu.VMEM((tm, d), BF16)],
        compiler_params=_cparams(("parallel", "arbitrary")),
        name="proj_main",
    )(x2, g.reshape(1, d), sc, sh, w)


def _proj_kv_call(x2, g, sc, sh, w, seq):
    t, d = x2.shape
    n = w.shape[1]
    tm = min(PROJ_TM, seq)
    return pl.pallas_call(
        functools.partial(_proj_kv_kernel, tiles_per_seq=seq // tm),
        grid=(t // tm, 1),
        in_specs=_proj_specs(tm, d, n, seq // tm),
        out_specs=pl.BlockSpec((tm, n), lambda i, j: (i, j)),
        out_shape=jax.ShapeDtypeStruct((t, n), BF16),
        compiler_params=_cparams(("parallel", "arbitrary")),
        name="proj_kv",
    )(x2, g.reshape(1, d), sc, sh, w)


def _compress_kernel(rows_ref, pos_ref, w1_ref, w2_ref, o_ref):
    rows = rows_ref[0, 0, 0]
    half = rows.shape[1]
    nr = rows.shape[0]
    w1 = w1_ref[0]
    top = jnp.dot(rows, w1[:half], preferred_element_type=F32)
    bot = jnp.dot(rows, w1[half:], preferred_element_type=F32)
    posb = jnp.dot(pos_ref[0], w1, preferred_element_type=F32)[0:1]
    hid = top + pltpu.roll(bot, nr - 1, 0) + posb
    o_ref[0, 0, 0] = _mm(_silu(hid), w2_ref[0]).astype(o_ref.dtype)


def _compress_call(rows, pos, w1, w2):
    two, b, g, nr, wdt = rows.shape
    hid = w1.shape[2]
    return pl.pallas_call(
        _compress_kernel,
        grid=(two, b, g),
        in_specs=[pl.BlockSpec((1, 1, 1, nr, wdt), lambda a, i, j: (a, i, j, 0, 0)),
                  pl.BlockSpec((1, SUBLANES, 2 * wdt), lambda a, i, j: (a, 0, 0)),
                  pl.BlockSpec((1, 2 * wdt, hid), lambda a, i, j: (a, 0, 0)),
                  pl.BlockSpec((1, hid, LANES), lambda a, i, j: (a, 0, 0))],
        out_specs=pl.BlockSpec((1, 1, 1, nr, LANES), lambda a, i, j: (a, i, j, 0, 0)),
        out_shape=jax.ShapeDtypeStruct((two, b, g, nr, LANES), BF16),
        compiler_params=_cparams(("parallel", "parallel", "parallel")),
        name="compress",
    )(rows, pos, w1, w2)


def _gate_cols(gates_ref, branch):
    sg = jax.nn.sigmoid(gates_ref[...])
    return [sg[:, 3 * r + branch:3 * r + branch + 1] for r in range(NSA_REP)]


def _stack_heads_dup(q):
    qa = q[:, :LANES]
    qb = q[:, LANES:]
    lo = lax.broadcasted_iota(jnp.int32, qa.shape, 1) < NSA_DH
    z = jnp.zeros_like(qa)
    return jnp.concatenate([jnp.where(lo, qa, z), jnp.where(lo, z, qa),
                            jnp.where(lo, qb, z), jnp.where(lo, z, qb)], axis=0)


def _unstack_heads_dup(o, gates):
    nq = o.shape[0] // NSA_REP
    lo = lax.broadcasted_iota(jnp.int32, (nq, LANES), 1) < NSA_DH
    a = jnp.where(lo, gates[0] * o[0:nq], gates[1] * o[nq:2 * nq])
    b = jnp.where(lo, gates[2] * o[2 * nq:3 * nq], gates[3] * o[3 * nq:])
    return jnp.concatenate([a, b], axis=1)


def _stack_heads_pos(q, slopes):
    qa = q[:, :LANES]
    qb = q[:, LANES:]
    lane = lax.broadcasted_iota(jnp.int32, qa.shape, 1)
    lo = lane < NSA_DH
    heads = [qa, pltpu.roll(qa, NSA_DH, 1), qb, pltpu.roll(qb, NSA_DH, 1)]
    out = []
    for r in range(NSA_REP):
        extra = jnp.where(lane == LANE_POS_HI, slopes[r] * LANES, jnp.where(lane == LANE_POS_LO, slopes[r], 0.0))
        out.append(jnp.where(lo, heads[r], extra))
    return jnp.concatenate(out, axis=0)


def _finish_heads(acc, gates):
    nq = acc.shape[0] // NSA_REP
    o = acc * pltpu.roll(1.0 / acc, NSA_DH, 1)
    lo = lax.broadcasted_iota(jnp.int32, (nq, LANES), 1) < NSA_DH
    hs = [gates[r] * o[r * nq:(r + 1) * nq] for r in range(NSA_REP)]
    a = jnp.where(lo, hs[0], pltpu.roll(hs[1], NSA_DH, 1))
    b = jnp.where(lo, hs[2], pltpu.roll(hs[3], NSA_DH, 1))
    return jnp.concatenate([a, b], axis=1)


def _select_kernel(slopes_ref, q_ref, kc_ref, vc_ref, gates_ref, ov_ref, sel_ref, act_ref, yc_ref):
    g = pl.program_id(1)
    t0 = pl.program_id(2) * Q_TILE
    nq = Q_TILE
    qs = _stack_heads_dup(q_ref[...] * (NSA_DH ** -0.5)).astype(BF16)
    kc = kc_ref[0, 0]
    vc = vc_ref[0, 0]
    ncp = kc.shape[0]
    s = _mm_nt(qs, kc)
    c_end = lax.broadcasted_iota(jnp.int32, (nq, ncp), 1) * CMP_STRIDE + (CMP_LEN - 1)
    tq = t0 + lax.broadcasted_iota(jnp.int32, (nq, ncp), 0)
    vis = c_end <= tq
    rel = (c_end[0:1] - t0).astype(F32)
    psum = jnp.zeros((nq, ncp), F32)
    ps = []
    for r in range(NSA_REP):
        m_r = slopes_ref[g * NSA_REP + r]
        sr = jnp.where(vis, s[r * nq:(r + 1) * nq] + m_r * rel, NEG)
        mx = jnp.max(sr, axis=-1, keepdims=True)
        e = jnp.where(vis, jnp.exp(sr - mx), 0.0)
        den = jnp.sum(e, axis=-1, keepdims=True)
        p = e * (1.0 / jnp.where(den > 0.0, den, 1.0))
        psum = psum + p
        ps.append(p.astype(BF16))
    o_c = jnp.dot(jnp.concatenate(ps, axis=0), vc, preferred_element_type=F32)
    yc_ref[...] = _unstack_heads_dup(o_c, _gate_cols(gates_ref, 0))

    p_hi = psum.astype(BF16)
    p_lo = (psum - p_hi.astype(F32)).astype(BF16)
    ov = ov_ref[...]
    imp_t = _mm_nt(ov, p_hi) + _mm_nt(ov, p_lo)
    nb = ov.shape[0]
    jb = lax.broadcasted_iota(jnp.int32, (nb, nq), 0)
    tql = t0 + lax.broadcasted_iota(jnp.int32, (nb, nq), 1)
    valid = jb * SEL_LEN <= tql
    forced = (jb == 0) | (jb == jnp.right_shift(tql, SEL_SHIFT))
    score = jnp.where(valid, imp_t + FORCE_BONUS * forced.astype(F32), NEG)
    sel = jnp.zeros((nb, nq), jnp.bool_)
    for _ in range(min(TOP_N, nb)):
        mx = jnp.max(score, axis=0, keepdims=True)
        idx = jnp.min(jnp.where(score == mx, jb, nb), axis=0, keepdims=True)
        hit = jb == idx
        sel = sel | hit
        score = jnp.where(hit, -jnp.inf, score)
    sel_t = jnp.where(sel & valid, 1.0, 0.0)
    sel_q = sel_t.T
    sel_ref[0, 0] = sel_q.astype(sel_ref.dtype)
    act_ref[0, 0, 0] = jnp.max(sel_q, axis=0, keepdims=True)


def _select_call(slopes, main, kcmp, vcmp, ov, bsz, seq):
    nqt = seq // Q_TILE
    nb, ncp = ov.shape
    grid = (bsz, NSA_GROUPS, nqt)
    row = lambda b, g, q: b * nqt + q
    return pl.pallas_call(
        _select_kernel,
        grid=grid,
        in_specs=[pl.BlockSpec(memory_space=pltpu.SMEM),
                  pl.BlockSpec((Q_TILE, 256), lambda b, g, q: (row(b, g, q), COL_Q // 256 + g)),
                  pl.BlockSpec((1, 1, ncp, LANES), lambda b, g, q: (b, g, 0, 0)),
                  pl.BlockSpec((1, 1, ncp, LANES), lambda b, g, q: (b, g, 0, 0)),
                  pl.BlockSpec((Q_TILE, LANES), lambda b, g, q: (row(b, g, q), COL_GATE // LANES + g)),
                  pl.BlockSpec((nb, ncp), lambda b, g, q: (0, 0))],
        out_specs=[pl.BlockSpec((1, 1, Q_TILE, nb), lambda b, g, q: (b, g, q, 0)),
                   pl.BlockSpec((1, 1, 1, 1, nb), lambda b, g, q: (b, g, q, 0, 0)),
                   pl.BlockSpec((Q_TILE, 256), lambda b, g, q: (row(b, g, q), g))],
        out_shape=[jax.ShapeDtypeStruct((bsz, NSA_GROUPS, seq, nb), BF16),
                   jax.ShapeDtypeStruct((bsz, NSA_GROUPS, nqt, 1, nb), F32),
                   jax.ShapeDtypeStruct((bsz * seq, NSA_Q_W), F32)],
        compiler_params=_cparams(("parallel", "parallel", "parallel")),
        name="select",
    )(slopes, main, kcmp, vcmp, main, ov)


def _attn_kernel(flags_ref, slopes_ref, q_ref, sel_ref, ks_ref, vs_ref, kw_ref, vw_ref, gates_ref, yc_ref, et_ref,
                 o_ref, m_scr, acc_scr, *, chunk):
    b = pl.program_id(0)
    g = pl.program_id(1)
    qt = pl.program_id(2)
    nqt = pl.num_programs(2)
    t0 = qt * Q_TILE
    nq = Q_TILE
    nch_total = ks_ref.shape[0] // chunk
    slopes = [slopes_ref[g * NSA_REP + r] for r in range(NSA_REP)]
    qs = _stack_heads_pos(q_ref[...] * (NSA_DH ** -0.5), slopes).astype(BF16)
    pen = ((sel_ref[0, 0].astype(F32) - 1.0) * (-NEG)).astype(BF16)
    qx = jnp.concatenate([qs, jnp.concatenate([pen] * NSA_REP, axis=0)], axis=1)

    m_scr[...] = jnp.full(m_scr.shape, NEG, F32)
    acc_scr[...] = jnp.zeros(acc_scr.shape, F32)
    flag_base = ((b * NSA_GROUPS + g) * nqt + qt) * nch_total

    def flash_step(c, causal):
        start = pl.multiple_of(c * chunk, chunk)
        kx = jnp.concatenate([ks_ref[pl.ds(start, chunk), :], et_ref[pl.ds(start, chunk), :]], axis=1)
        s = _mm_nt(qx, kx)
        if causal:
            pos = start + lax.broadcasted_iota(jnp.int32, (nq, chunk), 1)
            tq = t0 + lax.broadcasted_iota(jnp.int32, (nq, chunk), 0)
            ok = pos <= tq
            s = jnp.concatenate([jnp.where(ok, s[r * nq:(r + 1) * nq], NEG) for r in range(NSA_REP)], axis=0)
        m_prev = m_scr[...]
        m_new = jnp.maximum(m_prev, jnp.max(s, axis=-1, keepdims=True))
        p = jnp.exp(s - m_new).astype(BF16)
        acc_scr[...] = (jnp.exp(m_prev - m_new) * acc_scr[...]
                        + jnp.dot(p, vs_ref[pl.ds(start, chunk), :], preferred_element_type=F32))
        m_scr[...] = m_new

    def body(c, carry):
        @pl.when(flags_ref[flag_base + c] != 0)
        def _():
            flash_step(c, causal=False)
        return carry

    c_diag = t0 // chunk
    lax.fori_loop(0, c_diag, body, 0)
    flash_step(c_diag, causal=True)
    y_sel = _finish_heads(acc_scr[...], _gate_cols(gates_ref, 1))

    nw = WINDOW + nq
    base = pl.multiple_of(jnp.maximum(t0 - WINDOW, 0), nq)
    s = _mm_nt(qs, kw_ref[pl.ds(base, nw), :])
    pos = base + lax.broadcasted_iota(jnp.int32, (nq, nw), 1)
    tq = t0 + lax.broadcasted_iota(jnp.int32, (nq, nw), 0)
    ok = (pos <= tq) & (pos > tq - WINDOW)
    s = jnp.concatenate([jnp.where(ok, s[r * nq:(r + 1) * nq], NEG) for r in range(NSA_REP)], axis=0)
    p = jnp.exp(s - jnp.max(s, axis=-1, keepdims=True)).astype(BF16)
    acc_w = jnp.dot(p, vw_ref[pl.ds(base, nw), :], preferred_element_type=F32)
    o_ref[...] = yc_ref[...] + y_sel + _finish_heads(acc_w, _gate_cols(gates_ref, 2))


def _attn_call(flags, slopes, main, sel, kv, yc, et, bsz, seq, chunk):
    nqt = seq // Q_TILE
    nb = sel.shape[-1]
    row = lambda b, g, q: b * nqt + q
    kvspec = lambda blk: pl.BlockSpec((seq, LANES), lambda b, g, q, f: (b, blk + g))
    grid_spec = pltpu.PrefetchScalarGridSpec(
        num_scalar_prefetch=1,
        grid=(bsz, NSA_GROUPS, nqt),
        in_specs=[pl.BlockSpec(memory_space=pltpu.SMEM),
                  pl.BlockSpec((Q_TILE, 256), lambda b, g, q, f: (row(b, g, q), COL_Q // 256 + g)),
                  pl.BlockSpec((1, 1, Q_TILE, nb), lambda b, g, q, f: (b, g, q, 0)),
                  kvspec(KVB_KS), kvspec(KVB_VS), kvspec(KVB_KW), kvspec(KVB_VW),
                  pl.BlockSpec((Q_TILE, LANES), lambda b, g, q, f: (row(b, g, q), COL_GATE // LANES + g)),
                  pl.BlockSpec((Q_TILE, 256), lambda b, g, q, f: (row(b, g, q), g)),
                  pl.BlockSpec(et.shape, lambda b, g, q, f: (0, 0))],
        out_specs=pl.BlockSpec((Q_TILE, 256), lambda b, g, q, f: (row(b, g, q), g)),
        scratch_shapes=[pltpu.VMEM((NSA_REP * Q_TILE, 1), F32),
                        pltpu.VMEM((NSA_REP * Q_TILE, LANES), F32)],
    )
    return pl.pallas_call(
        functools.partial(_attn_kernel, chunk=chunk),
        grid_spec=grid_spec,
        out_shape=jax.ShapeDtypeStruct((bsz * seq, NSA_Q_W), F32),
        compiler_params=_cparams(("parallel", "parallel", "parallel")),
        name="attn",
    )(flags, slopes, main, sel, kv, kv, kv, kv, main, yc, et)


def _pool_kernel(x_ref, halo_ref, w_ref, sc_ref, o_ref):
    s = pl.program_id(1)
    ts = x_ref.shape[0]
    x = x_ref[...]
    halo = jnp.where(s > 0, halo_ref[...], 0.0)
    t = s * ts + lax.broadcasted_iota(jnp.int32, (ts, 1), 0)
    for gi, win in enumerate(POOL_WINDOWS):
        lanes = slice(gi * POOL_GROUP_DIM, (gi + 1) * POOL_GROUP_DIM)
        ext = jnp.concatenate([halo[:, lanes], x[:, lanes]], axis=0)
        acc = ext
        span = 1
        while span < win:
            acc = acc + pltpu.roll(acc, span, 0)
            span *= 2
        cnt = jnp.minimum(t + 1, win).astype(F32)
        y = acc[POOL_HALO:] / cnt - x[:, lanes]
        o_ref[:, lanes] = _mm(y, w_ref[gi]) * sc_ref[:, lanes]


def _pool_call(main, w, scale, bsz, seq):
    ts = min(POOL_TS, seq)
    per_b = seq // ts
    halo_per_tile = ts // POOL_HALO
    return pl.pallas_call(
        _pool_kernel,
        grid=(bsz, per_b),
        in_specs=[pl.BlockSpec((ts, POOL_W), lambda b, s: (b * per_b + s, COL_POOL // POOL_W)),
                  pl.BlockSpec((POOL_HALO, POOL_W),
                               lambda b, s: (jnp.maximum((b * per_b + s) * halo_per_tile - 1, 0), COL_POOL // POOL_W)),
                  pl.BlockSpec(w.shape, lambda b, s: (0, 0, 0)),
                  pl.BlockSpec((1, POOL_W), lambda b, s: (0, 0))],
        out_specs=pl.BlockSpec((ts, POOL_W), lambda b, s: (b * per_b + s, 0)),
        out_shape=jax.ShapeDtypeStruct((bsz * seq, POOL_W), F32),
        compiler_params=_cparams(("parallel", "parallel")),
        name="pool",
    )(main, main, w, scale.reshape(1, POOL_W))


def _unit_lower_inverses(nms, ri, ci):
    c = nms[0].shape[0]
    eye = jnp.where(ri == ci, 1.0, 0.0)
    in8 = jnp.right_shift(ri, 3) == jnp.right_shift(ci, 3)
    n8 = [jnp.where(in8, nm, 0.0) for nm in nms]
    ps = [eye - a for a in n8]
    ms = [_mm(a, a) for a in n8]
    ps = [p + _mm(p, m) for p, m in zip(ps, ms)]
    ms = [_mm(m, m) for m in ms]
    ps = [p + _mm(p, m) for p, m in zip(ps, ms)]
    shift = 3
    while (1 << shift) < c:
        rb = jnp.right_shift(ri, shift)
        cb = jnp.right_shift(ci, shift)
        lower_left = ((rb & 1) == 1) & (cb == rb - 1)
        ts = [_mm(p, jnp.where(lower_left, nm, 0.0)) for p, nm in zip(ps, nms)]
        ps = [p - _mm(t, p) for p, t in zip(ps, ts)]
        shift += 1
    return ps


def _row_cumsum(x, ri):
    span = 1
    while span < x.shape[0]:
        x = x + jnp.where(ri >= span, pltpu.roll(x, span, 0), 0.0)
        span *= 2
    return x


def _dn_kernel(qkv_ref, halo_ref, z_ref, gba_ref, cw_ref, alog_ref, dtb_ref, ng_ref, o_ref, st_scr):
    s = pl.program_id(1)
    nbatch, c = qkv_ref.shape[0], qkv_ref.shape[1]

    @pl.when(s == 0)
    def _():
        st_scr[...] = jnp.zeros(st_scr.shape, F32)

    ri = lax.broadcasted_iota(jnp.int32, (c, c), 0)
    ci = lax.broadcasted_iota(jnp.int32, (c, c), 1)
    tril = ri >= ci
    strict = ri > ci
    cw = cw_ref[...]
    ng = ng_ref[...]
    chains = [(bi, h) for bi in range(nbatch) for h in range(DN_HEADS)]
    qn, kn, kb, vb, gcb, lmask = [], [], [], [], [], []
    for bi in range(nbatch):
        halo = jnp.where(s > 0, halo_ref[bi], 0.0)
        ext = jnp.concatenate([halo, qkv_ref[bi]], axis=0)
        conv = cw[DN_CONV - 1:DN_CONV] * ext
        for tap in range(1, DN_CONV):
            conv = conv + cw[DN_CONV - 1 - tap:DN_CONV - tap] * pltpu.roll(ext, tap, 0)
        act = _silu(conv[DN_HALO:])

        gba = gba_ref[bi]
        beta_all = jax.nn.sigmoid(gba)
        xs = gba + dtb_ref[...]
        softplus = jnp.maximum(xs, 0.0) + jnp.log(1.0 + jnp.exp(-jnp.abs(xs)))
        gc_all = _row_cumsum(-jnp.exp(alog_ref[...]) * softplus, ri)
        gct_all = gc_all.T
        for h in range(DN_HEADS):
            q = act[:, h * DN_DK:(h + 1) * DN_DK]
            k = act[:, DN_W + h * DN_DK:DN_W + (h + 1) * DN_DK]
            v = act[:, 2 * DN_W + h * DN_DK:2 * DN_W + (h + 1) * DN_DK]
            qn.append(q * lax.rsqrt(jnp.sum(q * q, axis=-1, keepdims=True) + EPS) * (DN_DK ** -0.5))
            kn.append(k * lax.rsqrt(jnp.sum(k * k, axis=-1, keepdims=True) + EPS))
            beta = beta_all[:, LANE_BETA + h:LANE_BETA + h + 1]
            gcb.append(jnp.broadcast_to(gc_all[:, LANE_DECAY + h:LANE_DECAY + h + 1], (c, DN_DK)))
            gct = jnp.broadcast_to(gct_all[LANE_DECAY + h:LANE_DECAY + h + 1, :], (c, c))
            lmask.append(jnp.where(tril, jnp.exp(jnp.where(tril, gcb[-1] - gct, 0.0)), 0.0))
            kb.append(kn[-1] * beta)
            vb.append(v * beta)

    n = range(len(chains))
    kk = [_mm_nt(kb[i], kn[i]) for i in n]
    tinv = _unit_lower_inverses([jnp.where(strict, kk[i] * lmask[i], 0.0) for i in n], ri, ci)
    egc = [jnp.exp(gcb[i]) for i in n]
    sol = [_mm(tinv[i], jnp.concatenate([vb[i], kb[i] * egc[i]], axis=1)) for i in n]
    aqk = [_mm_nt(qn[i], kn[i]) * lmask[i] for i in n]
    state = [st_scr[i] for i in n]
    glast = [gcb[i][c - 1:c, :] for i in n]
    v_new = [sol[i][:, :DN_DK] - _mm(sol[i][:, DN_DK:], state[i]) for i in n]
    o_state = [_mm(qn[i] * egc[i], state[i]) for i in n]
    o = [o_state[i] + _mm(aqk[i], v_new[i]) for i in n]
    upd = [_mm_tn(kn[i] * jnp.exp(glast[i] - gcb[i]), v_new[i]) for i in n]
    for i, (bi, h) in enumerate(chains):
        st_scr[i] = state[i] * jnp.exp(glast[i]) + upd[i]
        on = o[i] * lax.rsqrt(jnp.mean(o[i] * o[i], axis=-1, keepdims=True) + EPS) * ng
        o_ref[bi, :, h * DN_DK:(h + 1) * DN_DK] = on * _silu(z_ref[bi, :, h * DN_DK:(h + 1) * DN_DK])


def _dn_call(main3, conv_w, alog_row, dtb_row, norm_g):
    bsz, seq, _ = main3.shape
    c = DN_CHUNK
    nbt = DN_BATCH if bsz % DN_BATCH == 0 else 1
    halo_per_chunk = c // DN_HALO
    w3 = 3 * DN_W
    return pl.pallas_call(
        _dn_kernel,
        grid=(bsz // nbt, seq // c),
        in_specs=[pl.BlockSpec((nbt, c, w3), lambda b, s: (b, s, COL_QKV // w3)),
                  pl.BlockSpec((nbt, DN_HALO, w3),
                               lambda b, s: (b, jnp.maximum(s * halo_per_chunk - 1, 0), COL_QKV // w3)),
                  pl.BlockSpec((nbt, c, DN_W), lambda b, s: (b, s, COL_Z // DN_W)),
                  pl.BlockSpec((nbt, c, LANES), lambda b, s: (b, s, COL_GATE // LANES)),
                  pl.BlockSpec((SUBLANES, w3), lambda b, s: (0, 0)),
                  pl.BlockSpec((1, LANES), lambda b, s: (0, 0)),
                  pl.BlockSpec((1, LANES), lambda b, s: (0, 0)),
                  pl.BlockSpec((1, DN_DK), lambda b, s: (0, 0))],
        out_specs=pl.BlockSpec((nbt, c, DN_W), lambda b, s: (b, s, 0)),
        out_shape=jax.ShapeDtypeStruct((bsz, seq, DN_W), F32),
        scratch_shapes=[pltpu.VMEM((nbt * DN_HEADS, DN_DK, DN_DK), F32)],
        compiler_params=_cparams(("parallel", "arbitrary")),
        name="dn",
    )(main3, main3, main3, main3, conv_w, alog_row, dtb_row, norm_g.reshape(1, DN_DK))


def _merge_kernel(ya_ref, yb_ref, yc_ref, mg_ref, x_ref, gt_ref, wa_ref, wb_ref, wc_ref, wo_ref, o_ref):
    d = x_ref.shape[1]
    gates = jax.nn.sigmoid(mg_ref[...])
    merged = (gates[:, :d] * jnp.dot(ya_ref[...].astype(BF16), wa_ref[...], preferred_element_type=F32)
              + gates[:, d:2 * d] * jnp.dot(yb_ref[...].astype(BF16), wb_ref[...], preferred_element_type=F32)
              + gates[:, 2 * d:] * jnp.dot(yc_ref[...].astype(BF16), wc_ref[...], preferred_element_type=F32))
    y = jnp.dot(merged.astype(BF16), wo_ref[...], preferred_element_type=F32)
    o_ref[...] = x_ref[...] + gt_ref[0] * y


def _merge_call(ya, yb, yc, main, x2, gt, wa, wb, wc, wo, seq):
    t, d = x2.shape
    tm = min(MERGE_TM, seq)
    per_b = seq // tm
    full = lambda a: pl.BlockSpec(a.shape, lambda i: (0, 0))
    return pl.pallas_call(
        _merge_kernel,
        grid=(t // tm,),
        in_specs=[pl.BlockSpec((tm, ya.shape[1]), lambda i: (i, 0)),
                  pl.BlockSpec((tm, yb.shape[1]), lambda i: (i, 0)),
                  pl.BlockSpec((tm, yc.shape[1]), lambda i: (i, 0)),
                  pl.BlockSpec((tm, 3 * d), lambda i: (i, COL_MG // (3 * d))),
                  pl.BlockSpec((tm, d), lambda i: (i, 0)),
                  pl.BlockSpec((1, 1, d), lambda i: (i // per_b, 0, 0)),
                  full(wa), full(wb), full(wc), full(wo)],
        out_specs=pl.BlockSpec((tm, d), lambda i: (i, 0)),
        out_shape=jax.ShapeDtypeStruct((t, d), F32),
        compiler_params=_cparams(("parallel",)),
        name="merge",
    )(ya, yb, yc, main, x2, gt, wa, wb, wc, wo)


def _mlp_kernel(x_ref, g_ref, sc_ref, sh_ref, gt_ref, w1_ref, w2_ref, fg_ref, o_ref, h_scr, acc_scr, *, final):
    j = pl.program_id(1)

    @pl.when(j == 0)
    def _():
        h_scr[...] = _norm_mod(x_ref[...], g_ref[...], sc_ref[0], sh_ref[0]).astype(BF16)
        acc_scr[...] = jnp.zeros(acc_scr.shape, F32)

    a = jnp.maximum(jnp.dot(h_scr[...], w1_ref[...], preferred_element_type=F32), 0.0)
    acc_scr[...] += jnp.dot((a * a).astype(BF16), w2_ref[...], preferred_element_type=F32)

    @pl.when(j == pl.num_programs(1) - 1)
    def _():
        y = x_ref[...] + gt_ref[0] * acc_scr[...]
        if final:
            ms = jnp.mean(y * y, axis=-1, keepdims=True)
            y = y * lax.rsqrt(ms + EPS) * fg_ref[...]
        o_ref[...] = y


def _mlp_call(x2, g, sc, sh, gt, w1, w2, fg, seq, final):
    t, d = x2.shape
    f = w1.shape[1]
    tm = min(MLP_TM, seq)
    tf = MLP_TF
    per_b = seq // tm
    mod = lambda: pl.BlockSpec((1, 1, d), lambda i, j: (i // per_b, 0, 0))
    return pl.pallas_call(
        functools.partial(_mlp_kernel, final=final),
        grid=(t // tm, f // tf),
        in_specs=[pl.BlockSpec((tm, d), lambda i, j: (i, 0)),
                  pl.BlockSpec((1, d), lambda i, j: (0, 0)),
                  mod(), mod(), mod(),
                  pl.BlockSpec((d, tf), lambda i, j: (0, j)),
                  pl.BlockSpec((tf, d), lambda i, j: (j, 0)),
                  pl.BlockSpec((1, d), lambda i, j: (0, 0))],
        out_specs=pl.BlockSpec((tm, d), lambda i, j: (i, 0)),
        out_shape=jax.ShapeDtypeStruct((t, d), F32),
        scratch_shapes=[pltpu.VMEM((tm, d), BF16), pltpu.VMEM((tm, d), F32)],
        compiler_params=_cparams(("parallel", "arbitrary")),
        name="mlp",
    )(x2, g.reshape(1, d), sc, sh, gt, w1, w2, fg.reshape(1, d))


def _split_w_in(w_in):
    offs = [0]
    for sz in IN_SIZES:
        offs.append(offs[-1] + sz)
    return [w_in[:, offs[i]:offs[i + 1]] for i in range(len(IN_SIZES))]


def _layout_w_in(w_in):
    nq, nkc, nvc, nks, nvs, nkw, nvw, ngate, pin, dqkv, dz, dbeta, da, mg = _split_w_in(w_in)
    d = w_in.shape[0]
    per_g = 3 * NSA_REP
    zeros = lambda n: jnp.zeros((d, n), w_in.dtype)
    gate0 = jnp.concatenate([ngate[:, :per_g], zeros(LANE_BETA - per_g), dbeta, da,
                             zeros(LANES - LANE_DECAY - DN_HEADS)], axis=1)
    gate1 = jnp.concatenate([ngate[:, per_g:], zeros(LANES - per_g)], axis=1)
    main = jnp.concatenate([nq, pin, dz, dqkv, mg, gate0, gate1], axis=1)

    def per_group(w):
        pad = zeros(LANES - NSA_DH)
        return jnp.concatenate([w[:, :NSA_DH], pad, w[:, NSA_DH:], pad], axis=1)

    kv = jnp.concatenate([per_group(nks), per_group(nvs), per_group(nkw), per_group(nvw), nkc, nvc], axis=1)
    return main.astype(BF16), kv.astype(BF16)


def _nsa_constants(seq):
    nb = seq // SEL_LEN
    ncp = seq // CMP_STRIDE
    j = jnp.arange(nb)[:, None] * SEL_LEN
    i = jnp.arange(ncp)[None, :] * CMP_STRIDE
    overlap = jnp.clip(jnp.minimum(j + SEL_LEN, i + CMP_LEN) - jnp.maximum(j, i), 0) // CMP_STRIDE
    block_onehot = (jnp.arange(seq)[:, None] // SEL_LEN) == jnp.arange(nb)[None, :]
    return overlap.astype(BF16), block_onehot.astype(BF16)


def kernel(x, c, ada_w, ada_b, norm1_g, norm2_g, w_in, phi_k1, phi_k2, phi_v1, phi_v2, pos_k, pos_v, pool_w, pool_scale, dn_conv_w, dn_A_log, dn_dt_bias, dn_norm_g, w_branch_nsa, w_branch_pool, w_branch_dn, w_out, mlp_w1, mlp_w2, final_g):
    bsz, seq, d = x.shape
    assert d == D_MODEL and seq % max(PROJ_TM, MLP_TM, POOL_TS) == 0 and seq % KV_CHUNK == 0
    assert DN_CHUNK == DN_DK == LANES
    t = bsz * seq
    depth = ada_w.shape[0]
    chunk = KV_CHUNK
    nqt = seq // Q_TILE
    nb = seq // SEL_LEN
    blocks_per_chunk = chunk // SEL_LEN

    slopes = 2.0 ** (-(8.0 / NSA_HEADS) * (jnp.arange(NSA_HEADS, dtype=F32) + 1.0))
    overlap, block_onehot = _nsa_constants(seq)
    c_pad = jnp.zeros((SUBLANES, d), F32).at[:bsz].set(c)
    x2 = x.reshape(t, d)

    for l in range(depth):
        mod = _mod_call(c_pad, ada_w[l], ada_b[l])[:bsz]
        sh1, sc1, gt1, sh2, sc2, gt2 = [m.reshape(bsz, 1, d) for m in jnp.split(mod, 6, axis=-1)]

        w_main, w_kv = _layout_w_in(w_in[l])
        main = _proj_call(x2, norm1_g[l], sc1, sh1, w_main, seq)
        kv = _proj_kv_call(x2, norm1_g[l], sc1, sh1, w_kv, seq)

        def rows16(blk):
            a = kv[:, blk * LANES:(blk + 1) * LANES].reshape(bsz, seq // CMP_STRIDE, CMP_STRIDE, NSA_GROUPS, NSA_DH)
            return a.transpose(0, 3, 1, 2, 4).reshape(bsz, NSA_GROUPS, seq // CMP_STRIDE, CMP_STRIDE * NSA_DH)

        rows = jnp.stack([rows16(KVB_KC), rows16(KVB_VC)])
        pos = jnp.zeros((2, SUBLANES, CMP_LEN * NSA_DH), F32).at[:, 0].set(
            jnp.stack([pos_k[l].reshape(-1), pos_v[l].reshape(-1)])).astype(BF16)
        w1 = jnp.stack([phi_k1[l], phi_v1[l]]).astype(BF16)
        w2 = jnp.stack([phi_k2[l], phi_v2[l]])
        w2 = jnp.concatenate([w2, w2], axis=-1).astype(BF16)
        cmp_kv = _compress_call(rows, pos, w1, w2)

        sel, act, yc = _select_call(slopes, main, cmp_kv[0], cmp_kv[1], overlap, bsz, seq)
        flags = (act.reshape(bsz, NSA_GROUPS, nqt, nb // blocks_per_chunk, blocks_per_chunk).max(axis=-1) > 0.0)
        flags = flags.astype(jnp.int32).reshape(-1)
        y_a = _attn_call(flags, slopes, main, sel, kv, yc, block_onehot, bsz, seq, chunk)

        y_b = _pool_call(main, pool_w[l].astype(BF16), pool_scale[l], bsz, seq)

        conv_w = jnp.zeros((SUBLANES, 3 * DN_W), F32).at[:DN_CONV].set(dn_conv_w[l])
        lane_row = lambda v: jnp.zeros((1, LANES), F32).at[0, LANE_DECAY:LANE_DECAY + DN_HEADS].set(v)
        y_c = _dn_call(main.reshape(bsz, seq, N_MAIN), conv_w, lane_row(dn_A_log[l]), lane_row(dn_dt_bias[l]),
                       dn_norm_g[l]).reshape(t, DN_W)

        x2 = _merge_call(y_a, y_b, y_c, main, x2, gt1,
                         w_branch_nsa[l].astype(BF16), w_branch_pool[l].astype(BF16),
                         w_branch_dn[l].astype(BF16), w_out[l].astype(BF16), seq)
        x2 = _mlp_call(x2, norm2_g[l], sc2, sh2, gt2, mlp_w1[l].astype(BF16), mlp_w2[l].astype(BF16),
                       final_g, seq, final=(l == depth - 1))
    return x2.reshape(bsz, seq, d)
```

```python
import functools

import jax
import jax.numpy as jnp
from jax import lax
from jax.experimental import pallas as pl
from jax.experimental.pallas import tpu as pltpu

F32 = jnp.float32
BF16 = jnp.bfloat16
HIGHEST = lax.Precision.HIGHEST

D_MODEL = 1024
NSA_HEADS = 8
NSA_GROUPS = 2
NSA_REP = NSA_HEADS // NSA_GROUPS
NSA_DH = 64
CMP_LEN = 32
CMP_STRIDE = 16
SEL_LEN = 64
SEL_SHIFT = 6
TOP_N = 16
WINDOW = 512
FORCE_BONUS = 1.0e4
POOL_WINDOWS = (2, 4, 8, 16)
POOL_GROUP_DIM = 128
POOL_W = len(POOL_WINDOWS) * POOL_GROUP_DIM
DN_HEADS = 4
DN_DK = 128
DN_W = DN_HEADS * DN_DK
DN_CONV = 4
EPS = 1e-6
NEG = -1e30
NSA_Q_W = NSA_HEADS * NSA_DH
NSA_KV_W = NSA_GROUPS * NSA_DH
IN_SIZES = (NSA_Q_W, NSA_KV_W, NSA_KV_W, NSA_KV_W, NSA_KV_W, NSA_KV_W, NSA_KV_W,
            3 * NSA_HEADS, POOL_W, 3 * DN_W, DN_W, DN_HEADS, DN_HEADS, 3 * D_MODEL)

LANES = 128
LANE_SHIFT = 7
SUBLANES = 8
VMEM_LIMIT_BYTES = 48 * 1024 * 1024

Q_TILE = 128
SELECT_BUCKETS = 4
ATTN_Q_TILE = 128
KV_CHUNK = 256
DN_CHUNK = 128
DN_BATCH = 4
PROJ_TM = 1024
MLP_TM = 1024
MLP_TF = 1024
MERGE_TM = 512
POOL_TS = 1024
POOL_HALO = 16
DN_HALO = 8

COL_Q = 0
COL_POOL = 512
COL_Z = 1024
COL_QKV = 1536
COL_MG = 3072
COL_GATE = 6144
COL_KC = 6400
COL_VC = 6528
N_MAIN = 6656
PROJ_TN = 1664
LANE_BETA = 16
LANE_DECAY = 20
KVB_KS = 0
KVB_VS = 2
KVB_KW = 4
KVB_VW = 6
KVB_END = 8
N_KV = KVB_END * LANES
LANE_POS_HI = NSA_DH
LANE_POS_LO = NSA_DH + 1


def _cparams(sem):
    return pltpu.CompilerParams(dimension_semantics=sem, vmem_limit_bytes=VMEM_LIMIT_BYTES)


def _silu(v):
    return v * jax.nn.sigmoid(v)


def _mm(a, b):
    return jnp.dot(a.astype(BF16), b.astype(BF16), preferred_element_type=F32)


def _mm_nt(a, b):
    return lax.dot_general(a.astype(BF16), b.astype(BF16), (((1,), (1,)), ((), ())), preferred_element_type=F32)


def _mm_tn(a, b):
    return lax.dot_general(a.astype(BF16), b.astype(BF16), (((0,), (0,)), ((), ())), preferred_element_type=F32)


def _mod_kernel(c_ref, w_ref, b_ref, o_ref):
    cond = _silu(c_ref[...])
    o_ref[...] = jnp.dot(cond, w_ref[...], preferred_element_type=F32, precision=HIGHEST) + b_ref[...]


def _mod_call(c_pad, w, b):
    d, n = w.shape
    tn = 1536
    return pl.pallas_call(
        _mod_kernel,
        grid=(n // tn,),
        in_specs=[pl.BlockSpec((SUBLANES, d), lambda j: (0, 0)),
                  pl.BlockSpec((d, tn), lambda j: (0, j)),
                  pl.BlockSpec((1, tn), lambda j: (0, j))],
        out_specs=pl.BlockSpec((SUBLANES, tn), lambda j: (0, j)),
        out_shape=jax.ShapeDtypeStruct((SUBLANES, n), F32),
        compiler_params=_cparams(("parallel",)),
        name="mod",
    )(c_pad, w, b.reshape(1, n))


def _norm_mod(x, g, sc, sh):
    ms = jnp.mean(x * x, axis=-1, keepdims=True)
    return (x * lax.rsqrt(ms + EPS) * g) * (1.0 + sc) + sh


def _proj_kernel(x_ref, g_ref, sc_ref, sh_ref, w_ref, o_ref, h_scr):
    @pl.when(pl.program_id(1) == 0)
    def _():
        h_scr[...] = _norm_mod(x_ref[...], g_ref[...], sc_ref[0], sh_ref[0]).astype(BF16)

    o_ref[...] = jnp.dot(h_scr[...], w_ref[...], preferred_element_type=F32)


def _proj_kv_kernel(x_ref, g_ref, sc_ref, sh_ref, w_ref, o_ref, *, tiles_per_seq):
    h = _norm_mod(x_ref[...], g_ref[...], sc_ref[0], sh_ref[0]).astype(BF16)
    acc = jnp.dot(h, w_ref[...], preferred_element_type=F32)
    tm, n = acc.shape
    col = lax.broadcasted_iota(jnp.int32, (1, n), 1)
    blk = jnp.right_shift(col, LANE_SHIFT)
    lane = col & (LANES - 1)
    is_key = ((blk >= KVB_KS) & (blk < KVB_VS)) | ((blk >= KVB_KW) & (blk < KVB_VW))
    is_val = ((blk >= KVB_VS) & (blk < KVB_KW)) | ((blk >= KVB_VW) & (blk < KVB_END))
    ones_row = jnp.where(is_val & (lane >= NSA_DH), 1.0, 0.0)
    hi_row = jnp.where(is_key & (lane == LANE_POS_HI), 1.0, 0.0)
    lo_row = jnp.where(is_key & (lane == LANE_POS_LO), 1.0, 0.0)
    pos = (pl.program_id(0) % tiles_per_seq) * tm + lax.broadcasted_iota(jnp.int32, (tm, 1), 0)
    pos_hi = jnp.right_shift(pos, LANE_SHIFT).astype(F32)
    pos_lo = (pos & (LANES - 1)).astype(F32)
    o_ref[...] = (acc + ones_row + pos_hi * hi_row + pos_lo * lo_row).astype(o_ref.dtype)


def _proj_specs(tm, d, tn, per_b):
    return [pl.BlockSpec((tm, d), lambda i, j: (i, 0)),
            pl.BlockSpec((1, d), lambda i, j: (0, 0)),
            pl.BlockSpec((1, 1, d), lambda i, j: (i // per_b, 0, 0)),
            pl.BlockSpec((1, 1, d), lambda i, j: (i // per_b, 0, 0)),
            pl.BlockSpec((d, tn), lambda i, j: (0, j))]


def _proj_call(x2, g, sc, sh, w, seq):
    t, d = x2.shape
    n = w.shape[1]
    tm = min(PROJ_TM, seq)
    tn = PROJ_TN
    return pl.pallas_call(
        _proj_kernel,
        grid=(t // tm, n // tn),
        in_specs=_proj_specs(tm, d, tn, seq // tm),
        out_specs=pl.BlockSpec((tm, tn), lambda i, j: (i, j)),
        out_shape=jax.ShapeDtypeStruct((t, n), F32),
        scratch_shapes=[pltpu.VMEM((tm, d), BF16)],
        compiler_params=_cparams(("parallel", "arbitrary")),
        name="proj_main",
    )(x2, g.reshape(1, d), sc, sh, w)


def _proj_kv_call(x2, g, sc, sh, w, seq):
    t, d = x2.shape
    n = w.shape[1]
    tm = min(PROJ_TM, seq)
    return pl.pallas_call(
        functools.partial(_proj_kv_kernel, tiles_per_seq=seq // tm),
        grid=(t // tm, 1),
        in_specs=_proj_specs(tm, d, n, seq // tm),
        out_specs=pl.BlockSpec((tm, n), lambda i, j: (i, j)),
        out_shape=jax.ShapeDtypeStruct((t, n), BF16),
        compiler_params=_cparams(("parallel", "arbitrary")),
        name="proj_kv",
    )(x2, g.reshape(1, d), sc, sh, w)


def _compress_kernel(x_ref, pos_ref, w1_ref, w1g_ref, w2_ref, o_ref):
    nr = x_ref.shape[1] // CMP_STRIDE
    xs = jnp.concatenate([x_ref[0, pl.ds(rho, nr, stride=CMP_STRIDE), :].astype(BF16) for rho in range(CMP_STRIDE)],
                         axis=1)
    posb = jnp.dot(pos_ref[0], w1_ref[0], preferred_element_type=F32)[0:1]
    for g in range(NSA_GROUPS):
        top = jnp.dot(xs, w1g_ref[0, g, 0], preferred_element_type=F32)
        bot = jnp.dot(xs, w1g_ref[0, g, 1], preferred_element_type=F32)
        hid = top + pltpu.roll(bot, nr - 1, 0) + posb
        o_ref[0, 0, g] = _mm(_silu(hid), w2_ref[0]).astype(o_ref.dtype)


def _compress_call(main3, pos, w1, w1g, w2):
    b, seq, _ = main3.shape
    nr = seq // CMP_STRIDE
    two, l64, hid = w1.shape
    return pl.pallas_call(
        _compress_kernel,
        grid=(two, b),
        in_specs=[pl.BlockSpec((1, seq, LANES), lambda a, i: (i, 0, COL_KC // LANES + a)),
                  pl.BlockSpec((1, SUBLANES, l64), lambda a, i: (a, 0, 0)),
                  pl.BlockSpec((1, l64, hid), lambda a, i: (a, 0, 0)),
                  pl.BlockSpec((1,) + w1g.shape[1:], lambda a, i: (a, 0, 0, 0, 0)),
                  pl.BlockSpec((1, hid, LANES), lambda a, i: (a, 0, 0))],
        out_specs=pl.BlockSpec((1, 1, NSA_GROUPS, nr, LANES), lambda a, i: (a, i, 0, 0, 0)),
        out_shape=jax.ShapeDtypeStruct((two, b, NSA_GROUPS, nr, LANES), BF16),
        compiler_params=_cparams(("parallel", "parallel")),
        name="compress",
    )(main3, pos, w1, w1g, w2)


def _gate_cols(gate_logits, branch):
    sg = jax.nn.sigmoid(gate_logits)
    return [sg[:, 3 * r + branch:3 * r + branch + 1] for r in range(NSA_REP)]


def _stack_heads_dup(q):
    qa = q[:, :LANES]
    qb = q[:, LANES:]
    lo = lax.broadcasted_iota(jnp.int32, qa.shape, 1) < NSA_DH
    z = jnp.zeros_like(qa)
    return jnp.concatenate([jnp.where(lo, qa, z), jnp.where(lo, z, qa),
                            jnp.where(lo, qb, z), jnp.where(lo, z, qb)], axis=0)


def _unstack_heads_dup(o, gates):
    nq = o.shape[0] // NSA_REP
    lo = lax.broadcasted_iota(jnp.int32, (nq, LANES), 1) < NSA_DH
    a = jnp.where(lo, gates[0] * o[0:nq], gates[1] * o[nq:2 * nq])
    b = jnp.where(lo, gates[2] * o[2 * nq:3 * nq], gates[3] * o[3 * nq:])
    return jnp.concatenate([a, b], axis=1)


def _stack_heads_pos(q, slopes):
    qa = q[:, :LANES]
    qb = q[:, LANES:]
    lane = lax.broadcasted_iota(jnp.int32, qa.shape, 1)
    lo = lane < NSA_DH
    heads = [qa, pltpu.roll(qa, NSA_DH, 1), qb, pltpu.roll(qb, NSA_DH, 1)]
    out = []
    for r in range(NSA_REP):
        extra = jnp.where(lane == LANE_POS_HI, slopes[r] * LANES, jnp.where(lane == LANE_POS_LO, slopes[r], 0.0))
        out.append(jnp.where(lo, heads[r], extra))
    return jnp.concatenate(out, axis=0)


def _finish_heads(acc, gates):
    nq = acc.shape[0] // NSA_REP
    o = acc * pltpu.roll(1.0 / acc, NSA_DH, 1)
    lo = lax.broadcasted_iota(jnp.int32, (nq, LANES), 1) < NSA_DH
    hs = [gates[r] * o[r * nq:(r + 1) * nq] for r in range(NSA_REP)]
    a = jnp.where(lo, hs[0], pltpu.roll(hs[1], NSA_DH, 1))
    b = jnp.where(lo, hs[2], pltpu.roll(hs[3], NSA_DH, 1))
    return jnp.concatenate([a, b], axis=1)


def _select_kernel(slopes_ref, q_ref, kc_ref, vc_ref, gates_ref, ov_ref, sel_ref, act_ref, yc_ref):
    qt = pl.program_id(1)
    nqt = pl.num_programs(1)
    nb_all, ncp_all = ov_ref.shape
    for k in range(SELECT_BUCKETS):
        @pl.when((qt * SELECT_BUCKETS) // nqt == k)
        def _(k=k):
            _select_prefix(slopes_ref, q_ref, kc_ref, vc_ref, gates_ref, ov_ref, sel_ref, act_ref, yc_ref,
                           nb=(k + 1) * nb_all // SELECT_BUCKETS, ncp=(k + 1) * ncp_all // SELECT_BUCKETS)


def _select_prefix(slopes_ref, q_ref, kc_ref, vc_ref, gates_ref, ov_ref, sel_ref, act_ref, yc_ref, *, nb, ncp):
    t0 = pl.program_id(1) * Q_TILE
    nq = Q_TILE
    nb_all = ov_ref.shape[0]
    c_end = lax.broadcasted_iota(jnp.int32, (nq, ncp), 1) * CMP_STRIDE + (CMP_LEN - 1)
    tq = t0 + lax.broadcasted_iota(jnp.int32, (nq, ncp), 0)
    vis = c_end <= tq
    rel = (c_end[0:1] - t0).astype(F32)
    ov = ov_ref[:nb, :ncp]
    jb = lax.broadcasted_iota(jnp.int32, (nb, nq), 0)
    tql = t0 + lax.broadcasted_iota(jnp.int32, (nb, nq), 1)
    valid = jb * SEL_LEN <= tql
    forced = (jb == 0) | (jb == jnp.right_shift(tql, SEL_SHIFT))
    scores = []
    for g in range(NSA_GROUPS):
        qcols = slice(g * NSA_REP * NSA_DH, (g + 1) * NSA_REP * NSA_DH)
        qs = _stack_heads_dup(q_ref[:, qcols] * (NSA_DH ** -0.5)).astype(BF16)
        s = _mm_nt(qs, kc_ref[0, g, :ncp, :])
        psum = jnp.zeros((nq, ncp), F32)
        ps = []
        for r in range(NSA_REP):
            m_r = slopes_ref[g * NSA_REP + r]
            sr = jnp.where(vis, s[r * nq:(r + 1) * nq] + m_r * rel, NEG)
            mx = jnp.max(sr, axis=-1, keepdims=True)
            e = jnp.where(vis, jnp.exp(sr - mx), 0.0)
            den = jnp.sum(e, axis=-1, keepdims=True)
            p = e * (1.0 / jnp.where(den > 0.0, den, 1.0))
            psum = psum + p
            ps.append(p.astype(BF16))
        o_c = jnp.dot(jnp.concatenate(ps, axis=0), vc_ref[0, g, :ncp, :], preferred_element_type=F32)
        yc_ref[:, qcols] = _unstack_heads_dup(o_c, _gate_cols(gates_ref[:, g * LANES:(g + 1) * LANES], 0))
        p_hi = psum.astype(BF16)
        p_lo = (psum - p_hi.astype(F32)).astype(BF16)
        imp_t = _mm_nt(ov, p_hi) + _mm_nt(ov, p_lo)
        scores.append(jnp.where(valid, imp_t + FORCE_BONUS * forced.astype(F32), NEG))

    sels = [jnp.zeros((nb, nq), jnp.bool_) for _ in range(NSA_GROUPS)]
    for _ in range(min(TOP_N, nb)):
        for g in range(NSA_GROUPS):
            mx = jnp.max(scores[g], axis=0, keepdims=True)
            idx = jnp.min(jnp.where(scores[g] == mx, jb, nb), axis=0, keepdims=True)
            hit = jb == idx
            sels[g] = sels[g] | hit
            scores[g] = jnp.where(hit, -jnp.inf, scores[g])
    for g in range(NSA_GROUPS):
        sel_t = jnp.where(sels[g] & valid, 1.0, 0.0)
        if nb < nb_all:
            sel_t = jnp.concatenate([sel_t, jnp.zeros((nb_all - nb, nq), F32)], axis=0)
        sel_q = sel_t.T
        sel_ref[0, g] = sel_q.astype(sel_ref.dtype)
        act_ref[0, g, 0] = jnp.max(sel_q, axis=0, keepdims=True)


def _select_call(slopes, main, kcmp, vcmp, ov, bsz, seq):
    nqt = seq // Q_TILE
    nb, ncp = ov.shape
    row = lambda b, q: b * nqt + q
    gw = NSA_GROUPS * LANES
    return pl.pallas_call(
        _select_kernel,
        grid=(bsz, nqt),
        in_specs=[pl.BlockSpec(memory_space=pltpu.SMEM),
                  pl.BlockSpec((Q_TILE, NSA_Q_W), lambda b, q: (row(b, q), COL_Q // NSA_Q_W)),
                  pl.BlockSpec((1, NSA_GROUPS, ncp, LANES), lambda b, q: (b, 0, 0, 0)),
                  pl.BlockSpec((1, NSA_GROUPS, ncp, LANES), lambda b, q: (b, 0, 0, 0)),
                  pl.BlockSpec((Q_TILE, gw), lambda b, q: (row(b, q), COL_GATE // gw)),
                  pl.BlockSpec((nb, ncp), lambda b, q: (0, 0))],
        out_specs=[pl.BlockSpec((1, NSA_GROUPS, Q_TILE, nb), lambda b, q: (b, 0, q, 0)),
                   pl.BlockSpec((1, NSA_GROUPS, 1, 1, nb), lambda b, q: (b, 0, q, 0, 0)),
                   pl.BlockSpec((Q_TILE, NSA_Q_W), lambda b, q: (row(b, q), 0))],
        out_shape=[jax.ShapeDtypeStruct((bsz, NSA_GROUPS, seq, nb), BF16),
                   jax.ShapeDtypeStruct((bsz, NSA_GROUPS, nqt, 1, nb), F32),
                   jax.ShapeDtypeStruct((bsz * seq, NSA_Q_W), F32)],
        compiler_params=_cparams(("parallel", "parallel")),
        name="select",
    )(slopes, main, kcmp, vcmp, main, ov)


def _attn_kernel(lst_ref, cnt_ref, slopes_ref, q_ref, sel_ref, ks_ref, vs_ref, kw_ref, vw_ref, gates_ref, yc_ref,
                 et_ref, o_ref, s_scr, p_scr, m_scr, a_scr, acc_scr, *, chunk):
    b = pl.program_id(0)
    g = pl.program_id(1)
    qt = pl.program_id(2)
    nqt = pl.num_programs(2)
    t0 = qt * ATTN_Q_TILE
    nq = ATTN_Q_TILE
    nch_total = ks_ref.shape[0] // chunk
    slopes = [slopes_ref[g * NSA_REP + r] for r in range(NSA_REP)]
    qs = _stack_heads_pos(q_ref[...] * (NSA_DH ** -0.5), slopes).astype(BF16)
    pen = ((sel_ref[0, 0].astype(F32) - 1.0) * (-NEG)).astype(BF16)
    qx = jnp.concatenate([qs, jnp.concatenate([pen] * NSA_REP, axis=0)], axis=1)

    tile = (b * NSA_GROUPS + g) * nqt + qt
    base = tile * nch_total
    cnt = cnt_ref[tile]

    def logits(c):
        start = pl.multiple_of(c * chunk, chunk)
        kx = jnp.concatenate([ks_ref[pl.ds(start, chunk), :], et_ref[pl.ds(start, chunk), :]], axis=1)
        return _mm_nt(qx, kx)

    def values(p, c):
        start = pl.multiple_of(c * chunk, chunk)
        return jnp.dot(p, vs_ref[pl.ds(start, chunk), :], preferred_element_type=F32)

    m_scr[...] = jnp.full(m_scr.shape, NEG, F32)
    a_scr[...] = jnp.ones(a_scr.shape, F32)
    acc_scr[...] = jnp.zeros(acc_scr.shape, F32)
    p_scr[1] = jnp.zeros(p_scr.shape[1:], BF16)
    s_scr[0] = logits(lst_ref[base])

    def body(i, carry):
        slot = i & 1
        acc_scr[...] = a_scr[...] * acc_scr[...] + values(p_scr[1 - slot], lst_ref[base + jnp.maximum(i - 1, 0)])
        s = s_scr[slot]
        m_prev = m_scr[...]
        m_new = jnp.maximum(m_prev, jnp.max(s, axis=-1, keepdims=True))
        a_scr[...] = jnp.exp(m_prev - m_new)
        p_scr[slot] = jnp.exp(s - m_new).astype(BF16)
        m_scr[...] = m_new
        s_scr[1 - slot] = logits(lst_ref[base + i + 1])
        return carry

    lax.fori_loop(0, cnt, body, 0)
    slot = cnt & 1
    acc = a_scr[...] * acc_scr[...] + values(p_scr[1 - slot], lst_ref[base + jnp.maximum(cnt - 1, 0)])
    c_diag = t0 // chunk
    s = s_scr[slot]
    pos = c_diag * chunk + lax.broadcasted_iota(jnp.int32, (nq, chunk), 1)
    tq = t0 + lax.broadcasted_iota(jnp.int32, (nq, chunk), 0)
    ok = pos <= tq
    s = jnp.concatenate([jnp.where(ok, s[r * nq:(r + 1) * nq], NEG) for r in range(NSA_REP)], axis=0)
    m_prev = m_scr[...]
    m_new = jnp.maximum(m_prev, jnp.max(s, axis=-1, keepdims=True))
    acc = jnp.exp(m_prev - m_new) * acc + values(jnp.exp(s - m_new).astype(BF16), c_diag)
    y_sel = _finish_heads(acc, _gate_cols(gates_ref[...], 1))

    nw = WINDOW + nq
    base = pl.multiple_of(jnp.maximum(t0 - WINDOW, 0), nq)
    s = _mm_nt(qs, kw_ref[pl.ds(base, nw), :])
    pos = base + lax.broadcasted_iota(jnp.int32, (nq, nw), 1)
    tq = t0 + lax.broadcasted_iota(jnp.int32, (nq, nw), 0)
    ok = (pos <= tq) & (pos > tq - WINDOW)
    s = jnp.concatenate([jnp.where(ok, s[r * nq:(r + 1) * nq], NEG) for r in range(NSA_REP)], axis=0)
    p = jnp.exp(s - jnp.max(s, axis=-1, keepdims=True)).astype(BF16)
    acc_w = jnp.dot(p, vw_ref[pl.ds(base, nw), :], preferred_element_type=F32)
    o_ref[...] = yc_ref[...] + y_sel + _finish_heads(acc_w, _gate_cols(gates_ref[...], 2))


def _chunk_lists(act, seq, chunk):
    bsz, ngrp = act.shape[:2]
    nqt = seq // ATTN_Q_TILE
    nch = seq // chunk
    active = act.reshape(bsz, ngrp, nqt, ATTN_Q_TILE // Q_TILE, nch, chunk // SEL_LEN).max(axis=(3, 5)) > 0.0
    c_diag = (jnp.arange(nqt) * ATTN_Q_TILE) // chunk
    active = active & (jnp.arange(nch)[None, :] < c_diag[:, None])
    cnt = active.sum(axis=-1).astype(jnp.int32)
    order = jnp.argsort(jnp.logical_not(active), axis=-1, stable=True).astype(jnp.int32)
    lst = jnp.where(jnp.arange(nch) < cnt[..., None], order, c_diag[:, None].astype(jnp.int32))
    return lst.reshape(-1), cnt.reshape(-1)


def _attn_call(lst, cnt, slopes, main, sel, kv, yc, et, bsz, seq, chunk):
    tq = ATTN_Q_TILE
    nqt = seq // tq
    nb = sel.shape[-1]
    rows = NSA_REP * tq
    row = lambda b, g, q: b * nqt + q
    kvspec = lambda blk: pl.BlockSpec((seq, LANES), lambda b, g, q, *_: (b, blk + g))
    grid_spec = pltpu.PrefetchScalarGridSpec(
        num_scalar_prefetch=2,
        grid=(bsz, NSA_GROUPS, nqt),
        in_specs=[pl.BlockSpec(memory_space=pltpu.SMEM),
                  pl.BlockSpec((tq, 256), lambda b, g, q, *_: (row(b, g, q), COL_Q // 256 + g)),
                  pl.BlockSpec((1, 1, tq, nb), lambda b, g, q, *_: (b, g, q, 0)),
                  kvspec(KVB_KS), kvspec(KVB_VS), kvspec(KVB_KW), kvspec(KVB_VW),
                  pl.BlockSpec((tq, LANES), lambda b, g, q, *_: (row(b, g, q), COL_GATE // LANES + g)),
                  pl.BlockSpec((tq, 256), lambda b, g, q, *_: (row(b, g, q), g)),
                  pl.BlockSpec(et.shape, lambda b, g, q, *_: (0, 0))],
        out_specs=pl.BlockSpec((tq, 256), lambda b, g, q, *_: (row(b, g, q), g)),
        scratch_shapes=[pltpu.VMEM((2, rows, chunk), F32),
                        pltpu.VMEM((2, rows, chunk), BF16),
                        pltpu.VMEM((rows, 1), F32),
                        pltpu.VMEM((rows, 1), F32),
                        pltpu.VMEM((rows, LANES), F32)],
    )
    return pl.pallas_call(
        functools.partial(_attn_kernel, chunk=chunk),
        grid_spec=grid_spec,
        out_shape=jax.ShapeDtypeStruct((bsz * seq, NSA_Q_W), F32),
        compiler_params=_cparams(("parallel", "parallel", "parallel")),
        name="attn",
    )(lst, cnt, slopes, main, sel, kv, kv, kv, kv, main, yc, et)


def _pool_kernel(x_ref, halo_ref, w_ref, sc_ref, o_ref):
    s = pl.program_id(1)
    ts = x_ref.shape[0]
    x = x_ref[...]
    halo = jnp.where(s > 0, halo_ref[...], 0.0)
    t = s * ts + lax.broadcasted_iota(jnp.int32, (ts, 1), 0)
    for gi, win in enumerate(POOL_WINDOWS):
        lanes = slice(gi * POOL_GROUP_DIM, (gi + 1) * POOL_GROUP_DIM)
        ext = jnp.concatenate([halo[:, lanes], x[:, lanes]], axis=0)
        acc = ext
        span = 1
        while span < win:
            acc = acc + pltpu.roll(acc, span, 0)
            span *= 2
        cnt = jnp.minimum(t + 1, win).astype(F32)
        y = acc[POOL_HALO:] / cnt - x[:, lanes]
        o_ref[:, lanes] = _mm(y, w_ref[gi]) * sc_ref[:, lanes]


def _pool_call(main, w, scale, bsz, seq):
    ts = min(POOL_TS, seq)
    per_b = seq // ts
    halo_per_tile = ts // POOL_HALO
    return pl.pallas_call(
        _pool_kernel,
        grid=(bsz, per_b),
        in_specs=[pl.BlockSpec((ts, POOL_W), lambda b, s: (b * per_b + s, COL_POOL // POOL_W)),
                  pl.BlockSpec((POOL_HALO, POOL_W),
                               lambda b, s: (jnp.maximum((b * per_b + s) * halo_per_tile - 1, 0), COL_POOL // POOL_W)),
                  pl.BlockSpec(w.shape, lambda b, s: (0, 0, 0)),
                  pl.BlockSpec((1, POOL_W), lambda b, s: (0, 0))],
        out_specs=pl.BlockSpec((ts, POOL_W), lambda b, s: (b * per_b + s, 0)),
        out_shape=jax.ShapeDtypeStruct((bsz * seq, POOL_W), F32),
        compiler_params=_cparams(("parallel", "parallel")),
        name="pool",
    )(main, main, w, scale.reshape(1, POOL_W))


def _unit_lower_inverses(nms, ri, ci):
    c = nms[0].shape[0]
    eye = jnp.where(ri == ci, 1.0, 0.0)
    in8 = jnp.right_shift(ri, 3) == jnp.right_shift(ci, 3)
    n8 = [jnp.where(in8, nm, 0.0) for nm in nms]
    ps = [eye - a for a in n8]
    ms = [_mm(a, a) for a in n8]
    ps = [p + _mm(p, m) for p, m in zip(ps, ms)]
    ms = [_mm(m, m) for m in ms]
    ps = [p + _mm(p, m) for p, m in zip(ps, ms)]
    shift = 3
    while (1 << shift) < c:
        rb = jnp.right_shift(ri, shift)
        cb = jnp.right_shift(ci, shift)
        lower_left = ((rb & 1) == 1) & (cb == rb - 1)
        ts = [_mm(p, jnp.where(lower_left, nm, 0.0)) for p, nm in zip(ps, nms)]
        ps = [p - _mm(t, p) for p, t in zip(ps, ts)]
        shift += 1
    return ps


def _row_cumsum(x, ri):
    span = 1
    while span < x.shape[0]:
        x = x + jnp.where(ri >= span, pltpu.roll(x, span, 0), 0.0)
        span *= 2
    return x


def _dn_kernel(qkv_ref, halo_ref, z_ref, gba_ref, cw_ref, alog_ref, dtb_ref, ng_ref, o_ref, st_scr):
    s = pl.program_id(1)
    nbatch, c = qkv_ref.shape[0], qkv_ref.shape[1]

    @pl.when(s == 0)
    def _():
        st_scr[...] = jnp.zeros(st_scr.shape, F32)

    ri = lax.broadcasted_iota(jnp.int32, (c, c), 0)
    ci = lax.broadcasted_iota(jnp.int32, (c, c), 1)
    tril = ri >= ci
    strict = ri > ci
    cw = cw_ref[...]
    ng = ng_ref[...]
    chains = [(bi, h) for bi in range(nbatch) for h in range(DN_HEADS)]
    qn, kn, kb, vb, gcb, lmask = [], [], [], [], [], []
    for bi in range(nbatch):
        halo = jnp.where(s > 0, halo_ref[bi], 0.0)
        ext = jnp.concatenate([halo, qkv_ref[bi]], axis=0)
        conv = cw[DN_CONV - 1:DN_CONV] * ext
        for tap in range(1, DN_CONV):
            conv = conv + cw[DN_CONV - 1 - tap:DN_CONV - tap] * pltpu.roll(ext, tap, 0)
        act = _silu(conv[DN_HALO:])

        gba = gba_ref[bi]
        beta_all = jax.nn.sigmoid(gba)
        xs = gba + dtb_ref[...]
        softplus = jnp.maximum(xs, 0.0) + jnp.log(1.0 + jnp.exp(-jnp.abs(xs)))
        gc_all = _row_cumsum(-jnp.exp(alog_ref[...]) * softplus, ri)
        gct_all = gc_all.T
        for h in range(DN_HEADS):
            q = act[:, h * DN_DK:(h + 1) * DN_DK]
            k = act[:, DN_W + h * DN_DK:DN_W + (h + 1) * DN_DK]
            v = act[:, 2 * DN_W + h * DN_DK:2 * DN_W + (h + 1) * DN_DK]
            qn.append(q * lax.rsqrt(jnp.sum(q * q, axis=-1, keepdims=True) + EPS) * (DN_DK ** -0.5))
            kn.append(k * lax.rsqrt(jnp.sum(k * k, axis=-1, keepdims=True) + EPS))
            beta = beta_all[:, LANE_BETA + h:LANE_BETA + h + 1]
            gcb.append(jnp.broadcast_to(gc_all[:, LANE_DECAY + h:LANE_DECAY + h + 1], (c, DN_DK)))
            gct = jnp.broadcast_to(gct_all[LANE_DECAY + h:LANE_DECAY + h + 1, :], (c, c))
            lmask.append(jnp.where(tril, jnp.exp(jnp.where(tril, gcb[-1] - gct, 0.0)), 0.0))
            kb.append(kn[-1] * beta)
            vb.append(v * beta)

    n = range(len(chains))
    kk = [_mm_nt(kb[i], kn[i]) for i in n]
    tinv = _unit_lower_inverses([jnp.where(strict, kk[i] * lmask[i], 0.0) for i in n], ri, ci)
    egc = [jnp.exp(gcb[i]) for i in n]
    sol = [_mm(tinv[i], jnp.concatenate([vb[i], kb[i] * egc[i]], axis=1)) for i in n]
    aqk = [_mm_nt(qn[i], kn[i]) * lmask[i] for i in n]
    state = [st_scr[i] for i in n]
    glast = [gcb[i][c - 1:c, :] for i in n]
    v_new = [sol[i][:, :DN_DK] - _mm(sol[i][:, DN_DK:], state[i]) for i in n]
    o_state = [_mm(qn[i] * egc[i], state[i]) for i in n]
    o = [o_state[i] + _mm(aqk[i], v_new[i]) for i in n]
    upd = [_mm_tn(kn[i] * jnp.exp(glast[i] - gcb[i]), v_new[i]) for i in n]
    for i, (bi, h) in enumerate(chains):
        st_scr[i] = state[i] * jnp.exp(glast[i]) + upd[i]
        on = o[i] * lax.rsqrt(jnp.mean(o[i] * o[i], axis=-1, keepdims=True) + EPS) * ng
        o_ref[bi, :, h * DN_DK:(h + 1) * DN_DK] = on * _silu(z_ref[bi, :, h * DN_DK:(h + 1) * DN_DK])


def _dn_call(main3, conv_w, alog_row, dtb_row, norm_g):
    bsz, seq, _ = main3.shape
    c = DN_CHUNK
    nbt = DN_BATCH if bsz % DN_BATCH == 0 else 1
    halo_per_chunk = c // DN_HALO
    w3 = 3 * DN_W
    return pl.pallas_call(
        _dn_kernel,
        grid=(bsz // nbt, seq // c),
        in_specs=[pl.BlockSpec((nbt, c, w3), lambda b, s: (b, s, COL_QKV // w3)),
                  pl.BlockSpec((nbt, DN_HALO, w3),
                               lambda b, s: (b, jnp.maximum(s * halo_per_chunk - 1, 0), COL_QKV // w3)),
                  pl.BlockSpec((nbt, c, DN_W), lambda b, s: (b, s, COL_Z // DN_W)),
                  pl.BlockSpec((nbt, c, LANES), lambda b, s: (b, s, COL_GATE // LANES)),
                  pl.BlockSpec((SUBLANES, w3), lambda b, s: (0, 0)),
                  pl.BlockSpec((1, LANES), lambda b, s: (0, 0)),
                  pl.BlockSpec((1, LANES), lambda b, s: (0, 0)),
                  pl.BlockSpec((1, DN_DK), lambda b, s: (0, 0))],
        out_specs=pl.BlockSpec((nbt, c, DN_W), lambda b, s: (b, s, 0)),
        out_shape=jax.ShapeDtypeStruct((bsz, seq, DN_W), F32),
        scratch_shapes=[pltpu.VMEM((nbt * DN_HEADS, DN_DK, DN_DK), F32)],
        compiler_params=_cparams(("parallel", "arbitrary")),
        name="dn",
    )(main3, main3, main3, main3, conv_w, alog_row, dtb_row, norm_g.reshape(1, DN_DK))


def _merge_kernel(ya_ref, yb_ref, yc_ref, mg_ref, x_ref, gt_ref, wa_ref, wb_ref, wc_ref, wo_ref, o_ref):
    d = x_ref.shape[1]
    gates = jax.nn.sigmoid(mg_ref[...])
    merged = (gates[:, :d] * jnp.dot(ya_ref[...].astype(BF16), wa_ref[...], preferred_element_type=F32)
              + gates[:, d:2 * d] * jnp.dot(yb_ref[...].astype(BF16), wb_ref[...], preferred_element_type=F32)
              + gates[:, 2 * d:] * jnp.dot(yc_ref[...].astype(BF16), wc_ref[...], preferred_element_type=F32))
    y = jnp.dot(merged.astype(BF16), wo_ref[...], preferred_element_type=F32)
    o_ref[...] = x_ref[...] + gt_ref[0] * y


def _merge_call(ya, yb, yc, main, x2, gt, wa, wb, wc, wo, seq):
    t, d = x2.shape
    tm = min(MERGE_TM, seq)
    per_b = seq // tm
    full = lambda a: pl.BlockSpec(a.shape, lambda i: (0, 0))
    return pl.pallas_call(
        _merge_kernel,
        grid=(t // tm,),
        in_specs=[pl.BlockSpec((tm, ya.shape[1]), lambda i: (i, 0)),
                  pl.BlockSpec((tm, yb.shape[1]), lambda i: (i, 0)),
                  pl.BlockSpec((tm, yc.shape[1]), lambda i: (i, 0)),
                  pl.BlockSpec((tm, 3 * d), lambda i: (i, COL_MG // (3 * d))),
                  pl.BlockSpec((tm, d), lambda i: (i, 0)),
                  pl.BlockSpec((1, 1, d), lambda i: (i // per_b, 0, 0)),
                  full(wa), full(wb), full(wc), full(wo)],
        out_specs=pl.BlockSpec((tm, d), lambda i: (i, 0)),
        out_shape=jax.ShapeDtypeStruct((t, d), F32),
        compiler_params=_cparams(("parallel",)),
        name="merge",
    )(ya, yb, yc, main, x2, gt, wa, wb, wc, wo)


def _mlp_kernel(x_ref, g_ref, sc_ref, sh_ref, gt_ref, w1_ref, w2_ref, fg_ref, o_ref, h_scr, acc_scr, *, final):
    j = pl.program_id(1)

    @pl.when(j == 0)
    def _():
        h_scr[...] = _norm_mod(x_ref[...], g_ref[...], sc_ref[0], sh_ref[0]).astype(BF16)
        acc_scr[...] = jnp.zeros(acc_scr.shape, F32)

    a = jnp.maximum(jnp.dot(h_scr[...], w1_ref[...], preferred_element_type=F32), 0.0)
    acc_scr[...] += jnp.dot((a * a).astype(BF16), w2_ref[...], preferred_element_type=F32)

    @pl.when(j == pl.num_programs(1) - 1)
    def _():
        y = x_ref[...] + gt_ref[0] * acc_scr[...]
        if final:
            ms = jnp.mean(y * y, axis=-1, keepdims=True)
            y = y * lax.rsqrt(ms + EPS) * fg_ref[...]
        o_ref[...] = y


def _mlp_call(x2, g, sc, sh, gt, w1, w2, fg, seq, final):
    t, d = x2.shape
    f = w1.shape[1]
    tm = min(MLP_TM, seq)
    tf = MLP_TF
    per_b = seq // tm
    mod = lambda: pl.BlockSpec((1, 1, d), lambda i, j: (i // per_b, 0, 0))
    return pl.pallas_call(
        functools.partial(_mlp_kernel, final=final),
        grid=(t // tm, f // tf),
        in_specs=[pl.BlockSpec((tm, d), lambda i, j: (i, 0)),
                  pl.BlockSpec((1, d), lambda i, j: (0, 0)),
                  mod(), mod(), mod(),
                  pl.BlockSpec((d, tf), lambda i, j: (0, j)),
                  pl.BlockSpec((tf, d), lambda i, j: (j, 0)),
                  pl.BlockSpec((1, d), lambda i, j: (0, 0))],
        out_specs=pl.BlockSpec((tm, d), lambda i, j: (i, 0)),
        out_shape=jax.ShapeDtypeStruct((t, d), F32),
        scratch_shapes=[pltpu.VMEM((tm, d), BF16), pltpu.VMEM((tm, d), F32)],
        compiler_params=_cparams(("parallel", "arbitrary")),
        name="mlp",
    )(x2, g.reshape(1, d), sc, sh, gt, w1, w2, fg.reshape(1, d))


def _split_w_in(w_in):
    offs = [0]
    for sz in IN_SIZES:
        offs.append(offs[-1] + sz)
    return [w_in[:, offs[i]:offs[i + 1]] for i in range(len(IN_SIZES))]


def _layout_w_in(w_in):
    nq, nkc, nvc, nks, nvs, nkw, nvw, ngate, pin, dqkv, dz, dbeta, da, mg = _split_w_in(w_in)
    d = w_in.shape[0]
    per_g = 3 * NSA_REP
    zeros = lambda n: jnp.zeros((d, n), w_in.dtype)
    gate0 = jnp.concatenate([ngate[:, :per_g], zeros(LANE_BETA - per_g), dbeta, da,
                             zeros(LANES - LANE_DECAY - DN_HEADS)], axis=1)
    gate1 = jnp.concatenate([ngate[:, per_g:], zeros(LANES - per_g)], axis=1)
    main = jnp.concatenate([nq, pin, dz, dqkv, mg, gate0, gate1, nkc, nvc], axis=1)

    def per_group(w):
        pad = zeros(LANES - NSA_DH)
        return jnp.concatenate([w[:, :NSA_DH], pad, w[:, NSA_DH:], pad], axis=1)

    kv = jnp.concatenate([per_group(nks), per_group(nvs), per_group(nkw), per_group(nvw)], axis=1)
    return main.astype(BF16), kv.astype(BF16)


def _group_lane_w1(w1):
    two, _, hid = w1.shape
    halves = w1.reshape(two, 2, CMP_STRIDE, NSA_DH, hid)
    out = []
    for g in range(NSA_GROUPS):
        padded = jnp.zeros((two, 2, CMP_STRIDE, LANES, hid), w1.dtype).at[:, :, :, g * NSA_DH:(g + 1) * NSA_DH].set(halves)
        out.append(padded.reshape(two, 2, CMP_STRIDE * LANES, hid))
    return jnp.stack(out, axis=1)


def _nsa_constants(seq):
    nb = seq // SEL_LEN
    ncp = seq // CMP_STRIDE
    j = jnp.arange(nb)[:, None] * SEL_LEN
    i = jnp.arange(ncp)[None, :] * CMP_STRIDE
    overlap = jnp.clip(jnp.minimum(j + SEL_LEN, i + CMP_LEN) - jnp.maximum(j, i), 0) // CMP_STRIDE
    block_onehot = (jnp.arange(seq)[:, None] // SEL_LEN) == jnp.arange(nb)[None, :]
    return overlap.astype(BF16), block_onehot.astype(BF16)


def kernel(x, c, ada_w, ada_b, norm1_g, norm2_g, w_in, phi_k1, phi_k2, phi_v1, phi_v2, pos_k, pos_v, pool_w, pool_scale, dn_conv_w, dn_A_log, dn_dt_bias, dn_norm_g, w_branch_nsa, w_branch_pool, w_branch_dn, w_out, mlp_w1, mlp_w2, final_g):
    bsz, seq, d = x.shape
    assert d == D_MODEL and seq % max(PROJ_TM, MLP_TM, POOL_TS) == 0 and seq % KV_CHUNK == 0
    assert DN_CHUNK == DN_DK == LANES
    t = bsz * seq
    depth = ada_w.shape[0]
    chunk = KV_CHUNK
    nqt = seq // Q_TILE
    nb = seq // SEL_LEN
    blocks_per_chunk = chunk // SEL_LEN

    slopes = 2.0 ** (-(8.0 / NSA_HEADS) * (jnp.arange(NSA_HEADS, dtype=F32) + 1.0))
    overlap, block_onehot = _nsa_constants(seq)
    c_pad = jnp.zeros((SUBLANES, d), F32).at[:bsz].set(c)
    x2 = x.reshape(t, d)

    for l in range(depth):
        mod = _mod_call(c_pad, ada_w[l], ada_b[l])[:bsz]
        sh1, sc1, gt1, sh2, sc2, gt2 = [m.reshape(bsz, 1, d) for m in jnp.split(mod, 6, axis=-1)]

        w_main, w_kv = _layout_w_in(w_in[l])
        main = _proj_call(x2, norm1_g[l], sc1, sh1, w_main, seq)
        kv = _proj_kv_call(x2, norm1_g[l], sc1, sh1, w_kv, seq)

        main3 = main.reshape(bsz, seq, N_MAIN)
        pos = jnp.zeros((2, SUBLANES, CMP_LEN * NSA_DH), F32).at[:, 0].set(
            jnp.stack([pos_k[l].reshape(-1), pos_v[l].reshape(-1)])).astype(BF16)
        w1 = jnp.stack([phi_k1[l], phi_v1[l]]).astype(BF16)
        w2 = jnp.stack([phi_k2[l], phi_v2[l]])
        w2 = jnp.concatenate([w2, w2], axis=-1).astype(BF16)
        cmp_kv = _compress_call(main3, pos, w1, _group_lane_w1(w1), w2)

        sel, act, yc = _select_call(slopes, main, cmp_kv[0], cmp_kv[1], overlap, bsz, seq)
        lst, cnt = _chunk_lists(act, seq, chunk)
        y_a = _attn_call(lst, cnt, slopes, main, sel, kv, yc, block_onehot, bsz, seq, chunk)

        y_b = _pool_call(main, pool_w[l].astype(BF16), pool_scale[l], bsz, seq)

        conv_w = jnp.zeros((SUBLANES, 3 * DN_W), F32).at[:DN_CONV].set(dn_conv_w[l])
        lane_row = lambda v: jnp.zeros((1, LANES), F32).at[0, LANE_DECAY:LANE_DECAY + DN_HEADS].set(v)
        y_c = _dn_call(main3, conv_w, lane_row(dn_A_log[l]), lane_row(dn_dt_bias[l]),
                       dn_norm_g[l]).reshape(t, DN_W)

        x2 = _merge_call(y_a, y_b, y_c, main, x2, gt1,
                         w_branch_nsa[l].astype(BF16), w_branch_pool[l].astype(BF16),
                         w_branch_dn[l].astype(BF16), w_out[l].astype(BF16), seq)
        x2 = _mlp_call(x2, norm2_g[l], sc2, sh2, gt2, mlp_w1[l].astype(BF16), mlp_w2[l].astype(BF16),
                       final_g, seq, final=(l == depth - 1))
    return x2.reshape(bsz, seq, d)
```

```python
import functools

import jax
import jax.numpy as jnp
from jax import lax
from jax.experimental import pallas as pl
from jax.experimental.pallas import tpu as pltpu

F32 = jnp.float32
BF16 = jnp.bfloat16
HIGHEST = lax.Precision.HIGHEST

D_MODEL = 1024
NSA_HEADS = 8
NSA_GROUPS = 2
NSA_REP = NSA_HEADS // NSA_GROUPS
NSA_DH = 64
CMP_LEN = 32
CMP_STRIDE = 16
SEL_LEN = 64
SEL_SHIFT = 6
TOP_N = 16
WINDOW = 512
FORCE_BONUS = 1.0e4
POOL_WINDOWS = (2, 4, 8, 16)
POOL_GROUP_DIM = 128
POOL_W = len(POOL_WINDOWS) * POOL_GROUP_DIM
DN_HEADS = 4
DN_DK = 128
DN_W = DN_HEADS * DN_DK
DN_CONV = 4
EPS = 1e-6
NEG = -1e30
NSA_Q_W = NSA_HEADS * NSA_DH
NSA_KV_W = NSA_GROUPS * NSA_DH
IN_SIZES = (NSA_Q_W, NSA_KV_W, NSA_KV_W, NSA_KV_W, NSA_KV_W, NSA_KV_W, NSA_KV_W,
            3 * NSA_HEADS, POOL_W, 3 * DN_W, DN_W, DN_HEADS, DN_HEADS, 3 * D_MODEL)

LANES = 128
LANE_SHIFT = 7
SUBLANES = 8
VMEM_LIMIT_BYTES = 48 * 1024 * 1024

Q_TILE = 128
SELECT_BUCKETS = 4
ATTN_Q_TILE = 128
ATTN_TILES = 2
KV_CHUNK = 256
DN_CHUNK = 128
DN_BATCH = 4
PROJ_TM = 1024
MLP_TM = 1024
MLP_TF = 1024
MERGE_TM = 512
POOL_TS = 1024
POOL_HALO = 16
DN_HALO = 8

COL_Q = 0
COL_POOL = 512
COL_Z = 1024
COL_QKV = 1536
COL_MG = 3072
COL_GATE = 6144
COL_KC = 6400
COL_VC = 6528
N_MAIN = 6656
PROJ_TN = 1664
LANE_BETA = 16
LANE_DECAY = 20
KVB_KS = 0
KVB_VS = 2
KVB_KW = 4
KVB_VW = 6
KVB_END = 8
N_KV = KVB_END * LANES
LANE_POS_HI = NSA_DH
LANE_POS_LO = NSA_DH + 1


def _cparams(sem):
    return pltpu.CompilerParams(dimension_semantics=sem, vmem_limit_bytes=VMEM_LIMIT_BYTES)


def _silu(v):
    return v * jax.nn.sigmoid(v)


def _mm(a, b):
    return jnp.dot(a.astype(BF16), b.astype(BF16), preferred_element_type=F32)


def _mm_nt(a, b):
    return lax.dot_general(a.astype(BF16), b.astype(BF16), (((1,), (1,)), ((), ())), preferred_element_type=F32)


def _mm_tn(a, b):
    return lax.dot_general(a.astype(BF16), b.astype(BF16), (((0,), (0,)), ((), ())), preferred_element_type=F32)


def _mod_kernel(c_ref, w_ref, b_ref, o_ref):
    cond = _silu(c_ref[...])
    o_ref[...] = jnp.dot(cond, w_ref[...], preferred_element_type=F32, precision=HIGHEST) + b_ref[...]


def _mod_call(c_pad, w, b):
    d, n = w.shape
    tn = 1536
    return pl.pallas_call(
        _mod_kernel,
        grid=(n // tn,),
        in_specs=[pl.BlockSpec((SUBLANES, d), lambda j: (0, 0)),
                  pl.BlockSpec((d, tn), lambda j: (0, j)),
                  pl.BlockSpec((1, tn), lambda j: (0, j))],
        out_specs=pl.BlockSpec((SUBLANES, tn), lambda j: (0, j)),
        out_shape=jax.ShapeDtypeStruct((SUBLANES, n), F32),
        compiler_params=_cparams(("parallel",)),
        name="mod",
    )(c_pad, w, b.reshape(1, n))


def _norm_mod(x, g, sc, sh):
    ms = jnp.mean(x * x, axis=-1, keepdims=True)
    return (x * lax.rsqrt(ms + EPS) * g) * (1.0 + sc) + sh


def _proj_kernel(x_ref, g_ref, sc_ref, sh_ref, w_ref, o_ref, h_scr):
    @pl.when(pl.program_id(1) == 0)
    def _():
        h_scr[...] = _norm_mod(x_ref[...], g_ref[...], sc_ref[0], sh_ref[0]).astype(BF16)

    o_ref[...] = jnp.dot(h_scr[...], w_ref[...], preferred_element_type=F32)


def _proj_kv_kernel(x_ref, g_ref, sc_ref, sh_ref, w_ref, o_ref, *, tiles_per_seq):
    h = _norm_mod(x_ref[...], g_ref[...], sc_ref[0], sh_ref[0]).astype(BF16)
    acc = jnp.dot(h, w_ref[...], preferred_element_type=F32)
    tm, n = acc.shape
    col = lax.broadcasted_iota(jnp.int32, (1, n), 1)
    blk = jnp.right_shift(col, LANE_SHIFT)
    lane = col & (LANES - 1)
    is_key = ((blk >= KVB_KS) & (blk < KVB_VS)) | ((blk >= KVB_KW) & (blk < KVB_VW))
    is_val = ((blk >= KVB_VS) & (blk < KVB_KW)) | ((blk >= KVB_VW) & (blk < KVB_END))
    ones_row = jnp.where(is_val & (lane >= NSA_DH), 1.0, 0.0)
    hi_row = jnp.where(is_key & (lane == LANE_POS_HI), 1.0, 0.0)
    lo_row = jnp.where(is_key & (lane == LANE_POS_LO), 1.0, 0.0)
    pos = (pl.program_id(0) % tiles_per_seq) * tm + lax.broadcasted_iota(jnp.int32, (tm, 1), 0)
    pos_hi = jnp.right_shift(pos, LANE_SHIFT).astype(F32)
    pos_lo = (pos & (LANES - 1)).astype(F32)
    o_ref[...] = (acc + ones_row + pos_hi * hi_row + pos_lo * lo_row).astype(o_ref.dtype)


def _proj_specs(tm, d, tn, per_b):
    return [pl.BlockSpec((tm, d), lambda i, j: (i, 0)),
            pl.BlockSpec((1, d), lambda i, j: (0, 0)),
            pl.BlockSpec((1, 1, d), lambda i, j: (i // per_b, 0, 0)),
            pl.BlockSpec((1, 1, d), lambda i, j: (i // per_b, 0, 0)),
            pl.BlockSpec((d, tn), lambda i, j: (0, j))]


def _proj_call(x2, g, sc, sh, w, seq):
    t, d = x2.shape
    n = w.shape[1]
    tm = min(PROJ_TM, seq)
    tn = PROJ_TN
    return pl.pallas_call(
        _proj_kernel,
        grid=(t // tm, n // tn),
        in_specs=_proj_specs(tm, d, tn, seq // tm),
        out_specs=pl.BlockSpec((tm, tn), lambda i, j: (i, j)),
        out_shape=jax.ShapeDtypeStruct((t, n), F32),
        scratch_shapes=[pltpu.VMEM((tm, d), BF16)],
        compiler_params=_cparams(("parallel", "arbitrary")),
        name="proj_main",
    )(x2, g.reshape(1, d), sc, sh, w)


def _proj_kv_call(x2, g, sc, sh, w, seq):
    t, d = x2.shape
    n = w.shape[1]
    tm = min(PROJ_TM, seq)
    return pl.pallas_call(
        functools.partial(_proj_kv_kernel, tiles_per_seq=seq // tm),
        grid=(t // tm, 1),
        in_specs=_proj_specs(tm, d, n, seq // tm),
        out_specs=pl.BlockSpec((tm, n), lambda i, j: (i, j)),
        out_shape=jax.ShapeDtypeStruct((t, n), BF16),
        compiler_params=_cparams(("parallel", "arbitrary")),
        name="proj_kv",
    )(x2, g.reshape(1, d), sc, sh, w)


def _compress_kernel(x_ref, pos_ref, w1_ref, w1g_ref, w2_ref, o_ref):
    nr = x_ref.shape[1] // CMP_STRIDE
    xs = jnp.concatenate([x_ref[0, pl.ds(rho, nr, stride=CMP_STRIDE), :].astype(BF16) for rho in range(CMP_STRIDE)],
                         axis=1)
    posb = jnp.dot(pos_ref[0], w1_ref[0], preferred_element_type=F32)[0:1]
    for g in range(NSA_GROUPS):
        top = jnp.dot(xs, w1g_ref[0, g, 0], preferred_element_type=F32)
        bot = jnp.dot(xs, w1g_ref[0, g, 1], preferred_element_type=F32)
        hid = top + pltpu.roll(bot, nr - 1, 0) + posb
        o_ref[0, 0, g] = _mm(_silu(hid), w2_ref[0]).astype(o_ref.dtype)


def _compress_call(main3, pos, w1, w1g, w2):
    b, seq, _ = main3.shape
    nr = seq // CMP_STRIDE
    two, l64, hid = w1.shape
    return pl.pallas_call(
        _compress_kernel,
        grid=(two, b),
        in_specs=[pl.BlockSpec((1, seq, LANES), lambda a, i: (i, 0, COL_KC // LANES + a)),
                  pl.BlockSpec((1, SUBLANES, l64), lambda a, i: (a, 0, 0)),
                  pl.BlockSpec((1, l64, hid), lambda a, i: (a, 0, 0)),
                  pl.BlockSpec((1,) + w1g.shape[1:], lambda a, i: (a, 0, 0, 0, 0)),
                  pl.BlockSpec((1, hid, LANES), lambda a, i: (a, 0, 0))],
        out_specs=pl.BlockSpec((1, 1, NSA_GROUPS, nr, LANES), lambda a, i: (a, i, 0, 0, 0)),
        out_shape=jax.ShapeDtypeStruct((two, b, NSA_GROUPS, nr, LANES), BF16),
        compiler_params=_cparams(("parallel", "parallel")),
        name="compress",
    )(main3, pos, w1, w1g, w2)


def _gate_cols(gate_logits, branch):
    sg = jax.nn.sigmoid(gate_logits)
    return [sg[:, 3 * r + branch:3 * r + branch + 1] for r in range(NSA_REP)]


def _stack_heads_dup(q):
    qa = q[:, :LANES]
    qb = q[:, LANES:]
    lo = lax.broadcasted_iota(jnp.int32, qa.shape, 1) < NSA_DH
    z = jnp.zeros_like(qa)
    return jnp.concatenate([jnp.where(lo, qa, z), jnp.where(lo, z, qa),
                            jnp.where(lo, qb, z), jnp.where(lo, z, qb)], axis=0)


def _unstack_heads_dup(o, gates):
    nq = o.shape[0] // NSA_REP
    lo = lax.broadcasted_iota(jnp.int32, (nq, LANES), 1) < NSA_DH
    a = jnp.where(lo, gates[0] * o[0:nq], gates[1] * o[nq:2 * nq])
    b = jnp.where(lo, gates[2] * o[2 * nq:3 * nq], gates[3] * o[3 * nq:])
    return jnp.concatenate([a, b], axis=1)


def _stack_heads_pos(q, slopes):
    qa = q[:, :LANES]
    qb = q[:, LANES:]
    lane = lax.broadcasted_iota(jnp.int32, qa.shape, 1)
    lo = lane < NSA_DH
    heads = [qa, pltpu.roll(qa, NSA_DH, 1), qb, pltpu.roll(qb, NSA_DH, 1)]
    out = []
    for r in range(NSA_REP):
        extra = jnp.where(lane == LANE_POS_HI, slopes[r] * LANES, jnp.where(lane == LANE_POS_LO, slopes[r], 0.0))
        out.append(jnp.where(lo, heads[r], extra))
    return jnp.concatenate(out, axis=0)


def _finish_heads(acc, gates):
    nq = acc.shape[0] // NSA_REP
    o = acc * pltpu.roll(1.0 / acc, NSA_DH, 1)
    lo = lax.broadcasted_iota(jnp.int32, (nq, LANES), 1) < NSA_DH
    hs = [gates[r] * o[r * nq:(r + 1) * nq] for r in range(NSA_REP)]
    a = jnp.where(lo, hs[0], pltpu.roll(hs[1], NSA_DH, 1))
    b = jnp.where(lo, hs[2], pltpu.roll(hs[3], NSA_DH, 1))
    return jnp.concatenate([a, b], axis=1)


def _select_kernel(slopes_ref, q_ref, kc_ref, vc_ref, gates_ref, ov_ref, sel_ref, act_ref, yc_ref):
    qt = pl.program_id(1)
    nqt = pl.num_programs(1)
    nb_all, ncp_all = ov_ref.shape
    for k in range(SELECT_BUCKETS):
        @pl.when((qt * SELECT_BUCKETS) // nqt == k)
        def _(k=k):
            _select_prefix(slopes_ref, q_ref, kc_ref, vc_ref, gates_ref, ov_ref, sel_ref, act_ref, yc_ref,
                           nb=(k + 1) * nb_all // SELECT_BUCKETS, ncp=(k + 1) * ncp_all // SELECT_BUCKETS)


def _select_prefix(slopes_ref, q_ref, kc_ref, vc_ref, gates_ref, ov_ref, sel_ref, act_ref, yc_ref, *, nb, ncp):
    t0 = pl.program_id(1) * Q_TILE
    nq = Q_TILE
    nb_all = ov_ref.shape[0]
    c_end = lax.broadcasted_iota(jnp.int32, (nq, ncp), 1) * CMP_STRIDE + (CMP_LEN - 1)
    tq = t0 + lax.broadcasted_iota(jnp.int32, (nq, ncp), 0)
    vis = c_end <= tq
    rel = (c_end[0:1] - t0).astype(F32)
    ov = ov_ref[:nb, :ncp]
    jb = lax.broadcasted_iota(jnp.int32, (nb, nq), 0)
    tql = t0 + lax.broadcasted_iota(jnp.int32, (nb, nq), 1)
    valid = jb * SEL_LEN <= tql
    forced = (jb == 0) | (jb == jnp.right_shift(tql, SEL_SHIFT))
    scores = []
    for g in range(NSA_GROUPS):
        qcols = slice(g * NSA_REP * NSA_DH, (g + 1) * NSA_REP * NSA_DH)
        qs = _stack_heads_dup(q_ref[:, qcols] * (NSA_DH ** -0.5)).astype(BF16)
        s = _mm_nt(qs, kc_ref[0, g, :ncp, :])
        psum = jnp.zeros((nq, ncp), F32)
        ps = []
        for r in range(NSA_REP):
            m_r = slopes_ref[g * NSA_REP + r]
            sr = jnp.where(vis, s[r * nq:(r + 1) * nq] + m_r * rel, NEG)
            mx = jnp.max(sr, axis=-1, keepdims=True)
            e = jnp.where(vis, jnp.exp(sr - mx), 0.0)
            den = jnp.sum(e, axis=-1, keepdims=True)
            p = e * (1.0 / jnp.where(den > 0.0, den, 1.0))
            psum = psum + p
            ps.append(p.astype(BF16))
        o_c = jnp.dot(jnp.concatenate(ps, axis=0), vc_ref[0, g, :ncp, :], preferred_element_type=F32)
        yc_ref[:, qcols] = _unstack_heads_dup(o_c, _gate_cols(gates_ref[:, g * LANES:(g + 1) * LANES], 0))
        p_hi = psum.astype(BF16)
        p_lo = (psum - p_hi.astype(F32)).astype(BF16)
        imp_t = _mm_nt(ov, p_hi) + _mm_nt(ov, p_lo)
        scores.append(jnp.where(valid, imp_t + FORCE_BONUS * forced.astype(F32), NEG))

    sels = [jnp.zeros((nb, nq), jnp.bool_) for _ in range(NSA_GROUPS)]
    for _ in range(min(TOP_N, nb)):
        for g in range(NSA_GROUPS):
            mx = jnp.max(scores[g], axis=0, keepdims=True)
            idx = jnp.min(jnp.where(scores[g] == mx, jb, nb), axis=0, keepdims=True)
            hit = jb == idx
            sels[g] = sels[g] | hit
            scores[g] = jnp.where(hit, -jnp.inf, scores[g])
    for g in range(NSA_GROUPS):
        sel_t = jnp.where(sels[g] & valid, 1.0, 0.0)
        if nb < nb_all:
            sel_t = jnp.concatenate([sel_t, jnp.zeros((nb_all - nb, nq), F32)], axis=0)
        sel_q = sel_t.T
        sel_ref[0, g] = sel_q.astype(sel_ref.dtype)
        act_ref[0, g, 0] = jnp.max(sel_q, axis=0, keepdims=True)


def _select_call(slopes, main, kcmp, vcmp, ov, bsz, seq):
    nqt = seq // Q_TILE
    nb, ncp = ov.shape
    row = lambda b, q: b * nqt + q
    gw = NSA_GROUPS * LANES
    return pl.pallas_call(
        _select_kernel,
        grid=(bsz, nqt),
        in_specs=[pl.BlockSpec(memory_space=pltpu.SMEM),
                  pl.BlockSpec((Q_TILE, NSA_Q_W), lambda b, q: (row(b, q), COL_Q // NSA_Q_W)),
                  pl.BlockSpec((1, NSA_GROUPS, ncp, LANES), lambda b, q: (b, 0, 0, 0)),
                  pl.BlockSpec((1, NSA_GROUPS, ncp, LANES), lambda b, q: (b, 0, 0, 0)),
                  pl.BlockSpec((Q_TILE, gw), lambda b, q: (row(b, q), COL_GATE // gw)),
                  pl.BlockSpec((nb, ncp), lambda b, q: (0, 0))],
        out_specs=[pl.BlockSpec((1, NSA_GROUPS, Q_TILE, nb), lambda b, q: (b, 0, q, 0)),
                   pl.BlockSpec((1, NSA_GROUPS, 1, 1, nb), lambda b, q: (b, 0, q, 0, 0)),
                   pl.BlockSpec((Q_TILE, NSA_Q_W), lambda b, q: (row(b, q), 0))],
        out_shape=[jax.ShapeDtypeStruct((bsz, NSA_GROUPS, seq, nb), BF16),
                   jax.ShapeDtypeStruct((bsz, NSA_GROUPS, nqt, 1, nb), F32),
                   jax.ShapeDtypeStruct((bsz * seq, NSA_Q_W), F32)],
        compiler_params=_cparams(("parallel", "parallel")),
        name="select",
    )(slopes, main, kcmp, vcmp, main, ov)


def _attn_kernel(lst_ref, cnt_ref, slopes_ref, q_ref, sel_ref, ks_ref, vs_ref, kw_ref, vw_ref, gates_ref, yc_ref,
                 et_ref, o_ref, s_scr, p_scr, m_scr, a_scr, acc_scr, *, chunk):
    b = pl.program_id(0)
    g = pl.program_id(1)
    step = pl.program_id(2)
    nqt = pl.num_programs(2) * ATTN_TILES
    nq = ATTN_Q_TILE
    tiles = range(ATTN_TILES)
    nch_total = ks_ref.shape[0] // chunk
    slopes = [slopes_ref[g * NSA_REP + r] for r in range(NSA_REP)]
    qts = [step * ATTN_TILES + u for u in tiles]
    t0s = [qt * nq for qt in qts]
    rws = [slice(u * nq, (u + 1) * nq) for u in tiles]
    qs = [_stack_heads_pos(q_ref[rws[u], :] * (NSA_DH ** -0.5), slopes).astype(BF16) for u in tiles]
    pens = [((sel_ref[0, 0, rws[u], :].astype(F32) - 1.0) * (-NEG)).astype(BF16) for u in tiles]
    qx = [jnp.concatenate([qs[u], jnp.concatenate([pens[u]] * NSA_REP, axis=0)], axis=1) for u in tiles]

    bases = [((b * NSA_GROUPS + g) * nqt + qts[u]) * nch_total for u in tiles]
    cnts = [cnt_ref[(b * NSA_GROUPS + g) * nqt + qts[u]] for u in tiles]

    def logits(u, c):
        start = pl.multiple_of(c * chunk, chunk)
        kx = jnp.concatenate([ks_ref[pl.ds(start, chunk), :], et_ref[pl.ds(start, chunk), :]], axis=1)
        return _mm_nt(qx[u], kx)

    def values(p, c):
        start = pl.multiple_of(c * chunk, chunk)
        return jnp.dot(p, vs_ref[pl.ds(start, chunk), :], preferred_element_type=F32)

    def row_max(s):
        return jnp.broadcast_to(jnp.max(s, axis=-1, keepdims=True), (s.shape[0], LANES))

    def lane_tile(m, width):
        return jnp.concatenate([m] * (width // LANES), axis=1)

    for u in tiles:
        m_scr[u] = jnp.full(m_scr.shape[1:], NEG, F32)
        a_scr[u] = jnp.ones(a_scr.shape[1:], F32)
        acc_scr[u] = jnp.zeros(acc_scr.shape[1:], F32)
        p_scr[u, 1] = jnp.zeros(p_scr.shape[2:], BF16)
    for u in tiles:
        s_scr[u, 0] = logits(u, lst_ref[bases[u]])

    def flash_steps(us):
        def body(i, carry):
            slot = i & 1
            for u in us:
                acc_scr[u] = a_scr[u] * acc_scr[u] + values(p_scr[u, 1 - slot],
                                                            lst_ref[bases[u] + jnp.maximum(i - 1, 0)])
            for u in us:
                s = s_scr[u, slot]
                m_prev = m_scr[u]
                m_new = jnp.maximum(m_prev, row_max(s))
                a_scr[u] = jnp.exp(m_prev - m_new)
                p_scr[u, slot] = jnp.exp(s - lane_tile(m_new, chunk)).astype(BF16)
                m_scr[u] = m_new
            for u in us:
                s_scr[u, 1 - slot] = logits(u, lst_ref[bases[u] + i + 1])
            return carry
        return body

    joint = functools.reduce(jnp.minimum, cnts)
    lax.fori_loop(0, joint, flash_steps(list(tiles)), 0)
    for u in tiles:
        lax.fori_loop(joint, cnts[u], flash_steps([u]), 0)

    slots = [cnts[u] & 1 for u in tiles]
    c_diag = [t0s[u] // chunk for u in tiles]
    acc = [a_scr[u] * acc_scr[u] + values(p_scr[u, 1 - slots[u]], lst_ref[bases[u] + jnp.maximum(cnts[u] - 1, 0)])
           for u in tiles]
    nw = WINDOW + nq
    wbase = [pl.multiple_of(jnp.maximum(t0s[u] - WINDOW, 0), nq) for u in tiles]
    s_win = [_mm_nt(qs[u], kw_ref[pl.ds(wbase[u], nw), :]) for u in tiles]

    def masked(s, ok):
        return jnp.concatenate([jnp.where(ok, s[r * nq:(r + 1) * nq], NEG) for r in range(NSA_REP)], axis=0)

    s_diag = []
    for u in tiles:
        pos = c_diag[u] * chunk + lax.broadcasted_iota(jnp.int32, (nq, chunk), 1)
        tq = t0s[u] + lax.broadcasted_iota(jnp.int32, (nq, chunk), 0)
        s_diag.append(masked(s_scr[u, slots[u]], pos <= tq))
    m_prev = [m_scr[u] for u in tiles]
    m_new = [jnp.maximum(m_prev[u], row_max(s_diag[u])) for u in tiles]
    acc = [jnp.exp(m_prev[u] - m_new[u]) * acc[u]
           + values(jnp.exp(s_diag[u] - lane_tile(m_new[u], chunk)).astype(BF16), c_diag[u]) for u in tiles]
    p_win = []
    for u in tiles:
        pos = wbase[u] + lax.broadcasted_iota(jnp.int32, (nq, nw), 1)
        tq = t0s[u] + lax.broadcasted_iota(jnp.int32, (nq, nw), 0)
        s = masked(s_win[u], (pos <= tq) & (pos > tq - WINDOW))
        p_win.append(jnp.exp(s - jnp.max(s, axis=-1, keepdims=True)).astype(BF16))
    acc_w = [jnp.dot(p_win[u], vw_ref[pl.ds(wbase[u], nw), :], preferred_element_type=F32) for u in tiles]
    for u in tiles:
        gates = gates_ref[rws[u], :]
        o_ref[rws[u], :] = (yc_ref[rws[u], :] + _finish_heads(acc[u], _gate_cols(gates, 1))
                            + _finish_heads(acc_w[u], _gate_cols(gates, 2)))


def _chunk_lists(act, seq, chunk):
    bsz, ngrp = act.shape[:2]
    nqt = seq // ATTN_Q_TILE
    nch = seq // chunk
    active = act.reshape(bsz, ngrp, nqt, ATTN_Q_TILE // Q_TILE, nch, chunk // SEL_LEN).max(axis=(3, 5)) > 0.0
    c_diag = (jnp.arange(nqt) * ATTN_Q_TILE) // chunk
    active = active & (jnp.arange(nch)[None, :] < c_diag[:, None])
    cnt = active.sum(axis=-1).astype(jnp.int32)
    order = jnp.argsort(jnp.logical_not(active), axis=-1, stable=True).astype(jnp.int32)
    lst = jnp.where(jnp.arange(nch) < cnt[..., None], order, c_diag[:, None].astype(jnp.int32))
    return lst.reshape(-1), cnt.reshape(-1)


def _attn_call(lst, cnt, slopes, main, sel, kv, yc, et, bsz, seq, chunk):
    tq = ATTN_Q_TILE * ATTN_TILES
    nqt = seq // tq
    nb = sel.shape[-1]
    rows = NSA_REP * ATTN_Q_TILE
    row = lambda b, g, q: b * nqt + q
    kvspec = lambda blk: pl.BlockSpec((seq, LANES), lambda b, g, q, *_: (b, blk + g))
    grid_spec = pltpu.PrefetchScalarGridSpec(
        num_scalar_prefetch=2,
        grid=(bsz, NSA_GROUPS, nqt),
        in_specs=[pl.BlockSpec(memory_space=pltpu.SMEM),
                  pl.BlockSpec((tq, 256), lambda b, g, q, *_: (row(b, g, q), COL_Q // 256 + g)),
                  pl.BlockSpec((1, 1, tq, nb), lambda b, g, q, *_: (b, g, q, 0)),
                  kvspec(KVB_KS), kvspec(KVB_VS), kvspec(KVB_KW), kvspec(KVB_VW),
                  pl.BlockSpec((tq, LANES), lambda b, g, q, *_: (row(b, g, q), COL_GATE // LANES + g)),
                  pl.BlockSpec((tq, 256), lambda b, g, q, *_: (row(b, g, q), g)),
                  pl.BlockSpec(et.shape, lambda b, g, q, *_: (0, 0))],
        out_specs=pl.BlockSpec((tq, 256), lambda b, g, q, *_: (row(b, g, q), g)),
        scratch_shapes=[pltpu.VMEM((ATTN_TILES, 2, rows, chunk), F32),
                        pltpu.VMEM((ATTN_TILES, 2, rows, chunk), BF16),
                        pltpu.VMEM((ATTN_TILES, rows, LANES), F32),
                        pltpu.VMEM((ATTN_TILES, rows, LANES), F32),
                        pltpu.VMEM((ATTN_TILES, rows, LANES), F32)],
    )
    return pl.pallas_call(
        functools.partial(_attn_kernel, chunk=chunk),
        grid_spec=grid_spec,
        out_shape=jax.ShapeDtypeStruct((bsz * seq, NSA_Q_W), F32),
        compiler_params=_cparams(("parallel", "parallel", "parallel")),
        name="attn",
    )(lst, cnt, slopes, main, sel, kv, kv, kv, kv, main, yc, et)


def _pool_kernel(x_ref, halo_ref, w_ref, sc_ref, o_ref):
    s = pl.program_id(1)
    ts = x_ref.shape[0]
    x = x_ref[...]
    halo = jnp.where(s > 0, halo_ref[...], 0.0)
    t = s * ts + lax.broadcasted_iota(jnp.int32, (ts, 1), 0)
    for gi, win in enumerate(POOL_WINDOWS):
        lanes = slice(gi * POOL_GROUP_DIM, (gi + 1) * POOL_GROUP_DIM)
        ext = jnp.concatenate([halo[:, lanes], x[:, lanes]], axis=0)
        acc = ext
        span = 1
        while span < win:
            acc = acc + pltpu.roll(acc, span, 0)
            span *= 2
        cnt = jnp.minimum(t + 1, win).astype(F32)
        y = acc[POOL_HALO:] / cnt - x[:, lanes]
        o_ref[:, lanes] = _mm(y, w_ref[gi]) * sc_ref[:, lanes]


def _pool_call(main, w, scale, bsz, seq):
    ts = min(POOL_TS, seq)
    per_b = seq // ts
    halo_per_tile = ts // POOL_HALO
    return pl.pallas_call(
        _pool_kernel,
        grid=(bsz, per_b),
        in_specs=[pl.BlockSpec((ts, POOL_W), lambda b, s: (b * per_b + s, COL_POOL // POOL_W)),
                  pl.BlockSpec((POOL_HALO, POOL_W),
                               lambda b, s: (jnp.maximum((b * per_b + s) * halo_per_tile - 1, 0), COL_POOL // POOL_W)),
                  pl.BlockSpec(w.shape, lambda b, s: (0, 0, 0)),
                  pl.BlockSpec((1, POOL_W), lambda b, s: (0, 0))],
        out_specs=pl.BlockSpec((ts, POOL_W), lambda b, s: (b * per_b + s, 0)),
        out_shape=jax.ShapeDtypeStruct((bsz * seq, POOL_W), F32),
        compiler_params=_cparams(("parallel", "parallel")),
        name="pool",
    )(main, main, w, scale.reshape(1, POOL_W))


def _unit_lower_inverses(nms, ri, ci):
    c = nms[0].shape[0]
    eye = jnp.where(ri == ci, 1.0, 0.0)
    in8 = jnp.right_shift(ri, 3) == jnp.right_shift(ci, 3)
    n8 = [jnp.where(in8, nm, 0.0) for nm in nms]
    ps = [eye - a for a in n8]
    ms = [_mm(a, a) for a in n8]
    ps = [p + _mm(p, m) for p, m in zip(ps, ms)]
    ms = [_mm(m, m) for m in ms]
    ps = [p + _mm(p, m) for p, m in zip(ps, ms)]
    shift = 3
    while (1 << shift) < c:
        rb = jnp.right_shift(ri, shift)
        cb = jnp.right_shift(ci, shift)
        lower_left = ((rb & 1) == 1) & (cb == rb - 1)
        ts = [_mm(p, jnp.where(lower_left, nm, 0.0)) for p, nm in zip(ps, nms)]
        ps = [p - _mm(t, p) for p, t in zip(ps, ts)]
        shift += 1
    return ps


def _row_cumsum(x, ri):
    span = 1
    while span < x.shape[0]:
        x = x + jnp.where(ri >= span, pltpu.roll(x, span, 0), 0.0)
        span *= 2
    return x


def _dn_kernel(qkv_ref, halo_ref, z_ref, gba_ref, cw_ref, alog_ref, dtb_ref, ng_ref, o_ref, st_scr):
    s = pl.program_id(1)
    nbatch, c = qkv_ref.shape[0], qkv_ref.shape[1]

    @pl.when(s == 0)
    def _():
        st_scr[...] = jnp.zeros(st_scr.shape, F32)

    ri = lax.broadcasted_iota(jnp.int32, (c, c), 0)
    ci = lax.broadcasted_iota(jnp.int32, (c, c), 1)
    tril = ri >= ci
    strict = ri > ci
    cw = cw_ref[...]
    ng = ng_ref[...]
    chains = [(bi, h) for bi in range(nbatch) for h in range(DN_HEADS)]
    qn, kn, kb, vb, gcb, lmask = [], [], [], [], [], []
    for bi in range(nbatch):
        halo = jnp.where(s > 0, halo_ref[bi], 0.0)
        ext = jnp.concatenate([halo, qkv_ref[bi]], axis=0)
        conv = cw[DN_CONV - 1:DN_CONV] * ext
        for tap in range(1, DN_CONV):
            conv = conv + cw[DN_CONV - 1 - tap:DN_CONV - tap] * pltpu.roll(ext, tap, 0)
        act = _silu(conv[DN_HALO:])

        gba = gba_ref[bi]
        beta_all = jax.nn.sigmoid(gba)
        xs = gba + dtb_ref[...]
        softplus = jnp.maximum(xs, 0.0) + jnp.log(1.0 + jnp.exp(-jnp.abs(xs)))
        gc_all = _row_cumsum(-jnp.exp(alog_ref[...]) * softplus, ri)
        gct_all = gc_all.T
        for h in range(DN_HEADS):
            q = act[:, h * DN_DK:(h + 1) * DN_DK]
            k = act[:, DN_W + h * DN_DK:DN_W + (h + 1) * DN_DK]
            v = act[:, 2 * DN_W + h * DN_DK:2 * DN_W + (h + 1) * DN_DK]
            qn.append(q * lax.rsqrt(jnp.sum(q * q, axis=-1, keepdims=True) + EPS) * (DN_DK ** -0.5))
            kn.append(k * lax.rsqrt(jnp.sum(k * k, axis=-1, keepdims=True) + EPS))
            beta = beta_all[:, LANE_BETA + h:LANE_BETA + h + 1]
            gcb.append(jnp.broadcast_to(gc_all[:, LANE_DECAY + h:LANE_DECAY + h + 1], (c, DN_DK)))
            gct = jnp.broadcast_to(gct_all[LANE_DECAY + h:LANE_DECAY + h + 1, :], (c, c))
            lmask.append(jnp.where(tril, jnp.exp(jnp.where(tril, gcb[-1] - gct, 0.0)), 0.0))
            kb.append(kn[-1] * beta)
            vb.append(v * beta)

    n = range(len(chains))
    kk = [_mm_nt(kb[i], kn[i]) for i in n]
    tinv = _unit_lower_inverses([jnp.where(strict, kk[i] * lmask[i], 0.0) for i in n], ri, ci)
    egc = [jnp.exp(gcb[i]) for i in n]
    sol = [_mm(tinv[i], jnp.concatenate([vb[i], kb[i] * egc[i]], axis=1)) for i in n]
    aqk = [_mm_nt(qn[i], kn[i]) * lmask[i] for i in n]
    state = [st_scr[i] for i in n]
    glast = [gcb[i][c - 1:c, :] for i in n]
    v_new = [sol[i][:, :DN_DK] - _mm(sol[i][:, DN_DK:], state[i]) for i in n]
    o_state = [_mm(qn[i] * egc[i], state[i]) for i in n]
    o = [o_state[i] + _mm(aqk[i], v_new[i]) for i in n]
    upd = [_mm_tn(kn[i] * jnp.exp(glast[i] - gcb[i]), v_new[i]) for i in n]
    for i, (bi, h) in enumerate(chains):
        st_scr[i] = state[i] * jnp.exp(glast[i]) + upd[i]
        on = o[i] * lax.rsqrt(jnp.mean(o[i] * o[i], axis=-1, keepdims=True) + EPS) * ng
        o_ref[bi, :, h * DN_DK:(h + 1) * DN_DK] = on * _silu(z_ref[bi, :, h * DN_DK:(h + 1) * DN_DK])


def _dn_call(main3, conv_w, alog_row, dtb_row, norm_g):
    bsz, seq, _ = main3.shape
    c = DN_CHUNK
    nbt = DN_BATCH if bsz % DN_BATCH == 0 else 1
    halo_per_chunk = c // DN_HALO
    w3 = 3 * DN_W
    return pl.pallas_call(
        _dn_kernel,
        grid=(bsz // nbt, seq // c),
        in_specs=[pl.BlockSpec((nbt, c, w3), lambda b, s: (b, s, COL_QKV // w3)),
                  pl.BlockSpec((nbt, DN_HALO, w3),
                               lambda b, s: (b, jnp.maximum(s * halo_per_chunk - 1, 0), COL_QKV // w3)),
                  pl.BlockSpec((nbt, c, DN_W), lambda b, s: (b, s, COL_Z // DN_W)),
                  pl.BlockSpec((nbt, c, LANES), lambda b, s: (b, s, COL_GATE // LANES)),
                  pl.BlockSpec((SUBLANES, w3), lambda b, s: (0, 0)),
                  pl.BlockSpec((1, LANES), lambda b, s: (0, 0)),
                  pl.BlockSpec((1, LANES), lambda b, s: (0, 0)),
                  pl.BlockSpec((1, DN_DK), lambda b, s: (0, 0))],
        out_specs=pl.BlockSpec((nbt, c, DN_W), lambda b, s: (b, s, 0)),
        out_shape=jax.ShapeDtypeStruct((bsz, seq, DN_W), F32),
        scratch_shapes=[pltpu.VMEM((nbt * DN_HEADS, DN_DK, DN_DK), F32)],
        compiler_params=_cparams(("parallel", "arbitrary")),
        name="dn",
    )(main3, main3, main3, main3, conv_w, alog_row, dtb_row, norm_g.reshape(1, DN_DK))


def _merge_kernel(ya_ref, yb_ref, yc_ref, mg_ref, x_ref, gt_ref, wa_ref, wb_ref, wc_ref, wo_ref, o_ref):
    d = x_ref.shape[1]
    gates = jax.nn.sigmoid(mg_ref[...])
    merged = (gates[:, :d] * jnp.dot(ya_ref[...].astype(BF16), wa_ref[...], preferred_element_type=F32)
              + gates[:, d:2 * d] * jnp.dot(yb_ref[...].astype(BF16), wb_ref[...], preferred_element_type=F32)
              + gates[:, 2 * d:] * jnp.dot(yc_ref[...].astype(BF16), wc_ref[...], preferred_element_type=F32))
    y = jnp.dot(merged.astype(BF16), wo_ref[...], preferred_element_type=F32)
    o_ref[...] = x_ref[...] + gt_ref[0] * y


def _merge_call(ya, yb, yc, main, x2, gt, wa, wb, wc, wo, seq):
    t, d = x2.shape
    tm = min(MERGE_TM, seq)
    per_b = seq // tm
    full = lambda a: pl.BlockSpec(a.shape, lambda i: (0, 0))
    return pl.pallas_call(
        _merge_kernel,
        grid=(t // tm,),
        in_specs=[pl.BlockSpec((tm, ya.shape[1]), lambda i: (i, 0)),
                  pl.BlockSpec((tm, yb.shape[1]), lambda i: (i, 0)),
                  pl.BlockSpec((tm, yc.shape[1]), lambda i: (i, 0)),
                  pl.BlockSpec((tm, 3 * d), lambda i: (i, COL_MG // (3 * d))),
                  pl.BlockSpec((tm, d), lambda i: (i, 0)),
                  pl.BlockSpec((1, 1, d), lambda i: (i // per_b, 0, 0)),
                  full(wa), full(wb), full(wc), full(wo)],
        out_specs=pl.BlockSpec((tm, d), lambda i: (i, 0)),
        out_shape=jax.ShapeDtypeStruct((t, d), F32),
        compiler_params=_cparams(("parallel",)),
        name="merge",
    )(ya, yb, yc, main, x2, gt, wa, wb, wc, wo)


def _mlp_kernel(x_ref, g_ref, sc_ref, sh_ref, gt_ref, w1_ref, w2_ref, fg_ref, o_ref, h_scr, acc_scr, *, final):
    j = pl.program_id(1)

    @pl.when(j == 0)
    def _():
        h_scr[...] = _norm_mod(x_ref[...], g_ref[...], sc_ref[0], sh_ref[0]).astype(BF16)
        acc_scr[...] = jnp.zeros(acc_scr.shape, F32)

    a = jnp.maximum(jnp.dot(h_scr[...], w1_ref[...], preferred_element_type=F32), 0.0)
    acc_scr[...] += jnp.dot((a * a).astype(BF16), w2_ref[...], preferred_element_type=F32)

    @pl.when(j == pl.num_programs(1) - 1)
    def _():
        y = x_ref[...] + gt_ref[0] * acc_scr[...]
        if final:
            ms = jnp.mean(y * y, axis=-1, keepdims=True)
            y = y * lax.rsqrt(ms + EPS) * fg_ref[...]
        o_ref[...] = y


def _mlp_call(x2, g, sc, sh, gt, w1, w2, fg, seq, final):
    t, d = x2.shape
    f = w1.shape[1]
    tm = min(MLP_TM, seq)
    tf = MLP_TF
    per_b = seq // tm
    mod = lambda: pl.BlockSpec((1, 1, d), lambda i, j: (i // per_b, 0, 0))
    return pl.pallas_call(
        functools.partial(_mlp_kernel, final=final),
        grid=(t // tm, f // tf),
        in_specs=[pl.BlockSpec((tm, d), lambda i, j: (i, 0)),
                  pl.BlockSpec((1, d), lambda i, j: (0, 0)),
                  mod(), mod(), mod(),
                  pl.BlockSpec((d, tf), lambda i, j: (0, j)),
                  pl.BlockSpec((tf, d), lambda i, j: (j, 0)),
                  pl.BlockSpec((1, d), lambda i, j: (0, 0))],
        out_specs=pl.BlockSpec((tm, d), lambda i, j: (i, 0)),
        out_shape=jax.ShapeDtypeStruct((t, d), F32),
        scratch_shapes=[pltpu.VMEM((tm, d), BF16), pltpu.VMEM((tm, d), F32)],
        compiler_params=_cparams(("parallel", "arbitrary")),
        name="mlp",
    )(x2, g.reshape(1, d), sc, sh, gt, w1, w2, fg.reshape(1, d))


def _split_w_in(w_in):
    offs = [0]
    for sz in IN_SIZES:
        offs.append(offs[-1] + sz)
    return [w_in[:, offs[i]:offs[i + 1]] for i in range(len(IN_SIZES))]


def _layout_w_in(w_in):
    nq, nkc, nvc, nks, nvs, nkw, nvw, ngate, pin, dqkv, dz, dbeta, da, mg = _split_w_in(w_in)
    d = w_in.shape[0]
    per_g = 3 * NSA_REP
    zeros = lambda n: jnp.zeros((d, n), w_in.dtype)
    gate0 = jnp.concatenate([ngate[:, :per_g], zeros(LANE_BETA - per_g), dbeta, da,
                             zeros(LANES - LANE_DECAY - DN_HEADS)], axis=1)
    gate1 = jnp.concatenate([ngate[:, per_g:], zeros(LANES - per_g)], axis=1)
    main = jnp.concatenate([nq, pin, dz, dqkv, mg, gate0, gate1, nkc, nvc], axis=1)

    def per_group(w):
        pad = zeros(LANES - NSA_DH)
        return jnp.concatenate([w[:, :NSA_DH], pad, w[:, NSA_DH:], pad], axis=1)

    kv = jnp.concatenate([per_group(nks), per_group(nvs), per_group(nkw), per_group(nvw)], axis=1)
    return main.astype(BF16), kv.astype(BF16)


def _group_lane_w1(w1):
    two, _, hid = w1.shape
    halves = w1.reshape(two, 2, CMP_STRIDE, NSA_DH, hid)
    out = []
    for g in range(NSA_GROUPS):
        padded = jnp.zeros((two, 2, CMP_STRIDE, LANES, hid), w1.dtype).at[:, :, :, g * NSA_DH:(g + 1) * NSA_DH].set(halves)
        out.append(padded.reshape(two, 2, CMP_STRIDE * LANES, hid))
    return jnp.stack(out, axis=1)


def _nsa_constants(seq):
    nb = seq // SEL_LEN
    ncp = seq // CMP_STRIDE
    j = jnp.arange(nb)[:, None] * SEL_LEN
    i = jnp.arange(ncp)[None, :] * CMP_STRIDE
    overlap = jnp.clip(jnp.minimum(j + SEL_LEN, i + CMP_LEN) - jnp.maximum(j, i), 0) // CMP_STRIDE
    block_onehot = (jnp.arange(seq)[:, None] // SEL_LEN) == jnp.arange(nb)[None, :]
    return overlap.astype(BF16), block_onehot.astype(BF16)


def kernel(x, c, ada_w, ada_b, norm1_g, norm2_g, w_in, phi_k1, phi_k2, phi_v1, phi_v2, pos_k, pos_v, pool_w, pool_scale, dn_conv_w, dn_A_log, dn_dt_bias, dn_norm_g, w_branch_nsa, w_branch_pool, w_branch_dn, w_out, mlp_w1, mlp_w2, final_g):
    bsz, seq, d = x.shape
    assert d == D_MODEL and seq % max(PROJ_TM, MLP_TM, POOL_TS) == 0 and seq % KV_CHUNK == 0
    assert DN_CHUNK == DN_DK == LANES
    t = bsz * seq
    depth = ada_w.shape[0]
    chunk = KV_CHUNK
    nqt = seq // Q_TILE
    nb = seq // SEL_LEN
    blocks_per_chunk = chunk // SEL_LEN

    slopes = 2.0 ** (-(8.0 / NSA_HEADS) * (jnp.arange(NSA_HEADS, dtype=F32) + 1.0))
    overlap, block_onehot = _nsa_constants(seq)
    c_pad = jnp.zeros((SUBLANES, d), F32).at[:bsz].set(c)
    x2 = x.reshape(t, d)

    for l in range(depth):
        mod = _mod_call(c_pad, ada_w[l], ada_b[l])[:bsz]
        sh1, sc1, gt1, sh2, sc2, gt2 = [m.reshape(bsz, 1, d) for m in jnp.split(mod, 6, axis=-1)]

        w_main, w_kv = _layout_w_in(w_in[l])
        main = _proj_call(x2, norm1_g[l], sc1, sh1, w_main, seq)
        kv = _proj_kv_call(x2, norm1_g[l], sc1, sh1, w_kv, seq)

        main3 = main.reshape(bsz, seq, N_MAIN)
        pos = jnp.zeros((2, SUBLANES, CMP_LEN * NSA_DH), F32).at[:, 0].set(
            jnp.stack([pos_k[l].reshape(-1), pos_v[l].reshape(-1)])).astype(BF16)
        w1 = jnp.stack([phi_k1[l], phi_v1[l]]).astype(BF16)
        w2 = jnp.stack([phi_k2[l], phi_v2[l]])
        w2 = jnp.concatenate([w2, w2], axis=-1).astype(BF16)
        cmp_kv = _compress_call(main3, pos, w1, _group_lane_w1(w1), w2)

        sel, act, yc = _select_call(slopes, main, cmp_kv[0], cmp_kv[1], overlap, bsz, seq)
        lst, cnt = _chunk_lists(act, seq, chunk)
        y_a = _attn_call(lst, cnt, slopes, main, sel, kv, yc, block_onehot, bsz, seq, chunk)

        y_b = _pool_call(main, pool_w[l].astype(BF16), pool_scale[l], bsz, seq)

        conv_w = jnp.zeros((SUBLANES, 3 * DN_W), F32).at[:DN_CONV].set(dn_conv_w[l])
        lane_row = lambda v: jnp.zeros((1, LANES), F32).at[0, LANE_DECAY:LANE_DECAY + DN_HEADS].set(v)
        y_c = _dn_call(main3, conv_w, lane_row(dn_A_log[l]), lane_row(dn_dt_bias[l]),
                       dn_norm_g[l]).reshape(t, DN_W)

        x2 = _merge_call(y_a, y_b, y_c, main, x2, gt1,
                         w_branch_nsa[l].astype(BF16), w_branch_pool[l].astype(BF16),
                         w_branch_dn[l].astype(BF16), w_out[l].astype(BF16), seq)
        x2 = _mlp_call(x2, norm2_g[l], sc2, sh2, gt2, mlp_w1[l].astype(BF16), mlp_w2[l].astype(BF16),
                       final_g, seq, final=(l == depth - 1))
    return x2.reshape(bsz, seq, d)
```

```python
import functools

import jax
import jax.numpy as jnp
from jax import lax
from jax.experimental import pallas as pl
from jax.experimental.pallas import tpu as pltpu

F32 = jnp.float32
BF16 = jnp.bfloat16
HIGHEST = lax.Precision.HIGHEST

D_MODEL = 1024
NSA_HEADS = 8
NSA_GROUPS = 2
NSA_REP = NSA_HEADS // NSA_GROUPS
NSA_DH = 64
CMP_LEN = 32
CMP_STRIDE = 16
SEL_LEN = 64
SEL_SHIFT = 6
TOP_N = 16
WINDOW = 512
FORCE_BONUS = 1.0e4
POOL_WINDOWS = (2, 4, 8, 16)
POOL_GROUP_DIM = 128
POOL_W = len(POOL_WINDOWS) * POOL_GROUP_DIM
DN_HEADS = 4
DN_DK = 128
DN_W = DN_HEADS * DN_DK
DN_CONV = 4
EPS = 1e-6
NEG = -1e30
NSA_Q_W = NSA_HEADS * NSA_DH
NSA_KV_W = NSA_GROUPS * NSA_DH
IN_SIZES = (NSA_Q_W, NSA_KV_W, NSA_KV_W, NSA_KV_W, NSA_KV_W, NSA_KV_W, NSA_KV_W,
            3 * NSA_HEADS, POOL_W, 3 * DN_W, DN_W, DN_HEADS, DN_HEADS, 3 * D_MODEL)

LANES = 128
LANE_SHIFT = 7
SUBLANES = 8
VMEM_LIMIT_BYTES = 48 * 1024 * 1024

Q_TILE = 128
SELECT_BUCKETS = 4
ATTN_Q_TILE = 128
ATTN_TILES = 2
KV_CHUNK = 256
DN_CHUNK = 128
DN_BATCH = 4
PROJ_TM = 1024
MLP_TM = 1024
MLP_TF = 1024
MERGE_TM = 512
POOL_TS = 1024
POOL_HALO = 16
DN_HALO = 8

COL_Q = 0
COL_POOL = 512
COL_Z = 1024
COL_QKV = 1536
COL_MG = 3072
COL_GATE = 6144
COL_KC = 6400
COL_VC = 6528
N_MAIN = 6656
PROJ_TN = 1664
LANE_BETA = 16
LANE_DECAY = 20
KVB_KS = 0
KVB_VS = 2
KVB_KW = 4
KVB_VW = 6
KVB_END = 8
N_KV = KVB_END * LANES
LANE_POS_HI = NSA_DH
LANE_POS_LO = NSA_DH + 1


def _cparams(sem):
    return pltpu.CompilerParams(dimension_semantics=sem, vmem_limit_bytes=VMEM_LIMIT_BYTES)


def _silu(v):
    return v * jax.nn.sigmoid(v)


def _mm(a, b):
    return jnp.dot(a.astype(BF16), b.astype(BF16), preferred_element_type=F32)


def _mm_nt(a, b):
    return lax.dot_general(a.astype(BF16), b.astype(BF16), (((1,), (1,)), ((), ())), preferred_element_type=F32)


def _mm_tn(a, b):
    return lax.dot_general(a.astype(BF16), b.astype(BF16), (((0,), (0,)), ((), ())), preferred_element_type=F32)


def _mod_kernel(c_ref, w_ref, b_ref, o_ref):
    cond = _silu(c_ref[...])
    o_ref[...] = jnp.dot(cond, w_ref[...], preferred_element_type=F32, precision=HIGHEST) + b_ref[...]


def _mod_call(c_pad, w, b):
    d, n = w.shape
    tn = 1536
    return pl.pallas_call(
        _mod_kernel,
        grid=(n // tn,),
        in_specs=[pl.BlockSpec((SUBLANES, d), lambda j: (0, 0)),
                  pl.BlockSpec((d, tn), lambda j: (0, j)),
                  pl.BlockSpec((1, tn), lambda j: (0, j))],
        out_specs=pl.BlockSpec((SUBLANES, tn), lambda j: (0, j)),
        out_shape=jax.ShapeDtypeStruct((SUBLANES, n), F32),
        compiler_params=_cparams(("parallel",)),
        name="mod",
    )(c_pad, w, b.reshape(1, n))


def _norm_mod(x, g, sc, sh):
    ms = jnp.mean(x * x, axis=-1, keepdims=True)
    return (x * lax.rsqrt(ms + EPS) * g) * (1.0 + sc) + sh


def _proj_kernel(x_ref, g_ref, sc_ref, sh_ref, w_ref, o_ref, h_scr):
    @pl.when(pl.program_id(1) == 0)
    def _():
        h_scr[...] = _norm_mod(x_ref[...], g_ref[...], sc_ref[0], sh_ref[0]).astype(BF16)

    o_ref[...] = jnp.dot(h_scr[...], w_ref[...], preferred_element_type=F32)


def _proj_kv_kernel(x_ref, g_ref, sc_ref, sh_ref, w_ref, o_ref, *, tiles_per_seq):
    h = _norm_mod(x_ref[...], g_ref[...], sc_ref[0], sh_ref[0]).astype(BF16)
    acc = jnp.dot(h, w_ref[...], preferred_element_type=F32)
    tm, n = acc.shape
    col = lax.broadcasted_iota(jnp.int32, (1, n), 1)
    blk = jnp.right_shift(col, LANE_SHIFT)
    lane = col & (LANES - 1)
    is_key = ((blk >= KVB_KS) & (blk < KVB_VS)) | ((blk >= KVB_KW) & (blk < KVB_VW))
    is_val = ((blk >= KVB_VS) & (blk < KVB_KW)) | ((blk >= KVB_VW) & (blk < KVB_END))
    ones_row = jnp.where(is_val & (lane >= NSA_DH), 1.0, 0.0)
    hi_row = jnp.where(is_key & (lane == LANE_POS_HI), 1.0, 0.0)
    lo_row = jnp.where(is_key & (lane == LANE_POS_LO), 1.0, 0.0)
    pos = (pl.program_id(0) % tiles_per_seq) * tm + lax.broadcasted_iota(jnp.int32, (tm, 1), 0)
    pos_hi = jnp.right_shift(pos, LANE_SHIFT).astype(F32)
    pos_lo = (pos & (LANES - 1)).astype(F32)
    o_ref[...] = (acc + ones_row + pos_hi * hi_row + pos_lo * lo_row).astype(o_ref.dtype)


def _proj_specs(tm, d, tn, per_b):
    return [pl.BlockSpec((tm, d), lambda i, j: (i, 0)),
            pl.BlockSpec((1, d), lambda i, j: (0, 0)),
            pl.BlockSpec((1, 1, d), lambda i, j: (i // per_b, 0, 0)),
            pl.BlockSpec((1, 1, d), lambda i, j: (i // per_b, 0, 0)),
            pl.BlockSpec((d, tn), lambda i, j: (0, j))]


def _proj_call(x2, g, sc, sh, w, seq):
    t, d = x2.shape
    n = w.shape[1]
    tm = min(PROJ_TM, seq)
    tn = PROJ_TN
    return pl.pallas_call(
        _proj_kernel,
        grid=(t // tm, n // tn),
        in_specs=_proj_specs(tm, d, tn, seq // tm),
        out_specs=pl.BlockSpec((tm, tn), lambda i, j: (i, j)),
        out_shape=jax.ShapeDtypeStruct((t, n), F32),
        scratch_shapes=[pltpu.VMEM((tm, d), BF16)],
        compiler_params=_cparams(("parallel", "arbitrary")),
        name="proj_main",
    )(x2, g.reshape(1, d), sc, sh, w)


def _proj_kv_call(x2, g, sc, sh, w, seq):
    t, d = x2.shape
    n = w.shape[1]
    tm = min(PROJ_TM, seq)
    return pl.pallas_call(
        functools.partial(_proj_kv_kernel, tiles_per_seq=seq // tm),
        grid=(t // tm, 1),
        in_specs=_proj_specs(tm, d, n, seq // tm),
        out_specs=pl.BlockSpec((tm, n), lambda i, j: (i, j)),
        out_shape=jax.ShapeDtypeStruct((t, n), BF16),
        compiler_params=_cparams(("parallel", "arbitrary")),
        name="proj_kv",
    )(x2, g.reshape(1, d), sc, sh, w)


def _compress_kernel(x_ref, pos_ref, w1_ref, w1g_ref, w2_ref, o_ref):
    nr = x_ref.shape[1] // CMP_STRIDE
    xs = jnp.concatenate([x_ref[0, pl.ds(rho, nr, stride=CMP_STRIDE), :].astype(BF16) for rho in range(CMP_STRIDE)],
                         axis=1)
    posb = jnp.dot(pos_ref[0], w1_ref[0], preferred_element_type=F32)[0:1]
    for g in range(NSA_GROUPS):
        top = jnp.dot(xs, w1g_ref[0, g, 0], preferred_element_type=F32)
        bot = jnp.dot(xs, w1g_ref[0, g, 1], preferred_element_type=F32)
        hid = top + pltpu.roll(bot, nr - 1, 0) + posb
        o_ref[0, 0, g] = _mm(_silu(hid), w2_ref[0]).astype(o_ref.dtype)


def _compress_call(main3, pos, w1, w1g, w2):
    b, seq, _ = main3.shape
    nr = seq // CMP_STRIDE
    two, l64, hid = w1.shape
    return pl.pallas_call(
        _compress_kernel,
        grid=(two, b),
        in_specs=[pl.BlockSpec((1, seq, LANES), lambda a, i: (i, 0, COL_KC // LANES + a)),
                  pl.BlockSpec((1, SUBLANES, l64), lambda a, i: (a, 0, 0)),
                  pl.BlockSpec((1, l64, hid), lambda a, i: (a, 0, 0)),
                  pl.BlockSpec((1,) + w1g.shape[1:], lambda a, i: (a, 0, 0, 0, 0)),
                  pl.BlockSpec((1, hid, LANES), lambda a, i: (a, 0, 0))],
        out_specs=pl.BlockSpec((1, 1, NSA_GROUPS, nr, LANES), lambda a, i: (a, i, 0, 0, 0)),
        out_shape=jax.ShapeDtypeStruct((two, b, NSA_GROUPS, nr, LANES), BF16),
        compiler_params=_cparams(("parallel", "parallel")),
        name="compress",
    )(main3, pos, w1, w1g, w2)


def _gate_cols(gate_logits, branch):
    sg = jax.nn.sigmoid(gate_logits)
    return [sg[:, 3 * r + branch:3 * r + branch + 1] for r in range(NSA_REP)]


def _stack_heads_dup(q):
    qa = q[:, :LANES]
    qb = q[:, LANES:]
    lo = lax.broadcasted_iota(jnp.int32, qa.shape, 1) < NSA_DH
    z = jnp.zeros_like(qa)
    return jnp.concatenate([jnp.where(lo, qa, z), jnp.where(lo, z, qa),
                            jnp.where(lo, qb, z), jnp.where(lo, z, qb)], axis=0)


def _unstack_heads_dup(o, gates):
    nq = o.shape[0] // NSA_REP
    lo = lax.broadcasted_iota(jnp.int32, (nq, LANES), 1) < NSA_DH
    a = jnp.where(lo, gates[0] * o[0:nq], gates[1] * o[nq:2 * nq])
    b = jnp.where(lo, gates[2] * o[2 * nq:3 * nq], gates[3] * o[3 * nq:])
    return jnp.concatenate([a, b], axis=1)


def _stack_heads_pos(q, slopes):
    qa = q[:, :LANES]
    qb = q[:, LANES:]
    lane = lax.broadcasted_iota(jnp.int32, qa.shape, 1)
    lo = lane < NSA_DH
    heads = [qa, pltpu.roll(qa, NSA_DH, 1), qb, pltpu.roll(qb, NSA_DH, 1)]
    out = []
    for r in range(NSA_REP):
        extra = jnp.where(lane == LANE_POS_HI, slopes[r] * LANES, jnp.where(lane == LANE_POS_LO, slopes[r], 0.0))
        out.append(jnp.where(lo, heads[r], extra))
    return jnp.concatenate(out, axis=0)


def _finish_heads(acc, gates):
    nq = acc.shape[0] // NSA_REP
    o = acc * pltpu.roll(1.0 / acc, NSA_DH, 1)
    lo = lax.broadcasted_iota(jnp.int32, (nq, LANES), 1) < NSA_DH
    hs = [gates[r] * o[r * nq:(r + 1) * nq] for r in range(NSA_REP)]
    a = jnp.where(lo, hs[0], pltpu.roll(hs[1], NSA_DH, 1))
    b = jnp.where(lo, hs[2], pltpu.roll(hs[3], NSA_DH, 1))
    return jnp.concatenate([a, b], axis=1)


def _select_kernel(slopes_ref, q_ref, kc_ref, vc_ref, gates_ref, ov_ref, sel_ref, act_ref, yc_ref):
    qt = pl.program_id(1)
    nqt = pl.num_programs(1)
    nb_all, ncp_all = ov_ref.shape
    for k in range(SELECT_BUCKETS):
        @pl.when((qt * SELECT_BUCKETS) // nqt == k)
        def _(k=k):
            _select_prefix(slopes_ref, q_ref, kc_ref, vc_ref, gates_ref, ov_ref, sel_ref, act_ref, yc_ref,
                           nb=(k + 1) * nb_all // SELECT_BUCKETS, ncp=(k + 1) * ncp_all // SELECT_BUCKETS)


def _select_prefix(slopes_ref, q_ref, kc_ref, vc_ref, gates_ref, ov_ref, sel_ref, act_ref, yc_ref, *, nb, ncp):
    t0 = pl.program_id(1) * Q_TILE
    nq = Q_TILE
    nb_all = ov_ref.shape[0]
    c_end = lax.broadcasted_iota(jnp.int32, (nq, ncp), 1) * CMP_STRIDE + (CMP_LEN - 1)
    tq = t0 + lax.broadcasted_iota(jnp.int32, (nq, ncp), 0)
    vis = c_end <= tq
    any_vis = tq[:, 0:1] >= CMP_LEN - 1
    rel = (c_end[0:1] - t0).astype(F32)
    ov = ov_ref[:nb, :ncp]
    jb = lax.broadcasted_iota(jnp.int32, (nb, nq), 0)
    tql = t0 + lax.broadcasted_iota(jnp.int32, (nb, nq), 1)
    valid = jb * SEL_LEN <= tql
    forced = (jb == 0) | (jb == jnp.right_shift(tql, SEL_SHIFT))
    scores = []
    for g in range(NSA_GROUPS):
        qcols = slice(g * NSA_REP * NSA_DH, (g + 1) * NSA_REP * NSA_DH)
        qs = _stack_heads_dup(q_ref[:, qcols] * (NSA_DH ** -0.5)).astype(BF16)
        s = _mm_nt(qs, kc_ref[0, g, :ncp, :])
        psum = jnp.zeros((nq, ncp), F32)
        ps = []
        for r in range(NSA_REP):
            m_r = slopes_ref[g * NSA_REP + r]
            sr = jnp.where(vis, s[r * nq:(r + 1) * nq] + m_r * rel, NEG)
            mx = jnp.max(sr, axis=-1, keepdims=True)
            e = jnp.exp(sr - mx)
            den = jnp.sum(e, axis=-1, keepdims=True)
            p = e * jnp.where(any_vis, 1.0 / den, 0.0)
            psum = psum + p
            ps.append(p.astype(BF16))
        o_c = jnp.dot(jnp.concatenate(ps, axis=0), vc_ref[0, g, :ncp, :], preferred_element_type=F32)
        yc_ref[:, qcols] = _unstack_heads_dup(o_c, _gate_cols(gates_ref[:, g * LANES:(g + 1) * LANES], 0))
        p_hi = psum.astype(BF16)
        p_lo = (psum - p_hi.astype(F32)).astype(BF16)
        imp_t = _mm_nt(ov, p_hi) + _mm_nt(ov, p_lo)
        scores.append(jnp.where(valid, imp_t + FORCE_BONUS * forced.astype(F32), NEG))

    for _ in range(min(TOP_N, nb)):
        for g in range(NSA_GROUPS):
            mx = jnp.max(scores[g], axis=0, keepdims=True)
            idx = jnp.min(jnp.where(scores[g] == mx, jb, nb), axis=0, keepdims=True)
            scores[g] = jnp.where(jb == idx, -jnp.inf, scores[g])
    for g in range(NSA_GROUPS):
        sel_t = jnp.where((scores[g] == -jnp.inf) & valid, 1.0, 0.0)
        if nb < nb_all:
            sel_t = jnp.concatenate([sel_t, jnp.zeros((nb_all - nb, nq), F32)], axis=0)
        sel_q = sel_t.T
        sel_ref[0, g] = sel_q.astype(sel_ref.dtype)
        act_ref[0, g, 0] = jnp.max(sel_q, axis=0, keepdims=True)


def _select_call(slopes, main, kcmp, vcmp, ov, bsz, seq):
    nqt = seq // Q_TILE
    nb, ncp = ov.shape
    row = lambda b, q: b * nqt + q
    gw = NSA_GROUPS * LANES
    return pl.pallas_call(
        _select_kernel,
        grid=(bsz, nqt),
        in_specs=[pl.BlockSpec(memory_space=pltpu.SMEM),
                  pl.BlockSpec((Q_TILE, NSA_Q_W), lambda b, q: (row(b, q), COL_Q // NSA_Q_W)),
                  pl.BlockSpec((1, NSA_GROUPS, ncp, LANES), lambda b, q: (b, 0, 0, 0)),
                  pl.BlockSpec((1, NSA_GROUPS, ncp, LANES), lambda b, q: (b, 0, 0, 0)),
                  pl.BlockSpec((Q_TILE, gw), lambda b, q: (row(b, q), COL_GATE // gw)),
                  pl.BlockSpec((nb, ncp), lambda b, q: (0, 0))],
        out_specs=[pl.BlockSpec((1, NSA_GROUPS, Q_TILE, nb), lambda b, q: (b, 0, q, 0)),
                   pl.BlockSpec((1, NSA_GROUPS, 1, 1, nb), lambda b, q: (b, 0, q, 0, 0)),
                   pl.BlockSpec((Q_TILE, NSA_Q_W), lambda b, q: (row(b, q), 0))],
        out_shape=[jax.ShapeDtypeStruct((bsz, NSA_GROUPS, seq, nb), BF16),
                   jax.ShapeDtypeStruct((bsz, NSA_GROUPS, nqt, 1, nb), F32),
                   jax.ShapeDtypeStruct((bsz * seq, NSA_Q_W), F32)],
        compiler_params=_cparams(("parallel", "parallel")),
        name="select",
    )(slopes, main, kcmp, vcmp, main, ov)


def _attn_kernel(lst_ref, cnt_ref, slopes_ref, q_ref, sel_ref, ks_ref, vs_ref, kw_ref, vw_ref, gates_ref, yc_ref,
                 et_ref, o_ref, s_scr, p_scr, m_scr, a_scr, acc_scr, *, chunk):
    b = pl.program_id(0)
    g = pl.program_id(1)
    step = pl.program_id(2)
    nqt = pl.num_programs(2) * ATTN_TILES
    nq = ATTN_Q_TILE
    tiles = range(ATTN_TILES)
    nch_total = ks_ref.shape[0] // chunk
    slopes = [slopes_ref[g * NSA_REP + r] for r in range(NSA_REP)]
    qts = [step * ATTN_TILES + u for u in tiles]
    t0s = [qt * nq for qt in qts]
    rws = [slice(u * nq, (u + 1) * nq) for u in tiles]
    qs = [_stack_heads_pos(q_ref[rws[u], :] * (NSA_DH ** -0.5), slopes).astype(BF16) for u in tiles]
    pens = [((sel_ref[0, 0, rws[u], :].astype(F32) - 1.0) * (-NEG)).astype(BF16) for u in tiles]
    qx = [jnp.concatenate([qs[u], jnp.concatenate([pens[u]] * NSA_REP, axis=0)], axis=1) for u in tiles]

    bases = [((b * NSA_GROUPS + g) * nqt + qts[u]) * nch_total for u in tiles]
    cnts = [cnt_ref[(b * NSA_GROUPS + g) * nqt + qts[u]] for u in tiles]

    def logits(u, c):
        start = pl.multiple_of(c * chunk, chunk)
        kx = jnp.concatenate([ks_ref[pl.ds(start, chunk), :], et_ref[pl.ds(start, chunk), :]], axis=1)
        return _mm_nt(qx[u], kx)

    def values(p, c):
        start = pl.multiple_of(c * chunk, chunk)
        return jnp.dot(p, vs_ref[pl.ds(start, chunk), :], preferred_element_type=F32)

    def row_max(s):
        return jnp.broadcast_to(jnp.max(s, axis=-1, keepdims=True), (s.shape[0], LANES))

    def lane_tile(m, width):
        return jnp.concatenate([m] * (width // LANES), axis=1)

    for u in tiles:
        m_scr[u] = jnp.full(m_scr.shape[1:], NEG, F32)
        a_scr[u] = jnp.ones(a_scr.shape[1:], F32)
        acc_scr[u] = jnp.zeros(acc_scr.shape[1:], F32)
        p_scr[u, 1] = jnp.zeros(p_scr.shape[2:], BF16)
    for u in tiles:
        s_scr[u, 0] = logits(u, lst_ref[bases[u]])

    def flash_step(us, i, slot):
        for u in us:
            acc_scr[u] = a_scr[u] * acc_scr[u] + values(p_scr[u, 1 - slot], lst_ref[bases[u] + jnp.maximum(i - 1, 0)])
        for u in us:
            s = s_scr[u, slot]
            m_prev = m_scr[u]
            m_new = jnp.maximum(m_prev, row_max(s))
            a_scr[u] = jnp.exp(m_prev - m_new)
            p_scr[u, slot] = jnp.exp(s - lane_tile(m_new, chunk)).astype(BF16)
            m_scr[u] = m_new
        for u in us:
            s_scr[u, 1 - slot] = logits(u, lst_ref[bases[u] + i + 1])

    def single_steps(us):
        def body(i, carry):
            flash_step(us, i, i & 1)
            return carry
        return body

    def double_steps(us):
        def body(j, carry):
            flash_step(us, 2 * j, 0)
            flash_step(us, 2 * j + 1, 1)
            return carry
        return body

    joint = functools.reduce(jnp.minimum, cnts)
    lax.fori_loop(0, joint // 2, double_steps(list(tiles)), 0)
    lax.fori_loop(joint - (joint & 1), joint, single_steps(list(tiles)), 0)
    for u in tiles:
        lax.fori_loop(joint, cnts[u], single_steps([u]), 0)

    slots = [cnts[u] & 1 for u in tiles]
    c_diag = [t0s[u] // chunk for u in tiles]
    acc = [a_scr[u] * acc_scr[u] + values(p_scr[u, 1 - slots[u]], lst_ref[bases[u] + jnp.maximum(cnts[u] - 1, 0)])
           for u in tiles]
    nw = WINDOW + nq
    wbase = [pl.multiple_of(jnp.maximum(t0s[u] - WINDOW, 0), nq) for u in tiles]
    s_win = [_mm_nt(qs[u], kw_ref[pl.ds(wbase[u], nw), :]) for u in tiles]

    def masked(s, ok):
        return jnp.concatenate([jnp.where(ok, s[r * nq:(r + 1) * nq], NEG) for r in range(NSA_REP)], axis=0)

    s_diag = []
    for u in tiles:
        pos = c_diag[u] * chunk + lax.broadcasted_iota(jnp.int32, (nq, chunk), 1)
        tq = t0s[u] + lax.broadcasted_iota(jnp.int32, (nq, chunk), 0)
        s_diag.append(masked(s_scr[u, slots[u]], pos <= tq))
    m_prev = [m_scr[u] for u in tiles]
    m_new = [jnp.maximum(m_prev[u], row_max(s_diag[u])) for u in tiles]
    acc = [jnp.exp(m_prev[u] - m_new[u]) * acc[u]
           + values(jnp.exp(s_diag[u] - lane_tile(m_new[u], chunk)).astype(BF16), c_diag[u]) for u in tiles]
    p_win = []
    for u in tiles:
        pos = wbase[u] + lax.broadcasted_iota(jnp.int32, (nq, nw), 1)
        tq = t0s[u] + lax.broadcasted_iota(jnp.int32, (nq, nw), 0)
        s = masked(s_win[u], (pos <= tq) & (pos > tq - WINDOW))
        p_win.append(jnp.exp(s - jnp.max(s, axis=-1, keepdims=True)).astype(BF16))
    acc_w = [jnp.dot(p_win[u], vw_ref[pl.ds(wbase[u], nw), :], preferred_element_type=F32) for u in tiles]
    for u in tiles:
        gates = gates_ref[rws[u], :]
        o_ref[rws[u], :] = (yc_ref[rws[u], :] + _finish_heads(acc[u], _gate_cols(gates, 1))
                            + _finish_heads(acc_w[u], _gate_cols(gates, 2)))


def _chunk_lists(act, seq, chunk):
    bsz, ngrp = act.shape[:2]
    nqt = seq // ATTN_Q_TILE
    nch = seq // chunk
    active = act.reshape(bsz, ngrp, nqt, ATTN_Q_TILE // Q_TILE, nch, chunk // SEL_LEN).max(axis=(3, 5)) > 0.0
    c_diag = (jnp.arange(nqt) * ATTN_Q_TILE) // chunk
    active = active & (jnp.arange(nch)[None, :] < c_diag[:, None])
    cnt = active.sum(axis=-1).astype(jnp.int32)
    order = jnp.argsort(jnp.logical_not(active), axis=-1, stable=True).astype(jnp.int32)
    lst = jnp.where(jnp.arange(nch) < cnt[..., None], order, c_diag[:, None].astype(jnp.int32))
    return lst.reshape(-1), cnt.reshape(-1)


def _attn_call(lst, cnt, slopes, main, sel, kv, yc, et, bsz, seq, chunk):
    tq = ATTN_Q_TILE * ATTN_TILES
    nqt = seq // tq
    nb = sel.shape[-1]
    rows = NSA_REP * ATTN_Q_TILE
    row = lambda b, g, q: b * nqt + q
    kvspec = lambda blk: pl.BlockSpec((seq, LANES), lambda b, g, q, *_: (b, blk + g))
    grid_spec = pltpu.PrefetchScalarGridSpec(
        num_scalar_prefetch=2,
        grid=(bsz, NSA_GROUPS, nqt),
        in_specs=[pl.BlockSpec(memory_space=pltpu.SMEM),
                  pl.BlockSpec((tq, 256), lambda b, g, q, *_: (row(b, g, q), COL_Q // 256 + g)),
                  pl.BlockSpec((1, 1, tq, nb), lambda b, g, q, *_: (b, g, q, 0)),
                  kvspec(KVB_KS), kvspec(KVB_VS), kvspec(KVB_KW), kvspec(KVB_VW),
                  pl.BlockSpec((tq, LANES), lambda b, g, q, *_: (row(b, g, q), COL_GATE // LANES + g)),
                  pl.BlockSpec((tq, 256), lambda b, g, q, *_: (row(b, g, q), g)),
                  pl.BlockSpec(et.shape, lambda b, g, q, *_: (0, 0))],
        out_specs=pl.BlockSpec((tq, 256), lambda b, g, q, *_: (row(b, g, q), g)),
        scratch_shapes=[pltpu.VMEM((ATTN_TILES, 2, rows, chunk), F32),
                        pltpu.VMEM((ATTN_TILES, 2, rows, chunk), BF16),
                        pltpu.VMEM((ATTN_TILES, rows, LANES), F32),
                        pltpu.VMEM((ATTN_TILES, rows, LANES), F32),
                        pltpu.VMEM((ATTN_TILES, rows, LANES), F32)],
    )
    return pl.pallas_call(
        functools.partial(_attn_kernel, chunk=chunk),
        grid_spec=grid_spec,
        out_shape=jax.ShapeDtypeStruct((bsz * seq, NSA_Q_W), F32),
        compiler_params=_cparams(("parallel", "parallel", "parallel")),
        name="attn",
    )(lst, cnt, slopes, main, sel, kv, kv, kv, kv, main, yc, et)


def _pool_kernel(x_ref, halo_ref, w_ref, sc_ref, o_ref):
    s = pl.program_id(1)
    ts = x_ref.shape[0]
    x = x_ref[...]
    halo = jnp.where(s > 0, halo_ref[...], 0.0)
    t = s * ts + lax.broadcasted_iota(jnp.int32, (ts, 1), 0)
    for gi, win in enumerate(POOL_WINDOWS):
        lanes = slice(gi * POOL_GROUP_DIM, (gi + 1) * POOL_GROUP_DIM)
        ext = jnp.concatenate([halo[:, lanes], x[:, lanes]], axis=0)
        acc = ext
        span = 1
        while span < win:
            acc = acc + pltpu.roll(acc, span, 0)
            span *= 2
        cnt = jnp.minimum(t + 1, win).astype(F32)
        y = acc[POOL_HALO:] / cnt - x[:, lanes]
        o_ref[:, lanes] = _mm(y, w_ref[gi]) * sc_ref[:, lanes]


def _pool_call(main, w, scale, bsz, seq):
    ts = min(POOL_TS, seq)
    per_b = seq // ts
    halo_per_tile = ts // POOL_HALO
    return pl.pallas_call(
        _pool_kernel,
        grid=(bsz, per_b),
        in_specs=[pl.BlockSpec((ts, POOL_W), lambda b, s: (b * per_b + s, COL_POOL // POOL_W)),
                  pl.BlockSpec((POOL_HALO, POOL_W),
                               lambda b, s: (jnp.maximum((b * per_b + s) * halo_per_tile - 1, 0), COL_POOL // POOL_W)),
                  pl.BlockSpec(w.shape, lambda b, s: (0, 0, 0)),
                  pl.BlockSpec((1, POOL_W), lambda b, s: (0, 0))],
        out_specs=pl.BlockSpec((ts, POOL_W), lambda b, s: (b * per_b + s, 0)),
        out_shape=jax.ShapeDtypeStruct((bsz * seq, POOL_W), F32),
        compiler_params=_cparams(("parallel", "parallel")),
        name="pool",
    )(main, main, w, scale.reshape(1, POOL_W))


def _unit_lower_inverses(nms, ri, ci):
    c = nms[0].shape[0]
    eye = jnp.where(ri == ci, 1.0, 0.0)
    in8 = jnp.right_shift(ri, 3) == jnp.right_shift(ci, 3)
    n8 = [jnp.where(in8, nm, 0.0) for nm in nms]
    ps = [eye - a for a in n8]
    ms = [_mm(a, a) for a in n8]
    ps = [p + _mm(p, m) for p, m in zip(ps, ms)]
    ms = [_mm(m, m) for m in ms]
    ps = [p + _mm(p, m) for p, m in zip(ps, ms)]
    shift = 3
    while (1 << shift) < c:
        rb = jnp.right_shift(ri, shift)
        cb = jnp.right_shift(ci, shift)
        lower_left = ((rb & 1) == 1) & (cb == rb - 1)
        ts = [_mm(p, jnp.where(lower_left, nm, 0.0)) for p, nm in zip(ps, nms)]
        ps = [p - _mm(t, p) for p, t in zip(ps, ts)]
        shift += 1
    return ps


def _row_cumsum(x, ri):
    span = 1
    while span < x.shape[0]:
        x = x + jnp.where(ri >= span, pltpu.roll(x, span, 0), 0.0)
        span *= 2
    return x


def _dn_kernel(qkv_ref, halo_ref, z_ref, gba_ref, cw_ref, alog_ref, dtb_ref, ng_ref, o_ref, st_scr):
    s = pl.program_id(1)
    nbatch, c = qkv_ref.shape[0], qkv_ref.shape[1]

    @pl.when(s == 0)
    def _():
        st_scr[...] = jnp.zeros(st_scr.shape, F32)

    ri = lax.broadcasted_iota(jnp.int32, (c, c), 0)
    ci = lax.broadcasted_iota(jnp.int32, (c, c), 1)
    tril = ri >= ci
    strict = ri > ci
    cw = cw_ref[...]
    ng = ng_ref[...]
    chains = [(bi, h) for bi in range(nbatch) for h in range(DN_HEADS)]
    qn, kn, kb, vb, gcb, lmask = [], [], [], [], [], []
    for bi in range(nbatch):
        halo = jnp.where(s > 0, halo_ref[bi], 0.0)
        ext = jnp.concatenate([halo, qkv_ref[bi]], axis=0)
        conv = cw[DN_CONV - 1:DN_CONV] * ext
        for tap in range(1, DN_CONV):
            conv = conv + cw[DN_CONV - 1 - tap:DN_CONV - tap] * pltpu.roll(ext, tap, 0)
        act = _silu(conv[DN_HALO:])

        gba = gba_ref[bi]
        beta_all = jax.nn.sigmoid(gba)
        xs = gba + dtb_ref[...]
        softplus = jnp.maximum(xs, 0.0) + jnp.log(1.0 + jnp.exp(-jnp.abs(xs)))
        gc_all = _row_cumsum(-jnp.exp(alog_ref[...]) * softplus, ri)
        gct_all = gc_all.T
        for h in range(DN_HEADS):
            q = act[:, h * DN_DK:(h + 1) * DN_DK]
            k = act[:, DN_W + h * DN_DK:DN_W + (h + 1) * DN_DK]
            v = act[:, 2 * DN_W + h * DN_DK:2 * DN_W + (h + 1) * DN_DK]
            qn.append(q * lax.rsqrt(jnp.sum(q * q, axis=-1, keepdims=True) + EPS) * (DN_DK ** -0.5))
            kn.append(k * lax.rsqrt(jnp.sum(k * k, axis=-1, keepdims=True) + EPS))
            beta = beta_all[:, LANE_BETA + h:LANE_BETA + h + 1]
            gcb.append(jnp.broadcast_to(gc_all[:, LANE_DECAY + h:LANE_DECAY + h + 1], (c, DN_DK)))
            gct = jnp.broadcast_to(gct_all[LANE_DECAY + h:LANE_DECAY + h + 1, :], (c, c))
            lmask.append(jnp.where(tril, jnp.exp(jnp.where(tril, gcb[-1] - gct, 0.0)), 0.0))
            kb.append(kn[-1] * beta)
            vb.append(v * beta)

    n = range(len(chains))
    kk = [_mm_nt(kb[i], kn[i]) for i in n]
    tinv = _unit_lower_inverses([jnp.where(strict, kk[i] * lmask[i], 0.0) for i in n], ri, ci)
    egc = [jnp.exp(gcb[i]) for i in n]
    sol = [_mm(tinv[i], jnp.concatenate([vb[i], kb[i] * egc[i]], axis=1)) for i in n]
    aqk = [_mm_nt(qn[i], kn[i]) * lmask[i] for i in n]
    state = [st_scr[i] for i in n]
    glast = [gcb[i][c - 1:c, :] for i in n]
    v_new = [sol[i][:, :DN_DK] - _mm(sol[i][:, DN_DK:], state[i]) for i in n]
    o_state = [_mm(qn[i] * egc[i], state[i]) for i in n]
    o = [o_state[i] + _mm(aqk[i], v_new[i]) for i in n]
    upd = [_mm_tn(kn[i] * jnp.exp(glast[i] - gcb[i]), v_new[i]) for i in n]
    for i, (bi, h) in enumerate(chains):
        st_scr[i] = state[i] * jnp.exp(glast[i]) + upd[i]
        on = o[i] * lax.rsqrt(jnp.mean(o[i] * o[i], axis=-1, keepdims=True) + EPS) * ng
        o_ref[bi, :, h * DN_DK:(h + 1) * DN_DK] = on * _silu(z_ref[bi, :, h * DN_DK:(h + 1) * DN_DK])


def _dn_call(main3, conv_w, alog_row, dtb_row, norm_g):
    bsz, seq, _ = main3.shape
    c = DN_CHUNK
    nbt = DN_BATCH if bsz % DN_BATCH == 0 else 1
    halo_per_chunk = c // DN_HALO
    w3 = 3 * DN_W
    return pl.pallas_call(
        _dn_kernel,
        grid=(bsz // nbt, seq // c),
        in_specs=[pl.BlockSpec((nbt, c, w3), lambda b, s: (b, s, COL_QKV // w3)),
                  pl.BlockSpec((nbt, DN_HALO, w3),
                               lambda b, s: (b, jnp.maximum(s * halo_per_chunk - 1, 0), COL_QKV // w3)),
                  pl.BlockSpec((nbt, c, DN_W), lambda b, s: (b, s, COL_Z // DN_W)),
                  pl.BlockSpec((nbt, c, LANES), lambda b, s: (b, s, COL_GATE // LANES)),
                  pl.BlockSpec((SUBLANES, w3), lambda b, s: (0, 0)),
                  pl.BlockSpec((1, LANES), lambda b, s: (0, 0)),
                  pl.BlockSpec((1, LANES), lambda b, s: (0, 0)),
                  pl.BlockSpec((1, DN_DK), lambda b, s: (0, 0))],
        out_specs=pl.BlockSpec((nbt, c, DN_W), lambda b, s: (b, s, 0)),
        out_shape=jax.ShapeDtypeStruct((bsz, seq, DN_W), F32),
        scratch_shapes=[pltpu.VMEM((nbt * DN_HEADS, DN_DK, DN_DK), F32)],
        compiler_params=_cparams(("parallel", "arbitrary")),
        name="dn",
    )(main3, main3, main3, main3, conv_w, alog_row, dtb_row, norm_g.reshape(1, DN_DK))


def _merge_kernel(ya_ref, yb_ref, yc_ref, mg_ref, x_ref, gt_ref, wa_ref, wb_ref, wc_ref, wo_ref, o_ref):
    d = x_ref.shape[1]
    gates = jax.nn.sigmoid(mg_ref[...])
    merged = (gates[:, :d] * jnp.dot(ya_ref[...].astype(BF16), wa_ref[...], preferred_element_type=F32)
              + gates[:, d:2 * d] * jnp.dot(yb_ref[...].astype(BF16), wb_ref[...], preferred_element_type=F32)
              + gates[:, 2 * d:] * jnp.dot(yc_ref[...].astype(BF16), wc_ref[...], preferred_element_type=F32))
    y = jnp.dot(merged.astype(BF16), wo_ref[...], preferred_element_type=F32)
    o_ref[...] = x_ref[...] + gt_ref[0] * y


def _merge_call(ya, yb, yc, main, x2, gt, wa, wb, wc, wo, seq):
    t, d = x2.shape
    tm = min(MERGE_TM, seq)
    per_b = seq // tm
    full = lambda a: pl.BlockSpec(a.shape, lambda i: (0, 0))
    return pl.pallas_call(
        _merge_kernel,
        grid=(t // tm,),
        in_specs=[pl.BlockSpec((tm, ya.shape[1]), lambda i: (i, 0)),
                  pl.BlockSpec((tm, yb.shape[1]), lambda i: (i, 0)),
                  pl.BlockSpec((tm, yc.shape[1]), lambda i: (i, 0)),
                  pl.BlockSpec((tm, 3 * d), lambda i: (i, COL_MG // (3 * d))),
                  pl.BlockSpec((tm, d), lambda i: (i, 0)),
                  pl.BlockSpec((1, 1, d), lambda i: (i // per_b, 0, 0)),
                  full(wa), full(wb), full(wc), full(wo)],
        out_specs=pl.BlockSpec((tm, d), lambda i: (i, 0)),
        out_shape=jax.ShapeDtypeStruct((t, d), F32),
        compiler_params=_cparams(("parallel",)),
        name="merge",
    )(ya, yb, yc, main, x2, gt, wa, wb, wc, wo)


def _mlp_kernel(x_ref, g_ref, sc_ref, sh_ref, gt_ref, w1_ref, w2_ref, fg_ref, o_ref, h_scr, acc_scr, *, final):
    j = pl.program_id(1)

    @pl.when(j == 0)
    def _():
        h_scr[...] = _norm_mod(x_ref[...], g_ref[...], sc_ref[0], sh_ref[0]).astype(BF16)
        acc_scr[...] = jnp.zeros(acc_scr.shape, F32)

    a = jnp.maximum(jnp.dot(h_scr[...], w1_ref[...], preferred_element_type=F32), 0.0)
    acc_scr[...] += jnp.dot((a * a).astype(BF16), w2_ref[...], preferred_element_type=F32)

    @pl.when(j == pl.num_programs(1) - 1)
    def _():
        y = x_ref[...] + gt_ref[0] * acc_scr[...]
        if final:
            ms = jnp.mean(y * y, axis=-1, keepdims=True)
            y = y * lax.rsqrt(ms + EPS) * fg_ref[...]
        o_ref[...] = y


def _mlp_call(x2, g, sc, sh, gt, w1, w2, fg, seq, final):
    t, d = x2.shape
    f = w1.shape[1]
    tm = min(MLP_TM, seq)
    tf = MLP_TF
    per_b = seq // tm
    mod = lambda: pl.BlockSpec((1, 1, d), lambda i, j: (i // per_b, 0, 0))
    return pl.pallas_call(
        functools.partial(_mlp_kernel, final=final),
        grid=(t // tm, f // tf),
        in_specs=[pl.BlockSpec((tm, d), lambda i, j: (i, 0)),
                  pl.BlockSpec((1, d), lambda i, j: (0, 0)),
                  mod(), mod(), mod(),
                  pl.BlockSpec((d, tf), lambda i, j: (0, j)),
                  pl.BlockSpec((tf, d), lambda i, j: (j, 0)),
                  pl.BlockSpec((1, d), lambda i, j: (0, 0))],
        out_specs=pl.BlockSpec((tm, d), lambda i, j: (i, 0)),
        out_shape=jax.ShapeDtypeStruct((t, d), F32),
        scratch_shapes=[pltpu.VMEM((tm, d), BF16), pltpu.VMEM((tm, d), F32)],
        compiler_params=_cparams(("parallel", "arbitrary")),
        name="mlp",
    )(x2, g.reshape(1, d), sc, sh, gt, w1, w2, fg.reshape(1, d))


def _split_w_in(w_in):
    offs = [0]
    for sz in IN_SIZES:
        offs.append(offs[-1] + sz)
    return [w_in[:, offs[i]:offs[i + 1]] for i in range(len(IN_SIZES))]


def _layout_w_in(w_in):
    nq, nkc, nvc, nks, nvs, nkw, nvw, ngate, pin, dqkv, dz, dbeta, da, mg = _split_w_in(w_in)
    d = w_in.shape[0]
    per_g = 3 * NSA_REP
    zeros = lambda n: jnp.zeros((d, n), w_in.dtype)
    gate0 = jnp.concatenate([ngate[:, :per_g], zeros(LANE_BETA - per_g), dbeta, da,
                             zeros(LANES - LANE_DECAY - DN_HEADS)], axis=1)
    gate1 = jnp.concatenate([ngate[:, per_g:], zeros(LANES - per_g)], axis=1)
    main = jnp.concatenate([nq, pin, dz, dqkv, mg, gate0, gate1, nkc, nvc], axis=1)

    def per_group(w):
        pad = zeros(LANES - NSA_DH)
        return jnp.concatenate([w[:, :NSA_DH], pad, w[:, NSA_DH:], pad], axis=1)

    kv = jnp.concatenate([per_group(nks), per_group(nvs), per_group(nkw), per_group(nvw)], axis=1)
    return main.astype(BF16), kv.astype(BF16)


def _group_lane_w1(w1):
    two, _, hid = w1.shape
    halves = w1.reshape(two, 2, CMP_STRIDE, NSA_DH, hid)
    out = []
    for g in range(NSA_GROUPS):
        padded = jnp.zeros((two, 2, CMP_STRIDE, LANES, hid), w1.dtype).at[:, :, :, g * NSA_DH:(g + 1) * NSA_DH].set(halves)
        out.append(padded.reshape(two, 2, CMP_STRIDE * LANES, hid))
    return jnp.stack(out, axis=1)


def _nsa_constants(seq):
    nb = seq // SEL_LEN
    ncp = seq // CMP_STRIDE
    j = jnp.arange(nb)[:, None] * SEL_LEN
    i = jnp.arange(ncp)[None, :] * CMP_STRIDE
    overlap = jnp.clip(jnp.minimum(j + SEL_LEN, i + CMP_LEN) - jnp.maximum(j, i), 0) // CMP_STRIDE
    block_onehot = (jnp.arange(seq)[:, None] // SEL_LEN) == jnp.arange(nb)[None, :]
    return overlap.astype(BF16), block_onehot.astype(BF16)


def kernel(x, c, ada_w, ada_b, norm1_g, norm2_g, w_in, phi_k1, phi_k2, phi_v1, phi_v2, pos_k, pos_v, pool_w, pool_scale, dn_conv_w, dn_A_log, dn_dt_bias, dn_norm_g, w_branch_nsa, w_branch_pool, w_branch_dn, w_out, mlp_w1, mlp_w2, final_g):
    bsz, seq, d = x.shape
    assert d == D_MODEL and seq % max(PROJ_TM, MLP_TM, POOL_TS) == 0 and seq % KV_CHUNK == 0
    assert DN_CHUNK == DN_DK == LANES
    t = bsz * seq
    depth = ada_w.shape[0]
    chunk = KV_CHUNK
    nqt = seq // Q_TILE
    nb = seq // SEL_LEN
    blocks_per_chunk = chunk // SEL_LEN

    slopes = 2.0 ** (-(8.0 / NSA_HEADS) * (jnp.arange(NSA_HEADS, dtype=F32) + 1.0))
    overlap, block_onehot = _nsa_constants(seq)
    c_pad = jnp.zeros((SUBLANES, d), F32).at[:bsz].set(c)
    x2 = x.reshape(t, d)

    for l in range(depth):
        mod = _mod_call(c_pad, ada_w[l], ada_b[l])[:bsz]
        sh1, sc1, gt1, sh2, sc2, gt2 = [m.reshape(bsz, 1, d) for m in jnp.split(mod, 6, axis=-1)]

        w_main, w_kv = _layout_w_in(w_in[l])
        main = _proj_call(x2, norm1_g[l], sc1, sh1, w_main, seq)
        kv = _proj_kv_call(x2, norm1_g[l], sc1, sh1, w_kv, seq)

        main3 = main.reshape(bsz, seq, N_MAIN)
        pos = jnp.zeros((2, SUBLANES, CMP_LEN * NSA_DH), F32).at[:, 0].set(
            jnp.stack([pos_k[l].reshape(-1), pos_v[l].reshape(-1)])).astype(BF16)
        w1 = jnp.stack([phi_k1[l], phi_v1[l]]).astype(BF16)
        w2 = jnp.stack([phi_k2[l], phi_v2[l]])
        w2 = jnp.concatenate([w2, w2], axis=-1).astype(BF16)
        cmp_kv = _compress_call(main3, pos, w1, _group_lane_w1(w1), w2)

        sel, act, yc = _select_call(slopes, main, cmp_kv[0], cmp_kv[1], overlap, bsz, seq)
        lst, cnt = _chunk_lists(act, seq, chunk)
        y_a = _attn_call(lst, cnt, slopes, main, sel, kv, yc, block_onehot, bsz, seq, chunk)

        y_b = _pool_call(main, pool_w[l].astype(BF16), pool_scale[l], bsz, seq)

        conv_w = jnp.zeros((SUBLANES, 3 * DN_W), F32).at[:DN_CONV].set(dn_conv_w[l])
        lane_row = lambda v: jnp.zeros((1, LANES), F32).at[0, LANE_DECAY:LANE_DECAY + DN_HEADS].set(v)
        y_c = _dn_call(main3, conv_w, lane_row(dn_A_log[l]), lane_row(dn_dt_bias[l]),
                       dn_norm_g[l]).reshape(t, DN_W)

        x2 = _merge_call(y_a, y_b, y_c, main, x2, gt1,
                         w_branch_nsa[l].astype(BF16), w_branch_pool[l].astype(BF16),
                         w_branch_dn[l].astype(BF16), w_out[l].astype(BF16), seq)
        x2 = _mlp_call(x2, norm2_g[l], sc2, sh2, gt2, mlp_w1[l].astype(BF16), mlp_w2[l].astype(BF16),
                       final_g, seq, final=(l == depth - 1))
    return x2.reshape(bsz, seq, d)
```

```python
import functools

import jax
import jax.numpy as jnp
from jax import lax
from jax.experimental import pallas as pl
from jax.experimental.pallas import tpu as pltpu

F32 = jnp.float32
BF16 = jnp.bfloat16
HIGHEST = lax.Precision.HIGHEST

D_MODEL = 1024
NSA_HEADS = 8
NSA_GROUPS = 2
NSA_REP = NSA_HEADS // NSA_GROUPS
NSA_DH = 64
CMP_LEN = 32
CMP_STRIDE = 16
SEL_LEN = 64
SEL_SHIFT = 6
TOP_N = 16
WINDOW = 512
FORCE_BONUS = 1.0e4
POOL_WINDOWS = (2, 4, 8, 16)
POOL_GROUP_DIM = 128
POOL_W = len(POOL_WINDOWS) * POOL_GROUP_DIM
DN_HEADS = 4
DN_DK = 128
DN_W = DN_HEADS * DN_DK
DN_CONV = 4
EPS = 1e-6
NEG = -1e30
NSA_Q_W = NSA_HEADS * NSA_DH
NSA_KV_W = NSA_GROUPS * NSA_DH
IN_SIZES = (NSA_Q_W, NSA_KV_W, NSA_KV_W, NSA_KV_W, NSA_KV_W, NSA_KV_W, NSA_KV_W,
            3 * NSA_HEADS, POOL_W, 3 * DN_W, DN_W, DN_HEADS, DN_HEADS, 3 * D_MODEL)

LANES = 128
LANE_SHIFT = 7
SUBLANES = 8
VMEM_LIMIT_BYTES = 48 * 1024 * 1024

Q_TILE = 128
SELECT_BUCKETS = 4
ATTN_Q_TILE = 128
ATTN_TILES = 2
KV_CHUNK = 256
DN_CHUNK = 128
DN_BATCH = 4
PROJ_TM = 1024
MLP_TM = 1024
MLP_TF = 2048
MERGE_TM = 512
POOL_TS = 1024
POOL_HALO = 16
DN_HALO = 8

COL_Q = 0
COL_POOL = 512
COL_Z = 1024
COL_QKV = 1536
COL_GATE = 3072
COL_KC = 3328
COL_VC = 3456
N_MAIN = 3584
PROJ_TN = 1792
N_MG = 3072
PROJ_MG_TN = 1536
LANE_BETA = 16
LANE_DECAY = 20
KVB_KS = 0
KVB_VS = 2
KVB_KW = 4
KVB_VW = 6
KVB_END = 8
N_KV = KVB_END * LANES
LANE_POS_HI = NSA_DH
LANE_POS_LO = NSA_DH + 1


def _cparams(sem):
    return pltpu.CompilerParams(dimension_semantics=sem, vmem_limit_bytes=VMEM_LIMIT_BYTES)


def _silu(v):
    return v * jax.nn.sigmoid(v)


def _mm(a, b):
    return jnp.dot(a.astype(BF16), b.astype(BF16), preferred_element_type=F32)


def _mm_nt(a, b):
    return lax.dot_general(a.astype(BF16), b.astype(BF16), (((1,), (1,)), ((), ())), preferred_element_type=F32)


def _mm_tn(a, b):
    return lax.dot_general(a.astype(BF16), b.astype(BF16), (((0,), (0,)), ((), ())), preferred_element_type=F32)


def _mod_kernel(c_ref, w_ref, b_ref, o_ref):
    cond = _silu(c_ref[...])
    o_ref[...] = jnp.dot(cond, w_ref[...], preferred_element_type=F32, precision=HIGHEST) + b_ref[...]


def _mod_call(c_pad, w, b):
    d, n = w.shape
    tn = 1536
    return pl.pallas_call(
        _mod_kernel,
        grid=(n // tn,),
        in_specs=[pl.BlockSpec((SUBLANES, d), lambda j: (0, 0)),
                  pl.BlockSpec((d, tn), lambda j: (0, j)),
                  pl.BlockSpec((1, tn), lambda j: (0, j))],
        out_specs=pl.BlockSpec((SUBLANES, tn), lambda j: (0, j)),
        out_shape=jax.ShapeDtypeStruct((SUBLANES, n), F32),
        compiler_params=_cparams(("parallel",)),
        name="mod",
    )(c_pad, w, b.reshape(1, n))


def _norm_mod(x, g, sc, sh):
    ms = jnp.mean(x * x, axis=-1, keepdims=True)
    return (x * lax.rsqrt(ms + EPS) * g) * (1.0 + sc) + sh


def _proj_kernel(x_ref, g_ref, sc_ref, sh_ref, w_ref, o_ref, h_scr):
    @pl.when(pl.program_id(1) == 0)
    def _():
        h_scr[...] = _norm_mod(x_ref[...], g_ref[...], sc_ref[0], sh_ref[0]).astype(BF16)

    o_ref[...] = jnp.dot(h_scr[...], w_ref[...], preferred_element_type=F32).astype(o_ref.dtype)


def _proj_kv_kernel(x_ref, g_ref, sc_ref, sh_ref, w_ref, o_ref, *, tiles_per_seq):
    h = _norm_mod(x_ref[...], g_ref[...], sc_ref[0], sh_ref[0]).astype(BF16)
    acc = jnp.dot(h, w_ref[...], preferred_element_type=F32)
    tm, n = acc.shape
    col = lax.broadcasted_iota(jnp.int32, (1, n), 1)
    blk = jnp.right_shift(col, LANE_SHIFT)
    lane = col & (LANES - 1)
    is_key = ((blk >= KVB_KS) & (blk < KVB_VS)) | ((blk >= KVB_KW) & (blk < KVB_VW))
    is_val = ((blk >= KVB_VS) & (blk < KVB_KW)) | ((blk >= KVB_VW) & (blk < KVB_END))
    ones_row = jnp.where(is_val & (lane >= NSA_DH), 1.0, 0.0)
    hi_row = jnp.where(is_key & (lane == LANE_POS_HI), 1.0, 0.0)
    lo_row = jnp.where(is_key & (lane == LANE_POS_LO), 1.0, 0.0)
    pos = (pl.program_id(0) % tiles_per_seq) * tm + lax.broadcasted_iota(jnp.int32, (tm, 1), 0)
    pos_hi = jnp.right_shift(pos, LANE_SHIFT).astype(F32)
    pos_lo = (pos & (LANES - 1)).astype(F32)
    o_ref[...] = (acc + ones_row + pos_hi * hi_row + pos_lo * lo_row).astype(o_ref.dtype)


def _proj_specs(tm, d, tn, per_b):
    return [pl.BlockSpec((tm, d), lambda i, j: (i, 0)),
            pl.BlockSpec((1, d), lambda i, j: (0, 0)),
            pl.BlockSpec((1, 1, d), lambda i, j: (i // per_b, 0, 0)),
            pl.BlockSpec((1, 1, d), lambda i, j: (i // per_b, 0, 0)),
            pl.BlockSpec((d, tn), lambda i, j: (0, j))]


def _proj_call(x2, g, sc, sh, w, seq, tn, out_dtype, name):
    t, d = x2.shape
    n = w.shape[1]
    tm = min(PROJ_TM, seq)
    return pl.pallas_call(
        _proj_kernel,
        grid=(t // tm, n // tn),
        in_specs=_proj_specs(tm, d, tn, seq // tm),
        out_specs=pl.BlockSpec((tm, tn), lambda i, j: (i, j)),
        out_shape=jax.ShapeDtypeStruct((t, n), out_dtype),
        scratch_shapes=[pltpu.VMEM((tm, d), BF16)],
        compiler_params=_cparams(("parallel", "arbitrary")),
        name=name,
    )(x2, g.reshape(1, d), sc, sh, w)


def _proj_kv_call(x2, g, sc, sh, w, seq):
    t, d = x2.shape
    n = w.shape[1]
    tm = min(PROJ_TM, seq)
    return pl.pallas_call(
        functools.partial(_proj_kv_kernel, tiles_per_seq=seq // tm),
        grid=(t // tm, 1),
        in_specs=_proj_specs(tm, d, n, seq // tm),
        out_specs=pl.BlockSpec((tm, n), lambda i, j: (i, j)),
        out_shape=jax.ShapeDtypeStruct((t, n), BF16),
        compiler_params=_cparams(("parallel", "arbitrary")),
        name="proj_kv",
    )(x2, g.reshape(1, d), sc, sh, w)


def _compress_kernel(x_ref, pos_ref, w1_ref, w1g_ref, w2_ref, o_ref):
    nr = x_ref.shape[1] // CMP_STRIDE
    xs = jnp.concatenate([x_ref[0, pl.ds(rho, nr, stride=CMP_STRIDE), :].astype(BF16) for rho in range(CMP_STRIDE)],
                         axis=1)
    posb = jnp.dot(pos_ref[0], w1_ref[0], preferred_element_type=F32)[0:1]
    for g in range(NSA_GROUPS):
        top = jnp.dot(xs, w1g_ref[0, g, 0], preferred_element_type=F32)
        bot = jnp.dot(xs, w1g_ref[0, g, 1], preferred_element_type=F32)
        hid = top + pltpu.roll(bot, nr - 1, 0) + posb
        o_ref[0, 0, g] = _mm(_silu(hid), w2_ref[0]).astype(o_ref.dtype)


def _compress_call(main3, pos, w1, w1g, w2):
    b, seq, _ = main3.shape
    nr = seq // CMP_STRIDE
    two, l64, hid = w1.shape
    return pl.pallas_call(
        _compress_kernel,
        grid=(two, b),
        in_specs=[pl.BlockSpec((1, seq, LANES), lambda a, i: (i, 0, COL_KC // LANES + a)),
                  pl.BlockSpec((1, SUBLANES, l64), lambda a, i: (a, 0, 0)),
                  pl.BlockSpec((1, l64, hid), lambda a, i: (a, 0, 0)),
                  pl.BlockSpec((1,) + w1g.shape[1:], lambda a, i: (a, 0, 0, 0, 0)),
                  pl.BlockSpec((1, hid, LANES), lambda a, i: (a, 0, 0))],
        out_specs=pl.BlockSpec((1, 1, NSA_GROUPS, nr, LANES), lambda a, i: (a, i, 0, 0, 0)),
        out_shape=jax.ShapeDtypeStruct((two, b, NSA_GROUPS, nr, LANES), BF16),
        compiler_params=_cparams(("parallel", "parallel")),
        name="compress",
    )(main3, pos, w1, w1g, w2)


def _gate_cols(gate_logits, branch):
    sg = jax.nn.sigmoid(gate_logits)
    return [sg[:, 3 * r + branch:3 * r + branch + 1] for r in range(NSA_REP)]


def _stack_heads_dup(q):
    qa = q[:, :LANES]
    qb = q[:, LANES:]
    lo = lax.broadcasted_iota(jnp.int32, qa.shape, 1) < NSA_DH
    z = jnp.zeros_like(qa)
    return jnp.concatenate([jnp.where(lo, qa, z), jnp.where(lo, z, qa),
                            jnp.where(lo, qb, z), jnp.where(lo, z, qb)], axis=0)


def _unstack_heads_dup(o, gates):
    nq = o.shape[0] // NSA_REP
    lo = lax.broadcasted_iota(jnp.int32, (nq, LANES), 1) < NSA_DH
    a = jnp.where(lo, gates[0] * o[0:nq], gates[1] * o[nq:2 * nq])
    b = jnp.where(lo, gates[2] * o[2 * nq:3 * nq], gates[3] * o[3 * nq:])
    return jnp.concatenate([a, b], axis=1)


def _stack_heads_pos(q, slopes):
    qa = q[:, :LANES]
    qb = q[:, LANES:]
    lane = lax.broadcasted_iota(jnp.int32, qa.shape, 1)
    lo = lane < NSA_DH
    heads = [qa, pltpu.roll(qa, NSA_DH, 1), qb, pltpu.roll(qb, NSA_DH, 1)]
    out = []
    for r in range(NSA_REP):
        extra = jnp.where(lane == LANE_POS_HI, slopes[r] * LANES, jnp.where(lane == LANE_POS_LO, slopes[r], 0.0))
        out.append(jnp.where(lo, heads[r], extra))
    return jnp.concatenate(out, axis=0)


def _finish_heads(acc, gates):
    nq = acc.shape[0] // NSA_REP
    o = acc * pltpu.roll(1.0 / acc, NSA_DH, 1)
    lo = lax.broadcasted_iota(jnp.int32, (nq, LANES), 1) < NSA_DH
    hs = [gates[r] * o[r * nq:(r + 1) * nq] for r in range(NSA_REP)]
    a = jnp.where(lo, hs[0], pltpu.roll(hs[1], NSA_DH, 1))
    b = jnp.where(lo, hs[2], pltpu.roll(hs[3], NSA_DH, 1))
    return jnp.concatenate([a, b], axis=1)


def _select_kernel(slopes_ref, q_ref, kc_ref, vc_ref, gates_ref, ov_ref, sel_ref, act_ref, yc_ref):
    qt = pl.program_id(1)
    nqt = pl.num_programs(1)
    nb_all, ncp_all = ov_ref.shape
    for k in range(SELECT_BUCKETS):
        @pl.when((qt * SELECT_BUCKETS) // nqt == k)
        def _(k=k):
            _select_prefix(slopes_ref, q_ref, kc_ref, vc_ref, gates_ref, ov_ref, sel_ref, act_ref, yc_ref,
                           nb=(k + 1) * nb_all // SELECT_BUCKETS, ncp=(k + 1) * ncp_all // SELECT_BUCKETS)


def _select_prefix(slopes_ref, q_ref, kc_ref, vc_ref, gates_ref, ov_ref, sel_ref, act_ref, yc_ref, *, nb, ncp):
    t0 = pl.program_id(1) * Q_TILE
    nq = Q_TILE
    nb_all = ov_ref.shape[0]
    c_end = lax.broadcasted_iota(jnp.int32, (nq, ncp), 1) * CMP_STRIDE + (CMP_LEN - 1)
    tq = t0 + lax.broadcasted_iota(jnp.int32, (nq, ncp), 0)
    vis = c_end <= tq
    any_vis = tq[:, 0:1] >= CMP_LEN - 1
    rel = (c_end[0:1] - t0).astype(F32)
    ov = ov_ref[:nb, :ncp]
    jb = lax.broadcasted_iota(jnp.int32, (nb, nq), 0)
    tql = t0 + lax.broadcasted_iota(jnp.int32, (nb, nq), 1)
    valid = jb * SEL_LEN <= tql
    forced = (jb == 0) | (jb == jnp.right_shift(tql, SEL_SHIFT))
    scores = []
    for g in range(NSA_GROUPS):
        qcols = slice(g * NSA_REP * NSA_DH, (g + 1) * NSA_REP * NSA_DH)
        qs = _stack_heads_dup(q_ref[:, qcols] * (NSA_DH ** -0.5)).astype(BF16)
        s = _mm_nt(qs, kc_ref[0, g, :ncp, :])
        psum = jnp.zeros((nq, ncp), F32)
        ps = []
        for r in range(NSA_REP):
            m_r = slopes_ref[g * NSA_REP + r]
            sr = jnp.where(vis, s[r * nq:(r + 1) * nq] + m_r * rel, NEG)
            mx = jnp.max(sr, axis=-1, keepdims=True)
            e = jnp.exp(sr - mx)
            den = jnp.sum(e, axis=-1, keepdims=True)
            p = e * jnp.where(any_vis, 1.0 / den, 0.0)
            psum = psum + p
            ps.append(p.astype(BF16))
        o_c = jnp.dot(jnp.concatenate(ps, axis=0), vc_ref[0, g, :ncp, :], preferred_element_type=F32)
        yc_ref[:, qcols] = _unstack_heads_dup(o_c, _gate_cols(gates_ref[:, g * LANES:(g + 1) * LANES], 0))
        p_hi = psum.astype(BF16)
        p_lo = (psum - p_hi.astype(F32)).astype(BF16)
        imp_t = _mm_nt(ov, p_hi) + _mm_nt(ov, p_lo)
        scores.append(jnp.where(valid, imp_t + FORCE_BONUS * forced.astype(F32), NEG))

    for _ in range(min(TOP_N, nb)):
        for g in range(NSA_GROUPS):
            mx = jnp.max(scores[g], axis=0, keepdims=True)
            idx = jnp.min(jnp.where(scores[g] == mx, jb, nb), axis=0, keepdims=True)
            scores[g] = jnp.where(jb == idx, -jnp.inf, scores[g])
    for g in range(NSA_GROUPS):
        sel_t = jnp.where((scores[g] == -jnp.inf) & valid, 1.0, 0.0)
        if nb < nb_all:
            sel_t = jnp.concatenate([sel_t, jnp.zeros((nb_all - nb, nq), F32)], axis=0)
        sel_q = sel_t.T
        sel_ref[0, g] = sel_q.astype(sel_ref.dtype)
        act_ref[0, g, 0] = jnp.max(sel_q, axis=0, keepdims=True)


def _select_call(slopes, main, kcmp, vcmp, ov, bsz, seq):
    nqt = seq // Q_TILE
    nb, ncp = ov.shape
    row = lambda b, q: b * nqt + q
    gw = NSA_GROUPS * LANES
    return pl.pallas_call(
        _select_kernel,
        grid=(bsz, nqt),
        in_specs=[pl.BlockSpec(memory_space=pltpu.SMEM),
                  pl.BlockSpec((Q_TILE, NSA_Q_W), lambda b, q: (row(b, q), COL_Q // NSA_Q_W)),
                  pl.BlockSpec((1, NSA_GROUPS, ncp, LANES), lambda b, q: (b, 0, 0, 0)),
                  pl.BlockSpec((1, NSA_GROUPS, ncp, LANES), lambda b, q: (b, 0, 0, 0)),
                  pl.BlockSpec((Q_TILE, gw), lambda b, q: (row(b, q), COL_GATE // gw)),
                  pl.BlockSpec((nb, ncp), lambda b, q: (0, 0))],
        out_specs=[pl.BlockSpec((1, NSA_GROUPS, Q_TILE, nb), lambda b, q: (b, 0, q, 0)),
                   pl.BlockSpec((1, NSA_GROUPS, 1, 1, nb), lambda b, q: (b, 0, q, 0, 0)),
                   pl.BlockSpec((Q_TILE, NSA_Q_W), lambda b, q: (row(b, q), 0))],
        out_shape=[jax.ShapeDtypeStruct((bsz, NSA_GROUPS, seq, nb), BF16),
                   jax.ShapeDtypeStruct((bsz, NSA_GROUPS, nqt, 1, nb), F32),
                   jax.ShapeDtypeStruct((bsz * seq, NSA_Q_W), F32)],
        compiler_params=_cparams(("parallel", "parallel")),
        name="select",
    )(slopes, main, kcmp, vcmp, main, ov)


def _attn_kernel(lst_ref, cnt_ref, slopes_ref, q_ref, sel_ref, ks_ref, vs_ref, kw_ref, vw_ref, gates_ref, yc_ref,
                 et_ref, o_ref, s_scr, p_scr, m_scr, a_scr, acc_scr, *, chunk):
    b = pl.program_id(0)
    g = pl.program_id(1)
    step = pl.program_id(2)
    nqt = pl.num_programs(2) * ATTN_TILES
    nq = ATTN_Q_TILE
    tiles = range(ATTN_TILES)
    nch_total = ks_ref.shape[0] // chunk
    slopes = [slopes_ref[g * NSA_REP + r] for r in range(NSA_REP)]
    qts = [step * ATTN_TILES + u for u in tiles]
    t0s = [qt * nq for qt in qts]
    rws = [slice(u * nq, (u + 1) * nq) for u in tiles]
    qs = [_stack_heads_pos(q_ref[rws[u], :] * (NSA_DH ** -0.5), slopes).astype(BF16) for u in tiles]
    pens = [((sel_ref[0, 0, rws[u], :].astype(F32) - 1.0) * (-NEG)).astype(BF16) for u in tiles]
    qx = [jnp.concatenate([qs[u], jnp.concatenate([pens[u]] * NSA_REP, axis=0)], axis=1) for u in tiles]

    bases = [((b * NSA_GROUPS + g) * nqt + qts[u]) * nch_total for u in tiles]
    cnts = [cnt_ref[(b * NSA_GROUPS + g) * nqt + qts[u]] for u in tiles]

    def logits(u, c):
        start = pl.multiple_of(c * chunk, chunk)
        kx = jnp.concatenate([ks_ref[pl.ds(start, chunk), :], et_ref[pl.ds(start, chunk), :]], axis=1)
        return _mm_nt(qx[u], kx)

    def values(p, c):
        start = pl.multiple_of(c * chunk, chunk)
        return jnp.dot(p, vs_ref[pl.ds(start, chunk), :], preferred_element_type=F32)

    def row_max(s):
        return jnp.broadcast_to(jnp.max(s, axis=-1, keepdims=True), (s.shape[0], LANES))

    def lane_tile(m, width):
        return jnp.concatenate([m] * (width // LANES), axis=1)

    for u in tiles:
        m_scr[u] = jnp.full(m_scr.shape[1:], NEG, F32)
        a_scr[u] = jnp.ones(a_scr.shape[1:], F32)
        acc_scr[u] = jnp.zeros(acc_scr.shape[1:], F32)
        p_scr[u, 1] = jnp.zeros(p_scr.shape[2:], BF16)
    for u in tiles:
        s_scr[u, 0] = logits(u, lst_ref[bases[u]])

    def flash_step(us, i, slot):
        for u in us:
            acc_scr[u] = a_scr[u] * acc_scr[u] + values(p_scr[u, 1 - slot], lst_ref[bases[u] + jnp.maximum(i - 1, 0)])
        for u in us:
            s = s_scr[u, slot]
            m_prev = m_scr[u]
            m_new = jnp.maximum(m_prev, row_max(s))
            a_scr[u] = jnp.exp(m_prev - m_new)
            p_scr[u, slot] = jnp.exp(s - lane_tile(m_new, chunk)).astype(BF16)
            m_scr[u] = m_new
        for u in us:
            s_scr[u, 1 - slot] = logits(u, lst_ref[bases[u] + i + 1])

    def single_steps(us):
        def body(i, carry):
            flash_step(us, i, i & 1)
            return carry
        return body

    def double_steps(us):
        def body(j, carry):
            flash_step(us, 2 * j, 0)
            flash_step(us, 2 * j + 1, 1)
            return carry
        return body

    joint = functools.reduce(jnp.minimum, cnts)
    lax.fori_loop(0, joint // 2, double_steps(list(tiles)), 0)
    lax.fori_loop(joint - (joint & 1), joint, single_steps(list(tiles)), 0)
    for u in tiles:
        lax.fori_loop(joint, cnts[u], single_steps([u]), 0)

    slots = [cnts[u] & 1 for u in tiles]
    c_diag = [t0s[u] // chunk for u in tiles]
    acc = [a_scr[u] * acc_scr[u] + values(p_scr[u, 1 - slots[u]], lst_ref[bases[u] + jnp.maximum(cnts[u] - 1, 0)])
           for u in tiles]
    nw = WINDOW + nq
    wbase = [pl.multiple_of(jnp.maximum(t0s[u] - WINDOW, 0), nq) for u in tiles]
    s_win = [_mm_nt(qs[u], kw_ref[pl.ds(wbase[u], nw), :]) for u in tiles]

    def masked(s, ok):
        return jnp.concatenate([jnp.where(ok, s[r * nq:(r + 1) * nq], NEG) for r in range(NSA_REP)], axis=0)

    s_diag = []
    for u in tiles:
        pos = c_diag[u] * chunk + lax.broadcasted_iota(jnp.int32, (nq, chunk), 1)
        tq = t0s[u] + lax.broadcasted_iota(jnp.int32, (nq, chunk), 0)
        s_diag.append(masked(s_scr[u, slots[u]], pos <= tq))
    m_prev = [m_scr[u] for u in tiles]
    m_new = [jnp.maximum(m_prev[u], row_max(s_diag[u])) for u in tiles]
    acc = [jnp.exp(m_prev[u] - m_new[u]) * acc[u]
           + values(jnp.exp(s_diag[u] - lane_tile(m_new[u], chunk)).astype(BF16), c_diag[u]) for u in tiles]
    p_win = []
    for u in tiles:
        pos = wbase[u] + lax.broadcasted_iota(jnp.int32, (nq, nw), 1)
        tq = t0s[u] + lax.broadcasted_iota(jnp.int32, (nq, nw), 0)
        s = masked(s_win[u], (pos <= tq) & (pos > tq - WINDOW))
        p_win.append(jnp.exp(s - jnp.max(s, axis=-1, keepdims=True)).astype(BF16))
    acc_w = [jnp.dot(p_win[u], vw_ref[pl.ds(wbase[u], nw), :], preferred_element_type=F32) for u in tiles]
    for u in tiles:
        gates = gates_ref[rws[u], :]
        o_ref[rws[u], :] = (yc_ref[rws[u], :] + _finish_heads(acc[u], _gate_cols(gates, 1))
                            + _finish_heads(acc_w[u], _gate_cols(gates, 2))).astype(o_ref.dtype)


def _chunk_lists(act, seq, chunk):
    bsz, ngrp = act.shape[:2]
    nqt = seq // ATTN_Q_TILE
    nch = seq // chunk
    active = act.reshape(bsz, ngrp, nqt, ATTN_Q_TILE // Q_TILE, nch, chunk // SEL_LEN).max(axis=(3, 5)) > 0.0
    c_diag = (jnp.arange(nqt) * ATTN_Q_TILE) // chunk
    active = active & (jnp.arange(nch)[None, :] < c_diag[:, None])
    cnt = active.sum(axis=-1).astype(jnp.int32)
    order = jnp.argsort(jnp.logical_not(active), axis=-1, stable=True).astype(jnp.int32)
    lst = jnp.where(jnp.arange(nch) < cnt[..., None], order, c_diag[:, None].astype(jnp.int32))
    return lst.reshape(-1), cnt.reshape(-1)


def _attn_call(lst, cnt, slopes, main, sel, kv, yc, et, bsz, seq, chunk):
    tq = ATTN_Q_TILE * ATTN_TILES
    nqt = seq // tq
    nb = sel.shape[-1]
    rows = NSA_REP * ATTN_Q_TILE
    row = lambda b, g, q: b * nqt + q
    kvspec = lambda blk: pl.BlockSpec((seq, LANES), lambda b, g, q, *_: (b, blk + g))
    grid_spec = pltpu.PrefetchScalarGridSpec(
        num_scalar_prefetch=2,
        grid=(bsz, NSA_GROUPS, nqt),
        in_specs=[pl.BlockSpec(memory_space=pltpu.SMEM),
                  pl.BlockSpec((tq, 256), lambda b, g, q, *_: (row(b, g, q), COL_Q // 256 + g)),
                  pl.BlockSpec((1, 1, tq, nb), lambda b, g, q, *_: (b, g, q, 0)),
                  kvspec(KVB_KS), kvspec(KVB_VS), kvspec(KVB_KW), kvspec(KVB_VW),
                  pl.BlockSpec((tq, LANES), lambda b, g, q, *_: (row(b, g, q), COL_GATE // LANES + g)),
                  pl.BlockSpec((tq, 256), lambda b, g, q, *_: (row(b, g, q), g)),
                  pl.BlockSpec(et.shape, lambda b, g, q, *_: (0, 0))],
        out_specs=pl.BlockSpec((tq, 256), lambda b, g, q, *_: (row(b, g, q), g)),
        scratch_shapes=[pltpu.VMEM((ATTN_TILES, 2, rows, chunk), F32),
                        pltpu.VMEM((ATTN_TILES, 2, rows, chunk), BF16),
                        pltpu.VMEM((ATTN_TILES, rows, LANES), F32),
                        pltpu.VMEM((ATTN_TILES, rows, LANES), F32),
                        pltpu.VMEM((ATTN_TILES, rows, LANES), F32)],
    )
    return pl.pallas_call(
        functools.partial(_attn_kernel, chunk=chunk),
        grid_spec=grid_spec,
        out_shape=jax.ShapeDtypeStruct((bsz * seq, NSA_Q_W), BF16),
        compiler_params=_cparams(("parallel", "parallel", "parallel")),
        name="attn",
    )(lst, cnt, slopes, main, sel, kv, kv, kv, kv, main, yc, et)


def _pool_kernel(x_ref, halo_ref, w_ref, sc_ref, o_ref):
    s = pl.program_id(1)
    ts = x_ref.shape[0]
    x = x_ref[...]
    halo = jnp.where(s > 0, halo_ref[...], 0.0)
    t = s * ts + lax.broadcasted_iota(jnp.int32, (ts, 1), 0)
    for gi, win in enumerate(POOL_WINDOWS):
        lanes = slice(gi * POOL_GROUP_DIM, (gi + 1) * POOL_GROUP_DIM)
        ext = jnp.concatenate([halo[:, lanes], x[:, lanes]], axis=0)
        acc = ext
        span = 1
        while span < win:
            acc = acc + pltpu.roll(acc, span, 0)
            span *= 2
        cnt = jnp.minimum(t + 1, win).astype(F32)
        y = acc[POOL_HALO:] / cnt - x[:, lanes]
        o_ref[:, lanes] = (_mm(y, w_ref[gi]) * sc_ref[:, lanes]).astype(o_ref.dtype)


def _pool_call(main, w, scale, bsz, seq):
    ts = min(POOL_TS, seq)
    per_b = seq // ts
    halo_per_tile = ts // POOL_HALO
    return pl.pallas_call(
        _pool_kernel,
        grid=(bsz, per_b),
        in_specs=[pl.BlockSpec((ts, POOL_W), lambda b, s: (b * per_b + s, COL_POOL // POOL_W)),
                  pl.BlockSpec((POOL_HALO, POOL_W),
                               lambda b, s: (jnp.maximum((b * per_b + s) * halo_per_tile - 1, 0), COL_POOL // POOL_W)),
                  pl.BlockSpec(w.shape, lambda b, s: (0, 0, 0)),
                  pl.BlockSpec((1, POOL_W), lambda b, s: (0, 0))],
        out_specs=pl.BlockSpec((ts, POOL_W), lambda b, s: (b * per_b + s, 0)),
        out_shape=jax.ShapeDtypeStruct((bsz * seq, POOL_W), BF16),
        compiler_params=_cparams(("parallel", "parallel")),
        name="pool",
    )(main, main, w, scale.reshape(1, POOL_W))


def _unit_lower_inverses(nms, ri, ci):
    c = nms[0].shape[0]
    eye = jnp.where(ri == ci, 1.0, 0.0)
    in8 = jnp.right_shift(ri, 3) == jnp.right_shift(ci, 3)
    n8 = [jnp.where(in8, nm, 0.0) for nm in nms]
    ps = [eye - a for a in n8]
    ms = [_mm(a, a) for a in n8]
    ps = [p + _mm(p, m) for p, m in zip(ps, ms)]
    ms = [_mm(m, m) for m in ms]
    ps = [p + _mm(p, m) for p, m in zip(ps, ms)]
    shift = 3
    while (1 << shift) < c:
        rb = jnp.right_shift(ri, shift)
        cb = jnp.right_shift(ci, shift)
        lower_left = ((rb & 1) == 1) & (cb == rb - 1)
        ts = [_mm(p, jnp.where(lower_left, nm, 0.0)) for p, nm in zip(ps, nms)]
        ps = [p - _mm(t, p) for p, t in zip(ps, ts)]
        shift += 1
    return ps


def _row_cumsum(x, ri):
    span = 1
    while span < x.shape[0]:
        x = x + jnp.where(ri >= span, pltpu.roll(x, span, 0), 0.0)
        span *= 2
    return x


def _dn_kernel(qkv_ref, halo_ref, z_ref, gba_ref, cw_ref, alog_ref, dtb_ref, ng_ref, o_ref, st_scr):
    s = pl.program_id(1)
    nbatch, c = qkv_ref.shape[0], qkv_ref.shape[1]

    @pl.when(s == 0)
    def _():
        st_scr[...] = jnp.zeros(st_scr.shape, F32)

    ri = lax.broadcasted_iota(jnp.int32, (c, c), 0)
    ci = lax.broadcasted_iota(jnp.int32, (c, c), 1)
    tril = ri >= ci
    strict = ri > ci
    cw = cw_ref[...]
    ng = ng_ref[...]
    chains = [(bi, h) for bi in range(nbatch) for h in range(DN_HEADS)]
    qn, kn, kb, vb, gcb, lmask = [], [], [], [], [], []
    for bi in range(nbatch):
        halo = jnp.where(s > 0, halo_ref[bi], 0.0)
        ext = jnp.concatenate([halo, qkv_ref[bi]], axis=0)
        conv = cw[DN_CONV - 1:DN_CONV] * ext
        for tap in range(1, DN_CONV):
            conv = conv + cw[DN_CONV - 1 - tap:DN_CONV - tap] * pltpu.roll(ext, tap, 0)
        act = _silu(conv[DN_HALO:])

        gba = gba_ref[bi]
        beta_all = jax.nn.sigmoid(gba)
        xs = gba + dtb_ref[...]
        softplus = jnp.maximum(xs, 0.0) + jnp.log(1.0 + jnp.exp(-jnp.abs(xs)))
        gc_all = _row_cumsum(-jnp.exp(alog_ref[...]) * softplus, ri)
        gct_all = gc_all.T
        for h in range(DN_HEADS):
            q = act[:, h * DN_DK:(h + 1) * DN_DK]
            k = act[:, DN_W + h * DN_DK:DN_W + (h + 1) * DN_DK]
            v = act[:, 2 * DN_W + h * DN_DK:2 * DN_W + (h + 1) * DN_DK]
            qn.append(q * lax.rsqrt(jnp.sum(q * q, axis=-1, keepdims=True) + EPS) * (DN_DK ** -0.5))
            kn.append(k * lax.rsqrt(jnp.sum(k * k, axis=-1, keepdims=True) + EPS))
            beta = beta_all[:, LANE_BETA + h:LANE_BETA + h + 1]
            gcb.append(jnp.broadcast_to(gc_all[:, LANE_DECAY + h:LANE_DECAY + h + 1], (c, DN_DK)))
            gct = jnp.broadcast_to(gct_all[LANE_DECAY + h:LANE_DECAY + h + 1, :], (c, c))
            lmask.append(jnp.where(tril, jnp.exp(jnp.where(tril, gcb[-1] - gct, 0.0)), 0.0))
            kb.append(kn[-1] * beta)
            vb.append(v * beta)

    n = range(len(chains))
    kk = [_mm_nt(kb[i], kn[i]) for i in n]
    tinv = _unit_lower_inverses([jnp.where(strict, kk[i] * lmask[i], 0.0) for i in n], ri, ci)
    egc = [jnp.exp(gcb[i]) for i in n]
    sol = [_mm(tinv[i], jnp.concatenate([vb[i], kb[i] * egc[i]], axis=1)) for i in n]
    aqk = [_mm_nt(qn[i], kn[i]) * lmask[i] for i in n]
    state = [st_scr[i] for i in n]
    glast = [gcb[i][c - 1:c, :] for i in n]
    v_new = [sol[i][:, :DN_DK] - _mm(sol[i][:, DN_DK:], state[i]) for i in n]
    o_state = [_mm(qn[i] * egc[i], state[i]) for i in n]
    o = [o_state[i] + _mm(aqk[i], v_new[i]) for i in n]
    upd = [_mm_tn(kn[i] * jnp.exp(glast[i] - gcb[i]), v_new[i]) for i in n]
    for i, (bi, h) in enumerate(chains):
        st_scr[i] = state[i] * jnp.exp(glast[i]) + upd[i]
        on = o[i] * lax.rsqrt(jnp.mean(o[i] * o[i], axis=-1, keepdims=True) + EPS) * ng
        o_ref[bi, :, h * DN_DK:(h + 1) * DN_DK] = (
            on * _silu(z_ref[bi, :, h * DN_DK:(h + 1) * DN_DK])).astype(o_ref.dtype)


def _dn_call(main3, conv_w, alog_row, dtb_row, norm_g):
    bsz, seq, _ = main3.shape
    c = DN_CHUNK
    nbt = DN_BATCH if bsz % DN_BATCH == 0 else 1
    halo_per_chunk = c // DN_HALO
    w3 = 3 * DN_W
    return pl.pallas_call(
        _dn_kernel,
        grid=(bsz // nbt, seq // c),
        in_specs=[pl.BlockSpec((nbt, c, w3), lambda b, s: (b, s, COL_QKV // w3)),
                  pl.BlockSpec((nbt, DN_HALO, w3),
                               lambda b, s: (b, jnp.maximum(s * halo_per_chunk - 1, 0), COL_QKV // w3)),
                  pl.BlockSpec((nbt, c, DN_W), lambda b, s: (b, s, COL_Z // DN_W)),
                  pl.BlockSpec((nbt, c, LANES), lambda b, s: (b, s, COL_GATE // LANES)),
                  pl.BlockSpec((SUBLANES, w3), lambda b, s: (0, 0)),
                  pl.BlockSpec((1, LANES), lambda b, s: (0, 0)),
                  pl.BlockSpec((1, LANES), lambda b, s: (0, 0)),
                  pl.BlockSpec((1, DN_DK), lambda b, s: (0, 0))],
        out_specs=pl.BlockSpec((nbt, c, DN_W), lambda b, s: (b, s, 0)),
        out_shape=jax.ShapeDtypeStruct((bsz, seq, DN_W), BF16),
        scratch_shapes=[pltpu.VMEM((nbt * DN_HEADS, DN_DK, DN_DK), F32)],
        compiler_params=_cparams(("parallel", "arbitrary")),
        name="dn",
    )(main3, main3, main3, main3, conv_w, alog_row, dtb_row, norm_g.reshape(1, DN_DK))


def _merge_kernel(ya_ref, yb_ref, yc_ref, mg_ref, x_ref, gt_ref, wa_ref, wb_ref, wc_ref, wo_ref, o_ref):
    d = x_ref.shape[1]
    gates = jax.nn.sigmoid(mg_ref[...].astype(F32))
    merged = (gates[:, :d] * jnp.dot(ya_ref[...], wa_ref[...], preferred_element_type=F32)
              + gates[:, d:2 * d] * jnp.dot(yb_ref[...], wb_ref[...], preferred_element_type=F32)
              + gates[:, 2 * d:] * jnp.dot(yc_ref[...], wc_ref[...], preferred_element_type=F32))
    y = jnp.dot(merged.astype(BF16), wo_ref[...], preferred_element_type=F32)
    o_ref[...] = x_ref[...] + gt_ref[0] * y


def _merge_call(ya, yb, yc, mg, x2, gt, wa, wb, wc, wo, seq):
    t, d = x2.shape
    tm = min(MERGE_TM, seq)
    per_b = seq // tm
    full = lambda a: pl.BlockSpec(a.shape, lambda i: (0, 0))
    return pl.pallas_call(
        _merge_kernel,
        grid=(t // tm,),
        in_specs=[pl.BlockSpec((tm, ya.shape[1]), lambda i: (i, 0)),
                  pl.BlockSpec((tm, yb.shape[1]), lambda i: (i, 0)),
                  pl.BlockSpec((tm, yc.shape[1]), lambda i: (i, 0)),
                  pl.BlockSpec((tm, mg.shape[1]), lambda i: (i, 0)),
                  pl.BlockSpec((tm, d), lambda i: (i, 0)),
                  pl.BlockSpec((1, 1, d), lambda i: (i // per_b, 0, 0)),
                  full(wa), full(wb), full(wc), full(wo)],
        out_specs=pl.BlockSpec((tm, d), lambda i: (i, 0)),
        out_shape=jax.ShapeDtypeStruct((t, d), F32),
        compiler_params=_cparams(("parallel",)),
        name="merge",
    )(ya, yb, yc, mg, x2, gt, wa, wb, wc, wo)


def _mlp_kernel(x_ref, g_ref, sc_ref, sh_ref, gt_ref, w1_ref, w2_ref, fg_ref, o_ref, h_scr, acc_scr, *, final):
    j = pl.program_id(1)

    @pl.when(j == 0)
    def _():
        h_scr[...] = _norm_mod(x_ref[...], g_ref[...], sc_ref[0], sh_ref[0]).astype(BF16)
        acc_scr[...] = jnp.zeros(acc_scr.shape, F32)

    a = jnp.maximum(jnp.dot(h_scr[...], w1_ref[...], preferred_element_type=F32), 0.0)
    acc_scr[...] += jnp.dot((a * a).astype(BF16), w2_ref[...], preferred_element_type=F32)

    @pl.when(j == pl.num_programs(1) - 1)
    def _():
        y = x_ref[...] + gt_ref[0] * acc_scr[...]
        if final:
            ms = jnp.mean(y * y, axis=-1, keepdims=True)
            y = y * lax.rsqrt(ms + EPS) * fg_ref[...]
        o_ref[...] = y


def _mlp_call(x2, g, sc, sh, gt, w1, w2, fg, seq, final):
    t, d = x2.shape
    f = w1.shape[1]
    tm = min(MLP_TM, seq)
    tf = MLP_TF
    per_b = seq // tm
    mod = lambda: pl.BlockSpec((1, 1, d), lambda i, j: (i // per_b, 0, 0))
    return pl.pallas_call(
        functools.partial(_mlp_kernel, final=final),
        grid=(t // tm, f // tf),
        in_specs=[pl.BlockSpec((tm, d), lambda i, j: (i, 0)),
                  pl.BlockSpec((1, d), lambda i, j: (0, 0)),
                  mod(), mod(), mod(),
                  pl.BlockSpec((d, tf), lambda i, j: (0, j)),
                  pl.BlockSpec((tf, d), lambda i, j: (j, 0)),
                  pl.BlockSpec((1, d), lambda i, j: (0, 0))],
        out_specs=pl.BlockSpec((tm, d), lambda i, j: (i, 0)),
        out_shape=jax.ShapeDtypeStruct((t, d), F32),
        scratch_shapes=[pltpu.VMEM((tm, d), BF16), pltpu.VMEM((tm, d), F32)],
        compiler_params=_cparams(("parallel", "arbitrary")),
        name="mlp",
    )(x2, g.reshape(1, d), sc, sh, gt, w1, w2, fg.reshape(1, d))


def _split_w_in(w_in):
    offs = [0]
    for sz in IN_SIZES:
        offs.append(offs[-1] + sz)
    return [w_in[:, offs[i]:offs[i + 1]] for i in range(len(IN_SIZES))]


def _layout_w_in(w_in):
    nq, nkc, nvc, nks, nvs, nkw, nvw, ngate, pin, dqkv, dz, dbeta, da, mg = _split_w_in(w_in)
    d = w_in.shape[0]
    per_g = 3 * NSA_REP
    zeros = lambda n: jnp.zeros((d, n), w_in.dtype)
    gate0 = jnp.concatenate([ngate[:, :per_g], zeros(LANE_BETA - per_g), dbeta, da,
                             zeros(LANES - LANE_DECAY - DN_HEADS)], axis=1)
    gate1 = jnp.concatenate([ngate[:, per_g:], zeros(LANES - per_g)], axis=1)
    main = jnp.concatenate([nq, pin, dz, dqkv, gate0, gate1, nkc, nvc], axis=1)

    def per_group(w):
        pad = zeros(LANES - NSA_DH)
        return jnp.concatenate([w[:, :NSA_DH], pad, w[:, NSA_DH:], pad], axis=1)

    kv = jnp.concatenate([per_group(nks), per_group(nvs), per_group(nkw), per_group(nvw)], axis=1)
    return main.astype(BF16), mg.astype(BF16), kv.astype(BF16)


def _group_lane_w1(w1):
    two, _, hid = w1.shape
    halves = w1.reshape(two, 2, CMP_STRIDE, NSA_DH, hid)
    out = []
    for g in range(NSA_GROUPS):
        padded = jnp.zeros((two, 2, CMP_STRIDE, LANES, hid), w1.dtype).at[:, :, :, g * NSA_DH:(g + 1) * NSA_DH].set(halves)
        out.append(padded.reshape(two, 2, CMP_STRIDE * LANES, hid))
    return jnp.stack(out, axis=1)


def _nsa_constants(seq):
    nb = seq // SEL_LEN
    ncp = seq // CMP_STRIDE
    j = jnp.arange(nb)[:, None] * SEL_LEN
    i = jnp.arange(ncp)[None, :] * CMP_STRIDE
    overlap = jnp.clip(jnp.minimum(j + SEL_LEN, i + CMP_LEN) - jnp.maximum(j, i), 0) // CMP_STRIDE
    block_onehot = (jnp.arange(seq)[:, None] // SEL_LEN) == jnp.arange(nb)[None, :]
    return overlap.astype(BF16), block_onehot.astype(BF16)


def kernel(x, c, ada_w, ada_b, norm1_g, norm2_g, w_in, phi_k1, phi_k2, phi_v1, phi_v2, pos_k, pos_v, pool_w, pool_scale, dn_conv_w, dn_A_log, dn_dt_bias, dn_norm_g, w_branch_nsa, w_branch_pool, w_branch_dn, w_out, mlp_w1, mlp_w2, final_g):
    bsz, seq, d = x.shape
    assert d == D_MODEL and seq % max(PROJ_TM, MLP_TM, POOL_TS) == 0 and seq % KV_CHUNK == 0
    assert DN_CHUNK == DN_DK == LANES
    t = bsz * seq
    depth = ada_w.shape[0]
    chunk = KV_CHUNK
    nqt = seq // Q_TILE
    nb = seq // SEL_LEN
    blocks_per_chunk = chunk // SEL_LEN

    slopes = 2.0 ** (-(8.0 / NSA_HEADS) * (jnp.arange(NSA_HEADS, dtype=F32) + 1.0))
    overlap, block_onehot = _nsa_constants(seq)
    c_pad = jnp.zeros((SUBLANES, d), F32).at[:bsz].set(c)
    x2 = x.reshape(t, d)

    for l in range(depth):
        mod = _mod_call(c_pad, ada_w[l], ada_b[l])[:bsz]
        sh1, sc1, gt1, sh2, sc2, gt2 = [m.reshape(bsz, 1, d) for m in jnp.split(mod, 6, axis=-1)]

        w_main, w_mg, w_kv = _layout_w_in(w_in[l])
        main = _proj_call(x2, norm1_g[l], sc1, sh1, w_main, seq, PROJ_TN, F32, "proj_main")
        mg = _proj_call(x2, norm1_g[l], sc1, sh1, w_mg, seq, PROJ_MG_TN, BF16, "proj_mg")
        kv = _proj_kv_call(x2, norm1_g[l], sc1, sh1, w_kv, seq)

        main3 = main.reshape(bsz, seq, N_MAIN)
        pos = jnp.zeros((2, SUBLANES, CMP_LEN * NSA_DH), F32).at[:, 0].set(
            jnp.stack([pos_k[l].reshape(-1), pos_v[l].reshape(-1)])).astype(BF16)
        w1 = jnp.stack([phi_k1[l], phi_v1[l]]).astype(BF16)
        w2 = jnp.stack([phi_k2[l], phi_v2[l]])
        w2 = jnp.concatenate([w2, w2], axis=-1).astype(BF16)
        cmp_kv = _compress_call(main3, pos, w1, _group_lane_w1(w1), w2)

        sel, act, yc = _select_call(slopes, main, cmp_kv[0], cmp_kv[1], overlap, bsz, seq)
        lst, cnt = _chunk_lists(act, seq, chunk)
        y_a = _attn_call(lst, cnt, slopes, main, sel, kv, yc, block_onehot, bsz, seq, chunk)

        y_b = _pool_call(main, pool_w[l].astype(BF16), pool_scale[l], bsz, seq)

        conv_w = jnp.zeros((SUBLANES, 3 * DN_W), F32).at[:DN_CONV].set(dn_conv_w[l])
        lane_row = lambda v: jnp.zeros((1, LANES), F32).at[0, LANE_DECAY:LANE_DECAY + DN_HEADS].set(v)
        y_c = _dn_call(main3, conv_w, lane_row(dn_A_log[l]), lane_row(dn_dt_bias[l]),
                       dn_norm_g[l]).reshape(t, DN_W)

        x2 = _merge_call(y_a, y_b, y_c, mg, x2, gt1,
                         w_branch_nsa[l].astype(BF16), w_branch_pool[l].astype(BF16),
                         w_branch_dn[l].astype(BF16), w_out[l].astype(BF16), seq)
        x2 = _mlp_call(x2, norm2_g[l], sc2, sh2, gt2, mlp_w1[l].astype(BF16), mlp_w2[l].astype(BF16),
                       final_g, seq, final=(l == depth - 1))
    return x2.reshape(bsz, seq, d)
```

```python
import functools

import jax
import jax.numpy as jnp
from jax import lax
from jax.experimental import pallas as pl
from jax.experimental.pallas import tpu as pltpu

F32 = jnp.float32
BF16 = jnp.bfloat16
HIGHEST = lax.Precision.HIGHEST

D_MODEL = 1024
NSA_HEADS = 8
NSA_GROUPS = 2
NSA_REP = NSA_HEADS // NSA_GROUPS
NSA_DH = 64
CMP_LEN = 32
CMP_STRIDE = 16
SEL_LEN = 64
SEL_SHIFT = 6
TOP_N = 16
WINDOW = 512
FORCE_BONUS = 1.0e4
POOL_WINDOWS = (2, 4, 8, 16)
POOL_GROUP_DIM = 128
POOL_W = len(POOL_WINDOWS) * POOL_GROUP_DIM
DN_HEADS = 4
DN_DK = 128
DN_W = DN_HEADS * DN_DK
DN_CONV = 4
EPS = 1e-6
NEG = -1e30
NSA_Q_W = NSA_HEADS * NSA_DH
NSA_KV_W = NSA_GROUPS * NSA_DH
IN_SIZES = (NSA_Q_W, NSA_KV_W, NSA_KV_W, NSA_KV_W, NSA_KV_W, NSA_KV_W, NSA_KV_W,
            3 * NSA_HEADS, POOL_W, 3 * DN_W, DN_W, DN_HEADS, DN_HEADS, 3 * D_MODEL)

LANES = 128
LANE_SHIFT = 7
SUBLANES = 8
VMEM_LIMIT_BYTES = 48 * 1024 * 1024

Q_TILE = 128
SELECT_BUCKETS = 4
ATTN_Q_TILE = 128
ATTN_TILES = 2
KV_CHUNK = 256
DN_CHUNK = 128
DN_BATCH = 4
PROJ_TM = 1024
MLP_TM = 1024
MLP_TF = 2048
MERGE_TM = 512
POOL_TS = 1024
POOL_HALO = 16
DN_HALO = 8

COL_Q = 0
COL_POOL = 512
COL_Z = 1024
COL_QKV = 1536
COL_GATE = 3072
COL_KC = 3328
COL_VC = 3456
N_MAIN = 3584
PROJ_MAIN_TM = 512
N_MG = 3072
LANE_BETA = 16
LANE_DECAY = 20
KVB_KS = 0
KVB_VS = 2
KVB_KW = 4
KVB_VW = 6
KVB_END = 8
N_KV = KVB_END * LANES
LANE_POS_HI = NSA_DH
LANE_POS_LO = NSA_DH + 1


def _cparams(sem):
    return pltpu.CompilerParams(dimension_semantics=sem, vmem_limit_bytes=VMEM_LIMIT_BYTES)


def _silu(v):
    return v * jax.nn.sigmoid(v)


def _mm(a, b):
    return jnp.dot(a.astype(BF16), b.astype(BF16), preferred_element_type=F32)


def _mm_nt(a, b):
    return lax.dot_general(a.astype(BF16), b.astype(BF16), (((1,), (1,)), ((), ())), preferred_element_type=F32)


def _mm_tn(a, b):
    return lax.dot_general(a.astype(BF16), b.astype(BF16), (((0,), (0,)), ((), ())), preferred_element_type=F32)


def _mod_kernel(c_ref, w_ref, b_ref, o_ref):
    cond = _silu(c_ref[...])
    o_ref[...] = jnp.dot(cond, w_ref[...], preferred_element_type=F32, precision=HIGHEST) + b_ref[...]


def _mod_call(c_pad, w, b):
    d, n = w.shape
    tn = 1536
    return pl.pallas_call(
        _mod_kernel,
        grid=(n // tn,),
        in_specs=[pl.BlockSpec((SUBLANES, d), lambda j: (0, 0)),
                  pl.BlockSpec((d, tn), lambda j: (0, j)),
                  pl.BlockSpec((1, tn), lambda j: (0, j))],
        out_specs=pl.BlockSpec((SUBLANES, tn), lambda j: (0, j)),
        out_shape=jax.ShapeDtypeStruct((SUBLANES, n), F32),
        compiler_params=_cparams(("parallel",)),
        name="mod",
    )(c_pad, w, b.reshape(1, n))


def _norm_mod(x, g, sc, sh):
    ms = jnp.mean(x * x, axis=-1, keepdims=True)
    return (x * lax.rsqrt(ms + EPS) * g) * (1.0 + sc) + sh


def _proj_kernel(x_ref, g_ref, sc_ref, sh_ref, w_ref, o_ref):
    h = _norm_mod(x_ref[...], g_ref[...], sc_ref[0], sh_ref[0]).astype(BF16)
    o_ref[...] = jnp.dot(h, w_ref[...], preferred_element_type=F32).astype(o_ref.dtype)


def _proj_kv_kernel(x_ref, g_ref, sc_ref, sh_ref, w_ref, o_ref, *, tiles_per_seq):
    h = _norm_mod(x_ref[...], g_ref[...], sc_ref[0], sh_ref[0]).astype(BF16)
    acc = jnp.dot(h, w_ref[...], preferred_element_type=F32)
    tm, n = acc.shape
    col = lax.broadcasted_iota(jnp.int32, (1, n), 1)
    blk = jnp.right_shift(col, LANE_SHIFT)
    lane = col & (LANES - 1)
    is_key = ((blk >= KVB_KS) & (blk < KVB_VS)) | ((blk >= KVB_KW) & (blk < KVB_VW))
    is_val = ((blk >= KVB_VS) & (blk < KVB_KW)) | ((blk >= KVB_VW) & (blk < KVB_END))
    ones_row = jnp.where(is_val & (lane >= NSA_DH), 1.0, 0.0)
    hi_row = jnp.where(is_key & (lane == LANE_POS_HI), 1.0, 0.0)
    lo_row = jnp.where(is_key & (lane == LANE_POS_LO), 1.0, 0.0)
    pos = (pl.program_id(0) % tiles_per_seq) * tm + lax.broadcasted_iota(jnp.int32, (tm, 1), 0)
    pos_hi = jnp.right_shift(pos, LANE_SHIFT).astype(F32)
    pos_lo = (pos & (LANES - 1)).astype(F32)
    o_ref[...] = (acc + ones_row + pos_hi * hi_row + pos_lo * lo_row).astype(o_ref.dtype)


def _proj_specs(tm, d, tn, per_b):
    return [pl.BlockSpec((tm, d), lambda i, j: (i, 0)),
            pl.BlockSpec((1, d), lambda i, j: (0, 0)),
            pl.BlockSpec((1, 1, d), lambda i, j: (i // per_b, 0, 0)),
            pl.BlockSpec((1, 1, d), lambda i, j: (i // per_b, 0, 0)),
            pl.BlockSpec((d, tn), lambda i, j: (0, j))]


def _proj_call(x2, g, sc, sh, w, seq, tm, out_dtype, name):
    t, d = x2.shape
    n = w.shape[1]
    tm = min(tm, seq)
    return pl.pallas_call(
        _proj_kernel,
        grid=(t // tm, 1),
        in_specs=_proj_specs(tm, d, n, seq // tm),
        out_specs=pl.BlockSpec((tm, n), lambda i, j: (i, j)),
        out_shape=jax.ShapeDtypeStruct((t, n), out_dtype),
        compiler_params=_cparams(("parallel", "arbitrary")),
        name=name,
    )(x2, g.reshape(1, d), sc, sh, w)


def _proj_kv_call(x2, g, sc, sh, w, seq):
    t, d = x2.shape
    n = w.shape[1]
    tm = min(PROJ_TM, seq)
    return pl.pallas_call(
        functools.partial(_proj_kv_kernel, tiles_per_seq=seq // tm),
        grid=(t // tm, 1),
        in_specs=_proj_specs(tm, d, n, seq // tm),
        out_specs=pl.BlockSpec((tm, n), lambda i, j: (i, j)),
        out_shape=jax.ShapeDtypeStruct((t, n), BF16),
        compiler_params=_cparams(("parallel", "arbitrary")),
        name="proj_kv",
    )(x2, g.reshape(1, d), sc, sh, w)


def _compress_kernel(x_ref, pos_ref, w1_ref, w1g_ref, w2_ref, o_ref):
    nr = x_ref.shape[1] // CMP_STRIDE
    xs = jnp.concatenate([x_ref[0, pl.ds(rho, nr, stride=CMP_STRIDE), :].astype(BF16) for rho in range(CMP_STRIDE)],
                         axis=1)
    posb = jnp.dot(pos_ref[0], w1_ref[0], preferred_element_type=F32)[0:1]
    for g in range(NSA_GROUPS):
        top = jnp.dot(xs, w1g_ref[0, g, 0], preferred_element_type=F32)
        bot = jnp.dot(xs, w1g_ref[0, g, 1], preferred_element_type=F32)
        hid = top + pltpu.roll(bot, nr - 1, 0) + posb
        o_ref[0, 0, g] = _mm(_silu(hid), w2_ref[0]).astype(o_ref.dtype)


def _compress_call(main3, pos, w1, w1g, w2):
    b, seq, _ = main3.shape
    nr = seq // CMP_STRIDE
    two, l64, hid = w1.shape
    return pl.pallas_call(
        _compress_kernel,
        grid=(two, b),
        in_specs=[pl.BlockSpec((1, seq, LANES), lambda a, i: (i, 0, COL_KC // LANES + a)),
                  pl.BlockSpec((1, SUBLANES, l64), lambda a, i: (a, 0, 0)),
                  pl.BlockSpec((1, l64, hid), lambda a, i: (a, 0, 0)),
                  pl.BlockSpec((1,) + w1g.shape[1:], lambda a, i: (a, 0, 0, 0, 0)),
                  pl.BlockSpec((1, hid, LANES), lambda a, i: (a, 0, 0))],
        out_specs=pl.BlockSpec((1, 1, NSA_GROUPS, nr, LANES), lambda a, i: (a, i, 0, 0, 0)),
        out_shape=jax.ShapeDtypeStruct((two, b, NSA_GROUPS, nr, LANES), BF16),
        compiler_params=_cparams(("parallel", "parallel")),
        name="compress",
    )(main3, pos, w1, w1g, w2)


def _gate_cols(gate_logits, branch):
    sg = jax.nn.sigmoid(gate_logits)
    return [sg[:, 3 * r + branch:3 * r + branch + 1] for r in range(NSA_REP)]


def _stack_heads_dup(q):
    qa = q[:, :LANES]
    qb = q[:, LANES:]
    lo = lax.broadcasted_iota(jnp.int32, qa.shape, 1) < NSA_DH
    z = jnp.zeros_like(qa)
    return jnp.concatenate([jnp.where(lo, qa, z), jnp.where(lo, z, qa),
                            jnp.where(lo, qb, z), jnp.where(lo, z, qb)], axis=0)


def _unstack_heads_dup(o, gates):
    nq = o.shape[0] // NSA_REP
    lo = lax.broadcasted_iota(jnp.int32, (nq, LANES), 1) < NSA_DH
    a = jnp.where(lo, gates[0] * o[0:nq], gates[1] * o[nq:2 * nq])
    b = jnp.where(lo, gates[2] * o[2 * nq:3 * nq], gates[3] * o[3 * nq:])
    return jnp.concatenate([a, b], axis=1)


def _stack_heads_pos(q, slopes):
    qa = q[:, :LANES]
    qb = q[:, LANES:]
    lane = lax.broadcasted_iota(jnp.int32, qa.shape, 1)
    lo = lane < NSA_DH
    heads = [qa, pltpu.roll(qa, NSA_DH, 1), qb, pltpu.roll(qb, NSA_DH, 1)]
    out = []
    for r in range(NSA_REP):
        extra = jnp.where(lane == LANE_POS_HI, slopes[r] * LANES, jnp.where(lane == LANE_POS_LO, slopes[r], 0.0))
        out.append(jnp.where(lo, heads[r], extra))
    return jnp.concatenate(out, axis=0)


def _finish_heads(acc, gates):
    nq = acc.shape[0] // NSA_REP
    o = acc * pltpu.roll(1.0 / acc, NSA_DH, 1)
    lo = lax.broadcasted_iota(jnp.int32, (nq, LANES), 1) < NSA_DH
    hs = [gates[r] * o[r * nq:(r + 1) * nq] for r in range(NSA_REP)]
    a = jnp.where(lo, hs[0], pltpu.roll(hs[1], NSA_DH, 1))
    b = jnp.where(lo, hs[2], pltpu.roll(hs[3], NSA_DH, 1))
    return jnp.concatenate([a, b], axis=1)


def _select_kernel(slopes_ref, q_ref, kc_ref, vc_ref, gates_ref, ov_ref, sel_ref, act_ref, yc_ref):
    qt = pl.program_id(1)
    nqt = pl.num_programs(1)
    nb_all, ncp_all = ov_ref.shape
    for k in range(SELECT_BUCKETS):
        @pl.when((qt * SELECT_BUCKETS) // nqt == k)
        def _(k=k):
            _select_prefix(slopes_ref, q_ref, kc_ref, vc_ref, gates_ref, ov_ref, sel_ref, act_ref, yc_ref,
                           nb=(k + 1) * nb_all // SELECT_BUCKETS, ncp=(k + 1) * ncp_all // SELECT_BUCKETS)


def _select_prefix(slopes_ref, q_ref, kc_ref, vc_ref, gates_ref, ov_ref, sel_ref, act_ref, yc_ref, *, nb, ncp):
    t0 = pl.program_id(1) * Q_TILE
    nq = Q_TILE
    nb_all = ov_ref.shape[0]
    c_end = lax.broadcasted_iota(jnp.int32, (nq, ncp), 1) * CMP_STRIDE + (CMP_LEN - 1)
    tq = t0 + lax.broadcasted_iota(jnp.int32, (nq, ncp), 0)
    vis = c_end <= tq
    any_vis = tq[:, 0:1] >= CMP_LEN - 1
    rel = (c_end[0:1] - t0).astype(F32)
    ov = ov_ref[:nb, :ncp]
    jb = lax.broadcasted_iota(jnp.int32, (nb, nq), 0)
    tql = t0 + lax.broadcasted_iota(jnp.int32, (nb, nq), 1)
    valid = jb * SEL_LEN <= tql
    forced = (jb == 0) | (jb == jnp.right_shift(tql, SEL_SHIFT))
    scores = []
    for g in range(NSA_GROUPS):
        qcols = slice(g * NSA_REP * NSA_DH, (g + 1) * NSA_REP * NSA_DH)
        qs = _stack_heads_dup(q_ref[:, qcols] * (NSA_DH ** -0.5)).astype(BF16)
        s = _mm_nt(qs, kc_ref[0, g, :ncp, :])
        psum = jnp.zeros((nq, ncp), F32)
        ps = []
        for r in range(NSA_REP):
            m_r = slopes_ref[g * NSA_REP + r]
            sr = jnp.where(vis, s[r * nq:(r + 1) * nq] + m_r * rel, NEG)
            mx = jnp.max(sr, axis=-1, keepdims=True)
            e = jnp.exp(sr - mx)
            den = jnp.sum(e, axis=-1, keepdims=True)
            p = e * jnp.where(any_vis, 1.0 / den, 0.0)
            psum = psum + p
            ps.append(p.astype(BF16))
        o_c = jnp.dot(jnp.concatenate(ps, axis=0), vc_ref[0, g, :ncp, :], preferred_element_type=F32)
        yc_ref[:, qcols] = _unstack_heads_dup(o_c, _gate_cols(gates_ref[:, g * LANES:(g + 1) * LANES], 0))
        p_hi = psum.astype(BF16)
        p_lo = (psum - p_hi.astype(F32)).astype(BF16)
        imp_t = _mm_nt(ov, p_hi) + _mm_nt(ov, p_lo)
        scores.append(jnp.where(valid, imp_t + FORCE_BONUS * forced.astype(F32), NEG))

    for _ in range(min(TOP_N, nb)):
        for g in range(NSA_GROUPS):
            mx = jnp.max(scores[g], axis=0, keepdims=True)
            idx = jnp.min(jnp.where(scores[g] == mx, jb, nb), axis=0, keepdims=True)
            scores[g] = jnp.where(jb == idx, -jnp.inf, scores[g])
    for g in range(NSA_GROUPS):
        sel_t = jnp.where((scores[g] == -jnp.inf) & valid, 1.0, 0.0)
        if nb < nb_all:
            sel_t = jnp.concatenate([sel_t, jnp.zeros((nb_all - nb, nq), F32)], axis=0)
        sel_q = sel_t.T
        sel_ref[0, g] = sel_q.astype(sel_ref.dtype)
        act_ref[0, g, 0] = jnp.max(sel_q, axis=0, keepdims=True)


def _select_call(slopes, main, kcmp, vcmp, ov, bsz, seq):
    nqt = seq // Q_TILE
    nb, ncp = ov.shape
    row = lambda b, q: b * nqt + q
    gw = NSA_GROUPS * LANES
    return pl.pallas_call(
        _select_kernel,
        grid=(bsz, nqt),
        in_specs=[pl.BlockSpec(memory_space=pltpu.SMEM),
                  pl.BlockSpec((Q_TILE, NSA_Q_W), lambda b, q: (row(b, q), COL_Q // NSA_Q_W)),
                  pl.BlockSpec((1, NSA_GROUPS, ncp, LANES), lambda b, q: (b, 0, 0, 0)),
                  pl.BlockSpec((1, NSA_GROUPS, ncp, LANES), lambda b, q: (b, 0, 0, 0)),
                  pl.BlockSpec((Q_TILE, gw), lambda b, q: (row(b, q), COL_GATE // gw)),
                  pl.BlockSpec((nb, ncp), lambda b, q: (0, 0))],
        out_specs=[pl.BlockSpec((1, NSA_GROUPS, Q_TILE, nb), lambda b, q: (b, 0, q, 0)),
                   pl.BlockSpec((1, NSA_GROUPS, 1, 1, nb), lambda b, q: (b, 0, q, 0, 0)),
                   pl.BlockSpec((Q_TILE, NSA_Q_W), lambda b, q: (row(b, q), 0))],
        out_shape=[jax.ShapeDtypeStruct((bsz, NSA_GROUPS, seq, nb), BF16),
                   jax.ShapeDtypeStruct((bsz, NSA_GROUPS, nqt, 1, nb), F32),
                   jax.ShapeDtypeStruct((bsz * seq, NSA_Q_W), F32)],
        compiler_params=_cparams(("parallel", "parallel")),
        name="select",
    )(slopes, main, kcmp, vcmp, main, ov)


def _attn_kernel(lst_ref, cnt_ref, slopes_ref, q_ref, sel_ref, ks_ref, vs_ref, kw_ref, vw_ref, gates_ref, yc_ref,
                 et_ref, o_ref, s_scr, p_scr, m_scr, a_scr, acc_scr, *, chunk):
    b = pl.program_id(0)
    g = pl.program_id(1)
    step = pl.program_id(2)
    nqt = pl.num_programs(2) * ATTN_TILES
    nq = ATTN_Q_TILE
    tiles = range(ATTN_TILES)
    nch_total = ks_ref.shape[0] // chunk
    slopes = [slopes_ref[g * NSA_REP + r] for r in range(NSA_REP)]
    qts = [step * ATTN_TILES + u for u in tiles]
    t0s = [qt * nq for qt in qts]
    rws = [slice(u * nq, (u + 1) * nq) for u in tiles]
    qs = [_stack_heads_pos(q_ref[rws[u], :] * (NSA_DH ** -0.5), slopes).astype(BF16) for u in tiles]
    pens = [((sel_ref[0, 0, rws[u], :].astype(F32) - 1.0) * (-NEG)).astype(BF16) for u in tiles]
    qx = [jnp.concatenate([qs[u], jnp.concatenate([pens[u]] * NSA_REP, axis=0)], axis=1) for u in tiles]

    bases = [((b * NSA_GROUPS + g) * nqt + qts[u]) * nch_total for u in tiles]
    cnts = [cnt_ref[(b * NSA_GROUPS + g) * nqt + qts[u]] for u in tiles]

    def logits(u, c):
        start = pl.multiple_of(c * chunk, chunk)
        kx = jnp.concatenate([ks_ref[pl.ds(start, chunk), :], et_ref[pl.ds(start, chunk), :]], axis=1)
        return _mm_nt(qx[u], kx)

    def values(p, c):
        start = pl.multiple_of(c * chunk, chunk)
        return jnp.dot(p, vs_ref[pl.ds(start, chunk), :], preferred_element_type=F32)

    def row_max(s):
        return jnp.broadcast_to(jnp.max(s, axis=-1, keepdims=True), (s.shape[0], LANES))

    def lane_tile(m, width):
        return jnp.concatenate([m] * (width // LANES), axis=1)

    for u in tiles:
        m_scr[u] = jnp.full(m_scr.shape[1:], NEG, F32)
        a_scr[u] = jnp.ones(a_scr.shape[1:], F32)
        acc_scr[u] = jnp.zeros(acc_scr.shape[1:], F32)
        p_scr[u, 1] = jnp.zeros(p_scr.shape[2:], BF16)
    for u in tiles:
        s_scr[u, 0] = logits(u, lst_ref[bases[u]])

    def flash_step(us, i, slot):
        for u in us:
            acc_scr[u] = a_scr[u] * acc_scr[u] + values(p_scr[u, 1 - slot], lst_ref[bases[u] + jnp.maximum(i - 1, 0)])
        for u in us:
            s = s_scr[u, slot]
            m_prev = m_scr[u]
            m_new = jnp.maximum(m_prev, row_max(s))
            a_scr[u] = jnp.exp(m_prev - m_new)
            p_scr[u, slot] = jnp.exp(s - lane_tile(m_new, chunk)).astype(BF16)
            m_scr[u] = m_new
        for u in us:
            s_scr[u, 1 - slot] = logits(u, lst_ref[bases[u] + i + 1])

    def single_steps(us):
        def body(i, carry):
            flash_step(us, i, i & 1)
            return carry
        return body

    def double_steps(us):
        def body(j, carry):
            flash_step(us, 2 * j, 0)
            flash_step(us, 2 * j + 1, 1)
            return carry
        return body

    joint = functools.reduce(jnp.minimum, cnts)
    lax.fori_loop(0, joint // 2, double_steps(list(tiles)), 0)
    lax.fori_loop(joint - (joint & 1), joint, single_steps(list(tiles)), 0)
    for u in tiles:
        lax.fori_loop(joint, cnts[u], single_steps([u]), 0)

    slots = [cnts[u] & 1 for u in tiles]
    c_diag = [t0s[u] // chunk for u in tiles]
    acc = [a_scr[u] * acc_scr[u] + values(p_scr[u, 1 - slots[u]], lst_ref[bases[u] + jnp.maximum(cnts[u] - 1, 0)])
           for u in tiles]
    nw = WINDOW + nq
    wbase = [pl.multiple_of(jnp.maximum(t0s[u] - WINDOW, 0), nq) for u in tiles]
    s_win = [_mm_nt(qs[u], kw_ref[pl.ds(wbase[u], nw), :]) for u in tiles]

    def masked(s, ok):
        return jnp.concatenate([jnp.where(ok, s[r * nq:(r + 1) * nq], NEG) for r in range(NSA_REP)], axis=0)

    s_diag = []
    for u in tiles:
        pos = c_diag[u] * chunk + lax.broadcasted_iota(jnp.int32, (nq, chunk), 1)
        tq = t0s[u] + lax.broadcasted_iota(jnp.int32, (nq, chunk), 0)
        s_diag.append(masked(s_scr[u, slots[u]], pos <= tq))
    m_prev = [m_scr[u] for u in tiles]
    m_new = [jnp.maximum(m_prev[u], row_max(s_diag[u])) for u in tiles]
    acc = [jnp.exp(m_prev[u] - m_new[u]) * acc[u]
           + values(jnp.exp(s_diag[u] - lane_tile(m_new[u], chunk)).astype(BF16), c_diag[u]) for u in tiles]
    p_win = []
    for u in tiles:
        pos = wbase[u] + lax.broadcasted_iota(jnp.int32, (nq, nw), 1)
        tq = t0s[u] + lax.broadcasted_iota(jnp.int32, (nq, nw), 0)
        s = masked(s_win[u], (pos <= tq) & (pos > tq - WINDOW))
        p_win.append(jnp.exp(s - jnp.max(s, axis=-1, keepdims=True)).astype(BF16))
    acc_w = [jnp.dot(p_win[u], vw_ref[pl.ds(wbase[u], nw), :], preferred_element_type=F32) for u in tiles]
    for u in tiles:
        gates = gates_ref[rws[u], :]
        o_ref[rws[u], :] = (yc_ref[rws[u], :] + _finish_heads(acc[u], _gate_cols(gates, 1))
                            + _finish_heads(acc_w[u], _gate_cols(gates, 2))).astype(o_ref.dtype)


def _chunk_lists(act, seq, chunk):
    bsz, ngrp = act.shape[:2]
    nqt = seq // ATTN_Q_TILE
    nch = seq // chunk
    active = act.reshape(bsz, ngrp, nqt, ATTN_Q_TILE // Q_TILE, nch, chunk // SEL_LEN).max(axis=(3, 5)) > 0.0
    c_diag = (jnp.arange(nqt) * ATTN_Q_TILE) // chunk
    active = active & (jnp.arange(nch)[None, :] < c_diag[:, None])
    cnt = active.sum(axis=-1).astype(jnp.int32)
    order = jnp.argsort(jnp.logical_not(active), axis=-1, stable=True).astype(jnp.int32)
    lst = jnp.where(jnp.arange(nch) < cnt[..., None], order, c_diag[:, None].astype(jnp.int32))
    return lst.reshape(-1), cnt.reshape(-1)


def _attn_call(lst, cnt, slopes, main, sel, kv, yc, et, bsz, seq, chunk):
    tq = ATTN_Q_TILE * ATTN_TILES
    nqt = seq // tq
    nb = sel.shape[-1]
    rows = NSA_REP * ATTN_Q_TILE
    row = lambda b, g, q: b * nqt + q
    kvspec = lambda blk: pl.BlockSpec((seq, LANES), lambda b, g, q, *_: (b, blk + g))
    grid_spec = pltpu.PrefetchScalarGridSpec(
        num_scalar_prefetch=2,
        grid=(bsz, NSA_GROUPS, nqt),
        in_specs=[pl.BlockSpec(memory_space=pltpu.SMEM),
                  pl.BlockSpec((tq, 256), lambda b, g, q, *_: (row(b, g, q), COL_Q // 256 + g)),
                  pl.BlockSpec((1, 1, tq, nb), lambda b, g, q, *_: (b, g, q, 0)),
                  kvspec(KVB_KS), kvspec(KVB_VS), kvspec(KVB_KW), kvspec(KVB_VW),
                  pl.BlockSpec((tq, LANES), lambda b, g, q, *_: (row(b, g, q), COL_GATE // LANES + g)),
                  pl.BlockSpec((tq, 256), lambda b, g, q, *_: (row(b, g, q), g)),
                  pl.BlockSpec(et.shape, lambda b, g, q, *_: (0, 0))],
        out_specs=pl.BlockSpec((tq, 256), lambda b, g, q, *_: (row(b, g, q), g)),
        scratch_shapes=[pltpu.VMEM((ATTN_TILES, 2, rows, chunk), F32),
                        pltpu.VMEM((ATTN_TILES, 2, rows, chunk), BF16),
                        pltpu.VMEM((ATTN_TILES, rows, LANES), F32),
                        pltpu.VMEM((ATTN_TILES, rows, LANES), F32),
                        pltpu.VMEM((ATTN_TILES, rows, LANES), F32)],
    )
    return pl.pallas_call(
        functools.partial(_attn_kernel, chunk=chunk),
        grid_spec=grid_spec,
        out_shape=jax.ShapeDtypeStruct((bsz * seq, NSA_Q_W), BF16),
        compiler_params=_cparams(("parallel", "parallel", "parallel")),
        name="attn",
    )(lst, cnt, slopes, main, sel, kv, kv, kv, kv, main, yc, et)


def _pool_kernel(x_ref, halo_ref, w_ref, sc_ref, o_ref):
    s = pl.program_id(1)
    ts = x_ref.shape[0]
    x = x_ref[...]
    halo = jnp.where(s > 0, halo_ref[...], 0.0)
    t = s * ts + lax.broadcasted_iota(jnp.int32, (ts, 1), 0)
    for gi, win in enumerate(POOL_WINDOWS):
        lanes = slice(gi * POOL_GROUP_DIM, (gi + 1) * POOL_GROUP_DIM)
        ext = jnp.concatenate([halo[:, lanes], x[:, lanes]], axis=0)
        acc = ext
        span = 1
        while span < win:
            acc = acc + pltpu.roll(acc, span, 0)
            span *= 2
        cnt = jnp.minimum(t + 1, win).astype(F32)
        y = acc[POOL_HALO:] / cnt - x[:, lanes]
        o_ref[:, lanes] = (_mm(y, w_ref[gi]) * sc_ref[:, lanes]).astype(o_ref.dtype)


def _pool_call(main, w, scale, bsz, seq):
    ts = min(POOL_TS, seq)
    per_b = seq // ts
    halo_per_tile = ts // POOL_HALO
    return pl.pallas_call(
        _pool_kernel,
        grid=(bsz, per_b),
        in_specs=[pl.BlockSpec((ts, POOL_W), lambda b, s: (b * per_b + s, COL_POOL // POOL_W)),
                  pl.BlockSpec((POOL_HALO, POOL_W),
                               lambda b, s: (jnp.maximum((b * per_b + s) * halo_per_tile - 1, 0), COL_POOL // POOL_W)),
                  pl.BlockSpec(w.shape, lambda b, s: (0, 0, 0)),
                  pl.BlockSpec((1, POOL_W), lambda b, s: (0, 0))],
        out_specs=pl.BlockSpec((ts, POOL_W), lambda b, s: (b * per_b + s, 0)),
        out_shape=jax.ShapeDtypeStruct((bsz * seq, POOL_W), BF16),
        compiler_params=_cparams(("parallel", "parallel")),
        name="pool",
    )(main, main, w, scale.reshape(1, POOL_W))


def _unit_lower_inverses(nms, ri, ci):
    c = nms[0].shape[0]
    eye = jnp.where(ri == ci, 1.0, 0.0)
    in8 = jnp.right_shift(ri, 3) == jnp.right_shift(ci, 3)
    n8 = [jnp.where(in8, nm, 0.0) for nm in nms]
    ps = [eye - a for a in n8]
    ms = [_mm(a, a) for a in n8]
    ps = [p + _mm(p, m) for p, m in zip(ps, ms)]
    ms = [_mm(m, m) for m in ms]
    ps = [p + _mm(p, m) for p, m in zip(ps, ms)]
    shift = 3
    while (1 << shift) < c:
        rb = jnp.right_shift(ri, shift)
        cb = jnp.right_shift(ci, shift)
        lower_left = ((rb & 1) == 1) & (cb == rb - 1)
        ts = [_mm(p, jnp.where(lower_left, nm, 0.0)) for p, nm in zip(ps, nms)]
        ps = [p - _mm(t, p) for p, t in zip(ps, ts)]
        shift += 1
    return ps


def _row_cumsum(x, ri):
    span = 1
    while span < x.shape[0]:
        x = x + jnp.where(ri >= span, pltpu.roll(x, span, 0), 0.0)
        span *= 2
    return x


def _dn_kernel(qkv_ref, halo_ref, z_ref, gba_ref, cw_ref, alog_ref, dtb_ref, ng_ref, o_ref, st_scr):
    s = pl.program_id(1)
    nbatch, c = qkv_ref.shape[0], qkv_ref.shape[1]

    @pl.when(s == 0)
    def _():
        st_scr[...] = jnp.zeros(st_scr.shape, F32)

    ri = lax.broadcasted_iota(jnp.int32, (c, c), 0)
    ci = lax.broadcasted_iota(jnp.int32, (c, c), 1)
    tril = ri >= ci
    strict = ri > ci
    cw = cw_ref[...]
    ng = ng_ref[...]
    chains = [(bi, h) for bi in range(nbatch) for h in range(DN_HEADS)]
    qn, kn, kb, vb, gcb, lmask = [], [], [], [], [], []
    for bi in range(nbatch):
        halo = jnp.where(s > 0, halo_ref[bi], 0.0)
        ext = jnp.concatenate([halo, qkv_ref[bi]], axis=0)
        conv = cw[DN_CONV - 1:DN_CONV] * ext
        for tap in range(1, DN_CONV):
            conv = conv + cw[DN_CONV - 1 - tap:DN_CONV - tap] * pltpu.roll(ext, tap, 0)
        act = _silu(conv[DN_HALO:])

        gba = gba_ref[bi]
        beta_all = jax.nn.sigmoid(gba)
        xs = gba + dtb_ref[...]
        softplus = jnp.maximum(xs, 0.0) + jnp.log(1.0 + jnp.exp(-jnp.abs(xs)))
        gc_all = _row_cumsum(-jnp.exp(alog_ref[...]) * softplus, ri)
        gct_all = gc_all.T
        for h in range(DN_HEADS):
            q = act[:, h * DN_DK:(h + 1) * DN_DK]
            k = act[:, DN_W + h * DN_DK:DN_W + (h + 1) * DN_DK]
            v = act[:, 2 * DN_W + h * DN_DK:2 * DN_W + (h + 1) * DN_DK]
            qn.append(q * lax.rsqrt(jnp.sum(q * q, axis=-1, keepdims=True) + EPS) * (DN_DK ** -0.5))
            kn.append(k * lax.rsqrt(jnp.sum(k * k, axis=-1, keepdims=True) + EPS))
            beta = beta_all[:, LANE_BETA + h:LANE_BETA + h + 1]
            gcb.append(jnp.broadcast_to(gc_all[:, LANE_DECAY + h:LANE_DECAY + h + 1], (c, DN_DK)))
            gct = jnp.broadcast_to(gct_all[LANE_DECAY + h:LANE_DECAY + h + 1, :], (c, c))
            lmask.append(jnp.where(tril, jnp.exp(jnp.where(tril, gcb[-1] - gct, 0.0)), 0.0))
            kb.append(kn[-1] * beta)
            vb.append(v * beta)

    n = range(len(chains))
    kk = [_mm_nt(kb[i], kn[i]) for i in n]
    tinv = _unit_lower_inverses([jnp.where(strict, kk[i] * lmask[i], 0.0) for i in n], ri, ci)
    egc = [jnp.exp(gcb[i]) for i in n]
    sol = [_mm(tinv[i], jnp.concatenate([vb[i], kb[i] * egc[i]], axis=1)) for i in n]
    aqk = [_mm_nt(qn[i], kn[i]) * lmask[i] for i in n]
    state = [st_scr[i] for i in n]
    glast = [gcb[i][c - 1:c, :] for i in n]
    v_new = [sol[i][:, :DN_DK] - _mm(sol[i][:, DN_DK:], state[i]) for i in n]
    o_state = [_mm(qn[i] * egc[i], state[i]) for i in n]
    o = [o_state[i] + _mm(aqk[i], v_new[i]) for i in n]
    upd = [_mm_tn(kn[i] * jnp.exp(glast[i] - gcb[i]), v_new[i]) for i in n]
    for i, (bi, h) in enumerate(chains):
        st_scr[i] = state[i] * jnp.exp(glast[i]) + upd[i]
        on = o[i] * lax.rsqrt(jnp.mean(o[i] * o[i], axis=-1, keepdims=True) + EPS) * ng
        o_ref[bi, :, h * DN_DK:(h + 1) * DN_DK] = (
            on * _silu(z_ref[bi, :, h * DN_DK:(h + 1) * DN_DK])).astype(o_ref.dtype)


def _dn_call(main3, conv_w, alog_row, dtb_row, norm_g):
    bsz, seq, _ = main3.shape
    c = DN_CHUNK
    nbt = DN_BATCH if bsz % DN_BATCH == 0 else 1
    halo_per_chunk = c // DN_HALO
    w3 = 3 * DN_W
    return pl.pallas_call(
        _dn_kernel,
        grid=(bsz // nbt, seq // c),
        in_specs=[pl.BlockSpec((nbt, c, w3), lambda b, s: (b, s, COL_QKV // w3)),
                  pl.BlockSpec((nbt, DN_HALO, w3),
                               lambda b, s: (b, jnp.maximum(s * halo_per_chunk - 1, 0), COL_QKV // w3)),
                  pl.BlockSpec((nbt, c, DN_W), lambda b, s: (b, s, COL_Z // DN_W)),
                  pl.BlockSpec((nbt, c, LANES), lambda b, s: (b, s, COL_GATE // LANES)),
                  pl.BlockSpec((SUBLANES, w3), lambda b, s: (0, 0)),
                  pl.BlockSpec((1, LANES), lambda b, s: (0, 0)),
                  pl.BlockSpec((1, LANES), lambda b, s: (0, 0)),
                  pl.BlockSpec((1, DN_DK), lambda b, s: (0, 0))],
        out_specs=pl.BlockSpec((nbt, c, DN_W), lambda b, s: (b, s, 0)),
        out_shape=jax.ShapeDtypeStruct((bsz, seq, DN_W), BF16),
        scratch_shapes=[pltpu.VMEM((nbt * DN_HEADS, DN_DK, DN_DK), F32)],
        compiler_params=_cparams(("parallel", "arbitrary")),
        name="dn",
    )(main3, main3, main3, main3, conv_w, alog_row, dtb_row, norm_g.reshape(1, DN_DK))


def _merge_kernel(ya_ref, yb_ref, yc_ref, mg_ref, x_ref, gt_ref, wa_ref, wb_ref, wc_ref, wo_ref, o_ref):
    d = x_ref.shape[1]
    gates = jax.nn.sigmoid(mg_ref[...].astype(F32))
    merged = (gates[:, :d] * jnp.dot(ya_ref[...], wa_ref[...], preferred_element_type=F32)
              + gates[:, d:2 * d] * jnp.dot(yb_ref[...], wb_ref[...], preferred_element_type=F32)
              + gates[:, 2 * d:] * jnp.dot(yc_ref[...], wc_ref[...], preferred_element_type=F32))
    y = jnp.dot(merged.astype(BF16), wo_ref[...], preferred_element_type=F32)
    o_ref[...] = x_ref[...] + gt_ref[0] * y


def _merge_call(ya, yb, yc, mg, x2, gt, wa, wb, wc, wo, seq):
    t, d = x2.shape
    tm = min(MERGE_TM, seq)
    per_b = seq // tm
    full = lambda a: pl.BlockSpec(a.shape, lambda i: (0, 0))
    return pl.pallas_call(
        _merge_kernel,
        grid=(t // tm,),
        in_specs=[pl.BlockSpec((tm, ya.shape[1]), lambda i: (i, 0)),
                  pl.BlockSpec((tm, yb.shape[1]), lambda i: (i, 0)),
                  pl.BlockSpec((tm, yc.shape[1]), lambda i: (i, 0)),
                  pl.BlockSpec((tm, mg.shape[1]), lambda i: (i, 0)),
                  pl.BlockSpec((tm, d), lambda i: (i, 0)),
                  pl.BlockSpec((1, 1, d), lambda i: (i // per_b, 0, 0)),
                  full(wa), full(wb), full(wc), full(wo)],
        out_specs=pl.BlockSpec((tm, d), lambda i: (i, 0)),
        out_shape=jax.ShapeDtypeStruct((t, d), F32),
        compiler_params=_cparams(("parallel",)),
        name="merge",
    )(ya, yb, yc, mg, x2, gt, wa, wb, wc, wo)


def _mlp_kernel(x_ref, g_ref, sc_ref, sh_ref, gt_ref, w1_ref, w2_ref, fg_ref, o_ref, h_scr, acc_scr, *, final):
    j = pl.program_id(1)

    @pl.when(j == 0)
    def _():
        h_scr[...] = _norm_mod(x_ref[...], g_ref[...], sc_ref[0], sh_ref[0]).astype(BF16)
        acc_scr[...] = jnp.zeros(acc_scr.shape, F32)

    a = jnp.maximum(jnp.dot(h_scr[...], w1_ref[...], preferred_element_type=F32), 0.0)
    acc_scr[...] += jnp.dot((a * a).astype(BF16), w2_ref[...], preferred_element_type=F32)

    @pl.when(j == pl.num_programs(1) - 1)
    def _():
        y = x_ref[...] + gt_ref[0] * acc_scr[...]
        if final:
            ms = jnp.mean(y * y, axis=-1, keepdims=True)
            y = y * lax.rsqrt(ms + EPS) * fg_ref[...]
        o_ref[...] = y


def _mlp_call(x2, g, sc, sh, gt, w1, w2, fg, seq, final):
    t, d = x2.shape
    f = w1.shape[1]
    tm = min(MLP_TM, seq)
    tf = MLP_TF
    per_b = seq // tm
    mod = lambda: pl.BlockSpec((1, 1, d), lambda i, j: (i // per_b, 0, 0))
    return pl.pallas_call(
        functools.partial(_mlp_kernel, final=final),
        grid=(t // tm, f // tf),
        in_specs=[pl.BlockSpec((tm, d), lambda i, j: (i, 0)),
                  pl.BlockSpec((1, d), lambda i, j: (0, 0)),
                  mod(), mod(), mod(),
                  pl.BlockSpec((d, tf), lambda i, j: (0, j)),
                  pl.BlockSpec((tf, d), lambda i, j: (j, 0)),
                  pl.BlockSpec((1, d), lambda i, j: (0, 0))],
        out_specs=pl.BlockSpec((tm, d), lambda i, j: (i, 0)),
        out_shape=jax.ShapeDtypeStruct((t, d), F32),
        scratch_shapes=[pltpu.VMEM((tm, d), BF16), pltpu.VMEM((tm, d), F32)],
        compiler_params=_cparams(("parallel", "arbitrary")),
        name="mlp",
    )(x2, g.reshape(1, d), sc, sh, gt, w1, w2, fg.reshape(1, d))


def _split_w_in(w_in):
    offs = [0]
    for sz in IN_SIZES:
        offs.append(offs[-1] + sz)
    return [w_in[:, offs[i]:offs[i + 1]] for i in range(len(IN_SIZES))]


def _layout_w_in(w_in):
    nq, nkc, nvc, nks, nvs, nkw, nvw, ngate, pin, dqkv, dz, dbeta, da, mg = _split_w_in(w_in)
    d = w_in.shape[0]
    per_g = 3 * NSA_REP
    zeros = lambda n: jnp.zeros((d, n), w_in.dtype)
    gate0 = jnp.concatenate([ngate[:, :per_g], zeros(LANE_BETA - per_g), dbeta, da,
                             zeros(LANES - LANE_DECAY - DN_HEADS)], axis=1)
    gate1 = jnp.concatenate([ngate[:, per_g:], zeros(LANES - per_g)], axis=1)
    main = jnp.concatenate([nq, pin, dz, dqkv, gate0, gate1, nkc, nvc], axis=1)

    def per_group(w):
        pad = zeros(LANES - NSA_DH)
        return jnp.concatenate([w[:, :NSA_DH], pad, w[:, NSA_DH:], pad], axis=1)

    kv = jnp.concatenate([per_group(nks), per_group(nvs), per_group(nkw), per_group(nvw)], axis=1)
    return main.astype(BF16), mg.astype(BF16), kv.astype(BF16)


def _group_lane_w1(w1):
    two, _, hid = w1.shape
    halves = w1.reshape(two, 2, CMP_STRIDE, NSA_DH, hid)
    out = []
    for g in range(NSA_GROUPS):
        padded = jnp.zeros((two, 2, CMP_STRIDE, LANES, hid), w1.dtype).at[:, :, :, g * NSA_DH:(g + 1) * NSA_DH].set(halves)
        out.append(padded.reshape(two, 2, CMP_STRIDE * LANES, hid))
    return jnp.stack(out, axis=1)


def _nsa_constants(seq):
    nb = seq // SEL_LEN
    ncp = seq // CMP_STRIDE
    j = jnp.arange(nb)[:, None] * SEL_LEN
    i = jnp.arange(ncp)[None, :] * CMP_STRIDE
    overlap = jnp.clip(jnp.minimum(j + SEL_LEN, i + CMP_LEN) - jnp.maximum(j, i), 0) // CMP_STRIDE
    block_onehot = (jnp.arange(seq)[:, None] // SEL_LEN) == jnp.arange(nb)[None, :]
    return overlap.astype(BF16), block_onehot.astype(BF16)


def kernel(x, c, ada_w, ada_b, norm1_g, norm2_g, w_in, phi_k1, phi_k2, phi_v1, phi_v2, pos_k, pos_v, pool_w, pool_scale, dn_conv_w, dn_A_log, dn_dt_bias, dn_norm_g, w_branch_nsa, w_branch_pool, w_branch_dn, w_out, mlp_w1, mlp_w2, final_g):
    bsz, seq, d = x.shape
    assert d == D_MODEL and seq % max(PROJ_TM, MLP_TM, POOL_TS) == 0 and seq % KV_CHUNK == 0
    assert DN_CHUNK == DN_DK == LANES
    t = bsz * seq
    depth = ada_w.shape[0]
    chunk = KV_CHUNK
    nqt = seq // Q_TILE
    nb = seq // SEL_LEN
    blocks_per_chunk = chunk // SEL_LEN

    slopes = 2.0 ** (-(8.0 / NSA_HEADS) * (jnp.arange(NSA_HEADS, dtype=F32) + 1.0))
    overlap, block_onehot = _nsa_constants(seq)
    c_pad = jnp.zeros((SUBLANES, d), F32).at[:bsz].set(c)
    x2 = x.reshape(t, d)

    for l in range(depth):
        mod = _mod_call(c_pad, ada_w[l], ada_b[l])[:bsz]
        sh1, sc1, gt1, sh2, sc2, gt2 = [m.reshape(bsz, 1, d) for m in jnp.split(mod, 6, axis=-1)]

        w_main, w_mg, w_kv = _layout_w_in(w_in[l])
        main = _proj_call(x2, norm1_g[l], sc1, sh1, w_main, seq, PROJ_MAIN_TM, F32, "proj_main")
        mg = _proj_call(x2, norm1_g[l], sc1, sh1, w_mg, seq, PROJ_TM, BF16, "proj_mg")
        kv = _proj_kv_call(x2, norm1_g[l], sc1, sh1, w_kv, seq)

        main3 = main.reshape(bsz, seq, N_MAIN)
        pos = jnp.zeros((2, SUBLANES, CMP_LEN * NSA_DH), F32).at[:, 0].set(
            jnp.stack([pos_k[l].reshape(-1), pos_v[l].reshape(-1)])).astype(BF16)
        w1 = jnp.stack([phi_k1[l], phi_v1[l]]).astype(BF16)
        w2 = jnp.stack([phi_k2[l], phi_v2[l]])
        w2 = jnp.concatenate([w2, w2], axis=-1).astype(BF16)
        cmp_kv = _compress_call(main3, pos, w1, _group_lane_w1(w1), w2)

        sel, act, yc = _select_call(slopes, main, cmp_kv[0], cmp_kv[1], overlap, bsz, seq)
        lst, cnt = _chunk_lists(act, seq, chunk)
        y_a = _attn_call(lst, cnt, slopes, main, sel, kv, yc, block_onehot, bsz, seq, chunk)

        y_b = _pool_call(main, pool_w[l].astype(BF16), pool_scale[l], bsz, seq)

        conv_w = jnp.zeros((SUBLANES, 3 * DN_W), F32).at[:DN_CONV].set(dn_conv_w[l])
        lane_row = lambda v: jnp.zeros((1, LANES), F32).at[0, LANE_DECAY:LANE_DECAY + DN_HEADS].set(v)
        y_c = _dn_call(main3, conv_w, lane_row(dn_A_log[l]), lane_row(dn_dt_bias[l]),
                       dn_norm_g[l]).reshape(t, DN_W)

        x2 = _merge_call(y_a, y_b, y_c, mg, x2, gt1,
                         w_branch_nsa[l].astype(BF16), w_branch_pool[l].astype(BF16),
                         w_branch_dn[l].astype(BF16), w_out[l].astype(BF16), seq)
        x2 = _mlp_call(x2, norm2_g[l], sc2, sh2, gt2, mlp_w1[l].astype(BF16), mlp_w2[l].astype(BF16),
                       final_g, seq, final=(l == depth - 1))
    return x2.reshape(bsz, seq, d)
```

```python
import functools

import jax
import jax.numpy as jnp
from jax import lax
from jax.experimental import pallas as pl
from jax.experimental.pallas import tpu as pltpu

F32 = jnp.float32
BF16 = jnp.bfloat16
HIGHEST = lax.Precision.HIGHEST

D_MODEL = 1024
NSA_HEADS = 8
NSA_GROUPS = 2
NSA_REP = NSA_HEADS // NSA_GROUPS
NSA_DH = 64
CMP_LEN = 32
CMP_STRIDE = 16
SEL_LEN = 64
SEL_SHIFT = 6
TOP_N = 16
WINDOW = 512
FORCE_BONUS = 1.0e4
POOL_WINDOWS = (2, 4, 8, 16)
POOL_GROUP_DIM = 128
POOL_W = len(POOL_WINDOWS) * POOL_GROUP_DIM
DN_HEADS = 4
DN_DK = 128
DN_W = DN_HEADS * DN_DK
DN_CONV = 4
EPS = 1e-6
NEG = -1e30
NSA_Q_W = NSA_HEADS * NSA_DH
NSA_KV_W = NSA_GROUPS * NSA_DH
IN_SIZES = (NSA_Q_W, NSA_KV_W, NSA_KV_W, NSA_KV_W, NSA_KV_W, NSA_KV_W, NSA_KV_W,
            3 * NSA_HEADS, POOL_W, 3 * DN_W, DN_W, DN_HEADS, DN_HEADS, 3 * D_MODEL)

LANES = 128
LANE_SHIFT = 7
SUBLANES = 8
VMEM_LIMIT_BYTES = 48 * 1024 * 1024

Q_TILE = 128
SELECT_BUCKETS = 4
ATTN_Q_TILE = 128
ATTN_TILES = 2
KV_CHUNK = 256
DN_CHUNK = 128
DN_BATCH = 4
PROJ_TM = 1024
MLP_TM = 1024
MLP_TF = 2048
MERGE_TM = 512
POOL_TS = 1024
POOL_HALO = 16
DN_HALO = 8

COL_Q = 0
COL_POOL = 512
COL_Z = 1024
COL_QKV = 1536
COL_GATE = 3072
COL_KC = 3328
COL_VC = 3456
N_MAIN = 3584
PROJ_MAIN_TM = 512
N_MG = 3072
LANE_BETA = 16
LANE_DECAY = 20
KVB_KS = 0
KVB_VS = 2
KVB_KW = 4
KVB_VW = 6
KVB_END = 8
N_KV = KVB_END * LANES
LANE_POS_HI = NSA_DH
LANE_POS_LO = NSA_DH + 1


def _cparams(sem):
    return pltpu.CompilerParams(dimension_semantics=sem, vmem_limit_bytes=VMEM_LIMIT_BYTES)


def _silu(v):
    return v * jax.nn.sigmoid(v)


def _mm(a, b):
    return jnp.dot(a.astype(BF16), b.astype(BF16), preferred_element_type=F32)


def _mm_nt(a, b):
    return lax.dot_general(a.astype(BF16), b.astype(BF16), (((1,), (1,)), ((), ())), preferred_element_type=F32)


def _mm_tn(a, b):
    return lax.dot_general(a.astype(BF16), b.astype(BF16), (((0,), (0,)), ((), ())), preferred_element_type=F32)


def _mod_kernel(c_ref, w_ref, b_ref, o_ref):
    cond = _silu(c_ref[...])
    o_ref[...] = jnp.dot(cond, w_ref[...], preferred_element_type=F32, precision=HIGHEST) + b_ref[...]


def _mod_call(c_pad, w, b):
    d, n = w.shape
    tn = 1536
    return pl.pallas_call(
        _mod_kernel,
        grid=(n // tn,),
        in_specs=[pl.BlockSpec((SUBLANES, d), lambda j: (0, 0)),
                  pl.BlockSpec((d, tn), lambda j: (0, j)),
                  pl.BlockSpec((1, tn), lambda j: (0, j))],
        out_specs=pl.BlockSpec((SUBLANES, tn), lambda j: (0, j)),
        out_shape=jax.ShapeDtypeStruct((SUBLANES, n), F32),
        compiler_params=_cparams(("parallel",)),
        name="mod",
    )(c_pad, w, b.reshape(1, n))


def _norm_mod(x, g, sc, sh):
    ms = jnp.mean(x * x, axis=-1, keepdims=True)
    return (x * lax.rsqrt(ms + EPS) * g) * (1.0 + sc) + sh


def _proj_kernel(x_ref, g_ref, sc_ref, sh_ref, w_ref, o_ref):
    h = _norm_mod(x_ref[...], g_ref[...], sc_ref[0], sh_ref[0]).astype(BF16)
    o_ref[...] = jnp.dot(h, w_ref[...], preferred_element_type=F32).astype(o_ref.dtype)


def _proj_kv_kernel(x_ref, g_ref, sc_ref, sh_ref, w_ref, o_ref, *, tiles_per_seq):
    h = _norm_mod(x_ref[...], g_ref[...], sc_ref[0], sh_ref[0]).astype(BF16)
    acc = jnp.dot(h, w_ref[...], preferred_element_type=F32)
    tm, n = acc.shape
    col = lax.broadcasted_iota(jnp.int32, (1, n), 1)
    blk = jnp.right_shift(col, LANE_SHIFT)
    lane = col & (LANES - 1)
    is_key = ((blk >= KVB_KS) & (blk < KVB_VS)) | ((blk >= KVB_KW) & (blk < KVB_VW))
    is_val = ((blk >= KVB_VS) & (blk < KVB_KW)) | ((blk >= KVB_VW) & (blk < KVB_END))
    ones_row = jnp.where(is_val & (lane >= NSA_DH), 1.0, 0.0)
    hi_row = jnp.where(is_key & (lane == LANE_POS_HI), 1.0, 0.0)
    lo_row = jnp.where(is_key & (lane == LANE_POS_LO), 1.0, 0.0)
    pos = (pl.program_id(0) % tiles_per_seq) * tm + lax.broadcasted_iota(jnp.int32, (tm, 1), 0)
    pos_hi = jnp.right_shift(pos, LANE_SHIFT).astype(F32)
    pos_lo = (pos & (LANES - 1)).astype(F32)
    o_ref[...] = (acc + ones_row + pos_hi * hi_row + pos_lo * lo_row).astype(o_ref.dtype)


def _proj_specs(tm, d, tn, per_b):
    return [pl.BlockSpec((tm, d), lambda i, j: (i, 0)),
            pl.BlockSpec((1, d), lambda i, j: (0, 0)),
            pl.BlockSpec((1, 1, d), lambda i, j: (i // per_b, 0, 0)),
            pl.BlockSpec((1, 1, d), lambda i, j: (i // per_b, 0, 0)),
            pl.BlockSpec((d, tn), lambda i, j: (0, j))]


def _proj_call(x2, g, sc, sh, w, seq, tm, out_dtype, name):
    t, d = x2.shape
    n = w.shape[1]
    tm = min(tm, seq)
    return pl.pallas_call(
        _proj_kernel,
        grid=(t // tm, 1),
        in_specs=_proj_specs(tm, d, n, seq // tm),
        out_specs=pl.BlockSpec((tm, n), lambda i, j: (i, j)),
        out_shape=jax.ShapeDtypeStruct((t, n), out_dtype),
        compiler_params=_cparams(("parallel", "arbitrary")),
        name=name,
    )(x2, g.reshape(1, d), sc, sh, w)


def _proj_kv_call(x2, g, sc, sh, w, seq):
    t, d = x2.shape
    n = w.shape[1]
    tm = min(PROJ_TM, seq)
    return pl.pallas_call(
        functools.partial(_proj_kv_kernel, tiles_per_seq=seq // tm),
        grid=(t // tm, 1),
        in_specs=_proj_specs(tm, d, n, seq // tm),
        out_specs=pl.BlockSpec((tm, n), lambda i, j: (i, j)),
        out_shape=jax.ShapeDtypeStruct((t, n), BF16),
        compiler_params=_cparams(("parallel", "arbitrary")),
        name="proj_kv",
    )(x2, g.reshape(1, d), sc, sh, w)


def _compress_kernel(x_ref, pos_ref, w1_ref, w1g_ref, w2_ref, o_ref):
    nr = x_ref.shape[1] // CMP_STRIDE
    xs = jnp.concatenate([x_ref[0, pl.ds(rho, nr, stride=CMP_STRIDE), :].astype(BF16) for rho in range(CMP_STRIDE)],
                         axis=1)
    posb = jnp.dot(pos_ref[0], w1_ref[0], preferred_element_type=F32)[0:1]
    for g in range(NSA_GROUPS):
        top = jnp.dot(xs, w1g_ref[0, g, 0], preferred_element_type=F32)
        bot = jnp.dot(xs, w1g_ref[0, g, 1], preferred_element_type=F32)
        hid = top + pltpu.roll(bot, nr - 1, 0) + posb
        o_ref[0, 0, g] = _mm(_silu(hid), w2_ref[0]).astype(o_ref.dtype)


def _compress_call(main3, pos, w1, w1g, w2):
    b, seq, _ = main3.shape
    nr = seq // CMP_STRIDE
    two, l64, hid = w1.shape
    return pl.pallas_call(
        _compress_kernel,
        grid=(two, b),
        in_specs=[pl.BlockSpec((1, seq, LANES), lambda a, i: (i, 0, COL_KC // LANES + a)),
                  pl.BlockSpec((1, SUBLANES, l64), lambda a, i: (a, 0, 0)),
                  pl.BlockSpec((1, l64, hid), lambda a, i: (a, 0, 0)),
                  pl.BlockSpec((1,) + w1g.shape[1:], lambda a, i: (a, 0, 0, 0, 0)),
                  pl.BlockSpec((1, hid, LANES), lambda a, i: (a, 0, 0))],
        out_specs=pl.BlockSpec((1, 1, NSA_GROUPS, nr, LANES), lambda a, i: (a, i, 0, 0, 0)),
        out_shape=jax.ShapeDtypeStruct((two, b, NSA_GROUPS, nr, LANES), BF16),
        compiler_params=_cparams(("parallel", "parallel")),
        name="compress",
    )(main3, pos, w1, w1g, w2)


def _gate_cols(gate_logits, branch):
    sg = jax.nn.sigmoid(gate_logits)
    return [sg[:, 3 * r + branch:3 * r + branch + 1] for r in range(NSA_REP)]


def _stack_heads_dup(q):
    qa = q[:, :LANES]
    qb = q[:, LANES:]
    lo = lax.broadcasted_iota(jnp.int32, qa.shape, 1) < NSA_DH
    z = jnp.zeros_like(qa)
    return jnp.concatenate([jnp.where(lo, qa, z), jnp.where(lo, z, qa),
                            jnp.where(lo, qb, z), jnp.where(lo, z, qb)], axis=0)


def _unstack_heads_dup(o, gates):
    nq = o.shape[0] // NSA_REP
    lo = lax.broadcasted_iota(jnp.int32, (nq, LANES), 1) < NSA_DH
    a = jnp.where(lo, gates[0] * o[0:nq], gates[1] * o[nq:2 * nq])
    b = jnp.where(lo, gates[2] * o[2 * nq:3 * nq], gates[3] * o[3 * nq:])
    return jnp.concatenate([a, b], axis=1)


def _stack_heads_pos(q, slopes):
    qa = q[:, :LANES]
    qb = q[:, LANES:]
    lane = lax.broadcasted_iota(jnp.int32, qa.shape, 1)
    lo = lane < NSA_DH
    heads = [qa, pltpu.roll(qa, NSA_DH, 1), qb, pltpu.roll(qb, NSA_DH, 1)]
    out = []
    for r in range(NSA_REP):
        extra = jnp.where(lane == LANE_POS_HI, slopes[r] * LANES, jnp.where(lane == LANE_POS_LO, slopes[r], 0.0))
        out.append(jnp.where(lo, heads[r], extra))
    return jnp.concatenate(out, axis=0)


def _finish_heads(acc, gates):
    nq = acc.shape[0] // NSA_REP
    o = acc * pltpu.roll(1.0 / acc, NSA_DH, 1)
    lo = lax.broadcasted_iota(jnp.int32, (nq, LANES), 1) < NSA_DH
    hs = [gates[r] * o[r * nq:(r + 1) * nq] for r in range(NSA_REP)]
    a = jnp.where(lo, hs[0], pltpu.roll(hs[1], NSA_DH, 1))
    b = jnp.where(lo, hs[2], pltpu.roll(hs[3], NSA_DH, 1))
    return jnp.concatenate([a, b], axis=1)


def _select_kernel(slopes_ref, q_ref, kc_ref, vc_ref, gates_ref, ov_ref, sel_ref, act_ref, yc_ref):
    qt = pl.program_id(1)
    nqt = pl.num_programs(1)
    nb_all, ncp_all = ov_ref.shape
    for k in range(SELECT_BUCKETS):
        @pl.when((qt * SELECT_BUCKETS) // nqt == k)
        def _(k=k):
            _select_prefix(slopes_ref, q_ref, kc_ref, vc_ref, gates_ref, ov_ref, sel_ref, act_ref, yc_ref,
                           nb=(k + 1) * nb_all // SELECT_BUCKETS, ncp=(k + 1) * ncp_all // SELECT_BUCKETS)


def _select_prefix(slopes_ref, q_ref, kc_ref, vc_ref, gates_ref, ov_ref, sel_ref, act_ref, yc_ref, *, nb, ncp):
    t0 = pl.program_id(1) * Q_TILE
    nq = Q_TILE
    nb_all = ov_ref.shape[0]
    c_end = lax.broadcasted_iota(jnp.int32, (nq, ncp), 1) * CMP_STRIDE + (CMP_LEN - 1)
    tq = t0 + lax.broadcasted_iota(jnp.int32, (nq, ncp), 0)
    vis = c_end <= tq
    any_vis = tq[:, 0:1] >= CMP_LEN - 1
    rel = (c_end[0:1] - t0).astype(F32)
    ov = ov_ref[:nb, :ncp]
    jb = lax.broadcasted_iota(jnp.int32, (nb, nq), 0)
    tql = t0 + lax.broadcasted_iota(jnp.int32, (nb, nq), 1)
    valid = jb * SEL_LEN <= tql
    forced = (jb == 0) | (jb == jnp.right_shift(tql, SEL_SHIFT))
    scores = []
    for g in range(NSA_GROUPS):
        qcols = slice(g * NSA_REP * NSA_DH, (g + 1) * NSA_REP * NSA_DH)
        qs = _stack_heads_dup(q_ref[:, qcols] * (NSA_DH ** -0.5)).astype(BF16)
        s = _mm_nt(qs, kc_ref[0, g, :ncp, :])
        psum = jnp.zeros((nq, ncp), F32)
        ps = []
        for r in range(NSA_REP):
            m_r = slopes_ref[g * NSA_REP + r]
            sr = jnp.where(vis, s[r * nq:(r + 1) * nq] + m_r * rel, NEG)
            mx = jnp.max(sr, axis=-1, keepdims=True)
            e = jnp.exp(sr - mx)
            den = jnp.sum(e, axis=-1, keepdims=True)
            p = e * jnp.where(any_vis, 1.0 / den, 0.0)
            psum = psum + p
            ps.append(p.astype(BF16))
        o_c = jnp.dot(jnp.concatenate(ps, axis=0), vc_ref[0, g, :ncp, :], preferred_element_type=F32)
        yc_ref[:, qcols] = _unstack_heads_dup(o_c, _gate_cols(gates_ref[:, g * LANES:(g + 1) * LANES], 0))
        p_hi = psum.astype(BF16)
        p_lo = (psum - p_hi.astype(F32)).astype(BF16)
        imp_t = _mm_nt(ov, p_hi) + _mm_nt(ov, p_lo)
        scores.append(jnp.where(valid, imp_t + FORCE_BONUS * forced.astype(F32), NEG))

    for _ in range(min(TOP_N, nb)):
        for g in range(NSA_GROUPS):
            mx = jnp.max(scores[g], axis=0, keepdims=True)
            idx = jnp.min(jnp.where(scores[g] == mx, jb, nb), axis=0, keepdims=True)
            scores[g] = jnp.where(jb == idx, -jnp.inf, scores[g])
    for g in range(NSA_GROUPS):
        sel_t = jnp.where((scores[g] == -jnp.inf) & valid, 1.0, 0.0)
        if nb < nb_all:
            sel_t = jnp.concatenate([sel_t, jnp.zeros((nb_all - nb, nq), F32)], axis=0)
        sel_q = sel_t.T
        sel_ref[0, g] = sel_q.astype(sel_ref.dtype)
        act_ref[0, g, 0] = jnp.max(sel_q, axis=0, keepdims=True)


def _select_call(slopes, main, kcmp, vcmp, ov, bsz, seq):
    nqt = seq // Q_TILE
    nb, ncp = ov.shape
    row = lambda b, q: b * nqt + q
    gw = NSA_GROUPS * LANES
    return pl.pallas_call(
        _select_kernel,
        grid=(bsz, nqt),
        in_specs=[pl.BlockSpec(memory_space=pltpu.SMEM),
                  pl.BlockSpec((Q_TILE, NSA_Q_W), lambda b, q: (row(b, q), COL_Q // NSA_Q_W)),
                  pl.BlockSpec((1, NSA_GROUPS, ncp, LANES), lambda b, q: (b, 0, 0, 0)),
                  pl.BlockSpec((1, NSA_GROUPS, ncp, LANES), lambda b, q: (b, 0, 0, 0)),
                  pl.BlockSpec((Q_TILE, gw), lambda b, q: (row(b, q), COL_GATE // gw)),
                  pl.BlockSpec((nb, ncp), lambda b, q: (0, 0))],
        out_specs=[pl.BlockSpec((1, NSA_GROUPS, Q_TILE, nb), lambda b, q: (b, 0, q, 0)),
                   pl.BlockSpec((1, NSA_GROUPS, 1, 1, nb), lambda b, q: (b, 0, q, 0, 0)),
                   pl.BlockSpec((Q_TILE, NSA_Q_W), lambda b, q: (row(b, q), 0))],
        out_shape=[jax.ShapeDtypeStruct((bsz, NSA_GROUPS, seq, nb), BF16),
                   jax.ShapeDtypeStruct((bsz, NSA_GROUPS, nqt, 1, nb), F32),
                   jax.ShapeDtypeStruct((bsz * seq, NSA_Q_W), F32)],
        compiler_params=_cparams(("parallel", "parallel")),
        name="select",
    )(slopes, main, kcmp, vcmp, main, ov)


def _attn_kernel(lst_ref, cnt_ref, slopes_ref, q_ref, sel_ref, ks_ref, vs_ref, kw_ref, vw_ref, gates_ref, yc_ref,
                 et_ref, o_ref, s_scr, p_scr, m_scr, a_scr, acc_scr, *, chunk):
    b = pl.program_id(0)
    g = pl.program_id(1)
    step = pl.program_id(2)
    nqt = pl.num_programs(2) * ATTN_TILES
    nq = ATTN_Q_TILE
    tiles = range(ATTN_TILES)
    nch_total = ks_ref.shape[0] // chunk
    slopes = [slopes_ref[g * NSA_REP + r] for r in range(NSA_REP)]
    qts = [step * ATTN_TILES + u for u in tiles]
    t0s = [qt * nq for qt in qts]
    rws = [slice(u * nq, (u + 1) * nq) for u in tiles]
    qs = [_stack_heads_pos(q_ref[rws[u], :] * (NSA_DH ** -0.5), slopes).astype(BF16) for u in tiles]
    pens = [((sel_ref[0, 0, rws[u], :].astype(F32) - 1.0) * (-NEG)).astype(BF16) for u in tiles]
    qx = [jnp.concatenate([qs[u], jnp.concatenate([pens[u]] * NSA_REP, axis=0)], axis=1) for u in tiles]

    bases = [((b * NSA_GROUPS + g) * nqt + qts[u]) * nch_total for u in tiles]
    cnts = [cnt_ref[(b * NSA_GROUPS + g) * nqt + qts[u]] for u in tiles]

    def logits(u, c):
        start = pl.multiple_of(c * chunk, chunk)
        kx = jnp.concatenate([ks_ref[pl.ds(start, chunk), :], et_ref[pl.ds(start, chunk), :]], axis=1)
        return _mm_nt(qx[u], kx)

    def values(p, c):
        start = pl.multiple_of(c * chunk, chunk)
        return jnp.dot(p, vs_ref[pl.ds(start, chunk), :], preferred_element_type=F32)

    def row_max(s):
        return jnp.broadcast_to(jnp.max(s, axis=-1, keepdims=True), (s.shape[0], LANES))

    def lane_tile(m, width):
        return jnp.concatenate([m] * (width // LANES), axis=1)

    for u in tiles:
        m_scr[u] = jnp.full(m_scr.shape[1:], NEG, F32)
        a_scr[u] = jnp.ones(a_scr.shape[1:], F32)
        acc_scr[u] = jnp.zeros(acc_scr.shape[1:], F32)
        p_scr[u, 1] = jnp.zeros(p_scr.shape[2:], BF16)
    for u in tiles:
        s_scr[u, 0] = logits(u, lst_ref[bases[u]])

    def flash_step(us, i, slot):
        for u in us:
            acc_scr[u] = a_scr[u] * acc_scr[u] + values(p_scr[u, 1 - slot], lst_ref[bases[u] + jnp.maximum(i - 1, 0)])
        for u in us:
            s = s_scr[u, slot]
            m_prev = m_scr[u]
            m_new = jnp.maximum(m_prev, row_max(s))
            a_scr[u] = jnp.exp(m_prev - m_new)
            p_scr[u, slot] = jnp.exp(s - lane_tile(m_new, chunk)).astype(BF16)
            m_scr[u] = m_new
        for u in us:
            s_scr[u, 1 - slot] = logits(u, lst_ref[bases[u] + i + 1])

    def single_steps(us):
        def body(i, carry):
            flash_step(us, i, i & 1)
            return carry
        return body

    def double_steps(us):
        def body(j, carry):
            flash_step(us, 2 * j, 0)
            flash_step(us, 2 * j + 1, 1)
            return carry
        return body

    joint = functools.reduce(jnp.minimum, cnts)
    lax.fori_loop(0, joint // 2, double_steps(list(tiles)), 0)
    lax.fori_loop(joint - (joint & 1), joint, single_steps(list(tiles)), 0)
    for u in tiles:
        lax.fori_loop(joint, cnts[u], single_steps([u]), 0)

    slots = [cnts[u] & 1 for u in tiles]
    c_diag = [t0s[u] // chunk for u in tiles]
    acc = [a_scr[u] * acc_scr[u] + values(p_scr[u, 1 - slots[u]], lst_ref[bases[u] + jnp.maximum(cnts[u] - 1, 0)])
           for u in tiles]
    nw = WINDOW + nq
    wbase = [pl.multiple_of(jnp.maximum(t0s[u] - WINDOW, 0), nq) for u in tiles]
    s_win = [_mm_nt(qs[u], kw_ref[pl.ds(wbase[u], nw), :]) for u in tiles]

    def masked(s, ok):
        return jnp.concatenate([jnp.where(ok, s[r * nq:(r + 1) * nq], NEG) for r in range(NSA_REP)], axis=0)

    s_diag = []
    for u in tiles:
        pos = c_diag[u] * chunk + lax.broadcasted_iota(jnp.int32, (nq, chunk), 1)
        tq = t0s[u] + lax.broadcasted_iota(jnp.int32, (nq, chunk), 0)
        s_diag.append(masked(s_scr[u, slots[u]], pos <= tq))
    m_prev = [m_scr[u] for u in tiles]
    m_new = [jnp.maximum(m_prev[u], row_max(s_diag[u])) for u in tiles]
    acc = [jnp.exp(m_prev[u] - m_new[u]) * acc[u]
           + values(jnp.exp(s_diag[u] - lane_tile(m_new[u], chunk)).astype(BF16), c_diag[u]) for u in tiles]
    p_win = []
    for u in tiles:
        pos = wbase[u] + lax.broadcasted_iota(jnp.int32, (nq, nw), 1)
        tq = t0s[u] + lax.broadcasted_iota(jnp.int32, (nq, nw), 0)
        s = masked(s_win[u], (pos <= tq) & (pos > tq - WINDOW))
        p_win.append(jnp.exp(s - jnp.max(s, axis=-1, keepdims=True)).astype(BF16))
    acc_w = [jnp.dot(p_win[u], vw_ref[pl.ds(wbase[u], nw), :], preferred_element_type=F32) for u in tiles]
    for u in tiles:
        gates = gates_ref[rws[u], :]
        o_ref[rws[u], :] = (yc_ref[rws[u], :] + _finish_heads(acc[u], _gate_cols(gates, 1))
                            + _finish_heads(acc_w[u], _gate_cols(gates, 2))).astype(o_ref.dtype)


def _chunk_lists(act, seq, chunk):
    bsz, ngrp = act.shape[:2]
    nqt = seq // ATTN_Q_TILE
    nch = seq // chunk
    active = act.reshape(bsz, ngrp, nqt, ATTN_Q_TILE // Q_TILE, nch, chunk // SEL_LEN).max(axis=(3, 5)) > 0.0
    c_diag = (jnp.arange(nqt) * ATTN_Q_TILE) // chunk
    active = active & (jnp.arange(nch)[None, :] < c_diag[:, None])
    cnt = active.sum(axis=-1).astype(jnp.int32)
    order = jnp.argsort(jnp.logical_not(active), axis=-1, stable=True).astype(jnp.int32)
    lst = jnp.where(jnp.arange(nch) < cnt[..., None], order, c_diag[:, None].astype(jnp.int32))
    return lst.reshape(-1), cnt.reshape(-1)


def _attn_call(lst, cnt, slopes, main, sel, kv, yc, et, bsz, seq, chunk):
    tq = ATTN_Q_TILE * ATTN_TILES
    nqt = seq // tq
    nb = sel.shape[-1]
    rows = NSA_REP * ATTN_Q_TILE
    row = lambda b, g, q: b * nqt + q
    kvspec = lambda blk: pl.BlockSpec((seq, LANES), lambda b, g, q, *_: (b, blk + g))
    grid_spec = pltpu.PrefetchScalarGridSpec(
        num_scalar_prefetch=2,
        grid=(bsz, NSA_GROUPS, nqt),
        in_specs=[pl.BlockSpec(memory_space=pltpu.SMEM),
                  pl.BlockSpec((tq, 256), lambda b, g, q, *_: (row(b, g, q), COL_Q // 256 + g)),
                  pl.BlockSpec((1, 1, tq, nb), lambda b, g, q, *_: (b, g, q, 0)),
                  kvspec(KVB_KS), kvspec(KVB_VS), kvspec(KVB_KW), kvspec(KVB_VW),
                  pl.BlockSpec((tq, LANES), lambda b, g, q, *_: (row(b, g, q), COL_GATE // LANES + g)),
                  pl.BlockSpec((tq, 256), lambda b, g, q, *_: (row(b, g, q), g)),
                  pl.BlockSpec(et.shape, lambda b, g, q, *_: (0, 0))],
        out_specs=pl.BlockSpec((tq, 256), lambda b, g, q, *_: (row(b, g, q), g)),
        scratch_shapes=[pltpu.VMEM((ATTN_TILES, 2, rows, chunk), F32),
                        pltpu.VMEM((ATTN_TILES, 2, rows, chunk), BF16),
                        pltpu.VMEM((ATTN_TILES, rows, LANES), F32),
                        pltpu.VMEM((ATTN_TILES, rows, LANES), F32),
                        pltpu.VMEM((ATTN_TILES, rows, LANES), F32)],
    )
    return pl.pallas_call(
        functools.partial(_attn_kernel, chunk=chunk),
        grid_spec=grid_spec,
        out_shape=jax.ShapeDtypeStruct((bsz * seq, NSA_Q_W), BF16),
        compiler_params=_cparams(("parallel", "parallel", "parallel")),
        name="attn",
    )(lst, cnt, slopes, main, sel, kv, kv, kv, kv, main, yc, et)


def _pool_kernel(x_ref, halo_ref, w_ref, sc_ref, o_ref):
    s = pl.program_id(1)
    ts = x_ref.shape[0]
    x = x_ref[...]
    halo = jnp.where(s > 0, halo_ref[...], 0.0)
    t = s * ts + lax.broadcasted_iota(jnp.int32, (ts, 1), 0)
    for gi, win in enumerate(POOL_WINDOWS):
        lanes = slice(gi * POOL_GROUP_DIM, (gi + 1) * POOL_GROUP_DIM)
        ext = jnp.concatenate([halo[:, lanes], x[:, lanes]], axis=0)
        acc = ext
        span = 1
        while span < win:
            acc = acc + pltpu.roll(acc, span, 0)
            span *= 2
        cnt = jnp.minimum(t + 1, win).astype(F32)
        y = acc[POOL_HALO:] / cnt - x[:, lanes]
        o_ref[:, lanes] = (_mm(y, w_ref[gi]) * sc_ref[:, lanes]).astype(o_ref.dtype)


def _pool_call(main, w, scale, bsz, seq):
    ts = min(POOL_TS, seq)
    per_b = seq // ts
    halo_per_tile = ts // POOL_HALO
    return pl.pallas_call(
        _pool_kernel,
        grid=(bsz, per_b),
        in_specs=[pl.BlockSpec((ts, POOL_W), lambda b, s: (b * per_b + s, COL_POOL // POOL_W)),
                  pl.BlockSpec((POOL_HALO, POOL_W),
                               lambda b, s: (jnp.maximum((b * per_b + s) * halo_per_tile - 1, 0), COL_POOL // POOL_W)),
                  pl.BlockSpec(w.shape, lambda b, s: (0, 0, 0)),
                  pl.BlockSpec((1, POOL_W), lambda b, s: (0, 0))],
        out_specs=pl.BlockSpec((ts, POOL_W), lambda b, s: (b * per_b + s, 0)),
        out_shape=jax.ShapeDtypeStruct((bsz * seq, POOL_W), BF16),
        compiler_params=_cparams(("parallel", "parallel")),
        name="pool",
    )(main, main, w, scale.reshape(1, POOL_W))


def _unit_lower_inverses(nms, ri, ci):
    c = nms[0].shape[0]
    eye = jnp.where(ri == ci, 1.0, 0.0)
    in8 = jnp.right_shift(ri, 3) == jnp.right_shift(ci, 3)
    n8 = [jnp.where(in8, nm, 0.0) for nm in nms]
    ps = [eye - a for a in n8]
    ms = [_mm(a, a) for a in n8]
    ps = [p + _mm(p, m) for p, m in zip(ps, ms)]
    ms = [_mm(m, m) for m in ms]
    ps = [p + _mm(p, m) for p, m in zip(ps, ms)]
    shift = 3
    while (1 << shift) < c:
        rb = jnp.right_shift(ri, shift)
        cb = jnp.right_shift(ci, shift)
        lower_left = ((rb & 1) == 1) & (cb == rb - 1)
        ts = [_mm(p, jnp.where(lower_left, nm, 0.0)) for p, nm in zip(ps, nms)]
        ps = [p - _mm(t, p) for p, t in zip(ps, ts)]
        shift += 1
    return ps


def _row_cumsum(x, ri):
    span = 1
    while span < x.shape[0]:
        x = x + jnp.where(ri >= span, pltpu.roll(x, span, 0), 0.0)
        span *= 2
    return x


def _dn_kernel(qkv_ref, halo_ref, z_ref, gba_ref, cw_ref, alog_ref, dtb_ref, ng_ref, o_ref, st_scr):
    s = pl.program_id(1)
    nbatch, c = qkv_ref.shape[0], qkv_ref.shape[1]

    @pl.when(s == 0)
    def _():
        st_scr[...] = jnp.zeros(st_scr.shape, F32)

    ri = lax.broadcasted_iota(jnp.int32, (c, c), 0)
    ci = lax.broadcasted_iota(jnp.int32, (c, c), 1)
    tril = ri >= ci
    strict = ri > ci
    cw = cw_ref[...]
    ng = ng_ref[...]
    chains = [(bi, h) for bi in range(nbatch) for h in range(DN_HEADS)]
    qn, kn, kb, vb, gcb, lmask = [], [], [], [], [], []
    for bi in range(nbatch):
        halo = jnp.where(s > 0, halo_ref[bi], 0.0)
        ext = jnp.concatenate([halo, qkv_ref[bi]], axis=0)
        conv = cw[DN_CONV - 1:DN_CONV] * ext
        for tap in range(1, DN_CONV):
            conv = conv + cw[DN_CONV - 1 - tap:DN_CONV - tap] * pltpu.roll(ext, tap, 0)
        act = _silu(conv[DN_HALO:])

        gba = gba_ref[bi]
        beta_all = jax.nn.sigmoid(gba)
        xs = gba + dtb_ref[...]
        softplus = jnp.maximum(xs, 0.0) + jnp.log(1.0 + jnp.exp(-jnp.abs(xs)))
        gc_all = _row_cumsum(-jnp.exp(alog_ref[...]) * softplus, ri)
        gct_all = gc_all.T
        for h in range(DN_HEADS):
            q = act[:, h * DN_DK:(h + 1) * DN_DK]
            k = act[:, DN_W + h * DN_DK:DN_W + (h + 1) * DN_DK]
            v = act[:, 2 * DN_W + h * DN_DK:2 * DN_W + (h + 1) * DN_DK]
            qn.append(q * lax.rsqrt(jnp.sum(q * q, axis=-1, keepdims=True) + EPS) * (DN_DK ** -0.5))
            kn.append(k * lax.rsqrt(jnp.sum(k * k, axis=-1, keepdims=True) + EPS))
            beta = beta_all[:, LANE_BETA + h:LANE_BETA + h + 1]
            gcb.append(jnp.broadcast_to(gc_all[:, LANE_DECAY + h:LANE_DECAY + h + 1], (c, DN_DK)))
            gct = jnp.broadcast_to(gct_all[LANE_DECAY + h:LANE_DECAY + h + 1, :], (c, c))
            lmask.append(jnp.where(tril, jnp.exp(jnp.where(tril, gcb[-1] - gct, 0.0)), 0.0))
            kb.append(kn[-1] * beta)
            vb.append(v * beta)

    n = range(len(chains))
    kk = [_mm_nt(kb[i], kn[i]) for i in n]
    tinv = _unit_lower_inverses([jnp.where(strict, kk[i] * lmask[i], 0.0) for i in n], ri, ci)
    egc = [jnp.exp(gcb[i]) for i in n]
    sol = [_mm(tinv[i], jnp.concatenate([vb[i], kb[i] * egc[i]], axis=1)) for i in n]
    aqk = [_mm_nt(qn[i], kn[i]) * lmask[i] for i in n]
    state = [st_scr[i] for i in n]
    glast = [gcb[i][c - 1:c, :] for i in n]
    v_new = [sol[i][:, :DN_DK] - _mm(sol[i][:, DN_DK:], state[i]) for i in n]
    o_state = [_mm(qn[i] * egc[i], state[i]) for i in n]
    o = [o_state[i] + _mm(aqk[i], v_new[i]) for i in n]
    upd = [_mm_tn(kn[i] * jnp.exp(glast[i] - gcb[i]), v_new[i]) for i in n]
    for i, (bi, h) in enumerate(chains):
        st_scr[i] = state[i] * jnp.exp(glast[i]) + upd[i]
        on = o[i] * lax.rsqrt(jnp.mean(o[i] * o[i], axis=-1, keepdims=True) + EPS) * ng
        o_ref[bi, :, h * DN_DK:(h + 1) * DN_DK] = (
            on * _silu(z_ref[bi, :, h * DN_DK:(h + 1) * DN_DK])).astype(o_ref.dtype)


def _dn_call(main3, conv_w, alog_row, dtb_row, norm_g):
    bsz, seq, _ = main3.shape
    c = DN_CHUNK
    nbt = DN_BATCH if bsz % DN_BATCH == 0 else 1
    halo_per_chunk = c // DN_HALO
    w3 = 3 * DN_W
    return pl.pallas_call(
        _dn_kernel,
        grid=(bsz // nbt, seq // c),
        in_specs=[pl.BlockSpec((nbt, c, w3), lambda b, s: (b, s, COL_QKV // w3)),
                  pl.BlockSpec((nbt, DN_HALO, w3),
                               lambda b, s: (b, jnp.maximum(s * halo_per_chunk - 1, 0), COL_QKV // w3)),
                  pl.BlockSpec((nbt, c, DN_W), lambda b, s: (b, s, COL_Z // DN_W)),
                  pl.BlockSpec((nbt, c, LANES), lambda b, s: (b, s, COL_GATE // LANES)),
                  pl.BlockSpec((SUBLANES, w3), lambda b, s: (0, 0)),
                  pl.BlockSpec((1, LANES), lambda b, s: (0, 0)),
                  pl.BlockSpec((1, LANES), lambda b, s: (0, 0)),
                  pl.BlockSpec((1, DN_DK), lambda b, s: (0, 0))],
        out_specs=pl.BlockSpec((nbt, c, DN_W), lambda b, s: (b, s, 0)),
        out_shape=jax.ShapeDtypeStruct((bsz, seq, DN_W), BF16),
        scratch_shapes=[pltpu.VMEM((nbt * DN_HEADS, DN_DK, DN_DK), F32)],
        compiler_params=_cparams(("parallel", "arbitrary")),
        name="dn",
    )(main3, main3, main3, main3, conv_w, alog_row, dtb_row, norm_g.reshape(1, DN_DK))


def _merge_kernel(ya_ref, yb_ref, yc_ref, mg_ref, x_ref, gt_ref, wa_ref, wb_ref, wc_ref, wo_ref, o_ref):
    d = x_ref.shape[1]
    gates = jax.nn.sigmoid(mg_ref[...].astype(F32))
    merged = (gates[:, :d] * jnp.dot(ya_ref[...], wa_ref[...], preferred_element_type=F32)
              + gates[:, d:2 * d] * jnp.dot(yb_ref[...], wb_ref[...], preferred_element_type=F32)
              + gates[:, 2 * d:] * jnp.dot(yc_ref[...], wc_ref[...], preferred_element_type=F32))
    y = jnp.dot(merged.astype(BF16), wo_ref[...], preferred_element_type=F32)
    o_ref[...] = x_ref[...] + gt_ref[0] * y


def _merge_call(ya, yb, yc, mg, x2, gt, wa, wb, wc, wo, seq):
    t, d = x2.shape
    tm = min(MERGE_TM, seq)
    per_b = seq // tm
    full = lambda a: pl.BlockSpec(a.shape, lambda i: (0, 0))
    return pl.pallas_call(
        _merge_kernel,
        grid=(t // tm,),
        in_specs=[pl.BlockSpec((tm, ya.shape[1]), lambda i: (i, 0)),
                  pl.BlockSpec((tm, yb.shape[1]), lambda i: (i, 0)),
                  pl.BlockSpec((tm, yc.shape[1]), lambda i: (i, 0)),
                  pl.BlockSpec((tm, mg.shape[1]), lambda i: (i, 0)),
                  pl.BlockSpec((tm, d), lambda i: (i, 0)),
                  pl.BlockSpec((1, 1, d), lambda i: (i // per_b, 0, 0)),
                  full(wa), full(wb), full(wc), full(wo)],
        out_specs=pl.BlockSpec((tm, d), lambda i: (i, 0)),
        out_shape=jax.ShapeDtypeStruct((t, d), F32),
        compiler_params=_cparams(("parallel",)),
        name="merge",
    )(ya, yb, yc, mg, x2, gt, wa, wb, wc, wo)


def _mlp_kernel(x_ref, g_ref, sc_ref, sh_ref, gt_ref, w1_ref, w2_ref, fg_ref, o_ref, acc_scr, *, final):
    j = pl.program_id(1)

    @pl.when((pl.program_id(0) == 0) & (j == 0))
    def _():
        acc_scr[...] = jnp.zeros(acc_scr.shape, F32)

    x = x_ref[...]
    h = _norm_mod(x, g_ref[...], sc_ref[0], sh_ref[0]).astype(BF16)
    a = jnp.maximum(jnp.dot(h, w1_ref[...], preferred_element_type=F32), 0.0)
    acc = acc_scr[...] + jnp.dot((a * a).astype(BF16), w2_ref[...], preferred_element_type=F32)
    acc_scr[...] = jnp.where(j == pl.num_programs(1) - 1, 0.0, acc)
    y = x + gt_ref[0] * acc
    if final:
        ms = jnp.mean(y * y, axis=-1, keepdims=True)
        y = y * lax.rsqrt(ms + EPS) * fg_ref[...]
    o_ref[...] = y


def _mlp_call(x2, g, sc, sh, gt, w1, w2, fg, seq, final):
    t, d = x2.shape
    f = w1.shape[1]
    tm = min(MLP_TM, seq)
    tf = MLP_TF
    per_b = seq // tm
    mod = lambda: pl.BlockSpec((1, 1, d), lambda i, j: (i // per_b, 0, 0))
    return pl.pallas_call(
        functools.partial(_mlp_kernel, final=final),
        grid=(t // tm, f // tf),
        in_specs=[pl.BlockSpec((tm, d), lambda i, j: (i, 0)),
                  pl.BlockSpec((1, d), lambda i, j: (0, 0)),
                  mod(), mod(), mod(),
                  pl.BlockSpec((d, tf), lambda i, j: (0, j)),
                  pl.BlockSpec((tf, d), lambda i, j: (j, 0)),
                  pl.BlockSpec((1, d), lambda i, j: (0, 0))],
        out_specs=pl.BlockSpec((tm, d), lambda i, j: (i, 0)),
        out_shape=jax.ShapeDtypeStruct((t, d), F32),
        scratch_shapes=[pltpu.VMEM((tm, d), F32)],
        compiler_params=_cparams(("arbitrary", "arbitrary")),
        name="mlp",
    )(x2, g.reshape(1, d), sc, sh, gt, w1, w2, fg.reshape(1, d))


def _split_w_in(w_in):
    offs = [0]
    for sz in IN_SIZES:
        offs.append(offs[-1] + sz)
    return [w_in[:, offs[i]:offs[i + 1]] for i in range(len(IN_SIZES))]


def _layout_w_in(w_in):
    nq, nkc, nvc, nks, nvs, nkw, nvw, ngate, pin, dqkv, dz, dbeta, da, mg = _split_w_in(w_in)
    d = w_in.shape[0]
    per_g = 3 * NSA_REP
    zeros = lambda n: jnp.zeros((d, n), w_in.dtype)
    gate0 = jnp.concatenate([ngate[:, :per_g], zeros(LANE_BETA - per_g), dbeta, da,
                             zeros(LANES - LANE_DECAY - DN_HEADS)], axis=1)
    gate1 = jnp.concatenate([ngate[:, per_g:], zeros(LANES - per_g)], axis=1)
    main = jnp.concatenate([nq, pin, dz, dqkv, gate0, gate1, nkc, nvc], axis=1)

    def per_group(w):
        pad = zeros(LANES - NSA_DH)
        return jnp.concatenate([w[:, :NSA_DH], pad, w[:, NSA_DH:], pad], axis=1)

    kv = jnp.concatenate([per_group(nks), per_group(nvs), per_group(nkw), per_group(nvw)], axis=1)
    return main.astype(BF16), mg.astype(BF16), kv.astype(BF16)


def _group_lane_w1(w1):
    two, _, hid = w1.shape
    halves = w1.reshape(two, 2, CMP_STRIDE, NSA_DH, hid)
    out = []
    for g in range(NSA_GROUPS):
        padded = jnp.zeros((two, 2, CMP_STRIDE, LANES, hid), w1.dtype).at[:, :, :, g * NSA_DH:(g + 1) * NSA_DH].set(halves)
        out.append(padded.reshape(two, 2, CMP_STRIDE * LANES, hid))
    return jnp.stack(out, axis=1)


def _nsa_constants(seq):
    nb = seq // SEL_LEN
    ncp = seq // CMP_STRIDE
    j = jnp.arange(nb)[:, None] * SEL_LEN
    i = jnp.arange(ncp)[None, :] * CMP_STRIDE
    overlap = jnp.clip(jnp.minimum(j + SEL_LEN, i + CMP_LEN) - jnp.maximum(j, i), 0) // CMP_STRIDE
    block_onehot = (jnp.arange(seq)[:, None] // SEL_LEN) == jnp.arange(nb)[None, :]
    return overlap.astype(BF16), block_onehot.astype(BF16)


def kernel(x, c, ada_w, ada_b, norm1_g, norm2_g, w_in, phi_k1, phi_k2, phi_v1, phi_v2, pos_k, pos_v, pool_w, pool_scale, dn_conv_w, dn_A_log, dn_dt_bias, dn_norm_g, w_branch_nsa, w_branch_pool, w_branch_dn, w_out, mlp_w1, mlp_w2, final_g):
    bsz, seq, d = x.shape
    assert d == D_MODEL and seq % max(PROJ_TM, MLP_TM, POOL_TS) == 0 and seq % KV_CHUNK == 0
    assert DN_CHUNK == DN_DK == LANES
    t = bsz * seq
    depth = ada_w.shape[0]
    chunk = KV_CHUNK
    nqt = seq // Q_TILE
    nb = seq // SEL_LEN
    blocks_per_chunk = chunk // SEL_LEN

    slopes = 2.0 ** (-(8.0 / NSA_HEADS) * (jnp.arange(NSA_HEADS, dtype=F32) + 1.0))
    overlap, block_onehot = _nsa_constants(seq)
    c_pad = jnp.zeros((SUBLANES, d), F32).at[:bsz].set(c)
    x2 = x.reshape(t, d)

    for l in range(depth):
        mod = _mod_call(c_pad, ada_w[l], ada_b[l])[:bsz]
        sh1, sc1, gt1, sh2, sc2, gt2 = [m.reshape(bsz, 1, d) for m in jnp.split(mod, 6, axis=-1)]

        w_main, w_mg, w_kv = _layout_w_in(w_in[l])
        main = _proj_call(x2, norm1_g[l], sc1, sh1, w_main, seq, PROJ_MAIN_TM, F32, "proj_main")
        mg = _proj_call(x2, norm1_g[l], sc1, sh1, w_mg, seq, PROJ_TM, BF16, "proj_mg")
        kv = _proj_kv_call(x2, norm1_g[l], sc1, sh1, w_kv, seq)

        main3 = main.reshape(bsz, seq, N_MAIN)
        pos = jnp.zeros((2, SUBLANES, CMP_LEN * NSA_DH), F32).at[:, 0].set(
            jnp.stack([pos_k[l].reshape(-1), pos_v[l].reshape(-1)])).astype(BF16)
        w1 = jnp.stack([phi_k1[l], phi_v1[l]]).astype(BF16)
        w2 = jnp.stack([phi_k2[l], phi_v2[l]])
        w2 = jnp.concatenate([w2, w2], axis=-1).astype(BF16)
        cmp_kv = _compress_call(main3, pos, w1, _group_lane_w1(w1), w2)

        sel, act, yc = _select_call(slopes, main, cmp_kv[0], cmp_kv[1], overlap, bsz, seq)
        lst, cnt = _chunk_lists(act, seq, chunk)
        y_a = _attn_call(lst, cnt, slopes, main, sel, kv, yc, block_onehot, bsz, seq, chunk)

        y_b = _pool_call(main, pool_w[l].astype(BF16), pool_scale[l], bsz, seq)

        conv_w = jnp.zeros((SUBLANES, 3 * DN_W), F32).at[:DN_CONV].set(dn_conv_w[l])
        lane_row = lambda v: jnp.zeros((1, LANES), F32).at[0, LANE_DECAY:LANE_DECAY + DN_HEADS].set(v)
        y_c = _dn_call(main3, conv_w, lane_row(dn_A_log[l]), lane_row(dn_dt_bias[l]),
                       dn_norm_g[l]).reshape(t, DN_W)

        x2 = _merge_call(y_a, y_b, y_c, mg, x2, gt1,
                         w_branch_nsa[l].astype(BF16), w_branch_pool[l].astype(BF16),
                         w_branch_dn[l].astype(BF16), w_out[l].astype(BF16), seq)
        x2 = _mlp_call(x2, norm2_g[l], sc2, sh2, gt2, mlp_w1[l].astype(BF16), mlp_w2[l].astype(BF16),
                       final_g, seq, final=(l == depth - 1))
    return x2.reshape(bsz, seq, d)
```

```python
import functools

import jax
import jax.numpy as jnp
from jax import lax
from jax.experimental import pallas as pl
from jax.experimental.pallas import tpu as pltpu

F32 = jnp.float32
BF16 = jnp.bfloat16
HIGHEST = lax.Precision.HIGHEST

D_MODEL = 1024
NSA_HEADS = 8
NSA_GROUPS = 2
NSA_REP = NSA_HEADS // NSA_GROUPS
NSA_DH = 64
CMP_LEN = 32
CMP_STRIDE = 16
SEL_LEN = 64
SEL_SHIFT = 6
TOP_N = 16
WINDOW = 512
FORCE_BONUS = 1.0e4
POOL_WINDOWS = (2, 4, 8, 16)
POOL_GROUP_DIM = 128
POOL_W = len(POOL_WINDOWS) * POOL_GROUP_DIM
DN_HEADS = 4
DN_DK = 128
DN_W = DN_HEADS * DN_DK
DN_CONV = 4
EPS = 1e-6
NEG = -1e30
NSA_Q_W = NSA_HEADS * NSA_DH
NSA_KV_W = NSA_GROUPS * NSA_DH
IN_SIZES = (NSA_Q_W, NSA_KV_W, NSA_KV_W, NSA_KV_W, NSA_KV_W, NSA_KV_W, NSA_KV_W,
            3 * NSA_HEADS, POOL_W, 3 * DN_W, DN_W, DN_HEADS, DN_HEADS, 3 * D_MODEL)

LANES = 128
LANE_SHIFT = 7
SUBLANES = 8
VMEM_LIMIT_BYTES = 48 * 1024 * 1024

Q_TILE = 128
SELECT_BUCKETS = 4
ATTN_Q_TILE = 128
ATTN_TILES = 2
ATTN_GROUPS = 4
KV_CHUNK = 256
DN_CHUNK = 128
DN_BATCH = 4
PROJ_TM = 1024
MLP_TM = 1024
MLP_TF = 2048
MERGE_TM = 512
POOL_TS = 1024
POOL_HALO = 16
DN_HALO = 8

COL_Q = 0
COL_POOL = 512
COL_Z = 1024
COL_QKV = 1536
COL_GATE = 3072
COL_KC = 3328
COL_VC = 3456
N_MAIN = 3584
PROJ_MAIN_TM = 512
N_MG = 3072
LANE_BETA = 16
LANE_DECAY = 20
KVB_KS = 0
KVB_VS = 2
KVB_KW = 4
KVB_VW = 6
KVB_END = 8
N_KV = KVB_END * LANES
LANE_POS_HI = NSA_DH
LANE_POS_LO = NSA_DH + 1


def _cparams(sem):
    return pltpu.CompilerParams(dimension_semantics=sem, vmem_limit_bytes=VMEM_LIMIT_BYTES)


def _silu(v):
    return v * jax.nn.sigmoid(v)


def _mm(a, b):
    return jnp.dot(a.astype(BF16), b.astype(BF16), preferred_element_type=F32)


def _mm_nt(a, b):
    return lax.dot_general(a.astype(BF16), b.astype(BF16), (((1,), (1,)), ((), ())), preferred_element_type=F32)


def _mm_tn(a, b):
    return lax.dot_general(a.astype(BF16), b.astype(BF16), (((0,), (0,)), ((), ())), preferred_element_type=F32)


def _mod_kernel(c_ref, w_ref, b_ref, o_ref):
    cond = _silu(c_ref[...])
    o_ref[...] = jnp.dot(cond, w_ref[...], preferred_element_type=F32, precision=HIGHEST) + b_ref[...]


def _mod_call(c_pad, w, b):
    d, n = w.shape
    tn = 1536
    return pl.pallas_call(
        _mod_kernel,
        grid=(n // tn,),
        in_specs=[pl.BlockSpec((SUBLANES, d), lambda j: (0, 0)),
                  pl.BlockSpec((d, tn), lambda j: (0, j)),
                  pl.BlockSpec((1, tn), lambda j: (0, j))],
        out_specs=pl.BlockSpec((SUBLANES, tn), lambda j: (0, j)),
        out_shape=jax.ShapeDtypeStruct((SUBLANES, n), F32),
        compiler_params=_cparams(("parallel",)),
        name="mod",
    )(c_pad, w, b.reshape(1, n))


def _norm_mod(x, g, sc, sh):
    ms = jnp.mean(x * x, axis=-1, keepdims=True)
    return (x * lax.rsqrt(ms + EPS) * g) * (1.0 + sc) + sh


def _proj_kernel(x_ref, g_ref, sc_ref, sh_ref, w_ref, o_ref):
    h = _norm_mod(x_ref[...], g_ref[...], sc_ref[0], sh_ref[0]).astype(BF16)
    o_ref[...] = jnp.dot(h, w_ref[...], preferred_element_type=F32).astype(o_ref.dtype)


def _proj_kv_kernel(x_ref, g_ref, sc_ref, sh_ref, w_ref, o_ref, *, tiles_per_seq):
    h = _norm_mod(x_ref[...], g_ref[...], sc_ref[0], sh_ref[0]).astype(BF16)
    acc = jnp.dot(h, w_ref[...], preferred_element_type=F32)
    tm, n = acc.shape
    col = lax.broadcasted_iota(jnp.int32, (1, n), 1)
    blk = jnp.right_shift(col, LANE_SHIFT)
    lane = col & (LANES - 1)
    is_key = ((blk >= KVB_KS) & (blk < KVB_VS)) | ((blk >= KVB_KW) & (blk < KVB_VW))
    is_val = ((blk >= KVB_VS) & (blk < KVB_KW)) | ((blk >= KVB_VW) & (blk < KVB_END))
    ones_row = jnp.where(is_val & (lane >= NSA_DH), 1.0, 0.0)
    hi_row = jnp.where(is_key & (lane == LANE_POS_HI), 1.0, 0.0)
    lo_row = jnp.where(is_key & (lane == LANE_POS_LO), 1.0, 0.0)
    pos = (pl.program_id(0) % tiles_per_seq) * tm + lax.broadcasted_iota(jnp.int32, (tm, 1), 0)
    pos_hi = jnp.right_shift(pos, LANE_SHIFT).astype(F32)
    pos_lo = (pos & (LANES - 1)).astype(F32)
    o_ref[...] = (acc + ones_row + pos_hi * hi_row + pos_lo * lo_row).astype(o_ref.dtype)


def _proj_specs(tm, d, tn, per_b):
    return [pl.BlockSpec((tm, d), lambda i, j: (i, 0)),
            pl.BlockSpec((1, d), lambda i, j: (0, 0)),
            pl.BlockSpec((1, 1, d), lambda i, j: (i // per_b, 0, 0)),
            pl.BlockSpec((1, 1, d), lambda i, j: (i // per_b, 0, 0)),
            pl.BlockSpec((d, tn), lambda i, j: (0, j))]


def _proj_call(x2, g, sc, sh, w, seq, tm, out_dtype, name):
    t, d = x2.shape
    n = w.shape[1]
    tm = min(tm, seq)
    return pl.pallas_call(
        _proj_kernel,
        grid=(t // tm, 1),
        in_specs=_proj_specs(tm, d, n, seq // tm),
        out_specs=pl.BlockSpec((tm, n), lambda i, j: (i, j)),
        out_shape=jax.ShapeDtypeStruct((t, n), out_dtype),
        compiler_params=_cparams(("parallel", "arbitrary")),
        name=name,
    )(x2, g.reshape(1, d), sc, sh, w)


def _proj_kv_call(x2, g, sc, sh, w, seq):
    t, d = x2.shape
    n = w.shape[1]
    tm = min(PROJ_TM, seq)
    return pl.pallas_call(
        functools.partial(_proj_kv_kernel, tiles_per_seq=seq // tm),
        grid=(t // tm, 1),
        in_specs=_proj_specs(tm, d, n, seq // tm),
        out_specs=pl.BlockSpec((tm, n), lambda i, j: (i, j)),
        out_shape=jax.ShapeDtypeStruct((t, n), BF16),
        compiler_params=_cparams(("parallel", "arbitrary")),
        name="proj_kv",
    )(x2, g.reshape(1, d), sc, sh, w)


def _compress_kernel(x_ref, pos_ref, w1_ref, w1g_ref, w2_ref, o_ref):
    nr = x_ref.shape[1] // CMP_STRIDE
    xs = jnp.concatenate([x_ref[0, pl.ds(rho, nr, stride=CMP_STRIDE), :].astype(BF16) for rho in range(CMP_STRIDE)],
                         axis=1)
    posb = jnp.dot(pos_ref[0], w1_ref[0], preferred_element_type=F32)[0:1]
    for g in range(NSA_GROUPS):
        top = jnp.dot(xs, w1g_ref[0, g, 0], preferred_element_type=F32)
        bot = jnp.dot(xs, w1g_ref[0, g, 1], preferred_element_type=F32)
        hid = top + pltpu.roll(bot, nr - 1, 0) + posb
        o_ref[0, 0, g] = _mm(_silu(hid), w2_ref[0]).astype(o_ref.dtype)


def _compress_call(main3, pos, w1, w1g, w2):
    b, seq, _ = main3.shape
    nr = seq // CMP_STRIDE
    two, l64, hid = w1.shape
    return pl.pallas_call(
        _compress_kernel,
        grid=(two, b),
        in_specs=[pl.BlockSpec((1, seq, LANES), lambda a, i: (i, 0, COL_KC // LANES + a)),
                  pl.BlockSpec((1, SUBLANES, l64), lambda a, i: (a, 0, 0)),
                  pl.BlockSpec((1, l64, hid), lambda a, i: (a, 0, 0)),
                  pl.BlockSpec((1,) + w1g.shape[1:], lambda a, i: (a, 0, 0, 0, 0)),
                  pl.BlockSpec((1, hid, LANES), lambda a, i: (a, 0, 0))],
        out_specs=pl.BlockSpec((1, 1, NSA_GROUPS, nr, LANES), lambda a, i: (a, i, 0, 0, 0)),
        out_shape=jax.ShapeDtypeStruct((two, b, NSA_GROUPS, nr, LANES), BF16),
        compiler_params=_cparams(("parallel", "parallel")),
        name="compress",
    )(main3, pos, w1, w1g, w2)


def _gate_cols(gate_logits, branch):
    sg = jax.nn.sigmoid(gate_logits)
    return [sg[:, 3 * r + branch:3 * r + branch + 1] for r in range(NSA_REP)]


def _stack_heads_dup(q):
    qa = q[:, :LANES]
    qb = q[:, LANES:]
    lo = lax.broadcasted_iota(jnp.int32, qa.shape, 1) < NSA_DH
    z = jnp.zeros_like(qa)
    return jnp.concatenate([jnp.where(lo, qa, z), jnp.where(lo, z, qa),
                            jnp.where(lo, qb, z), jnp.where(lo, z, qb)], axis=0)


def _unstack_heads_dup(o, gates):
    nq = o.shape[0] // NSA_REP
    lo = lax.broadcasted_iota(jnp.int32, (nq, LANES), 1) < NSA_DH
    a = jnp.where(lo, gates[0] * o[0:nq], gates[1] * o[nq:2 * nq])
    b = jnp.where(lo, gates[2] * o[2 * nq:3 * nq], gates[3] * o[3 * nq:])
    return jnp.concatenate([a, b], axis=1)


def _stack_heads_pos(q, slopes):
    qa = q[:, :LANES]
    qb = q[:, LANES:]
    lane = lax.broadcasted_iota(jnp.int32, qa.shape, 1)
    lo = lane < NSA_DH
    heads = [qa, pltpu.roll(qa, NSA_DH, 1), qb, pltpu.roll(qb, NSA_DH, 1)]
    out = []
    for r in range(NSA_REP):
        extra = jnp.where(lane == LANE_POS_HI, slopes[r] * LANES, jnp.where(lane == LANE_POS_LO, slopes[r], 0.0))
        out.append(jnp.where(lo, heads[r], extra))
    return jnp.concatenate(out, axis=0)


def _finish_heads(acc, gates):
    nq = acc.shape[0] // NSA_REP
    o = acc * pltpu.roll(1.0 / acc, NSA_DH, 1)
    lo = lax.broadcasted_iota(jnp.int32, (nq, LANES), 1) < NSA_DH
    hs = [gates[r] * o[r * nq:(r + 1) * nq] for r in range(NSA_REP)]
    a = jnp.where(lo, hs[0], pltpu.roll(hs[1], NSA_DH, 1))
    b = jnp.where(lo, hs[2], pltpu.roll(hs[3], NSA_DH, 1))
    return jnp.concatenate([a, b], axis=1)


def _select_kernel(slopes_ref, q_ref, kc_ref, vc_ref, gates_ref, ov_ref, sel_ref, act_ref, yc_ref):
    qt = pl.program_id(1)
    nqt = pl.num_programs(1)
    nb_all, ncp_all = ov_ref.shape
    for k in range(SELECT_BUCKETS):
        @pl.when((qt * SELECT_BUCKETS) // nqt == k)
        def _(k=k):
            _select_prefix(slopes_ref, q_ref, kc_ref, vc_ref, gates_ref, ov_ref, sel_ref, act_ref, yc_ref,
                           nb=(k + 1) * nb_all // SELECT_BUCKETS, ncp=(k + 1) * ncp_all // SELECT_BUCKETS)


def _select_prefix(slopes_ref, q_ref, kc_ref, vc_ref, gates_ref, ov_ref, sel_ref, act_ref, yc_ref, *, nb, ncp):
    t0 = pl.program_id(1) * Q_TILE
    nq = Q_TILE
    nb_all = ov_ref.shape[0]
    c_end = lax.broadcasted_iota(jnp.int32, (nq, ncp), 1) * CMP_STRIDE + (CMP_LEN - 1)
    tq = t0 + lax.broadcasted_iota(jnp.int32, (nq, ncp), 0)
    vis = c_end <= tq
    any_vis = tq[:, 0:1] >= CMP_LEN - 1
    rel = (c_end[0:1] - t0).astype(F32)
    ov = ov_ref[:nb, :ncp]
    jb = lax.broadcasted_iota(jnp.int32, (nb, nq), 0)
    tql = t0 + lax.broadcasted_iota(jnp.int32, (nb, nq), 1)
    valid = jb * SEL_LEN <= tql
    forced = (jb == 0) | (jb == jnp.right_shift(tql, SEL_SHIFT))
    scores = []
    for g in range(NSA_GROUPS):
        qcols = slice(g * NSA_REP * NSA_DH, (g + 1) * NSA_REP * NSA_DH)
        qs = _stack_heads_dup(q_ref[:, qcols] * (NSA_DH ** -0.5)).astype(BF16)
        s = _mm_nt(qs, kc_ref[0, g, :ncp, :])
        psum = jnp.zeros((nq, ncp), F32)
        ps = []
        for r in range(NSA_REP):
            m_r = slopes_ref[g * NSA_REP + r]
            sr = jnp.where(vis, s[r * nq:(r + 1) * nq] + m_r * rel, NEG)
            mx = jnp.max(sr, axis=-1, keepdims=True)
            e = jnp.exp(sr - mx)
            den = jnp.sum(e, axis=-1, keepdims=True)
            p = e * jnp.where(any_vis, 1.0 / den, 0.0)
            psum = psum + p
            ps.append(p.astype(BF16))
        o_c = jnp.dot(jnp.concatenate(ps, axis=0), vc_ref[0, g, :ncp, :], preferred_element_type=F32)
        yc_ref[:, qcols] = _unstack_heads_dup(o_c, _gate_cols(gates_ref[:, g * LANES:(g + 1) * LANES], 0))
        p_hi = psum.astype(BF16)
        p_lo = (psum - p_hi.astype(F32)).astype(BF16)
        imp_t = _mm_nt(ov, p_hi) + _mm_nt(ov, p_lo)
        scores.append(jnp.where(valid, imp_t + FORCE_BONUS * forced.astype(F32), NEG))

    for _ in range(min(TOP_N, nb)):
        for g in range(NSA_GROUPS):
            mx = jnp.max(scores[g], axis=0, keepdims=True)
            idx = jnp.min(jnp.where(scores[g] == mx, jb, nb), axis=0, keepdims=True)
            scores[g] = jnp.where(jb == idx, -jnp.inf, scores[g])
    for g in range(NSA_GROUPS):
        sel_t = jnp.where((scores[g] == -jnp.inf) & valid, 1.0, 0.0)
        if nb < nb_all:
            sel_t = jnp.concatenate([sel_t, jnp.zeros((nb_all - nb, nq), F32)], axis=0)
        sel_q = sel_t.T
        sel_ref[0, g] = sel_q.astype(sel_ref.dtype)
        act_ref[0, g, 0] = jnp.max(sel_q, axis=0, keepdims=True)


def _select_call(slopes, main, kcmp, vcmp, ov, bsz, seq):
    nqt = seq // Q_TILE
    nb, ncp = ov.shape
    row = lambda b, q: b * nqt + q
    gw = NSA_GROUPS * LANES
    return pl.pallas_call(
        _select_kernel,
        grid=(bsz, nqt),
        in_specs=[pl.BlockSpec(memory_space=pltpu.SMEM),
                  pl.BlockSpec((Q_TILE, NSA_Q_W), lambda b, q: (row(b, q), COL_Q // NSA_Q_W)),
                  pl.BlockSpec((1, NSA_GROUPS, ncp, LANES), lambda b, q: (b, 0, 0, 0)),
                  pl.BlockSpec((1, NSA_GROUPS, ncp, LANES), lambda b, q: (b, 0, 0, 0)),
                  pl.BlockSpec((Q_TILE, gw), lambda b, q: (row(b, q), COL_GATE // gw)),
                  pl.BlockSpec((nb, ncp), lambda b, q: (0, 0))],
        out_specs=[pl.BlockSpec((1, NSA_GROUPS, Q_TILE, nb), lambda b, q: (b, 0, q, 0)),
                   pl.BlockSpec((1, NSA_GROUPS, 1, 1, nb), lambda b, q: (b, 0, q, 0, 0)),
                   pl.BlockSpec((Q_TILE, NSA_Q_W), lambda b, q: (row(b, q), 0))],
        out_shape=[jax.ShapeDtypeStruct((bsz, NSA_GROUPS, seq, nb), BF16),
                   jax.ShapeDtypeStruct((bsz, NSA_GROUPS, nqt, 1, nb), F32),
                   jax.ShapeDtypeStruct((bsz * seq, NSA_Q_W), F32)],
        compiler_params=_cparams(("parallel", "parallel")),
        name="select",
    )(slopes, main, kcmp, vcmp, main, ov)


def _attn_kernel(lst_ref, cnt_ref, slopes_ref, q_ref, sel_ref, ks_ref, vs_ref, kw_ref, vw_ref, gates_ref, yc_ref,
                 et_ref, o_ref, s_scr, p_scr, m_scr, a_scr, acc_scr, *, chunk):
    b = pl.program_id(0)
    g = pl.program_id(1)
    step = pl.program_id(2)
    tiles_per_step = ATTN_TILES * ATTN_GROUPS
    nqt = pl.num_programs(2) * tiles_per_step
    nq = ATTN_Q_TILE
    nw = WINDOW + nq
    nch_total = ks_ref.shape[0] // chunk
    slopes = [slopes_ref[g * NSA_REP + r] for r in range(NSA_REP)]

    def values(p, c):
        start = pl.multiple_of(c * chunk, chunk)
        return jnp.dot(p, vs_ref[pl.ds(start, chunk), :], preferred_element_type=F32)

    def row_max(s):
        return jnp.broadcast_to(jnp.max(s, axis=-1, keepdims=True), (s.shape[0], LANES))

    def lane_tile(m, width):
        return jnp.concatenate([m] * (width // LANES), axis=1)

    def masked(s, ok):
        return jnp.concatenate([jnp.where(ok, s[r * nq:(r + 1) * nq], NEG) for r in range(NSA_REP)], axis=0)

    class Group:
        pass

    def setup(k):
        gr = Group()
        gr.tiles = range(ATTN_TILES)
        gr.scr = [(k % 2) * ATTN_TILES + u for u in gr.tiles]
        local = [k * ATTN_TILES + u for u in gr.tiles]
        gr.rws = [slice(v * nq, (v + 1) * nq) for v in local]
        qts = [step * tiles_per_step + v for v in local]
        gr.t0s = [qt * nq for qt in qts]
        gr.qs = [_stack_heads_pos(q_ref[rw, :] * (NSA_DH ** -0.5), slopes).astype(BF16) for rw in gr.rws]
        pens = [((sel_ref[0, 0, rw, :].astype(F32) - 1.0) * (-NEG)).astype(BF16) for rw in gr.rws]
        gr.qx = [jnp.concatenate([gr.qs[u], jnp.concatenate([pens[u]] * NSA_REP, axis=0)], axis=1) for u in gr.tiles]
        gr.bases = [((b * NSA_GROUPS + g) * nqt + qt) * nch_total for qt in qts]
        gr.cnts = [cnt_ref[(b * NSA_GROUPS + g) * nqt + qt] for qt in qts]
        for u in gr.tiles:
            su = gr.scr[u]
            m_scr[su] = jnp.full(m_scr.shape[1:], NEG, F32)
            a_scr[su] = jnp.ones(a_scr.shape[1:], F32)
            acc_scr[su] = jnp.zeros(acc_scr.shape[1:], F32)
            p_scr[su, 1] = jnp.zeros(p_scr.shape[2:], BF16)
        for u in gr.tiles:
            s_scr[gr.scr[u], 0] = logits(gr, u, lst_ref[gr.bases[u]])
        return gr

    def logits(gr, u, c):
        start = pl.multiple_of(c * chunk, chunk)
        kx = jnp.concatenate([ks_ref[pl.ds(start, chunk), :], et_ref[pl.ds(start, chunk), :]], axis=1)
        return _mm_nt(gr.qx[u], kx)

    def flash_step(gr, us, i, slot):
        for u in us:
            su = gr.scr[u]
            acc_scr[su] = a_scr[su] * acc_scr[su] + values(p_scr[su, 1 - slot],
                                                           lst_ref[gr.bases[u] + jnp.maximum(i - 1, 0)])
        for u in us:
            su = gr.scr[u]
            s = s_scr[su, slot]
            m_prev = m_scr[su]
            m_new = jnp.maximum(m_prev, row_max(s))
            a_scr[su] = jnp.exp(m_prev - m_new)
            p_scr[su, slot] = jnp.exp(s - lane_tile(m_new, chunk)).astype(BF16)
            m_scr[su] = m_new
        for u in us:
            s_scr[gr.scr[u], 1 - slot] = logits(gr, u, lst_ref[gr.bases[u] + i + 1])

    def flash_loops(gr):
        def single_steps(us):
            def body(i, carry):
                flash_step(gr, us, i, i & 1)
                return carry
            return body

        def double_steps(us):
            def body(j, carry):
                flash_step(gr, us, 2 * j, 0)
                flash_step(gr, us, 2 * j + 1, 1)
                return carry
            return body

        joint = functools.reduce(jnp.minimum, gr.cnts)
        lax.fori_loop(0, joint // 2, double_steps(list(gr.tiles)), 0)
        lax.fori_loop(joint - (joint & 1), joint, single_steps(list(gr.tiles)), 0)
        for u in gr.tiles:
            lax.fori_loop(joint, gr.cnts[u], single_steps([u]), 0)

    def finish(gr):
        tiles = gr.tiles
        scr = gr.scr
        slots = [gr.cnts[u] & 1 for u in tiles]
        c_diag = [gr.t0s[u] // chunk for u in tiles]
        acc = [a_scr[scr[u]] * acc_scr[scr[u]]
               + values(p_scr[scr[u], 1 - slots[u]], lst_ref[gr.bases[u] + jnp.maximum(gr.cnts[u] - 1, 0)])
               for u in tiles]
        wbase = [pl.multiple_of(jnp.maximum(gr.t0s[u] - WINDOW, 0), nq) for u in tiles]
        s_win = [_mm_nt(gr.qs[u], kw_ref[pl.ds(wbase[u], nw), :]) for u in tiles]
        s_diag = []
        for u in tiles:
            pos = c_diag[u] * chunk + lax.broadcasted_iota(jnp.int32, (nq, chunk), 1)
            tq = gr.t0s[u] + lax.broadcasted_iota(jnp.int32, (nq, chunk), 0)
            s_diag.append(masked(s_scr[scr[u], slots[u]], pos <= tq))
        m_prev = [m_scr[scr[u]] for u in tiles]
        m_new = [jnp.maximum(m_prev[u], row_max(s_diag[u])) for u in tiles]
        acc = [jnp.exp(m_prev[u] - m_new[u]) * acc[u]
               + values(jnp.exp(s_diag[u] - lane_tile(m_new[u], chunk)).astype(BF16), c_diag[u]) for u in tiles]
        p_win = []
        for u in tiles:
            pos = wbase[u] + lax.broadcasted_iota(jnp.int32, (nq, nw), 1)
            tq = gr.t0s[u] + lax.broadcasted_iota(jnp.int32, (nq, nw), 0)
            s = masked(s_win[u], (pos <= tq) & (pos > tq - WINDOW))
            p_win.append(jnp.exp(s - jnp.max(s, axis=-1, keepdims=True)).astype(BF16))
        acc_w = [jnp.dot(p_win[u], vw_ref[pl.ds(wbase[u], nw), :], preferred_element_type=F32) for u in tiles]
        for u in tiles:
            gates = gates_ref[gr.rws[u], :]
            o_ref[gr.rws[u], :] = (yc_ref[gr.rws[u], :] + _finish_heads(acc[u], _gate_cols(gates, 1))
                                   + _finish_heads(acc_w[u], _gate_cols(gates, 2))).astype(o_ref.dtype)

    cur = setup(0)
    for k in range(ATTN_GROUPS):
        flash_loops(cur)
        nxt = setup(k + 1) if k + 1 < ATTN_GROUPS else None
        finish(cur)
        cur = nxt


def _chunk_lists(act, seq, chunk):
    bsz, ngrp = act.shape[:2]
    nqt = seq // ATTN_Q_TILE
    nch = seq // chunk
    active = act.reshape(bsz, ngrp, nqt, ATTN_Q_TILE // Q_TILE, nch, chunk // SEL_LEN).max(axis=(3, 5)) > 0.0
    c_diag = (jnp.arange(nqt) * ATTN_Q_TILE) // chunk
    active = active & (jnp.arange(nch)[None, :] < c_diag[:, None])
    cnt = active.sum(axis=-1).astype(jnp.int32)
    order = jnp.argsort(jnp.logical_not(active), axis=-1, stable=True).astype(jnp.int32)
    lst = jnp.where(jnp.arange(nch) < cnt[..., None], order, c_diag[:, None].astype(jnp.int32))
    return lst.reshape(-1), cnt.reshape(-1)


def _attn_call(lst, cnt, slopes, main, sel, kv, yc, et, bsz, seq, chunk):
    tq = ATTN_Q_TILE * ATTN_TILES * ATTN_GROUPS
    nset = 2 * ATTN_TILES
    nqt = seq // tq
    nb = sel.shape[-1]
    rows = NSA_REP * ATTN_Q_TILE
    row = lambda b, g, q: b * nqt + q
    kvspec = lambda blk: pl.BlockSpec((seq, LANES), lambda b, g, q, *_: (b, blk + g))
    grid_spec = pltpu.PrefetchScalarGridSpec(
        num_scalar_prefetch=2,
        grid=(bsz, NSA_GROUPS, nqt),
        in_specs=[pl.BlockSpec(memory_space=pltpu.SMEM),
                  pl.BlockSpec((tq, 256), lambda b, g, q, *_: (row(b, g, q), COL_Q // 256 + g)),
                  pl.BlockSpec((1, 1, tq, nb), lambda b, g, q, *_: (b, g, q, 0)),
                  kvspec(KVB_KS), kvspec(KVB_VS), kvspec(KVB_KW), kvspec(KVB_VW),
                  pl.BlockSpec((tq, LANES), lambda b, g, q, *_: (row(b, g, q), COL_GATE // LANES + g)),
                  pl.BlockSpec((tq, 256), lambda b, g, q, *_: (row(b, g, q), g)),
                  pl.BlockSpec(et.shape, lambda b, g, q, *_: (0, 0))],
        out_specs=pl.BlockSpec((tq, 256), lambda b, g, q, *_: (row(b, g, q), g)),
        scratch_shapes=[pltpu.VMEM((nset, 2, rows, chunk), F32),
                        pltpu.VMEM((nset, 2, rows, chunk), BF16),
                        pltpu.VMEM((nset, rows, LANES), F32),
                        pltpu.VMEM((nset, rows, LANES), F32),
                        pltpu.VMEM((nset, rows, LANES), F32)],
    )
    return pl.pallas_call(
        functools.partial(_attn_kernel, chunk=chunk),
        grid_spec=grid_spec,
        out_shape=jax.ShapeDtypeStruct((bsz * seq, NSA_Q_W), BF16),
        compiler_params=_cparams(("parallel", "parallel", "parallel")),
        name="attn",
    )(lst, cnt, slopes, main, sel, kv, kv, kv, kv, main, yc, et)


def _pool_kernel(x_ref, halo_ref, w_ref, sc_ref, o_ref):
    s = pl.program_id(1)
    ts = x_ref.shape[0]
    x = x_ref[...]
    halo = jnp.where(s > 0, halo_ref[...], 0.0)
    t = s * ts + lax.broadcasted_iota(jnp.int32, (ts, 1), 0)
    for gi, win in enumerate(POOL_WINDOWS):
        lanes = slice(gi * POOL_GROUP_DIM, (gi + 1) * POOL_GROUP_DIM)
        ext = jnp.concatenate([halo[:, lanes], x[:, lanes]], axis=0)
        acc = ext
        span = 1
        while span < win:
            acc = acc + pltpu.roll(acc, span, 0)
            span *= 2
        cnt = jnp.minimum(t + 1, win).astype(F32)
        y = acc[POOL_HALO:] / cnt - x[:, lanes]
        o_ref[:, lanes] = (_mm(y, w_ref[gi]) * sc_ref[:, lanes]).astype(o_ref.dtype)


def _pool_call(main, w, scale, bsz, seq):
    ts = min(POOL_TS, seq)
    per_b = seq // ts
    halo_per_tile = ts // POOL_HALO
    return pl.pallas_call(
        _pool_kernel,
        grid=(bsz, per_b),
        in_specs=[pl.BlockSpec((ts, POOL_W), lambda b, s: (b * per_b + s, COL_POOL // POOL_W)),
                  pl.BlockSpec((POOL_HALO, POOL_W),
                               lambda b, s: (jnp.maximum((b * per_b + s) * halo_per_tile - 1, 0), COL_POOL // POOL_W)),
                  pl.BlockSpec(w.shape, lambda b, s: (0, 0, 0)),
                  pl.BlockSpec((1, POOL_W), lambda b, s: (0, 0))],
        out_specs=pl.BlockSpec((ts, POOL_W), lambda b, s: (b * per_b + s, 0)),
        out_shape=jax.ShapeDtypeStruct((bsz * seq, POOL_W), BF16),
        compiler_params=_cparams(("parallel", "parallel")),
        name="pool",
    )(main, main, w, scale.reshape(1, POOL_W))


def _unit_lower_inverses(nms, ri, ci):
    c = nms[0].shape[0]
    eye = jnp.where(ri == ci, 1.0, 0.0)
    in8 = jnp.right_shift(ri, 3) == jnp.right_shift(ci, 3)
    n8 = [jnp.where(in8, nm, 0.0) for nm in nms]
    ps = [eye - a for a in n8]
    ms = [_mm(a, a) for a in n8]
    ps = [p + _mm(p, m) for p, m in zip(ps, ms)]
    ms = [_mm(m, m) for m in ms]
    ps = [p + _mm(p, m) for p, m in zip(ps, ms)]
    shift = 3
    while (1 << shift) < c:
        rb = jnp.right_shift(ri, shift)
        cb = jnp.right_shift(ci, shift)
        lower_left = ((rb & 1) == 1) & (cb == rb - 1)
        ts = [_mm(p, jnp.where(lower_left, nm, 0.0)) for p, nm in zip(ps, nms)]
        ps = [p - _mm(t, p) for p, t in zip(ps, ts)]
        shift += 1
    return ps


def _row_cumsum(x, ri):
    span = 1
    while span < x.shape[0]:
        x = x + jnp.where(ri >= span, pltpu.roll(x, span, 0), 0.0)
        span *= 2
    return x


def _dn_kernel(qkv_ref, halo_ref, z_ref, gba_ref, cw_ref, alog_ref, dtb_ref, ng_ref, o_ref, st_scr):
    s = pl.program_id(1)
    nbatch, c = qkv_ref.shape[0], qkv_ref.shape[1]

    @pl.when(s == 0)
    def _():
        st_scr[...] = jnp.zeros(st_scr.shape, F32)

    ri = lax.broadcasted_iota(jnp.int32, (c, c), 0)
    ci = lax.broadcasted_iota(jnp.int32, (c, c), 1)
    tril = ri >= ci
    strict = ri > ci
    cw = cw_ref[...]
    ng = ng_ref[...]
    chains = [(bi, h) for bi in range(nbatch) for h in range(DN_HEADS)]
    qn, kn, kb, vb, gcb, lmask = [], [], [], [], [], []
    for bi in range(nbatch):
        halo = jnp.where(s > 0, halo_ref[bi], 0.0)
        ext = jnp.concatenate([halo, qkv_ref[bi]], axis=0)
        conv = cw[DN_CONV - 1:DN_CONV] * ext
        for tap in range(1, DN_CONV):
            conv = conv + cw[DN_CONV - 1 - tap:DN_CONV - tap] * pltpu.roll(ext, tap, 0)
        act = _silu(conv[DN_HALO:])

        gba = gba_ref[bi]
        beta_all = jax.nn.sigmoid(gba)
        xs = gba + dtb_ref[...]
        softplus = jnp.maximum(xs, 0.0) + jnp.log(1.0 + jnp.exp(-jnp.abs(xs)))
        gc_all = _row_cumsum(-jnp.exp(alog_ref[...]) * softplus, ri)
        gct_all = gc_all.T
        for h in range(DN_HEADS):
            q = act[:, h * DN_DK:(h + 1) * DN_DK]
            k = act[:, DN_W + h * DN_DK:DN_W + (h + 1) * DN_DK]
            v = act[:, 2 * DN_W + h * DN_DK:2 * DN_W + (h + 1) * DN_DK]
            qn.append(q * lax.rsqrt(jnp.sum(q * q, axis=-1, keepdims=True) + EPS) * (DN_DK ** -0.5))
            kn.append(k * lax.rsqrt(jnp.sum(k * k, axis=-1, keepdims=True) + EPS))
            beta = beta_all[:, LANE_BETA + h:LANE_BETA + h + 1]
            gcb.append(jnp.broadcast_to(gc_all[:, LANE_DECAY + h:LANE_DECAY + h + 1], (c, DN_DK)))
            gct = jnp.broadcast_to(gct_all[LANE_DECAY + h:LANE_DECAY + h + 1, :], (c, c))
            lmask.append(jnp.where(tril, jnp.exp(jnp.where(tril, gcb[-1] - gct, 0.0)), 0.0))
            kb.append(kn[-1] * beta)
            vb.append(v * beta)

    n = range(len(chains))
    kk = [_mm_nt(kb[i], kn[i]) for i in n]
    tinv = _unit_lower_inverses([jnp.where(strict, kk[i] * lmask[i], 0.0) for i in n], ri, ci)
    egc = [jnp.exp(gcb[i]) for i in n]
    sol = [_mm(tinv[i], jnp.concatenate([vb[i], kb[i] * egc[i]], axis=1)) for i in n]
    aqk = [_mm_nt(qn[i], kn[i]) * lmask[i] for i in n]
    state = [st_scr[i] for i in n]
    glast = [gcb[i][c - 1:c, :] for i in n]
    v_new = [sol[i][:, :DN_DK] - _mm(sol[i][:, DN_DK:], state[i]) for i in n]
    o_state = [_mm(qn[i] * egc[i], state[i]) for i in n]
    o = [o_state[i] + _mm(aqk[i], v_new[i]) for i in n]
    upd = [_mm_tn(kn[i] * jnp.exp(glast[i] - gcb[i]), v_new[i]) for i in n]
    for i, (bi, h) in enumerate(chains):
        st_scr[i] = state[i] * jnp.exp(glast[i]) + upd[i]
        on = o[i] * lax.rsqrt(jnp.mean(o[i] * o[i], axis=-1, keepdims=True) + EPS) * ng
        o_ref[bi, :, h * DN_DK:(h + 1) * DN_DK] = (
            on * _silu(z_ref[bi, :, h * DN_DK:(h + 1) * DN_DK])).astype(o_ref.dtype)


def _dn_call(main3, conv_w, alog_row, dtb_row, norm_g):
    bsz, seq, _ = main3.shape
    c = DN_CHUNK
    nbt = DN_BATCH if bsz % DN_BATCH == 0 else 1
    halo_per_chunk = c // DN_HALO
    w3 = 3 * DN_W
    return pl.pallas_call(
        _dn_kernel,
        grid=(bsz // nbt, seq // c),
        in_specs=[pl.BlockSpec((nbt, c, w3), lambda b, s: (b, s, COL_QKV // w3)),
                  pl.BlockSpec((nbt, DN_HALO, w3),
                               lambda b, s: (b, jnp.maximum(s * halo_per_chunk - 1, 0), COL_QKV // w3)),
                  pl.BlockSpec((nbt, c, DN_W), lambda b, s: (b, s, COL_Z // DN_W)),
                  pl.BlockSpec((nbt, c, LANES), lambda b, s: (b, s, COL_GATE // LANES)),
                  pl.BlockSpec((SUBLANES, w3), lambda b, s: (0, 0)),
                  pl.BlockSpec((1, LANES), lambda b, s: (0, 0)),
                  pl.BlockSpec((1, LANES), lambda b, s: (0, 0)),
                  pl.BlockSpec((1, DN_DK), lambda b, s: (0, 0))],
        out_specs=pl.BlockSpec((nbt, c, DN_W), lambda b, s: (b, s, 0)),
        out_shape=jax.ShapeDtypeStruct((bsz, seq, DN_W), BF16),
        scratch_shapes=[pltpu.VMEM((nbt * DN_HEADS, DN_DK, DN_DK), F32)],
        compiler_params=_cparams(("parallel", "arbitrary")),
        name="dn",
    )(main3, main3, main3, main3, conv_w, alog_row, dtb_row, norm_g.reshape(1, DN_DK))


def _merge_kernel(ya_ref, yb_ref, yc_ref, mg_ref, x_ref, gt_ref, wa_ref, wb_ref, wc_ref, wo_ref, o_ref):
    d = x_ref.shape[1]
    gates = jax.nn.sigmoid(mg_ref[...].astype(F32))
    merged = (gates[:, :d] * jnp.dot(ya_ref[...], wa_ref[...], preferred_element_type=F32)
              + gates[:, d:2 * d] * jnp.dot(yb_ref[...], wb_ref[...], preferred_element_type=F32)
              + gates[:, 2 * d:] * jnp.dot(yc_ref[...], wc_ref[...], preferred_element_type=F32))
    y = jnp.dot(merged.astype(BF16), wo_ref[...], preferred_element_type=F32)
    o_ref[...] = x_ref[...] + gt_ref[0] * y


def _merge_call(ya, yb, yc, mg, x2, gt, wa, wb, wc, wo, seq):
    t, d = x2.shape
    tm = min(MERGE_TM, seq)
    per_b = seq // tm
    full = lambda a: pl.BlockSpec(a.shape, lambda i: (0, 0))
    return pl.pallas_call(
        _merge_kernel,
        grid=(t // tm,),
        in_specs=[pl.BlockSpec((tm, ya.shape[1]), lambda i: (i, 0)),
                  pl.BlockSpec((tm, yb.shape[1]), lambda i: (i, 0)),
                  pl.BlockSpec((tm, yc.shape[1]), lambda i: (i, 0)),
                  pl.BlockSpec((tm, mg.shape[1]), lambda i: (i, 0)),
                  pl.BlockSpec((tm, d), lambda i: (i, 0)),
                  pl.BlockSpec((1, 1, d), lambda i: (i // per_b, 0, 0)),
                  full(wa), full(wb), full(wc), full(wo)],
        out_specs=pl.BlockSpec((tm, d), lambda i: (i, 0)),
        out_shape=jax.ShapeDtypeStruct((t, d), F32),
        compiler_params=_cparams(("parallel",)),
        name="merge",
    )(ya, yb, yc, mg, x2, gt, wa, wb, wc, wo)


def _mlp_kernel(x_ref, g_ref, sc_ref, sh_ref, gt_ref, w1_ref, w2_ref, fg_ref, o_ref, acc_scr, *, final):
    j = pl.program_id(1)

    @pl.when((pl.program_id(0) == 0) & (j == 0))
    def _():
        acc_scr[...] = jnp.zeros(acc_scr.shape, F32)

    x = x_ref[...]
    h = _norm_mod(x, g_ref[...], sc_ref[0], sh_ref[0]).astype(BF16)
    a = jnp.maximum(jnp.dot(h, w1_ref[...], preferred_element_type=F32), 0.0)
    acc = acc_scr[...] + jnp.dot((a * a).astype(BF16), w2_ref[...], preferred_element_type=F32)
    acc_scr[...] = jnp.where(j == pl.num_programs(1) - 1, 0.0, acc)
    y = x + gt_ref[0] * acc
    if final:
        ms = jnp.mean(y * y, axis=-1, keepdims=True)
        y = y * lax.rsqrt(ms + EPS) * fg_ref[...]
    o_ref[...] = y


def _mlp_call(x2, g, sc, sh, gt, w1, w2, fg, seq, final):
    t, d = x2.shape
    f = w1.shape[1]
    tm = min(MLP_TM, seq)
    tf = MLP_TF
    per_b = seq // tm
    mod = lambda: pl.BlockSpec((1, 1, d), lambda i, j: (i // per_b, 0, 0))
    return pl.pallas_call(
        functools.partial(_mlp_kernel, final=final),
        grid=(t // tm, f // tf),
        in_specs=[pl.BlockSpec((tm, d), lambda i, j: (i, 0)),
                  pl.BlockSpec((1, d), lambda i, j: (0, 0)),
                  mod(), mod(), mod(),
                  pl.BlockSpec((d, tf), lambda i, j: (0, j)),
                  pl.BlockSpec((tf, d), lambda i, j: (j, 0)),
                  pl.BlockSpec((1, d), lambda i, j: (0, 0))],
        out_specs=pl.BlockSpec((tm, d), lambda i, j: (i, 0)),
        out_shape=jax.ShapeDtypeStruct((t, d), F32),
        scratch_shapes=[pltpu.VMEM((tm, d), F32)],
        compiler_params=_cparams(("arbitrary", "arbitrary")),
        name="mlp",
    )(x2, g.reshape(1, d), sc, sh, gt, w1, w2, fg.reshape(1, d))


def _split_w_in(w_in):
    offs = [0]
    for sz in IN_SIZES:
        offs.append(offs[-1] + sz)
    return [w_in[:, offs[i]:offs[i + 1]] for i in range(len(IN_SIZES))]


def _layout_w_in(w_in):
    nq, nkc, nvc, nks, nvs, nkw, nvw, ngate, pin, dqkv, dz, dbeta, da, mg = _split_w_in(w_in)
    d = w_in.shape[0]
    per_g = 3 * NSA_REP
    zeros = lambda n: jnp.zeros((d, n), w_in.dtype)
    gate0 = jnp.concatenate([ngate[:, :per_g], zeros(LANE_BETA - per_g), dbeta, da,
                             zeros(LANES - LANE_DECAY - DN_HEADS)], axis=1)
    gate1 = jnp.concatenate([ngate[:, per_g:], zeros(LANES - per_g)], axis=1)
    main = jnp.concatenate([nq, pin, dz, dqkv, gate0, gate1, nkc, nvc], axis=1)

    def per_group(w):
        pad = zeros(LANES - NSA_DH)
        return jnp.concatenate([w[:, :NSA_DH], pad, w[:, NSA_DH:], pad], axis=1)

    kv = jnp.concatenate([per_group(nks), per_group(nvs), per_group(nkw), per_group(nvw)], axis=1)
    return main.astype(BF16), mg.astype(BF16), kv.astype(BF16)


def _group_lane_w1(w1):
    two, _, hid = w1.shape
    halves = w1.reshape(two, 2, CMP_STRIDE, NSA_DH, hid)
    out = []
    for g in range(NSA_GROUPS):
        padded = jnp.zeros((two, 2, CMP_STRIDE, LANES, hid), w1.dtype).at[:, :, :, g * NSA_DH:(g + 1) * NSA_DH].set(halves)
        out.append(padded.reshape(two, 2, CMP_STRIDE * LANES, hid))
    return jnp.stack(out, axis=1)


def _nsa_constants(seq):
    nb = seq // SEL_LEN
    ncp = seq // CMP_STRIDE
    j = jnp.arange(nb)[:, None] * SEL_LEN
    i = jnp.arange(ncp)[None, :] * CMP_STRIDE
    overlap = jnp.clip(jnp.minimum(j + SEL_LEN, i + CMP_LEN) - jnp.maximum(j, i), 0) // CMP_STRIDE
    block_onehot = (jnp.arange(seq)[:, None] // SEL_LEN) == jnp.arange(nb)[None, :]
    return overlap.astype(BF16), block_onehot.astype(BF16)


def kernel(x, c, ada_w, ada_b, norm1_g, norm2_g, w_in, phi_k1, phi_k2, phi_v1, phi_v2, pos_k, pos_v, pool_w, pool_scale, dn_conv_w, dn_A_log, dn_dt_bias, dn_norm_g, w_branch_nsa, w_branch_pool, w_branch_dn, w_out, mlp_w1, mlp_w2, final_g):
    bsz, seq, d = x.shape
    assert d == D_MODEL and seq % max(PROJ_TM, MLP_TM, POOL_TS) == 0 and seq % KV_CHUNK == 0
    assert seq % (ATTN_Q_TILE * ATTN_TILES * ATTN_GROUPS) == 0 and (seq // Q_TILE) % SELECT_BUCKETS == 0
    assert DN_CHUNK == DN_DK == LANES
    t = bsz * seq
    depth = ada_w.shape[0]
    chunk = KV_CHUNK

    slopes = 2.0 ** (-(8.0 / NSA_HEADS) * (jnp.arange(NSA_HEADS, dtype=F32) + 1.0))
    overlap, block_onehot = _nsa_constants(seq)
    c_pad = jnp.zeros((SUBLANES, d), F32).at[:bsz].set(c)
    x2 = x.reshape(t, d)

    for l in range(depth):
        mod = _mod_call(c_pad, ada_w[l], ada_b[l])[:bsz]
        sh1, sc1, gt1, sh2, sc2, gt2 = [m.reshape(bsz, 1, d) for m in jnp.split(mod, 6, axis=-1)]

        w_main, w_mg, w_kv = _layout_w_in(w_in[l])
        main = _proj_call(x2, norm1_g[l], sc1, sh1, w_main, seq, PROJ_MAIN_TM, F32, "proj_main")
        mg = _proj_call(x2, norm1_g[l], sc1, sh1, w_mg, seq, PROJ_TM, BF16, "proj_mg")
        kv = _proj_kv_call(x2, norm1_g[l], sc1, sh1, w_kv, seq)

        main3 = main.reshape(bsz, seq, N_MAIN)
        pos = jnp.zeros((2, SUBLANES, CMP_LEN * NSA_DH), F32).at[:, 0].set(
            jnp.stack([pos_k[l].reshape(-1), pos_v[l].reshape(-1)])).astype(BF16)
        w1 = jnp.stack([phi_k1[l], phi_v1[l]]).astype(BF16)
        w2 = jnp.stack([phi_k2[l], phi_v2[l]])
        w2 = jnp.concatenate([w2, w2], axis=-1).astype(BF16)
        cmp_kv = _compress_call(main3, pos, w1, _group_lane_w1(w1), w2)

        sel, act, yc = _select_call(slopes, main, cmp_kv[0], cmp_kv[1], overlap, bsz, seq)
        lst, cnt = _chunk_lists(act, seq, chunk)
        y_a = _attn_call(lst, cnt, slopes, main, sel, kv, yc, block_onehot, bsz, seq, chunk)

        y_b = _pool_call(main, pool_w[l].astype(BF16), pool_scale[l], bsz, seq)

        conv_w = jnp.zeros((SUBLANES, 3 * DN_W), F32).at[:DN_CONV].set(dn_conv_w[l])
        lane_row = lambda v: jnp.zeros((1, LANES), F32).at[0, LANE_DECAY:LANE_DECAY + DN_HEADS].set(v)
        y_c = _dn_call(main3, conv_w, lane_row(dn_A_log[l]), lane_row(dn_dt_bias[l]),
                       dn_norm_g[l]).reshape(t, DN_W)

        x2 = _merge_call(y_a, y_b, y_c, mg, x2, gt1,
                         w_branch_nsa[l].astype(BF16), w_branch_pool[l].astype(BF16),
                         w_branch_dn[l].astype(BF16), w_out[l].astype(BF16), seq)
        x2 = _mlp_call(x2, norm2_g[l], sc2, sh2, gt2, mlp_w1[l].astype(BF16), mlp_w2[l].astype(BF16),
                       final_g, seq, final=(l == depth - 1))
    return x2.reshape(bsz, seq, d)
```

```python
import functools

import jax
import jax.numpy as jnp
from jax import lax
from jax.experimental import pallas as pl
from jax.experimental.pallas import tpu as pltpu

F32 = jnp.float32
BF16 = jnp.bfloat16
HIGHEST = lax.Precision.HIGHEST

D_MODEL = 1024
NSA_HEADS = 8
NSA_GROUPS = 2
NSA_REP = NSA_HEADS // NSA_GROUPS
NSA_DH = 64
CMP_LEN = 32
CMP_STRIDE = 16
SEL_LEN = 64
SEL_SHIFT = 6
TOP_N = 16
WINDOW = 512
FORCE_BONUS = 1.0e4
POOL_WINDOWS = (2, 4, 8, 16)
POOL_GROUP_DIM = 128
POOL_W = len(POOL_WINDOWS) * POOL_GROUP_DIM
DN_HEADS = 4
DN_DK = 128
DN_W = DN_HEADS * DN_DK
DN_CONV = 4
EPS = 1e-6
NEG = -1e30
NSA_Q_W = NSA_HEADS * NSA_DH
NSA_KV_W = NSA_GROUPS * NSA_DH
IN_SIZES = (NSA_Q_W, NSA_KV_W, NSA_KV_W, NSA_KV_W, NSA_KV_W, NSA_KV_W, NSA_KV_W,
            3 * NSA_HEADS, POOL_W, 3 * DN_W, DN_W, DN_HEADS, DN_HEADS, 3 * D_MODEL)

LANES = 128
LANE_SHIFT = 7
SUBLANES = 8
VMEM_LIMIT_BYTES = 48 * 1024 * 1024

Q_TILE = 128
SELECT_BUCKETS = 4
ATTN_Q_TILE = 128
ATTN_TILES = 2
ATTN_GROUPS = 8
KV_CHUNK = 256
DN_CHUNK = 128
DN_BATCH = 4
PROJ_TM = 1024
MLP_TM = 1024
MLP_TF = 2048
MERGE_TM = 512
POOL_TS = 1024
POOL_HALO = 16
DN_HALO = 8

COL_Q = 0
COL_POOL = 512
COL_Z = 1024
COL_QKV = 1536
COL_GATE = 3072
COL_KC = 3328
COL_VC = 3456
N_MAIN = 3584
PROJ_MAIN_TM = 512
N_MG = 3072
LANE_BETA = 16
LANE_DECAY = 20
KVB_KS = 0
KVB_VS = 2
KVB_KW = 4
KVB_VW = 6
KVB_END = 8
N_KV = KVB_END * LANES
LANE_POS_HI = NSA_DH
LANE_POS_LO = NSA_DH + 1


def _cparams(sem):
    return pltpu.CompilerParams(dimension_semantics=sem, vmem_limit_bytes=VMEM_LIMIT_BYTES)


def _silu(v):
    return v * jax.nn.sigmoid(v)


def _mm(a, b):
    return jnp.dot(a.astype(BF16), b.astype(BF16), preferred_element_type=F32)


def _mm_nt(a, b):
    return lax.dot_general(a.astype(BF16), b.astype(BF16), (((1,), (1,)), ((), ())), preferred_element_type=F32)


def _mm_tn(a, b):
    return lax.dot_general(a.astype(BF16), b.astype(BF16), (((0,), (0,)), ((), ())), preferred_element_type=F32)


def _mod_kernel(c_ref, w_ref, b_ref, o_ref):
    cond = _silu(c_ref[...])
    o_ref[...] = jnp.dot(cond, w_ref[...], preferred_element_type=F32, precision=HIGHEST) + b_ref[...]


def _mod_call(c_pad, w, b):
    d, n = w.shape
    tn = 1536
    return pl.pallas_call(
        _mod_kernel,
        grid=(n // tn,),
        in_specs=[pl.BlockSpec((SUBLANES, d), lambda j: (0, 0)),
                  pl.BlockSpec((d, tn), lambda j: (0, j)),
                  pl.BlockSpec((1, tn), lambda j: (0, j))],
        out_specs=pl.BlockSpec((SUBLANES, tn), lambda j: (0, j)),
        out_shape=jax.ShapeDtypeStruct((SUBLANES, n), F32),
        compiler_params=_cparams(("parallel",)),
        name="mod",
    )(c_pad, w, b.reshape(1, n))


def _norm_mod(x, g, sc, sh):
    ms = jnp.mean(x * x, axis=-1, keepdims=True)
    return (x * lax.rsqrt(ms + EPS) * g) * (1.0 + sc) + sh


def _proj_kernel(x_ref, g_ref, sc_ref, sh_ref, w_ref, o_ref):
    h = _norm_mod(x_ref[...], g_ref[...], sc_ref[0], sh_ref[0]).astype(BF16)
    o_ref[...] = jnp.dot(h, w_ref[...], preferred_element_type=F32).astype(o_ref.dtype)


def _proj_kv_kernel(x_ref, g_ref, sc_ref, sh_ref, w_ref, o_ref, *, tiles_per_seq):
    h = _norm_mod(x_ref[...], g_ref[...], sc_ref[0], sh_ref[0]).astype(BF16)
    acc = jnp.dot(h, w_ref[...], preferred_element_type=F32)
    tm, n = acc.shape
    col = lax.broadcasted_iota(jnp.int32, (1, n), 1)
    blk = jnp.right_shift(col, LANE_SHIFT)
    lane = col & (LANES - 1)
    is_key = ((blk >= KVB_KS) & (blk < KVB_VS)) | ((blk >= KVB_KW) & (blk < KVB_VW))
    is_val = ((blk >= KVB_VS) & (blk < KVB_KW)) | ((blk >= KVB_VW) & (blk < KVB_END))
    ones_row = jnp.where(is_val & (lane >= NSA_DH), 1.0, 0.0)
    hi_row = jnp.where(is_key & (lane == LANE_POS_HI), 1.0, 0.0)
    lo_row = jnp.where(is_key & (lane == LANE_POS_LO), 1.0, 0.0)
    pos = (pl.program_id(0) % tiles_per_seq) * tm + lax.broadcasted_iota(jnp.int32, (tm, 1), 0)
    pos_hi = jnp.right_shift(pos, LANE_SHIFT).astype(F32)
    pos_lo = (pos & (LANES - 1)).astype(F32)
    o_ref[...] = (acc + ones_row + pos_hi * hi_row + pos_lo * lo_row).astype(o_ref.dtype)


def _proj_specs(tm, d, tn, per_b):
    return [pl.BlockSpec((tm, d), lambda i, j: (i, 0)),
            pl.BlockSpec((1, d), lambda i, j: (0, 0)),
            pl.BlockSpec((1, 1, d), lambda i, j: (i // per_b, 0, 0)),
            pl.BlockSpec((1, 1, d), lambda i, j: (i // per_b, 0, 0)),
            pl.BlockSpec((d, tn), lambda i, j: (0, j))]


def _proj_call(x2, g, sc, sh, w, seq, tm, out_dtype, name):
    t, d = x2.shape
    n = w.shape[1]
    tm = min(tm, seq)
    return pl.pallas_call(
        _proj_kernel,
        grid=(t // tm, 1),
        in_specs=_proj_specs(tm, d, n, seq // tm),
        out_specs=pl.BlockSpec((tm, n), lambda i, j: (i, j)),
        out_shape=jax.ShapeDtypeStruct((t, n), out_dtype),
        compiler_params=_cparams(("parallel", "arbitrary")),
        name=name,
    )(x2, g.reshape(1, d), sc, sh, w)


def _proj_kv_call(x2, g, sc, sh, w, seq):
    t, d = x2.shape
    n = w.shape[1]
    tm = min(PROJ_TM, seq)
    return pl.pallas_call(
        functools.partial(_proj_kv_kernel, tiles_per_seq=seq // tm),
        grid=(t // tm, 1),
        in_specs=_proj_specs(tm, d, n, seq // tm),
        out_specs=pl.BlockSpec((tm, n), lambda i, j: (i, j)),
        out_shape=jax.ShapeDtypeStruct((t, n), BF16),
        compiler_params=_cparams(("parallel", "arbitrary")),
        name="proj_kv",
    )(x2, g.reshape(1, d), sc, sh, w)


def _compress_kernel(x_ref, pos_ref, w1_ref, w1g_ref, w2_ref, o_ref):
    nr = x_ref.shape[1] // CMP_STRIDE
    xs = jnp.concatenate([x_ref[0, pl.ds(rho, nr, stride=CMP_STRIDE), :].astype(BF16) for rho in range(CMP_STRIDE)],
                         axis=1)
    posb = jnp.dot(pos_ref[0], w1_ref[0], preferred_element_type=F32)[0:1]
    for g in range(NSA_GROUPS):
        top = jnp.dot(xs, w1g_ref[0, g, 0], preferred_element_type=F32)
        bot = jnp.dot(xs, w1g_ref[0, g, 1], preferred_element_type=F32)
        hid = top + pltpu.roll(bot, nr - 1, 0) + posb
        o_ref[0, 0, g] = _mm(_silu(hid), w2_ref[0]).astype(o_ref.dtype)


def _compress_call(main3, pos, w1, w1g, w2):
    b, seq, _ = main3.shape
    nr = seq // CMP_STRIDE
    two, l64, hid = w1.shape
    return pl.pallas_call(
        _compress_kernel,
        grid=(two, b),
        in_specs=[pl.BlockSpec((1, seq, LANES), lambda a, i: (i, 0, COL_KC // LANES + a)),
                  pl.BlockSpec((1, SUBLANES, l64), lambda a, i: (a, 0, 0)),
                  pl.BlockSpec((1, l64, hid), lambda a, i: (a, 0, 0)),
                  pl.BlockSpec((1,) + w1g.shape[1:], lambda a, i: (a, 0, 0, 0, 0)),
                  pl.BlockSpec((1, hid, LANES), lambda a, i: (a, 0, 0))],
        out_specs=pl.BlockSpec((1, 1, NSA_GROUPS, nr, LANES), lambda a, i: (a, i, 0, 0, 0)),
        out_shape=jax.ShapeDtypeStruct((two, b, NSA_GROUPS, nr, LANES), BF16),
        compiler_params=_cparams(("parallel", "parallel")),
        name="compress",
    )(main3, pos, w1, w1g, w2)


def _gate_cols(gate_logits, branch):
    sg = jax.nn.sigmoid(gate_logits)
    return [sg[:, 3 * r + branch:3 * r + branch + 1] for r in range(NSA_REP)]


def _stack_heads_dup(q):
    qa = q[:, :LANES]
    qb = q[:, LANES:]
    lo = lax.broadcasted_iota(jnp.int32, qa.shape, 1) < NSA_DH
    z = jnp.zeros_like(qa)
    return jnp.concatenate([jnp.where(lo, qa, z), jnp.where(lo, z, qa),
                            jnp.where(lo, qb, z), jnp.where(lo, z, qb)], axis=0)


def _unstack_heads_dup(o, gates):
    nq = o.shape[0] // NSA_REP
    lo = lax.broadcasted_iota(jnp.int32, (nq, LANES), 1) < NSA_DH
    a = jnp.where(lo, gates[0] * o[0:nq], gates[1] * o[nq:2 * nq])
    b = jnp.where(lo, gates[2] * o[2 * nq:3 * nq], gates[3] * o[3 * nq:])
    return jnp.concatenate([a, b], axis=1)


def _stack_heads_pos(q, slopes):
    qa = q[:, :LANES]
    qb = q[:, LANES:]
    lane = lax.broadcasted_iota(jnp.int32, qa.shape, 1)
    lo = lane < NSA_DH
    heads = [qa, pltpu.roll(qa, NSA_DH, 1), qb, pltpu.roll(qb, NSA_DH, 1)]
    out = []
    for r in range(NSA_REP):
        extra = jnp.where(lane == LANE_POS_HI, slopes[r] * LANES, jnp.where(lane == LANE_POS_LO, slopes[r], 0.0))
        out.append(jnp.where(lo, heads[r], extra))
    return jnp.concatenate(out, axis=0)


def _finish_heads(acc, gates):
    nq = acc.shape[0] // NSA_REP
    o = acc * pltpu.roll(1.0 / acc, NSA_DH, 1)
    lo = lax.broadcasted_iota(jnp.int32, (nq, LANES), 1) < NSA_DH
    hs = [gates[r] * o[r * nq:(r + 1) * nq] for r in range(NSA_REP)]
    a = jnp.where(lo, hs[0], pltpu.roll(hs[1], NSA_DH, 1))
    b = jnp.where(lo, hs[2], pltpu.roll(hs[3], NSA_DH, 1))
    return jnp.concatenate([a, b], axis=1)


def _select_kernel(slopes_ref, q_ref, kc_ref, vc_ref, gates_ref, ov_ref, sel_ref, act_ref, yc_ref):
    qt = pl.program_id(1)
    nqt = pl.num_programs(1)
    nb_all, ncp_all = ov_ref.shape
    for k in range(SELECT_BUCKETS):
        @pl.when((qt * SELECT_BUCKETS) // nqt == k)
        def _(k=k):
            _select_prefix(slopes_ref, q_ref, kc_ref, vc_ref, gates_ref, ov_ref, sel_ref, act_ref, yc_ref,
                           nb=(k + 1) * nb_all // SELECT_BUCKETS, ncp=(k + 1) * ncp_all // SELECT_BUCKETS)


def _select_prefix(slopes_ref, q_ref, kc_ref, vc_ref, gates_ref, ov_ref, sel_ref, act_ref, yc_ref, *, nb, ncp):
    t0 = pl.program_id(1) * Q_TILE
    nq = Q_TILE
    nb_all = ov_ref.shape[0]
    c_end = lax.broadcasted_iota(jnp.int32, (nq, ncp), 1) * CMP_STRIDE + (CMP_LEN - 1)
    tq = t0 + lax.broadcasted_iota(jnp.int32, (nq, ncp), 0)
    vis = c_end <= tq
    any_vis = tq[:, 0:1] >= CMP_LEN - 1
    rel = (c_end[0:1] - t0).astype(F32)
    ov = ov_ref[:nb, :ncp]
    jb = lax.broadcasted_iota(jnp.int32, (nb, nq), 0)
    tql = t0 + lax.broadcasted_iota(jnp.int32, (nb, nq), 1)
    valid = jb * SEL_LEN <= tql
    forced = (jb == 0) | (jb == jnp.right_shift(tql, SEL_SHIFT))
    scores = []
    for g in range(NSA_GROUPS):
        qcols = slice(g * NSA_REP * NSA_DH, (g + 1) * NSA_REP * NSA_DH)
        qs = _stack_heads_dup(q_ref[:, qcols] * (NSA_DH ** -0.5)).astype(BF16)
        s = _mm_nt(qs, kc_ref[0, g, :ncp, :])
        psum = jnp.zeros((nq, ncp), F32)
        ps = []
        for r in range(NSA_REP):
            m_r = slopes_ref[g * NSA_REP + r]
            sr = jnp.where(vis, s[r * nq:(r + 1) * nq] + m_r * rel, NEG)
            mx = jnp.max(sr, axis=-1, keepdims=True)
            e = jnp.exp(sr - mx)
            den = jnp.sum(e, axis=-1, keepdims=True)
            p = e * jnp.where(any_vis, 1.0 / den, 0.0)
            psum = psum + p
            ps.append(p.astype(BF16))
        o_c = jnp.dot(jnp.concatenate(ps, axis=0), vc_ref[0, g, :ncp, :], preferred_element_type=F32)
        yc_ref[:, qcols] = _unstack_heads_dup(o_c, _gate_cols(gates_ref[:, g * LANES:(g + 1) * LANES], 0))
        p_hi = psum.astype(BF16)
        p_lo = (psum - p_hi.astype(F32)).astype(BF16)
        imp_t = _mm_nt(ov, p_hi) + _mm_nt(ov, p_lo)
        scores.append(jnp.where(valid, imp_t + FORCE_BONUS * forced.astype(F32), NEG))

    for _ in range(min(TOP_N, nb)):
        for g in range(NSA_GROUPS):
            mx = jnp.max(scores[g], axis=0, keepdims=True)
            idx = jnp.min(jnp.where(scores[g] == mx, jb, nb), axis=0, keepdims=True)
            scores[g] = jnp.where(jb == idx, -jnp.inf, scores[g])
    for g in range(NSA_GROUPS):
        sel_t = jnp.where((scores[g] == -jnp.inf) & valid, 1.0, 0.0)
        if nb < nb_all:
            sel_t = jnp.concatenate([sel_t, jnp.zeros((nb_all - nb, nq), F32)], axis=0)
        sel_q = sel_t.T
        sel_ref[0, g] = sel_q.astype(sel_ref.dtype)
        act_ref[0, g, 0] = jnp.max(sel_q, axis=0, keepdims=True)


def _select_call(slopes, main, kcmp, vcmp, ov, bsz, seq):
    nqt = seq // Q_TILE
    nb, ncp = ov.shape
    row = lambda b, q: b * nqt + q
    gw = NSA_GROUPS * LANES
    return pl.pallas_call(
        _select_kernel,
        grid=(bsz, nqt),
        in_specs=[pl.BlockSpec(memory_space=pltpu.SMEM),
                  pl.BlockSpec((Q_TILE, NSA_Q_W), lambda b, q: (row(b, q), COL_Q // NSA_Q_W)),
                  pl.BlockSpec((1, NSA_GROUPS, ncp, LANES), lambda b, q: (b, 0, 0, 0)),
                  pl.BlockSpec((1, NSA_GROUPS, ncp, LANES), lambda b, q: (b, 0, 0, 0)),
                  pl.BlockSpec((Q_TILE, gw), lambda b, q: (row(b, q), COL_GATE // gw)),
                  pl.BlockSpec((nb, ncp), lambda b, q: (0, 0))],
        out_specs=[pl.BlockSpec((1, NSA_GROUPS, Q_TILE, nb), lambda b, q: (b, 0, q, 0)),
                   pl.BlockSpec((1, NSA_GROUPS, 1, 1, nb), lambda b, q: (b, 0, q, 0, 0)),
                   pl.BlockSpec((Q_TILE, NSA_Q_W), lambda b, q: (row(b, q), 0))],
        out_shape=[jax.ShapeDtypeStruct((bsz, NSA_GROUPS, seq, nb), BF16),
                   jax.ShapeDtypeStruct((bsz, NSA_GROUPS, nqt, 1, nb), F32),
                   jax.ShapeDtypeStruct((bsz * seq, NSA_Q_W), F32)],
        compiler_params=_cparams(("parallel", "parallel")),
        name="select",
    )(slopes, main, kcmp, vcmp, main, ov)


def _attn_kernel(lst_ref, cnt_ref, slopes_ref, q_ref, sel_ref, ks_ref, vs_ref, kw_ref, vw_ref, gates_ref, yc_ref,
                 et_ref, o_ref, s_scr, p_scr, m_scr, a_scr, acc_scr, *, chunk):
    b = pl.program_id(0)
    g = pl.program_id(1)
    step = pl.program_id(2)
    tiles_per_step = ATTN_TILES * ATTN_GROUPS
    nqt = pl.num_programs(2) * tiles_per_step
    nq = ATTN_Q_TILE
    nw = WINDOW + nq
    nch_total = ks_ref.shape[0] // chunk
    slopes = [slopes_ref[g * NSA_REP + r] for r in range(NSA_REP)]

    def values(p, c):
        start = pl.multiple_of(c * chunk, chunk)
        return jnp.dot(p, vs_ref[pl.ds(start, chunk), :], preferred_element_type=F32)

    def row_max(s):
        return jnp.broadcast_to(jnp.max(s, axis=-1, keepdims=True), (s.shape[0], LANES))

    def lane_tile(m, width):
        return jnp.concatenate([m] * (width // LANES), axis=1)

    def masked(s, ok):
        return jnp.concatenate([jnp.where(ok, s[r * nq:(r + 1) * nq], NEG) for r in range(NSA_REP)], axis=0)

    class Group:
        pass

    def setup(k):
        gr = Group()
        gr.tiles = range(ATTN_TILES)
        gr.scr = [(k % 2) * ATTN_TILES + u for u in gr.tiles]
        local = [k * ATTN_TILES + u for u in gr.tiles]
        gr.rws = [slice(v * nq, (v + 1) * nq) for v in local]
        qts = [step * tiles_per_step + v for v in local]
        gr.t0s = [qt * nq for qt in qts]
        gr.qs = [_stack_heads_pos(q_ref[rw, :] * (NSA_DH ** -0.5), slopes).astype(BF16) for rw in gr.rws]
        pens = [((sel_ref[0, 0, rw, :].astype(F32) - 1.0) * (-NEG)).astype(BF16) for rw in gr.rws]
        gr.qx = [jnp.concatenate([gr.qs[u], jnp.concatenate([pens[u]] * NSA_REP, axis=0)], axis=1) for u in gr.tiles]
        gr.bases = [((b * NSA_GROUPS + g) * nqt + qt) * nch_total for qt in qts]
        gr.cnts = [cnt_ref[(b * NSA_GROUPS + g) * nqt + qt] for qt in qts]
        for u in gr.tiles:
            su = gr.scr[u]
            m_scr[su] = jnp.full(m_scr.shape[1:], NEG, F32)
            a_scr[su] = jnp.ones(a_scr.shape[1:], F32)
            acc_scr[su] = jnp.zeros(acc_scr.shape[1:], F32)
            p_scr[su, 1] = jnp.zeros(p_scr.shape[2:], BF16)
        for u in gr.tiles:
            s_scr[gr.scr[u], 0] = logits(gr, u, lst_ref[gr.bases[u]])
        return gr

    def logits(gr, u, c):
        start = pl.multiple_of(c * chunk, chunk)
        kx = jnp.concatenate([ks_ref[pl.ds(start, chunk), :], et_ref[pl.ds(start, chunk), :]], axis=1)
        return _mm_nt(gr.qx[u], kx)

    def flash_step(gr, us, i, slot):
        for u in us:
            su = gr.scr[u]
            acc_scr[su] = a_scr[su] * acc_scr[su] + values(p_scr[su, 1 - slot],
                                                           lst_ref[gr.bases[u] + jnp.maximum(i - 1, 0)])
        for u in us:
            su = gr.scr[u]
            s = s_scr[su, slot]
            m_prev = m_scr[su]
            m_new = jnp.maximum(m_prev, row_max(s))
            a_scr[su] = jnp.exp(m_prev - m_new)
            p_scr[su, slot] = jnp.exp(s - lane_tile(m_new, chunk)).astype(BF16)
            m_scr[su] = m_new
        for u in us:
            s_scr[gr.scr[u], 1 - slot] = logits(gr, u, lst_ref[gr.bases[u] + i + 1])

    def flash_loops(gr):
        def single_steps(us):
            def body(i, carry):
                flash_step(gr, us, i, i & 1)
                return carry
            return body

        def double_steps(us):
            def body(j, carry):
                flash_step(gr, us, 2 * j, 0)
                flash_step(gr, us, 2 * j + 1, 1)
                return carry
            return body

        joint = functools.reduce(jnp.minimum, gr.cnts)
        lax.fori_loop(0, joint // 2, double_steps(list(gr.tiles)), 0)
        lax.fori_loop(joint - (joint & 1), joint, single_steps(list(gr.tiles)), 0)
        for u in gr.tiles:
            lax.fori_loop(joint, gr.cnts[u], single_steps([u]), 0)

    def finish(gr):
        tiles = gr.tiles
        scr = gr.scr
        slots = [gr.cnts[u] & 1 for u in tiles]
        c_diag = [gr.t0s[u] // chunk for u in tiles]
        acc = [a_scr[scr[u]] * acc_scr[scr[u]]
               + values(p_scr[scr[u], 1 - slots[u]], lst_ref[gr.bases[u] + jnp.maximum(gr.cnts[u] - 1, 0)])
               for u in tiles]
        wbase = [pl.multiple_of(jnp.maximum(gr.t0s[u] - WINDOW, 0), nq) for u in tiles]
        s_win = [_mm_nt(gr.qs[u], kw_ref[pl.ds(wbase[u], nw), :]) for u in tiles]
        s_diag = []
        for u in tiles:
            pos = c_diag[u] * chunk + lax.broadcasted_iota(jnp.int32, (nq, chunk), 1)
            tq = gr.t0s[u] + lax.broadcasted_iota(jnp.int32, (nq, chunk), 0)
            s_diag.append(masked(s_scr[scr[u], slots[u]], pos <= tq))
        m_prev = [m_scr[scr[u]] for u in tiles]
        m_new = [jnp.maximum(m_prev[u], row_max(s_diag[u])) for u in tiles]
        acc = [jnp.exp(m_prev[u] - m_new[u]) * acc[u]
               + values(jnp.exp(s_diag[u] - lane_tile(m_new[u], chunk)).astype(BF16), c_diag[u]) for u in tiles]
        p_win = []
        for u in tiles:
            pos = wbase[u] + lax.broadcasted_iota(jnp.int32, (nq, nw), 1)
            tq = gr.t0s[u] + lax.broadcasted_iota(jnp.int32, (nq, nw), 0)
            s = masked(s_win[u], (pos <= tq) & (pos > tq - WINDOW))
            p_win.append(jnp.exp(s - jnp.max(s, axis=-1, keepdims=True)).astype(BF16))
        acc_w = [jnp.dot(p_win[u], vw_ref[pl.ds(wbase[u], nw), :], preferred_element_type=F32) for u in tiles]
        for u in tiles:
            gates = gates_ref[gr.rws[u], :]
            o_ref[gr.rws[u], :] = (yc_ref[gr.rws[u], :] + _finish_heads(acc[u], _gate_cols(gates, 1))
                                   + _finish_heads(acc_w[u], _gate_cols(gates, 2))).astype(o_ref.dtype)

    cur = setup(0)
    for k in range(ATTN_GROUPS):
        flash_loops(cur)
        nxt = setup(k + 1) if k + 1 < ATTN_GROUPS else None
        finish(cur)
        cur = nxt


def _chunk_lists(act, seq, chunk):
    bsz, ngrp = act.shape[:2]
    nqt = seq // ATTN_Q_TILE
    nch = seq // chunk
    active = act.reshape(bsz, ngrp, nqt, ATTN_Q_TILE // Q_TILE, nch, chunk // SEL_LEN).max(axis=(3, 5)) > 0.0
    c_diag = (jnp.arange(nqt) * ATTN_Q_TILE) // chunk
    active = active & (jnp.arange(nch)[None, :] < c_diag[:, None])
    cnt = active.sum(axis=-1).astype(jnp.int32)
    order = jnp.argsort(jnp.logical_not(active), axis=-1, stable=True).astype(jnp.int32)
    lst = jnp.where(jnp.arange(nch) < cnt[..., None], order, c_diag[:, None].astype(jnp.int32))
    return lst.reshape(-1), cnt.reshape(-1)


def _attn_call(lst, cnt, slopes, main, sel, kv, yc, et, bsz, seq, chunk):
    tq = ATTN_Q_TILE * ATTN_TILES * ATTN_GROUPS
    nset = 2 * ATTN_TILES
    nqt = seq // tq
    nb = sel.shape[-1]
    rows = NSA_REP * ATTN_Q_TILE
    row = lambda b, g, q: b * nqt + q
    kvspec = lambda blk: pl.BlockSpec((seq, LANES), lambda b, g, q, *_: (b, blk + g))
    grid_spec = pltpu.PrefetchScalarGridSpec(
        num_scalar_prefetch=2,
        grid=(bsz, NSA_GROUPS, nqt),
        in_specs=[pl.BlockSpec(memory_space=pltpu.SMEM),
                  pl.BlockSpec((tq, 256), lambda b, g, q, *_: (row(b, g, q), COL_Q // 256 + g)),
                  pl.BlockSpec((1, 1, tq, nb), lambda b, g, q, *_: (b, g, q, 0)),
                  kvspec(KVB_KS), kvspec(KVB_VS), kvspec(KVB_KW), kvspec(KVB_VW),
                  pl.BlockSpec((tq, LANES), lambda b, g, q, *_: (row(b, g, q), COL_GATE // LANES + g)),
                  pl.BlockSpec((tq, 256), lambda b, g, q, *_: (row(b, g, q), g)),
                  pl.BlockSpec(et.shape, lambda b, g, q, *_: (0, 0))],
        out_specs=pl.BlockSpec((tq, 256), lambda b, g, q, *_: (row(b, g, q), g)),
        scratch_shapes=[pltpu.VMEM((nset, 2, rows, chunk), F32),
                        pltpu.VMEM((nset, 2, rows, chunk), BF16),
                        pltpu.VMEM((nset, rows, LANES), F32),
                        pltpu.VMEM((nset, rows, LANES), F32),
                        pltpu.VMEM((nset, rows, LANES), F32)],
    )
    return pl.pallas_call(
        functools.partial(_attn_kernel, chunk=chunk),
        grid_spec=grid_spec,
        out_shape=jax.ShapeDtypeStruct((bsz * seq, NSA_Q_W), BF16),
        compiler_params=_cparams(("parallel", "parallel", "parallel")),
        name="attn",
    )(lst, cnt, slopes, main, sel, kv, kv, kv, kv, main, yc, et)


def _pool_kernel(x_ref, halo_ref, w_ref, sc_ref, o_ref):
    s = pl.program_id(1)
    ts = x_ref.shape[0]
    x = x_ref[...]
    halo = jnp.where(s > 0, halo_ref[...], 0.0)
    t = s * ts + lax.broadcasted_iota(jnp.int32, (ts, 1), 0)
    for gi, win in enumerate(POOL_WINDOWS):
        lanes = slice(gi * POOL_GROUP_DIM, (gi + 1) * POOL_GROUP_DIM)
        ext = jnp.concatenate([halo[:, lanes], x[:, lanes]], axis=0)
        acc = ext
        span = 1
        while span < win:
            acc = acc + pltpu.roll(acc, span, 0)
            span *= 2
        cnt = jnp.minimum(t + 1, win).astype(F32)
        y = acc[POOL_HALO:] / cnt - x[:, lanes]
        o_ref[:, lanes] = (_mm(y, w_ref[gi]) * sc_ref[:, lanes]).astype(o_ref.dtype)


def _pool_call(main, w, scale, bsz, seq):
    ts = min(POOL_TS, seq)
    per_b = seq // ts
    halo_per_tile = ts // POOL_HALO
    return pl.pallas_call(
        _pool_kernel,
        grid=(bsz, per_b),
        in_specs=[pl.BlockSpec((ts, POOL_W), lambda b, s: (b * per_b + s, COL_POOL // POOL_W)),
                  pl.BlockSpec((POOL_HALO, POOL_W),
                               lambda b, s: (jnp.maximum((b * per_b + s) * halo_per_tile - 1, 0), COL_POOL // POOL_W)),
                  pl.BlockSpec(w.shape, lambda b, s: (0, 0, 0)),
                  pl.BlockSpec((1, POOL_W), lambda b, s: (0, 0))],
        out_specs=pl.BlockSpec((ts, POOL_W), lambda b, s: (b * per_b + s, 0)),
        out_shape=jax.ShapeDtypeStruct((bsz * seq, POOL_W), BF16),
        compiler_params=_cparams(("parallel", "parallel")),
        name="pool",
    )(main, main, w, scale.reshape(1, POOL_W))


def _unit_lower_inverses(nms, ri, ci):
    c = nms[0].shape[0]
    eye = jnp.where(ri == ci, 1.0, 0.0)
    in8 = jnp.right_shift(ri, 3) == jnp.right_shift(ci, 3)
    n8 = [jnp.where(in8, nm, 0.0) for nm in nms]
    ps = [eye - a for a in n8]
    ms = [_mm(a, a) for a in n8]
    ps = [p + _mm(p, m) for p, m in zip(ps, ms)]
    ms = [_mm(m, m) for m in ms]
    ps = [p + _mm(p, m) for p, m in zip(ps, ms)]
    shift = 3
    while (1 << shift) < c:
        rb = jnp.right_shift(ri, shift)
        cb = jnp.right_shift(ci, shift)
        lower_left = ((rb & 1) == 1) & (cb == rb - 1)
        ts = [_mm(p, jnp.where(lower_left, nm, 0.0)) for p, nm in zip(ps, nms)]
        ps = [p - _mm(t, p) for p, t in zip(ps, ts)]
        shift += 1
    return ps


def _row_cumsum(x, ri):
    span = 1
    while span < x.shape[0]:
        x = x + jnp.where(ri >= span, pltpu.roll(x, span, 0), 0.0)
        span *= 2
    return x


def _dn_kernel(qkv_ref, halo_ref, z_ref, gba_ref, cw_ref, alog_ref, dtb_ref, ng_ref, o_ref, st_scr):
    s = pl.program_id(1)
    nbatch, c = qkv_ref.shape[0], qkv_ref.shape[1]

    @pl.when(s == 0)
    def _():
        st_scr[...] = jnp.zeros(st_scr.shape, F32)

    ri = lax.broadcasted_iota(jnp.int32, (c, c), 0)
    ci = lax.broadcasted_iota(jnp.int32, (c, c), 1)
    tril = ri >= ci
    strict = ri > ci
    cw = cw_ref[...]
    ng = ng_ref[...]
    chains = [(bi, h) for bi in range(nbatch) for h in range(DN_HEADS)]
    qn, kn, kb, vb, gcb, lmask = [], [], [], [], [], []
    for bi in range(nbatch):
        halo = jnp.where(s > 0, halo_ref[bi], 0.0)
        ext = jnp.concatenate([halo, qkv_ref[bi]], axis=0)
        conv = cw[DN_CONV - 1:DN_CONV] * ext
        for tap in range(1, DN_CONV):
            conv = conv + cw[DN_CONV - 1 - tap:DN_CONV - tap] * pltpu.roll(ext, tap, 0)
        act = _silu(conv[DN_HALO:])

        gba = gba_ref[bi]
        beta_all = jax.nn.sigmoid(gba)
        xs = gba + dtb_ref[...]
        softplus = jnp.maximum(xs, 0.0) + jnp.log(1.0 + jnp.exp(-jnp.abs(xs)))
        gc_all = _row_cumsum(-jnp.exp(alog_ref[...]) * softplus, ri)
        gct_all = gc_all.T
        for h in range(DN_HEADS):
            q = act[:, h * DN_DK:(h + 1) * DN_DK]
            k = act[:, DN_W + h * DN_DK:DN_W + (h + 1) * DN_DK]
            v = act[:, 2 * DN_W + h * DN_DK:2 * DN_W + (h + 1) * DN_DK]
            qn.append(q * (lax.rsqrt(jnp.sum(q * q, axis=-1, keepdims=True) + EPS) * (DN_DK ** -0.5)))
            kn.append(k * lax.rsqrt(jnp.sum(k * k, axis=-1, keepdims=True) + EPS))
            beta = beta_all[:, LANE_BETA + h:LANE_BETA + h + 1]
            gcb.append(jnp.broadcast_to(gc_all[:, LANE_DECAY + h:LANE_DECAY + h + 1], (c, DN_DK)))
            gct = jnp.broadcast_to(gct_all[LANE_DECAY + h:LANE_DECAY + h + 1, :], (c, c))
            lmask.append(jnp.where(tril, jnp.exp(gcb[-1] - gct), 0.0))
            kb.append(kn[-1] * beta)
            vb.append(v * beta)

    n = range(len(chains))
    kk = [_mm_nt(kb[i], kn[i]) for i in n]
    tinv = _unit_lower_inverses([jnp.where(strict, kk[i] * lmask[i], 0.0) for i in n], ri, ci)
    egc = [jnp.exp(gcb[i]) for i in n]
    sol = [_mm(tinv[i], jnp.concatenate([vb[i], kb[i] * egc[i]], axis=1)) for i in n]
    aqk = [_mm_nt(qn[i], kn[i]) * lmask[i] for i in n]
    state = [st_scr[i] for i in n]
    glast = [gcb[i][c - 1:c, :] for i in n]
    v_new = [sol[i][:, :DN_DK] - _mm(sol[i][:, DN_DK:], state[i]) for i in n]
    o_state = [_mm(qn[i] * egc[i], state[i]) for i in n]
    o = [o_state[i] + _mm(aqk[i], v_new[i]) for i in n]
    upd = [_mm_tn(kn[i] * jnp.exp(glast[i] - gcb[i]), v_new[i]) for i in n]
    for i, (bi, h) in enumerate(chains):
        st_scr[i] = state[i] * jnp.exp(glast[i]) + upd[i]
        on = o[i] * lax.rsqrt(jnp.mean(o[i] * o[i], axis=-1, keepdims=True) + EPS) * ng
        o_ref[bi, :, h * DN_DK:(h + 1) * DN_DK] = (
            on * _silu(z_ref[bi, :, h * DN_DK:(h + 1) * DN_DK])).astype(o_ref.dtype)


def _dn_call(main3, conv_w, alog_row, dtb_row, norm_g):
    bsz, seq, _ = main3.shape
    c = DN_CHUNK
    nbt = DN_BATCH if bsz % DN_BATCH == 0 else 1
    halo_per_chunk = c // DN_HALO
    w3 = 3 * DN_W
    return pl.pallas_call(
        _dn_kernel,
        grid=(bsz // nbt, seq // c),
        in_specs=[pl.BlockSpec((nbt, c, w3), lambda b, s: (b, s, COL_QKV // w3)),
                  pl.BlockSpec((nbt, DN_HALO, w3),
                               lambda b, s: (b, jnp.maximum(s * halo_per_chunk - 1, 0), COL_QKV // w3)),
                  pl.BlockSpec((nbt, c, DN_W), lambda b, s: (b, s, COL_Z // DN_W)),
                  pl.BlockSpec((nbt, c, LANES), lambda b, s: (b, s, COL_GATE // LANES)),
                  pl.BlockSpec((SUBLANES, w3), lambda b, s: (0, 0)),
                  pl.BlockSpec((1, LANES), lambda b, s: (0, 0)),
                  pl.BlockSpec((1, LANES), lambda b, s: (0, 0)),
                  pl.BlockSpec((1, DN_DK), lambda b, s: (0, 0))],
        out_specs=pl.BlockSpec((nbt, c, DN_W), lambda b, s: (b, s, 0)),
        out_shape=jax.ShapeDtypeStruct((bsz, seq, DN_W), BF16),
        scratch_shapes=[pltpu.VMEM((nbt * DN_HEADS, DN_DK, DN_DK), F32)],
        compiler_params=_cparams(("parallel", "arbitrary")),
        name="dn",
    )(main3, main3, main3, main3, conv_w, alog_row, dtb_row, norm_g.reshape(1, DN_DK))


def _merge_kernel(ya_ref, yb_ref, yc_ref, mg_ref, x_ref, gt_ref, wa_ref, wb_ref, wc_ref, wo_ref, o_ref):
    d = x_ref.shape[1]
    gates = jax.nn.sigmoid(mg_ref[...].astype(F32))
    merged = (gates[:, :d] * jnp.dot(ya_ref[...], wa_ref[...], preferred_element_type=F32)
              + gates[:, d:2 * d] * jnp.dot(yb_ref[...], wb_ref[...], preferred_element_type=F32)
              + gates[:, 2 * d:] * jnp.dot(yc_ref[...], wc_ref[...], preferred_element_type=F32))
    y = jnp.dot(merged.astype(BF16), wo_ref[...], preferred_element_type=F32)
    o_ref[...] = x_ref[...] + gt_ref[0] * y


def _merge_call(ya, yb, yc, mg, x2, gt, wa, wb, wc, wo, seq):
    t, d = x2.shape
    tm = min(MERGE_TM, seq)
    per_b = seq // tm
    full = lambda a: pl.BlockSpec(a.shape, lambda i: (0, 0))
    return pl.pallas_call(
        _merge_kernel,
        grid=(t // tm,),
        in_specs=[pl.BlockSpec((tm, ya.shape[1]), lambda i: (i, 0)),
                  pl.BlockSpec((tm, yb.shape[1]), lambda i: (i, 0)),
                  pl.BlockSpec((tm, yc.shape[1]), lambda i: (i, 0)),
                  pl.BlockSpec((tm, mg.shape[1]), lambda i: (i, 0)),
                  pl.BlockSpec((tm, d), lambda i: (i, 0)),
                  pl.BlockSpec((1, 1, d), lambda i: (i // per_b, 0, 0)),
                  full(wa), full(wb), full(wc), full(wo)],
        out_specs=pl.BlockSpec((tm, d), lambda i: (i, 0)),
        out_shape=jax.ShapeDtypeStruct((t, d), F32),
        compiler_params=_cparams(("parallel",)),
        name="merge",
    )(ya, yb, yc, mg, x2, gt, wa, wb, wc, wo)


def _mlp_kernel(x_ref, g_ref, sc_ref, sh_ref, gt_ref, w1_ref, w2_ref, fg_ref, o_ref, acc_scr, *, final):
    j = pl.program_id(1)

    @pl.when((pl.program_id(0) == 0) & (j == 0))
    def _():
        acc_scr[...] = jnp.zeros(acc_scr.shape, F32)

    x = x_ref[...]
    h = _norm_mod(x, g_ref[...], sc_ref[0], sh_ref[0]).astype(BF16)
    a = jnp.maximum(jnp.dot(h, w1_ref[...], preferred_element_type=F32), 0.0)
    acc = acc_scr[...] + jnp.dot((a * a).astype(BF16), w2_ref[...], preferred_element_type=F32)
    acc_scr[...] = jnp.where(j == pl.num_programs(1) - 1, 0.0, acc)
    y = x + gt_ref[0] * acc
    if final:
        ms = jnp.mean(y * y, axis=-1, keepdims=True)
        y = y * lax.rsqrt(ms + EPS) * fg_ref[...]
    o_ref[...] = y


def _mlp_call(x2, g, sc, sh, gt, w1, w2, fg, seq, final):
    t, d = x2.shape
    f = w1.shape[1]
    tm = min(MLP_TM, seq)
    tf = MLP_TF
    per_b = seq // tm
    mod = lambda: pl.BlockSpec((1, 1, d), lambda i, j: (i // per_b, 0, 0))
    return pl.pallas_call(
        functools.partial(_mlp_kernel, final=final),
        grid=(t // tm, f // tf),
        in_specs=[pl.BlockSpec((tm, d), lambda i, j: (i, 0)),
                  pl.BlockSpec((1, d), lambda i, j: (0, 0)),
                  mod(), mod(), mod(),
                  pl.BlockSpec((d, tf), lambda i, j: (0, j)),
                  pl.BlockSpec((tf, d), lambda i, j: (j, 0)),
                  pl.BlockSpec((1, d), lambda i, j: (0, 0))],
        out_specs=pl.BlockSpec((tm, d), lambda i, j: (i, 0)),
        out_shape=jax.ShapeDtypeStruct((t, d), F32),
        scratch_shapes=[pltpu.VMEM((tm, d), F32)],
        compiler_params=_cparams(("arbitrary", "arbitrary")),
        name="mlp",
    )(x2, g.reshape(1, d), sc, sh, gt, w1, w2, fg.reshape(1, d))


def _split_w_in(w_in):
    offs = [0]
    for sz in IN_SIZES:
        offs.append(offs[-1] + sz)
    return [w_in[:, offs[i]:offs[i + 1]] for i in range(len(IN_SIZES))]


def _layout_w_in(w_in):
    nq, nkc, nvc, nks, nvs, nkw, nvw, ngate, pin, dqkv, dz, dbeta, da, mg = _split_w_in(w_in)
    d = w_in.shape[0]
    per_g = 3 * NSA_REP
    zeros = lambda n: jnp.zeros((d, n), w_in.dtype)
    gate0 = jnp.concatenate([ngate[:, :per_g], zeros(LANE_BETA - per_g), dbeta, da,
                             zeros(LANES - LANE_DECAY - DN_HEADS)], axis=1)
    gate1 = jnp.concatenate([ngate[:, per_g:], zeros(LANES - per_g)], axis=1)
    main = jnp.concatenate([nq, pin, dz, dqkv, gate0, gate1, nkc, nvc], axis=1)

    def per_group(w):
        pad = zeros(LANES - NSA_DH)
        return jnp.concatenate([w[:, :NSA_DH], pad, w[:, NSA_DH:], pad], axis=1)

    kv = jnp.concatenate([per_group(nks), per_group(nvs), per_group(nkw), per_group(nvw)], axis=1)
    return main.astype(BF16), mg.astype(BF16), kv.astype(BF16)


def _group_lane_w1(w1):
    two, _, hid = w1.shape
    halves = w1.reshape(two, 2, CMP_STRIDE, NSA_DH, hid)
    out = []
    for g in range(NSA_GROUPS):
        padded = jnp.zeros((two, 2, CMP_STRIDE, LANES, hid), w1.dtype).at[:, :, :, g * NSA_DH:(g + 1) * NSA_DH].set(halves)
        out.append(padded.reshape(two, 2, CMP_STRIDE * LANES, hid))
    return jnp.stack(out, axis=1)


def _nsa_constants(seq):
    nb = seq // SEL_LEN
    ncp = seq // CMP_STRIDE
    j = jnp.arange(nb)[:, None] * SEL_LEN
    i = jnp.arange(ncp)[None, :] * CMP_STRIDE
    overlap = jnp.clip(jnp.minimum(j + SEL_LEN, i + CMP_LEN) - jnp.maximum(j, i), 0) // CMP_STRIDE
    block_onehot = (jnp.arange(seq)[:, None] // SEL_LEN) == jnp.arange(nb)[None, :]
    return overlap.astype(BF16), block_onehot.astype(BF16)


def kernel(x, c, ada_w, ada_b, norm1_g, norm2_g, w_in, phi_k1, phi_k2, phi_v1, phi_v2, pos_k, pos_v, pool_w, pool_scale, dn_conv_w, dn_A_log, dn_dt_bias, dn_norm_g, w_branch_nsa, w_branch_pool, w_branch_dn, w_out, mlp_w1, mlp_w2, final_g):
    bsz, seq, d = x.shape
    assert d == D_MODEL and seq % max(PROJ_TM, MLP_TM, POOL_TS) == 0 and seq % KV_CHUNK == 0
    assert seq % (ATTN_Q_TILE * ATTN_TILES * ATTN_GROUPS) == 0 and (seq // Q_TILE) % SELECT_BUCKETS == 0
    assert DN_CHUNK == DN_DK == LANES
    t = bsz * seq
    depth = ada_w.shape[0]
    chunk = KV_CHUNK

    slopes = 2.0 ** (-(8.0 / NSA_HEADS) * (jnp.arange(NSA_HEADS, dtype=F32) + 1.0))
    overlap, block_onehot = _nsa_constants(seq)
    c_pad = jnp.zeros((SUBLANES, d), F32).at[:bsz].set(c)
    x2 = x.reshape(t, d)

    for l in range(depth):
        mod = _mod_call(c_pad, ada_w[l], ada_b[l])[:bsz]
        sh1, sc1, gt1, sh2, sc2, gt2 = [m.reshape(bsz, 1, d) for m in jnp.split(mod, 6, axis=-1)]

        w_main, w_mg, w_kv = _layout_w_in(w_in[l])
        main = _proj_call(x2, norm1_g[l], sc1, sh1, w_main, seq, PROJ_MAIN_TM, F32, "proj_main")
        mg = _proj_call(x2, norm1_g[l], sc1, sh1, w_mg, seq, PROJ_TM, BF16, "proj_mg")
        kv = _proj_kv_call(x2, norm1_g[l], sc1, sh1, w_kv, seq)

        main3 = main.reshape(bsz, seq, N_MAIN)
        pos = jnp.zeros((2, SUBLANES, CMP_LEN * NSA_DH), F32).at[:, 0].set(
            jnp.stack([pos_k[l].reshape(-1), pos_v[l].reshape(-1)])).astype(BF16)
        w1 = jnp.stack([phi_k1[l], phi_v1[l]]).astype(BF16)
        w2 = jnp.stack([phi_k2[l], phi_v2[l]])
        w2 = jnp.concatenate([w2, w2], axis=-1).astype(BF16)
        cmp_kv = _compress_call(main3, pos, w1, _group_lane_w1(w1), w2)

        sel, act, yc = _select_call(slopes, main, cmp_kv[0], cmp_kv[1], overlap, bsz, seq)
        lst, cnt = _chunk_lists(act, seq, chunk)
        y_a = _attn_call(lst, cnt, slopes, main, sel, kv, yc, block_onehot, bsz, seq, chunk)

        y_b = _pool_call(main, pool_w[l].astype(BF16), pool_scale[l], bsz, seq)

        conv_w = jnp.zeros((SUBLANES, 3 * DN_W), F32).at[:DN_CONV].set(dn_conv_w[l])
        lane_row = lambda v: jnp.zeros((1, LANES), F32).at[0, LANE_DECAY:LANE_DECAY + DN_HEADS].set(v)
        y_c = _dn_call(main3, conv_w, lane_row(dn_A_log[l]), lane_row(dn_dt_bias[l]),
                       dn_norm_g[l]).reshape(t, DN_W)

        x2 = _merge_call(y_a, y_b, y_c, mg, x2, gt1,
                         w_branch_nsa[l].astype(BF16), w_branch_pool[l].astype(BF16),
                         w_branch_dn[l].astype(BF16), w_out[l].astype(BF16), seq)
        x2 = _mlp_call(x2, norm2_g[l], sc2, sh2, gt2, mlp_w1[l].astype(BF16), mlp_w2[l].astype(BF16),
                       final_g, seq, final=(l == depth - 1))
    return x2.reshape(bsz, seq, d)
```

```python
import functools

import jax
import jax.numpy as jnp
from jax import lax
from jax.experimental import pallas as pl
from jax.experimental.pallas import tpu as pltpu

F32 = jnp.float32
BF16 = jnp.bfloat16
HIGHEST = lax.Precision.HIGHEST

D_MODEL = 1024
NSA_HEADS = 8
NSA_GROUPS = 2
NSA_REP = NSA_HEADS // NSA_GROUPS
NSA_DH = 64
CMP_LEN = 32
CMP_STRIDE = 16
SEL_LEN = 64
SEL_SHIFT = 6
TOP_N = 16
WINDOW = 512
FORCE_BONUS = 1.0e4
POOL_WINDOWS = (2, 4, 8, 16)
POOL_GROUP_DIM = 128
POOL_W = len(POOL_WINDOWS) * POOL_GROUP_DIM
DN_HEADS = 4
DN_DK = 128
DN_W = DN_HEADS * DN_DK
DN_CONV = 4
EPS = 1e-6
NEG = -1e30
NSA_Q_W = NSA_HEADS * NSA_DH
NSA_KV_W = NSA_GROUPS * NSA_DH
IN_SIZES = (NSA_Q_W, NSA_KV_W, NSA_KV_W, NSA_KV_W, NSA_KV_W, NSA_KV_W, NSA_KV_W,
            3 * NSA_HEADS, POOL_W, 3 * DN_W, DN_W, DN_HEADS, DN_HEADS, 3 * D_MODEL)

LANES = 128
LANE_SHIFT = 7
SUBLANES = 8
VMEM_LIMIT_BYTES = 48 * 1024 * 1024

Q_TILE = 128
SELECT_BUCKETS = 4
ATTN_Q_TILE = 128
ATTN_TILES = 2
ATTN_GROUPS = 4
KV_CHUNK = 256
DN_CHUNK = 128
DN_BATCH = 4
PROJ_TM = 1024
MLP_TM = 1024
MLP_TF = 2048
MERGE_TM = 512
POOL_TS = 1024
POOL_HALO = 16
DN_HALO = 8

COL_Q = 0
COL_POOL = 512
COL_Z = 1024
COL_QKV = 1536
COL_GATE = 3072
COL_KC = 3328
COL_VC = 3456
N_MAIN = 3584
PROJ_MAIN_TM = 512
N_MG = 3072
LANE_BETA = 16
LANE_DECAY = 20
KVB_KS = 0
KVB_VS = 2
KVB_KW = 4
KVB_VW = 6
KVB_END = 8
N_KV = KVB_END * LANES
LANE_POS_HI = NSA_DH
LANE_POS_LO = NSA_DH + 1


def _cparams(sem):
    return pltpu.CompilerParams(dimension_semantics=sem, vmem_limit_bytes=VMEM_LIMIT_BYTES)


def _silu(v):
    return v * jax.nn.sigmoid(v)


def _mm(a, b):
    return jnp.dot(a.astype(BF16), b.astype(BF16), preferred_element_type=F32)


def _mm_nt(a, b):
    return lax.dot_general(a.astype(BF16), b.astype(BF16), (((1,), (1,)), ((), ())), preferred_element_type=F32)


def _mm_tn(a, b):
    return lax.dot_general(a.astype(BF16), b.astype(BF16), (((0,), (0,)), ((), ())), preferred_element_type=F32)


def _mod_kernel(c_ref, w_ref, b_ref, o_ref):
    cond = _silu(c_ref[...])
    o_ref[...] = jnp.dot(cond, w_ref[...], preferred_element_type=F32, precision=HIGHEST) + b_ref[...]


def _mod_call(c_pad, w, b, layer):
    _, d, n = w.shape
    tn = 1536
    return pl.pallas_call(
        _mod_kernel,
        grid=(n // tn,),
        in_specs=[pl.BlockSpec((SUBLANES, d), lambda j: (0, 0)),
                  pl.BlockSpec((None, d, tn), lambda j: (layer, 0, j)),
                  pl.BlockSpec((None, 1, tn), lambda j: (layer, 0, j))],
        out_specs=pl.BlockSpec((SUBLANES, tn), lambda j: (0, j)),
        out_shape=jax.ShapeDtypeStruct((SUBLANES, n), F32),
        compiler_params=_cparams(("parallel",)),
        name="mod",
    )(c_pad, w, b)


def _norm_mod(x, g, sc, sh):
    ms = jnp.mean(x * x, axis=-1, keepdims=True)
    return (x * lax.rsqrt(ms + EPS) * g) * (1.0 + sc) + sh


def _proj_kernel(x_ref, g_ref, sc_ref, sh_ref, w_ref, o_ref):
    h = _norm_mod(x_ref[...], g_ref[...], sc_ref[0], sh_ref[0]).astype(BF16)
    o_ref[...] = jnp.dot(h, w_ref[...], preferred_element_type=F32).astype(o_ref.dtype)


def _proj_kv_kernel(x_ref, g_ref, sc_ref, sh_ref, w_ref, o_ref, *, tiles_per_seq):
    h = _norm_mod(x_ref[...], g_ref[...], sc_ref[0], sh_ref[0]).astype(BF16)
    acc = jnp.dot(h, w_ref[...], preferred_element_type=F32)
    tm, n = acc.shape
    col = lax.broadcasted_iota(jnp.int32, (1, n), 1)
    blk = jnp.right_shift(col, LANE_SHIFT)
    lane = col & (LANES - 1)
    is_key = ((blk >= KVB_KS) & (blk < KVB_VS)) | ((blk >= KVB_KW) & (blk < KVB_VW))
    is_val = ((blk >= KVB_VS) & (blk < KVB_KW)) | ((blk >= KVB_VW) & (blk < KVB_END))
    ones_row = jnp.where(is_val & (lane >= NSA_DH), 1.0, 0.0)
    hi_row = jnp.where(is_key & (lane == LANE_POS_HI), 1.0, 0.0)
    lo_row = jnp.where(is_key & (lane == LANE_POS_LO), 1.0, 0.0)
    pos = (pl.program_id(0) % tiles_per_seq) * tm + lax.broadcasted_iota(jnp.int32, (tm, 1), 0)
    pos_hi = jnp.right_shift(pos, LANE_SHIFT).astype(F32)
    pos_lo = (pos & (LANES - 1)).astype(F32)
    o_ref[...] = (acc + ones_row + pos_hi * hi_row + pos_lo * lo_row).astype(o_ref.dtype)


def _proj_specs(tm, d, tn, per_b):
    return [pl.BlockSpec((tm, d), lambda i, j: (i, 0)),
            pl.BlockSpec((1, d), lambda i, j: (0, 0)),
            pl.BlockSpec((1, 1, d), lambda i, j: (i // per_b, 0, 0)),
            pl.BlockSpec((1, 1, d), lambda i, j: (i // per_b, 0, 0)),
            pl.BlockSpec((d, tn), lambda i, j: (0, j))]


def _proj_call(x2, g, sc, sh, w, seq, tm, out_dtype, name):
    t, d = x2.shape
    n = w.shape[1]
    tm = min(tm, seq)
    return pl.pallas_call(
        _proj_kernel,
        grid=(t // tm, 1),
        in_specs=_proj_specs(tm, d, n, seq // tm),
        out_specs=pl.BlockSpec((tm, n), lambda i, j: (i, j)),
        out_shape=jax.ShapeDtypeStruct((t, n), out_dtype),
        compiler_params=_cparams(("parallel", "arbitrary")),
        name=name,
    )(x2, g.reshape(1, d), sc, sh, w)


def _proj_kv_call(x2, g, sc, sh, w, seq):
    t, d = x2.shape
    n = w.shape[1]
    tm = min(PROJ_TM, seq)
    return pl.pallas_call(
        functools.partial(_proj_kv_kernel, tiles_per_seq=seq // tm),
        grid=(t // tm, 1),
        in_specs=_proj_specs(tm, d, n, seq // tm),
        out_specs=pl.BlockSpec((tm, n), lambda i, j: (i, j)),
        out_shape=jax.ShapeDtypeStruct((t, n), BF16),
        compiler_params=_cparams(("parallel", "arbitrary")),
        name="proj_kv",
    )(x2, g.reshape(1, d), sc, sh, w)


def _compress_kernel(x_ref, pos_ref, w1_ref, w1g_ref, w2_ref, o_ref):
    nr = x_ref.shape[1] // CMP_STRIDE
    xs = jnp.concatenate([x_ref[0, pl.ds(rho, nr, stride=CMP_STRIDE), :].astype(BF16) for rho in range(CMP_STRIDE)],
                         axis=1)
    posb = jnp.dot(pos_ref[0], w1_ref[0], preferred_element_type=F32)[0:1]
    for g in range(NSA_GROUPS):
        top = jnp.dot(xs, w1g_ref[0, g, 0], preferred_element_type=F32)
        bot = jnp.dot(xs, w1g_ref[0, g, 1], preferred_element_type=F32)
        hid = top + pltpu.roll(bot, nr - 1, 0) + posb
        o_ref[0, 0, g] = _mm(_silu(hid), w2_ref[0]).astype(o_ref.dtype)


def _compress_call(main3, pos, w1, w1g, w2):
    b, seq, _ = main3.shape
    nr = seq // CMP_STRIDE
    two, l64, hid = w1.shape
    return pl.pallas_call(
        _compress_kernel,
        grid=(two, b),
        in_specs=[pl.BlockSpec((1, seq, LANES), lambda a, i: (i, 0, COL_KC // LANES + a)),
                  pl.BlockSpec((1, SUBLANES, l64), lambda a, i: (a, 0, 0)),
                  pl.BlockSpec((1, l64, hid), lambda a, i: (a, 0, 0)),
                  pl.BlockSpec((1,) + w1g.shape[1:], lambda a, i: (a, 0, 0, 0, 0)),
                  pl.BlockSpec((1, hid, LANES), lambda a, i: (a, 0, 0))],
        out_specs=pl.BlockSpec((1, 1, NSA_GROUPS, nr, LANES), lambda a, i: (a, i, 0, 0, 0)),
        out_shape=jax.ShapeDtypeStruct((two, b, NSA_GROUPS, nr, LANES), BF16),
        compiler_params=_cparams(("parallel", "parallel")),
        name="compress",
    )(main3, pos, w1, w1g, w2)


def _gate_cols(gate_logits, branch):
    sg = jax.nn.sigmoid(gate_logits)
    return [sg[:, 3 * r + branch:3 * r + branch + 1] for r in range(NSA_REP)]


def _stack_heads_dup(q):
    qa = q[:, :LANES]
    qb = q[:, LANES:]
    lo = lax.broadcasted_iota(jnp.int32, qa.shape, 1) < NSA_DH
    z = jnp.zeros_like(qa)
    return jnp.concatenate([jnp.where(lo, qa, z), jnp.where(lo, z, qa),
                            jnp.where(lo, qb, z), jnp.where(lo, z, qb)], axis=0)


def _unstack_heads_dup(o, gates):
    nq = o.shape[0] // NSA_REP
    lo = lax.broadcasted_iota(jnp.int32, (nq, LANES), 1) < NSA_DH
    a = jnp.where(lo, gates[0] * o[0:nq], gates[1] * o[nq:2 * nq])
    b = jnp.where(lo, gates[2] * o[2 * nq:3 * nq], gates[3] * o[3 * nq:])
    return jnp.concatenate([a, b], axis=1)


def _stack_heads_pos(q, slopes):
    qa = q[:, :LANES]
    qb = q[:, LANES:]
    lane = lax.broadcasted_iota(jnp.int32, qa.shape, 1)
    lo = lane < NSA_DH
    heads = [qa, pltpu.roll(qa, NSA_DH, 1), qb, pltpu.roll(qb, NSA_DH, 1)]
    out = []
    for r in range(NSA_REP):
        extra = jnp.where(lane == LANE_POS_HI, slopes[r] * LANES, jnp.where(lane == LANE_POS_LO, slopes[r], 0.0))
        out.append(jnp.where(lo, heads[r], extra))
    return jnp.concatenate(out, axis=0)


def _finish_heads(acc, gates):
    nq = acc.shape[0] // NSA_REP
    o = acc * pltpu.roll(1.0 / acc, NSA_DH, 1)
    lo = lax.broadcasted_iota(jnp.int32, (nq, LANES), 1) < NSA_DH
    hs = [gates[r] * o[r * nq:(r + 1) * nq] for r in range(NSA_REP)]
    a = jnp.where(lo, hs[0], pltpu.roll(hs[1], NSA_DH, 1))
    b = jnp.where(lo, hs[2], pltpu.roll(hs[3], NSA_DH, 1))
    return jnp.concatenate([a, b], axis=1)


def _select_kernel(slopes_ref, q_ref, kc_ref, vc_ref, gates_ref, ov_ref, sel_ref, act_ref, yc_ref):
    qt = pl.program_id(1)
    nqt = pl.num_programs(1)
    nb_all, ncp_all = ov_ref.shape
    for k in range(SELECT_BUCKETS):
        @pl.when((qt * SELECT_BUCKETS) // nqt == k)
        def _(k=k):
            _select_prefix(slopes_ref, q_ref, kc_ref, vc_ref, gates_ref, ov_ref, sel_ref, act_ref, yc_ref,
                           nb=(k + 1) * nb_all // SELECT_BUCKETS, ncp=(k + 1) * ncp_all // SELECT_BUCKETS)


def _select_prefix(slopes_ref, q_ref, kc_ref, vc_ref, gates_ref, ov_ref, sel_ref, act_ref, yc_ref, *, nb, ncp):
    t0 = pl.program_id(1) * Q_TILE
    nq = Q_TILE
    nb_all = ov_ref.shape[0]
    c_end = lax.broadcasted_iota(jnp.int32, (nq, ncp), 1) * CMP_STRIDE + (CMP_LEN - 1)
    tq = t0 + lax.broadcasted_iota(jnp.int32, (nq, ncp), 0)
    vis = c_end <= tq
    any_vis = tq[:, 0:1] >= CMP_LEN - 1
    rel = (c_end[0:1] - t0).astype(F32)
    ov = ov_ref[:nb, :ncp]
    jb = lax.broadcasted_iota(jnp.int32, (nb, nq), 0)
    tql = t0 + lax.broadcasted_iota(jnp.int32, (nb, nq), 1)
    valid = jb * SEL_LEN <= tql
    forced = (jb == 0) | (jb == jnp.right_shift(tql, SEL_SHIFT))
    scores = []
    for g in range(NSA_GROUPS):
        qcols = slice(g * NSA_REP * NSA_DH, (g + 1) * NSA_REP * NSA_DH)
        qs = _stack_heads_dup(q_ref[:, qcols] * (NSA_DH ** -0.5)).astype(BF16)
        s = _mm_nt(qs, kc_ref[0, g, :ncp, :])
        psum = jnp.zeros((nq, ncp), F32)
        ps = []
        for r in range(NSA_REP):
            m_r = slopes_ref[g * NSA_REP + r]
            sr = jnp.where(vis, s[r * nq:(r + 1) * nq] + m_r * rel, NEG)
            mx = jnp.max(sr, axis=-1, keepdims=True)
            e = jnp.exp(sr - mx)
            den = jnp.sum(e, axis=-1, keepdims=True)
            p = e * jnp.where(any_vis, 1.0 / den, 0.0)
            psum = psum + p
            ps.append(p.astype(BF16))
        o_c = jnp.dot(jnp.concatenate(ps, axis=0), vc_ref[0, g, :ncp, :], preferred_element_type=F32)
        yc_ref[:, qcols] = _unstack_heads_dup(o_c, _gate_cols(gates_ref[:, g * LANES:(g + 1) * LANES], 0))
        p_hi = psum.astype(BF16)
        p_lo = (psum - p_hi.astype(F32)).astype(BF16)
        imp_t = _mm_nt(ov, p_hi) + _mm_nt(ov, p_lo)
        scores.append(jnp.where(valid, imp_t + FORCE_BONUS * forced.astype(F32), NEG))

    for _ in range(min(TOP_N, nb)):
        for g in range(NSA_GROUPS):
            mx = jnp.max(scores[g], axis=0, keepdims=True)
            idx = jnp.min(jnp.where(scores[g] == mx, jb, nb), axis=0, keepdims=True)
            scores[g] = jnp.where(jb == idx, -jnp.inf, scores[g])
    for g in range(NSA_GROUPS):
        sel_t = jnp.where((scores[g] == -jnp.inf) & valid, 1.0, 0.0)
        if nb < nb_all:
            sel_t = jnp.concatenate([sel_t, jnp.zeros((nb_all - nb, nq), F32)], axis=0)
        sel_q = sel_t.T
        sel_ref[0, g] = sel_q.astype(sel_ref.dtype)
        act_ref[0, g, 0] = jnp.max(sel_q, axis=0, keepdims=True)


def _select_call(slopes, main, kcmp, vcmp, ov, bsz, seq):
    nqt = seq // Q_TILE
    nb, ncp = ov.shape
    row = lambda b, q: b * nqt + q
    gw = NSA_GROUPS * LANES
    return pl.pallas_call(
        _select_kernel,
        grid=(bsz, nqt),
        in_specs=[pl.BlockSpec(memory_space=pltpu.SMEM),
                  pl.BlockSpec((Q_TILE, NSA_Q_W), lambda b, q: (row(b, q), COL_Q // NSA_Q_W)),
                  pl.BlockSpec((1, NSA_GROUPS, ncp, LANES), lambda b, q: (b, 0, 0, 0)),
                  pl.BlockSpec((1, NSA_GROUPS, ncp, LANES), lambda b, q: (b, 0, 0, 0)),
                  pl.BlockSpec((Q_TILE, gw), lambda b, q: (row(b, q), COL_GATE // gw)),
                  pl.BlockSpec((nb, ncp), lambda b, q: (0, 0))],
        out_specs=[pl.BlockSpec((1, NSA_GROUPS, Q_TILE, nb), lambda b, q: (b, 0, q, 0)),
                   pl.BlockSpec((1, NSA_GROUPS, 1, 1, nb), lambda b, q: (b, 0, q, 0, 0)),
                   pl.BlockSpec((Q_TILE, NSA_Q_W), lambda b, q: (row(b, q), 0))],
        out_shape=[jax.ShapeDtypeStruct((bsz, NSA_GROUPS, seq, nb), BF16),
                   jax.ShapeDtypeStruct((bsz, NSA_GROUPS, nqt, 1, nb), F32),
                   jax.ShapeDtypeStruct((bsz * seq, NSA_Q_W), F32)],
        compiler_params=_cparams(("parallel", "parallel")),
        name="select",
    )(slopes, main, kcmp, vcmp, main, ov)


def _attn_kernel(lst_ref, cnt_ref, slopes_ref, q_ref, sel_ref, ks_ref, vs_ref, kw_ref, vw_ref, gates_ref, yc_ref,
                 et_ref, o_ref, s_scr, p_scr, m_scr, a_scr, acc_scr, *, chunk):
    b = pl.program_id(0)
    g = pl.program_id(1)
    step = pl.program_id(2)
    tiles_per_step = ATTN_TILES * ATTN_GROUPS
    nqt = pl.num_programs(2) * tiles_per_step
    nq = ATTN_Q_TILE
    nw = WINDOW + nq
    nch_total = ks_ref.shape[0] // chunk
    slopes = [slopes_ref[g * NSA_REP + r] for r in range(NSA_REP)]

    def values(p, c):
        start = pl.multiple_of(c * chunk, chunk)
        return jnp.dot(p, vs_ref[pl.ds(start, chunk), :], preferred_element_type=F32)

    def row_max(s):
        return jnp.broadcast_to(jnp.max(s, axis=-1, keepdims=True), (s.shape[0], LANES))

    def lane_tile(m, width):
        return jnp.concatenate([m] * (width // LANES), axis=1)

    def masked(s, ok):
        return jnp.concatenate([jnp.where(ok, s[r * nq:(r + 1) * nq], NEG) for r in range(NSA_REP)], axis=0)

    class Group:
        pass

    def setup(k):
        gr = Group()
        gr.tiles = range(ATTN_TILES)
        gr.scr = [(k % 2) * ATTN_TILES + u for u in gr.tiles]
        local = [k * ATTN_TILES + u for u in gr.tiles]
        gr.rws = [slice(v * nq, (v + 1) * nq) for v in local]
        qts = [step * tiles_per_step + v for v in local]
        gr.t0s = [qt * nq for qt in qts]
        gr.qs = [_stack_heads_pos(q_ref[rw, :] * (NSA_DH ** -0.5), slopes).astype(BF16) for rw in gr.rws]
        pens = [((sel_ref[0, 0, rw, :].astype(F32) - 1.0) * (-NEG)).astype(BF16) for rw in gr.rws]
        gr.qx = [jnp.concatenate([gr.qs[u], jnp.concatenate([pens[u]] * NSA_REP, axis=0)], axis=1) for u in gr.tiles]
        gr.bases = [((b * NSA_GROUPS + g) * nqt + qt) * nch_total for qt in qts]
        gr.cnts = [cnt_ref[(b * NSA_GROUPS + g) * nqt + qt] for qt in qts]
        for u in gr.tiles:
            su = gr.scr[u]
            m_scr[su] = jnp.full(m_scr.shape[1:], NEG, F32)
            a_scr[su] = jnp.ones(a_scr.shape[1:], F32)
            acc_scr[su] = jnp.zeros(acc_scr.shape[1:], F32)
            p_scr[su, 1] = jnp.zeros(p_scr.shape[2:], BF16)
        for u in gr.tiles:
            s_scr[gr.scr[u], 0] = logits(gr, u, lst_ref[gr.bases[u]])
        return gr

    def logits(gr, u, c):
        start = pl.multiple_of(c * chunk, chunk)
        kx = jnp.concatenate([ks_ref[pl.ds(start, chunk), :], et_ref[pl.ds(start, chunk), :]], axis=1)
        return _mm_nt(gr.qx[u], kx)

    def flash_step(gr, us, i, slot):
        for u in us:
            su = gr.scr[u]
            acc_scr[su] = a_scr[su] * acc_scr[su] + values(p_scr[su, 1 - slot],
                                                           lst_ref[gr.bases[u] + jnp.maximum(i - 1, 0)])
        for u in us:
            su = gr.scr[u]
            s = s_scr[su, slot]
            m_prev = m_scr[su]
            m_new = jnp.maximum(m_prev, row_max(s))
            a_scr[su] = jnp.exp(m_prev - m_new)
            p_scr[su, slot] = jnp.exp(s - lane_tile(m_new, chunk)).astype(BF16)
            m_scr[su] = m_new
        for u in us:
            s_scr[gr.scr[u], 1 - slot] = logits(gr, u, lst_ref[gr.bases[u] + i + 1])

    def flash_loops(gr):
        def single_steps(us):
            def body(i, carry):
                flash_step(gr, us, i, i & 1)
                return carry
            return body

        def double_steps(us):
            def body(j, carry):
                flash_step(gr, us, 2 * j, 0)
                flash_step(gr, us, 2 * j + 1, 1)
                return carry
            return body

        joint = functools.reduce(jnp.minimum, gr.cnts)
        lax.fori_loop(0, joint // 2, double_steps(list(gr.tiles)), 0)
        lax.fori_loop(joint - (joint & 1), joint, single_steps(list(gr.tiles)), 0)
        for u in gr.tiles:
            lax.fori_loop(joint, gr.cnts[u], single_steps([u]), 0)

    def finish(gr):
        tiles = gr.tiles
        scr = gr.scr
        slots = [gr.cnts[u] & 1 for u in tiles]
        c_diag = [gr.t0s[u] // chunk for u in tiles]
        acc = [a_scr[scr[u]] * acc_scr[scr[u]]
               + values(p_scr[scr[u], 1 - slots[u]], lst_ref[gr.bases[u] + jnp.maximum(gr.cnts[u] - 1, 0)])
               for u in tiles]
        wbase = [pl.multiple_of(jnp.maximum(gr.t0s[u] - WINDOW, 0), nq) for u in tiles]
        s_win = [_mm_nt(gr.qs[u], kw_ref[pl.ds(wbase[u], nw), :]) for u in tiles]
        s_diag = []
        for u in tiles:
            pos = c_diag[u] * chunk + lax.broadcasted_iota(jnp.int32, (nq, chunk), 1)
            tq = gr.t0s[u] + lax.broadcasted_iota(jnp.int32, (nq, chunk), 0)
            s_diag.append(masked(s_scr[scr[u], slots[u]], pos <= tq))
        m_prev = [m_scr[scr[u]] for u in tiles]
        m_new = [jnp.maximum(m_prev[u], row_max(s_diag[u])) for u in tiles]
        acc = [jnp.exp(m_prev[u] - m_new[u]) * acc[u]
               + values(jnp.exp(s_diag[u] - lane_tile(m_new[u], chunk)).astype(BF16), c_diag[u]) for u in tiles]
        p_win = []
        for u in tiles:
            pos = wbase[u] + lax.broadcasted_iota(jnp.int32, (nq, nw), 1)
            tq = gr.t0s[u] + lax.broadcasted_iota(jnp.int32, (nq, nw), 0)
            s = masked(s_win[u], (pos <= tq) & (pos > tq - WINDOW))
            p_win.append(jnp.exp(s - jnp.max(s, axis=-1, keepdims=True)).astype(BF16))
        acc_w = [jnp.dot(p_win[u], vw_ref[pl.ds(wbase[u], nw), :], preferred_element_type=F32) for u in tiles]
        for u in tiles:
            gates = gates_ref[gr.rws[u], :]
            o_ref[gr.rws[u], :] = (yc_ref[gr.rws[u], :] + _finish_heads(acc[u], _gate_cols(gates, 1))
                                   + _finish_heads(acc_w[u], _gate_cols(gates, 2))).astype(o_ref.dtype)

    cur = setup(0)
    for k in range(ATTN_GROUPS):
        flash_loops(cur)
        nxt = setup(k + 1) if k + 1 < ATTN_GROUPS else None
        finish(cur)
        cur = nxt


def _chunk_lists(act, seq, chunk):
    bsz, ngrp = act.shape[:2]
    nqt = seq // ATTN_Q_TILE
    nch = seq // chunk
    active = act.reshape(bsz, ngrp, nqt, ATTN_Q_TILE // Q_TILE, nch, chunk // SEL_LEN).max(axis=(3, 5)) > 0.0
    c_diag = (jnp.arange(nqt) * ATTN_Q_TILE) // chunk
    active = active & (jnp.arange(nch)[None, :] < c_diag[:, None])
    cnt = active.sum(axis=-1).astype(jnp.int32)
    order = jnp.argsort(jnp.logical_not(active), axis=-1, stable=True).astype(jnp.int32)
    lst = jnp.where(jnp.arange(nch) < cnt[..., None], order, c_diag[:, None].astype(jnp.int32))
    return lst.reshape(-1), cnt.reshape(-1)


def _attn_call(lst, cnt, slopes, main, sel, kv, yc, et, bsz, seq, chunk):
    tq = ATTN_Q_TILE * ATTN_TILES * ATTN_GROUPS
    nset = 2 * ATTN_TILES
    nqt = seq // tq
    nb = sel.shape[-1]
    rows = NSA_REP * ATTN_Q_TILE
    row = lambda b, g, q: b * nqt + q
    kvspec = lambda blk: pl.BlockSpec((seq, LANES), lambda b, g, q, *_: (b, blk + g))
    grid_spec = pltpu.PrefetchScalarGridSpec(
        num_scalar_prefetch=2,
        grid=(bsz, NSA_GROUPS, nqt),
        in_specs=[pl.BlockSpec(memory_space=pltpu.SMEM),
                  pl.BlockSpec((tq, 256), lambda b, g, q, *_: (row(b, g, q), COL_Q // 256 + g)),
                  pl.BlockSpec((1, 1, tq, nb), lambda b, g, q, *_: (b, g, q, 0)),
                  kvspec(KVB_KS), kvspec(KVB_VS), kvspec(KVB_KW), kvspec(KVB_VW),
                  pl.BlockSpec((tq, LANES), lambda b, g, q, *_: (row(b, g, q), COL_GATE // LANES + g)),
                  pl.BlockSpec((tq, 256), lambda b, g, q, *_: (row(b, g, q), g)),
                  pl.BlockSpec(et.shape, lambda b, g, q, *_: (0, 0))],
        out_specs=pl.BlockSpec((tq, 256), lambda b, g, q, *_: (row(b, g, q), g)),
        scratch_shapes=[pltpu.VMEM((nset, 2, rows, chunk), F32),
                        pltpu.VMEM((nset, 2, rows, chunk), BF16),
                        pltpu.VMEM((nset, rows, LANES), F32),
                        pltpu.VMEM((nset, rows, LANES), F32),
                        pltpu.VMEM((nset, rows, LANES), F32)],
    )
    return pl.pallas_call(
        functools.partial(_attn_kernel, chunk=chunk),
        grid_spec=grid_spec,
        out_shape=jax.ShapeDtypeStruct((bsz * seq, NSA_Q_W), BF16),
        compiler_params=_cparams(("parallel", "parallel", "parallel")),
        name="attn",
    )(lst, cnt, slopes, main, sel, kv, kv, kv, kv, main, yc, et)


def _pool_kernel(x_ref, halo_ref, w_ref, sc_ref, o_ref):
    s = pl.program_id(1)
    ts = x_ref.shape[0]
    x = x_ref[...]
    halo = jnp.where(s > 0, halo_ref[...], 0.0)
    t = s * ts + lax.broadcasted_iota(jnp.int32, (ts, 1), 0)
    for gi, win in enumerate(POOL_WINDOWS):
        lanes = slice(gi * POOL_GROUP_DIM, (gi + 1) * POOL_GROUP_DIM)
        ext = jnp.concatenate([halo[:, lanes], x[:, lanes]], axis=0)
        acc = ext
        span = 1
        while span < win:
            acc = acc + pltpu.roll(acc, span, 0)
            span *= 2
        cnt = jnp.minimum(t + 1, win).astype(F32)
        y = acc[POOL_HALO:] / cnt - x[:, lanes]
        o_ref[:, lanes] = (_mm(y, w_ref[gi]) * sc_ref[:, lanes]).astype(o_ref.dtype)


def _pool_call(main, w, scale, bsz, seq):
    ts = min(POOL_TS, seq)
    per_b = seq // ts
    halo_per_tile = ts // POOL_HALO
    return pl.pallas_call(
        _pool_kernel,
        grid=(bsz, per_b),
        in_specs=[pl.BlockSpec((ts, POOL_W), lambda b, s: (b * per_b + s, COL_POOL // POOL_W)),
                  pl.BlockSpec((POOL_HALO, POOL_W),
                               lambda b, s: (jnp.maximum((b * per_b + s) * halo_per_tile - 1, 0), COL_POOL // POOL_W)),
                  pl.BlockSpec(w.shape, lambda b, s: (0, 0, 0)),
                  pl.BlockSpec((1, POOL_W), lambda b, s: (0, 0))],
        out_specs=pl.BlockSpec((ts, POOL_W), lambda b, s: (b * per_b + s, 0)),
        out_shape=jax.ShapeDtypeStruct((bsz * seq, POOL_W), BF16),
        compiler_params=_cparams(("parallel", "parallel")),
        name="pool",
    )(main, main, w, scale.reshape(1, POOL_W))


def _unit_lower_inverses(nms, ri, ci):
    c = nms[0].shape[0]
    eye = jnp.where(ri == ci, 1.0, 0.0)
    in8 = jnp.right_shift(ri, 3) == jnp.right_shift(ci, 3)
    n8 = [jnp.where(in8, nm, 0.0) for nm in nms]
    ps = [eye - a for a in n8]
    ms = [_mm(a, a) for a in n8]
    ps = [p + _mm(p, m) for p, m in zip(ps, ms)]
    ms = [_mm(m, m) for m in ms]
    ps = [p + _mm(p, m) for p, m in zip(ps, ms)]
    shift = 3
    while (1 << shift) < c:
        rb = jnp.right_shift(ri, shift)
        cb = jnp.right_shift(ci, shift)
        lower_left = ((rb & 1) == 1) & (cb == rb - 1)
        ts = [_mm(p, jnp.where(lower_left, nm, 0.0)) for p, nm in zip(ps, nms)]
        ps = [p - _mm(t, p) for p, t in zip(ps, ts)]
        shift += 1
    return ps


def _row_cumsum(x, ri):
    span = 1
    while span < x.shape[0]:
        x = x + jnp.where(ri >= span, pltpu.roll(x, span, 0), 0.0)
        span *= 2
    return x


def _dn_kernel(qkv_ref, halo_ref, z_ref, gba_ref, cw_ref, alog_ref, dtb_ref, ng_ref, o_ref, st_scr):
    s = pl.program_id(1)
    nbatch, c = qkv_ref.shape[0], qkv_ref.shape[1]

    @pl.when(s == 0)
    def _():
        st_scr[...] = jnp.zeros(st_scr.shape, F32)

    ri = lax.broadcasted_iota(jnp.int32, (c, c), 0)
    ci = lax.broadcasted_iota(jnp.int32, (c, c), 1)
    tril = ri >= ci
    strict = ri > ci
    cw = cw_ref[...]
    ng = ng_ref[...]
    chains = [(bi, h) for bi in range(nbatch) for h in range(DN_HEADS)]
    qn, kn, kb, vb, gcb, lmask = [], [], [], [], [], []
    for bi in range(nbatch):
        halo = jnp.where(s > 0, halo_ref[bi], 0.0)
        ext = jnp.concatenate([halo, qkv_ref[bi]], axis=0)
        conv = cw[DN_CONV - 1:DN_CONV] * ext
        for tap in range(1, DN_CONV):
            conv = conv + cw[DN_CONV - 1 - tap:DN_CONV - tap] * pltpu.roll(ext, tap, 0)
        act = _silu(conv[DN_HALO:])

        gba = gba_ref[bi]
        beta_all = jax.nn.sigmoid(gba)
        xs = gba + dtb_ref[...]
        softplus = jnp.maximum(xs, 0.0) + jnp.log(1.0 + jnp.exp(-jnp.abs(xs)))
        gc_all = _row_cumsum(-jnp.exp(alog_ref[...]) * softplus, ri)
        gct_all = gc_all.T
        for h in range(DN_HEADS):
            q = act[:, h * DN_DK:(h + 1) * DN_DK]
            k = act[:, DN_W + h * DN_DK:DN_W + (h + 1) * DN_DK]
            v = act[:, 2 * DN_W + h * DN_DK:2 * DN_W + (h + 1) * DN_DK]
            qn.append(q * lax.rsqrt(jnp.sum(q * q, axis=-1, keepdims=True) + EPS) * (DN_DK ** -0.5))
            kn.append(k * lax.rsqrt(jnp.sum(k * k, axis=-1, keepdims=True) + EPS))
            beta = beta_all[:, LANE_BETA + h:LANE_BETA + h + 1]
            gcb.append(jnp.broadcast_to(gc_all[:, LANE_DECAY + h:LANE_DECAY + h + 1], (c, DN_DK)))
            gct = jnp.broadcast_to(gct_all[LANE_DECAY + h:LANE_DECAY + h + 1, :], (c, c))
            lmask.append(jnp.where(tril, jnp.exp(jnp.where(tril, gcb[-1] - gct, 0.0)), 0.0))
            kb.append(kn[-1] * beta)
            vb.append(v * beta)

    n = range(len(chains))
    kk = [_mm_nt(kb[i], kn[i]) for i in n]
    tinv = _unit_lower_inverses([jnp.where(strict, kk[i] * lmask[i], 0.0) for i in n], ri, ci)
    egc = [jnp.exp(gcb[i]) for i in n]
    sol = [_mm(tinv[i], jnp.concatenate([vb[i], kb[i] * egc[i]], axis=1)) for i in n]
    aqk = [_mm_nt(qn[i], kn[i]) * lmask[i] for i in n]
    state = [st_scr[i] for i in n]
    glast = [gcb[i][c - 1:c, :] for i in n]
    v_new = [sol[i][:, :DN_DK] - _mm(sol[i][:, DN_DK:], state[i]) for i in n]
    o_state = [_mm(qn[i] * egc[i], state[i]) for i in n]
    o = [o_state[i] + _mm(aqk[i], v_new[i]) for i in n]
    upd = [_mm_tn(kn[i] * jnp.exp(glast[i] - gcb[i]), v_new[i]) for i in n]
    for i, (bi, h) in enumerate(chains):
        st_scr[i] = state[i] * jnp.exp(glast[i]) + upd[i]
        on = o[i] * lax.rsqrt(jnp.mean(o[i] * o[i], axis=-1, keepdims=True) + EPS) * ng
        o_ref[bi, :, h * DN_DK:(h + 1) * DN_DK] = (
            on * _silu(z_ref[bi, :, h * DN_DK:(h + 1) * DN_DK])).astype(o_ref.dtype)


def _dn_call(main3, conv_w, alog_row, dtb_row, norm_g):
    bsz, seq, _ = main3.shape
    c = DN_CHUNK
    nbt = DN_BATCH if bsz % DN_BATCH == 0 else 1
    halo_per_chunk = c // DN_HALO
    w3 = 3 * DN_W
    return pl.pallas_call(
        _dn_kernel,
        grid=(bsz // nbt, seq // c),
        in_specs=[pl.BlockSpec((nbt, c, w3), lambda b, s: (b, s, COL_QKV // w3)),
                  pl.BlockSpec((nbt, DN_HALO, w3),
                               lambda b, s: (b, jnp.maximum(s * halo_per_chunk - 1, 0), COL_QKV // w3)),
                  pl.BlockSpec((nbt, c, DN_W), lambda b, s: (b, s, COL_Z // DN_W)),
                  pl.BlockSpec((nbt, c, LANES), lambda b, s: (b, s, COL_GATE // LANES)),
                  pl.BlockSpec((SUBLANES, w3), lambda b, s: (0, 0)),
                  pl.BlockSpec((1, LANES), lambda b, s: (0, 0)),
                  pl.BlockSpec((1, LANES), lambda b, s: (0, 0)),
                  pl.BlockSpec((1, DN_DK), lambda b, s: (0, 0))],
        out_specs=pl.BlockSpec((nbt, c, DN_W), lambda b, s: (b, s, 0)),
        out_shape=jax.ShapeDtypeStruct((bsz, seq, DN_W), BF16),
        scratch_shapes=[pltpu.VMEM((nbt * DN_HEADS, DN_DK, DN_DK), F32)],
        compiler_params=_cparams(("parallel", "arbitrary")),
        name="dn",
    )(main3, main3, main3, main3, conv_w, alog_row, dtb_row, norm_g.reshape(1, DN_DK))


def _merge_kernel(ya_ref, yb_ref, yc_ref, mg_ref, x_ref, gt_ref, wa_ref, wb_ref, wc_ref, wo_ref, o_ref):
    d = x_ref.shape[1]
    gates = jax.nn.sigmoid(mg_ref[...].astype(F32))
    merged = (gates[:, :d] * jnp.dot(ya_ref[...], wa_ref[...], preferred_element_type=F32)
              + gates[:, d:2 * d] * jnp.dot(yb_ref[...], wb_ref[...], preferred_element_type=F32)
              + gates[:, 2 * d:] * jnp.dot(yc_ref[...], wc_ref[...], preferred_element_type=F32))
    y = jnp.dot(merged.astype(BF16), wo_ref[...], preferred_element_type=F32)
    o_ref[...] = x_ref[...] + gt_ref[0] * y


def _merge_call(ya, yb, yc, mg, x2, gt, wa, wb, wc, wo, seq):
    t, d = x2.shape
    tm = min(MERGE_TM, seq)
    per_b = seq // tm
    full = lambda a: pl.BlockSpec(a.shape, lambda i: (0, 0))
    return pl.pallas_call(
        _merge_kernel,
        grid=(t // tm,),
        in_specs=[pl.BlockSpec((tm, ya.shape[1]), lambda i: (i, 0)),
                  pl.BlockSpec((tm, yb.shape[1]), lambda i: (i, 0)),
                  pl.BlockSpec((tm, yc.shape[1]), lambda i: (i, 0)),
                  pl.BlockSpec((tm, mg.shape[1]), lambda i: (i, 0)),
                  pl.BlockSpec((tm, d), lambda i: (i, 0)),
                  pl.BlockSpec((1, 1, d), lambda i: (i // per_b, 0, 0)),
                  full(wa), full(wb), full(wc), full(wo)],
        out_specs=pl.BlockSpec((tm, d), lambda i: (i, 0)),
        out_shape=jax.ShapeDtypeStruct((t, d), F32),
        compiler_params=_cparams(("parallel",)),
        name="merge",
    )(ya, yb, yc, mg, x2, gt, wa, wb, wc, wo)


def _mlp_kernel(x_ref, g_ref, sc_ref, sh_ref, gt_ref, w1_ref, w2_ref, fg_ref, o_ref, acc_scr, *, final):
    j = pl.program_id(1)

    @pl.when((pl.program_id(0) == 0) & (j == 0))
    def _():
        acc_scr[...] = jnp.zeros(acc_scr.shape, F32)

    x = x_ref[...]
    h = _norm_mod(x, g_ref[...], sc_ref[0], sh_ref[0]).astype(BF16)
    a = jnp.maximum(jnp.dot(h, w1_ref[...], preferred_element_type=F32), 0.0)
    acc = acc_scr[...] + jnp.dot((a * a).astype(BF16), w2_ref[...], preferred_element_type=F32)
    acc_scr[...] = jnp.where(j == pl.num_programs(1) - 1, 0.0, acc)
    y = x + gt_ref[0] * acc
    if final:
        ms = jnp.mean(y * y, axis=-1, keepdims=True)
        y = y * lax.rsqrt(ms + EPS) * fg_ref[...]
    o_ref[...] = y


def _mlp_call(x2, g, sc, sh, gt, w1, w2, layer, fg, seq, final):
    t, d = x2.shape
    f = w1.shape[2]
    tm = min(MLP_TM, seq)
    tf = MLP_TF
    per_b = seq // tm
    mod = lambda: pl.BlockSpec((1, 1, d), lambda i, j: (i // per_b, 0, 0))
    return pl.pallas_call(
        functools.partial(_mlp_kernel, final=final),
        grid=(t // tm, f // tf),
        in_specs=[pl.BlockSpec((tm, d), lambda i, j: (i, 0)),
                  pl.BlockSpec((1, d), lambda i, j: (0, 0)),
                  mod(), mod(), mod(),
                  pl.BlockSpec((None, d, tf), lambda i, j: (layer, 0, j)),
                  pl.BlockSpec((None, tf, d), lambda i, j: (layer, j, 0)),
                  pl.BlockSpec((1, d), lambda i, j: (0, 0))],
        out_specs=pl.BlockSpec((tm, d), lambda i, j: (i, 0)),
        out_shape=jax.ShapeDtypeStruct((t, d), F32),
        scratch_shapes=[pltpu.VMEM((tm, d), F32)],
        compiler_params=_cparams(("arbitrary", "arbitrary")),
        name="mlp",
    )(x2, g.reshape(1, d), sc, sh, gt, w1, w2, fg.reshape(1, d))


def _split_w_in(w_in):
    offs = [0]
    for sz in IN_SIZES:
        offs.append(offs[-1] + sz)
    return [w_in[:, offs[i]:offs[i + 1]] for i in range(len(IN_SIZES))]


def _layout_w_in(w_in):
    nq, nkc, nvc, nks, nvs, nkw, nvw, ngate, pin, dqkv, dz, dbeta, da, mg = _split_w_in(w_in)
    d = w_in.shape[0]
    per_g = 3 * NSA_REP
    zeros = lambda n: jnp.zeros((d, n), w_in.dtype)
    gate0 = jnp.concatenate([ngate[:, :per_g], zeros(LANE_BETA - per_g), dbeta, da,
                             zeros(LANES - LANE_DECAY - DN_HEADS)], axis=1)
    gate1 = jnp.concatenate([ngate[:, per_g:], zeros(LANES - per_g)], axis=1)
    main = jnp.concatenate([nq, pin, dz, dqkv, gate0, gate1, nkc, nvc], axis=1)

    def per_group(w):
        pad = zeros(LANES - NSA_DH)
        return jnp.concatenate([w[:, :NSA_DH], pad, w[:, NSA_DH:], pad], axis=1)

    kv = jnp.concatenate([per_group(nks), per_group(nvs), per_group(nkw), per_group(nvw)], axis=1)
    return main.astype(BF16), mg.astype(BF16), kv.astype(BF16)


def _group_lane_w1(w1):
    two, _, hid = w1.shape
    halves = w1.reshape(two, 2, CMP_STRIDE, NSA_DH, hid)
    out = []
    for g in range(NSA_GROUPS):
        padded = jnp.zeros((two, 2, CMP_STRIDE, LANES, hid), w1.dtype).at[:, :, :, g * NSA_DH:(g + 1) * NSA_DH].set(halves)
        out.append(padded.reshape(two, 2, CMP_STRIDE * LANES, hid))
    return jnp.stack(out, axis=1)


def _nsa_constants(seq):
    nb = seq // SEL_LEN
    ncp = seq // CMP_STRIDE
    j = jnp.arange(nb)[:, None] * SEL_LEN
    i = jnp.arange(ncp)[None, :] * CMP_STRIDE
    overlap = jnp.clip(jnp.minimum(j + SEL_LEN, i + CMP_LEN) - jnp.maximum(j, i), 0) // CMP_STRIDE
    block_onehot = (jnp.arange(seq)[:, None] // SEL_LEN) == jnp.arange(nb)[None, :]
    return overlap.astype(BF16), block_onehot.astype(BF16)


def kernel(x, c, ada_w, ada_b, norm1_g, norm2_g, w_in, phi_k1, phi_k2, phi_v1, phi_v2, pos_k, pos_v, pool_w, pool_scale, dn_conv_w, dn_A_log, dn_dt_bias, dn_norm_g, w_branch_nsa, w_branch_pool, w_branch_dn, w_out, mlp_w1, mlp_w2, final_g):
    bsz, seq, d = x.shape
    assert d == D_MODEL and seq % max(PROJ_TM, MLP_TM, POOL_TS) == 0 and seq % KV_CHUNK == 0
    assert seq % (ATTN_Q_TILE * ATTN_TILES * ATTN_GROUPS) == 0 and (seq // Q_TILE) % SELECT_BUCKETS == 0
    assert DN_CHUNK == DN_DK == LANES
    t = bsz * seq
    depth = ada_w.shape[0]
    chunk = KV_CHUNK

    slopes = 2.0 ** (-(8.0 / NSA_HEADS) * (jnp.arange(NSA_HEADS, dtype=F32) + 1.0))
    overlap, block_onehot = _nsa_constants(seq)
    c_pad = jnp.zeros((SUBLANES, d), F32).at[:bsz].set(c)
    ada_b3 = ada_b.reshape(depth, 1, ada_b.shape[1])
    mlp_w1_bf, mlp_w2_bf = mlp_w1.astype(BF16), mlp_w2.astype(BF16)
    x2 = x.reshape(t, d)

    for l in range(depth):
        mod = _mod_call(c_pad, ada_w, ada_b3, l)[:bsz]
        sh1, sc1, gt1, sh2, sc2, gt2 = [m.reshape(bsz, 1, d) for m in jnp.split(mod, 6, axis=-1)]

        w_main, w_mg, w_kv = _layout_w_in(w_in[l])
        main = _proj_call(x2, norm1_g[l], sc1, sh1, w_main, seq, PROJ_MAIN_TM, F32, "proj_main")
        mg = _proj_call(x2, norm1_g[l], sc1, sh1, w_mg, seq, PROJ_TM, BF16, "proj_mg")
        kv = _proj_kv_call(x2, norm1_g[l], sc1, sh1, w_kv, seq)

        main3 = main.reshape(bsz, seq, N_MAIN)
        pos = jnp.zeros((2, SUBLANES, CMP_LEN * NSA_DH), F32).at[:, 0].set(
            jnp.stack([pos_k[l].reshape(-1), pos_v[l].reshape(-1)])).astype(BF16)
        w1 = jnp.stack([phi_k1[l], phi_v1[l]]).astype(BF16)
        w2 = jnp.stack([phi_k2[l], phi_v2[l]])
        w2 = jnp.concatenate([w2, w2], axis=-1).astype(BF16)
        cmp_kv = _compress_call(main3, pos, w1, _group_lane_w1(w1), w2)

        sel, act, yc = _select_call(slopes, main, cmp_kv[0], cmp_kv[1], overlap, bsz, seq)
        lst, cnt = _chunk_lists(act, seq, chunk)
        y_a = _attn_call(lst, cnt, slopes, main, sel, kv, yc, block_onehot, bsz, seq, chunk)

        y_b = _pool_call(main, pool_w[l].astype(BF16), pool_scale[l], bsz, seq)

        conv_w = jnp.zeros((SUBLANES, 3 * DN_W), F32).at[:DN_CONV].set(dn_conv_w[l])
        lane_row = lambda v: jnp.zeros((1, LANES), F32).at[0, LANE_DECAY:LANE_DECAY + DN_HEADS].set(v)
        y_c = _dn_call(main3, conv_w, lane_row(dn_A_log[l]), lane_row(dn_dt_bias[l]),
                       dn_norm_g[l]).reshape(t, DN_W)

        x2 = _merge_call(y_a, y_b, y_c, mg, x2, gt1,
                         w_branch_nsa[l].astype(BF16), w_branch_pool[l].astype(BF16),
                         w_branch_dn[l].astype(BF16), w_out[l].astype(BF16), seq)
        x2 = _mlp_call(x2, norm2_g[l], sc2, sh2, gt2, mlp_w1_bf, mlp_w2_bf, l,
                       final_g, seq, final=(l == depth - 1))
    return x2.reshape(bsz, seq, d)
```

```python
import functools

import jax
import jax.numpy as jnp
from jax import lax
from jax.experimental import pallas as pl
from jax.experimental.pallas import tpu as pltpu

F32 = jnp.float32
BF16 = jnp.bfloat16
HIGHEST = lax.Precision.HIGHEST

D_MODEL = 1024
NSA_HEADS = 8
NSA_GROUPS = 2
NSA_REP = NSA_HEADS // NSA_GROUPS
NSA_DH = 64
CMP_LEN = 32
CMP_STRIDE = 16
SEL_LEN = 64
SEL_SHIFT = 6
TOP_N = 16
WINDOW = 512
FORCE_BONUS = 1.0e4
POOL_WINDOWS = (2, 4, 8, 16)
POOL_GROUP_DIM = 128
POOL_W = len(POOL_WINDOWS) * POOL_GROUP_DIM
DN_HEADS = 4
DN_DK = 128
DN_W = DN_HEADS * DN_DK
DN_CONV = 4
EPS = 1e-6
NEG = -1e30
NSA_Q_W = NSA_HEADS * NSA_DH
NSA_KV_W = NSA_GROUPS * NSA_DH
IN_SIZES = (NSA_Q_W, NSA_KV_W, NSA_KV_W, NSA_KV_W, NSA_KV_W, NSA_KV_W, NSA_KV_W,
            3 * NSA_HEADS, POOL_W, 3 * DN_W, DN_W, DN_HEADS, DN_HEADS, 3 * D_MODEL)

LANES = 128
LANE_SHIFT = 7
SUBLANES = 8
VMEM_LIMIT_BYTES = 48 * 1024 * 1024

Q_TILE = 128
SELECT_BUCKETS = 4
ATTN_Q_TILE = 128
ATTN_TILES = 2
ATTN_GROUPS = 4
KV_CHUNK = 256
DN_CHUNK = 128
DN_BATCH = 4
PROJ_TM = 1024
MLP_TM = 1024
MLP_TF = 2048
MERGE_TM = 512
POOL_TS = 1024
POOL_HALO = 16
DN_HALO = 8

COL_Q = 0
COL_POOL = 512
COL_Z = 1024
COL_QKV = 1536
COL_GATE = 3072
COL_KC = 3328
COL_VC = 3456
N_MAIN = 3584
PROJ_MAIN_TM = 512
N_MG = 3072
LANE_BETA = 16
LANE_DECAY = 20
KVB_KS = 0
KVB_VS = 2
KVB_KW = 4
KVB_VW = 6
KVB_END = 8
N_KV = KVB_END * LANES
LANE_POS_HI = NSA_DH
LANE_POS_LO = NSA_DH + 1


def _cparams(sem):
    return pltpu.CompilerParams(dimension_semantics=sem, vmem_limit_bytes=VMEM_LIMIT_BYTES)


def _silu(v):
    return v * jax.nn.sigmoid(v)


def _mm(a, b):
    return jnp.dot(a.astype(BF16), b.astype(BF16), preferred_element_type=F32)


def _mm_nt(a, b):
    return lax.dot_general(a.astype(BF16), b.astype(BF16), (((1,), (1,)), ((), ())), preferred_element_type=F32)


def _mm_tn(a, b):
    return lax.dot_general(a.astype(BF16), b.astype(BF16), (((0,), (0,)), ((), ())), preferred_element_type=F32)


def _mod_kernel(c_ref, w_ref, b_ref, o_ref):
    cond = _silu(c_ref[...])
    o_ref[...] = jnp.dot(cond, w_ref[...], preferred_element_type=F32, precision=HIGHEST) + b_ref[...]


def _mod_call(c_pad, w, b, layer):
    _, d, n = w.shape
    tn = 1536
    return pl.pallas_call(
        _mod_kernel,
        grid=(n // tn,),
        in_specs=[pl.BlockSpec((SUBLANES, d), lambda j: (0, 0)),
                  pl.BlockSpec((None, d, tn), lambda j: (layer, 0, j)),
                  pl.BlockSpec((None, 1, tn), lambda j: (layer, 0, j))],
        out_specs=pl.BlockSpec((SUBLANES, tn), lambda j: (0, j)),
        out_shape=jax.ShapeDtypeStruct((SUBLANES, n), F32),
        compiler_params=_cparams(("parallel",)),
        name="mod",
    )(c_pad, w, b)


def _norm_mod(x, g, sc, sh):
    ms = jnp.mean(x * x, axis=-1, keepdims=True)
    return (x * lax.rsqrt(ms + EPS) * g) * (1.0 + sc) + sh


def _proj_kernel(x_ref, g_ref, sc_ref, sh_ref, w_ref, o_ref):
    h = _norm_mod(x_ref[...], g_ref[...], sc_ref[0], sh_ref[0]).astype(BF16)
    o_ref[...] = jnp.dot(h, w_ref[...], preferred_element_type=F32).astype(o_ref.dtype)


def _proj_kv_kernel(x_ref, g_ref, sc_ref, sh_ref, w_ref, o_ref, *, tiles_per_seq):
    h = _norm_mod(x_ref[...], g_ref[...], sc_ref[0], sh_ref[0]).astype(BF16)
    full = jnp.dot(h, w_ref[...], preferred_element_type=F32)
    o_ref[:, :N_MG] = full[:, :N_MG].astype(o_ref.dtype)
    acc = full[:, N_MG:]
    tm, n = acc.shape
    col = lax.broadcasted_iota(jnp.int32, (1, n), 1)
    blk = jnp.right_shift(col, LANE_SHIFT)
    lane = col & (LANES - 1)
    is_key = ((blk >= KVB_KS) & (blk < KVB_VS)) | ((blk >= KVB_KW) & (blk < KVB_VW))
    is_val = ((blk >= KVB_VS) & (blk < KVB_KW)) | ((blk >= KVB_VW) & (blk < KVB_END))
    ones_row = jnp.where(is_val & (lane >= NSA_DH), 1.0, 0.0)
    hi_row = jnp.where(is_key & (lane == LANE_POS_HI), 1.0, 0.0)
    lo_row = jnp.where(is_key & (lane == LANE_POS_LO), 1.0, 0.0)
    pos = (pl.program_id(0) % tiles_per_seq) * tm + lax.broadcasted_iota(jnp.int32, (tm, 1), 0)
    pos_hi = jnp.right_shift(pos, LANE_SHIFT).astype(F32)
    pos_lo = (pos & (LANES - 1)).astype(F32)
    o_ref[:, N_MG:] = (acc + ones_row + pos_hi * hi_row + pos_lo * lo_row).astype(o_ref.dtype)


def _proj_specs(tm, d, tn, per_b):
    return [pl.BlockSpec((tm, d), lambda i, j: (i, 0)),
            pl.BlockSpec((1, d), lambda i, j: (0, 0)),
            pl.BlockSpec((1, 1, d), lambda i, j: (i // per_b, 0, 0)),
            pl.BlockSpec((1, 1, d), lambda i, j: (i // per_b, 0, 0)),
            pl.BlockSpec((d, tn), lambda i, j: (0, j))]


def _proj_call(x2, g, sc, sh, w, seq, tm, out_dtype, name):
    t, d = x2.shape
    n = w.shape[1]
    tm = min(tm, seq)
    return pl.pallas_call(
        _proj_kernel,
        grid=(t // tm, 1),
        in_specs=_proj_specs(tm, d, n, seq // tm),
        out_specs=pl.BlockSpec((tm, n), lambda i, j: (i, j)),
        out_shape=jax.ShapeDtypeStruct((t, n), out_dtype),
        compiler_params=_cparams(("parallel", "arbitrary")),
        name=name,
    )(x2, g.reshape(1, d), sc, sh, w)


def _proj_kv_call(x2, g, sc, sh, w, seq):
    t, d = x2.shape
    n = w.shape[1]
    tm = min(PROJ_TM, seq)
    return pl.pallas_call(
        functools.partial(_proj_kv_kernel, tiles_per_seq=seq // tm),
        grid=(t // tm, 1),
        in_specs=_proj_specs(tm, d, n, seq // tm),
        out_specs=pl.BlockSpec((tm, n), lambda i, j: (i, j)),
        out_shape=jax.ShapeDtypeStruct((t, n), BF16),
        compiler_params=_cparams(("parallel", "arbitrary")),
        name="proj_kv",
    )(x2, g.reshape(1, d), sc, sh, w)


def _compress_kernel(x_ref, pos_ref, w1_ref, w1g_ref, w2_ref, o_ref):
    nr = x_ref.shape[1] // CMP_STRIDE
    xs = jnp.concatenate([x_ref[0, pl.ds(rho, nr, stride=CMP_STRIDE), :].astype(BF16) for rho in range(CMP_STRIDE)],
                         axis=1)
    posb = jnp.dot(pos_ref[0], w1_ref[0], preferred_element_type=F32)[0:1]
    for g in range(NSA_GROUPS):
        top = jnp.dot(xs, w1g_ref[0, g, 0], preferred_element_type=F32)
        bot = jnp.dot(xs, w1g_ref[0, g, 1], preferred_element_type=F32)
        hid = top + pltpu.roll(bot, nr - 1, 0) + posb
        o_ref[0, 0, g] = _mm(_silu(hid), w2_ref[0]).astype(o_ref.dtype)


def _compress_call(main3, pos, w1, w1g, w2):
    b, seq, _ = main3.shape
    nr = seq // CMP_STRIDE
    two, l64, hid = w1.shape
    return pl.pallas_call(
        _compress_kernel,
        grid=(two, b),
        in_specs=[pl.BlockSpec((1, seq, LANES), lambda a, i: (i, 0, COL_KC // LANES + a)),
                  pl.BlockSpec((1, SUBLANES, l64), lambda a, i: (a, 0, 0)),
                  pl.BlockSpec((1, l64, hid), lambda a, i: (a, 0, 0)),
                  pl.BlockSpec((1,) + w1g.shape[1:], lambda a, i: (a, 0, 0, 0, 0)),
                  pl.BlockSpec((1, hid, LANES), lambda a, i: (a, 0, 0))],
        out_specs=pl.BlockSpec((1, 1, NSA_GROUPS, nr, LANES), lambda a, i: (a, i, 0, 0, 0)),
        out_shape=jax.ShapeDtypeStruct((two, b, NSA_GROUPS, nr, LANES), BF16),
        compiler_params=_cparams(("parallel", "parallel")),
        name="compress",
    )(main3, pos, w1, w1g, w2)


def _gate_cols(gate_logits, branch):
    sg = jax.nn.sigmoid(gate_logits)
    return [sg[:, 3 * r + branch:3 * r + branch + 1] for r in range(NSA_REP)]


def _stack_heads_dup(q):
    qa = q[:, :LANES]
    qb = q[:, LANES:]
    lo = lax.broadcasted_iota(jnp.int32, qa.shape, 1) < NSA_DH
    z = jnp.zeros_like(qa)
    return jnp.concatenate([jnp.where(lo, qa, z), jnp.where(lo, z, qa),
                            jnp.where(lo, qb, z), jnp.where(lo, z, qb)], axis=0)


def _unstack_heads_dup(o, gates):
    nq = o.shape[0] // NSA_REP
    lo = lax.broadcasted_iota(jnp.int32, (nq, LANES), 1) < NSA_DH
    a = jnp.where(lo, gates[0] * o[0:nq], gates[1] * o[nq:2 * nq])
    b = jnp.where(lo, gates[2] * o[2 * nq:3 * nq], gates[3] * o[3 * nq:])
    return jnp.concatenate([a, b], axis=1)


def _stack_heads_pos(q, slopes):
    qa = q[:, :LANES]
    qb = q[:, LANES:]
    lane = lax.broadcasted_iota(jnp.int32, qa.shape, 1)
    lo = lane < NSA_DH
    heads = [qa, pltpu.roll(qa, NSA_DH, 1), qb, pltpu.roll(qb, NSA_DH, 1)]
    out = []
    for r in range(NSA_REP):
        extra = jnp.where(lane == LANE_POS_HI, slopes[r] * LANES, jnp.where(lane == LANE_POS_LO, slopes[r], 0.0))
        out.append(jnp.where(lo, heads[r], extra))
    return jnp.concatenate(out, axis=0)


def _finish_heads(acc, gates):
    nq = acc.shape[0] // NSA_REP
    o = acc * pltpu.roll(1.0 / acc, NSA_DH, 1)
    lo = lax.broadcasted_iota(jnp.int32, (nq, LANES), 1) < NSA_DH
    hs = [gates[r] * o[r * nq:(r + 1) * nq] for r in range(NSA_REP)]
    a = jnp.where(lo, hs[0], pltpu.roll(hs[1], NSA_DH, 1))
    b = jnp.where(lo, hs[2], pltpu.roll(hs[3], NSA_DH, 1))
    return jnp.concatenate([a, b], axis=1)


def _select_kernel(slopes_ref, q_ref, kc_ref, vc_ref, gates_ref, ov_ref, sel_ref, act_ref, yc_ref):
    qt = pl.program_id(1)
    nqt = pl.num_programs(1)
    nb_all, ncp_all = ov_ref.shape
    for k in range(SELECT_BUCKETS):
        @pl.when((qt * SELECT_BUCKETS) // nqt == k)
        def _(k=k):
            _select_prefix(slopes_ref, q_ref, kc_ref, vc_ref, gates_ref, ov_ref, sel_ref, act_ref, yc_ref,
                           nb=(k + 1) * nb_all // SELECT_BUCKETS, ncp=(k + 1) * ncp_all // SELECT_BUCKETS)


def _select_prefix(slopes_ref, q_ref, kc_ref, vc_ref, gates_ref, ov_ref, sel_ref, act_ref, yc_ref, *, nb, ncp):
    t0 = pl.program_id(1) * Q_TILE
    nq = Q_TILE
    nb_all = ov_ref.shape[0]
    c_end = lax.broadcasted_iota(jnp.int32, (nq, ncp), 1) * CMP_STRIDE + (CMP_LEN - 1)
    tq = t0 + lax.broadcasted_iota(jnp.int32, (nq, ncp), 0)
    vis = c_end <= tq
    any_vis = tq[:, 0:1] >= CMP_LEN - 1
    rel = (c_end[0:1] - t0).astype(F32)
    ov = ov_ref[:nb, :ncp]
    jb = lax.broadcasted_iota(jnp.int32, (nb, nq), 0)
    tql = t0 + lax.broadcasted_iota(jnp.int32, (nb, nq), 1)
    valid = jb * SEL_LEN <= tql
    forced = (jb == 0) | (jb == jnp.right_shift(tql, SEL_SHIFT))
    scores = []
    for g in range(NSA_GROUPS):
        qcols = slice(g * NSA_REP * NSA_DH, (g + 1) * NSA_REP * NSA_DH)
        qs = _stack_heads_dup(q_ref[:, qcols] * (NSA_DH ** -0.5)).astype(BF16)
        s = _mm_nt(qs, kc_ref[0, g, :ncp, :])
        psum = jnp.zeros((nq, ncp), F32)
        ps = []
        for r in range(NSA_REP):
            m_r = slopes_ref[g * NSA_REP + r]
            sr = jnp.where(vis, s[r * nq:(r + 1) * nq] + m_r * rel, NEG)
            mx = jnp.max(sr, axis=-1, keepdims=True)
            e = jnp.exp(sr - mx)
            den = jnp.sum(e, axis=-1, keepdims=True)
            p = e * jnp.where(any_vis, 1.0 / den, 0.0)
            psum = psum + p
            ps.append(p.astype(BF16))
        o_c = jnp.dot(jnp.concatenate(ps, axis=0), vc_ref[0, g, :ncp, :], preferred_element_type=F32)
        yc_ref[:, qcols] = _unstack_heads_dup(o_c, _gate_cols(gates_ref[:, g * LANES:(g + 1) * LANES], 0))
        p_hi = psum.astype(BF16)
        p_lo = (psum - p_hi.astype(F32)).astype(BF16)
        imp_t = _mm_nt(ov, p_hi) + _mm_nt(ov, p_lo)
        scores.append(jnp.where(valid, imp_t + FORCE_BONUS * forced.astype(F32), NEG))

    for _ in range(min(TOP_N, nb)):
        for g in range(NSA_GROUPS):
            mx = jnp.max(scores[g], axis=0, keepdims=True)
            idx = jnp.min(jnp.where(scores[g] == mx, jb, nb), axis=0, keepdims=True)
            scores[g] = jnp.where(jb == idx, -jnp.inf, scores[g])
    for g in range(NSA_GROUPS):
        sel_t = jnp.where((scores[g] == -jnp.inf) & valid, 1.0, 0.0)
        if nb < nb_all:
            sel_t = jnp.concatenate([sel_t, jnp.zeros((nb_all - nb, nq), F32)], axis=0)
        sel_q = sel_t.T
        sel_ref[0, g] = sel_q.astype(sel_ref.dtype)
        act_ref[0, g, 0] = jnp.max(sel_q, axis=0, keepdims=True)


def _select_call(slopes, main, kcmp, vcmp, ov, bsz, seq):
    nqt = seq // Q_TILE
    nb, ncp = ov.shape
    row = lambda b, q: b * nqt + q
    gw = NSA_GROUPS * LANES
    return pl.pallas_call(
        _select_kernel,
        grid=(bsz, nqt),
        in_specs=[pl.BlockSpec(memory_space=pltpu.SMEM),
                  pl.BlockSpec((Q_TILE, NSA_Q_W), lambda b, q: (row(b, q), COL_Q // NSA_Q_W)),
                  pl.BlockSpec((1, NSA_GROUPS, ncp, LANES), lambda b, q: (b, 0, 0, 0)),
                  pl.BlockSpec((1, NSA_GROUPS, ncp, LANES), lambda b, q: (b, 0, 0, 0)),
                  pl.BlockSpec((Q_TILE, gw), lambda b, q: (row(b, q), COL_GATE // gw)),
                  pl.BlockSpec((nb, ncp), lambda b, q: (0, 0))],
        out_specs=[pl.BlockSpec((1, NSA_GROUPS, Q_TILE, nb), lambda b, q: (b, 0, q, 0)),
                   pl.BlockSpec((1, NSA_GROUPS, 1, 1, nb), lambda b, q: (b, 0, q, 0, 0)),
                   pl.BlockSpec((Q_TILE, NSA_Q_W), lambda b, q: (row(b, q), 0))],
        out_shape=[jax.ShapeDtypeStruct((bsz, NSA_GROUPS, seq, nb), BF16),
                   jax.ShapeDtypeStruct((bsz, NSA_GROUPS, nqt, 1, nb), F32),
                   jax.ShapeDtypeStruct((bsz * seq, NSA_Q_W), F32)],
        compiler_params=_cparams(("parallel", "parallel")),
        name="select",
    )(slopes, main, kcmp, vcmp, main, ov)


def _attn_kernel(lst_ref, cnt_ref, slopes_ref, q_ref, sel_ref, ks_ref, vs_ref, kw_ref, vw_ref, gates_ref, yc_ref,
                 et_ref, o_ref, s_scr, p_scr, m_scr, a_scr, acc_scr, *, chunk):
    b = pl.program_id(0)
    g = pl.program_id(1)
    step = pl.program_id(2)
    tiles_per_step = ATTN_TILES * ATTN_GROUPS
    nqt = pl.num_programs(2) * tiles_per_step
    nq = ATTN_Q_TILE
    nw = WINDOW + nq
    nch_total = ks_ref.shape[0] // chunk
    slopes = [slopes_ref[g * NSA_REP + r] for r in range(NSA_REP)]

    def values(p, c):
        start = pl.multiple_of(c * chunk, chunk)
        return jnp.dot(p, vs_ref[pl.ds(start, chunk), :], preferred_element_type=F32)

    def row_max(s):
        return jnp.broadcast_to(jnp.max(s, axis=-1, keepdims=True), (s.shape[0], LANES))

    def lane_tile(m, width):
        return jnp.concatenate([m] * (width // LANES), axis=1)

    def masked(s, ok):
        return jnp.concatenate([jnp.where(ok, s[r * nq:(r + 1) * nq], NEG) for r in range(NSA_REP)], axis=0)

    class Group:
        pass

    def setup(k):
        gr = Group()
        gr.tiles = range(ATTN_TILES)
        gr.scr = [(k % 2) * ATTN_TILES + u for u in gr.tiles]
        local = [k * ATTN_TILES + u for u in gr.tiles]
        gr.rws = [slice(v * nq, (v + 1) * nq) for v in local]
        qts = [step * tiles_per_step + v for v in local]
        gr.t0s = [qt * nq for qt in qts]
        gr.qs = [_stack_heads_pos(q_ref[rw, :] * (NSA_DH ** -0.5), slopes).astype(BF16) for rw in gr.rws]
        pens = [((sel_ref[0, 0, rw, :].astype(F32) - 1.0) * (-NEG)).astype(BF16) for rw in gr.rws]
        gr.qx = [jnp.concatenate([gr.qs[u], jnp.concatenate([pens[u]] * NSA_REP, axis=0)], axis=1) for u in gr.tiles]
        gr.bases = [((b * NSA_GROUPS + g) * nqt + qt) * nch_total for qt in qts]
        gr.cnts = [cnt_ref[(b * NSA_GROUPS + g) * nqt + qt] for qt in qts]
        for u in gr.tiles:
            su = gr.scr[u]
            m_scr[su] = jnp.full(m_scr.shape[1:], NEG, F32)
            a_scr[su] = jnp.ones(a_scr.shape[1:], F32)
            acc_scr[su] = jnp.zeros(acc_scr.shape[1:], F32)
            p_scr[su, 1] = jnp.zeros(p_scr.shape[2:], BF16)
        for u in gr.tiles:
            s_scr[gr.scr[u], 0] = logits(gr, u, lst_ref[gr.bases[u]])
        return gr

    def logits(gr, u, c):
        start = pl.multiple_of(c * chunk, chunk)
        kx = jnp.concatenate([ks_ref[pl.ds(start, chunk), :], et_ref[pl.ds(start, chunk), :]], axis=1)
        return _mm_nt(gr.qx[u], kx)

    def flash_step(gr, us, i, slot):
        for u in us:
            su = gr.scr[u]
            acc_scr[su] = a_scr[su] * acc_scr[su] + values(p_scr[su, 1 - slot],
                                                           lst_ref[gr.bases[u] + jnp.maximum(i - 1, 0)])
        for u in us:
            su = gr.scr[u]
            s = s_scr[su, slot]
            m_prev = m_scr[su]
            m_new = jnp.maximum(m_prev, row_max(s))
            a_scr[su] = jnp.exp(m_prev - m_new)
            p_scr[su, slot] = jnp.exp(s - lane_tile(m_new, chunk)).astype(BF16)
            m_scr[su] = m_new
        for u in us:
            s_scr[gr.scr[u], 1 - slot] = logits(gr, u, lst_ref[gr.bases[u] + i + 1])

    def flash_loops(gr):
        def single_steps(us):
            def body(i, carry):
                flash_step(gr, us, i, i & 1)
                return carry
            return body

        def double_steps(us):
            def body(j, carry):
                flash_step(gr, us, 2 * j, 0)
                flash_step(gr, us, 2 * j + 1, 1)
                return carry
            return body

        joint = functools.reduce(jnp.minimum, gr.cnts)
        lax.fori_loop(0, joint // 2, double_steps(list(gr.tiles)), 0)
        lax.fori_loop(joint - (joint & 1), joint, single_steps(list(gr.tiles)), 0)
        for u in gr.tiles:
            lax.fori_loop(joint, gr.cnts[u], single_steps([u]), 0)

    def finish(gr):
        tiles = gr.tiles
        scr = gr.scr
        slots = [gr.cnts[u] & 1 for u in tiles]
        c_diag = [gr.t0s[u] // chunk for u in tiles]
        acc = [a_scr[scr[u]] * acc_scr[scr[u]]
               + values(p_scr[scr[u], 1 - slots[u]], lst_ref[gr.bases[u] + jnp.maximum(gr.cnts[u] - 1, 0)])
               for u in tiles]
        wbase = [pl.multiple_of(jnp.maximum(gr.t0s[u] - WINDOW, 0), nq) for u in tiles]
        s_win = [_mm_nt(gr.qs[u], kw_ref[pl.ds(wbase[u], nw), :]) for u in tiles]
        s_diag = []
        for u in tiles:
            pos = c_diag[u] * chunk + lax.broadcasted_iota(jnp.int32, (nq, chunk), 1)
            tq = gr.t0s[u] + lax.broadcasted_iota(jnp.int32, (nq, chunk), 0)
            s_diag.append(masked(s_scr[scr[u], slots[u]], pos <= tq))
        m_prev = [m_scr[scr[u]] for u in tiles]
        m_new = [jnp.maximum(m_prev[u], row_max(s_diag[u])) for u in tiles]
        acc = [jnp.exp(m_prev[u] - m_new[u]) * acc[u]
               + values(jnp.exp(s_diag[u] - lane_tile(m_new[u], chunk)).astype(BF16), c_diag[u]) for u in tiles]
        p_win = []
        for u in tiles:
            pos = wbase[u] + lax.broadcasted_iota(jnp.int32, (nq, nw), 1)
            tq = gr.t0s[u] + lax.broadcasted_iota(jnp.int32, (nq, nw), 0)
            s = masked(s_win[u], (pos <= tq) & (pos > tq - WINDOW))
            p_win.append(jnp.exp(s - jnp.max(s, axis=-1, keepdims=True)).astype(BF16))
        acc_w = [jnp.dot(p_win[u], vw_ref[pl.ds(wbase[u], nw), :], preferred_element_type=F32) for u in tiles]
        for u in tiles:
            gates = gates_ref[gr.rws[u], :]
            o_ref[gr.rws[u], :] = (yc_ref[gr.rws[u], :] + _finish_heads(acc[u], _gate_cols(gates, 1))
                                   + _finish_heads(acc_w[u], _gate_cols(gates, 2))).astype(o_ref.dtype)

    cur = setup(0)
    for k in range(ATTN_GROUPS):
        flash_loops(cur)
        nxt = setup(k + 1) if k + 1 < ATTN_GROUPS else None
        finish(cur)
        cur = nxt


def _chunk_lists(act, seq, chunk):
    bsz, ngrp = act.shape[:2]
    nqt = seq // ATTN_Q_TILE
    nch = seq // chunk
    active = act.reshape(bsz, ngrp, nqt, ATTN_Q_TILE // Q_TILE, nch, chunk // SEL_LEN).max(axis=(3, 5)) > 0.0
    c_diag = (jnp.arange(nqt) * ATTN_Q_TILE) // chunk
    active = active & (jnp.arange(nch)[None, :] < c_diag[:, None])
    cnt = active.sum(axis=-1).astype(jnp.int32)
    order = jnp.argsort(jnp.logical_not(active), axis=-1, stable=True).astype(jnp.int32)
    lst = jnp.where(jnp.arange(nch) < cnt[..., None], order, c_diag[:, None].astype(jnp.int32))
    return lst.reshape(-1), cnt.reshape(-1)


def _attn_call(lst, cnt, slopes, main, sel, kv, yc, et, bsz, seq, chunk):
    tq = ATTN_Q_TILE * ATTN_TILES * ATTN_GROUPS
    nset = 2 * ATTN_TILES
    nqt = seq // tq
    nb = sel.shape[-1]
    rows = NSA_REP * ATTN_Q_TILE
    row = lambda b, g, q: b * nqt + q
    kvspec = lambda blk: pl.BlockSpec((seq, LANES), lambda b, g, q, *_: (b, N_MG // LANES + blk + g))
    grid_spec = pltpu.PrefetchScalarGridSpec(
        num_scalar_prefetch=2,
        grid=(bsz, NSA_GROUPS, nqt),
        in_specs=[pl.BlockSpec(memory_space=pltpu.SMEM),
                  pl.BlockSpec((tq, 256), lambda b, g, q, *_: (row(b, g, q), COL_Q // 256 + g)),
                  pl.BlockSpec((1, 1, tq, nb), lambda b, g, q, *_: (b, g, q, 0)),
                  kvspec(KVB_KS), kvspec(KVB_VS), kvspec(KVB_KW), kvspec(KVB_VW),
                  pl.BlockSpec((tq, LANES), lambda b, g, q, *_: (row(b, g, q), COL_GATE // LANES + g)),
                  pl.BlockSpec((tq, 256), lambda b, g, q, *_: (row(b, g, q), g)),
                  pl.BlockSpec(et.shape, lambda b, g, q, *_: (0, 0))],
        out_specs=pl.BlockSpec((tq, 256), lambda b, g, q, *_: (row(b, g, q), g)),
        scratch_shapes=[pltpu.VMEM((nset, 2, rows, chunk), F32),
                        pltpu.VMEM((nset, 2, rows, chunk), BF16),
                        pltpu.VMEM((nset, rows, LANES), F32),
                        pltpu.VMEM((nset, rows, LANES), F32),
                        pltpu.VMEM((nset, rows, LANES), F32)],
    )
    return pl.pallas_call(
        functools.partial(_attn_kernel, chunk=chunk),
        grid_spec=grid_spec,
        out_shape=jax.ShapeDtypeStruct((bsz * seq, NSA_Q_W), BF16),
        compiler_params=_cparams(("parallel", "parallel", "parallel")),
        name="attn",
    )(lst, cnt, slopes, main, sel, kv, kv, kv, kv, main, yc, et)


def _pool_kernel(x_ref, halo_ref, w_ref, sc_ref, o_ref):
    s = pl.program_id(1)
    ts = x_ref.shape[0]
    x = x_ref[...]
    halo = jnp.where(s > 0, halo_ref[...], 0.0)
    t = s * ts + lax.broadcasted_iota(jnp.int32, (ts, 1), 0)
    for gi, win in enumerate(POOL_WINDOWS):
        lanes = slice(gi * POOL_GROUP_DIM, (gi + 1) * POOL_GROUP_DIM)
        ext = jnp.concatenate([halo[:, lanes], x[:, lanes]], axis=0)
        acc = ext
        span = 1
        while span < win:
            acc = acc + pltpu.roll(acc, span, 0)
            span *= 2
        cnt = jnp.minimum(t + 1, win).astype(F32)
        y = acc[POOL_HALO:] / cnt - x[:, lanes]
        o_ref[:, lanes] = (_mm(y, w_ref[gi]) * sc_ref[:, lanes]).astype(o_ref.dtype)


def _pool_call(main, w, scale, bsz, seq):
    ts = min(POOL_TS, seq)
    per_b = seq // ts
    halo_per_tile = ts // POOL_HALO
    return pl.pallas_call(
        _pool_kernel,
        grid=(bsz, per_b),
        in_specs=[pl.BlockSpec((ts, POOL_W), lambda b, s: (b * per_b + s, COL_POOL // POOL_W)),
                  pl.BlockSpec((POOL_HALO, POOL_W),
                               lambda b, s: (jnp.maximum((b * per_b + s) * halo_per_tile - 1, 0), COL_POOL // POOL_W)),
                  pl.BlockSpec(w.shape, lambda b, s: (0, 0, 0)),
                  pl.BlockSpec((1, POOL_W), lambda b, s: (0, 0))],
        out_specs=pl.BlockSpec((ts, POOL_W), lambda b, s: (b * per_b + s, 0)),
        out_shape=jax.ShapeDtypeStruct((bsz * seq, POOL_W), BF16),
        compiler_params=_cparams(("parallel", "parallel")),
        name="pool",
    )(main, main, w, scale.reshape(1, POOL_W))


def _unit_lower_inverses(nms, ri, ci):
    c = nms[0].shape[0]
    eye = jnp.where(ri == ci, 1.0, 0.0)
    in8 = jnp.right_shift(ri, 3) == jnp.right_shift(ci, 3)
    n8 = [jnp.where(in8, nm, 0.0) for nm in nms]
    ps = [eye - a for a in n8]
    ms = [_mm(a, a) for a in n8]
    ps = [p + _mm(p, m) for p, m in zip(ps, ms)]
    ms = [_mm(m, m) for m in ms]
    ps = [p + _mm(p, m) for p, m in zip(ps, ms)]
    shift = 3
    while (1 << shift) < c:
        rb = jnp.right_shift(ri, shift)
        cb = jnp.right_shift(ci, shift)
        lower_left = ((rb & 1) == 1) & (cb == rb - 1)
        ts = [_mm(p, jnp.where(lower_left, nm, 0.0)) for p, nm in zip(ps, nms)]
        ps = [p - _mm(t, p) for p, t in zip(ps, ts)]
        shift += 1
    return ps


def _row_cumsum(x, ri):
    span = 1
    while span < x.shape[0]:
        x = x + jnp.where(ri >= span, pltpu.roll(x, span, 0), 0.0)
        span *= 2
    return x


def _dn_kernel(qkv_ref, halo_ref, z_ref, gba_ref, cw_ref, alog_ref, dtb_ref, ng_ref, o_ref, st_scr):
    s = pl.program_id(1)
    nbatch, c = qkv_ref.shape[0], qkv_ref.shape[1]

    @pl.when(s == 0)
    def _():
        st_scr[...] = jnp.zeros(st_scr.shape, F32)

    ri = lax.broadcasted_iota(jnp.int32, (c, c), 0)
    ci = lax.broadcasted_iota(jnp.int32, (c, c), 1)
    tril = ri >= ci
    strict = ri > ci
    cw = cw_ref[...]
    ng = ng_ref[...]
    chains = [(bi, h) for bi in range(nbatch) for h in range(DN_HEADS)]
    qn, kn, kb, vb, gcb, lmask = [], [], [], [], [], []
    for bi in range(nbatch):
        halo = jnp.where(s > 0, halo_ref[bi], 0.0)
        ext = jnp.concatenate([halo, qkv_ref[bi]], axis=0)
        conv = cw[DN_CONV - 1:DN_CONV] * ext
        for tap in range(1, DN_CONV):
            conv = conv + cw[DN_CONV - 1 - tap:DN_CONV - tap] * pltpu.roll(ext, tap, 0)
        act = _silu(conv[DN_HALO:])

        gba = gba_ref[bi]
        beta_all = jax.nn.sigmoid(gba)
        xs = gba + dtb_ref[...]
        softplus = jnp.maximum(xs, 0.0) + jnp.log(1.0 + jnp.exp(-jnp.abs(xs)))
        gc_all = _row_cumsum(-jnp.exp(alog_ref[...]) * softplus, ri)
        gct_all = gc_all.T
        for h in range(DN_HEADS):
            q = act[:, h * DN_DK:(h + 1) * DN_DK]
            k = act[:, DN_W + h * DN_DK:DN_W + (h + 1) * DN_DK]
            v = act[:, 2 * DN_W + h * DN_DK:2 * DN_W + (h + 1) * DN_DK]
            qn.append(q * lax.rsqrt(jnp.sum(q * q, axis=-1, keepdims=True) + EPS) * (DN_DK ** -0.5))
            kn.append(k * lax.rsqrt(jnp.sum(k * k, axis=-1, keepdims=True) + EPS))
            beta = beta_all[:, LANE_BETA + h:LANE_BETA + h + 1]
            gcb.append(jnp.broadcast_to(gc_all[:, LANE_DECAY + h:LANE_DECAY + h + 1], (c, DN_DK)))
            gct = jnp.broadcast_to(gct_all[LANE_DECAY + h:LANE_DECAY + h + 1, :], (c, c))
            lmask.append(jnp.where(tril, jnp.exp(jnp.where(tril, gcb[-1] - gct, 0.0)), 0.0))
            kb.append(kn[-1] * beta)
            vb.append(v * beta)

    n = range(len(chains))
    kk = [_mm_nt(kb[i], kn[i]) for i in n]
    tinv = _unit_lower_inverses([jnp.where(strict, kk[i] * lmask[i], 0.0) for i in n], ri, ci)
    egc = [jnp.exp(gcb[i]) for i in n]
    sol = [_mm(tinv[i], jnp.concatenate([vb[i], kb[i] * egc[i]], axis=1)) for i in n]
    aqk = [_mm_nt(qn[i], kn[i]) * lmask[i] for i in n]
    state = [st_scr[i] for i in n]
    glast = [gcb[i][c - 1:c, :] for i in n]
    v_new = [sol[i][:, :DN_DK] - _mm(sol[i][:, DN_DK:], state[i]) for i in n]
    o_state = [_mm(qn[i] * egc[i], state[i]) for i in n]
    o = [o_state[i] + _mm(aqk[i], v_new[i]) for i in n]
    upd = [_mm_tn(kn[i] * jnp.exp(glast[i] - gcb[i]), v_new[i]) for i in n]
    for i, (bi, h) in enumerate(chains):
        st_scr[i] = state[i] * jnp.exp(glast[i]) + upd[i]
        on = o[i] * lax.rsqrt(jnp.mean(o[i] * o[i], axis=-1, keepdims=True) + EPS) * ng
        o_ref[bi, :, h * DN_DK:(h + 1) * DN_DK] = (
            on * _silu(z_ref[bi, :, h * DN_DK:(h + 1) * DN_DK])).astype(o_ref.dtype)


def _dn_call(main3, conv_w, alog_row, dtb_row, norm_g):
    bsz, seq, _ = main3.shape
    c = DN_CHUNK
    nbt = DN_BATCH if bsz % DN_BATCH == 0 else 1
    halo_per_chunk = c // DN_HALO
    w3 = 3 * DN_W
    return pl.pallas_call(
        _dn_kernel,
        grid=(bsz // nbt, seq // c),
        in_specs=[pl.BlockSpec((nbt, c, w3), lambda b, s: (b, s, COL_QKV // w3)),
                  pl.BlockSpec((nbt, DN_HALO, w3),
                               lambda b, s: (b, jnp.maximum(s * halo_per_chunk - 1, 0), COL_QKV // w3)),
                  pl.BlockSpec((nbt, c, DN_W), lambda b, s: (b, s, COL_Z // DN_W)),
                  pl.BlockSpec((nbt, c, LANES), lambda b, s: (b, s, COL_GATE // LANES)),
                  pl.BlockSpec((SUBLANES, w3), lambda b, s: (0, 0)),
                  pl.BlockSpec((1, LANES), lambda b, s: (0, 0)),
                  pl.BlockSpec((1, LANES), lambda b, s: (0, 0)),
                  pl.BlockSpec((1, DN_DK), lambda b, s: (0, 0))],
        out_specs=pl.BlockSpec((nbt, c, DN_W), lambda b, s: (b, s, 0)),
        out_shape=jax.ShapeDtypeStruct((bsz, seq, DN_W), BF16),
        scratch_shapes=[pltpu.VMEM((nbt * DN_HEADS, DN_DK, DN_DK), F32)],
        compiler_params=_cparams(("parallel", "arbitrary")),
        name="dn",
    )(main3, main3, main3, main3, conv_w, alog_row, dtb_row, norm_g.reshape(1, DN_DK))


def _merge_kernel(ya_ref, yb_ref, yc_ref, mg_ref, x_ref, gt_ref, wa_ref, wb_ref, wc_ref, wo_ref, o_ref):
    d = x_ref.shape[1]
    gates = jax.nn.sigmoid(mg_ref[...].astype(F32))
    merged = (gates[:, :d] * jnp.dot(ya_ref[...], wa_ref[...], preferred_element_type=F32)
              + gates[:, d:2 * d] * jnp.dot(yb_ref[...], wb_ref[...], preferred_element_type=F32)
              + gates[:, 2 * d:] * jnp.dot(yc_ref[...], wc_ref[...], preferred_element_type=F32))
    y = jnp.dot(merged.astype(BF16), wo_ref[...], preferred_element_type=F32)
    o_ref[...] = x_ref[...] + gt_ref[0] * y


def _merge_call(ya, yb, yc, mg, x2, gt, wa, wb, wc, wo, seq):
    t, d = x2.shape
    tm = min(MERGE_TM, seq)
    per_b = seq // tm
    full = lambda a: pl.BlockSpec(a.shape, lambda i: (0, 0))
    return pl.pallas_call(
        _merge_kernel,
        grid=(t // tm,),
        in_specs=[pl.BlockSpec((tm, ya.shape[1]), lambda i: (i, 0)),
                  pl.BlockSpec((tm, yb.shape[1]), lambda i: (i, 0)),
                  pl.BlockSpec((tm, yc.shape[1]), lambda i: (i, 0)),
                  pl.BlockSpec((tm, N_MG), lambda i: (i, 0)),
                  pl.BlockSpec((tm, d), lambda i: (i, 0)),
                  pl.BlockSpec((1, 1, d), lambda i: (i // per_b, 0, 0)),
                  full(wa), full(wb), full(wc), full(wo)],
        out_specs=pl.BlockSpec((tm, d), lambda i: (i, 0)),
        out_shape=jax.ShapeDtypeStruct((t, d), F32),
        compiler_params=_cparams(("parallel",)),
        name="merge",
    )(ya, yb, yc, mg, x2, gt, wa, wb, wc, wo)


def _mlp_kernel(x_ref, g_ref, sc_ref, sh_ref, gt_ref, w1_ref, w2_ref, fg_ref, o_ref, acc_scr, *, final):
    j = pl.program_id(1)

    @pl.when((pl.program_id(0) == 0) & (j == 0))
    def _():
        acc_scr[...] = jnp.zeros(acc_scr.shape, F32)

    x = x_ref[...]
    h = _norm_mod(x, g_ref[...], sc_ref[0], sh_ref[0]).astype(BF16)
    a = jnp.maximum(jnp.dot(h, w1_ref[...], preferred_element_type=F32), 0.0)
    acc = acc_scr[...] + jnp.dot((a * a).astype(BF16), w2_ref[...], preferred_element_type=F32)
    acc_scr[...] = jnp.where(j == pl.num_programs(1) - 1, 0.0, acc)
    y = x + gt_ref[0] * acc
    if final:
        ms = jnp.mean(y * y, axis=-1, keepdims=True)
        y = y * lax.rsqrt(ms + EPS) * fg_ref[...]
    o_ref[...] = y


def _mlp_call(x2, g, sc, sh, gt, w1, w2, layer, fg, seq, final):
    t, d = x2.shape
    f = w1.shape[2]
    tm = min(MLP_TM, seq)
    tf = MLP_TF
    per_b = seq // tm
    mod = lambda: pl.BlockSpec((1, 1, d), lambda i, j: (i // per_b, 0, 0))
    return pl.pallas_call(
        functools.partial(_mlp_kernel, final=final),
        grid=(t // tm, f // tf),
        in_specs=[pl.BlockSpec((tm, d), lambda i, j: (i, 0)),
                  pl.BlockSpec((1, d), lambda i, j: (0, 0)),
                  mod(), mod(), mod(),
                  pl.BlockSpec((None, d, tf), lambda i, j: (layer, 0, j)),
                  pl.BlockSpec((None, tf, d), lambda i, j: (layer, j, 0)),
                  pl.BlockSpec((1, d), lambda i, j: (0, 0))],
        out_specs=pl.BlockSpec((tm, d), lambda i, j: (i, 0)),
        out_shape=jax.ShapeDtypeStruct((t, d), F32),
        scratch_shapes=[pltpu.VMEM((tm, d), F32)],
        compiler_params=_cparams(("arbitrary", "arbitrary")),
        name="mlp",
    )(x2, g.reshape(1, d), sc, sh, gt, w1, w2, fg.reshape(1, d))


def _split_w_in(w_in):
    offs = [0]
    for sz in IN_SIZES:
        offs.append(offs[-1] + sz)
    return [w_in[:, offs[i]:offs[i + 1]] for i in range(len(IN_SIZES))]


def _layout_w_in(w_in):
    nq, nkc, nvc, nks, nvs, nkw, nvw, ngate, pin, dqkv, dz, dbeta, da, mg = _split_w_in(w_in)
    d = w_in.shape[0]
    per_g = 3 * NSA_REP
    zeros = lambda n: jnp.zeros((d, n), w_in.dtype)
    gate0 = jnp.concatenate([ngate[:, :per_g], zeros(LANE_BETA - per_g), dbeta, da,
                             zeros(LANES - LANE_DECAY - DN_HEADS)], axis=1)
    gate1 = jnp.concatenate([ngate[:, per_g:], zeros(LANES - per_g)], axis=1)
    main = jnp.concatenate([nq, pin, dz, dqkv, gate0, gate1, nkc, nvc], axis=1)

    def per_group(w):
        pad = zeros(LANES - NSA_DH)
        return jnp.concatenate([w[:, :NSA_DH], pad, w[:, NSA_DH:], pad], axis=1)

    kv = jnp.concatenate([per_group(nks), per_group(nvs), per_group(nkw), per_group(nvw)], axis=1)
    return main.astype(BF16), jnp.concatenate([mg, kv], axis=1).astype(BF16)


def _group_lane_w1(w1):
    two, _, hid = w1.shape
    halves = w1.reshape(two, 2, CMP_STRIDE, NSA_DH, hid)
    out = []
    for g in range(NSA_GROUPS):
        padded = jnp.zeros((two, 2, CMP_STRIDE, LANES, hid), w1.dtype).at[:, :, :, g * NSA_DH:(g + 1) * NSA_DH].set(halves)
        out.append(padded.reshape(two, 2, CMP_STRIDE * LANES, hid))
    return jnp.stack(out, axis=1)


def _nsa_constants(seq):
    nb = seq // SEL_LEN
    ncp = seq // CMP_STRIDE
    j = jnp.arange(nb)[:, None] * SEL_LEN
    i = jnp.arange(ncp)[None, :] * CMP_STRIDE
    overlap = jnp.clip(jnp.minimum(j + SEL_LEN, i + CMP_LEN) - jnp.maximum(j, i), 0) // CMP_STRIDE
    block_onehot = (jnp.arange(seq)[:, None] // SEL_LEN) == jnp.arange(nb)[None, :]
    return overlap.astype(BF16), block_onehot.astype(BF16)


def kernel(x, c, ada_w, ada_b, norm1_g, norm2_g, w_in, phi_k1, phi_k2, phi_v1, phi_v2, pos_k, pos_v, pool_w, pool_scale, dn_conv_w, dn_A_log, dn_dt_bias, dn_norm_g, w_branch_nsa, w_branch_pool, w_branch_dn, w_out, mlp_w1, mlp_w2, final_g):
    bsz, seq, d = x.shape
    assert d == D_MODEL and seq % max(PROJ_TM, MLP_TM, POOL_TS) == 0 and seq % KV_CHUNK == 0
    assert seq % (ATTN_Q_TILE * ATTN_TILES * ATTN_GROUPS) == 0 and (seq // Q_TILE) % SELECT_BUCKETS == 0
    assert DN_CHUNK == DN_DK == LANES
    t = bsz * seq
    depth = ada_w.shape[0]
    chunk = KV_CHUNK

    slopes = 2.0 ** (-(8.0 / NSA_HEADS) * (jnp.arange(NSA_HEADS, dtype=F32) + 1.0))
    overlap, block_onehot = _nsa_constants(seq)
    c_pad = jnp.zeros((SUBLANES, d), F32).at[:bsz].set(c)
    ada_b3 = ada_b.reshape(depth, 1, ada_b.shape[1])
    mlp_w1_bf, mlp_w2_bf = mlp_w1.astype(BF16), mlp_w2.astype(BF16)
    x2 = x.reshape(t, d)

    for l in range(depth):
        mod = _mod_call(c_pad, ada_w, ada_b3, l)[:bsz]
        sh1, sc1, gt1, sh2, sc2, gt2 = [m.reshape(bsz, 1, d) for m in jnp.split(mod, 6, axis=-1)]

        w_main, w_mgkv = _layout_w_in(w_in[l])
        main = _proj_call(x2, norm1_g[l], sc1, sh1, w_main, seq, PROJ_MAIN_TM, F32, "proj_main")
        kv = mg = _proj_kv_call(x2, norm1_g[l], sc1, sh1, w_mgkv, seq)

        main3 = main.reshape(bsz, seq, N_MAIN)
        pos = jnp.zeros((2, SUBLANES, CMP_LEN * NSA_DH), F32).at[:, 0].set(
            jnp.stack([pos_k[l].reshape(-1), pos_v[l].reshape(-1)])).astype(BF16)
        w1 = jnp.stack([phi_k1[l], phi_v1[l]]).astype(BF16)
        w2 = jnp.stack([phi_k2[l], phi_v2[l]])
        w2 = jnp.concatenate([w2, w2], axis=-1).astype(BF16)
        cmp_kv = _compress_call(main3, pos, w1, _group_lane_w1(w1), w2)

        sel, act, yc = _select_call(slopes, main, cmp_kv[0], cmp_kv[1], overlap, bsz, seq)
        lst, cnt = _chunk_lists(act, seq, chunk)
        y_a = _attn_call(lst, cnt, slopes, main, sel, kv, yc, block_onehot, bsz, seq, chunk)

        y_b = _pool_call(main, pool_w[l].astype(BF16), pool_scale[l], bsz, seq)

        conv_w = jnp.zeros((SUBLANES, 3 * DN_W), F32).at[:DN_CONV].set(dn_conv_w[l])
        lane_row = lambda v: jnp.zeros((1, LANES), F32).at[0, LANE_DECAY:LANE_DECAY + DN_HEADS].set(v)
        y_c = _dn_call(main3, conv_w, lane_row(dn_A_log[l]), lane_row(dn_dt_bias[l]),
                       dn_norm_g[l]).reshape(t, DN_W)

        x2 = _merge_call(y_a, y_b, y_c, mg, x2, gt1,
                         w_branch_nsa[l].astype(BF16), w_branch_pool[l].astype(BF16),
                         w_branch_dn[l].astype(BF16), w_out[l].astype(BF16), seq)
        x2 = _mlp_call(x2, norm2_g[l], sc2, sh2, gt2, mlp_w1_bf, mlp_w2_bf, l,
                       final_g, seq, final=(l == depth - 1))
    return x2.reshape(bsz, seq, d)
```

```python
import functools

import jax
import jax.numpy as jnp
from jax import lax
from jax.experimental import pallas as pl
from jax.experimental.pallas import tpu as pltpu

F32 = jnp.float32
BF16 = jnp.bfloat16
HIGHEST = lax.Precision.HIGHEST

D_MODEL = 1024
NSA_HEADS = 8
NSA_GROUPS = 2
NSA_REP = NSA_HEADS // NSA_GROUPS
NSA_DH = 64
CMP_LEN = 32
CMP_STRIDE = 16
SEL_LEN = 64
SEL_SHIFT = 6
TOP_N = 16
WINDOW = 512
FORCE_BONUS = 1.0e4
POOL_WINDOWS = (2, 4, 8, 16)
POOL_GROUP_DIM = 128
POOL_W = len(POOL_WINDOWS) * POOL_GROUP_DIM
DN_HEADS = 4
DN_DK = 128
DN_W = DN_HEADS * DN_DK
DN_CONV = 4
EPS = 1e-6
NEG = -1e30
NSA_Q_W = NSA_HEADS * NSA_DH
NSA_KV_W = NSA_GROUPS * NSA_DH
IN_SIZES = (NSA_Q_W, NSA_KV_W, NSA_KV_W, NSA_KV_W, NSA_KV_W, NSA_KV_W, NSA_KV_W,
            3 * NSA_HEADS, POOL_W, 3 * DN_W, DN_W, DN_HEADS, DN_HEADS, 3 * D_MODEL)

LANES = 128
LANE_SHIFT = 7
SUBLANES = 8
VMEM_LIMIT_BYTES = 48 * 1024 * 1024

Q_TILE = 128
SELECT_BUCKETS = 4
ATTN_Q_TILE = 128
ATTN_TILES = 2
ATTN_GROUPS = 4
KV_CHUNK = 256
DN_CHUNK = 128
DN_BATCH = 4
PROJ_TM = 1024
MLP_TM = 1024
MLP_TF = 2048
MERGE_TM = 512
POOL_TS = 1024
POOL_HALO = 16
DN_HALO = 8

COL_Q = 0
COL_POOL = 512
COL_Z = 1024
COL_QKV = 1536
COL_GATE = 3072
COL_KC = 3328
COL_VC = 3456
N_MAIN = 3584
PROJ_MAIN_TM = 512
N_MG = 3072
LANE_BETA = 16
LANE_DECAY = 20
KVB_KS = 0
KVB_VS = 2
KVB_KW = 4
KVB_VW = 6
KVB_END = 8
N_KV = KVB_END * LANES
LANE_POS_HI = NSA_DH
LANE_POS_LO = NSA_DH + 1


def _cparams(sem):
    return pltpu.CompilerParams(dimension_semantics=sem, vmem_limit_bytes=VMEM_LIMIT_BYTES)


def _silu(v):
    return v * jax.nn.sigmoid(v)


def _mm(a, b):
    return jnp.dot(a.astype(BF16), b.astype(BF16), preferred_element_type=F32)


def _mm_nt(a, b):
    return lax.dot_general(a.astype(BF16), b.astype(BF16), (((1,), (1,)), ((), ())), preferred_element_type=F32)


def _mm_tn(a, b):
    return lax.dot_general(a.astype(BF16), b.astype(BF16), (((0,), (0,)), ((), ())), preferred_element_type=F32)


def _mod_kernel(c_ref, w_ref, b_ref, o_ref):
    cond = _silu(c_ref[...])
    o_ref[...] = jnp.dot(cond, w_ref[...], preferred_element_type=F32, precision=HIGHEST) + b_ref[...]


def _mod_call(c_pad, w, b, layer):
    _, d, n = w.shape
    tn = 1536
    return pl.pallas_call(
        _mod_kernel,
        grid=(n // tn,),
        in_specs=[pl.BlockSpec((SUBLANES, d), lambda j: (0, 0)),
                  pl.BlockSpec((None, d, tn), lambda j: (layer, 0, j)),
                  pl.BlockSpec((None, 1, tn), lambda j: (layer, 0, j))],
        out_specs=pl.BlockSpec((SUBLANES, tn), lambda j: (0, j)),
        out_shape=jax.ShapeDtypeStruct((SUBLANES, n), F32),
        compiler_params=_cparams(("parallel",)),
        name="mod",
    )(c_pad, w, b)


def _norm_mod(x, g, sc, sh):
    ms = jnp.mean(x * x, axis=-1, keepdims=True)
    return (x * lax.rsqrt(ms + EPS) * g) * (1.0 + sc) + sh


def _proj_kernel(x_ref, g_ref, sc_ref, sh_ref, w_ref, o_ref):
    h = _norm_mod(x_ref[...], g_ref[...], sc_ref[0], sh_ref[0]).astype(BF16)
    o_ref[...] = jnp.dot(h, w_ref[...], preferred_element_type=F32).astype(o_ref.dtype)


def _proj_kv_kernel(x_ref, g_ref, sc_ref, sh_ref, w_ref, o_ref, *, tiles_per_seq):
    h = _norm_mod(x_ref[...], g_ref[...], sc_ref[0], sh_ref[0]).astype(BF16)
    full = jnp.dot(h, w_ref[...], preferred_element_type=F32)
    o_ref[:, :N_MG] = full[:, :N_MG].astype(o_ref.dtype)
    acc = full[:, N_MG:]
    tm, n = acc.shape
    col = lax.broadcasted_iota(jnp.int32, (1, n), 1)
    blk = jnp.right_shift(col, LANE_SHIFT)
    lane = col & (LANES - 1)
    is_key = ((blk >= KVB_KS) & (blk < KVB_VS)) | ((blk >= KVB_KW) & (blk < KVB_VW))
    is_val = ((blk >= KVB_VS) & (blk < KVB_KW)) | ((blk >= KVB_VW) & (blk < KVB_END))
    ones_row = jnp.where(is_val & (lane >= NSA_DH), 1.0, 0.0)
    hi_row = jnp.where(is_key & (lane == LANE_POS_HI), 1.0, 0.0)
    lo_row = jnp.where(is_key & (lane == LANE_POS_LO), 1.0, 0.0)
    pos = (pl.program_id(0) % tiles_per_seq) * tm + lax.broadcasted_iota(jnp.int32, (tm, 1), 0)
    pos_hi = jnp.right_shift(pos, LANE_SHIFT).astype(F32)
    pos_lo = (pos & (LANES - 1)).astype(F32)
    o_ref[:, N_MG:] = (acc + ones_row + pos_hi * hi_row + pos_lo * lo_row).astype(o_ref.dtype)


def _proj_specs(tm, d, tn, per_b):
    return [pl.BlockSpec((tm, d), lambda i, j: (i, 0)),
            pl.BlockSpec((1, d), lambda i, j: (0, 0)),
            pl.BlockSpec((1, 1, d), lambda i, j: (i // per_b, 0, 0)),
            pl.BlockSpec((1, 1, d), lambda i, j: (i // per_b, 0, 0)),
            pl.BlockSpec((d, tn), lambda i, j: (0, j))]


def _proj_call(x2, g, sc, sh, w, seq, tm, out_dtype, name):
    t, d = x2.shape
    n = w.shape[1]
    tm = min(tm, seq)
    return pl.pallas_call(
        _proj_kernel,
        grid=(t // tm, 1),
        in_specs=_proj_specs(tm, d, n, seq // tm),
        out_specs=pl.BlockSpec((tm, n), lambda i, j: (i, j)),
        out_shape=jax.ShapeDtypeStruct((t, n), out_dtype),
        compiler_params=_cparams(("parallel", "arbitrary")),
        name=name,
    )(x2, g.reshape(1, d), sc, sh, w)


def _proj_kv_call(x2, g, sc, sh, w, seq):
    t, d = x2.shape
    n = w.shape[1]
    tm = min(PROJ_TM, seq)
    return pl.pallas_call(
        functools.partial(_proj_kv_kernel, tiles_per_seq=seq // tm),
        grid=(t // tm, 1),
        in_specs=_proj_specs(tm, d, n, seq // tm),
        out_specs=pl.BlockSpec((tm, n), lambda i, j: (i, j)),
        out_shape=jax.ShapeDtypeStruct((t, n), BF16),
        compiler_params=_cparams(("parallel", "arbitrary")),
        name="proj_kv",
    )(x2, g.reshape(1, d), sc, sh, w)


def _compress_kernel(x_ref, pos_ref, w1_ref, w1g_ref, w2_ref, o_ref):
    nr = x_ref.shape[1] // CMP_STRIDE
    xs = jnp.concatenate([x_ref[0, pl.ds(rho, nr, stride=CMP_STRIDE), :].astype(BF16) for rho in range(CMP_STRIDE)],
                         axis=1)
    posb = jnp.dot(pos_ref[0], w1_ref[0], preferred_element_type=F32)[0:1]
    for g in range(NSA_GROUPS):
        top = jnp.dot(xs, w1g_ref[0, g, 0], preferred_element_type=F32)
        bot = jnp.dot(xs, w1g_ref[0, g, 1], preferred_element_type=F32)
        hid = top + pltpu.roll(bot, nr - 1, 0) + posb
        o_ref[0, 0, g] = _mm(_silu(hid), w2_ref[0]).astype(o_ref.dtype)


def _compress_call(main3, pos, w1, w1g, w2):
    b, seq, _ = main3.shape
    nr = seq // CMP_STRIDE
    two, l64, hid = w1.shape
    return pl.pallas_call(
        _compress_kernel,
        grid=(two, b),
        in_specs=[pl.BlockSpec((1, seq, LANES), lambda a, i: (i, 0, COL_KC // LANES + a)),
                  pl.BlockSpec((1, SUBLANES, l64), lambda a, i: (a, 0, 0)),
                  pl.BlockSpec((1, l64, hid), lambda a, i: (a, 0, 0)),
                  pl.BlockSpec((1,) + w1g.shape[1:], lambda a, i: (a, 0, 0, 0, 0)),
                  pl.BlockSpec((1, hid, LANES), lambda a, i: (a, 0, 0))],
        out_specs=pl.BlockSpec((1, 1, NSA_GROUPS, nr, LANES), lambda a, i: (a, i, 0, 0, 0)),
        out_shape=jax.ShapeDtypeStruct((two, b, NSA_GROUPS, nr, LANES), BF16),
        compiler_params=_cparams(("parallel", "parallel")),
        name="compress",
    )(main3, pos, w1, w1g, w2)


def _gate_cols(gate_logits, branch):
    sg = jax.nn.sigmoid(gate_logits)
    return [sg[:, 3 * r + branch:3 * r + branch + 1] for r in range(NSA_REP)]


def _stack_heads_dup(q):
    qa = q[:, :LANES]
    qb = q[:, LANES:]
    lo = lax.broadcasted_iota(jnp.int32, qa.shape, 1) < NSA_DH
    z = jnp.zeros_like(qa)
    return jnp.concatenate([jnp.where(lo, qa, z), jnp.where(lo, z, qa),
                            jnp.where(lo, qb, z), jnp.where(lo, z, qb)], axis=0)


def _unstack_heads_dup(o, gates):
    nq = o.shape[0] // NSA_REP
    lo = lax.broadcasted_iota(jnp.int32, (nq, LANES), 1) < NSA_DH
    a = jnp.where(lo, gates[0] * o[0:nq], gates[1] * o[nq:2 * nq])
    b = jnp.where(lo, gates[2] * o[2 * nq:3 * nq], gates[3] * o[3 * nq:])
    return jnp.concatenate([a, b], axis=1)


def _stack_heads_pos(q, slopes):
    qa = q[:, :LANES]
    qb = q[:, LANES:]
    lane = lax.broadcasted_iota(jnp.int32, qa.shape, 1)
    lo = lane < NSA_DH
    heads = [qa, pltpu.roll(qa, NSA_DH, 1), qb, pltpu.roll(qb, NSA_DH, 1)]
    out = []
    for r in range(NSA_REP):
        extra = jnp.where(lane == LANE_POS_HI, slopes[r] * LANES, jnp.where(lane == LANE_POS_LO, slopes[r], 0.0))
        out.append(jnp.where(lo, heads[r], extra))
    return jnp.concatenate(out, axis=0)


def _finish_heads(acc, gates):
    nq = acc.shape[0] // NSA_REP
    o = acc * pltpu.roll(1.0 / acc, NSA_DH, 1)
    lo = lax.broadcasted_iota(jnp.int32, (nq, LANES), 1) < NSA_DH
    hs = [gates[r] * o[r * nq:(r + 1) * nq] for r in range(NSA_REP)]
    a = jnp.where(lo, hs[0], pltpu.roll(hs[1], NSA_DH, 1))
    b = jnp.where(lo, hs[2], pltpu.roll(hs[3], NSA_DH, 1))
    return jnp.concatenate([a, b], axis=1)


def _select_kernel(slopes_ref, q_ref, kc_ref, vc_ref, gates_ref, ov_ref, sel_ref, act_ref, yc_ref):
    qt = pl.program_id(1)
    nqt = pl.num_programs(1)
    nb_all, ncp_all = ov_ref.shape
    for k in range(SELECT_BUCKETS):
        @pl.when((qt * SELECT_BUCKETS) // nqt == k)
        def _(k=k):
            _select_prefix(slopes_ref, q_ref, kc_ref, vc_ref, gates_ref, ov_ref, sel_ref, act_ref, yc_ref,
                           nb=(k + 1) * nb_all // SELECT_BUCKETS, ncp=(k + 1) * ncp_all // SELECT_BUCKETS)


def _select_prefix(slopes_ref, q_ref, kc_ref, vc_ref, gates_ref, ov_ref, sel_ref, act_ref, yc_ref, *, nb, ncp):
    t0 = pl.program_id(1) * Q_TILE
    nq = Q_TILE
    nb_all = ov_ref.shape[0]
    c_end = lax.broadcasted_iota(jnp.int32, (nq, ncp), 1) * CMP_STRIDE + (CMP_LEN - 1)
    tq = t0 + lax.broadcasted_iota(jnp.int32, (nq, ncp), 0)
    vis = c_end <= tq
    any_vis = tq[:, 0:1] >= CMP_LEN - 1
    rel = (c_end[0:1] - t0).astype(F32)
    ov = ov_ref[:nb, :ncp]
    jb = lax.broadcasted_iota(jnp.int32, (nb, nq), 0)
    tql = t0 + lax.broadcasted_iota(jnp.int32, (nb, nq), 1)
    valid = jb * SEL_LEN <= tql
    forced = (jb == 0) | (jb == jnp.right_shift(tql, SEL_SHIFT))
    scores = []
    for g in range(NSA_GROUPS):
        qcols = slice(g * NSA_REP * NSA_DH, (g + 1) * NSA_REP * NSA_DH)
        qs = _stack_heads_dup(q_ref[:, qcols] * (NSA_DH ** -0.5)).astype(BF16)
        s = _mm_nt(qs, kc_ref[0, g, :ncp, :])
        psum = jnp.zeros((nq, ncp), F32)
        ps = []
        for r in range(NSA_REP):
            m_r = slopes_ref[g * NSA_REP + r]
            sr = jnp.where(vis, s[r * nq:(r + 1) * nq] + m_r * rel, NEG)
            mx = jnp.max(sr, axis=-1, keepdims=True)
            e = jnp.exp(sr - mx)
            den = jnp.sum(e, axis=-1, keepdims=True)
            p = e * jnp.where(any_vis, 1.0 / den, 0.0)
            psum = psum + p
            ps.append(p.astype(BF16))
        o_c = jnp.dot(jnp.concatenate(ps, axis=0), vc_ref[0, g, :ncp, :], preferred_element_type=F32)
        yc_ref[:, qcols] = _unstack_heads_dup(o_c, _gate_cols(gates_ref[:, g * LANES:(g + 1) * LANES], 0))
        p_hi = psum.astype(BF16)
        p_lo = (psum - p_hi.astype(F32)).astype(BF16)
        imp_t = _mm_nt(ov, p_hi) + _mm_nt(ov, p_lo)
        scores.append(jnp.where(valid, imp_t + FORCE_BONUS * forced.astype(F32), NEG))

    for _ in range(min(TOP_N, nb)):
        for g in range(NSA_GROUPS):
            mx = jnp.max(scores[g], axis=0, keepdims=True)
            idx = jnp.min(jnp.where(scores[g] == mx, jb, nb), axis=0, keepdims=True)
            scores[g] = jnp.where(jb == idx, -jnp.inf, scores[g])
    for g in range(NSA_GROUPS):
        sel_t = jnp.where((scores[g] == -jnp.inf) & valid, 1.0, 0.0)
        if nb < nb_all:
            sel_t = jnp.concatenate([sel_t, jnp.zeros((nb_all - nb, nq), F32)], axis=0)
        sel_q = sel_t.T
        sel_ref[0, g] = sel_q.astype(sel_ref.dtype)
        act_ref[0, g, 0] = jnp.max(sel_q, axis=0, keepdims=True)


def _select_call(slopes, main, kcmp, vcmp, ov, bsz, seq):
    nqt = seq // Q_TILE
    nb, ncp = ov.shape
    row = lambda b, q: b * nqt + q
    gw = NSA_GROUPS * LANES
    return pl.pallas_call(
        _select_kernel,
        grid=(bsz, nqt),
        in_specs=[pl.BlockSpec(memory_space=pltpu.SMEM),
                  pl.BlockSpec((Q_TILE, NSA_Q_W), lambda b, q: (row(b, q), COL_Q // NSA_Q_W)),
                  pl.BlockSpec((1, NSA_GROUPS, ncp, LANES), lambda b, q: (b, 0, 0, 0)),
                  pl.BlockSpec((1, NSA_GROUPS, ncp, LANES), lambda b, q: (b, 0, 0, 0)),
                  pl.BlockSpec((Q_TILE, gw), lambda b, q: (row(b, q), COL_GATE // gw)),
                  pl.BlockSpec((nb, ncp), lambda b, q: (0, 0))],
        out_specs=[pl.BlockSpec((1, NSA_GROUPS, Q_TILE, nb), lambda b, q: (b, 0, q, 0)),
                   pl.BlockSpec((1, NSA_GROUPS, 1, 1, nb), lambda b, q: (b, 0, q, 0, 0)),
                   pl.BlockSpec((Q_TILE, NSA_Q_W), lambda b, q: (row(b, q), 0))],
        out_shape=[jax.ShapeDtypeStruct((bsz, NSA_GROUPS, seq, nb), BF16),
                   jax.ShapeDtypeStruct((bsz, NSA_GROUPS, nqt, 1, nb), F32),
                   jax.ShapeDtypeStruct((bsz * seq, NSA_Q_W), F32)],
        compiler_params=_cparams(("parallel", "parallel")),
        name="select",
    )(slopes, main, kcmp, vcmp, main, ov)


def _attn_kernel(lst_ref, cnt_ref, slopes_ref, q_ref, sel_ref, ks_ref, vs_ref, kw_ref, vw_ref, gates_ref, yc_ref,
                 et_ref, o_ref, s_scr, p_scr, m_scr, a_scr, acc_scr, *, chunk):
    b = pl.program_id(0)
    g = pl.program_id(1)
    step = pl.program_id(2)
    tiles_per_step = ATTN_TILES * ATTN_GROUPS
    nqt = pl.num_programs(2) * tiles_per_step
    nq = ATTN_Q_TILE
    nw = WINDOW + nq
    nch_total = ks_ref.shape[0] // chunk
    slopes = [slopes_ref[g * NSA_REP + r] for r in range(NSA_REP)]

    def values(p, c):
        start = pl.multiple_of(c * chunk, chunk)
        return jnp.dot(p, vs_ref[pl.ds(start, chunk), :], preferred_element_type=F32)

    def row_max(s):
        return jnp.broadcast_to(jnp.max(s, axis=-1, keepdims=True), (s.shape[0], LANES))

    def lane_tile(m, width):
        return jnp.concatenate([m] * (width // LANES), axis=1)

    def masked(s, ok):
        return jnp.concatenate([jnp.where(ok, s[r * nq:(r + 1) * nq], NEG) for r in range(NSA_REP)], axis=0)

    class Group:
        pass

    def setup(k):
        gr = Group()
        gr.tiles = range(ATTN_TILES)
        gr.scr = [(k % 2) * ATTN_TILES + u for u in gr.tiles]
        local = [k * ATTN_TILES + u for u in gr.tiles]
        gr.rws = [slice(v * nq, (v + 1) * nq) for v in local]
        qts = [step * tiles_per_step + v for v in local]
        gr.t0s = [qt * nq for qt in qts]
        gr.qs = [_stack_heads_pos(q_ref[rw, :] * (NSA_DH ** -0.5), slopes).astype(BF16) for rw in gr.rws]
        pens = [((sel_ref[0, 0, rw, :].astype(F32) - 1.0) * (-NEG)).astype(BF16) for rw in gr.rws]
        gr.qx = [jnp.concatenate([gr.qs[u], jnp.concatenate([pens[u]] * NSA_REP, axis=0)], axis=1) for u in gr.tiles]
        gr.bases = [((b * NSA_GROUPS + g) * nqt + qt) * nch_total for qt in qts]
        gr.cnts = [cnt_ref[(b * NSA_GROUPS + g) * nqt + qt] for qt in qts]
        for u in gr.tiles:
            su = gr.scr[u]
            m_scr[su] = jnp.full(m_scr.shape[1:], NEG, F32)
            a_scr[su] = jnp.ones(a_scr.shape[1:], F32)
            acc_scr[su] = jnp.zeros(acc_scr.shape[1:], F32)
            p_scr[su, 1] = jnp.zeros(p_scr.shape[2:], BF16)
        for u in gr.tiles:
            s_scr[gr.scr[u], 0] = logits(gr, u, lst_ref[gr.bases[u]])
        return gr

    def logits(gr, u, c):
        start = pl.multiple_of(c * chunk, chunk)
        kx = jnp.concatenate([ks_ref[pl.ds(start, chunk), :], et_ref[pl.ds(start, chunk), :]], axis=1)
        return _mm_nt(gr.qx[u], kx)

    def flash_step(gr, us, i, slot):
        for u in us:
            su = gr.scr[u]
            acc_scr[su] = a_scr[su] * acc_scr[su] + values(p_scr[su, 1 - slot],
                                                           lst_ref[gr.bases[u] + jnp.maximum(i - 1, 0)])
        for u in us:
            su = gr.scr[u]
            s = s_scr[su, slot]
            m_prev = m_scr[su]
            m_new = jnp.maximum(m_prev, row_max(s))
            a_scr[su] = jnp.exp(m_prev - m_new)
            p_scr[su, slot] = jnp.exp(s - lane_tile(m_new, chunk)).astype(BF16)
            m_scr[su] = m_new
        for u in us:
            s_scr[gr.scr[u], 1 - slot] = logits(gr, u, lst_ref[gr.bases[u] + i + 1])

    def flash_loops(gr):
        def single_steps(us):
            def body(i, carry):
                flash_step(gr, us, i, i & 1)
                return carry
            return body

        def double_steps(us):
            def body(j, carry):
                flash_step(gr, us, 2 * j, 0)
                flash_step(gr, us, 2 * j + 1, 1)
                return carry
            return body

        joint = functools.reduce(jnp.minimum, gr.cnts)
        lax.fori_loop(0, joint // 2, double_steps(list(gr.tiles)), 0)
        lax.fori_loop(joint - (joint & 1), joint, single_steps(list(gr.tiles)), 0)
        for u in gr.tiles:
            lax.fori_loop(joint, gr.cnts[u], single_steps([u]), 0)

    def finish(gr):
        tiles = gr.tiles
        scr = gr.scr
        slots = [gr.cnts[u] & 1 for u in tiles]
        c_diag = [gr.t0s[u] // chunk for u in tiles]
        acc = [a_scr[scr[u]] * acc_scr[scr[u]]
               + values(p_scr[scr[u], 1 - slots[u]], lst_ref[gr.bases[u] + jnp.maximum(gr.cnts[u] - 1, 0)])
               for u in tiles]
        wbase = [pl.multiple_of(jnp.maximum(gr.t0s[u] - WINDOW, 0), nq) for u in tiles]
        s_win = [_mm_nt(gr.qs[u], kw_ref[pl.ds(wbase[u], nw), :]) for u in tiles]
        s_diag = []
        for u in tiles:
            pos = c_diag[u] * chunk + lax.broadcasted_iota(jnp.int32, (nq, chunk), 1)
            tq = gr.t0s[u] + lax.broadcasted_iota(jnp.int32, (nq, chunk), 0)
            s_diag.append(masked(s_scr[scr[u], slots[u]], pos <= tq))
        m_prev = [m_scr[scr[u]] for u in tiles]
        m_new = [jnp.maximum(m_prev[u], row_max(s_diag[u])) for u in tiles]
        acc = [jnp.exp(m_prev[u] - m_new[u]) * acc[u]
               + values(jnp.exp(s_diag[u] - lane_tile(m_new[u], chunk)).astype(BF16), c_diag[u]) for u in tiles]
        p_win = []
        for u in tiles:
            pos = wbase[u] + lax.broadcasted_iota(jnp.int32, (nq, nw), 1)
            tq = gr.t0s[u] + lax.broadcasted_iota(jnp.int32, (nq, nw), 0)
            s = masked(s_win[u], (pos <= tq) & (pos > tq - WINDOW))
            p_win.append(jnp.exp(s - jnp.max(s, axis=-1, keepdims=True)).astype(BF16))
        acc_w = [jnp.dot(p_win[u], vw_ref[pl.ds(wbase[u], nw), :], preferred_element_type=F32) for u in tiles]
        for u in tiles:
            gates = gates_ref[gr.rws[u], :]
            o_ref[gr.rws[u], :] = (yc_ref[gr.rws[u], :] + _finish_heads(acc[u], _gate_cols(gates, 1))
                                   + _finish_heads(acc_w[u], _gate_cols(gates, 2))).astype(o_ref.dtype)

    cur = setup(0)
    for k in range(ATTN_GROUPS):
        flash_loops(cur)
        nxt = setup(k + 1) if k + 1 < ATTN_GROUPS else None
        finish(cur)
        cur = nxt


def _chunk_lists(act, seq, chunk):
    bsz, ngrp = act.shape[:2]
    nqt = seq // ATTN_Q_TILE
    nch = seq // chunk
    active = act.reshape(bsz, ngrp, nqt, ATTN_Q_TILE // Q_TILE, nch, chunk // SEL_LEN).max(axis=(3, 5)) > 0.0
    c_diag = (jnp.arange(nqt) * ATTN_Q_TILE) // chunk
    active = active & (jnp.arange(nch)[None, :] < c_diag[:, None])
    cnt = active.sum(axis=-1).astype(jnp.int32)
    order = jnp.argsort(jnp.logical_not(active), axis=-1, stable=True).astype(jnp.int32)
    lst = jnp.where(jnp.arange(nch) < cnt[..., None], order, c_diag[:, None].astype(jnp.int32))
    return lst.reshape(-1), cnt.reshape(-1)


def _attn_call(lst, cnt, slopes, main, sel, kv, yc, et, bsz, seq, chunk):
    tq = ATTN_Q_TILE * ATTN_TILES * ATTN_GROUPS
    nset = 2 * ATTN_TILES
    nqt = seq // tq
    nb = sel.shape[-1]
    rows = NSA_REP * ATTN_Q_TILE
    row = lambda b, g, q: b * nqt + q
    kvspec = lambda blk: pl.BlockSpec((seq, LANES), lambda b, g, q, *_: (b, N_MG // LANES + blk + g))
    grid_spec = pltpu.PrefetchScalarGridSpec(
        num_scalar_prefetch=2,
        grid=(bsz, NSA_GROUPS, nqt),
        in_specs=[pl.BlockSpec(memory_space=pltpu.SMEM),
                  pl.BlockSpec((tq, 256), lambda b, g, q, *_: (row(b, g, q), COL_Q // 256 + g)),
                  pl.BlockSpec((1, 1, tq, nb), lambda b, g, q, *_: (b, g, q, 0)),
                  kvspec(KVB_KS), kvspec(KVB_VS), kvspec(KVB_KW), kvspec(KVB_VW),
                  pl.BlockSpec((tq, LANES), lambda b, g, q, *_: (row(b, g, q), COL_GATE // LANES + g)),
                  pl.BlockSpec((tq, 256), lambda b, g, q, *_: (row(b, g, q), g)),
                  pl.BlockSpec(et.shape, lambda b, g, q, *_: (0, 0))],
        out_specs=pl.BlockSpec((tq, 256), lambda b, g, q, *_: (row(b, g, q), g)),
        scratch_shapes=[pltpu.VMEM((nset, 2, rows, chunk), F32),
                        pltpu.VMEM((nset, 2, rows, chunk), BF16),
                        pltpu.VMEM((nset, rows, LANES), F32),
                        pltpu.VMEM((nset, rows, LANES), F32),
                        pltpu.VMEM((nset, rows, LANES), F32)],
    )
    return pl.pallas_call(
        functools.partial(_attn_kernel, chunk=chunk),
        grid_spec=grid_spec,
        out_shape=jax.ShapeDtypeStruct((bsz * seq, NSA_Q_W), BF16),
        compiler_params=_cparams(("parallel", "parallel", "parallel")),
        name="attn",
    )(lst, cnt, slopes, main, sel, kv, kv, kv, kv, main, yc, et)


def _pool_kernel(x_ref, halo_ref, w_ref, sc_ref, o_ref):
    s = pl.program_id(1)
    ts = x_ref.shape[0]
    x = x_ref[...]
    halo = jnp.where(s > 0, halo_ref[...], 0.0)
    t = s * ts + lax.broadcasted_iota(jnp.int32, (ts, 1), 0)
    for gi, win in enumerate(POOL_WINDOWS):
        lanes = slice(gi * POOL_GROUP_DIM, (gi + 1) * POOL_GROUP_DIM)
        ext = jnp.concatenate([halo[:, lanes], x[:, lanes]], axis=0)
        acc = ext
        span = 1
        while span < win:
            acc = acc + pltpu.roll(acc, span, 0)
            span *= 2
        cnt = jnp.minimum(t + 1, win).astype(F32)
        y = acc[POOL_HALO:] / cnt - x[:, lanes]
        o_ref[:, lanes] = (_mm(y, w_ref[gi]) * sc_ref[:, lanes]).astype(o_ref.dtype)


def _pool_call(main, w, scale, bsz, seq):
    ts = min(POOL_TS, seq)
    per_b = seq // ts
    halo_per_tile = ts // POOL_HALO
    return pl.pallas_call(
        _pool_kernel,
        grid=(bsz, per_b),
        in_specs=[pl.BlockSpec((ts, POOL_W), lambda b, s: (b * per_b + s, COL_POOL // POOL_W)),
                  pl.BlockSpec((POOL_HALO, POOL_W),
                               lambda b, s: (jnp.maximum((b * per_b + s) * halo_per_tile - 1, 0), COL_POOL // POOL_W)),
                  pl.BlockSpec(w.shape, lambda b, s: (0, 0, 0)),
                  pl.BlockSpec((1, POOL_W), lambda b, s: (0, 0))],
        out_specs=pl.BlockSpec((ts, POOL_W), lambda b, s: (b * per_b + s, 0)),
        out_shape=jax.ShapeDtypeStruct((bsz * seq, POOL_W), BF16),
        compiler_params=_cparams(("parallel", "parallel")),
        name="pool",
    )(main, main, w, scale.reshape(1, POOL_W))


def _unit_lower_inverses(nms, ri, ci):
    c = nms[0].shape[0]
    eye = jnp.where(ri == ci, 1.0, 0.0)
    in8 = jnp.right_shift(ri, 3) == jnp.right_shift(ci, 3)
    n8 = [jnp.where(in8, nm, 0.0) for nm in nms]
    ps = [eye - a for a in n8]
    ms = [_mm(a, a) for a in n8]
    ps = [p + _mm(p, m) for p, m in zip(ps, ms)]
    ms = [_mm(m, m) for m in ms]
    ps = [p + _mm(p, m) for p, m in zip(ps, ms)]
    shift = 3
    while (1 << shift) < c:
        rb = jnp.right_shift(ri, shift)
        cb = jnp.right_shift(ci, shift)
        lower_left = ((rb & 1) == 1) & (cb == rb - 1)
        ts = [_mm(p, jnp.where(lower_left, nm, 0.0)) for p, nm in zip(ps, nms)]
        ps = [p - _mm(t, p) for p, t in zip(ps, ts)]
        shift += 1
    return ps


def _row_cumsum(x, ri):
    span = 1
    while span < x.shape[0]:
        x = x + jnp.where(ri >= span, pltpu.roll(x, span, 0), 0.0)
        span *= 2
    return x


def _dn_kernel(qkv_ref, halo_ref, z_ref, gba_ref, cw_ref, alog_ref, dtb_ref, ng_ref, o_ref, st_scr):
    s = pl.program_id(1)
    nbatch, c = qkv_ref.shape[0], qkv_ref.shape[1]

    @pl.when(s == 0)
    def _():
        st_scr[...] = jnp.zeros(st_scr.shape, F32)

    ri = lax.broadcasted_iota(jnp.int32, (c, c), 0)
    ci = lax.broadcasted_iota(jnp.int32, (c, c), 1)
    tril = ri >= ci
    strict = ri > ci
    cw = cw_ref[...]
    ng = ng_ref[...]
    chains = [(bi, h) for bi in range(nbatch) for h in range(DN_HEADS)]
    qn, kn, kb, vb, gcb, lmask = [], [], [], [], [], []
    for bi in range(nbatch):
        halo = jnp.where(s > 0, halo_ref[bi], 0.0)
        ext = jnp.concatenate([halo, qkv_ref[bi]], axis=0)
        conv = cw[DN_CONV - 1:DN_CONV] * ext
        for tap in range(1, DN_CONV):
            conv = conv + cw[DN_CONV - 1 - tap:DN_CONV - tap] * pltpu.roll(ext, tap, 0)
        act = _silu(conv[DN_HALO:])

        gba = gba_ref[bi]
        beta_all = jax.nn.sigmoid(gba)
        xs = gba + dtb_ref[...]
        softplus = jnp.maximum(xs, 0.0) + jnp.log(1.0 + jnp.exp(-jnp.abs(xs)))
        gc_all = _row_cumsum(-jnp.exp(alog_ref[...]) * softplus, ri)
        gct_all = gc_all.T
        for h in range(DN_HEADS):
            q = act[:, h * DN_DK:(h + 1) * DN_DK]
            k = act[:, DN_W + h * DN_DK:DN_W + (h + 1) * DN_DK]
            v = act[:, 2 * DN_W + h * DN_DK:2 * DN_W + (h + 1) * DN_DK]
            qn.append(q * lax.rsqrt(jnp.sum(q * q, axis=-1, keepdims=True) + EPS) * (DN_DK ** -0.5))
            kn.append(k * lax.rsqrt(jnp.sum(k * k, axis=-1, keepdims=True) + EPS))
            beta = beta_all[:, LANE_BETA + h:LANE_BETA + h + 1]
            gcb.append(jnp.broadcast_to(gc_all[:, LANE_DECAY + h:LANE_DECAY + h + 1], (c, DN_DK)))
            gct = jnp.broadcast_to(gct_all[LANE_DECAY + h:LANE_DECAY + h + 1, :], (c, c))
            lmask.append(jnp.where(tril, jnp.exp(jnp.where(tril, gcb[-1] - gct, 0.0)), 0.0))
            kb.append(kn[-1] * beta)
            vb.append(v * beta)

    n = range(len(chains))
    kk = [_mm_nt(kb[i], kn[i]) for i in n]
    tinv = _unit_lower_inverses([jnp.where(strict, kk[i] * lmask[i], 0.0) for i in n], ri, ci)
    egc = [jnp.exp(gcb[i]) for i in n]
    sol = [_mm(tinv[i], jnp.concatenate([vb[i], kb[i] * egc[i]], axis=1)) for i in n]
    aqk = [_mm_nt(qn[i], kn[i]) * lmask[i] for i in n]
    state = [st_scr[i] for i in n]
    glast = [gcb[i][c - 1:c, :] for i in n]
    v_new = [sol[i][:, :DN_DK] - _mm(sol[i][:, DN_DK:], state[i]) for i in n]
    o_state = [_mm(qn[i] * egc[i], state[i]) for i in n]
    o = [o_state[i] + _mm(aqk[i], v_new[i]) for i in n]
    upd = [_mm_tn(kn[i] * jnp.exp(glast[i] - gcb[i]), v_new[i]) for i in n]
    for i, (bi, h) in enumerate(chains):
        st_scr[i] = state[i] * jnp.exp(glast[i]) + upd[i]
        on = o[i] * lax.rsqrt(jnp.mean(o[i] * o[i], axis=-1, keepdims=True) + EPS) * ng
        o_ref[bi, :, h * DN_DK:(h + 1) * DN_DK] = (
            on * _silu(z_ref[bi, :, h * DN_DK:(h + 1) * DN_DK])).astype(o_ref.dtype)


def _dn_call(main3, conv_w, alog_row, dtb_row, norm_g):
    bsz, seq, _ = main3.shape
    c = DN_CHUNK
    nbt = DN_BATCH if bsz % DN_BATCH == 0 else 1
    halo_per_chunk = c // DN_HALO
    w3 = 3 * DN_W
    return pl.pallas_call(
        _dn_kernel,
        grid=(bsz // nbt, seq // c),
        in_specs=[pl.BlockSpec((nbt, c, w3), lambda b, s: (b, s, COL_QKV // w3)),
                  pl.BlockSpec((nbt, DN_HALO, w3),
                               lambda b, s: (b, jnp.maximum(s * halo_per_chunk - 1, 0), COL_QKV // w3)),
                  pl.BlockSpec((nbt, c, DN_W), lambda b, s: (b, s, COL_Z // DN_W)),
                  pl.BlockSpec((nbt, c, LANES), lambda b, s: (b, s, COL_GATE // LANES)),
                  pl.BlockSpec((SUBLANES, w3), lambda b, s: (0, 0)),
                  pl.BlockSpec((1, LANES), lambda b, s: (0, 0)),
                  pl.BlockSpec((1, LANES), lambda b, s: (0, 0)),
                  pl.BlockSpec((1, DN_DK), lambda b, s: (0, 0))],
        out_specs=pl.BlockSpec((nbt, c, DN_W), lambda b, s: (b, s, 0)),
        out_shape=jax.ShapeDtypeStruct((bsz, seq, DN_W), BF16),
        scratch_shapes=[pltpu.VMEM((nbt * DN_HEADS, DN_DK, DN_DK), F32)],
        compiler_params=_cparams(("parallel", "arbitrary")),
        name="dn",
    )(main3, main3, main3, main3, conv_w, alog_row, dtb_row, norm_g.reshape(1, DN_DK))


def _merge_kernel(ya_ref, yb_ref, yc_ref, mg_ref, x_ref, gt_ref, wa_ref, wb_ref, wc_ref, wo_ref, o_ref):
    d = x_ref.shape[1]
    gates = jax.nn.sigmoid(mg_ref[...].astype(F32))
    merged = (gates[:, :d] * jnp.dot(ya_ref[...], wa_ref[...], preferred_element_type=F32)
              + gates[:, d:2 * d] * jnp.dot(yb_ref[...], wb_ref[...], preferred_element_type=F32)
              + gates[:, 2 * d:] * jnp.dot(yc_ref[...], wc_ref[...], preferred_element_type=F32))
    y = jnp.dot(merged.astype(BF16), wo_ref[...], preferred_element_type=F32)
    o_ref[...] = x_ref[...] + gt_ref[0] * y


def _merge_call(ya, yb, yc, mg, x2, gt, wa, wb, wc, wo, seq):
    t, d = x2.shape
    tm = min(MERGE_TM, seq)
    per_b = seq // tm
    full = lambda a: pl.BlockSpec(a.shape, lambda i: (0, 0))
    return pl.pallas_call(
        _merge_kernel,
        grid=(t // tm,),
        in_specs=[pl.BlockSpec((tm, ya.shape[1]), lambda i: (i, 0)),
                  pl.BlockSpec((tm, yb.shape[1]), lambda i: (i, 0)),
                  pl.BlockSpec((tm, yc.shape[1]), lambda i: (i, 0)),
                  pl.BlockSpec((tm, N_MG), lambda i: (i, 0)),
                  pl.BlockSpec((tm, d), lambda i: (i, 0)),
                  pl.BlockSpec((1, 1, d), lambda i: (i // per_b, 0, 0)),
                  full(wa), full(wb), full(wc), full(wo)],
        out_specs=pl.BlockSpec((tm, d), lambda i: (i, 0)),
        out_shape=jax.ShapeDtypeStruct((t, d), F32),
        compiler_params=_cparams(("parallel",)),
        name="merge",
    )(ya, yb, yc, mg, x2, gt, wa, wb, wc, wo)


def _mlp_kernel(x_ref, g_ref, sc_ref, sh_ref, gt_ref, w1_ref, w2_ref, fg_ref, o_ref, *, final):
    x = x_ref[...]
    h = _norm_mod(x, g_ref[...], sc_ref[0], sh_ref[0]).astype(BF16)
    acc = None
    for c0 in range(0, w1_ref.shape[1], MLP_TF):
        a = jnp.maximum(jnp.dot(h, w1_ref[:, c0:c0 + MLP_TF], preferred_element_type=F32), 0.0)
        part = jnp.dot((a * a).astype(BF16), w2_ref[c0:c0 + MLP_TF, :], preferred_element_type=F32)
        acc = part if acc is None else acc + part
    y = x + gt_ref[0] * acc
    if final:
        ms = jnp.mean(y * y, axis=-1, keepdims=True)
        y = y * lax.rsqrt(ms + EPS) * fg_ref[...]
    o_ref[...] = y


def _mlp_call(x2, g, sc, sh, gt, w1, w2, layer, fg, seq, final):
    t, d = x2.shape
    f = w1.shape[2]
    assert f % MLP_TF == 0
    tm = min(MLP_TM, seq)
    per_b = seq // tm
    mod = lambda: pl.BlockSpec((1, 1, d), lambda i: (i // per_b, 0, 0))
    resident = pl.Buffered(1)
    return pl.pallas_call(
        functools.partial(_mlp_kernel, final=final),
        grid=(t // tm,),
        in_specs=[pl.BlockSpec((tm, d), lambda i: (i, 0)),
                  pl.BlockSpec((1, d), lambda i: (0, 0)),
                  mod(), mod(), mod(),
                  pl.BlockSpec((None, d, f), lambda i: (layer, 0, 0), pipeline_mode=resident),
                  pl.BlockSpec((None, f, d), lambda i: (layer, 0, 0), pipeline_mode=resident),
                  pl.BlockSpec((1, d), lambda i: (0, 0))],
        out_specs=pl.BlockSpec((tm, d), lambda i: (i, 0)),
        out_shape=jax.ShapeDtypeStruct((t, d), F32),
        compiler_params=_cparams(("parallel",)),
        name="mlp",
    )(x2, g.reshape(1, d), sc, sh, gt, w1, w2, fg.reshape(1, d))


def _split_w_in(w_in):
    offs = [0]
    for sz in IN_SIZES:
        offs.append(offs[-1] + sz)
    return [w_in[:, offs[i]:offs[i + 1]] for i in range(len(IN_SIZES))]


def _layout_w_in(w_in):
    nq, nkc, nvc, nks, nvs, nkw, nvw, ngate, pin, dqkv, dz, dbeta, da, mg = _split_w_in(w_in)
    d = w_in.shape[0]
    per_g = 3 * NSA_REP
    zeros = lambda n: jnp.zeros((d, n), w_in.dtype)
    gate0 = jnp.concatenate([ngate[:, :per_g], zeros(LANE_BETA - per_g), dbeta, da,
                             zeros(LANES - LANE_DECAY - DN_HEADS)], axis=1)
    gate1 = jnp.concatenate([ngate[:, per_g:], zeros(LANES - per_g)], axis=1)
    main = jnp.concatenate([nq, pin, dz, dqkv, gate0, gate1, nkc, nvc], axis=1)

    def per_group(w):
        pad = zeros(LANES - NSA_DH)
        return jnp.concatenate([w[:, :NSA_DH], pad, w[:, NSA_DH:], pad], axis=1)

    kv = jnp.concatenate([per_group(nks), per_group(nvs), per_group(nkw), per_group(nvw)], axis=1)
    return main.astype(BF16), jnp.concatenate([mg, kv], axis=1).astype(BF16)


def _group_lane_w1(w1):
    two, _, hid = w1.shape
    halves = w1.reshape(two, 2, CMP_STRIDE, NSA_DH, hid)
    out = []
    for g in range(NSA_GROUPS):
        padded = jnp.zeros((two, 2, CMP_STRIDE, LANES, hid), w1.dtype).at[:, :, :, g * NSA_DH:(g + 1) * NSA_DH].set(halves)
        out.append(padded.reshape(two, 2, CMP_STRIDE * LANES, hid))
    return jnp.stack(out, axis=1)


def _nsa_constants(seq):
    nb = seq // SEL_LEN
    ncp = seq // CMP_STRIDE
    j = jnp.arange(nb)[:, None] * SEL_LEN
    i = jnp.arange(ncp)[None, :] * CMP_STRIDE
    overlap = jnp.clip(jnp.minimum(j + SEL_LEN, i + CMP_LEN) - jnp.maximum(j, i), 0) // CMP_STRIDE
    block_onehot = (jnp.arange(seq)[:, None] // SEL_LEN) == jnp.arange(nb)[None, :]
    return overlap.astype(BF16), block_onehot.astype(BF16)


def kernel(x, c, ada_w, ada_b, norm1_g, norm2_g, w_in, phi_k1, phi_k2, phi_v1, phi_v2, pos_k, pos_v, pool_w, pool_scale, dn_conv_w, dn_A_log, dn_dt_bias, dn_norm_g, w_branch_nsa, w_branch_pool, w_branch_dn, w_out, mlp_w1, mlp_w2, final_g):
    bsz, seq, d = x.shape
    assert d == D_MODEL and seq % max(PROJ_TM, MLP_TM, POOL_TS) == 0 and seq % KV_CHUNK == 0
    assert seq % (ATTN_Q_TILE * ATTN_TILES * ATTN_GROUPS) == 0 and (seq // Q_TILE) % SELECT_BUCKETS == 0
    assert DN_CHUNK == DN_DK == LANES
    t = bsz * seq
    depth = ada_w.shape[0]
    chunk = KV_CHUNK

    slopes = 2.0 ** (-(8.0 / NSA_HEADS) * (jnp.arange(NSA_HEADS, dtype=F32) + 1.0))
    overlap, block_onehot = _nsa_constants(seq)
    c_pad = jnp.zeros((SUBLANES, d), F32).at[:bsz].set(c)
    ada_b3 = ada_b.reshape(depth, 1, ada_b.shape[1])
    mlp_w1_bf, mlp_w2_bf = mlp_w1.astype(BF16), mlp_w2.astype(BF16)
    x2 = x.reshape(t, d)

    for l in range(depth):
        mod = _mod_call(c_pad, ada_w, ada_b3, l)[:bsz]
        sh1, sc1, gt1, sh2, sc2, gt2 = [m.reshape(bsz, 1, d) for m in jnp.split(mod, 6, axis=-1)]

        w_main, w_mgkv = _layout_w_in(w_in[l])
        main = _proj_call(x2, norm1_g[l], sc1, sh1, w_main, seq, PROJ_MAIN_TM, F32, "proj_main")
        kv = mg = _proj_kv_call(x2, norm1_g[l], sc1, sh1, w_mgkv, seq)

        main3 = main.reshape(bsz, seq, N_MAIN)
        pos = jnp.zeros((2, SUBLANES, CMP_LEN * NSA_DH), F32).at[:, 0].set(
            jnp.stack([pos_k[l].reshape(-1), pos_v[l].reshape(-1)])).astype(BF16)
        w1 = jnp.stack([phi_k1[l], phi_v1[l]]).astype(BF16)
        w2 = jnp.stack([phi_k2[l], phi_v2[l]])
        w2 = jnp.concatenate([w2, w2], axis=-1).astype(BF16)
        cmp_kv = _compress_call(main3, pos, w1, _group_lane_w1(w1), w2)

        sel, act, yc = _select_call(slopes, main, cmp_kv[0], cmp_kv[1], overlap, bsz, seq)
        lst, cnt = _chunk_lists(act, seq, chunk)
        y_a = _attn_call(lst, cnt, slopes, main, sel, kv, yc, block_onehot, bsz, seq, chunk)

        y_b = _pool_call(main, pool_w[l].astype(BF16), pool_scale[l], bsz, seq)

        conv_w = jnp.zeros((SUBLANES, 3 * DN_W), F32).at[:DN_CONV].set(dn_conv_w[l])
        lane_row = lambda v: jnp.zeros((1, LANES), F32).at[0, LANE_DECAY:LANE_DECAY + DN_HEADS].set(v)
        y_c = _dn_call(main3, conv_w, lane_row(dn_A_log[l]), lane_row(dn_dt_bias[l]),
                       dn_norm_g[l]).reshape(t, DN_W)

        x2 = _merge_call(y_a, y_b, y_c, mg, x2, gt1,
                         w_branch_nsa[l].astype(BF16), w_branch_pool[l].astype(BF16),
                         w_branch_dn[l].astype(BF16), w_out[l].astype(BF16), seq)
        x2 = _mlp_call(x2, norm2_g[l], sc2, sh2, gt2, mlp_w1_bf, mlp_w2_bf, l,
                       final_g, seq, final=(l == depth - 1))
    return x2.reshape(bsz, seq, d)
```

```python
import functools

import jax
import jax.numpy as jnp
from jax import lax
from jax.experimental import pallas as pl
from jax.experimental.pallas import tpu as pltpu

F32 = jnp.float32
BF16 = jnp.bfloat16
HIGHEST = lax.Precision.HIGHEST

D_MODEL = 1024
NSA_HEADS = 8
NSA_GROUPS = 2
NSA_REP = NSA_HEADS // NSA_GROUPS
NSA_DH = 64
CMP_LEN = 32
CMP_STRIDE = 16
SEL_LEN = 64
SEL_SHIFT = 6
TOP_N = 16
WINDOW = 512
FORCE_BONUS = 1.0e4
POOL_WINDOWS = (2, 4, 8, 16)
POOL_GROUP_DIM = 128
POOL_W = len(POOL_WINDOWS) * POOL_GROUP_DIM
DN_HEADS = 4
DN_DK = 128
DN_W = DN_HEADS * DN_DK
DN_CONV = 4
EPS = 1e-6
NEG = -1e30
NSA_Q_W = NSA_HEADS * NSA_DH
NSA_KV_W = NSA_GROUPS * NSA_DH
IN_SIZES = (NSA_Q_W, NSA_KV_W, NSA_KV_W, NSA_KV_W, NSA_KV_W, NSA_KV_W, NSA_KV_W,
            3 * NSA_HEADS, POOL_W, 3 * DN_W, DN_W, DN_HEADS, DN_HEADS, 3 * D_MODEL)

LANES = 128
LANE_SHIFT = 7
SUBLANES = 8
VMEM_LIMIT_BYTES = 48 * 1024 * 1024

Q_TILE = 128
SELECT_BUCKETS = 4
ATTN_Q_TILE = 128
ATTN_TILES = 2
ATTN_GROUPS = 4
KV_CHUNK = 256
DN_CHUNK = 128
DN_BATCH = 4
PROJ_TM = 1024
MLP_TM = 1024
MLP_TF = 2048
MERGE_TM = 512
POOL_TS = 1024
POOL_HALO = 16
DN_HALO = 8

COL_Q = 0
COL_POOL = 512
COL_Z = 1024
COL_QKV = 1536
COL_GATE = 3072
COL_KC = 3328
COL_VC = 3456
N_MAIN = 3584
PROJ_MAIN_TM = 1024
N_MG = 3072
LANE_BETA = 16
LANE_DECAY = 20
KVB_KS = 0
KVB_VS = 2
KVB_KW = 4
KVB_VW = 6
KVB_END = 8
N_KV = KVB_END * LANES
LANE_POS_HI = NSA_DH
LANE_POS_LO = NSA_DH + 1


def _cparams(sem):
    return pltpu.CompilerParams(dimension_semantics=sem, vmem_limit_bytes=VMEM_LIMIT_BYTES)


def _silu(v):
    return v * jax.nn.sigmoid(v)


def _mm(a, b):
    return jnp.dot(a.astype(BF16), b.astype(BF16), preferred_element_type=F32)


def _mm_nt(a, b):
    return lax.dot_general(a.astype(BF16), b.astype(BF16), (((1,), (1,)), ((), ())), preferred_element_type=F32)


def _mm_tn(a, b):
    return lax.dot_general(a.astype(BF16), b.astype(BF16), (((0,), (0,)), ((), ())), preferred_element_type=F32)


def _mod_kernel(c_ref, w_ref, b_ref, o_ref):
    cond = _silu(c_ref[...])
    o_ref[...] = jnp.dot(cond, w_ref[...], preferred_element_type=F32, precision=HIGHEST) + b_ref[...]


def _mod_call(c_pad, w, b, layer):
    _, d, n = w.shape
    tn = 1536
    return pl.pallas_call(
        _mod_kernel,
        grid=(n // tn,),
        in_specs=[pl.BlockSpec((SUBLANES, d), lambda j: (0, 0)),
                  pl.BlockSpec((None, d, tn), lambda j: (layer, 0, j)),
                  pl.BlockSpec((None, 1, tn), lambda j: (layer, 0, j))],
        out_specs=pl.BlockSpec((SUBLANES, tn), lambda j: (0, j)),
        out_shape=jax.ShapeDtypeStruct((SUBLANES, n), F32),
        compiler_params=_cparams(("parallel",)),
        name="mod",
    )(c_pad, w, b)


def _norm_mod(x, g, sc, sh):
    ms = jnp.mean(x * x, axis=-1, keepdims=True)
    return (x * lax.rsqrt(ms + EPS) * g) * (1.0 + sc) + sh


def _proj_kernel(x_ref, g_ref, sc_ref, sh_ref, w_ref, o_ref):
    h = _norm_mod(x_ref[...], g_ref[...], sc_ref[0], sh_ref[0]).astype(BF16)
    o_ref[...] = jnp.dot(h, w_ref[...], preferred_element_type=F32).astype(o_ref.dtype)


def _proj_kv_kernel(x_ref, g_ref, sc_ref, sh_ref, w_ref, o_ref, *, tiles_per_seq):
    h = _norm_mod(x_ref[...], g_ref[...], sc_ref[0], sh_ref[0]).astype(BF16)
    full = jnp.dot(h, w_ref[...], preferred_element_type=F32)
    o_ref[:, :N_MG] = full[:, :N_MG].astype(o_ref.dtype)
    acc = full[:, N_MG:]
    tm, n = acc.shape
    col = lax.broadcasted_iota(jnp.int32, (1, n), 1)
    blk = jnp.right_shift(col, LANE_SHIFT)
    lane = col & (LANES - 1)
    is_key = ((blk >= KVB_KS) & (blk < KVB_VS)) | ((blk >= KVB_KW) & (blk < KVB_VW))
    is_val = ((blk >= KVB_VS) & (blk < KVB_KW)) | ((blk >= KVB_VW) & (blk < KVB_END))
    ones_row = jnp.where(is_val & (lane >= NSA_DH), 1.0, 0.0)
    hi_row = jnp.where(is_key & (lane == LANE_POS_HI), 1.0, 0.0)
    lo_row = jnp.where(is_key & (lane == LANE_POS_LO), 1.0, 0.0)
    pos = (pl.program_id(0) % tiles_per_seq) * tm + lax.broadcasted_iota(jnp.int32, (tm, 1), 0)
    pos_hi = jnp.right_shift(pos, LANE_SHIFT).astype(F32)
    pos_lo = (pos & (LANES - 1)).astype(F32)
    o_ref[:, N_MG:] = (acc + ones_row + pos_hi * hi_row + pos_lo * lo_row).astype(o_ref.dtype)


def _proj_specs(tm, d, tn, per_b):
    return [pl.BlockSpec((tm, d), lambda i, j: (i, 0)),
            pl.BlockSpec((1, d), lambda i, j: (0, 0)),
            pl.BlockSpec((1, 1, d), lambda i, j: (i // per_b, 0, 0)),
            pl.BlockSpec((1, 1, d), lambda i, j: (i // per_b, 0, 0)),
            pl.BlockSpec((d, tn), lambda i, j: (0, j), pipeline_mode=pl.Buffered(1))]


def _proj_call(x2, g, sc, sh, w, seq, tm, out_dtype, name):
    t, d = x2.shape
    n = w.shape[1]
    tm = min(tm, seq)
    return pl.pallas_call(
        _proj_kernel,
        grid=(t // tm, 1),
        in_specs=_proj_specs(tm, d, n, seq // tm),
        out_specs=pl.BlockSpec((tm, n), lambda i, j: (i, j)),
        out_shape=jax.ShapeDtypeStruct((t, n), out_dtype),
        compiler_params=_cparams(("parallel", "arbitrary")),
        name=name,
    )(x2, g.reshape(1, d), sc, sh, w)


def _proj_kv_call(x2, g, sc, sh, w, seq):
    t, d = x2.shape
    n = w.shape[1]
    tm = min(PROJ_TM, seq)
    return pl.pallas_call(
        functools.partial(_proj_kv_kernel, tiles_per_seq=seq // tm),
        grid=(t // tm, 1),
        in_specs=_proj_specs(tm, d, n, seq // tm),
        out_specs=pl.BlockSpec((tm, n), lambda i, j: (i, j)),
        out_shape=jax.ShapeDtypeStruct((t, n), BF16),
        compiler_params=_cparams(("parallel", "arbitrary")),
        name="proj_kv",
    )(x2, g.reshape(1, d), sc, sh, w)


def _compress_kernel(x_ref, pos_ref, w1_ref, w1g_ref, w2_ref, o_ref):
    nr = x_ref.shape[1] // CMP_STRIDE
    xs = jnp.concatenate([x_ref[0, pl.ds(rho, nr, stride=CMP_STRIDE), :].astype(BF16) for rho in range(CMP_STRIDE)],
                         axis=1)
    posb = jnp.dot(pos_ref[0], w1_ref[0], preferred_element_type=F32)[0:1]
    for g in range(NSA_GROUPS):
        top = jnp.dot(xs, w1g_ref[0, g, 0], preferred_element_type=F32)
        bot = jnp.dot(xs, w1g_ref[0, g, 1], preferred_element_type=F32)
        hid = top + pltpu.roll(bot, nr - 1, 0) + posb
        o_ref[0, 0, g] = _mm(_silu(hid), w2_ref[0]).astype(o_ref.dtype)


def _compress_call(main3, pos, w1, w1g, w2):
    b, seq, _ = main3.shape
    nr = seq // CMP_STRIDE
    two, l64, hid = w1.shape
    return pl.pallas_call(
        _compress_kernel,
        grid=(two, b),
        in_specs=[pl.BlockSpec((1, seq, LANES), lambda a, i: (i, 0, COL_KC // LANES + a)),
                  pl.BlockSpec((1, SUBLANES, l64), lambda a, i: (a, 0, 0)),
                  pl.BlockSpec((1, l64, hid), lambda a, i: (a, 0, 0)),
                  pl.BlockSpec((1,) + w1g.shape[1:], lambda a, i: (a, 0, 0, 0, 0)),
                  pl.BlockSpec((1, hid, LANES), lambda a, i: (a, 0, 0))],
        out_specs=pl.BlockSpec((1, 1, NSA_GROUPS, nr, LANES), lambda a, i: (a, i, 0, 0, 0)),
        out_shape=jax.ShapeDtypeStruct((two, b, NSA_GROUPS, nr, LANES), BF16),
        compiler_params=_cparams(("parallel", "parallel")),
        name="compress",
    )(main3, pos, w1, w1g, w2)


def _gate_cols(gate_logits, branch):
    sg = jax.nn.sigmoid(gate_logits)
    return [sg[:, 3 * r + branch:3 * r + branch + 1] for r in range(NSA_REP)]


def _stack_heads_dup(q):
    qa = q[:, :LANES]
    qb = q[:, LANES:]
    lo = lax.broadcasted_iota(jnp.int32, qa.shape, 1) < NSA_DH
    z = jnp.zeros_like(qa)
    return jnp.concatenate([jnp.where(lo, qa, z), jnp.where(lo, z, qa),
                            jnp.where(lo, qb, z), jnp.where(lo, z, qb)], axis=0)


def _unstack_heads_dup(o, gates):
    nq = o.shape[0] // NSA_REP
    lo = lax.broadcasted_iota(jnp.int32, (nq, LANES), 1) < NSA_DH
    a = jnp.where(lo, gates[0] * o[0:nq], gates[1] * o[nq:2 * nq])
    b = jnp.where(lo, gates[2] * o[2 * nq:3 * nq], gates[3] * o[3 * nq:])
    return jnp.concatenate([a, b], axis=1)


def _stack_heads_pos(q, slopes):
    qa = q[:, :LANES]
    qb = q[:, LANES:]
    lane = lax.broadcasted_iota(jnp.int32, qa.shape, 1)
    lo = lane < NSA_DH
    heads = [qa, pltpu.roll(qa, NSA_DH, 1), qb, pltpu.roll(qb, NSA_DH, 1)]
    out = []
    for r in range(NSA_REP):
        extra = jnp.where(lane == LANE_POS_HI, slopes[r] * LANES, jnp.where(lane == LANE_POS_LO, slopes[r], 0.0))
        out.append(jnp.where(lo, heads[r], extra))
    return jnp.concatenate(out, axis=0)


def _finish_heads(acc, gates):
    nq = acc.shape[0] // NSA_REP
    o = acc * pltpu.roll(1.0 / acc, NSA_DH, 1)
    lo = lax.broadcasted_iota(jnp.int32, (nq, LANES), 1) < NSA_DH
    hs = [gates[r] * o[r * nq:(r + 1) * nq] for r in range(NSA_REP)]
    a = jnp.where(lo, hs[0], pltpu.roll(hs[1], NSA_DH, 1))
    b = jnp.where(lo, hs[2], pltpu.roll(hs[3], NSA_DH, 1))
    return jnp.concatenate([a, b], axis=1)


def _select_kernel(slopes_ref, q_ref, kc_ref, vc_ref, gates_ref, ov_ref, sel_ref, act_ref, yc_ref):
    qt = pl.program_id(1)
    nqt = pl.num_programs(1)
    nb_all, ncp_all = ov_ref.shape
    for k in range(SELECT_BUCKETS):
        @pl.when((qt * SELECT_BUCKETS) // nqt == k)
        def _(k=k):
            _select_prefix(slopes_ref, q_ref, kc_ref, vc_ref, gates_ref, ov_ref, sel_ref, act_ref, yc_ref,
                           nb=(k + 1) * nb_all // SELECT_BUCKETS, ncp=(k + 1) * ncp_all // SELECT_BUCKETS)


def _select_prefix(slopes_ref, q_ref, kc_ref, vc_ref, gates_ref, ov_ref, sel_ref, act_ref, yc_ref, *, nb, ncp):
    t0 = pl.program_id(1) * Q_TILE
    nq = Q_TILE
    nb_all = ov_ref.shape[0]
    c_end = lax.broadcasted_iota(jnp.int32, (nq, ncp), 1) * CMP_STRIDE + (CMP_LEN - 1)
    tq = t0 + lax.broadcasted_iota(jnp.int32, (nq, ncp), 0)
    vis = c_end <= tq
    any_vis = tq[:, 0:1] >= CMP_LEN - 1
    rel = (c_end[0:1] - t0).astype(F32)
    ov = ov_ref[:nb, :ncp]
    jb = lax.broadcasted_iota(jnp.int32, (nb, nq), 0)
    tql = t0 + lax.broadcasted_iota(jnp.int32, (nb, nq), 1)
    valid = jb * SEL_LEN <= tql
    forced = (jb == 0) | (jb == jnp.right_shift(tql, SEL_SHIFT))
    scores = []
    for g in range(NSA_GROUPS):
        qcols = slice(g * NSA_REP * NSA_DH, (g + 1) * NSA_REP * NSA_DH)
        qs = _stack_heads_dup(q_ref[:, qcols] * (NSA_DH ** -0.5)).astype(BF16)
        s = _mm_nt(qs, kc_ref[0, g, :ncp, :])
        psum = jnp.zeros((nq, ncp), F32)
        ps = []
        for r in range(NSA_REP):
            m_r = slopes_ref[g * NSA_REP + r]
            sr = jnp.where(vis, s[r * nq:(r + 1) * nq] + m_r * rel, NEG)
            mx = jnp.max(sr, axis=-1, keepdims=True)
            e = jnp.exp(sr - mx)
            den = jnp.sum(e, axis=-1, keepdims=True)
            p = e * jnp.where(any_vis, 1.0 / den, 0.0)
            psum = psum + p
            ps.append(p.astype(BF16))
        o_c = jnp.dot(jnp.concatenate(ps, axis=0), vc_ref[0, g, :ncp, :], preferred_element_type=F32)
        yc_ref[:, qcols] = _unstack_heads_dup(o_c, _gate_cols(gates_ref[:, g * LANES:(g + 1) * LANES], 0))
        p_hi = psum.astype(BF16)
        p_lo = (psum - p_hi.astype(F32)).astype(BF16)
        imp_t = _mm_nt(ov, p_hi) + _mm_nt(ov, p_lo)
        scores.append(jnp.where(valid, imp_t + FORCE_BONUS * forced.astype(F32), NEG))

    for _ in range(min(TOP_N, nb)):
        for g in range(NSA_GROUPS):
            mx = jnp.max(scores[g], axis=0, keepdims=True)
            idx = jnp.min(jnp.where(scores[g] == mx, jb, nb), axis=0, keepdims=True)
            scores[g] = jnp.where(jb == idx, -jnp.inf, scores[g])
    for g in range(NSA_GROUPS):
        sel_t = jnp.where((scores[g] == -jnp.inf) & valid, 1.0, 0.0)
        if nb < nb_all:
            sel_t = jnp.concatenate([sel_t, jnp.zeros((nb_all - nb, nq), F32)], axis=0)
        sel_q = sel_t.T
        sel_ref[0, g] = sel_q.astype(sel_ref.dtype)
        act_ref[0, g, 0] = jnp.max(sel_q, axis=0, keepdims=True)


def _select_call(slopes, main, kcmp, vcmp, ov, bsz, seq):
    nqt = seq // Q_TILE
    nb, ncp = ov.shape
    row = lambda b, q: b * nqt + q
    gw = NSA_GROUPS * LANES
    return pl.pallas_call(
        _select_kernel,
        grid=(bsz, nqt),
        in_specs=[pl.BlockSpec(memory_space=pltpu.SMEM),
                  pl.BlockSpec((Q_TILE, NSA_Q_W), lambda b, q: (row(b, q), COL_Q // NSA_Q_W)),
                  pl.BlockSpec((1, NSA_GROUPS, ncp, LANES), lambda b, q: (b, 0, 0, 0)),
                  pl.BlockSpec((1, NSA_GROUPS, ncp, LANES), lambda b, q: (b, 0, 0, 0)),
                  pl.BlockSpec((Q_TILE, gw), lambda b, q: (row(b, q), COL_GATE // gw)),
                  pl.BlockSpec((nb, ncp), lambda b, q: (0, 0))],
        out_specs=[pl.BlockSpec((1, NSA_GROUPS, Q_TILE, nb), lambda b, q: (b, 0, q, 0)),
                   pl.BlockSpec((1, NSA_GROUPS, 1, 1, nb), lambda b, q: (b, 0, q, 0, 0)),
                   pl.BlockSpec((Q_TILE, NSA_Q_W), lambda b, q: (row(b, q), 0))],
        out_shape=[jax.ShapeDtypeStruct((bsz, NSA_GROUPS, seq, nb), BF16),
                   jax.ShapeDtypeStruct((bsz, NSA_GROUPS, nqt, 1, nb), F32),
                   jax.ShapeDtypeStruct((bsz * seq, NSA_Q_W), F32)],
        compiler_params=_cparams(("parallel", "parallel")),
        name="select",
    )(slopes, main, kcmp, vcmp, main, ov)


def _attn_kernel(lst_ref, cnt_ref, slopes_ref, q_ref, sel_ref, ks_ref, vs_ref, kw_ref, vw_ref, gates_ref, yc_ref,
                 et_ref, o_ref, s_scr, p_scr, m_scr, a_scr, acc_scr, *, chunk):
    b = pl.program_id(0)
    g = pl.program_id(1)
    step = pl.program_id(2)
    tiles_per_step = ATTN_TILES * ATTN_GROUPS
    nqt = pl.num_programs(2) * tiles_per_step
    nq = ATTN_Q_TILE
    nw = WINDOW + nq
    nch_total = ks_ref.shape[0] // chunk
    slopes = [slopes_ref[g * NSA_REP + r] for r in range(NSA_REP)]

    def values(p, c):
        start = pl.multiple_of(c * chunk, chunk)
        return jnp.dot(p, vs_ref[pl.ds(start, chunk), :], preferred_element_type=F32)

    def row_max(s):
        return jnp.broadcast_to(jnp.max(s, axis=-1, keepdims=True), (s.shape[0], LANES))

    def lane_tile(m, width):
        return jnp.concatenate([m] * (width // LANES), axis=1)

    def masked(s, ok):
        return jnp.concatenate([jnp.where(ok, s[r * nq:(r + 1) * nq], NEG) for r in range(NSA_REP)], axis=0)

    class Group:
        pass

    def setup(k):
        gr = Group()
        gr.tiles = range(ATTN_TILES)
        gr.scr = [(k % 2) * ATTN_TILES + u for u in gr.tiles]
        local = [k * ATTN_TILES + u for u in gr.tiles]
        gr.rws = [slice(v * nq, (v + 1) * nq) for v in local]
        qts = [step * tiles_per_step + v for v in local]
        gr.t0s = [qt * nq for qt in qts]
        gr.qs = [_stack_heads_pos(q_ref[rw, :] * (NSA_DH ** -0.5), slopes).astype(BF16) for rw in gr.rws]
        pens = [((sel_ref[0, 0, rw, :].astype(F32) - 1.0) * (-NEG)).astype(BF16) for rw in gr.rws]
        gr.qx = [jnp.concatenate([gr.qs[u], jnp.concatenate([pens[u]] * NSA_REP, axis=0)], axis=1) for u in gr.tiles]
        gr.bases = [((b * NSA_GROUPS + g) * nqt + qt) * nch_total for qt in qts]
        gr.cnts = [cnt_ref[(b * NSA_GROUPS + g) * nqt + qt] for qt in qts]
        for u in gr.tiles:
            su = gr.scr[u]
            m_scr[su] = jnp.full(m_scr.shape[1:], NEG, F32)
            a_scr[su] = jnp.ones(a_scr.shape[1:], F32)
            acc_scr[su] = jnp.zeros(acc_scr.shape[1:], F32)
            p_scr[su, 1] = jnp.zeros(p_scr.shape[2:], BF16)
        for u in gr.tiles:
            s_scr[gr.scr[u], 0] = logits(gr, u, lst_ref[gr.bases[u]])
        return gr

    def logits(gr, u, c):
        start = pl.multiple_of(c * chunk, chunk)
        kx = jnp.concatenate([ks_ref[pl.ds(start, chunk), :], et_ref[pl.ds(start, chunk), :]], axis=1)
        return _mm_nt(gr.qx[u], kx)

    def flash_step(gr, us, i, slot):
        for u in us:
            su = gr.scr[u]
            acc_scr[su] = a_scr[su] * acc_scr[su] + values(p_scr[su, 1 - slot],
                                                           lst_ref[gr.bases[u] + jnp.maximum(i - 1, 0)])
        for u in us:
            su = gr.scr[u]
            s = s_scr[su, slot]
            m_prev = m_scr[su]
            m_new = jnp.maximum(m_prev, row_max(s))
            a_scr[su] = jnp.exp(m_prev - m_new)
            p_scr[su, slot] = jnp.exp(s - lane_tile(m_new, chunk)).astype(BF16)
            m_scr[su] = m_new
        for u in us:
            s_scr[gr.scr[u], 1 - slot] = logits(gr, u, lst_ref[gr.bases[u] + i + 1])

    def flash_loops(gr):
        def single_steps(us):
            def body(i, carry):
                flash_step(gr, us, i, i & 1)
                return carry
            return body

        def double_steps(us):
            def body(j, carry):
                flash_step(gr, us, 2 * j, 0)
                flash_step(gr, us, 2 * j + 1, 1)
                return carry
            return body

        joint = functools.reduce(jnp.minimum, gr.cnts)
        lax.fori_loop(0, joint // 2, double_steps(list(gr.tiles)), 0)
        lax.fori_loop(joint - (joint & 1), joint, single_steps(list(gr.tiles)), 0)
        for u in gr.tiles:
            lax.fori_loop(joint, gr.cnts[u], single_steps([u]), 0)

    def finish(gr):
        tiles = gr.tiles
        scr = gr.scr
        slots = [gr.cnts[u] & 1 for u in tiles]
        c_diag = [gr.t0s[u] // chunk for u in tiles]
        acc = [a_scr[scr[u]] * acc_scr[scr[u]]
               + values(p_scr[scr[u], 1 - slots[u]], lst_ref[gr.bases[u] + jnp.maximum(gr.cnts[u] - 1, 0)])
               for u in tiles]
        wbase = [pl.multiple_of(jnp.maximum(gr.t0s[u] - WINDOW, 0), nq) for u in tiles]
        s_win = [_mm_nt(gr.qs[u], kw_ref[pl.ds(wbase[u], nw), :]) for u in tiles]
        s_diag = []
        for u in tiles:
            pos = c_diag[u] * chunk + lax.broadcasted_iota(jnp.int32, (nq, chunk), 1)
            tq = gr.t0s[u] + lax.broadcasted_iota(jnp.int32, (nq, chunk), 0)
            s_diag.append(masked(s_scr[scr[u], slots[u]], pos <= tq))
        m_prev = [m_scr[scr[u]] for u in tiles]
        m_new = [jnp.maximum(m_prev[u], row_max(s_diag[u])) for u in tiles]
        acc = [jnp.exp(m_prev[u] - m_new[u]) * acc[u]
               + values(jnp.exp(s_diag[u] - lane_tile(m_new[u], chunk)).astype(BF16), c_diag[u]) for u in tiles]
        p_win = []
        for u in tiles:
            pos = wbase[u] + lax.broadcasted_iota(jnp.int32, (nq, nw), 1)
            tq = gr.t0s[u] + lax.broadcasted_iota(jnp.int32, (nq, nw), 0)
            s = masked(s_win[u], (pos <= tq) & (pos > tq - WINDOW))
            p_win.append(jnp.exp(s - jnp.max(s, axis=-1, keepdims=True)).astype(BF16))
        acc_w = [jnp.dot(p_win[u], vw_ref[pl.ds(wbase[u], nw), :], preferred_element_type=F32) for u in tiles]
        for u in tiles:
            gates = gates_ref[gr.rws[u], :]
            o_ref[gr.rws[u], :] = (yc_ref[gr.rws[u], :] + _finish_heads(acc[u], _gate_cols(gates, 1))
                                   + _finish_heads(acc_w[u], _gate_cols(gates, 2))).astype(o_ref.dtype)

    cur = setup(0)
    for k in range(ATTN_GROUPS):
        flash_loops(cur)
        nxt = setup(k + 1) if k + 1 < ATTN_GROUPS else None
        finish(cur)
        cur = nxt


def _chunk_lists(act, seq, chunk):
    bsz, ngrp = act.shape[:2]
    nqt = seq // ATTN_Q_TILE
    nch = seq // chunk
    active = act.reshape(bsz, ngrp, nqt, ATTN_Q_TILE // Q_TILE, nch, chunk // SEL_LEN).max(axis=(3, 5)) > 0.0
    c_diag = (jnp.arange(nqt) * ATTN_Q_TILE) // chunk
    active = active & (jnp.arange(nch)[None, :] < c_diag[:, None])
    cnt = active.sum(axis=-1).astype(jnp.int32)
    order = jnp.argsort(jnp.logical_not(active), axis=-1, stable=True).astype(jnp.int32)
    lst = jnp.where(jnp.arange(nch) < cnt[..., None], order, c_diag[:, None].astype(jnp.int32))
    return lst.reshape(-1), cnt.reshape(-1)


def _attn_call(lst, cnt, slopes, main, sel, kv, yc, et, bsz, seq, chunk):
    tq = ATTN_Q_TILE * ATTN_TILES * ATTN_GROUPS
    nset = 2 * ATTN_TILES
    nqt = seq // tq
    nb = sel.shape[-1]
    rows = NSA_REP * ATTN_Q_TILE
    row = lambda b, g, q: b * nqt + q
    kvspec = lambda blk: pl.BlockSpec((seq, LANES), lambda b, g, q, *_: (b, N_MG // LANES + blk + g))
    grid_spec = pltpu.PrefetchScalarGridSpec(
        num_scalar_prefetch=2,
        grid=(bsz, NSA_GROUPS, nqt),
        in_specs=[pl.BlockSpec(memory_space=pltpu.SMEM),
                  pl.BlockSpec((tq, 256), lambda b, g, q, *_: (row(b, g, q), COL_Q // 256 + g)),
                  pl.BlockSpec((1, 1, tq, nb), lambda b, g, q, *_: (b, g, q, 0)),
                  kvspec(KVB_KS), kvspec(KVB_VS), kvspec(KVB_KW), kvspec(KVB_VW),
                  pl.BlockSpec((tq, LANES), lambda b, g, q, *_: (row(b, g, q), COL_GATE // LANES + g)),
                  pl.BlockSpec((tq, 256), lambda b, g, q, *_: (row(b, g, q), g)),
                  pl.BlockSpec(et.shape, lambda b, g, q, *_: (0, 0))],
        out_specs=pl.BlockSpec((tq, 256), lambda b, g, q, *_: (row(b, g, q), g)),
        scratch_shapes=[pltpu.VMEM((nset, 2, rows, chunk), F32),
                        pltpu.VMEM((nset, 2, rows, chunk), BF16),
                        pltpu.VMEM((nset, rows, LANES), F32),
                        pltpu.VMEM((nset, rows, LANES), F32),
                        pltpu.VMEM((nset, rows, LANES), F32)],
    )
    return pl.pallas_call(
        functools.partial(_attn_kernel, chunk=chunk),
        grid_spec=grid_spec,
        out_shape=jax.ShapeDtypeStruct((bsz * seq, NSA_Q_W), BF16),
        compiler_params=_cparams(("parallel", "parallel", "parallel")),
        name="attn",
    )(lst, cnt, slopes, main, sel, kv, kv, kv, kv, main, yc, et)


def _pool_kernel(x_ref, halo_ref, w_ref, sc_ref, o_ref):
    s = pl.program_id(1)
    ts = x_ref.shape[0]
    x = x_ref[...]
    halo = jnp.where(s > 0, halo_ref[...], 0.0)
    t = s * ts + lax.broadcasted_iota(jnp.int32, (ts, 1), 0)
    for gi, win in enumerate(POOL_WINDOWS):
        lanes = slice(gi * POOL_GROUP_DIM, (gi + 1) * POOL_GROUP_DIM)
        ext = jnp.concatenate([halo[:, lanes], x[:, lanes]], axis=0)
        acc = ext
        span = 1
        while span < win:
            acc = acc + pltpu.roll(acc, span, 0)
            span *= 2
        cnt = jnp.minimum(t + 1, win).astype(F32)
        y = acc[POOL_HALO:] / cnt - x[:, lanes]
        o_ref[:, lanes] = (_mm(y, w_ref[gi]) * sc_ref[:, lanes]).astype(o_ref.dtype)


def _pool_call(main, w, scale, bsz, seq):
    ts = min(POOL_TS, seq)
    per_b = seq // ts
    halo_per_tile = ts // POOL_HALO
    return pl.pallas_call(
        _pool_kernel,
        grid=(bsz, per_b),
        in_specs=[pl.BlockSpec((ts, POOL_W), lambda b, s: (b * per_b + s, COL_POOL // POOL_W)),
                  pl.BlockSpec((POOL_HALO, POOL_W),
                               lambda b, s: (jnp.maximum((b * per_b + s) * halo_per_tile - 1, 0), COL_POOL // POOL_W)),
                  pl.BlockSpec(w.shape, lambda b, s: (0, 0, 0)),
                  pl.BlockSpec((1, POOL_W), lambda b, s: (0, 0))],
        out_specs=pl.BlockSpec((ts, POOL_W), lambda b, s: (b * per_b + s, 0)),
        out_shape=jax.ShapeDtypeStruct((bsz * seq, POOL_W), BF16),
        compiler_params=_cparams(("parallel", "parallel")),
        name="pool",
    )(main, main, w, scale.reshape(1, POOL_W))


def _unit_lower_inverses(nms, ri, ci):
    c = nms[0].shape[0]
    eye = jnp.where(ri == ci, 1.0, 0.0)
    in8 = jnp.right_shift(ri, 3) == jnp.right_shift(ci, 3)
    n8 = [jnp.where(in8, nm, 0.0) for nm in nms]
    ps = [eye - a for a in n8]
    ms = [_mm(a, a) for a in n8]
    ps = [p + _mm(p, m) for p, m in zip(ps, ms)]
    ms = [_mm(m, m) for m in ms]
    ps = [p + _mm(p, m) for p, m in zip(ps, ms)]
    shift = 3
    while (1 << shift) < c:
        rb = jnp.right_shift(ri, shift)
        cb = jnp.right_shift(ci, shift)
        lower_left = ((rb & 1) == 1) & (cb == rb - 1)
        ts = [_mm(p, jnp.where(lower_left, nm, 0.0)) for p, nm in zip(ps, nms)]
        ps = [p - _mm(t, p) for p, t in zip(ps, ts)]
        shift += 1
    return ps


def _row_cumsum(x, ri):
    span = 1
    while span < x.shape[0]:
        x = x + jnp.where(ri >= span, pltpu.roll(x, span, 0), 0.0)
        span *= 2
    return x


def _dn_kernel(qkv_ref, halo_ref, z_ref, gba_ref, cw_ref, alog_ref, dtb_ref, ng_ref, o_ref, st_scr):
    s = pl.program_id(1)
    nbatch, c = qkv_ref.shape[0], qkv_ref.shape[1]

    @pl.when(s == 0)
    def _():
        st_scr[...] = jnp.zeros(st_scr.shape, F32)

    ri = lax.broadcasted_iota(jnp.int32, (c, c), 0)
    ci = lax.broadcasted_iota(jnp.int32, (c, c), 1)
    tril = ri >= ci
    strict = ri > ci
    cw = cw_ref[...]
    ng = ng_ref[...]
    chains = [(bi, h) for bi in range(nbatch) for h in range(DN_HEADS)]
    qn, kn, kb, vb, gcb, lmask = [], [], [], [], [], []
    for bi in range(nbatch):
        halo = jnp.where(s > 0, halo_ref[bi], 0.0)
        ext = jnp.concatenate([halo, qkv_ref[bi]], axis=0)
        conv = cw[DN_CONV - 1:DN_CONV] * ext
        for tap in range(1, DN_CONV):
            conv = conv + cw[DN_CONV - 1 - tap:DN_CONV - tap] * pltpu.roll(ext, tap, 0)
        act = _silu(conv[DN_HALO:])

        gba = gba_ref[bi]
        beta_all = jax.nn.sigmoid(gba)
        xs = gba + dtb_ref[...]
        softplus = jnp.maximum(xs, 0.0) + jnp.log(1.0 + jnp.exp(-jnp.abs(xs)))
        gc_all = _row_cumsum(-jnp.exp(alog_ref[...]) * softplus, ri)
        gct_all = gc_all.T
        for h in range(DN_HEADS):
            q = act[:, h * DN_DK:(h + 1) * DN_DK]
            k = act[:, DN_W + h * DN_DK:DN_W + (h + 1) * DN_DK]
            v = act[:, 2 * DN_W + h * DN_DK:2 * DN_W + (h + 1) * DN_DK]
            qn.append(q * lax.rsqrt(jnp.sum(q * q, axis=-1, keepdims=True) + EPS) * (DN_DK ** -0.5))
            kn.append(k * lax.rsqrt(jnp.sum(k * k, axis=-1, keepdims=True) + EPS))
            beta = beta_all[:, LANE_BETA + h:LANE_BETA + h + 1]
            gcb.append(jnp.broadcast_to(gc_all[:, LANE_DECAY + h:LANE_DECAY + h + 1], (c, DN_DK)))
            gct = jnp.broadcast_to(gct_all[LANE_DECAY + h:LANE_DECAY + h + 1, :], (c, c))
            lmask.append(jnp.where(tril, jnp.exp(jnp.where(tril, gcb[-1] - gct, 0.0)), 0.0))
            kb.append(kn[-1] * beta)
            vb.append(v * beta)

    n = range(len(chains))
    kk = [_mm_nt(kb[i], kn[i]) for i in n]
    tinv = _unit_lower_inverses([jnp.where(strict, kk[i] * lmask[i], 0.0) for i in n], ri, ci)
    egc = [jnp.exp(gcb[i]) for i in n]
    sol = [_mm(tinv[i], jnp.concatenate([vb[i], kb[i] * egc[i]], axis=1)) for i in n]
    aqk = [_mm_nt(qn[i], kn[i]) * lmask[i] for i in n]
    state = [st_scr[i] for i in n]
    glast = [gcb[i][c - 1:c, :] for i in n]
    v_new = [sol[i][:, :DN_DK] - _mm(sol[i][:, DN_DK:], state[i]) for i in n]
    o_state = [_mm(qn[i] * egc[i], state[i]) for i in n]
    o = [o_state[i] + _mm(aqk[i], v_new[i]) for i in n]
    upd = [_mm_tn(kn[i] * jnp.exp(glast[i] - gcb[i]), v_new[i]) for i in n]
    for i, (bi, h) in enumerate(chains):
        st_scr[i] = state[i] * jnp.exp(glast[i]) + upd[i]
        on = o[i] * lax.rsqrt(jnp.mean(o[i] * o[i], axis=-1, keepdims=True) + EPS) * ng
        o_ref[bi, :, h * DN_DK:(h + 1) * DN_DK] = (
            on * _silu(z_ref[bi, :, h * DN_DK:(h + 1) * DN_DK])).astype(o_ref.dtype)


def _dn_call(main3, conv_w, alog_row, dtb_row, norm_g):
    bsz, seq, _ = main3.shape
    c = DN_CHUNK
    nbt = DN_BATCH if bsz % DN_BATCH == 0 else 1
    halo_per_chunk = c // DN_HALO
    w3 = 3 * DN_W
    return pl.pallas_call(
        _dn_kernel,
        grid=(bsz // nbt, seq // c),
        in_specs=[pl.BlockSpec((nbt, c, w3), lambda b, s: (b, s, COL_QKV // w3)),
                  pl.BlockSpec((nbt, DN_HALO, w3),
                               lambda b, s: (b, jnp.maximum(s * halo_per_chunk - 1, 0), COL_QKV // w3)),
                  pl.BlockSpec((nbt, c, DN_W), lambda b, s: (b, s, COL_Z // DN_W)),
                  pl.BlockSpec((nbt, c, LANES), lambda b, s: (b, s, COL_GATE // LANES)),
                  pl.BlockSpec((SUBLANES, w3), lambda b, s: (0, 0)),
                  pl.BlockSpec((1, LANES), lambda b, s: (0, 0)),
                  pl.BlockSpec((1, LANES), lambda b, s: (0, 0)),
                  pl.BlockSpec((1, DN_DK), lambda b, s: (0, 0))],
        out_specs=pl.BlockSpec((nbt, c, DN_W), lambda b, s: (b, s, 0)),
        out_shape=jax.ShapeDtypeStruct((bsz, seq, DN_W), BF16),
        scratch_shapes=[pltpu.VMEM((nbt * DN_HEADS, DN_DK, DN_DK), F32)],
        compiler_params=_cparams(("parallel", "arbitrary")),
        name="dn",
    )(main3, main3, main3, main3, conv_w, alog_row, dtb_row, norm_g.reshape(1, DN_DK))


def _merge_kernel(ya_ref, yb_ref, yc_ref, mg_ref, x_ref, gt_ref, wa_ref, wb_ref, wc_ref, wo_ref, o_ref):
    d = x_ref.shape[1]
    gates = jax.nn.sigmoid(mg_ref[...].astype(F32))
    merged = (gates[:, :d] * jnp.dot(ya_ref[...], wa_ref[...], preferred_element_type=F32)
              + gates[:, d:2 * d] * jnp.dot(yb_ref[...], wb_ref[...], preferred_element_type=F32)
              + gates[:, 2 * d:] * jnp.dot(yc_ref[...], wc_ref[...], preferred_element_type=F32))
    y = jnp.dot(merged.astype(BF16), wo_ref[...], preferred_element_type=F32)
    o_ref[...] = x_ref[...] + gt_ref[0] * y


def _merge_call(ya, yb, yc, mg, x2, gt, wa, wb, wc, wo, seq):
    t, d = x2.shape
    tm = min(MERGE_TM, seq)
    per_b = seq // tm
    full = lambda a: pl.BlockSpec(a.shape, lambda i: (0, 0))
    return pl.pallas_call(
        _merge_kernel,
        grid=(t // tm,),
        in_specs=[pl.BlockSpec((tm, ya.shape[1]), lambda i: (i, 0)),
                  pl.BlockSpec((tm, yb.shape[1]), lambda i: (i, 0)),
                  pl.BlockSpec((tm, yc.shape[1]), lambda i: (i, 0)),
                  pl.BlockSpec((tm, N_MG), lambda i: (i, 0)),
                  pl.BlockSpec((tm, d), lambda i: (i, 0)),
                  pl.BlockSpec((1, 1, d), lambda i: (i // per_b, 0, 0)),
                  full(wa), full(wb), full(wc), full(wo)],
        out_specs=pl.BlockSpec((tm, d), lambda i: (i, 0)),
        out_shape=jax.ShapeDtypeStruct((t, d), F32),
        compiler_params=_cparams(("parallel",)),
        name="merge",
    )(ya, yb, yc, mg, x2, gt, wa, wb, wc, wo)


def _mlp_kernel(x_ref, g_ref, sc_ref, sh_ref, gt_ref, w1_ref, w2_ref, fg_ref, o_ref, *, final):
    x = x_ref[...]
    h = _norm_mod(x, g_ref[...], sc_ref[0], sh_ref[0]).astype(BF16)
    acc = None
    for c0 in range(0, w1_ref.shape[1], MLP_TF):
        a = jnp.maximum(jnp.dot(h, w1_ref[:, c0:c0 + MLP_TF], preferred_element_type=F32), 0.0)
        part = jnp.dot((a * a).astype(BF16), w2_ref[c0:c0 + MLP_TF, :], preferred_element_type=F32)
        acc = part if acc is None else acc + part
    y = x + gt_ref[0] * acc
    if final:
        ms = jnp.mean(y * y, axis=-1, keepdims=True)
        y = y * lax.rsqrt(ms + EPS) * fg_ref[...]
    o_ref[...] = y


def _mlp_call(x2, g, sc, sh, gt, w1, w2, layer, fg, seq, final):
    t, d = x2.shape
    f = w1.shape[2]
    assert f % MLP_TF == 0
    tm = min(MLP_TM, seq)
    per_b = seq // tm
    mod = lambda: pl.BlockSpec((1, 1, d), lambda i: (i // per_b, 0, 0))
    resident = pl.Buffered(1)
    return pl.pallas_call(
        functools.partial(_mlp_kernel, final=final),
        grid=(t // tm,),
        in_specs=[pl.BlockSpec((tm, d), lambda i: (i, 0)),
                  pl.BlockSpec((1, d), lambda i: (0, 0)),
                  mod(), mod(), mod(),
                  pl.BlockSpec((None, d, f), lambda i: (layer, 0, 0), pipeline_mode=resident),
                  pl.BlockSpec((None, f, d), lambda i: (layer, 0, 0), pipeline_mode=resident),
                  pl.BlockSpec((1, d), lambda i: (0, 0))],
        out_specs=pl.BlockSpec((tm, d), lambda i: (i, 0)),
        out_shape=jax.ShapeDtypeStruct((t, d), F32),
        compiler_params=_cparams(("parallel",)),
        name="mlp",
    )(x2, g.reshape(1, d), sc, sh, gt, w1, w2, fg.reshape(1, d))


def _split_w_in(w_in):
    offs = [0]
    for sz in IN_SIZES:
        offs.append(offs[-1] + sz)
    return [w_in[:, offs[i]:offs[i + 1]] for i in range(len(IN_SIZES))]


def _layout_w_in(w_in):
    nq, nkc, nvc, nks, nvs, nkw, nvw, ngate, pin, dqkv, dz, dbeta, da, mg = _split_w_in(w_in)
    d = w_in.shape[0]
    per_g = 3 * NSA_REP
    zeros = lambda n: jnp.zeros((d, n), w_in.dtype)
    gate0 = jnp.concatenate([ngate[:, :per_g], zeros(LANE_BETA - per_g), dbeta, da,
                             zeros(LANES - LANE_DECAY - DN_HEADS)], axis=1)
    gate1 = jnp.concatenate([ngate[:, per_g:], zeros(LANES - per_g)], axis=1)
    main = jnp.concatenate([nq, pin, dz, dqkv, gate0, gate1, nkc, nvc], axis=1)

    def per_group(w):
        pad = zeros(LANES - NSA_DH)
        return jnp.concatenate([w[:, :NSA_DH], pad, w[:, NSA_DH:], pad], axis=1)

    kv = jnp.concatenate([per_group(nks), per_group(nvs), per_group(nkw), per_group(nvw)], axis=1)
    return main.astype(BF16), jnp.concatenate([mg, kv], axis=1).astype(BF16)


def _group_lane_w1(w1):
    two, _, hid = w1.shape
    halves = w1.reshape(two, 2, CMP_STRIDE, NSA_DH, hid)
    out = []
    for g in range(NSA_GROUPS):
        padded = jnp.zeros((two, 2, CMP_STRIDE, LANES, hid), w1.dtype).at[:, :, :, g * NSA_DH:(g + 1) * NSA_DH].set(halves)
        out.append(padded.reshape(two, 2, CMP_STRIDE * LANES, hid))
    return jnp.stack(out, axis=1)


def _nsa_constants(seq):
    nb = seq // SEL_LEN
    ncp = seq // CMP_STRIDE
    j = jnp.arange(nb)[:, None] * SEL_LEN
    i = jnp.arange(ncp)[None, :] * CMP_STRIDE
    overlap = jnp.clip(jnp.minimum(j + SEL_LEN, i + CMP_LEN) - jnp.maximum(j, i), 0) // CMP_STRIDE
    block_onehot = (jnp.arange(seq)[:, None] // SEL_LEN) == jnp.arange(nb)[None, :]
    return overlap.astype(BF16), block_onehot.astype(BF16)


def kernel(x, c, ada_w, ada_b, norm1_g, norm2_g, w_in, phi_k1, phi_k2, phi_v1, phi_v2, pos_k, pos_v, pool_w, pool_scale, dn_conv_w, dn_A_log, dn_dt_bias, dn_norm_g, w_branch_nsa, w_branch_pool, w_branch_dn, w_out, mlp_w1, mlp_w2, final_g):
    bsz, seq, d = x.shape
    assert d == D_MODEL and seq % max(PROJ_TM, MLP_TM, POOL_TS) == 0 and seq % KV_CHUNK == 0
    assert seq % (ATTN_Q_TILE * ATTN_TILES * ATTN_GROUPS) == 0 and (seq // Q_TILE) % SELECT_BUCKETS == 0
    assert DN_CHUNK == DN_DK == LANES
    t = bsz * seq
    depth = ada_w.shape[0]
    chunk = KV_CHUNK

    slopes = 2.0 ** (-(8.0 / NSA_HEADS) * (jnp.arange(NSA_HEADS, dtype=F32) + 1.0))
    overlap, block_onehot = _nsa_constants(seq)
    c_pad = jnp.zeros((SUBLANES, d), F32).at[:bsz].set(c)
    ada_b3 = ada_b.reshape(depth, 1, ada_b.shape[1])
    mlp_w1_bf, mlp_w2_bf = mlp_w1.astype(BF16), mlp_w2.astype(BF16)
    x2 = x.reshape(t, d)

    for l in range(depth):
        mod = _mod_call(c_pad, ada_w, ada_b3, l)[:bsz]
        sh1, sc1, gt1, sh2, sc2, gt2 = [m.reshape(bsz, 1, d) for m in jnp.split(mod, 6, axis=-1)]

        w_main, w_mgkv = _layout_w_in(w_in[l])
        main = _proj_call(x2, norm1_g[l], sc1, sh1, w_main, seq, PROJ_MAIN_TM, F32, "proj_main")
        kv = mg = _proj_kv_call(x2, norm1_g[l], sc1, sh1, w_mgkv, seq)

        main3 = main.reshape(bsz, seq, N_MAIN)
        pos = jnp.zeros((2, SUBLANES, CMP_LEN * NSA_DH), F32).at[:, 0].set(
            jnp.stack([pos_k[l].reshape(-1), pos_v[l].reshape(-1)])).astype(BF16)
        w1 = jnp.stack([phi_k1[l], phi_v1[l]]).astype(BF16)
        w2 = jnp.stack([phi_k2[l], phi_v2[l]])
        w2 = jnp.concatenate([w2, w2], axis=-1).astype(BF16)
        cmp_kv = _compress_call(main3, pos, w1, _group_lane_w1(w1), w2)

        sel, act, yc = _select_call(slopes, main, cmp_kv[0], cmp_kv[1], overlap, bsz, seq)
        lst, cnt = _chunk_lists(act, seq, chunk)
        y_a = _attn_call(lst, cnt, slopes, main, sel, kv, yc, block_onehot, bsz, seq, chunk)

        y_b = _pool_call(main, pool_w[l].astype(BF16), pool_scale[l], bsz, seq)

        conv_w = jnp.zeros((SUBLANES, 3 * DN_W), F32).at[:DN_CONV].set(dn_conv_w[l])
        lane_row = lambda v: jnp.zeros((1, LANES), F32).at[0, LANE_DECAY:LANE_DECAY + DN_HEADS].set(v)
        y_c = _dn_call(main3, conv_w, lane_row(dn_A_log[l]), lane_row(dn_dt_bias[l]),
                       dn_norm_g[l]).reshape(t, DN_W)

        x2 = _merge_call(y_a, y_b, y_c, mg, x2, gt1,
                         w_branch_nsa[l].astype(BF16), w_branch_pool[l].astype(BF16),
                         w_branch_dn[l].astype(BF16), w_out[l].astype(BF16), seq)
        x2 = _mlp_call(x2, norm2_g[l], sc2, sh2, gt2, mlp_w1_bf, mlp_w2_bf, l,
                       final_g, seq, final=(l == depth - 1))
    return x2.reshape(bsz, seq, d)
```

```python
import functools

import jax
import jax.numpy as jnp
from jax import lax
from jax.experimental import pallas as pl
from jax.experimental.pallas import tpu as pltpu

F32 = jnp.float32
BF16 = jnp.bfloat16
HIGHEST = lax.Precision.HIGHEST

D_MODEL = 1024
NSA_HEADS = 8
NSA_GROUPS = 2
NSA_REP = NSA_HEADS // NSA_GROUPS
NSA_DH = 64
CMP_LEN = 32
CMP_STRIDE = 16
SEL_LEN = 64
SEL_SHIFT = 6
TOP_N = 16
WINDOW = 512
FORCE_BONUS = 1.0e4
POOL_WINDOWS = (2, 4, 8, 16)
POOL_GROUP_DIM = 128
POOL_W = len(POOL_WINDOWS) * POOL_GROUP_DIM
DN_HEADS = 4
DN_DK = 128
DN_W = DN_HEADS * DN_DK
DN_CONV = 4
EPS = 1e-6
NEG = -1e30
NSA_Q_W = NSA_HEADS * NSA_DH
NSA_KV_W = NSA_GROUPS * NSA_DH
IN_SIZES = (NSA_Q_W, NSA_KV_W, NSA_KV_W, NSA_KV_W, NSA_KV_W, NSA_KV_W, NSA_KV_W,
            3 * NSA_HEADS, POOL_W, 3 * DN_W, DN_W, DN_HEADS, DN_HEADS, 3 * D_MODEL)

LANES = 128
LANE_SHIFT = 7
SUBLANES = 8
VMEM_LIMIT_BYTES = 48 * 1024 * 1024

Q_TILE = 128
SELECT_BUCKETS = 4
SELECT_TILES = 4
ATTN_Q_TILE = 128
ATTN_TILES = 2
ATTN_GROUPS = 4
KV_CHUNK = 256
DN_CHUNK = 128
DN_BATCH = 4
PROJ_TM = 1024
MLP_TM = 1024
MLP_TF = 2048
MERGE_TM = 512
POOL_TS = 1024
POOL_HALO = 16
DN_HALO = 8

COL_Q = 0
COL_POOL = 512
COL_Z = 1024
COL_QKV = 1536
COL_GATE = 3072
COL_KC = 3328
COL_VC = 3456
N_MAIN = 3584
PROJ_MAIN_TM = 1024
N_MG = 3072
LANE_BETA = 16
LANE_DECAY = 20
KVB_KS = 0
KVB_VS = 2
KVB_KW = 4
KVB_VW = 6
KVB_END = 8
N_KV = KVB_END * LANES
LANE_POS_HI = NSA_DH
LANE_POS_LO = NSA_DH + 1


def _cparams(sem):
    return pltpu.CompilerParams(dimension_semantics=sem, vmem_limit_bytes=VMEM_LIMIT_BYTES)


def _silu(v):
    return v * jax.nn.sigmoid(v)


def _mm(a, b):
    return jnp.dot(a.astype(BF16), b.astype(BF16), preferred_element_type=F32)


def _mm_nt(a, b):
    return lax.dot_general(a.astype(BF16), b.astype(BF16), (((1,), (1,)), ((), ())), preferred_element_type=F32)


def _mm_tn(a, b):
    return lax.dot_general(a.astype(BF16), b.astype(BF16), (((0,), (0,)), ((), ())), preferred_element_type=F32)


def _mod_kernel(c_ref, w_ref, b_ref, o_ref):
    cond = _silu(c_ref[...])
    o_ref[...] = jnp.dot(cond, w_ref[...], preferred_element_type=F32, precision=HIGHEST) + b_ref[...]


def _mod_call(c_pad, w, b, layer):
    _, d, n = w.shape
    tn = 1536
    return pl.pallas_call(
        _mod_kernel,
        grid=(n // tn,),
        in_specs=[pl.BlockSpec((SUBLANES, d), lambda j: (0, 0)),
                  pl.BlockSpec((None, d, tn), lambda j: (layer, 0, j)),
                  pl.BlockSpec((None, 1, tn), lambda j: (layer, 0, j))],
        out_specs=pl.BlockSpec((SUBLANES, tn), lambda j: (0, j)),
        out_shape=jax.ShapeDtypeStruct((SUBLANES, n), F32),
        compiler_params=_cparams(("parallel",)),
        name="mod",
    )(c_pad, w, b)


def _norm_mod(x, g, sc, sh):
    ms = jnp.mean(x * x, axis=-1, keepdims=True)
    return (x * lax.rsqrt(ms + EPS) * g) * (1.0 + sc) + sh


def _proj_kernel(x_ref, g_ref, sc_ref, sh_ref, w_ref, o_ref):
    h = _norm_mod(x_ref[...], g_ref[...], sc_ref[0], sh_ref[0]).astype(BF16)
    o_ref[...] = jnp.dot(h, w_ref[...], preferred_element_type=F32).astype(o_ref.dtype)


def _proj_kv_kernel(x_ref, g_ref, sc_ref, sh_ref, w_ref, o_ref, *, tiles_per_seq):
    h = _norm_mod(x_ref[...], g_ref[...], sc_ref[0], sh_ref[0]).astype(BF16)
    full = jnp.dot(h, w_ref[...], preferred_element_type=F32)
    o_ref[:, :N_MG] = full[:, :N_MG].astype(o_ref.dtype)
    acc = full[:, N_MG:]
    tm, n = acc.shape
    col = lax.broadcasted_iota(jnp.int32, (1, n), 1)
    blk = jnp.right_shift(col, LANE_SHIFT)
    lane = col & (LANES - 1)
    is_key = ((blk >= KVB_KS) & (blk < KVB_VS)) | ((blk >= KVB_KW) & (blk < KVB_VW))
    is_val = ((blk >= KVB_VS) & (blk < KVB_KW)) | ((blk >= KVB_VW) & (blk < KVB_END))
    ones_row = jnp.where(is_val & (lane >= NSA_DH), 1.0, 0.0)
    hi_row = jnp.where(is_key & (lane == LANE_POS_HI), 1.0, 0.0)
    lo_row = jnp.where(is_key & (lane == LANE_POS_LO), 1.0, 0.0)
    pos = (pl.program_id(0) % tiles_per_seq) * tm + lax.broadcasted_iota(jnp.int32, (tm, 1), 0)
    pos_hi = jnp.right_shift(pos, LANE_SHIFT).astype(F32)
    pos_lo = (pos & (LANES - 1)).astype(F32)
    o_ref[:, N_MG:] = (acc + ones_row + pos_hi * hi_row + pos_lo * lo_row).astype(o_ref.dtype)


def _proj_specs(tm, d, tn, per_b):
    return [pl.BlockSpec((tm, d), lambda i, j: (i, 0)),
            pl.BlockSpec((1, d), lambda i, j: (0, 0)),
            pl.BlockSpec((1, 1, d), lambda i, j: (i // per_b, 0, 0)),
            pl.BlockSpec((1, 1, d), lambda i, j: (i // per_b, 0, 0)),
            pl.BlockSpec((d, tn), lambda i, j: (0, j), pipeline_mode=pl.Buffered(1))]


def _proj_call(x2, g, sc, sh, w, seq, tm, out_dtype, name):
    t, d = x2.shape
    n = w.shape[1]
    tm = min(tm, seq)
    return pl.pallas_call(
        _proj_kernel,
        grid=(t // tm, 1),
        in_specs=_proj_specs(tm, d, n, seq // tm),
        out_specs=pl.BlockSpec((tm, n), lambda i, j: (i, j)),
        out_shape=jax.ShapeDtypeStruct((t, n), out_dtype),
        compiler_params=_cparams(("parallel", "arbitrary")),
        name=name,
    )(x2, g.reshape(1, d), sc, sh, w)


def _proj_kv_call(x2, g, sc, sh, w, seq):
    t, d = x2.shape
    n = w.shape[1]
    tm = min(PROJ_TM, seq)
    return pl.pallas_call(
        functools.partial(_proj_kv_kernel, tiles_per_seq=seq // tm),
        grid=(t // tm, 1),
        in_specs=_proj_specs(tm, d, n, seq // tm),
        out_specs=pl.BlockSpec((tm, n), lambda i, j: (i, j)),
        out_shape=jax.ShapeDtypeStruct((t, n), BF16),
        compiler_params=_cparams(("parallel", "arbitrary")),
        name="proj_kv",
    )(x2, g.reshape(1, d), sc, sh, w)


def _compress_kernel(x_ref, pos_ref, w1_ref, w1g_ref, w2_ref, o_ref):
    nr = x_ref.shape[1] // CMP_STRIDE
    xs = jnp.concatenate([x_ref[0, pl.ds(rho, nr, stride=CMP_STRIDE), :].astype(BF16) for rho in range(CMP_STRIDE)],
                         axis=1)
    posb = jnp.dot(pos_ref[0], w1_ref[0], preferred_element_type=F32)[0:1]
    for g in range(NSA_GROUPS):
        top = jnp.dot(xs, w1g_ref[0, g, 0], preferred_element_type=F32)
        bot = jnp.dot(xs, w1g_ref[0, g, 1], preferred_element_type=F32)
        hid = top + pltpu.roll(bot, nr - 1, 0) + posb
        o_ref[0, 0, g] = _mm(_silu(hid), w2_ref[0]).astype(o_ref.dtype)


def _compress_call(main3, pos, w1, w1g, w2):
    b, seq, _ = main3.shape
    nr = seq // CMP_STRIDE
    two, l64, hid = w1.shape
    return pl.pallas_call(
        _compress_kernel,
        grid=(two, b),
        in_specs=[pl.BlockSpec((1, seq, LANES), lambda a, i: (i, 0, COL_KC // LANES + a)),
                  pl.BlockSpec((1, SUBLANES, l64), lambda a, i: (a, 0, 0)),
                  pl.BlockSpec((1, l64, hid), lambda a, i: (a, 0, 0)),
                  pl.BlockSpec((1,) + w1g.shape[1:], lambda a, i: (a, 0, 0, 0, 0)),
                  pl.BlockSpec((1, hid, LANES), lambda a, i: (a, 0, 0))],
        out_specs=pl.BlockSpec((1, 1, NSA_GROUPS, nr, LANES), lambda a, i: (a, i, 0, 0, 0)),
        out_shape=jax.ShapeDtypeStruct((two, b, NSA_GROUPS, nr, LANES), BF16),
        compiler_params=_cparams(("parallel", "parallel")),
        name="compress",
    )(main3, pos, w1, w1g, w2)


def _gate_cols(gate_logits, branch):
    sg = jax.nn.sigmoid(gate_logits)
    return [sg[:, 3 * r + branch:3 * r + branch + 1] for r in range(NSA_REP)]


def _stack_heads_dup(q):
    qa = q[:, :LANES]
    qb = q[:, LANES:]
    lo = lax.broadcasted_iota(jnp.int32, qa.shape, 1) < NSA_DH
    z = jnp.zeros_like(qa)
    return jnp.concatenate([jnp.where(lo, qa, z), jnp.where(lo, z, qa),
                            jnp.where(lo, qb, z), jnp.where(lo, z, qb)], axis=0)


def _unstack_heads_dup(o, gates):
    nq = o.shape[0] // NSA_REP
    lo = lax.broadcasted_iota(jnp.int32, (nq, LANES), 1) < NSA_DH
    a = jnp.where(lo, gates[0] * o[0:nq], gates[1] * o[nq:2 * nq])
    b = jnp.where(lo, gates[2] * o[2 * nq:3 * nq], gates[3] * o[3 * nq:])
    return jnp.concatenate([a, b], axis=1)


def _stack_heads_pos(q, slopes):
    qa = q[:, :LANES]
    qb = q[:, LANES:]
    lane = lax.broadcasted_iota(jnp.int32, qa.shape, 1)
    lo = lane < NSA_DH
    heads = [qa, pltpu.roll(qa, NSA_DH, 1), qb, pltpu.roll(qb, NSA_DH, 1)]
    out = []
    for r in range(NSA_REP):
        extra = jnp.where(lane == LANE_POS_HI, slopes[r] * LANES, jnp.where(lane == LANE_POS_LO, slopes[r], 0.0))
        out.append(jnp.where(lo, heads[r], extra))
    return jnp.concatenate(out, axis=0)


def _finish_heads(acc, gates):
    nq = acc.shape[0] // NSA_REP
    o = acc * pltpu.roll(1.0 / acc, NSA_DH, 1)
    lo = lax.broadcasted_iota(jnp.int32, (nq, LANES), 1) < NSA_DH
    hs = [gates[r] * o[r * nq:(r + 1) * nq] for r in range(NSA_REP)]
    a = jnp.where(lo, hs[0], pltpu.roll(hs[1], NSA_DH, 1))
    b = jnp.where(lo, hs[2], pltpu.roll(hs[3], NSA_DH, 1))
    return jnp.concatenate([a, b], axis=1)


def _select_kernel(slopes_ref, q_ref, kc_ref, vc_ref, gates_ref, ov_ref, sel_ref, act_ref, yc_ref):
    step = pl.program_id(1)
    nsteps = pl.num_programs(1)
    nb_all, ncp_all = ov_ref.shape
    for k in range(SELECT_BUCKETS):
        @pl.when((step * SELECT_BUCKETS) // nsteps == k)
        def _(k=k):
            for u in range(SELECT_TILES):
                _select_prefix(slopes_ref, q_ref, kc_ref, vc_ref, gates_ref, ov_ref, sel_ref, act_ref, yc_ref, tile=u,
                               nb=(k + 1) * nb_all // SELECT_BUCKETS, ncp=(k + 1) * ncp_all // SELECT_BUCKETS)


def _select_prefix(slopes_ref, q_ref, kc_ref, vc_ref, gates_ref, ov_ref, sel_ref, act_ref, yc_ref, *, tile, nb, ncp):
    t0 = (pl.program_id(1) * SELECT_TILES + tile) * Q_TILE
    nq = Q_TILE
    rows = slice(tile * nq, (tile + 1) * nq)
    nb_all = ov_ref.shape[0]
    c_end = lax.broadcasted_iota(jnp.int32, (nq, ncp), 1) * CMP_STRIDE + (CMP_LEN - 1)
    tq = t0 + lax.broadcasted_iota(jnp.int32, (nq, ncp), 0)
    vis = c_end <= tq
    any_vis = tq[:, 0:1] >= CMP_LEN - 1
    rel = (c_end[0:1] - t0).astype(F32)
    ov = ov_ref[:nb, :ncp]
    jb = lax.broadcasted_iota(jnp.int32, (nb, nq), 0)
    tql = t0 + lax.broadcasted_iota(jnp.int32, (nb, nq), 1)
    valid = jb * SEL_LEN <= tql
    forced = (jb == 0) | (jb == jnp.right_shift(tql, SEL_SHIFT))
    scores = []
    for g in range(NSA_GROUPS):
        qcols = slice(g * NSA_REP * NSA_DH, (g + 1) * NSA_REP * NSA_DH)
        qs = _stack_heads_dup(q_ref[rows, qcols] * (NSA_DH ** -0.5)).astype(BF16)
        s = _mm_nt(qs, kc_ref[0, g, :ncp, :])
        psum = jnp.zeros((nq, ncp), F32)
        ps = []
        for r in range(NSA_REP):
            m_r = slopes_ref[g * NSA_REP + r]
            sr = jnp.where(vis, s[r * nq:(r + 1) * nq] + m_r * rel, NEG)
            mx = jnp.max(sr, axis=-1, keepdims=True)
            e = jnp.exp(sr - mx)
            den = jnp.sum(e, axis=-1, keepdims=True)
            p = e * jnp.where(any_vis, 1.0 / den, 0.0)
            psum = psum + p
            ps.append(p.astype(BF16))
        o_c = jnp.dot(jnp.concatenate(ps, axis=0), vc_ref[0, g, :ncp, :], preferred_element_type=F32)
        yc_ref[rows, qcols] = _unstack_heads_dup(o_c, _gate_cols(gates_ref[rows, g * LANES:(g + 1) * LANES], 0))
        p_hi = psum.astype(BF16)
        p_lo = (psum - p_hi.astype(F32)).astype(BF16)
        imp_t = _mm_nt(ov, p_hi) + _mm_nt(ov, p_lo)
        scores.append(jnp.where(valid, imp_t + FORCE_BONUS * forced.astype(F32), NEG))

    for _ in range(min(TOP_N, nb)):
        for g in range(NSA_GROUPS):
            mx = jnp.max(scores[g], axis=0, keepdims=True)
            idx = jnp.min(jnp.where(scores[g] == mx, jb, nb), axis=0, keepdims=True)
            scores[g] = jnp.where(jb == idx, -jnp.inf, scores[g])
    for g in range(NSA_GROUPS):
        sel_t = jnp.where((scores[g] == -jnp.inf) & valid, 1.0, 0.0)
        if nb < nb_all:
            sel_t = jnp.concatenate([sel_t, jnp.zeros((nb_all - nb, nq), F32)], axis=0)
        sel_q = sel_t.T
        sel_ref[0, g, rows, :] = sel_q.astype(sel_ref.dtype)
        act_ref[0, g, tile] = jnp.max(sel_q, axis=0, keepdims=True)


def _select_call(slopes, main, kcmp, vcmp, ov, bsz, seq):
    nqt = seq // Q_TILE
    nsteps = nqt // SELECT_TILES
    tq = Q_TILE * SELECT_TILES
    nb, ncp = ov.shape
    row = lambda b, q: b * nsteps + q
    gw = NSA_GROUPS * LANES
    return pl.pallas_call(
        _select_kernel,
        grid=(bsz, nsteps),
        in_specs=[pl.BlockSpec(memory_space=pltpu.SMEM),
                  pl.BlockSpec((tq, NSA_Q_W), lambda b, q: (row(b, q), COL_Q // NSA_Q_W)),
                  pl.BlockSpec((1, NSA_GROUPS, ncp, LANES), lambda b, q: (b, 0, 0, 0)),
                  pl.BlockSpec((1, NSA_GROUPS, ncp, LANES), lambda b, q: (b, 0, 0, 0)),
                  pl.BlockSpec((tq, gw), lambda b, q: (row(b, q), COL_GATE // gw)),
                  pl.BlockSpec((nb, ncp), lambda b, q: (0, 0))],
        out_specs=[pl.BlockSpec((1, NSA_GROUPS, tq, nb), lambda b, q: (b, 0, q, 0)),
                   pl.BlockSpec((1, NSA_GROUPS, SELECT_TILES, 1, nb), lambda b, q: (b, 0, q, 0, 0)),
                   pl.BlockSpec((tq, NSA_Q_W), lambda b, q: (row(b, q), 0))],
        out_shape=[jax.ShapeDtypeStruct((bsz, NSA_GROUPS, seq, nb), BF16),
                   jax.ShapeDtypeStruct((bsz, NSA_GROUPS, nqt, 1, nb), F32),
                   jax.ShapeDtypeStruct((bsz * seq, NSA_Q_W), F32)],
        compiler_params=_cparams(("parallel", "parallel")),
        name="select",
    )(slopes, main, kcmp, vcmp, main, ov)


def _attn_kernel(lst_ref, cnt_ref, slopes_ref, q_ref, sel_ref, ks_ref, vs_ref, kw_ref, vw_ref, gates_ref, yc_ref,
                 et_ref, o_ref, s_scr, p_scr, m_scr, a_scr, acc_scr, *, chunk):
    b = pl.program_id(0)
    g = pl.program_id(1)
    step = pl.program_id(2)
    tiles_per_step = ATTN_TILES * ATTN_GROUPS
    nqt = pl.num_programs(2) * tiles_per_step
    nq = ATTN_Q_TILE
    nw = WINDOW + nq
    nch_total = ks_ref.shape[0] // chunk
    slopes = [slopes_ref[g * NSA_REP + r] for r in range(NSA_REP)]

    def values(p, c):
        start = pl.multiple_of(c * chunk, chunk)
        return jnp.dot(p, vs_ref[pl.ds(start, chunk), :], preferred_element_type=F32)

    def row_max(s):
        return jnp.broadcast_to(jnp.max(s, axis=-1, keepdims=True), (s.shape[0], LANES))

    def lane_tile(m, width):
        return jnp.concatenate([m] * (width // LANES), axis=1)

    def masked(s, ok):
        return jnp.concatenate([jnp.where(ok, s[r * nq:(r + 1) * nq], NEG) for r in range(NSA_REP)], axis=0)

    class Group:
        pass

    def setup(k):
        gr = Group()
        gr.tiles = range(ATTN_TILES)
        gr.scr = [(k % 2) * ATTN_TILES + u for u in gr.tiles]
        local = [k * ATTN_TILES + u for u in gr.tiles]
        gr.rws = [slice(v * nq, (v + 1) * nq) for v in local]
        qts = [step * tiles_per_step + v for v in local]
        gr.t0s = [qt * nq for qt in qts]
        gr.qs = [_stack_heads_pos(q_ref[rw, :] * (NSA_DH ** -0.5), slopes).astype(BF16) for rw in gr.rws]
        pens = [((sel_ref[0, 0, rw, :].astype(F32) - 1.0) * (-NEG)).astype(BF16) for rw in gr.rws]
        gr.qx = [jnp.concatenate([gr.qs[u], jnp.concatenate([pens[u]] * NSA_REP, axis=0)], axis=1) for u in gr.tiles]
        gr.bases = [((b * NSA_GROUPS + g) * nqt + qt) * nch_total for qt in qts]
        gr.cnts = [cnt_ref[(b * NSA_GROUPS + g) * nqt + qt] for qt in qts]
        for u in gr.tiles:
            su = gr.scr[u]
            m_scr[su] = jnp.full(m_scr.shape[1:], NEG, F32)
            a_scr[su] = jnp.ones(a_scr.shape[1:], F32)
            acc_scr[su] = jnp.zeros(acc_scr.shape[1:], F32)
            p_scr[su, 1] = jnp.zeros(p_scr.shape[2:], BF16)
        for u in gr.tiles:
            s_scr[gr.scr[u], 0] = logits(gr, u, lst_ref[gr.bases[u]])
        return gr

    def logits(gr, u, c):
        start = pl.multiple_of(c * chunk, chunk)
        kx = jnp.concatenate([ks_ref[pl.ds(start, chunk), :], et_ref[pl.ds(start, chunk), :]], axis=1)
        return _mm_nt(gr.qx[u], kx)

    def flash_step(gr, us, i, slot):
        for u in us:
            su = gr.scr[u]
            acc_scr[su] = a_scr[su] * acc_scr[su] + values(p_scr[su, 1 - slot],
                                                           lst_ref[gr.bases[u] + jnp.maximum(i - 1, 0)])
        for u in us:
            su = gr.scr[u]
            s = s_scr[su, slot]
            m_prev = m_scr[su]
            m_new = jnp.maximum(m_prev, row_max(s))
            a_scr[su] = jnp.exp(m_prev - m_new)
            p_scr[su, slot] = jnp.exp(s - lane_tile(m_new, chunk)).astype(BF16)
            m_scr[su] = m_new
        for u in us:
            s_scr[gr.scr[u], 1 - slot] = logits(gr, u, lst_ref[gr.bases[u] + i + 1])

    def flash_loops(gr):
        def single_steps(us):
            def body(i, carry):
                flash_step(gr, us, i, i & 1)
                return carry
            return body

        def double_steps(us):
            def body(j, carry):
                flash_step(gr, us, 2 * j, 0)
                flash_step(gr, us, 2 * j + 1, 1)
                return carry
            return body

        joint = functools.reduce(jnp.minimum, gr.cnts)
        lax.fori_loop(0, joint // 2, double_steps(list(gr.tiles)), 0)
        lax.fori_loop(joint - (joint & 1), joint, single_steps(list(gr.tiles)), 0)
        for u in gr.tiles:
            lax.fori_loop(joint, gr.cnts[u], single_steps([u]), 0)

    def finish(gr):
        tiles = gr.tiles
        scr = gr.scr
        slots = [gr.cnts[u] & 1 for u in tiles]
        c_diag = [gr.t0s[u] // chunk for u in tiles]
        acc = [a_scr[scr[u]] * acc_scr[scr[u]]
               + values(p_scr[scr[u], 1 - slots[u]], lst_ref[gr.bases[u] + jnp.maximum(gr.cnts[u] - 1, 0)])
               for u in tiles]
        wbase = [pl.multiple_of(jnp.maximum(gr.t0s[u] - WINDOW, 0), nq) for u in tiles]
        s_win = [_mm_nt(gr.qs[u], kw_ref[pl.ds(wbase[u], nw), :]) for u in tiles]
        s_diag = []
        for u in tiles:
            pos = c_diag[u] * chunk + lax.broadcasted_iota(jnp.int32, (nq, chunk), 1)
            tq = gr.t0s[u] + lax.broadcasted_iota(jnp.int32, (nq, chunk), 0)
            s_diag.append(masked(s_scr[scr[u], slots[u]], pos <= tq))
        m_prev = [m_scr[scr[u]] for u in tiles]
        m_new = [jnp.maximum(m_prev[u], row_max(s_diag[u])) for u in tiles]
        acc = [jnp.exp(m_prev[u] - m_new[u]) * acc[u]
               + values(jnp.exp(s_diag[u] - lane_tile(m_new[u], chunk)).astype(BF16), c_diag[u]) for u in tiles]
        p_win = []
        for u in tiles:
            pos = wbase[u] + lax.broadcasted_iota(jnp.int32, (nq, nw), 1)
            tq = gr.t0s[u] + lax.broadcasted_iota(jnp.int32, (nq, nw), 0)
            s = masked(s_win[u], (pos <= tq) & (pos > tq - WINDOW))
            p_win.append(jnp.exp(s - jnp.max(s, axis=-1, keepdims=True)).astype(BF16))
        acc_w = [jnp.dot(p_win[u], vw_ref[pl.ds(wbase[u], nw), :], preferred_element_type=F32) for u in tiles]
        for u in tiles:
            gates = gates_ref[gr.rws[u], :]
            o_ref[gr.rws[u], :] = (yc_ref[gr.rws[u], :] + _finish_heads(acc[u], _gate_cols(gates, 1))
                                   + _finish_heads(acc_w[u], _gate_cols(gates, 2))).astype(o_ref.dtype)

    cur = setup(0)
    for k in range(ATTN_GROUPS):
        flash_loops(cur)
        nxt = setup(k + 1) if k + 1 < ATTN_GROUPS else None
        finish(cur)
        cur = nxt


def _chunk_lists(act, seq, chunk):
    bsz, ngrp = act.shape[:2]
    nqt = seq // ATTN_Q_TILE
    nch = seq // chunk
    active = act.reshape(bsz, ngrp, nqt, ATTN_Q_TILE // Q_TILE, nch, chunk // SEL_LEN).max(axis=(3, 5)) > 0.0
    c_diag = (jnp.arange(nqt) * ATTN_Q_TILE) // chunk
    active = active & (jnp.arange(nch)[None, :] < c_diag[:, None])
    cnt = active.sum(axis=-1).astype(jnp.int32)
    order = jnp.argsort(jnp.logical_not(active), axis=-1, stable=True).astype(jnp.int32)
    lst = jnp.where(jnp.arange(nch) < cnt[..., None], order, c_diag[:, None].astype(jnp.int32))
    return lst.reshape(-1), cnt.reshape(-1)


def _attn_call(lst, cnt, slopes, main, sel, kv, yc, et, bsz, seq, chunk):
    tq = ATTN_Q_TILE * ATTN_TILES * ATTN_GROUPS
    nset = 2 * ATTN_TILES
    nqt = seq // tq
    nb = sel.shape[-1]
    rows = NSA_REP * ATTN_Q_TILE
    row = lambda b, g, q: b * nqt + q
    kvspec = lambda blk: pl.BlockSpec((seq, LANES), lambda b, g, q, *_: (b, N_MG // LANES + blk + g))
    grid_spec = pltpu.PrefetchScalarGridSpec(
        num_scalar_prefetch=2,
        grid=(bsz, NSA_GROUPS, nqt),
        in_specs=[pl.BlockSpec(memory_space=pltpu.SMEM),
                  pl.BlockSpec((tq, 256), lambda b, g, q, *_: (row(b, g, q), COL_Q // 256 + g)),
                  pl.BlockSpec((1, 1, tq, nb), lambda b, g, q, *_: (b, g, q, 0)),
                  kvspec(KVB_KS), kvspec(KVB_VS), kvspec(KVB_KW), kvspec(KVB_VW),
                  pl.BlockSpec((tq, LANES), lambda b, g, q, *_: (row(b, g, q), COL_GATE // LANES + g)),
                  pl.BlockSpec((tq, 256), lambda b, g, q, *_: (row(b, g, q), g)),
                  pl.BlockSpec(et.shape, lambda b, g, q, *_: (0, 0))],
        out_specs=pl.BlockSpec((tq, 256), lambda b, g, q, *_: (row(b, g, q), g)),
        scratch_shapes=[pltpu.VMEM((nset, 2, rows, chunk), F32),
                        pltpu.VMEM((nset, 2, rows, chunk), BF16),
                        pltpu.VMEM((nset, rows, LANES), F32),
                        pltpu.VMEM((nset, rows, LANES), F32),
                        pltpu.VMEM((nset, rows, LANES), F32)],
    )
    return pl.pallas_call(
        functools.partial(_attn_kernel, chunk=chunk),
        grid_spec=grid_spec,
        out_shape=jax.ShapeDtypeStruct((bsz * seq, NSA_Q_W), BF16),
        compiler_params=_cparams(("parallel", "parallel", "parallel")),
        name="attn",
    )(lst, cnt, slopes, main, sel, kv, kv, kv, kv, main, yc, et)


def _pool_kernel(x_ref, halo_ref, w_ref, sc_ref, o_ref):
    s = pl.program_id(1)
    ts = x_ref.shape[0]
    x = x_ref[...]
    halo = jnp.where(s > 0, halo_ref[...], 0.0)
    t = s * ts + lax.broadcasted_iota(jnp.int32, (ts, 1), 0)
    for gi, win in enumerate(POOL_WINDOWS):
        lanes = slice(gi * POOL_GROUP_DIM, (gi + 1) * POOL_GROUP_DIM)
        ext = jnp.concatenate([halo[:, lanes], x[:, lanes]], axis=0)
        acc = ext
        span = 1
        while span < win:
            acc = acc + pltpu.roll(acc, span, 0)
            span *= 2
        cnt = jnp.minimum(t + 1, win).astype(F32)
        y = acc[POOL_HALO:] / cnt - x[:, lanes]
        o_ref[:, lanes] = (_mm(y, w_ref[gi]) * sc_ref[:, lanes]).astype(o_ref.dtype)


def _pool_call(main, w, scale, bsz, seq):
    ts = min(POOL_TS, seq)
    per_b = seq // ts
    halo_per_tile = ts // POOL_HALO
    return pl.pallas_call(
        _pool_kernel,
        grid=(bsz, per_b),
        in_specs=[pl.BlockSpec((ts, POOL_W), lambda b, s: (b * per_b + s, COL_POOL // POOL_W)),
                  pl.BlockSpec((POOL_HALO, POOL_W),
                               lambda b, s: (jnp.maximum((b * per_b + s) * halo_per_tile - 1, 0), COL_POOL // POOL_W)),
                  pl.BlockSpec(w.shape, lambda b, s: (0, 0, 0)),
                  pl.BlockSpec((1, POOL_W), lambda b, s: (0, 0))],
        out_specs=pl.BlockSpec((ts, POOL_W), lambda b, s: (b * per_b + s, 0)),
        out_shape=jax.ShapeDtypeStruct((bsz * seq, POOL_W), BF16),
        compiler_params=_cparams(("parallel", "parallel")),
        name="pool",
    )(main, main, w, scale.reshape(1, POOL_W))


def _unit_lower_inverses(nms, ri, ci):
    c = nms[0].shape[0]
    eye = jnp.where(ri == ci, 1.0, 0.0)
    in8 = jnp.right_shift(ri, 3) == jnp.right_shift(ci, 3)
    n8 = [jnp.where(in8, nm, 0.0) for nm in nms]
    ps = [eye - a for a in n8]
    ms = [_mm(a, a) for a in n8]
    ps = [p + _mm(p, m) for p, m in zip(ps, ms)]
    ms = [_mm(m, m) for m in ms]
    ps = [p + _mm(p, m) for p, m in zip(ps, ms)]
    shift = 3
    while (1 << shift) < c:
        rb = jnp.right_shift(ri, shift)
        cb = jnp.right_shift(ci, shift)
        lower_left = ((rb & 1) == 1) & (cb == rb - 1)
        ts = [_mm(p, jnp.where(lower_left, nm, 0.0)) for p, nm in zip(ps, nms)]
        ps = [p - _mm(t, p) for p, t in zip(ps, ts)]
        shift += 1
    return ps


def _row_cumsum(x, ri):
    span = 1
    while span < x.shape[0]:
        x = x + jnp.where(ri >= span, pltpu.roll(x, span, 0), 0.0)
        span *= 2
    return x


def _dn_kernel(qkv_ref, halo_ref, z_ref, gba_ref, cw_ref, alog_ref, dtb_ref, ng_ref, o_ref, st_scr):
    s = pl.program_id(1)
    nbatch, c = qkv_ref.shape[0], qkv_ref.shape[1]

    @pl.when(s == 0)
    def _():
        st_scr[...] = jnp.zeros(st_scr.shape, F32)

    ri = lax.broadcasted_iota(jnp.int32, (c, c), 0)
    ci = lax.broadcasted_iota(jnp.int32, (c, c), 1)
    tril = ri >= ci
    strict = ri > ci
    cw = cw_ref[...]
    ng = ng_ref[...]
    chains = [(bi, h) for bi in range(nbatch) for h in range(DN_HEADS)]
    qn, kn, kb, vb, gcb, lmask = [], [], [], [], [], []
    for bi in range(nbatch):
        halo = jnp.where(s > 0, halo_ref[bi], 0.0)
        ext = jnp.concatenate([halo, qkv_ref[bi]], axis=0)
        conv = cw[DN_CONV - 1:DN_CONV] * ext
        for tap in range(1, DN_CONV):
            conv = conv + cw[DN_CONV - 1 - tap:DN_CONV - tap] * pltpu.roll(ext, tap, 0)
        act = _silu(conv[DN_HALO:])

        gba = gba_ref[bi]
        beta_all = jax.nn.sigmoid(gba)
        xs = gba + dtb_ref[...]
        softplus = jnp.maximum(xs, 0.0) + jnp.log(1.0 + jnp.exp(-jnp.abs(xs)))
        gc_all = _row_cumsum(-jnp.exp(alog_ref[...]) * softplus, ri)
        gct_all = gc_all.T
        for h in range(DN_HEADS):
            q = act[:, h * DN_DK:(h + 1) * DN_DK]
            k = act[:, DN_W + h * DN_DK:DN_W + (h + 1) * DN_DK]
            v = act[:, 2 * DN_W + h * DN_DK:2 * DN_W + (h + 1) * DN_DK]
            qn.append(q * lax.rsqrt(jnp.sum(q * q, axis=-1, keepdims=True) + EPS) * (DN_DK ** -0.5))
            kn.append(k * lax.rsqrt(jnp.sum(k * k, axis=-1, keepdims=True) + EPS))
            beta = beta_all[:, LANE_BETA + h:LANE_BETA + h + 1]
            gcb.append(jnp.broadcast_to(gc_all[:, LANE_DECAY + h:LANE_DECAY + h + 1], (c, DN_DK)))
            gct = jnp.broadcast_to(gct_all[LANE_DECAY + h:LANE_DECAY + h + 1, :], (c, c))
            lmask.append(jnp.where(tril, jnp.exp(jnp.where(tril, gcb[-1] - gct, 0.0)), 0.0))
            kb.append(kn[-1] * beta)
            vb.append(v * beta)

    n = range(len(chains))
    kk = [_mm_nt(kb[i], kn[i]) for i in n]
    tinv = _unit_lower_inverses([jnp.where(strict, kk[i] * lmask[i], 0.0) for i in n], ri, ci)
    egc = [jnp.exp(gcb[i]) for i in n]
    sol = [_mm(tinv[i], jnp.concatenate([vb[i], kb[i] * egc[i]], axis=1)) for i in n]
    aqk = [_mm_nt(qn[i], kn[i]) * lmask[i] for i in n]
    state = [st_scr[i] for i in n]
    glast = [gcb[i][c - 1:c, :] for i in n]
    v_new = [sol[i][:, :DN_DK] - _mm(sol[i][:, DN_DK:], state[i]) for i in n]
    o_state = [_mm(qn[i] * egc[i], state[i]) for i in n]
    o = [o_state[i] + _mm(aqk[i], v_new[i]) for i in n]
    upd = [_mm_tn(kn[i] * jnp.exp(glast[i] - gcb[i]), v_new[i]) for i in n]
    for i, (bi, h) in enumerate(chains):
        st_scr[i] = state[i] * jnp.exp(glast[i]) + upd[i]
        on = o[i] * lax.rsqrt(jnp.mean(o[i] * o[i], axis=-1, keepdims=True) + EPS) * ng
        o_ref[bi, :, h * DN_DK:(h + 1) * DN_DK] = (
            on * _silu(z_ref[bi, :, h * DN_DK:(h + 1) * DN_DK])).astype(o_ref.dtype)


def _dn_call(main3, conv_w, alog_row, dtb_row, norm_g):
    bsz, seq, _ = main3.shape
    c = DN_CHUNK
    nbt = DN_BATCH if bsz % DN_BATCH == 0 else 1
    halo_per_chunk = c // DN_HALO
    w3 = 3 * DN_W
    return pl.pallas_call(
        _dn_kernel,
        grid=(bsz // nbt, seq // c),
        in_specs=[pl.BlockSpec((nbt, c, w3), lambda b, s: (b, s, COL_QKV // w3)),
                  pl.BlockSpec((nbt, DN_HALO, w3),
                               lambda b, s: (b, jnp.maximum(s * halo_per_chunk - 1, 0), COL_QKV // w3)),
                  pl.BlockSpec((nbt, c, DN_W), lambda b, s: (b, s, COL_Z // DN_W)),
                  pl.BlockSpec((nbt, c, LANES), lambda b, s: (b, s, COL_GATE // LANES)),
                  pl.BlockSpec((SUBLANES, w3), lambda b, s: (0, 0)),
                  pl.BlockSpec((1, LANES), lambda b, s: (0, 0)),
                  pl.BlockSpec((1, LANES), lambda b, s: (0, 0)),
                  pl.BlockSpec((1, DN_DK), lambda b, s: (0, 0))],
        out_specs=pl.BlockSpec((nbt, c, DN_W), lambda b, s: (b, s, 0)),
        out_shape=jax.ShapeDtypeStruct((bsz, seq, DN_W), BF16),
        scratch_shapes=[pltpu.VMEM((nbt * DN_HEADS, DN_DK, DN_DK), F32)],
        compiler_params=_cparams(("parallel", "arbitrary")),
        name="dn",
    )(main3, main3, main3, main3, conv_w, alog_row, dtb_row, norm_g.reshape(1, DN_DK))


def _merge_kernel(ya_ref, yb_ref, yc_ref, mg_ref, x_ref, gt_ref, wa_ref, wb_ref, wc_ref, wo_ref, o_ref):
    d = x_ref.shape[1]
    gates = jax.nn.sigmoid(mg_ref[...].astype(F32))
    merged = (gates[:, :d] * jnp.dot(ya_ref[...], wa_ref[...], preferred_element_type=F32)
              + gates[:, d:2 * d] * jnp.dot(yb_ref[...], wb_ref[...], preferred_element_type=F32)
              + gates[:, 2 * d:] * jnp.dot(yc_ref[...], wc_ref[...], preferred_element_type=F32))
    y = jnp.dot(merged.astype(BF16), wo_ref[...], preferred_element_type=F32)
    o_ref[...] = x_ref[...] + gt_ref[0] * y


def _merge_call(ya, yb, yc, mg, x2, gt, wa, wb, wc, wo, seq):
    t, d = x2.shape
    tm = min(MERGE_TM, seq)
    per_b = seq // tm
    full = lambda a: pl.BlockSpec(a.shape, lambda i: (0, 0))
    return pl.pallas_call(
        _merge_kernel,
        grid=(t // tm,),
        in_specs=[pl.BlockSpec((tm, ya.shape[1]), lambda i: (i, 0)),
                  pl.BlockSpec((tm, yb.shape[1]), lambda i: (i, 0)),
                  pl.BlockSpec((tm, yc.shape[1]), lambda i: (i, 0)),
                  pl.BlockSpec((tm, N_MG), lambda i: (i, 0)),
                  pl.BlockSpec((tm, d), lambda i: (i, 0)),
                  pl.BlockSpec((1, 1, d), lambda i: (i // per_b, 0, 0)),
                  full(wa), full(wb), full(wc), full(wo)],
        out_specs=pl.BlockSpec((tm, d), lambda i: (i, 0)),
        out_shape=jax.ShapeDtypeStruct((t, d), F32),
        compiler_params=_cparams(("parallel",)),
        name="merge",
    )(ya, yb, yc, mg, x2, gt, wa, wb, wc, wo)


def _mlp_kernel(x_ref, g_ref, sc_ref, sh_ref, gt_ref, w1_ref, w2_ref, fg_ref, o_ref, *, final):
    x = x_ref[...]
    h = _norm_mod(x, g_ref[...], sc_ref[0], sh_ref[0]).astype(BF16)
    acc = None
    for c0 in range(0, w1_ref.shape[1], MLP_TF):
        a = jnp.maximum(jnp.dot(h, w1_ref[:, c0:c0 + MLP_TF], preferred_element_type=F32), 0.0)
        part = jnp.dot((a * a).astype(BF16), w2_ref[c0:c0 + MLP_TF, :], preferred_element_type=F32)
        acc = part if acc is None else acc + part
    y = x + gt_ref[0] * acc
    if final:
        ms = jnp.mean(y * y, axis=-1, keepdims=True)
        y = y * lax.rsqrt(ms + EPS) * fg_ref[...]
    o_ref[...] = y


def _mlp_call(x2, g, sc, sh, gt, w1, w2, layer, fg, seq, final):
    t, d = x2.shape
    f = w1.shape[2]
    assert f % MLP_TF == 0
    tm = min(MLP_TM, seq)
    per_b = seq // tm
    mod = lambda: pl.BlockSpec((1, 1, d), lambda i: (i // per_b, 0, 0))
    resident = pl.Buffered(1)
    return pl.pallas_call(
        functools.partial(_mlp_kernel, final=final),
        grid=(t // tm,),
        in_specs=[pl.BlockSpec((tm, d), lambda i: (i, 0)),
                  pl.BlockSpec((1, d), lambda i: (0, 0)),
                  mod(), mod(), mod(),
                  pl.BlockSpec((None, d, f), lambda i: (layer, 0, 0), pipeline_mode=resident),
                  pl.BlockSpec((None, f, d), lambda i: (layer, 0, 0), pipeline_mode=resident),
                  pl.BlockSpec((1, d), lambda i: (0, 0))],
        out_specs=pl.BlockSpec((tm, d), lambda i: (i, 0)),
        out_shape=jax.ShapeDtypeStruct((t, d), F32),
        compiler_params=_cparams(("parallel",)),
        name="mlp",
    )(x2, g.reshape(1, d), sc, sh, gt, w1, w2, fg.reshape(1, d))


def _split_w_in(w_in):
    offs = [0]
    for sz in IN_SIZES:
        offs.append(offs[-1] + sz)
    return [w_in[:, offs[i]:offs[i + 1]] for i in range(len(IN_SIZES))]


def _layout_w_in(w_in):
    nq, nkc, nvc, nks, nvs, nkw, nvw, ngate, pin, dqkv, dz, dbeta, da, mg = _split_w_in(w_in)
    d = w_in.shape[0]
    per_g = 3 * NSA_REP
    zeros = lambda n: jnp.zeros((d, n), w_in.dtype)
    gate0 = jnp.concatenate([ngate[:, :per_g], zeros(LANE_BETA - per_g), dbeta, da,
                             zeros(LANES - LANE_DECAY - DN_HEADS)], axis=1)
    gate1 = jnp.concatenate([ngate[:, per_g:], zeros(LANES - per_g)], axis=1)
    main = jnp.concatenate([nq, pin, dz, dqkv, gate0, gate1, nkc, nvc], axis=1)

    def per_group(w):
        pad = zeros(LANES - NSA_DH)
        return jnp.concatenate([w[:, :NSA_DH], pad, w[:, NSA_DH:], pad], axis=1)

    kv = jnp.concatenate([per_group(nks), per_group(nvs), per_group(nkw), per_group(nvw)], axis=1)
    return main.astype(BF16), jnp.concatenate([mg, kv], axis=1).astype(BF16)


def _group_lane_w1(w1):
    two, _, hid = w1.shape
    halves = w1.reshape(two, 2, CMP_STRIDE, NSA_DH, hid)
    out = []
    for g in range(NSA_GROUPS):
        padded = jnp.zeros((two, 2, CMP_STRIDE, LANES, hid), w1.dtype).at[:, :, :, g * NSA_DH:(g + 1) * NSA_DH].set(halves)
        out.append(padded.reshape(two, 2, CMP_STRIDE * LANES, hid))
    return jnp.stack(out, axis=1)


def _nsa_constants(seq):
    nb = seq // SEL_LEN
    ncp = seq // CMP_STRIDE
    j = jnp.arange(nb)[:, None] * SEL_LEN
    i = jnp.arange(ncp)[None, :] * CMP_STRIDE
    overlap = jnp.clip(jnp.minimum(j + SEL_LEN, i + CMP_LEN) - jnp.maximum(j, i), 0) // CMP_STRIDE
    block_onehot = (jnp.arange(seq)[:, None] // SEL_LEN) == jnp.arange(nb)[None, :]
    return overlap.astype(BF16), block_onehot.astype(BF16)


def kernel(x, c, ada_w, ada_b, norm1_g, norm2_g, w_in, phi_k1, phi_k2, phi_v1, phi_v2, pos_k, pos_v, pool_w, pool_scale, dn_conv_w, dn_A_log, dn_dt_bias, dn_norm_g, w_branch_nsa, w_branch_pool, w_branch_dn, w_out, mlp_w1, mlp_w2, final_g):
    bsz, seq, d = x.shape
    assert d == D_MODEL and seq % max(PROJ_TM, MLP_TM, POOL_TS) == 0 and seq % KV_CHUNK == 0
    assert seq % (ATTN_Q_TILE * ATTN_TILES * ATTN_GROUPS) == 0
    assert (seq // Q_TILE) % (SELECT_BUCKETS * SELECT_TILES) == 0
    assert DN_CHUNK == DN_DK == LANES
    t = bsz * seq
    depth = ada_w.shape[0]
    chunk = KV_CHUNK

    slopes = 2.0 ** (-(8.0 / NSA_HEADS) * (jnp.arange(NSA_HEADS, dtype=F32) + 1.0))
    overlap, block_onehot = _nsa_constants(seq)
    c_pad = jnp.zeros((SUBLANES, d), F32).at[:bsz].set(c)
    ada_b3 = ada_b.reshape(depth, 1, ada_b.shape[1])
    mlp_w1_bf, mlp_w2_bf = mlp_w1.astype(BF16), mlp_w2.astype(BF16)
    x2 = x.reshape(t, d)

    for l in range(depth):
        mod = _mod_call(c_pad, ada_w, ada_b3, l)[:bsz]
        sh1, sc1, gt1, sh2, sc2, gt2 = [m.reshape(bsz, 1, d) for m in jnp.split(mod, 6, axis=-1)]

        w_main, w_mgkv = _layout_w_in(w_in[l])
        main = _proj_call(x2, norm1_g[l], sc1, sh1, w_main, seq, PROJ_MAIN_TM, F32, "proj_main")
        kv = mg = _proj_kv_call(x2, norm1_g[l], sc1, sh1, w_mgkv, seq)

        main3 = main.reshape(bsz, seq, N_MAIN)
        pos = jnp.zeros((2, SUBLANES, CMP_LEN * NSA_DH), F32).at[:, 0].set(
            jnp.stack([pos_k[l].reshape(-1), pos_v[l].reshape(-1)])).astype(BF16)
        w1 = jnp.stack([phi_k1[l], phi_v1[l]]).astype(BF16)
        w2 = jnp.stack([phi_k2[l], phi_v2[l]])
        w2 = jnp.concatenate([w2, w2], axis=-1).astype(BF16)
        cmp_kv = _compress_call(main3, pos, w1, _group_lane_w1(w1), w2)

        sel, act, yc = _select_call(slopes, main, cmp_kv[0], cmp_kv[1], overlap, bsz, seq)
        lst, cnt = _chunk_lists(act, seq, chunk)
        y_a = _attn_call(lst, cnt, slopes, main, sel, kv, yc, block_onehot, bsz, seq, chunk)

        y_b = _pool_call(main, pool_w[l].astype(BF16), pool_scale[l], bsz, seq)

        conv_w = jnp.zeros((SUBLANES, 3 * DN_W), F32).at[:DN_CONV].set(dn_conv_w[l])
        lane_row = lambda v: jnp.zeros((1, LANES), F32).at[0, LANE_DECAY:LANE_DECAY + DN_HEADS].set(v)
        y_c = _dn_call(main3, conv_w, lane_row(dn_A_log[l]), lane_row(dn_dt_bias[l]),
                       dn_norm_g[l]).reshape(t, DN_W)

        x2 = _merge_call(y_a, y_b, y_c, mg, x2, gt1,
                         w_branch_nsa[l].astype(BF16), w_branch_pool[l].astype(BF16),
                         w_branch_dn[l].astype(BF16), w_out[l].astype(BF16), seq)
        x2 = _mlp_call(x2, norm2_g[l], sc2, sh2, gt2, mlp_w1_bf, mlp_w2_bf, l,
                       final_g, seq, final=(l == depth - 1))
    return x2.reshape(bsz, seq, d)
```

```python
import functools

import jax
import jax.numpy as jnp
from jax import lax
from jax.experimental import pallas as pl
from jax.experimental.pallas import tpu as pltpu

F32 = jnp.float32
BF16 = jnp.bfloat16
HIGHEST = lax.Precision.HIGHEST

D_MODEL = 1024
NSA_HEADS = 8
NSA_GROUPS = 2
NSA_REP = NSA_HEADS // NSA_GROUPS
NSA_DH = 64
CMP_LEN = 32
CMP_STRIDE = 16
SEL_LEN = 64
SEL_SHIFT = 6
TOP_N = 16
WINDOW = 512
FORCE_BONUS = 1.0e4
POOL_WINDOWS = (2, 4, 8, 16)
POOL_GROUP_DIM = 128
POOL_W = len(POOL_WINDOWS) * POOL_GROUP_DIM
DN_HEADS = 4
DN_DK = 128
DN_W = DN_HEADS * DN_DK
DN_CONV = 4
EPS = 1e-6
NEG = -1e30
NSA_Q_W = NSA_HEADS * NSA_DH
NSA_KV_W = NSA_GROUPS * NSA_DH
IN_SIZES = (NSA_Q_W, NSA_KV_W, NSA_KV_W, NSA_KV_W, NSA_KV_W, NSA_KV_W, NSA_KV_W,
            3 * NSA_HEADS, POOL_W, 3 * DN_W, DN_W, DN_HEADS, DN_HEADS, 3 * D_MODEL)

LANES = 128
LANE_SHIFT = 7
SUBLANES = 8
VMEM_LIMIT_BYTES = 48 * 1024 * 1024

Q_TILE = 128
SELECT_BUCKETS = 4
SELECT_TILES = 4
ATTN_Q_TILE = 128
ATTN_TILES = 2
ATTN_GROUPS = 4
KV_CHUNK = 256
DN_CHUNK = 128
DN_BATCH = 4
PROJ_TM = 1024
MLP_TM = 1024
MLP_TF = 2048
MERGE_TM = 1024
POOL_TS = 1024
POOL_HALO = 16
DN_HALO = 8

COL_Q = 0
COL_POOL = 512
COL_Z = 1024
COL_QKV = 1536
COL_GATE = 3072
COL_KC = 3328
COL_VC = 3456
N_MAIN = 3584
PROJ_MAIN_TM = 1024
N_MG = 3072
LANE_BETA = 16
LANE_DECAY = 20
KVB_KS = 0
KVB_VS = 2
KVB_KW = 4
KVB_VW = 6
KVB_END = 8
N_KV = KVB_END * LANES
LANE_POS_HI = NSA_DH
LANE_POS_LO = NSA_DH + 1


def _cparams(sem):
    return pltpu.CompilerParams(dimension_semantics=sem, vmem_limit_bytes=VMEM_LIMIT_BYTES)


def _silu(v):
    return v * jax.nn.sigmoid(v)


def _mm(a, b):
    return jnp.dot(a.astype(BF16), b.astype(BF16), preferred_element_type=F32)


def _mm_nt(a, b):
    return lax.dot_general(a.astype(BF16), b.astype(BF16), (((1,), (1,)), ((), ())), preferred_element_type=F32)


def _mm_tn(a, b):
    return lax.dot_general(a.astype(BF16), b.astype(BF16), (((0,), (0,)), ((), ())), preferred_element_type=F32)


def _mod_kernel(c_ref, w_ref, b_ref, o_ref):
    cond = _silu(c_ref[...])
    o_ref[...] = jnp.dot(cond, w_ref[...], preferred_element_type=F32, precision=HIGHEST) + b_ref[...]


def _mod_call(c_pad, w, b, layer):
    _, d, n = w.shape
    tn = 1536
    return pl.pallas_call(
        _mod_kernel,
        grid=(n // tn,),
        in_specs=[pl.BlockSpec((SUBLANES, d), lambda j: (0, 0)),
                  pl.BlockSpec((None, d, tn), lambda j: (layer, 0, j)),
                  pl.BlockSpec((None, 1, tn), lambda j: (layer, 0, j))],
        out_specs=pl.BlockSpec((SUBLANES, tn), lambda j: (0, j)),
        out_shape=jax.ShapeDtypeStruct((SUBLANES, n), F32),
        compiler_params=_cparams(("parallel",)),
        name="mod",
    )(c_pad, w, b)


def _norm_mod(x, g, sc, sh):
    ms = jnp.mean(x * x, axis=-1, keepdims=True)
    return (x * lax.rsqrt(ms + EPS) * g) * (1.0 + sc) + sh


def _proj_kernel(x_ref, g_ref, sc_ref, sh_ref, w_ref, o_ref):
    h = _norm_mod(x_ref[...], g_ref[...], sc_ref[0], sh_ref[0]).astype(BF16)
    o_ref[...] = jnp.dot(h, w_ref[...], preferred_element_type=F32).astype(o_ref.dtype)


def _proj_kv_kernel(x_ref, g_ref, sc_ref, sh_ref, w_ref, o_ref, *, tiles_per_seq):
    h = _norm_mod(x_ref[...], g_ref[...], sc_ref[0], sh_ref[0]).astype(BF16)
    full = jnp.dot(h, w_ref[...], preferred_element_type=F32)
    o_ref[:, :N_MG] = full[:, :N_MG].astype(o_ref.dtype)
    acc = full[:, N_MG:]
    tm, n = acc.shape
    col = lax.broadcasted_iota(jnp.int32, (1, n), 1)
    blk = jnp.right_shift(col, LANE_SHIFT)
    lane = col & (LANES - 1)
    is_key = ((blk >= KVB_KS) & (blk < KVB_VS)) | ((blk >= KVB_KW) & (blk < KVB_VW))
    is_val = ((blk >= KVB_VS) & (blk < KVB_KW)) | ((blk >= KVB_VW) & (blk < KVB_END))
    ones_row = jnp.where(is_val & (lane >= NSA_DH), 1.0, 0.0)
    hi_row = jnp.where(is_key & (lane == LANE_POS_HI), 1.0, 0.0)
    lo_row = jnp.where(is_key & (lane == LANE_POS_LO), 1.0, 0.0)
    pos = (pl.program_id(0) % tiles_per_seq) * tm + lax.broadcasted_iota(jnp.int32, (tm, 1), 0)
    pos_hi = jnp.right_shift(pos, LANE_SHIFT).astype(F32)
    pos_lo = (pos & (LANES - 1)).astype(F32)
    o_ref[:, N_MG:] = (acc + ones_row + pos_hi * hi_row + pos_lo * lo_row).astype(o_ref.dtype)


def _proj_specs(tm, d, tn, per_b):
    return [pl.BlockSpec((tm, d), lambda i, j: (i, 0)),
            pl.BlockSpec((1, d), lambda i, j: (0, 0)),
            pl.BlockSpec((1, 1, d), lambda i, j: (i // per_b, 0, 0)),
            pl.BlockSpec((1, 1, d), lambda i, j: (i // per_b, 0, 0)),
            pl.BlockSpec((d, tn), lambda i, j: (0, j), pipeline_mode=pl.Buffered(1))]


def _proj_call(x2, g, sc, sh, w, seq, tm, out_dtype, name):
    t, d = x2.shape
    n = w.shape[1]
    tm = min(tm, seq)
    return pl.pallas_call(
        _proj_kernel,
        grid=(t // tm, 1),
        in_specs=_proj_specs(tm, d, n, seq // tm),
        out_specs=pl.BlockSpec((tm, n), lambda i, j: (i, j)),
        out_shape=jax.ShapeDtypeStruct((t, n), out_dtype),
        compiler_params=_cparams(("parallel", "arbitrary")),
        name=name,
    )(x2, g.reshape(1, d), sc, sh, w)


def _proj_kv_call(x2, g, sc, sh, w, seq):
    t, d = x2.shape
    n = w.shape[1]
    tm = min(PROJ_TM, seq)
    return pl.pallas_call(
        functools.partial(_proj_kv_kernel, tiles_per_seq=seq // tm),
        grid=(t // tm, 1),
        in_specs=_proj_specs(tm, d, n, seq // tm),
        out_specs=pl.BlockSpec((tm, n), lambda i, j: (i, j)),
        out_shape=jax.ShapeDtypeStruct((t, n), BF16),
        compiler_params=_cparams(("parallel", "arbitrary")),
        name="proj_kv",
    )(x2, g.reshape(1, d), sc, sh, w)


def _compress_kernel(x_ref, pos_ref, w1_ref, w1g_ref, w2_ref, o_ref):
    nr = x_ref.shape[1] // CMP_STRIDE
    xs = jnp.concatenate([x_ref[0, pl.ds(rho, nr, stride=CMP_STRIDE), :].astype(BF16) for rho in range(CMP_STRIDE)],
                         axis=1)
    posb = jnp.dot(pos_ref[0], w1_ref[0], preferred_element_type=F32)[0:1]
    for g in range(NSA_GROUPS):
        top = jnp.dot(xs, w1g_ref[0, g, 0], preferred_element_type=F32)
        bot = jnp.dot(xs, w1g_ref[0, g, 1], preferred_element_type=F32)
        hid = top + pltpu.roll(bot, nr - 1, 0) + posb
        o_ref[0, 0, g] = _mm(_silu(hid), w2_ref[0]).astype(o_ref.dtype)


def _compress_call(main3, pos, w1, w1g, w2):
    b, seq, _ = main3.shape
    nr = seq // CMP_STRIDE
    two, l64, hid = w1.shape
    return pl.pallas_call(
        _compress_kernel,
        grid=(two, b),
        in_specs=[pl.BlockSpec((1, seq, LANES), lambda a, i: (i, 0, COL_KC // LANES + a)),
                  pl.BlockSpec((1, SUBLANES, l64), lambda a, i: (a, 0, 0)),
                  pl.BlockSpec((1, l64, hid), lambda a, i: (a, 0, 0)),
                  pl.BlockSpec((1,) + w1g.shape[1:], lambda a, i: (a, 0, 0, 0, 0)),
                  pl.BlockSpec((1, hid, LANES), lambda a, i: (a, 0, 0))],
        out_specs=pl.BlockSpec((1, 1, NSA_GROUPS, nr, LANES), lambda a, i: (a, i, 0, 0, 0)),
        out_shape=jax.ShapeDtypeStruct((two, b, NSA_GROUPS, nr, LANES), BF16),
        compiler_params=_cparams(("parallel", "parallel")),
        name="compress",
    )(main3, pos, w1, w1g, w2)


def _gate_cols(gate_logits, branch):
    sg = jax.nn.sigmoid(gate_logits)
    return [sg[:, 3 * r + branch:3 * r + branch + 1] for r in range(NSA_REP)]


def _stack_heads_dup(q):
    qa = q[:, :LANES]
    qb = q[:, LANES:]
    lo = lax.broadcasted_iota(jnp.int32, qa.shape, 1) < NSA_DH
    z = jnp.zeros_like(qa)
    return jnp.concatenate([jnp.where(lo, qa, z), jnp.where(lo, z, qa),
                            jnp.where(lo, qb, z), jnp.where(lo, z, qb)], axis=0)


def _unstack_heads_dup(o, gates):
    nq = o.shape[0] // NSA_REP
    lo = lax.broadcasted_iota(jnp.int32, (nq, LANES), 1) < NSA_DH
    a = jnp.where(lo, gates[0] * o[0:nq], gates[1] * o[nq:2 * nq])
    b = jnp.where(lo, gates[2] * o[2 * nq:3 * nq], gates[3] * o[3 * nq:])
    return jnp.concatenate([a, b], axis=1)


def _stack_heads_pos(q, slopes):
    qa = q[:, :LANES]
    qb = q[:, LANES:]
    lane = lax.broadcasted_iota(jnp.int32, qa.shape, 1)
    lo = lane < NSA_DH
    heads = [qa, pltpu.roll(qa, NSA_DH, 1), qb, pltpu.roll(qb, NSA_DH, 1)]
    out = []
    for r in range(NSA_REP):
        extra = jnp.where(lane == LANE_POS_HI, slopes[r] * LANES, jnp.where(lane == LANE_POS_LO, slopes[r], 0.0))
        out.append(jnp.where(lo, heads[r], extra))
    return jnp.concatenate(out, axis=0)


def _finish_heads(acc, gates):
    nq = acc.shape[0] // NSA_REP
    o = acc * pltpu.roll(1.0 / acc, NSA_DH, 1)
    lo = lax.broadcasted_iota(jnp.int32, (nq, LANES), 1) < NSA_DH
    hs = [gates[r] * o[r * nq:(r + 1) * nq] for r in range(NSA_REP)]
    a = jnp.where(lo, hs[0], pltpu.roll(hs[1], NSA_DH, 1))
    b = jnp.where(lo, hs[2], pltpu.roll(hs[3], NSA_DH, 1))
    return jnp.concatenate([a, b], axis=1)


def _select_kernel(slopes_ref, q_ref, kc_ref, vc_ref, gates_ref, ov_ref, sel_ref, act_ref, yc_ref):
    step = pl.program_id(1)
    nsteps = pl.num_programs(1)
    nb_all, ncp_all = ov_ref.shape
    for k in range(SELECT_BUCKETS):
        @pl.when((step * SELECT_BUCKETS) // nsteps == k)
        def _(k=k):
            for u in range(SELECT_TILES):
                _select_prefix(slopes_ref, q_ref, kc_ref, vc_ref, gates_ref, ov_ref, sel_ref, act_ref, yc_ref, tile=u,
                               nb=(k + 1) * nb_all // SELECT_BUCKETS, ncp=(k + 1) * ncp_all // SELECT_BUCKETS)


def _select_prefix(slopes_ref, q_ref, kc_ref, vc_ref, gates_ref, ov_ref, sel_ref, act_ref, yc_ref, *, tile, nb, ncp):
    t0 = (pl.program_id(1) * SELECT_TILES + tile) * Q_TILE
    nq = Q_TILE
    rows = slice(tile * nq, (tile + 1) * nq)
    nb_all = ov_ref.shape[0]
    c_end = lax.broadcasted_iota(jnp.int32, (nq, ncp), 1) * CMP_STRIDE + (CMP_LEN - 1)
    tq = t0 + lax.broadcasted_iota(jnp.int32, (nq, ncp), 0)
    vis = c_end <= tq
    any_vis = tq[:, 0:1] >= CMP_LEN - 1
    rel = (c_end[0:1] - t0).astype(F32)
    ov = ov_ref[:nb, :ncp]
    jb = lax.broadcasted_iota(jnp.int32, (nb, nq), 0)
    tql = t0 + lax.broadcasted_iota(jnp.int32, (nb, nq), 1)
    valid = jb * SEL_LEN <= tql
    forced = (jb == 0) | (jb == jnp.right_shift(tql, SEL_SHIFT))
    scores = []
    for g in range(NSA_GROUPS):
        qcols = slice(g * NSA_REP * NSA_DH, (g + 1) * NSA_REP * NSA_DH)
        qs = _stack_heads_dup(q_ref[rows, qcols] * (NSA_DH ** -0.5)).astype(BF16)
        s = _mm_nt(qs, kc_ref[0, g, :ncp, :])
        psum = jnp.zeros((nq, ncp), F32)
        ps = []
        for r in range(NSA_REP):
            m_r = slopes_ref[g * NSA_REP + r]
            sr = jnp.where(vis, s[r * nq:(r + 1) * nq] + m_r * rel, NEG)
            mx = jnp.max(sr, axis=-1, keepdims=True)
            e = jnp.exp(sr - mx)
            den = jnp.sum(e, axis=-1, keepdims=True)
            p = e * jnp.where(any_vis, 1.0 / den, 0.0)
            psum = psum + p
            ps.append(p.astype(BF16))
        o_c = jnp.dot(jnp.concatenate(ps, axis=0), vc_ref[0, g, :ncp, :], preferred_element_type=F32)
        yc_ref[rows, qcols] = _unstack_heads_dup(o_c, _gate_cols(gates_ref[rows, g * LANES:(g + 1) * LANES], 0))
        p_hi = psum.astype(BF16)
        p_lo = (psum - p_hi.astype(F32)).astype(BF16)
        imp_t = _mm_nt(ov, p_hi) + _mm_nt(ov, p_lo)
        scores.append(jnp.where(valid, imp_t + FORCE_BONUS * forced.astype(F32), NEG))

    for _ in range(min(TOP_N, nb)):
        for g in range(NSA_GROUPS):
            mx = jnp.max(scores[g], axis=0, keepdims=True)
            idx = jnp.min(jnp.where(scores[g] == mx, jb, nb), axis=0, keepdims=True)
            scores[g] = jnp.where(jb == idx, -jnp.inf, scores[g])
    for g in range(NSA_GROUPS):
        sel_t = jnp.where((scores[g] == -jnp.inf) & valid, 1.0, 0.0)
        if nb < nb_all:
            sel_t = jnp.concatenate([sel_t, jnp.zeros((nb_all - nb, nq), F32)], axis=0)
        sel_q = sel_t.T
        sel_ref[0, g, rows, :] = sel_q.astype(sel_ref.dtype)
        act_ref[0, g, tile] = jnp.max(sel_q, axis=0, keepdims=True)


def _select_call(slopes, main, kcmp, vcmp, ov, bsz, seq):
    nqt = seq // Q_TILE
    nsteps = nqt // SELECT_TILES
    tq = Q_TILE * SELECT_TILES
    nb, ncp = ov.shape
    row = lambda b, q: b * nsteps + q
    gw = NSA_GROUPS * LANES
    return pl.pallas_call(
        _select_kernel,
        grid=(bsz, nsteps),
        in_specs=[pl.BlockSpec(memory_space=pltpu.SMEM),
                  pl.BlockSpec((tq, NSA_Q_W), lambda b, q: (row(b, q), COL_Q // NSA_Q_W)),
                  pl.BlockSpec((1, NSA_GROUPS, ncp, LANES), lambda b, q: (b, 0, 0, 0)),
                  pl.BlockSpec((1, NSA_GROUPS, ncp, LANES), lambda b, q: (b, 0, 0, 0)),
                  pl.BlockSpec((tq, gw), lambda b, q: (row(b, q), COL_GATE // gw)),
                  pl.BlockSpec((nb, ncp), lambda b, q: (0, 0))],
        out_specs=[pl.BlockSpec((1, NSA_GROUPS, tq, nb), lambda b, q: (b, 0, q, 0)),
                   pl.BlockSpec((1, NSA_GROUPS, SELECT_TILES, 1, nb), lambda b, q: (b, 0, q, 0, 0)),
                   pl.BlockSpec((tq, NSA_Q_W), lambda b, q: (row(b, q), 0))],
        out_shape=[jax.ShapeDtypeStruct((bsz, NSA_GROUPS, seq, nb), BF16),
                   jax.ShapeDtypeStruct((bsz, NSA_GROUPS, nqt, 1, nb), F32),
                   jax.ShapeDtypeStruct((bsz * seq, NSA_Q_W), F32)],
        compiler_params=_cparams(("parallel", "parallel")),
        name="select",
    )(slopes, main, kcmp, vcmp, main, ov)


def _attn_kernel(lst_ref, cnt_ref, slopes_ref, q_ref, sel_ref, ks_ref, vs_ref, kw_ref, vw_ref, gates_ref, yc_ref,
                 et_ref, o_ref, s_scr, p_scr, m_scr, a_scr, acc_scr, *, chunk):
    b = pl.program_id(0)
    g = pl.program_id(1)
    step = pl.program_id(2)
    tiles_per_step = ATTN_TILES * ATTN_GROUPS
    nqt = pl.num_programs(2) * tiles_per_step
    nq = ATTN_Q_TILE
    nw = WINDOW + nq
    nch_total = ks_ref.shape[0] // chunk
    slopes = [slopes_ref[g * NSA_REP + r] for r in range(NSA_REP)]

    def values(p, c):
        start = pl.multiple_of(c * chunk, chunk)
        return jnp.dot(p, vs_ref[pl.ds(start, chunk), :], preferred_element_type=F32)

    def row_max(s):
        return jnp.broadcast_to(jnp.max(s, axis=-1, keepdims=True), (s.shape[0], LANES))

    def lane_tile(m, width):
        return jnp.concatenate([m] * (width // LANES), axis=1)

    def masked(s, ok):
        return jnp.concatenate([jnp.where(ok, s[r * nq:(r + 1) * nq], NEG) for r in range(NSA_REP)], axis=0)

    class Group:
        pass

    def setup(k):
        gr = Group()
        gr.tiles = range(ATTN_TILES)
        gr.scr = [(k % 2) * ATTN_TILES + u for u in gr.tiles]
        local = [k * ATTN_TILES + u for u in gr.tiles]
        gr.rws = [slice(v * nq, (v + 1) * nq) for v in local]
        qts = [step * tiles_per_step + v for v in local]
        gr.t0s = [qt * nq for qt in qts]
        gr.qs = [_stack_heads_pos(q_ref[rw, :] * (NSA_DH ** -0.5), slopes).astype(BF16) for rw in gr.rws]
        pens = [((sel_ref[0, 0, rw, :].astype(F32) - 1.0) * (-NEG)).astype(BF16) for rw in gr.rws]
        gr.qx = [jnp.concatenate([gr.qs[u], jnp.concatenate([pens[u]] * NSA_REP, axis=0)], axis=1) for u in gr.tiles]
        gr.bases = [((b * NSA_GROUPS + g) * nqt + qt) * nch_total for qt in qts]
        gr.cnts = [cnt_ref[(b * NSA_GROUPS + g) * nqt + qt] for qt in qts]
        for u in gr.tiles:
            su = gr.scr[u]
            m_scr[su] = jnp.full(m_scr.shape[1:], NEG, F32)
            a_scr[su] = jnp.ones(a_scr.shape[1:], F32)
            acc_scr[su] = jnp.zeros(acc_scr.shape[1:], F32)
            p_scr[su, 1] = jnp.zeros(p_scr.shape[2:], BF16)
        for u in gr.tiles:
            s_scr[gr.scr[u], 0] = logits(gr, u, lst_ref[gr.bases[u]])
        return gr

    def logits(gr, u, c):
        start = pl.multiple_of(c * chunk, chunk)
        kx = jnp.concatenate([ks_ref[pl.ds(start, chunk), :], et_ref[pl.ds(start, chunk), :]], axis=1)
        return _mm_nt(gr.qx[u], kx)

    def flash_step(gr, us, i, slot):
        for u in us:
            su = gr.scr[u]
            acc_scr[su] = a_scr[su] * acc_scr[su] + values(p_scr[su, 1 - slot],
                                                           lst_ref[gr.bases[u] + jnp.maximum(i - 1, 0)])
        for u in us:
            su = gr.scr[u]
            s = s_scr[su, slot]
            m_prev = m_scr[su]
            m_new = jnp.maximum(m_prev, row_max(s))
            a_scr[su] = jnp.exp(m_prev - m_new)
            p_scr[su, slot] = jnp.exp(s - lane_tile(m_new, chunk)).astype(BF16)
            m_scr[su] = m_new
        for u in us:
            s_scr[gr.scr[u], 1 - slot] = logits(gr, u, lst_ref[gr.bases[u] + i + 1])

    def flash_loops(gr):
        def single_steps(us):
            def body(i, carry):
                flash_step(gr, us, i, i & 1)
                return carry
            return body

        def double_steps(us):
            def body(j, carry):
                flash_step(gr, us, 2 * j, 0)
                flash_step(gr, us, 2 * j + 1, 1)
                return carry
            return body

        joint = functools.reduce(jnp.minimum, gr.cnts)
        lax.fori_loop(0, joint // 2, double_steps(list(gr.tiles)), 0)
        lax.fori_loop(joint - (joint & 1), joint, single_steps(list(gr.tiles)), 0)
        for u in gr.tiles:
            lax.fori_loop(joint, gr.cnts[u], single_steps([u]), 0)

    def finish(gr):
        tiles = gr.tiles
        scr = gr.scr
        slots = [gr.cnts[u] & 1 for u in tiles]
        c_diag = [gr.t0s[u] // chunk for u in tiles]
        acc = [a_scr[scr[u]] * acc_scr[scr[u]]
               + values(p_scr[scr[u], 1 - slots[u]], lst_ref[gr.bases[u] + jnp.maximum(gr.cnts[u] - 1, 0)])
               for u in tiles]
        wbase = [pl.multiple_of(jnp.maximum(gr.t0s[u] - WINDOW, 0), nq) for u in tiles]
        s_win = [_mm_nt(gr.qs[u], kw_ref[pl.ds(wbase[u], nw), :]) for u in tiles]
        s_diag = []
        for u in tiles:
            pos = c_diag[u] * chunk + lax.broadcasted_iota(jnp.int32, (nq, chunk), 1)
            tq = gr.t0s[u] + lax.broadcasted_iota(jnp.int32, (nq, chunk), 0)
            s_diag.append(masked(s_scr[scr[u], slots[u]], pos <= tq))
        m_prev = [m_scr[scr[u]] for u in tiles]
        m_new = [jnp.maximum(m_prev[u], row_max(s_diag[u])) for u in tiles]
        acc = [jnp.exp(m_prev[u] - m_new[u]) * acc[u]
               + values(jnp.exp(s_diag[u] - lane_tile(m_new[u], chunk)).astype(BF16), c_diag[u]) for u in tiles]
        p_win = []
        for u in tiles:
            pos = wbase[u] + lax.broadcasted_iota(jnp.int32, (nq, nw), 1)
            tq = gr.t0s[u] + lax.broadcasted_iota(jnp.int32, (nq, nw), 0)
            s = masked(s_win[u], (pos <= tq) & (pos > tq - WINDOW))
            p_win.append(jnp.exp(s - jnp.max(s, axis=-1, keepdims=True)).astype(BF16))
        acc_w = [jnp.dot(p_win[u], vw_ref[pl.ds(wbase[u], nw), :], preferred_element_type=F32) for u in tiles]
        for u in tiles:
            gates = gates_ref[gr.rws[u], :]
            o_ref[gr.rws[u], :] = (yc_ref[gr.rws[u], :] + _finish_heads(acc[u], _gate_cols(gates, 1))
                                   + _finish_heads(acc_w[u], _gate_cols(gates, 2))).astype(o_ref.dtype)

    cur = setup(0)
    for k in range(ATTN_GROUPS):
        flash_loops(cur)
        nxt = setup(k + 1) if k + 1 < ATTN_GROUPS else None
        finish(cur)
        cur = nxt


def _chunk_lists(act, seq, chunk):
    bsz, ngrp = act.shape[:2]
    nqt = seq // ATTN_Q_TILE
    nch = seq // chunk
    active = act.reshape(bsz, ngrp, nqt, ATTN_Q_TILE // Q_TILE, nch, chunk // SEL_LEN).max(axis=(3, 5)) > 0.0
    c_diag = (jnp.arange(nqt) * ATTN_Q_TILE) // chunk
    active = active & (jnp.arange(nch)[None, :] < c_diag[:, None])
    cnt = active.sum(axis=-1).astype(jnp.int32)
    order = jnp.argsort(jnp.logical_not(active), axis=-1, stable=True).astype(jnp.int32)
    lst = jnp.where(jnp.arange(nch) < cnt[..., None], order, c_diag[:, None].astype(jnp.int32))
    return lst.reshape(-1), cnt.reshape(-1)


def _attn_call(lst, cnt, slopes, main, sel, kv, yc, et, bsz, seq, chunk):
    tq = ATTN_Q_TILE * ATTN_TILES * ATTN_GROUPS
    nset = 2 * ATTN_TILES
    nqt = seq // tq
    nb = sel.shape[-1]
    rows = NSA_REP * ATTN_Q_TILE
    row = lambda b, g, q: b * nqt + q
    kvspec = lambda blk: pl.BlockSpec((seq, LANES), lambda b, g, q, *_: (b, N_MG // LANES + blk + g))
    grid_spec = pltpu.PrefetchScalarGridSpec(
        num_scalar_prefetch=2,
        grid=(bsz, NSA_GROUPS, nqt),
        in_specs=[pl.BlockSpec(memory_space=pltpu.SMEM),
                  pl.BlockSpec((tq, 256), lambda b, g, q, *_: (row(b, g, q), COL_Q // 256 + g)),
                  pl.BlockSpec((1, 1, tq, nb), lambda b, g, q, *_: (b, g, q, 0)),
                  kvspec(KVB_KS), kvspec(KVB_VS), kvspec(KVB_KW), kvspec(KVB_VW),
                  pl.BlockSpec((tq, LANES), lambda b, g, q, *_: (row(b, g, q), COL_GATE // LANES + g)),
                  pl.BlockSpec((tq, 256), lambda b, g, q, *_: (row(b, g, q), g)),
                  pl.BlockSpec(et.shape, lambda b, g, q, *_: (0, 0))],
        out_specs=pl.BlockSpec((tq, 256), lambda b, g, q, *_: (row(b, g, q), g)),
        scratch_shapes=[pltpu.VMEM((nset, 2, rows, chunk), F32),
                        pltpu.VMEM((nset, 2, rows, chunk), BF16),
                        pltpu.VMEM((nset, rows, LANES), F32),
                        pltpu.VMEM((nset, rows, LANES), F32),
                        pltpu.VMEM((nset, rows, LANES), F32)],
    )
    return pl.pallas_call(
        functools.partial(_attn_kernel, chunk=chunk),
        grid_spec=grid_spec,
        out_shape=jax.ShapeDtypeStruct((bsz * seq, NSA_Q_W), BF16),
        compiler_params=_cparams(("parallel", "parallel", "parallel")),
        name="attn",
    )(lst, cnt, slopes, main, sel, kv, kv, kv, kv, main, yc, et)


def _pool_kernel(x_ref, halo_ref, w_ref, sc_ref, o_ref):
    s = pl.program_id(1)
    ts = x_ref.shape[0]
    x = x_ref[...]
    halo = jnp.where(s > 0, halo_ref[...], 0.0)
    t = s * ts + lax.broadcasted_iota(jnp.int32, (ts, 1), 0)
    for gi, win in enumerate(POOL_WINDOWS):
        lanes = slice(gi * POOL_GROUP_DIM, (gi + 1) * POOL_GROUP_DIM)
        ext = jnp.concatenate([halo[:, lanes], x[:, lanes]], axis=0)
        acc = ext
        span = 1
        while span < win:
            acc = acc + pltpu.roll(acc, span, 0)
            span *= 2
        cnt = jnp.minimum(t + 1, win).astype(F32)
        y = acc[POOL_HALO:] / cnt - x[:, lanes]
        o_ref[:, lanes] = (_mm(y, w_ref[gi]) * sc_ref[:, lanes]).astype(o_ref.dtype)


def _pool_call(main, w, scale, bsz, seq):
    ts = min(POOL_TS, seq)
    per_b = seq // ts
    halo_per_tile = ts // POOL_HALO
    return pl.pallas_call(
        _pool_kernel,
        grid=(bsz, per_b),
        in_specs=[pl.BlockSpec((ts, POOL_W), lambda b, s: (b * per_b + s, COL_POOL // POOL_W)),
                  pl.BlockSpec((POOL_HALO, POOL_W),
                               lambda b, s: (jnp.maximum((b * per_b + s) * halo_per_tile - 1, 0), COL_POOL // POOL_W)),
                  pl.BlockSpec(w.shape, lambda b, s: (0, 0, 0)),
                  pl.BlockSpec((1, POOL_W), lambda b, s: (0, 0))],
        out_specs=pl.BlockSpec((ts, POOL_W), lambda b, s: (b * per_b + s, 0)),
        out_shape=jax.ShapeDtypeStruct((bsz * seq, POOL_W), BF16),
        compiler_params=_cparams(("parallel", "parallel")),
        name="pool",
    )(main, main, w, scale.reshape(1, POOL_W))


def _unit_lower_inverses(nms, ri, ci):
    c = nms[0].shape[0]
    eye = jnp.where(ri == ci, 1.0, 0.0)
    in8 = jnp.right_shift(ri, 3) == jnp.right_shift(ci, 3)
    n8 = [jnp.where(in8, nm, 0.0) for nm in nms]
    ps = [eye - a for a in n8]
    ms = [_mm(a, a) for a in n8]
    ps = [p + _mm(p, m) for p, m in zip(ps, ms)]
    ms = [_mm(m, m) for m in ms]
    ps = [p + _mm(p, m) for p, m in zip(ps, ms)]
    shift = 3
    while (1 << shift) < c:
        rb = jnp.right_shift(ri, shift)
        cb = jnp.right_shift(ci, shift)
        lower_left = ((rb & 1) == 1) & (cb == rb - 1)
        ts = [_mm(p, jnp.where(lower_left, nm, 0.0)) for p, nm in zip(ps, nms)]
        ps = [p - _mm(t, p) for p, t in zip(ps, ts)]
        shift += 1
    return ps


def _row_cumsum(x, ri):
    span = 1
    while span < x.shape[0]:
        x = x + jnp.where(ri >= span, pltpu.roll(x, span, 0), 0.0)
        span *= 2
    return x


def _dn_kernel(qkv_ref, halo_ref, z_ref, gba_ref, cw_ref, alog_ref, dtb_ref, ng_ref, o_ref, st_scr):
    s = pl.program_id(1)
    nbatch, c = qkv_ref.shape[0], qkv_ref.shape[1]

    @pl.when(s == 0)
    def _():
        st_scr[...] = jnp.zeros(st_scr.shape, F32)

    ri = lax.broadcasted_iota(jnp.int32, (c, c), 0)
    ci = lax.broadcasted_iota(jnp.int32, (c, c), 1)
    tril = ri >= ci
    strict = ri > ci
    cw = cw_ref[...]
    ng = ng_ref[...]
    chains = [(bi, h) for bi in range(nbatch) for h in range(DN_HEADS)]
    qn, kn, kb, vb, gcb, lmask = [], [], [], [], [], []
    for bi in range(nbatch):
        halo = jnp.where(s > 0, halo_ref[bi], 0.0)
        ext = jnp.concatenate([halo, qkv_ref[bi]], axis=0)
        conv = cw[DN_CONV - 1:DN_CONV] * ext
        for tap in range(1, DN_CONV):
            conv = conv + cw[DN_CONV - 1 - tap:DN_CONV - tap] * pltpu.roll(ext, tap, 0)
        act = _silu(conv[DN_HALO:])

        gba = gba_ref[bi]
        beta_all = jax.nn.sigmoid(gba)
        xs = gba + dtb_ref[...]
        softplus = jnp.maximum(xs, 0.0) + jnp.log(1.0 + jnp.exp(-jnp.abs(xs)))
        gc_all = _row_cumsum(-jnp.exp(alog_ref[...]) * softplus, ri)
        gct_all = gc_all.T
        for h in range(DN_HEADS):
            q = act[:, h * DN_DK:(h + 1) * DN_DK]
            k = act[:, DN_W + h * DN_DK:DN_W + (h + 1) * DN_DK]
            v = act[:, 2 * DN_W + h * DN_DK:2 * DN_W + (h + 1) * DN_DK]
            qn.append(q * lax.rsqrt(jnp.sum(q * q, axis=-1, keepdims=True) + EPS) * (DN_DK ** -0.5))
            kn.append(k * lax.rsqrt(jnp.sum(k * k, axis=-1, keepdims=True) + EPS))
            beta = beta_all[:, LANE_BETA + h:LANE_BETA + h + 1]
            gcb.append(jnp.broadcast_to(gc_all[:, LANE_DECAY + h:LANE_DECAY + h + 1], (c, DN_DK)))
            gct = jnp.broadcast_to(gct_all[LANE_DECAY + h:LANE_DECAY + h + 1, :], (c, c))
            lmask.append(jnp.where(tril, jnp.exp(jnp.where(tril, gcb[-1] - gct, 0.0)), 0.0))
            kb.append(kn[-1] * beta)
            vb.append(v * beta)

    n = range(len(chains))
    kk = [_mm_nt(kb[i], kn[i]) for i in n]
    tinv = _unit_lower_inverses([jnp.where(strict, kk[i] * lmask[i], 0.0) for i in n], ri, ci)
    egc = [jnp.exp(gcb[i]) for i in n]
    sol = [_mm(tinv[i], jnp.concatenate([vb[i], kb[i] * egc[i]], axis=1)) for i in n]
    aqk = [_mm_nt(qn[i], kn[i]) * lmask[i] for i in n]
    state = [st_scr[i] for i in n]
    glast = [gcb[i][c - 1:c, :] for i in n]
    v_new = [sol[i][:, :DN_DK] - _mm(sol[i][:, DN_DK:], state[i]) for i in n]
    o_state = [_mm(qn[i] * egc[i], state[i]) for i in n]
    o = [o_state[i] + _mm(aqk[i], v_new[i]) for i in n]
    upd = [_mm_tn(kn[i] * jnp.exp(glast[i] - gcb[i]), v_new[i]) for i in n]
    for i, (bi, h) in enumerate(chains):
        st_scr[i] = state[i] * jnp.exp(glast[i]) + upd[i]
        on = o[i] * lax.rsqrt(jnp.mean(o[i] * o[i], axis=-1, keepdims=True) + EPS) * ng
        o_ref[bi, :, h * DN_DK:(h + 1) * DN_DK] = (
            on * _silu(z_ref[bi, :, h * DN_DK:(h + 1) * DN_DK])).astype(o_ref.dtype)


def _dn_call(main3, conv_w, alog_row, dtb_row, norm_g):
    bsz, seq, _ = main3.shape
    c = DN_CHUNK
    nbt = DN_BATCH if bsz % DN_BATCH == 0 else 1
    halo_per_chunk = c // DN_HALO
    w3 = 3 * DN_W
    return pl.pallas_call(
        _dn_kernel,
        grid=(bsz // nbt, seq // c),
        in_specs=[pl.BlockSpec((nbt, c, w3), lambda b, s: (b, s, COL_QKV // w3)),
                  pl.BlockSpec((nbt, DN_HALO, w3),
                               lambda b, s: (b, jnp.maximum(s * halo_per_chunk - 1, 0), COL_QKV // w3)),
                  pl.BlockSpec((nbt, c, DN_W), lambda b, s: (b, s, COL_Z // DN_W)),
                  pl.BlockSpec((nbt, c, LANES), lambda b, s: (b, s, COL_GATE // LANES)),
                  pl.BlockSpec((SUBLANES, w3), lambda b, s: (0, 0)),
                  pl.BlockSpec((1, LANES), lambda b, s: (0, 0)),
                  pl.BlockSpec((1, LANES), lambda b, s: (0, 0)),
                  pl.BlockSpec((1, DN_DK), lambda b, s: (0, 0))],
        out_specs=pl.BlockSpec((nbt, c, DN_W), lambda b, s: (b, s, 0)),
        out_shape=jax.ShapeDtypeStruct((bsz, seq, DN_W), BF16),
        scratch_shapes=[pltpu.VMEM((nbt * DN_HEADS, DN_DK, DN_DK), F32)],
        compiler_params=_cparams(("parallel", "arbitrary")),
        name="dn",
    )(main3, main3, main3, main3, conv_w, alog_row, dtb_row, norm_g.reshape(1, DN_DK))


def _merge_kernel(ya_ref, yb_ref, yc_ref, mg_ref, x_ref, gt_ref, wa_ref, wb_ref, wc_ref, wo_ref, o_ref):
    d = x_ref.shape[1]
    merged = None
    for i, (y_ref, w_ref) in enumerate(((ya_ref, wa_ref), (yb_ref, wb_ref), (yc_ref, wc_ref))):
        gate = jax.nn.sigmoid(mg_ref[:, i * d:(i + 1) * d].astype(F32))
        term = gate * jnp.dot(y_ref[...], w_ref[...], preferred_element_type=F32)
        merged = term if merged is None else merged + term
    y = jnp.dot(merged.astype(BF16), wo_ref[...], preferred_element_type=F32)
    o_ref[...] = x_ref[...] + gt_ref[0] * y


def _merge_call(ya, yb, yc, mg, x2, gt, wa, wb, wc, wo, seq):
    t, d = x2.shape
    tm = min(MERGE_TM, seq)
    per_b = seq // tm
    full = lambda a: pl.BlockSpec(a.shape, lambda i: (0, 0), pipeline_mode=pl.Buffered(1))
    return pl.pallas_call(
        _merge_kernel,
        grid=(t // tm,),
        in_specs=[pl.BlockSpec((tm, ya.shape[1]), lambda i: (i, 0)),
                  pl.BlockSpec((tm, yb.shape[1]), lambda i: (i, 0)),
                  pl.BlockSpec((tm, yc.shape[1]), lambda i: (i, 0)),
                  pl.BlockSpec((tm, N_MG), lambda i: (i, 0)),
                  pl.BlockSpec((tm, d), lambda i: (i, 0)),
                  pl.BlockSpec((1, 1, d), lambda i: (i // per_b, 0, 0)),
                  full(wa), full(wb), full(wc), full(wo)],
        out_specs=pl.BlockSpec((tm, d), lambda i: (i, 0)),
        out_shape=jax.ShapeDtypeStruct((t, d), F32),
        compiler_params=_cparams(("parallel",)),
        name="merge",
    )(ya, yb, yc, mg, x2, gt, wa, wb, wc, wo)


def _mlp_kernel(x_ref, g_ref, sc_ref, sh_ref, gt_ref, w1_ref, w2_ref, fg_ref, o_ref, *, final):
    x = x_ref[...]
    h = _norm_mod(x, g_ref[...], sc_ref[0], sh_ref[0]).astype(BF16)
    acc = None
    for c0 in range(0, w1_ref.shape[1], MLP_TF):
        a = jnp.maximum(jnp.dot(h, w1_ref[:, c0:c0 + MLP_TF], preferred_element_type=F32), 0.0)
        part = jnp.dot((a * a).astype(BF16), w2_ref[c0:c0 + MLP_TF, :], preferred_element_type=F32)
        acc = part if acc is None else acc + part
    y = x + gt_ref[0] * acc
    if final:
        ms = jnp.mean(y * y, axis=-1, keepdims=True)
        y = y * lax.rsqrt(ms + EPS) * fg_ref[...]
    o_ref[...] = y


def _mlp_call(x2, g, sc, sh, gt, w1, w2, layer, fg, seq, final):
    t, d = x2.shape
    f = w1.shape[2]
    assert f % MLP_TF == 0
    tm = min(MLP_TM, seq)
    per_b = seq // tm
    mod = lambda: pl.BlockSpec((1, 1, d), lambda i: (i // per_b, 0, 0))
    resident = pl.Buffered(1)
    return pl.pallas_call(
        functools.partial(_mlp_kernel, final=final),
        grid=(t // tm,),
        in_specs=[pl.BlockSpec((tm, d), lambda i: (i, 0)),
                  pl.BlockSpec((1, d), lambda i: (0, 0)),
                  mod(), mod(), mod(),
                  pl.BlockSpec((None, d, f), lambda i: (layer, 0, 0), pipeline_mode=resident),
                  pl.BlockSpec((None, f, d), lambda i: (layer, 0, 0), pipeline_mode=resident),
                  pl.BlockSpec((1, d), lambda i: (0, 0))],
        out_specs=pl.BlockSpec((tm, d), lambda i: (i, 0)),
        out_shape=jax.ShapeDtypeStruct((t, d), F32),
        compiler_params=_cparams(("parallel",)),
        name="mlp",
    )(x2, g.reshape(1, d), sc, sh, gt, w1, w2, fg.reshape(1, d))


def _split_w_in(w_in):
    offs = [0]
    for sz in IN_SIZES:
        offs.append(offs[-1] + sz)
    return [w_in[:, offs[i]:offs[i + 1]] for i in range(len(IN_SIZES))]


def _layout_w_in(w_in):
    nq, nkc, nvc, nks, nvs, nkw, nvw, ngate, pin, dqkv, dz, dbeta, da, mg = _split_w_in(w_in)
    d = w_in.shape[0]
    per_g = 3 * NSA_REP
    zeros = lambda n: jnp.zeros((d, n), w_in.dtype)
    gate0 = jnp.concatenate([ngate[:, :per_g], zeros(LANE_BETA - per_g), dbeta, da,
                             zeros(LANES - LANE_DECAY - DN_HEADS)], axis=1)
    gate1 = jnp.concatenate([ngate[:, per_g:], zeros(LANES - per_g)], axis=1)
    main = jnp.concatenate([nq, pin, dz, dqkv, gate0, gate1, nkc, nvc], axis=1)

    def per_group(w):
        pad = zeros(LANES - NSA_DH)
        return jnp.concatenate([w[:, :NSA_DH], pad, w[:, NSA_DH:], pad], axis=1)

    kv = jnp.concatenate([per_group(nks), per_group(nvs), per_group(nkw), per_group(nvw)], axis=1)
    return main.astype(BF16), jnp.concatenate([mg, kv], axis=1).astype(BF16)


def _group_lane_w1(w1):
    two, _, hid = w1.shape
    halves = w1.reshape(two, 2, CMP_STRIDE, NSA_DH, hid)
    out = []
    for g in range(NSA_GROUPS):
        padded = jnp.zeros((two, 2, CMP_STRIDE, LANES, hid), w1.dtype).at[:, :, :, g * NSA_DH:(g + 1) * NSA_DH].set(halves)
        out.append(padded.reshape(two, 2, CMP_STRIDE * LANES, hid))
    return jnp.stack(out, axis=1)


def _nsa_constants(seq):
    nb = seq // SEL_LEN
    ncp = seq // CMP_STRIDE
    j = jnp.arange(nb)[:, None] * SEL_LEN
    i = jnp.arange(ncp)[None, :] * CMP_STRIDE
    overlap = jnp.clip(jnp.minimum(j + SEL_LEN, i + CMP_LEN) - jnp.maximum(j, i), 0) // CMP_STRIDE
    block_onehot = (jnp.arange(seq)[:, None] // SEL_LEN) == jnp.arange(nb)[None, :]
    return overlap.astype(BF16), block_onehot.astype(BF16)


def kernel(x, c, ada_w, ada_b, norm1_g, norm2_g, w_in, phi_k1, phi_k2, phi_v1, phi_v2, pos_k, pos_v, pool_w, pool_scale, dn_conv_w, dn_A_log, dn_dt_bias, dn_norm_g, w_branch_nsa, w_branch_pool, w_branch_dn, w_out, mlp_w1, mlp_w2, final_g):
    bsz, seq, d = x.shape
    assert d == D_MODEL and seq % max(PROJ_TM, MLP_TM, POOL_TS) == 0 and seq % KV_CHUNK == 0
    assert seq % (ATTN_Q_TILE * ATTN_TILES * ATTN_GROUPS) == 0
    assert (seq // Q_TILE) % (SELECT_BUCKETS * SELECT_TILES) == 0
    assert DN_CHUNK == DN_DK == LANES
    t = bsz * seq
    depth = ada_w.shape[0]
    chunk = KV_CHUNK

    slopes = 2.0 ** (-(8.0 / NSA_HEADS) * (jnp.arange(NSA_HEADS, dtype=F32) + 1.0))
    overlap, block_onehot = _nsa_constants(seq)
    c_pad = jnp.zeros((SUBLANES, d), F32).at[:bsz].set(c)
    ada_b3 = ada_b.reshape(depth, 1, ada_b.shape[1])
    mlp_w1_bf, mlp_w2_bf = mlp_w1.astype(BF16), mlp_w2.astype(BF16)
    x2 = x.reshape(t, d)

    for l in range(depth):
        mod = _mod_call(c_pad, ada_w, ada_b3, l)[:bsz]
        sh1, sc1, gt1, sh2, sc2, gt2 = [m.reshape(bsz, 1, d) for m in jnp.split(mod, 6, axis=-1)]

        w_main, w_mgkv = _layout_w_in(w_in[l])
        main = _proj_call(x2, norm1_g[l], sc1, sh1, w_main, seq, PROJ_MAIN_TM, F32, "proj_main")
        kv = mg = _proj_kv_call(x2, norm1_g[l], sc1, sh1, w_mgkv, seq)

        main3 = main.reshape(bsz, seq, N_MAIN)
        pos = jnp.zeros((2, SUBLANES, CMP_LEN * NSA_DH), F32).at[:, 0].set(
            jnp.stack([pos_k[l].reshape(-1), pos_v[l].reshape(-1)])).astype(BF16)
        w1 = jnp.stack([phi_k1[l], phi_v1[l]]).astype(BF16)
        w2 = jnp.stack([phi_k2[l], phi_v2[l]])
        w2 = jnp.concatenate([w2, w2], axis=-1).astype(BF16)
        cmp_kv = _compress_call(main3, pos, w1, _group_lane_w1(w1), w2)

        sel, act, yc = _select_call(slopes, main, cmp_kv[0], cmp_kv[1], overlap, bsz, seq)
        lst, cnt = _chunk_lists(act, seq, chunk)
        y_a = _attn_call(lst, cnt, slopes, main, sel, kv, yc, block_onehot, bsz, seq, chunk)

        y_b = _pool_call(main, pool_w[l].astype(BF16), pool_scale[l], bsz, seq)

        conv_w = jnp.zeros((SUBLANES, 3 * DN_W), F32).at[:DN_CONV].set(dn_conv_w[l])
        lane_row = lambda v: jnp.zeros((1, LANES), F32).at[0, LANE_DECAY:LANE_DECAY + DN_HEADS].set(v)
        y_c = _dn_call(main3, conv_w, lane_row(dn_A_log[l]), lane_row(dn_dt_bias[l]),
                       dn_norm_g[l]).reshape(t, DN_W)

        x2 = _merge_call(y_a, y_b, y_c, mg, x2, gt1,
                         w_branch_nsa[l].astype(BF16), w_branch_pool[l].astype(BF16),
                         w_branch_dn[l].astype(BF16), w_out[l].astype(BF16), seq)
        x2 = _mlp_call(x2, norm2_g[l], sc2, sh2, gt2, mlp_w1_bf, mlp_w2_bf, l,
                       final_g, seq, final=(l == depth - 1))
    return x2.reshape(bsz, seq, d)
```

```python
import functools

import jax
import jax.numpy as jnp
from jax import lax
from jax.experimental import pallas as pl
from jax.experimental.pallas import tpu as pltpu

F32 = jnp.float32
BF16 = jnp.bfloat16
HIGHEST = lax.Precision.HIGHEST

D_MODEL = 1024
NSA_HEADS = 8
NSA_GROUPS = 2
NSA_REP = NSA_HEADS // NSA_GROUPS
NSA_DH = 64
CMP_LEN = 32
CMP_STRIDE = 16
SEL_LEN = 64
SEL_SHIFT = 6
TOP_N = 16
WINDOW = 512
FORCE_BONUS = 1.0e4
POOL_WINDOWS = (2, 4, 8, 16)
POOL_GROUP_DIM = 128
POOL_W = len(POOL_WINDOWS) * POOL_GROUP_DIM
DN_HEADS = 4
DN_DK = 128
DN_W = DN_HEADS * DN_DK
DN_CONV = 4
EPS = 1e-6
NEG = -1e30
NSA_Q_W = NSA_HEADS * NSA_DH
NSA_KV_W = NSA_GROUPS * NSA_DH
IN_SIZES = (NSA_Q_W, NSA_KV_W, NSA_KV_W, NSA_KV_W, NSA_KV_W, NSA_KV_W, NSA_KV_W,
            3 * NSA_HEADS, POOL_W, 3 * DN_W, DN_W, DN_HEADS, DN_HEADS, 3 * D_MODEL)

LANES = 128
LANE_SHIFT = 7
SUBLANES = 8
VMEM_LIMIT_BYTES = 48 * 1024 * 1024

Q_TILE = 128
SELECT_BUCKETS = 4
SELECT_TILES = 4
ATTN_Q_TILE = 128
ATTN_TILES = 2
ATTN_GROUPS = 2
KV_CHUNK = 256
DN_CHUNK = 128
DN_BATCH = 4
PROJ_TM = 1024
MLP_TM = 1024
MLP_TF = 2048
MERGE_TM = 1024
POOL_TS = 1024
POOL_HALO = 16
DN_HALO = 8

COL_Q = 0
COL_POOL = 512
COL_Z = 1024
COL_QKV = 1536
COL_GATE = 3072
COL_KC = 3328
COL_VC = 3456
N_MAIN = 3584
PROJ_MAIN_TM = 1024
N_MG = 3072
LANE_BETA = 16
LANE_DECAY = 20
KVB_KS = 0
KVB_VS = 2
KVB_KW = 4
KVB_VW = 6
KVB_END = 8
N_KV = KVB_END * LANES
LANE_POS_HI = NSA_DH
LANE_POS_LO = NSA_DH + 1


def _cparams(sem):
    return pltpu.CompilerParams(dimension_semantics=sem, vmem_limit_bytes=VMEM_LIMIT_BYTES)


def _silu(v):
    return v * jax.nn.sigmoid(v)


def _mm(a, b):
    return jnp.dot(a.astype(BF16), b.astype(BF16), preferred_element_type=F32)


def _mm_nt(a, b):
    return lax.dot_general(a.astype(BF16), b.astype(BF16), (((1,), (1,)), ((), ())), preferred_element_type=F32)


def _mm_tn(a, b):
    return lax.dot_general(a.astype(BF16), b.astype(BF16), (((0,), (0,)), ((), ())), preferred_element_type=F32)


def _mod_kernel(c_ref, w_ref, b_ref, o_ref):
    cond = _silu(c_ref[...])
    o_ref[...] = jnp.dot(cond, w_ref[...], preferred_element_type=F32, precision=HIGHEST) + b_ref[...]


def _mod_call(c_pad, w, b, layer):
    _, d, n = w.shape
    tn = 1536
    return pl.pallas_call(
        _mod_kernel,
        grid=(n // tn,),
        in_specs=[pl.BlockSpec((SUBLANES, d), lambda j: (0, 0)),
                  pl.BlockSpec((None, d, tn), lambda j: (layer, 0, j)),
                  pl.BlockSpec((None, 1, tn), lambda j: (layer, 0, j))],
        out_specs=pl.BlockSpec((SUBLANES, tn), lambda j: (0, j)),
        out_shape=jax.ShapeDtypeStruct((SUBLANES, n), F32),
        compiler_params=_cparams(("parallel",)),
        name="mod",
    )(c_pad, w, b)


def _norm_mod(x, g, sc, sh):
    ms = jnp.mean(x * x, axis=-1, keepdims=True)
    return (x * lax.rsqrt(ms + EPS) * g) * (1.0 + sc) + sh


def _proj_kernel(x_ref, g_ref, sc_ref, sh_ref, w_ref, o_ref):
    h = _norm_mod(x_ref[...], g_ref[...], sc_ref[0], sh_ref[0]).astype(BF16)
    o_ref[...] = jnp.dot(h, w_ref[...], preferred_element_type=F32).astype(o_ref.dtype)


def _proj_kv_kernel(x_ref, g_ref, sc_ref, sh_ref, w_ref, o_ref, *, tiles_per_seq):
    h = _norm_mod(x_ref[...], g_ref[...], sc_ref[0], sh_ref[0]).astype(BF16)
    full = jnp.dot(h, w_ref[...], preferred_element_type=F32)
    o_ref[:, :N_MG] = full[:, :N_MG].astype(o_ref.dtype)
    acc = full[:, N_MG:]
    tm, n = acc.shape
    col = lax.broadcasted_iota(jnp.int32, (1, n), 1)
    blk = jnp.right_shift(col, LANE_SHIFT)
    lane = col & (LANES - 1)
    is_key = ((blk >= KVB_KS) & (blk < KVB_VS)) | ((blk >= KVB_KW) & (blk < KVB_VW))
    is_val = ((blk >= KVB_VS) & (blk < KVB_KW)) | ((blk >= KVB_VW) & (blk < KVB_END))
    ones_row = jnp.where(is_val & (lane >= NSA_DH), 1.0, 0.0)
    hi_row = jnp.where(is_key & (lane == LANE_POS_HI), 1.0, 0.0)
    lo_row = jnp.where(is_key & (lane == LANE_POS_LO), 1.0, 0.0)
    pos = (pl.program_id(0) % tiles_per_seq) * tm + lax.broadcasted_iota(jnp.int32, (tm, 1), 0)
    pos_hi = jnp.right_shift(pos, LANE_SHIFT).astype(F32)
    pos_lo = (pos & (LANES - 1)).astype(F32)
    o_ref[:, N_MG:] = (acc + ones_row + pos_hi * hi_row + pos_lo * lo_row).astype(o_ref.dtype)


def _proj_specs(tm, d, tn, per_b):
    return [pl.BlockSpec((tm, d), lambda i, j: (i, 0)),
            pl.BlockSpec((1, d), lambda i, j: (0, 0)),
            pl.BlockSpec((1, 1, d), lambda i, j: (i // per_b, 0, 0)),
            pl.BlockSpec((1, 1, d), lambda i, j: (i // per_b, 0, 0)),
            pl.BlockSpec((d, tn), lambda i, j: (0, j), pipeline_mode=pl.Buffered(1))]


def _proj_call(x2, g, sc, sh, w, seq, tm, out_dtype, name):
    t, d = x2.shape
    n = w.shape[1]
    tm = min(tm, seq)
    return pl.pallas_call(
        _proj_kernel,
        grid=(t // tm, 1),
        in_specs=_proj_specs(tm, d, n, seq // tm),
        out_specs=pl.BlockSpec((tm, n), lambda i, j: (i, j)),
        out_shape=jax.ShapeDtypeStruct((t, n), out_dtype),
        compiler_params=_cparams(("parallel", "arbitrary")),
        name=name,
    )(x2, g.reshape(1, d), sc, sh, w)


def _proj_kv_call(x2, g, sc, sh, w, seq):
    t, d = x2.shape
    n = w.shape[1]
    tm = min(PROJ_TM, seq)
    return pl.pallas_call(
        functools.partial(_proj_kv_kernel, tiles_per_seq=seq // tm),
        grid=(t // tm, 1),
        in_specs=_proj_specs(tm, d, n, seq // tm),
        out_specs=pl.BlockSpec((tm, n), lambda i, j: (i, j)),
        out_shape=jax.ShapeDtypeStruct((t, n), BF16),
        compiler_params=_cparams(("parallel", "arbitrary")),
        name="proj_kv",
    )(x2, g.reshape(1, d), sc, sh, w)


def _compress_kernel(x_ref, pos_ref, w1_ref, w1g_ref, w2_ref, o_ref):
    nr = x_ref.shape[1] // CMP_STRIDE
    xs = jnp.concatenate([x_ref[0, pl.ds(rho, nr, stride=CMP_STRIDE), :].astype(BF16) for rho in range(CMP_STRIDE)],
                         axis=1)
    posb = jnp.dot(pos_ref[0], w1_ref[0], preferred_element_type=F32)[0:1]
    for g in range(NSA_GROUPS):
        top = jnp.dot(xs, w1g_ref[0, g, 0], preferred_element_type=F32)
        bot = jnp.dot(xs, w1g_ref[0, g, 1], preferred_element_type=F32)
        hid = top + pltpu.roll(bot, nr - 1, 0) + posb
        o_ref[0, 0, g] = _mm(_silu(hid), w2_ref[0]).astype(o_ref.dtype)


def _compress_call(main3, pos, w1, w1g, w2):
    b, seq, _ = main3.shape
    nr = seq // CMP_STRIDE
    two, l64, hid = w1.shape
    return pl.pallas_call(
        _compress_kernel,
        grid=(two, b),
        in_specs=[pl.BlockSpec((1, seq, LANES), lambda a, i: (i, 0, COL_KC // LANES + a)),
                  pl.BlockSpec((1, SUBLANES, l64), lambda a, i: (a, 0, 0)),
                  pl.BlockSpec((1, l64, hid), lambda a, i: (a, 0, 0)),
                  pl.BlockSpec((1,) + w1g.shape[1:], lambda a, i: (a, 0, 0, 0, 0)),
                  pl.BlockSpec((1, hid, LANES), lambda a, i: (a, 0, 0))],
        out_specs=pl.BlockSpec((1, 1, NSA_GROUPS, nr, LANES), lambda a, i: (a, i, 0, 0, 0)),
        out_shape=jax.ShapeDtypeStruct((two, b, NSA_GROUPS, nr, LANES), BF16),
        compiler_params=_cparams(("parallel", "parallel")),
        name="compress",
    )(main3, pos, w1, w1g, w2)


def _gate_cols(gate_logits, branch):
    sg = jax.nn.sigmoid(gate_logits)
    return [sg[:, 3 * r + branch:3 * r + branch + 1] for r in range(NSA_REP)]


def _stack_heads_dup(q):
    qa = q[:, :LANES]
    qb = q[:, LANES:]
    lo = lax.broadcasted_iota(jnp.int32, qa.shape, 1) < NSA_DH
    z = jnp.zeros_like(qa)
    return jnp.concatenate([jnp.where(lo, qa, z), jnp.where(lo, z, qa),
                            jnp.where(lo, qb, z), jnp.where(lo, z, qb)], axis=0)


def _unstack_heads_dup(o, gates):
    nq = o.shape[0] // NSA_REP
    lo = lax.broadcasted_iota(jnp.int32, (nq, LANES), 1) < NSA_DH
    a = jnp.where(lo, gates[0] * o[0:nq], gates[1] * o[nq:2 * nq])
    b = jnp.where(lo, gates[2] * o[2 * nq:3 * nq], gates[3] * o[3 * nq:])
    return jnp.concatenate([a, b], axis=1)


def _stack_heads_pos(q, slopes):
    qa = q[:, :LANES]
    qb = q[:, LANES:]
    lane = lax.broadcasted_iota(jnp.int32, qa.shape, 1)
    lo = lane < NSA_DH
    heads = [qa, pltpu.roll(qa, NSA_DH, 1), qb, pltpu.roll(qb, NSA_DH, 1)]
    out = []
    for r in range(NSA_REP):
        extra = jnp.where(lane == LANE_POS_HI, slopes[r] * LANES, jnp.where(lane == LANE_POS_LO, slopes[r], 0.0))
        out.append(jnp.where(lo, heads[r], extra))
    return jnp.concatenate(out, axis=0)


def _finish_heads(acc, gates):
    nq = acc.shape[0] // NSA_REP
    o = acc * pltpu.roll(1.0 / acc, NSA_DH, 1)
    lo = lax.broadcasted_iota(jnp.int32, (nq, LANES), 1) < NSA_DH
    hs = [gates[r] * o[r * nq:(r + 1) * nq] for r in range(NSA_REP)]
    a = jnp.where(lo, hs[0], pltpu.roll(hs[1], NSA_DH, 1))
    b = jnp.where(lo, hs[2], pltpu.roll(hs[3], NSA_DH, 1))
    return jnp.concatenate([a, b], axis=1)


def _select_kernel(slopes_ref, q_ref, kc_ref, vc_ref, gates_ref, ov_ref, sel_ref, act_ref, yc_ref):
    step = pl.program_id(1)
    nsteps = pl.num_programs(1)
    nb_all, ncp_all = ov_ref.shape
    for k in range(SELECT_BUCKETS):
        @pl.when((step * SELECT_BUCKETS) // nsteps == k)
        def _(k=k):
            for u in range(SELECT_TILES):
                _select_prefix(slopes_ref, q_ref, kc_ref, vc_ref, gates_ref, ov_ref, sel_ref, act_ref, yc_ref, tile=u,
                               nb=(k + 1) * nb_all // SELECT_BUCKETS, ncp=(k + 1) * ncp_all // SELECT_BUCKETS)


def _select_prefix(slopes_ref, q_ref, kc_ref, vc_ref, gates_ref, ov_ref, sel_ref, act_ref, yc_ref, *, tile, nb, ncp):
    t0 = (pl.program_id(1) * SELECT_TILES + tile) * Q_TILE
    nq = Q_TILE
    rows = slice(tile * nq, (tile + 1) * nq)
    nb_all = ov_ref.shape[0]
    c_end = lax.broadcasted_iota(jnp.int32, (nq, ncp), 1) * CMP_STRIDE + (CMP_LEN - 1)
    tq = t0 + lax.broadcasted_iota(jnp.int32, (nq, ncp), 0)
    vis = c_end <= tq
    any_vis = tq[:, 0:1] >= CMP_LEN - 1
    rel = (c_end[0:1] - t0).astype(F32)
    ov = ov_ref[:nb, :ncp]
    jb = lax.broadcasted_iota(jnp.int32, (nb, nq), 0)
    tql = t0 + lax.broadcasted_iota(jnp.int32, (nb, nq), 1)
    valid = jb * SEL_LEN <= tql
    forced = (jb == 0) | (jb == jnp.right_shift(tql, SEL_SHIFT))
    scores = []
    for g in range(NSA_GROUPS):
        qcols = slice(g * NSA_REP * NSA_DH, (g + 1) * NSA_REP * NSA_DH)
        qs = _stack_heads_dup(q_ref[rows, qcols] * (NSA_DH ** -0.5)).astype(BF16)
        s = _mm_nt(qs, kc_ref[0, g, :ncp, :])
        psum = jnp.zeros((nq, ncp), F32)
        ps = []
        for r in range(NSA_REP):
            m_r = slopes_ref[g * NSA_REP + r]
            sr = jnp.where(vis, s[r * nq:(r + 1) * nq] + m_r * rel, NEG)
            mx = jnp.max(sr, axis=-1, keepdims=True)
            e = jnp.exp(sr - mx)
            den = jnp.sum(e, axis=-1, keepdims=True)
            p = e * jnp.where(any_vis, 1.0 / den, 0.0)
            psum = psum + p
            ps.append(p.astype(BF16))
        o_c = jnp.dot(jnp.concatenate(ps, axis=0), vc_ref[0, g, :ncp, :], preferred_element_type=F32)
        yc_ref[rows, qcols] = _unstack_heads_dup(o_c, _gate_cols(gates_ref[rows, g * LANES:(g + 1) * LANES], 0))
        p_hi = psum.astype(BF16)
        p_lo = (psum - p_hi.astype(F32)).astype(BF16)
        imp_t = _mm_nt(ov, p_hi) + _mm_nt(ov, p_lo)
        scores.append(jnp.where(valid, imp_t + FORCE_BONUS * forced.astype(F32), NEG))

    for _ in range(min(TOP_N, nb)):
        for g in range(NSA_GROUPS):
            mx = jnp.max(scores[g], axis=0, keepdims=True)
            idx = jnp.min(jnp.where(scores[g] == mx, jb, nb), axis=0, keepdims=True)
            scores[g] = jnp.where(jb == idx, -jnp.inf, scores[g])
    for g in range(NSA_GROUPS):
        sel_t = jnp.where((scores[g] == -jnp.inf) & valid, 1.0, 0.0)
        if nb < nb_all:
            sel_t = jnp.concatenate([sel_t, jnp.zeros((nb_all - nb, nq), F32)], axis=0)
        sel_q = sel_t.T
        sel_ref[0, g, rows, :] = sel_q.astype(sel_ref.dtype)
        act_ref[0, g, tile] = jnp.max(sel_q, axis=0, keepdims=True)


def _select_call(slopes, main, kcmp, vcmp, ov, bsz, seq):
    nqt = seq // Q_TILE
    nsteps = nqt // SELECT_TILES
    tq = Q_TILE * SELECT_TILES
    nb, ncp = ov.shape
    row = lambda b, q: b * nsteps + q
    gw = NSA_GROUPS * LANES
    return pl.pallas_call(
        _select_kernel,
        grid=(bsz, nsteps),
        in_specs=[pl.BlockSpec(memory_space=pltpu.SMEM),
                  pl.BlockSpec((tq, NSA_Q_W), lambda b, q: (row(b, q), COL_Q // NSA_Q_W)),
                  pl.BlockSpec((1, NSA_GROUPS, ncp, LANES), lambda b, q: (b, 0, 0, 0)),
                  pl.BlockSpec((1, NSA_GROUPS, ncp, LANES), lambda b, q: (b, 0, 0, 0)),
                  pl.BlockSpec((tq, gw), lambda b, q: (row(b, q), COL_GATE // gw)),
                  pl.BlockSpec((nb, ncp), lambda b, q: (0, 0))],
        out_specs=[pl.BlockSpec((1, NSA_GROUPS, tq, nb), lambda b, q: (b, 0, q, 0)),
                   pl.BlockSpec((1, NSA_GROUPS, SELECT_TILES, 1, nb), lambda b, q: (b, 0, q, 0, 0)),
                   pl.BlockSpec((tq, NSA_Q_W), lambda b, q: (row(b, q), 0))],
        out_shape=[jax.ShapeDtypeStruct((bsz, NSA_GROUPS, seq, nb), BF16),
                   jax.ShapeDtypeStruct((bsz, NSA_GROUPS, nqt, 1, nb), F32),
                   jax.ShapeDtypeStruct((bsz * seq, NSA_Q_W), F32)],
        compiler_params=_cparams(("parallel", "parallel")),
        name="select",
    )(slopes, main, kcmp, vcmp, main, ov)


def _attn_kernel(lst_ref, cnt_ref, slopes_ref, q_ref, sel_ref, ks_ref, vs_ref, kw_ref, vw_ref, gates_ref, yc_ref,
                 et_ref, o_ref, s_scr, p_scr, m_scr, a_scr, acc_scr, *, chunk):
    b = pl.program_id(0)
    g = pl.program_id(1)
    step = pl.program_id(2)
    tiles_per_step = ATTN_TILES * ATTN_GROUPS
    nqt = pl.num_programs(2) * tiles_per_step
    nq = ATTN_Q_TILE
    nw = WINDOW + nq
    nch_total = ks_ref.shape[0] // chunk
    slopes = [slopes_ref[g * NSA_REP + r] for r in range(NSA_REP)]

    def values(p, c):
        start = pl.multiple_of(c * chunk, chunk)
        return jnp.dot(p, vs_ref[pl.ds(start, chunk), :], preferred_element_type=F32)

    def row_max(s):
        return jnp.broadcast_to(jnp.max(s, axis=-1, keepdims=True), (s.shape[0], LANES))

    def lane_tile(m, width):
        return jnp.concatenate([m] * (width // LANES), axis=1)

    def masked(s, ok):
        return jnp.concatenate([jnp.where(ok, s[r * nq:(r + 1) * nq], NEG) for r in range(NSA_REP)], axis=0)

    class Group:
        pass

    def setup(k):
        gr = Group()
        gr.tiles = range(ATTN_TILES)
        gr.scr = [(k % 2) * ATTN_TILES + u for u in gr.tiles]
        local = [k * ATTN_TILES + u for u in gr.tiles]
        gr.rws = [slice(v * nq, (v + 1) * nq) for v in local]
        qts = [step * tiles_per_step + v for v in local]
        gr.t0s = [qt * nq for qt in qts]
        gr.qs = [_stack_heads_pos(q_ref[rw, :] * (NSA_DH ** -0.5), slopes).astype(BF16) for rw in gr.rws]
        pens = [((sel_ref[0, 0, rw, :].astype(F32) - 1.0) * (-NEG)).astype(BF16) for rw in gr.rws]
        gr.qx = [jnp.concatenate([gr.qs[u], jnp.concatenate([pens[u]] * NSA_REP, axis=0)], axis=1) for u in gr.tiles]
        gr.bases = [((b * NSA_GROUPS + g) * nqt + qt) * nch_total for qt in qts]
        gr.cnts = [cnt_ref[(b * NSA_GROUPS + g) * nqt + qt] for qt in qts]
        for u in gr.tiles:
            su = gr.scr[u]
            m_scr[su] = jnp.full(m_scr.shape[1:], NEG, F32)
            a_scr[su] = jnp.ones(a_scr.shape[1:], F32)
            acc_scr[su] = jnp.zeros(acc_scr.shape[1:], F32)
            p_scr[su, 1] = jnp.zeros(p_scr.shape[2:], BF16)
        for u in gr.tiles:
            s_scr[gr.scr[u], 0] = logits(gr, u, lst_ref[gr.bases[u]])
        return gr

    def logits(gr, u, c):
        start = pl.multiple_of(c * chunk, chunk)
        kx = jnp.concatenate([ks_ref[pl.ds(start, chunk), :], et_ref[pl.ds(start, chunk), :]], axis=1)
        return _mm_nt(gr.qx[u], kx)

    def flash_step(gr, us, i, slot):
        for u in us:
            su = gr.scr[u]
            acc_scr[su] = a_scr[su] * acc_scr[su] + values(p_scr[su, 1 - slot],
                                                           lst_ref[gr.bases[u] + jnp.maximum(i - 1, 0)])
        for u in us:
            su = gr.scr[u]
            s = s_scr[su, slot]
            m_prev = m_scr[su]
            m_new = jnp.maximum(m_prev, row_max(s))
            a_scr[su] = jnp.exp(m_prev - m_new)
            p_scr[su, slot] = jnp.exp(s - lane_tile(m_new, chunk)).astype(BF16)
            m_scr[su] = m_new
        for u in us:
            s_scr[gr.scr[u], 1 - slot] = logits(gr, u, lst_ref[gr.bases[u] + i + 1])

    def flash_loops(gr):
        def single_steps(us):
            def body(i, carry):
                flash_step(gr, us, i, i & 1)
                return carry
            return body

        def double_steps(us):
            def body(j, carry):
                flash_step(gr, us, 2 * j, 0)
                flash_step(gr, us, 2 * j + 1, 1)
                return carry
            return body

        joint = functools.reduce(jnp.minimum, gr.cnts)
        lax.fori_loop(0, joint // 2, double_steps(list(gr.tiles)), 0)
        lax.fori_loop(joint - (joint & 1), joint, single_steps(list(gr.tiles)), 0)
        for u in gr.tiles:
            lax.fori_loop(joint, gr.cnts[u], single_steps([u]), 0)

    def finish(gr):
        tiles = gr.tiles
        scr = gr.scr
        slots = [gr.cnts[u] & 1 for u in tiles]
        c_diag = [gr.t0s[u] // chunk for u in tiles]
        acc = [a_scr[scr[u]] * acc_scr[scr[u]]
               + values(p_scr[scr[u], 1 - slots[u]], lst_ref[gr.bases[u] + jnp.maximum(gr.cnts[u] - 1, 0)])
               for u in tiles]
        wbase = [pl.multiple_of(jnp.maximum(gr.t0s[u] - WINDOW, 0), nq) for u in tiles]
        s_win = [_mm_nt(gr.qs[u], kw_ref[pl.ds(wbase[u], nw), :]) for u in tiles]
        s_diag = []
        for u in tiles:
            pos = c_diag[u] * chunk + lax.broadcasted_iota(jnp.int32, (nq, chunk), 1)
            tq = gr.t0s[u] + lax.broadcasted_iota(jnp.int32, (nq, chunk), 0)
            s_diag.append(masked(s_scr[scr[u], slots[u]], pos <= tq))
        m_prev = [m_scr[scr[u]] for u in tiles]
        m_new = [jnp.maximum(m_prev[u], row_max(s_diag[u])) for u in tiles]
        acc = [jnp.exp(m_prev[u] - m_new[u]) * acc[u]
               + values(jnp.exp(s_diag[u] - lane_tile(m_new[u], chunk)).astype(BF16), c_diag[u]) for u in tiles]
        p_win = []
        for u in tiles:
            pos = wbase[u] + lax.broadcasted_iota(jnp.int32, (nq, nw), 1)
            tq = gr.t0s[u] + lax.broadcasted_iota(jnp.int32, (nq, nw), 0)
            s = masked(s_win[u], (pos <= tq) & (pos > tq - WINDOW))
            p_win.append(jnp.exp(s - jnp.max(s, axis=-1, keepdims=True)).astype(BF16))
        acc_w = [jnp.dot(p_win[u], vw_ref[pl.ds(wbase[u], nw), :], preferred_element_type=F32) for u in tiles]
        for u in tiles:
            gates = gates_ref[gr.rws[u], :]
            o_ref[gr.rws[u], :] = (yc_ref[gr.rws[u], :] + _finish_heads(acc[u], _gate_cols(gates, 1))
                                   + _finish_heads(acc_w[u], _gate_cols(gates, 2))).astype(o_ref.dtype)

    cur = setup(0)
    for k in range(ATTN_GROUPS):
        flash_loops(cur)
        nxt = setup(k + 1) if k + 1 < ATTN_GROUPS else None
        finish(cur)
        cur = nxt


def _chunk_lists(act, seq, chunk):
    bsz, ngrp = act.shape[:2]
    nqt = seq // ATTN_Q_TILE
    nch = seq // chunk
    active = act.reshape(bsz, ngrp, nqt, ATTN_Q_TILE // Q_TILE, nch, chunk // SEL_LEN).max(axis=(3, 5)) > 0.0
    c_diag = (jnp.arange(nqt) * ATTN_Q_TILE) // chunk
    active = active & (jnp.arange(nch)[None, :] < c_diag[:, None])
    cnt = active.sum(axis=-1).astype(jnp.int32)
    order = jnp.argsort(jnp.logical_not(active), axis=-1, stable=True).astype(jnp.int32)
    lst = jnp.where(jnp.arange(nch) < cnt[..., None], order, c_diag[:, None].astype(jnp.int32))
    return lst.reshape(-1), cnt.reshape(-1)


def _attn_call(lst, cnt, slopes, main, sel, kv, yc, et, bsz, seq, chunk):
    tq = ATTN_Q_TILE * ATTN_TILES * ATTN_GROUPS
    nset = 2 * ATTN_TILES
    nqt = seq // tq
    nb = sel.shape[-1]
    rows = NSA_REP * ATTN_Q_TILE
    row = lambda b, g, q: b * nqt + q
    kvspec = lambda blk: pl.BlockSpec((seq, LANES), lambda b, g, q, *_: (b, N_MG // LANES + blk + g))
    grid_spec = pltpu.PrefetchScalarGridSpec(
        num_scalar_prefetch=2,
        grid=(bsz, NSA_GROUPS, nqt),
        in_specs=[pl.BlockSpec(memory_space=pltpu.SMEM),
                  pl.BlockSpec((tq, 256), lambda b, g, q, *_: (row(b, g, q), COL_Q // 256 + g)),
                  pl.BlockSpec((1, 1, tq, nb), lambda b, g, q, *_: (b, g, q, 0)),
                  kvspec(KVB_KS), kvspec(KVB_VS), kvspec(KVB_KW), kvspec(KVB_VW),
                  pl.BlockSpec((tq, LANES), lambda b, g, q, *_: (row(b, g, q), COL_GATE // LANES + g)),
                  pl.BlockSpec((tq, 256), lambda b, g, q, *_: (row(b, g, q), g)),
                  pl.BlockSpec(et.shape, lambda b, g, q, *_: (0, 0))],
        out_specs=pl.BlockSpec((tq, 256), lambda b, g, q, *_: (row(b, g, q), g)),
        scratch_shapes=[pltpu.VMEM((nset, 2, rows, chunk), F32),
                        pltpu.VMEM((nset, 2, rows, chunk), BF16),
                        pltpu.VMEM((nset, rows, LANES), F32),
                        pltpu.VMEM((nset, rows, LANES), F32),
                        pltpu.VMEM((nset, rows, LANES), F32)],
    )
    return pl.pallas_call(
        functools.partial(_attn_kernel, chunk=chunk),
        grid_spec=grid_spec,
        out_shape=jax.ShapeDtypeStruct((bsz * seq, NSA_Q_W), BF16),
        compiler_params=_cparams(("parallel", "parallel", "parallel")),
        name="attn",
    )(lst, cnt, slopes, main, sel, kv, kv, kv, kv, main, yc, et)


def _pool_kernel(x_ref, halo_ref, w_ref, sc_ref, o_ref):
    s = pl.program_id(1)
    ts = x_ref.shape[0]
    x = x_ref[...]
    halo = jnp.where(s > 0, halo_ref[...], 0.0)
    t = s * ts + lax.broadcasted_iota(jnp.int32, (ts, 1), 0)
    for gi, win in enumerate(POOL_WINDOWS):
        lanes = slice(gi * POOL_GROUP_DIM, (gi + 1) * POOL_GROUP_DIM)
        ext = jnp.concatenate([halo[:, lanes], x[:, lanes]], axis=0)
        acc = ext
        span = 1
        while span < win:
            acc = acc + pltpu.roll(acc, span, 0)
            span *= 2
        cnt = jnp.minimum(t + 1, win).astype(F32)
        y = acc[POOL_HALO:] / cnt - x[:, lanes]
        o_ref[:, lanes] = (_mm(y, w_ref[gi]) * sc_ref[:, lanes]).astype(o_ref.dtype)


def _pool_call(main, w, scale, bsz, seq):
    ts = min(POOL_TS, seq)
    per_b = seq // ts
    halo_per_tile = ts // POOL_HALO
    return pl.pallas_call(
        _pool_kernel,
        grid=(bsz, per_b),
        in_specs=[pl.BlockSpec((ts, POOL_W), lambda b, s: (b * per_b + s, COL_POOL // POOL_W)),
                  pl.BlockSpec((POOL_HALO, POOL_W),
                               lambda b, s: (jnp.maximum((b * per_b + s) * halo_per_tile - 1, 0), COL_POOL // POOL_W)),
                  pl.BlockSpec(w.shape, lambda b, s: (0, 0, 0)),
                  pl.BlockSpec((1, POOL_W), lambda b, s: (0, 0))],
        out_specs=pl.BlockSpec((ts, POOL_W), lambda b, s: (b * per_b + s, 0)),
        out_shape=jax.ShapeDtypeStruct((bsz * seq, POOL_W), BF16),
        compiler_params=_cparams(("parallel", "parallel")),
        name="pool",
    )(main, main, w, scale.reshape(1, POOL_W))


def _unit_lower_inverses(nms, ri, ci):
    c = nms[0].shape[0]
    eye = jnp.where(ri == ci, 1.0, 0.0)
    in8 = jnp.right_shift(ri, 3) == jnp.right_shift(ci, 3)
    n8 = [jnp.where(in8, nm, 0.0) for nm in nms]
    ps = [eye - a for a in n8]
    ms = [_mm(a, a) for a in n8]
    ps = [p + _mm(p, m) for p, m in zip(ps, ms)]
    ms = [_mm(m, m) for m in ms]
    ps = [p + _mm(p, m) for p, m in zip(ps, ms)]
    shift = 3
    while (1 << shift) < c:
        rb = jnp.right_shift(ri, shift)
        cb = jnp.right_shift(ci, shift)
        lower_left = ((rb & 1) == 1) & (cb == rb - 1)
        ts = [_mm(p, jnp.where(lower_left, nm, 0.0)) for p, nm in zip(ps, nms)]
        ps = [p - _mm(t, p) for p, t in zip(ps, ts)]
        shift += 1
    return ps


def _row_cumsum(x, ri):
    span = 1
    while span < x.shape[0]:
        x = x + jnp.where(ri >= span, pltpu.roll(x, span, 0), 0.0)
        span *= 2
    return x


def _dn_kernel(qkv_ref, halo_ref, z_ref, gba_ref, cw_ref, alog_ref, dtb_ref, ng_ref, o_ref, st_scr):
    s = pl.program_id(1)
    nbatch, c = qkv_ref.shape[0], qkv_ref.shape[1]

    @pl.when(s == 0)
    def _():
        st_scr[...] = jnp.zeros(st_scr.shape, F32)

    ri = lax.broadcasted_iota(jnp.int32, (c, c), 0)
    ci = lax.broadcasted_iota(jnp.int32, (c, c), 1)
    tril = ri >= ci
    strict = ri > ci
    cw = cw_ref[...]
    ng = ng_ref[...]
    chains = [(bi, h) for bi in range(nbatch) for h in range(DN_HEADS)]
    qn, kn, kb, vb, gcb, lmask = [], [], [], [], [], []
    for bi in range(nbatch):
        halo = jnp.where(s > 0, halo_ref[bi], 0.0)
        ext = jnp.concatenate([halo, qkv_ref[bi]], axis=0)
        conv = cw[DN_CONV - 1:DN_CONV] * ext
        for tap in range(1, DN_CONV):
            conv = conv + cw[DN_CONV - 1 - tap:DN_CONV - tap] * pltpu.roll(ext, tap, 0)
        act = _silu(conv[DN_HALO:])

        gba = gba_ref[bi]
        beta_all = jax.nn.sigmoid(gba)
        xs = gba + dtb_ref[...]
        softplus = jnp.maximum(xs, 0.0) + jnp.log(1.0 + jnp.exp(-jnp.abs(xs)))
        gc_all = _row_cumsum(-jnp.exp(alog_ref[...]) * softplus, ri)
        gct_all = gc_all.T
        for h in range(DN_HEADS):
            q = act[:, h * DN_DK:(h + 1) * DN_DK]
            k = act[:, DN_W + h * DN_DK:DN_W + (h + 1) * DN_DK]
            v = act[:, 2 * DN_W + h * DN_DK:2 * DN_W + (h + 1) * DN_DK]
            qn.append(q * lax.rsqrt(jnp.sum(q * q, axis=-1, keepdims=True) + EPS) * (DN_DK ** -0.5))
            kn.append(k * lax.rsqrt(jnp.sum(k * k, axis=-1, keepdims=True) + EPS))
            beta = beta_all[:, LANE_BETA + h:LANE_BETA + h + 1]
            gcb.append(jnp.broadcast_to(gc_all[:, LANE_DECAY + h:LANE_DECAY + h + 1], (c, DN_DK)))
            gct = jnp.broadcast_to(gct_all[LANE_DECAY + h:LANE_DECAY + h + 1, :], (c, c))
            lmask.append(jnp.where(tril, jnp.exp(jnp.where(tril, gcb[-1] - gct, 0.0)), 0.0))
            kb.append(kn[-1] * beta)
            vb.append(v * beta)

    n = range(len(chains))
    kk = [_mm_nt(kb[i], kn[i]) for i in n]
    tinv = _unit_lower_inverses([jnp.where(strict, kk[i] * lmask[i], 0.0) for i in n], ri, ci)
    egc = [jnp.exp(gcb[i]) for i in n]
    sol = [_mm(tinv[i], jnp.concatenate([vb[i], kb[i] * egc[i]], axis=1)) for i in n]
    aqk = [_mm_nt(qn[i], kn[i]) * lmask[i] for i in n]
    state = [st_scr[i] for i in n]
    glast = [gcb[i][c - 1:c, :] for i in n]
    v_new = [sol[i][:, :DN_DK] - _mm(sol[i][:, DN_DK:], state[i]) for i in n]
    o_state = [_mm(qn[i] * egc[i], state[i]) for i in n]
    o = [o_state[i] + _mm(aqk[i], v_new[i]) for i in n]
    upd = [_mm_tn(kn[i] * jnp.exp(glast[i] - gcb[i]), v_new[i]) for i in n]
    for i, (bi, h) in enumerate(chains):
        st_scr[i] = state[i] * jnp.exp(glast[i]) + upd[i]
        on = o[i] * lax.rsqrt(jnp.mean(o[i] * o[i], axis=-1, keepdims=True) + EPS) * ng
        o_ref[bi, :, h * DN_DK:(h + 1) * DN_DK] = (
            on * _silu(z_ref[bi, :, h * DN_DK:(h + 1) * DN_DK])).astype(o_ref.dtype)


def _dn_call(main3, conv_w, alog_row, dtb_row, norm_g):
    bsz, seq, _ = main3.shape
    c = DN_CHUNK
    nbt = DN_BATCH if bsz % DN_BATCH == 0 else 1
    halo_per_chunk = c // DN_HALO
    w3 = 3 * DN_W
    return pl.pallas_call(
        _dn_kernel,
        grid=(bsz // nbt, seq // c),
        in_specs=[pl.BlockSpec((nbt, c, w3), lambda b, s: (b, s, COL_QKV // w3)),
                  pl.BlockSpec((nbt, DN_HALO, w3),
                               lambda b, s: (b, jnp.maximum(s * halo_per_chunk - 1, 0), COL_QKV // w3)),
                  pl.BlockSpec((nbt, c, DN_W), lambda b, s: (b, s, COL_Z // DN_W)),
                  pl.BlockSpec((nbt, c, LANES), lambda b, s: (b, s, COL_GATE // LANES)),
                  pl.BlockSpec((SUBLANES, w3), lambda b, s: (0, 0)),
                  pl.BlockSpec((1, LANES), lambda b, s: (0, 0)),
                  pl.BlockSpec((1, LANES), lambda b, s: (0, 0)),
                  pl.BlockSpec((1, DN_DK), lambda b, s: (0, 0))],
        out_specs=pl.BlockSpec((nbt, c, DN_W), lambda b, s: (b, s, 0)),
        out_shape=jax.ShapeDtypeStruct((bsz, seq, DN_W), BF16),
        scratch_shapes=[pltpu.VMEM((nbt * DN_HEADS, DN_DK, DN_DK), F32)],
        compiler_params=_cparams(("parallel", "arbitrary")),
        name="dn",
    )(main3, main3, main3, main3, conv_w, alog_row, dtb_row, norm_g.reshape(1, DN_DK))


def _merge_kernel(ya_ref, yb_ref, yc_ref, mg_ref, x_ref, gt_ref, wa_ref, wb_ref, wc_ref, wo_ref, o_ref):
    d = x_ref.shape[1]
    merged = None
    for i, (y_ref, w_ref) in enumerate(((ya_ref, wa_ref), (yb_ref, wb_ref), (yc_ref, wc_ref))):
        gate = jax.nn.sigmoid(mg_ref[:, i * d:(i + 1) * d].astype(F32))
        term = gate * jnp.dot(y_ref[...], w_ref[...], preferred_element_type=F32)
        merged = term if merged is None else merged + term
    y = jnp.dot(merged.astype(BF16), wo_ref[...], preferred_element_type=F32)
    o_ref[...] = x_ref[...] + gt_ref[0] * y


def _merge_call(ya, yb, yc, mg, x2, gt, wa, wb, wc, wo, seq):
    t, d = x2.shape
    tm = min(MERGE_TM, seq)
    per_b = seq // tm
    full = lambda a: pl.BlockSpec(a.shape, lambda i: (0, 0), pipeline_mode=pl.Buffered(1))
    return pl.pallas_call(
        _merge_kernel,
        grid=(t // tm,),
        in_specs=[pl.BlockSpec((tm, ya.shape[1]), lambda i: (i, 0)),
                  pl.BlockSpec((tm, yb.shape[1]), lambda i: (i, 0)),
                  pl.BlockSpec((tm, yc.shape[1]), lambda i: (i, 0)),
                  pl.BlockSpec((tm, N_MG), lambda i: (i, 0)),
                  pl.BlockSpec((tm, d), lambda i: (i, 0)),
                  pl.BlockSpec((1, 1, d), lambda i: (i // per_b, 0, 0)),
                  full(wa), full(wb), full(wc), full(wo)],
        out_specs=pl.BlockSpec((tm, d), lambda i: (i, 0)),
        out_shape=jax.ShapeDtypeStruct((t, d), F32),
        compiler_params=_cparams(("parallel",)),
        name="merge",
    )(ya, yb, yc, mg, x2, gt, wa, wb, wc, wo)


def _mlp_kernel(x_ref, g_ref, sc_ref, sh_ref, gt_ref, w1_ref, w2_ref, fg_ref, o_ref, *, final):
    x = x_ref[...]
    h = _norm_mod(x, g_ref[...], sc_ref[0], sh_ref[0]).astype(BF16)
    acc = None
    for c0 in range(0, w1_ref.shape[1], MLP_TF):
        a = jnp.maximum(jnp.dot(h, w1_ref[:, c0:c0 + MLP_TF], preferred_element_type=F32), 0.0)
        part = jnp.dot((a * a).astype(BF16), w2_ref[c0:c0 + MLP_TF, :], preferred_element_type=F32)
        acc = part if acc is None else acc + part
    y = x + gt_ref[0] * acc
    if final:
        ms = jnp.mean(y * y, axis=-1, keepdims=True)
        y = y * lax.rsqrt(ms + EPS) * fg_ref[...]
    o_ref[...] = y


def _mlp_call(x2, g, sc, sh, gt, w1, w2, layer, fg, seq, final):
    t, d = x2.shape
    f = w1.shape[2]
    assert f % MLP_TF == 0
    tm = min(MLP_TM, seq)
    per_b = seq // tm
    mod = lambda: pl.BlockSpec((1, 1, d), lambda i: (i // per_b, 0, 0))
    resident = pl.Buffered(1)
    return pl.pallas_call(
        functools.partial(_mlp_kernel, final=final),
        grid=(t // tm,),
        in_specs=[pl.BlockSpec((tm, d), lambda i: (i, 0)),
                  pl.BlockSpec((1, d), lambda i: (0, 0)),
                  mod(), mod(), mod(),
                  pl.BlockSpec((None, d, f), lambda i: (layer, 0, 0), pipeline_mode=resident),
                  pl.BlockSpec((None, f, d), lambda i: (layer, 0, 0), pipeline_mode=resident),
                  pl.BlockSpec((1, d), lambda i: (0, 0))],
        out_specs=pl.BlockSpec((tm, d), lambda i: (i, 0)),
        out_shape=jax.ShapeDtypeStruct((t, d), F32),
        compiler_params=_cparams(("parallel",)),
        name="mlp",
    )(x2, g.reshape(1, d), sc, sh, gt, w1, w2, fg.reshape(1, d))


def _split_w_in(w_in):
    offs = [0]
    for sz in IN_SIZES:
        offs.append(offs[-1] + sz)
    return [w_in[:, offs[i]:offs[i + 1]] for i in range(len(IN_SIZES))]


def _layout_w_in(w_in):
    nq, nkc, nvc, nks, nvs, nkw, nvw, ngate, pin, dqkv, dz, dbeta, da, mg = _split_w_in(w_in)
    d = w_in.shape[0]
    per_g = 3 * NSA_REP
    zeros = lambda n: jnp.zeros((d, n), w_in.dtype)
    gate0 = jnp.concatenate([ngate[:, :per_g], zeros(LANE_BETA - per_g), dbeta, da,
                             zeros(LANES - LANE_DECAY - DN_HEADS)], axis=1)
    gate1 = jnp.concatenate([ngate[:, per_g:], zeros(LANES - per_g)], axis=1)
    main = jnp.concatenate([nq, pin, dz, dqkv, gate0, gate1, nkc, nvc], axis=1)

    def per_group(w):
        pad = zeros(LANES - NSA_DH)
        return jnp.concatenate([w[:, :NSA_DH], pad, w[:, NSA_DH:], pad], axis=1)

    kv = jnp.concatenate([per_group(nks), per_group(nvs), per_group(nkw), per_group(nvw)], axis=1)
    return main.astype(BF16), jnp.concatenate([mg, kv], axis=1).astype(BF16)


def _group_lane_w1(w1):
    two, _, hid = w1.shape
    halves = w1.reshape(two, 2, CMP_STRIDE, NSA_DH, hid)
    out = []
    for g in range(NSA_GROUPS):
        padded = jnp.zeros((two, 2, CMP_STRIDE, LANES, hid), w1.dtype).at[:, :, :, g * NSA_DH:(g + 1) * NSA_DH].set(halves)
        out.append(padded.reshape(two, 2, CMP_STRIDE * LANES, hid))
    return jnp.stack(out, axis=1)


def _nsa_constants(seq):
    nb = seq // SEL_LEN
    ncp = seq // CMP_STRIDE
    j = jnp.arange(nb)[:, None] * SEL_LEN
    i = jnp.arange(ncp)[None, :] * CMP_STRIDE
    overlap = jnp.clip(jnp.minimum(j + SEL_LEN, i + CMP_LEN) - jnp.maximum(j, i), 0) // CMP_STRIDE
    block_onehot = (jnp.arange(seq)[:, None] // SEL_LEN) == jnp.arange(nb)[None, :]
    return overlap.astype(BF16), block_onehot.astype(BF16)


def kernel(x, c, ada_w, ada_b, norm1_g, norm2_g, w_in, phi_k1, phi_k2, phi_v1, phi_v2, pos_k, pos_v, pool_w, pool_scale, dn_conv_w, dn_A_log, dn_dt_bias, dn_norm_g, w_branch_nsa, w_branch_pool, w_branch_dn, w_out, mlp_w1, mlp_w2, final_g):
    bsz, seq, d = x.shape
    assert d == D_MODEL and seq % max(PROJ_TM, MLP_TM, POOL_TS) == 0 and seq % KV_CHUNK == 0
    assert seq % (ATTN_Q_TILE * ATTN_TILES * ATTN_GROUPS) == 0
    assert (seq // Q_TILE) % (SELECT_BUCKETS * SELECT_TILES) == 0
    assert DN_CHUNK == DN_DK == LANES
    t = bsz * seq
    depth = ada_w.shape[0]
    chunk = KV_CHUNK

    slopes = 2.0 ** (-(8.0 / NSA_HEADS) * (jnp.arange(NSA_HEADS, dtype=F32) + 1.0))
    overlap, block_onehot = _nsa_constants(seq)
    c_pad = jnp.zeros((SUBLANES, d), F32).at[:bsz].set(c)
    ada_b3 = ada_b.reshape(depth, 1, ada_b.shape[1])
    mlp_w1_bf, mlp_w2_bf = mlp_w1.astype(BF16), mlp_w2.astype(BF16)
    x2 = x.reshape(t, d)

    for l in range(depth):
        mod = _mod_call(c_pad, ada_w, ada_b3, l)[:bsz]
        sh1, sc1, gt1, sh2, sc2, gt2 = [m.reshape(bsz, 1, d) for m in jnp.split(mod, 6, axis=-1)]

        w_main, w_mgkv = _layout_w_in(w_in[l])
        main = _proj_call(x2, norm1_g[l], sc1, sh1, w_main, seq, PROJ_MAIN_TM, F32, "proj_main")
        kv = mg = _proj_kv_call(x2, norm1_g[l], sc1, sh1, w_mgkv, seq)

        main3 = main.reshape(bsz, seq, N_MAIN)
        pos = jnp.zeros((2, SUBLANES, CMP_LEN * NSA_DH), F32).at[:, 0].set(
            jnp.stack([pos_k[l].reshape(-1), pos_v[l].reshape(-1)])).astype(BF16)
        w1 = jnp.stack([phi_k1[l], phi_v1[l]]).astype(BF16)
        w2 = jnp.stack([phi_k2[l], phi_v2[l]])
        w2 = jnp.concatenate([w2, w2], axis=-1).astype(BF16)
        cmp_kv = _compress_call(main3, pos, w1, _group_lane_w1(w1), w2)

        sel, act, yc = _select_call(slopes, main, cmp_kv[0], cmp_kv[1], overlap, bsz, seq)
        lst, cnt = _chunk_lists(act, seq, chunk)
        y_a = _attn_call(lst, cnt, slopes, main, sel, kv, yc, block_onehot, bsz, seq, chunk)

        y_b = _pool_call(main, pool_w[l].astype(BF16), pool_scale[l], bsz, seq)

        conv_w = jnp.zeros((SUBLANES, 3 * DN_W), F32).at[:DN_CONV].set(dn_conv_w[l])
        lane_row = lambda v: jnp.zeros((1, LANES), F32).at[0, LANE_DECAY:LANE_DECAY + DN_HEADS].set(v)
        y_c = _dn_call(main3, conv_w, lane_row(dn_A_log[l]), lane_row(dn_dt_bias[l]),
                       dn_norm_g[l]).reshape(t, DN_W)

        x2 = _merge_call(y_a, y_b, y_c, mg, x2, gt1,
                         w_branch_nsa[l].astype(BF16), w_branch_pool[l].astype(BF16),
                         w_branch_dn[l].astype(BF16), w_out[l].astype(BF16), seq)
        x2 = _mlp_call(x2, norm2_g[l], sc2, sh2, gt2, mlp_w1_bf, mlp_w2_bf, l,
                       final_g, seq, final=(l == depth - 1))
    return x2.reshape(bsz, seq, d)
```

```python
import functools

import jax
import jax.numpy as jnp
from jax import lax
from jax.experimental import pallas as pl
from jax.experimental.pallas import tpu as pltpu

F32 = jnp.float32
BF16 = jnp.bfloat16
HIGHEST = lax.Precision.HIGHEST

D_MODEL = 1024
NSA_HEADS = 8
NSA_GROUPS = 2
NSA_REP = NSA_HEADS // NSA_GROUPS
NSA_DH = 64
CMP_LEN = 32
CMP_STRIDE = 16
SEL_LEN = 64
SEL_SHIFT = 6
TOP_N = 16
WINDOW = 512
FORCE_BONUS = 1.0e4
POOL_WINDOWS = (2, 4, 8, 16)
POOL_GROUP_DIM = 128
POOL_W = len(POOL_WINDOWS) * POOL_GROUP_DIM
DN_HEADS = 4
DN_DK = 128
DN_W = DN_HEADS * DN_DK
DN_CONV = 4
EPS = 1e-6
NEG = -1e30
NSA_Q_W = NSA_HEADS * NSA_DH
NSA_KV_W = NSA_GROUPS * NSA_DH
IN_SIZES = (NSA_Q_W, NSA_KV_W, NSA_KV_W, NSA_KV_W, NSA_KV_W, NSA_KV_W, NSA_KV_W,
            3 * NSA_HEADS, POOL_W, 3 * DN_W, DN_W, DN_HEADS, DN_HEADS, 3 * D_MODEL)

LANES = 128
LANE_SHIFT = 7
SUBLANES = 8
VMEM_LIMIT_BYTES = 48 * 1024 * 1024

Q_TILE = 128
SELECT_BUCKETS = 4
SELECT_TILES = 4
ATTN_Q_TILE = 128
ATTN_TILES = 2
ATTN_GROUPS = 4
KV_CHUNK = 256
DN_CHUNK = 128
DN_BATCH = 4
PROJ_TM = 1024
MLP_TM = 1024
MLP_TF = 2048
MERGE_TM = 1024
POOL_TS = 1024
POOL_HALO = 16
DN_HALO = 8

COL_Q = 0
COL_POOL = 512
COL_Z = 1024
COL_QKV = 1536
COL_GATE = 3072
COL_KC = 3328
COL_VC = 3456
N_MAIN = 3584
PROJ_MAIN_TM = 1024
N_MG = 3072
LANE_BETA = 16
LANE_DECAY = 20
KVB_KS = 0
KVB_VS = 2
KVB_KW = 4
KVB_VW = 6
KVB_END = 8
N_KV = KVB_END * LANES
LANE_POS_HI = NSA_DH
LANE_POS_LO = NSA_DH + 1


def _cparams(sem):
    return pltpu.CompilerParams(dimension_semantics=sem, vmem_limit_bytes=VMEM_LIMIT_BYTES)


def _silu(v):
    return v * jax.nn.sigmoid(v)


def _mm(a, b):
    return jnp.dot(a.astype(BF16), b.astype(BF16), preferred_element_type=F32)


def _mm_nt(a, b):
    return lax.dot_general(a.astype(BF16), b.astype(BF16), (((1,), (1,)), ((), ())), preferred_element_type=F32)


def _mm_tn(a, b):
    return lax.dot_general(a.astype(BF16), b.astype(BF16), (((0,), (0,)), ((), ())), preferred_element_type=F32)


def _mod_kernel(c_ref, w_ref, b_ref, o_ref):
    cond = _silu(c_ref[...])
    o_ref[...] = jnp.dot(cond, w_ref[...], preferred_element_type=F32, precision=HIGHEST) + b_ref[...]


def _mod_call(c_pad, w, b, layer):
    _, d, n = w.shape
    tn = 1536
    return pl.pallas_call(
        _mod_kernel,
        grid=(n // tn,),
        in_specs=[pl.BlockSpec((SUBLANES, d), lambda j: (0, 0)),
                  pl.BlockSpec((None, d, tn), lambda j: (layer, 0, j)),
                  pl.BlockSpec((None, 1, tn), lambda j: (layer, 0, j))],
        out_specs=pl.BlockSpec((SUBLANES, tn), lambda j: (0, j)),
        out_shape=jax.ShapeDtypeStruct((SUBLANES, n), F32),
        compiler_params=_cparams(("parallel",)),
        name="mod",
    )(c_pad, w, b)


def _norm_mod(x, g, sc, sh):
    ms = jnp.mean(x * x, axis=-1, keepdims=True)
    return (x * lax.rsqrt(ms + EPS) * g) * (1.0 + sc) + sh


def _proj_kernel(x_ref, g_ref, sc_ref, sh_ref, w_ref, o_ref):
    h = _norm_mod(x_ref[...], g_ref[...], sc_ref[0], sh_ref[0]).astype(BF16)
    o_ref[...] = jnp.dot(h, w_ref[...], preferred_element_type=F32).astype(o_ref.dtype)


def _proj_kv_kernel(x_ref, g_ref, sc_ref, sh_ref, w_ref, o_ref, *, tiles_per_seq):
    h = _norm_mod(x_ref[...], g_ref[...], sc_ref[0], sh_ref[0]).astype(BF16)
    full = jnp.dot(h, w_ref[...], preferred_element_type=F32)
    o_ref[:, :N_MG] = full[:, :N_MG].astype(o_ref.dtype)
    acc = full[:, N_MG:]
    tm, n = acc.shape
    col = lax.broadcasted_iota(jnp.int32, (1, n), 1)
    blk = jnp.right_shift(col, LANE_SHIFT)
    lane = col & (LANES - 1)
    is_key = ((blk >= KVB_KS) & (blk < KVB_VS)) | ((blk >= KVB_KW) & (blk < KVB_VW))
    is_val = ((blk >= KVB_VS) & (blk < KVB_KW)) | ((blk >= KVB_VW) & (blk < KVB_END))
    ones_row = jnp.where(is_val & (lane >= NSA_DH), 1.0, 0.0)
    hi_row = jnp.where(is_key & (lane == LANE_POS_HI), 1.0, 0.0)
    lo_row = jnp.where(is_key & (lane == LANE_POS_LO), 1.0, 0.0)
    pos = (pl.program_id(0) % tiles_per_seq) * tm + lax.broadcasted_iota(jnp.int32, (tm, 1), 0)
    pos_hi = jnp.right_shift(pos, LANE_SHIFT).astype(F32)
    pos_lo = (pos & (LANES - 1)).astype(F32)
    o_ref[:, N_MG:] = (acc + ones_row + pos_hi * hi_row + pos_lo * lo_row).astype(o_ref.dtype)


def _proj_specs(tm, d, tn, per_b):
    return [pl.BlockSpec((tm, d), lambda i, j: (i, 0)),
            pl.BlockSpec((1, d), lambda i, j: (0, 0)),
            pl.BlockSpec((1, 1, d), lambda i, j: (i // per_b, 0, 0)),
            pl.BlockSpec((1, 1, d), lambda i, j: (i // per_b, 0, 0)),
            pl.BlockSpec((d, tn), lambda i, j: (0, j), pipeline_mode=pl.Buffered(1))]


def _proj_call(x2, g, sc, sh, w, seq, tm, out_dtype, name):
    t, d = x2.shape
    n = w.shape[1]
    tm = min(tm, seq)
    return pl.pallas_call(
        _proj_kernel,
        grid=(t // tm, 1),
        in_specs=_proj_specs(tm, d, n, seq // tm),
        out_specs=pl.BlockSpec((tm, n), lambda i, j: (i, j)),
        out_shape=jax.ShapeDtypeStruct((t, n), out_dtype),
        compiler_params=_cparams(("parallel", "arbitrary")),
        name=name,
    )(x2, g.reshape(1, d), sc, sh, w)


def _proj_kv_call(x2, g, sc, sh, w, seq):
    t, d = x2.shape
    n = w.shape[1]
    tm = min(PROJ_TM, seq)
    return pl.pallas_call(
        functools.partial(_proj_kv_kernel, tiles_per_seq=seq // tm),
        grid=(t // tm, 1),
        in_specs=_proj_specs(tm, d, n, seq // tm),
        out_specs=pl.BlockSpec((tm, n), lambda i, j: (i, j)),
        out_shape=jax.ShapeDtypeStruct((t, n), BF16),
        compiler_params=_cparams(("parallel", "arbitrary")),
        name="proj_kv",
    )(x2, g.reshape(1, d), sc, sh, w)


def _compress_kernel(x_ref, pos_ref, w1_ref, w1g_ref, w2_ref, o_ref):
    nr = x_ref.shape[1] // CMP_STRIDE
    xs = jnp.concatenate([x_ref[0, pl.ds(rho, nr, stride=CMP_STRIDE), :].astype(BF16) for rho in range(CMP_STRIDE)],
                         axis=1)
    posb = jnp.dot(pos_ref[0], w1_ref[0], preferred_element_type=F32)[0:1]
    for g in range(NSA_GROUPS):
        top = jnp.dot(xs, w1g_ref[0, g, 0], preferred_element_type=F32)
        bot = jnp.dot(xs, w1g_ref[0, g, 1], preferred_element_type=F32)
        hid = top + pltpu.roll(bot, nr - 1, 0) + posb
        o_ref[0, 0, g] = _mm(_silu(hid), w2_ref[0]).astype(o_ref.dtype)


def _compress_call(main3, pos, w1, w1g, w2):
    b, seq, _ = main3.shape
    nr = seq // CMP_STRIDE
    two, l64, hid = w1.shape
    return pl.pallas_call(
        _compress_kernel,
        grid=(two, b),
        in_specs=[pl.BlockSpec((1, seq, LANES), lambda a, i: (i, 0, COL_KC // LANES + a)),
                  pl.BlockSpec((1, SUBLANES, l64), lambda a, i: (a, 0, 0)),
                  pl.BlockSpec((1, l64, hid), lambda a, i: (a, 0, 0)),
                  pl.BlockSpec((1,) + w1g.shape[1:], lambda a, i: (a, 0, 0, 0, 0)),
                  pl.BlockSpec((1, hid, LANES), lambda a, i: (a, 0, 0))],
        out_specs=pl.BlockSpec((1, 1, NSA_GROUPS, nr, LANES), lambda a, i: (a, i, 0, 0, 0)),
        out_shape=jax.ShapeDtypeStruct((two, b, NSA_GROUPS, nr, LANES), BF16),
        compiler_params=_cparams(("parallel", "parallel")),
        name="compress",
    )(main3, pos, w1, w1g, w2)


def _gate_cols(gate_logits, branch):
    sg = jax.nn.sigmoid(gate_logits)
    return [sg[:, 3 * r + branch:3 * r + branch + 1] for r in range(NSA_REP)]


def _stack_heads_dup(q):
    qa = q[:, :LANES]
    qb = q[:, LANES:]
    lo = lax.broadcasted_iota(jnp.int32, qa.shape, 1) < NSA_DH
    z = jnp.zeros_like(qa)
    return jnp.concatenate([jnp.where(lo, qa, z), jnp.where(lo, z, qa),
                            jnp.where(lo, qb, z), jnp.where(lo, z, qb)], axis=0)


def _unstack_heads_dup(o, gates):
    nq = o.shape[0] // NSA_REP
    lo = lax.broadcasted_iota(jnp.int32, (nq, LANES), 1) < NSA_DH
    a = jnp.where(lo, gates[0] * o[0:nq], gates[1] * o[nq:2 * nq])
    b = jnp.where(lo, gates[2] * o[2 * nq:3 * nq], gates[3] * o[3 * nq:])
    return jnp.concatenate([a, b], axis=1)


def _stack_heads_pos(q, slopes):
    qa = q[:, :LANES]
    qb = q[:, LANES:]
    lane = lax.broadcasted_iota(jnp.int32, qa.shape, 1)
    lo = lane < NSA_DH
    heads = [qa, pltpu.roll(qa, NSA_DH, 1), qb, pltpu.roll(qb, NSA_DH, 1)]
    out = []
    for r in range(NSA_REP):
        extra = jnp.where(lane == LANE_POS_HI, slopes[r] * LANES, jnp.where(lane == LANE_POS_LO, slopes[r], 0.0))
        out.append(jnp.where(lo, heads[r], extra))
    return jnp.concatenate(out, axis=0)


def _finish_heads(acc, gates):
    nq = acc.shape[0] // NSA_REP
    o = acc * pltpu.roll(1.0 / acc, NSA_DH, 1)
    lo = lax.broadcasted_iota(jnp.int32, (nq, LANES), 1) < NSA_DH
    hs = [gates[r] * o[r * nq:(r + 1) * nq] for r in range(NSA_REP)]
    a = jnp.where(lo, hs[0], pltpu.roll(hs[1], NSA_DH, 1))
    b = jnp.where(lo, hs[2], pltpu.roll(hs[3], NSA_DH, 1))
    return jnp.concatenate([a, b], axis=1)


def _select_kernel(slopes_ref, q_ref, kc_ref, vc_ref, gates_ref, ov_ref, sel_ref, act_ref, yc_ref):
    step = pl.program_id(1)
    nsteps = pl.num_programs(1)
    nb_all, ncp_all = ov_ref.shape
    for k in range(SELECT_BUCKETS):
        @pl.when((step * SELECT_BUCKETS) // nsteps == k)
        def _(k=k):
            for u in range(SELECT_TILES):
                _select_prefix(slopes_ref, q_ref, kc_ref, vc_ref, gates_ref, ov_ref, sel_ref, act_ref, yc_ref, tile=u,
                               nb=(k + 1) * nb_all // SELECT_BUCKETS, ncp=(k + 1) * ncp_all // SELECT_BUCKETS)


def _select_prefix(slopes_ref, q_ref, kc_ref, vc_ref, gates_ref, ov_ref, sel_ref, act_ref, yc_ref, *, tile, nb, ncp):
    t0 = (pl.program_id(1) * SELECT_TILES + tile) * Q_TILE
    nq = Q_TILE
    rows = slice(tile * nq, (tile + 1) * nq)
    nb_all = ov_ref.shape[0]
    c_end = lax.broadcasted_iota(jnp.int32, (nq, ncp), 1) * CMP_STRIDE + (CMP_LEN - 1)
    tq = t0 + lax.broadcasted_iota(jnp.int32, (nq, ncp), 0)
    vis = c_end <= tq
    any_vis = tq[:, 0:1] >= CMP_LEN - 1
    rel = (c_end[0:1] - t0).astype(F32)
    ov = ov_ref[:nb, :ncp]
    jb = lax.broadcasted_iota(jnp.int32, (nb, nq), 0)
    tql = t0 + lax.broadcasted_iota(jnp.int32, (nb, nq), 1)
    valid = jb * SEL_LEN <= tql
    forced = (jb == 0) | (jb == jnp.right_shift(tql, SEL_SHIFT))
    scores = []
    for g in range(NSA_GROUPS):
        qcols = slice(g * NSA_REP * NSA_DH, (g + 1) * NSA_REP * NSA_DH)
        qs = _stack_heads_dup(q_ref[rows, qcols] * (NSA_DH ** -0.5)).astype(BF16)
        s = _mm_nt(qs, kc_ref[0, g, :ncp, :])
        psum = jnp.zeros((nq, ncp), F32)
        ps = []
        for r in range(NSA_REP):
            m_r = slopes_ref[g * NSA_REP + r]
            sr = jnp.where(vis, s[r * nq:(r + 1) * nq] + m_r * rel, NEG)
            mx = jnp.max(sr, axis=-1, keepdims=True)
            e = jnp.exp(sr - mx)
            den = jnp.sum(e, axis=-1, keepdims=True)
            p = e * jnp.where(any_vis, 1.0 / den, 0.0)
            psum = psum + p
            ps.append(p.astype(BF16))
        o_c = jnp.dot(jnp.concatenate(ps, axis=0), vc_ref[0, g, :ncp, :], preferred_element_type=F32)
        yc_ref[rows, qcols] = _unstack_heads_dup(
            o_c, _gate_cols(gates_ref[rows, g * LANES:(g + 1) * LANES], 0)).astype(yc_ref.dtype)
        p_hi = psum.astype(BF16)
        p_lo = (psum - p_hi.astype(F32)).astype(BF16)
        imp_t = _mm_nt(ov, p_hi) + _mm_nt(ov, p_lo)
        scores.append(jnp.where(valid, imp_t + FORCE_BONUS * forced.astype(F32), NEG))

    for _ in range(min(TOP_N, nb)):
        for g in range(NSA_GROUPS):
            mx = jnp.max(scores[g], axis=0, keepdims=True)
            idx = jnp.min(jnp.where(scores[g] == mx, jb, nb), axis=0, keepdims=True)
            scores[g] = jnp.where(jb == idx, -jnp.inf, scores[g])
    for g in range(NSA_GROUPS):
        sel_t = jnp.where((scores[g] == -jnp.inf) & valid, 1.0, 0.0)
        if nb < nb_all:
            sel_t = jnp.concatenate([sel_t, jnp.zeros((nb_all - nb, nq), F32)], axis=0)
        sel_q = sel_t.T
        sel_ref[0, g, rows, :] = sel_q.astype(sel_ref.dtype)
        act_ref[0, g, tile] = jnp.max(sel_q, axis=0, keepdims=True)


def _select_call(slopes, main, kcmp, vcmp, ov, bsz, seq):
    nqt = seq // Q_TILE
    nsteps = nqt // SELECT_TILES
    tq = Q_TILE * SELECT_TILES
    nb, ncp = ov.shape
    row = lambda b, q: b * nsteps + q
    gw = NSA_GROUPS * LANES
    return pl.pallas_call(
        _select_kernel,
        grid=(bsz, nsteps),
        in_specs=[pl.BlockSpec(memory_space=pltpu.SMEM),
                  pl.BlockSpec((tq, NSA_Q_W), lambda b, q: (row(b, q), COL_Q // NSA_Q_W)),
                  pl.BlockSpec((1, NSA_GROUPS, ncp, LANES), lambda b, q: (b, 0, 0, 0)),
                  pl.BlockSpec((1, NSA_GROUPS, ncp, LANES), lambda b, q: (b, 0, 0, 0)),
                  pl.BlockSpec((tq, gw), lambda b, q: (row(b, q), COL_GATE // gw)),
                  pl.BlockSpec((nb, ncp), lambda b, q: (0, 0))],
        out_specs=[pl.BlockSpec((1, NSA_GROUPS, tq, nb), lambda b, q: (b, 0, q, 0)),
                   pl.BlockSpec((1, NSA_GROUPS, SELECT_TILES, 1, nb), lambda b, q: (b, 0, q, 0, 0)),
                   pl.BlockSpec((tq, NSA_Q_W), lambda b, q: (row(b, q), 0))],
        out_shape=[jax.ShapeDtypeStruct((bsz, NSA_GROUPS, seq, nb), BF16),
                   jax.ShapeDtypeStruct((bsz, NSA_GROUPS, nqt, 1, nb), F32),
                   jax.ShapeDtypeStruct((bsz * seq, NSA_Q_W), BF16)],
        compiler_params=_cparams(("parallel", "parallel")),
        name="select",
    )(slopes, main, kcmp, vcmp, main, ov)


def _attn_kernel(lst_ref, cnt_ref, slopes_ref, q_ref, sel_ref, ks_ref, vs_ref, kw_ref, vw_ref, gates_ref, yc_ref,
                 et_ref, o_ref, s_scr, p_scr, m_scr, a_scr, acc_scr, *, chunk):
    b = pl.program_id(0)
    g = pl.program_id(1)
    step = pl.program_id(2)
    tiles_per_step = ATTN_TILES * ATTN_GROUPS
    nqt = pl.num_programs(2) * tiles_per_step
    nq = ATTN_Q_TILE
    nw = WINDOW + nq
    nch_total = ks_ref.shape[0] // chunk
    slopes = [slopes_ref[g * NSA_REP + r] for r in range(NSA_REP)]

    def values(p, c):
        start = pl.multiple_of(c * chunk, chunk)
        return jnp.dot(p, vs_ref[pl.ds(start, chunk), :], preferred_element_type=F32)

    def row_max(s):
        return jnp.broadcast_to(jnp.max(s, axis=-1, keepdims=True), (s.shape[0], LANES))

    def lane_tile(m, width):
        return jnp.concatenate([m] * (width // LANES), axis=1)

    def masked(s, ok):
        return jnp.concatenate([jnp.where(ok, s[r * nq:(r + 1) * nq], NEG) for r in range(NSA_REP)], axis=0)

    class Group:
        pass

    def setup(k):
        gr = Group()
        gr.tiles = range(ATTN_TILES)
        gr.scr = [(k % 2) * ATTN_TILES + u for u in gr.tiles]
        local = [k * ATTN_TILES + u for u in gr.tiles]
        gr.rws = [slice(v * nq, (v + 1) * nq) for v in local]
        qts = [step * tiles_per_step + v for v in local]
        gr.t0s = [qt * nq for qt in qts]
        gr.qs = [_stack_heads_pos(q_ref[rw, :] * (NSA_DH ** -0.5), slopes).astype(BF16) for rw in gr.rws]
        pens = [((sel_ref[0, 0, rw, :].astype(F32) - 1.0) * (-NEG)).astype(BF16) for rw in gr.rws]
        gr.qx = [jnp.concatenate([gr.qs[u], jnp.concatenate([pens[u]] * NSA_REP, axis=0)], axis=1) for u in gr.tiles]
        gr.bases = [((b * NSA_GROUPS + g) * nqt + qt) * nch_total for qt in qts]
        gr.cnts = [cnt_ref[(b * NSA_GROUPS + g) * nqt + qt] for qt in qts]
        for u in gr.tiles:
            su = gr.scr[u]
            m_scr[su] = jnp.full(m_scr.shape[1:], NEG, F32)
            a_scr[su] = jnp.ones(a_scr.shape[1:], F32)
            acc_scr[su] = jnp.zeros(acc_scr.shape[1:], F32)
            p_scr[su, 1] = jnp.zeros(p_scr.shape[2:], BF16)
        for u in gr.tiles:
            s_scr[gr.scr[u], 0] = logits(gr, u, lst_ref[gr.bases[u]])
        return gr

    def logits(gr, u, c):
        start = pl.multiple_of(c * chunk, chunk)
        kx = jnp.concatenate([ks_ref[pl.ds(start, chunk), :], et_ref[pl.ds(start, chunk), :]], axis=1)
        return _mm_nt(gr.qx[u], kx)

    def flash_step(gr, us, i, slot):
        for u in us:
            su = gr.scr[u]
            acc_scr[su] = a_scr[su] * acc_scr[su] + values(p_scr[su, 1 - slot],
                                                           lst_ref[gr.bases[u] + jnp.maximum(i - 1, 0)])
        for u in us:
            su = gr.scr[u]
            s = s_scr[su, slot]
            m_prev = m_scr[su]
            m_new = jnp.maximum(m_prev, row_max(s))
            a_scr[su] = jnp.exp(m_prev - m_new)
            p_scr[su, slot] = jnp.exp(s - lane_tile(m_new, chunk)).astype(BF16)
            m_scr[su] = m_new
        for u in us:
            s_scr[gr.scr[u], 1 - slot] = logits(gr, u, lst_ref[gr.bases[u] + i + 1])

    def flash_loops(gr):
        def single_steps(us):
            def body(i, carry):
                flash_step(gr, us, i, i & 1)
                return carry
            return body

        def double_steps(us):
            def body(j, carry):
                flash_step(gr, us, 2 * j, 0)
                flash_step(gr, us, 2 * j + 1, 1)
                return carry
            return body

        joint = functools.reduce(jnp.minimum, gr.cnts)
        lax.fori_loop(0, joint // 2, double_steps(list(gr.tiles)), 0)
        lax.fori_loop(joint - (joint & 1), joint, single_steps(list(gr.tiles)), 0)
        for u in gr.tiles:
            lax.fori_loop(joint, gr.cnts[u], single_steps([u]), 0)

    def finish(gr):
        tiles = gr.tiles
        scr = gr.scr
        slots = [gr.cnts[u] & 1 for u in tiles]
        c_diag = [gr.t0s[u] // chunk for u in tiles]
        acc = [a_scr[scr[u]] * acc_scr[scr[u]]
               + values(p_scr[scr[u], 1 - slots[u]], lst_ref[gr.bases[u] + jnp.maximum(gr.cnts[u] - 1, 0)])
               for u in tiles]
        wbase = [pl.multiple_of(jnp.maximum(gr.t0s[u] - WINDOW, 0), nq) for u in tiles]
        s_win = [_mm_nt(gr.qs[u], kw_ref[pl.ds(wbase[u], nw), :]) for u in tiles]
        s_diag = []
        for u in tiles:
            pos = c_diag[u] * chunk + lax.broadcasted_iota(jnp.int32, (nq, chunk), 1)
            tq = gr.t0s[u] + lax.broadcasted_iota(jnp.int32, (nq, chunk), 0)
            s_diag.append(masked(s_scr[scr[u], slots[u]], pos <= tq))
        m_prev = [m_scr[scr[u]] for u in tiles]
        m_new = [jnp.maximum(m_prev[u], row_max(s_diag[u])) for u in tiles]
        acc = [jnp.exp(m_prev[u] - m_new[u]) * acc[u]
               + values(jnp.exp(s_diag[u] - lane_tile(m_new[u], chunk)).astype(BF16), c_diag[u]) for u in tiles]
        p_win = []
        for u in tiles:
            pos = wbase[u] + lax.broadcasted_iota(jnp.int32, (nq, nw), 1)
            tq = gr.t0s[u] + lax.broadcasted_iota(jnp.int32, (nq, nw), 0)
            s = masked(s_win[u], (pos <= tq) & (pos > tq - WINDOW))
            p_win.append(jnp.exp(s - jnp.max(s, axis=-1, keepdims=True)).astype(BF16))
        acc_w = [jnp.dot(p_win[u], vw_ref[pl.ds(wbase[u], nw), :], preferred_element_type=F32) for u in tiles]
        for u in tiles:
            gates = gates_ref[gr.rws[u], :]
            o_ref[gr.rws[u], :] = (yc_ref[gr.rws[u], :].astype(F32) + _finish_heads(acc[u], _gate_cols(gates, 1))
                                   + _finish_heads(acc_w[u], _gate_cols(gates, 2))).astype(o_ref.dtype)

    cur = setup(0)
    for k in range(ATTN_GROUPS):
        flash_loops(cur)
        nxt = setup(k + 1) if k + 1 < ATTN_GROUPS else None
        finish(cur)
        cur = nxt


def _chunk_lists(act, seq, chunk):
    bsz, ngrp = act.shape[:2]
    nqt = seq // ATTN_Q_TILE
    nch = seq // chunk
    active = act.reshape(bsz, ngrp, nqt, ATTN_Q_TILE // Q_TILE, nch, chunk // SEL_LEN).max(axis=(3, 5)) > 0.0
    c_diag = (jnp.arange(nqt) * ATTN_Q_TILE) // chunk
    active = active & (jnp.arange(nch)[None, :] < c_diag[:, None])
    cnt = active.sum(axis=-1).astype(jnp.int32)
    order = jnp.argsort(jnp.logical_not(active), axis=-1, stable=True).astype(jnp.int32)
    lst = jnp.where(jnp.arange(nch) < cnt[..., None], order, c_diag[:, None].astype(jnp.int32))
    return lst.reshape(-1), cnt.reshape(-1)


def _attn_call(lst, cnt, slopes, main, sel, kv, yc, et, bsz, seq, chunk):
    tq = ATTN_Q_TILE * ATTN_TILES * ATTN_GROUPS
    nset = 2 * ATTN_TILES
    nqt = seq // tq
    nb = sel.shape[-1]
    rows = NSA_REP * ATTN_Q_TILE
    row = lambda b, g, q: b * nqt + q
    kvspec = lambda blk: pl.BlockSpec((seq, LANES), lambda b, g, q, *_: (b, N_MG // LANES + blk + g))
    grid_spec = pltpu.PrefetchScalarGridSpec(
        num_scalar_prefetch=2,
        grid=(bsz, NSA_GROUPS, nqt),
        in_specs=[pl.BlockSpec(memory_space=pltpu.SMEM),
                  pl.BlockSpec((tq, 256), lambda b, g, q, *_: (row(b, g, q), COL_Q // 256 + g)),
                  pl.BlockSpec((1, 1, tq, nb), lambda b, g, q, *_: (b, g, q, 0)),
                  kvspec(KVB_KS), kvspec(KVB_VS), kvspec(KVB_KW), kvspec(KVB_VW),
                  pl.BlockSpec((tq, LANES), lambda b, g, q, *_: (row(b, g, q), COL_GATE // LANES + g)),
                  pl.BlockSpec((tq, 256), lambda b, g, q, *_: (row(b, g, q), g)),
                  pl.BlockSpec(et.shape, lambda b, g, q, *_: (0, 0))],
        out_specs=pl.BlockSpec((tq, 256), lambda b, g, q, *_: (row(b, g, q), g)),
        scratch_shapes=[pltpu.VMEM((nset, 2, rows, chunk), F32),
                        pltpu.VMEM((nset, 2, rows, chunk), BF16),
                        pltpu.VMEM((nset, rows, LANES), F32),
                        pltpu.VMEM((nset, rows, LANES), F32),
                        pltpu.VMEM((nset, rows, LANES), F32)],
    )
    return pl.pallas_call(
        functools.partial(_attn_kernel, chunk=chunk),
        grid_spec=grid_spec,
        out_shape=jax.ShapeDtypeStruct((bsz * seq, NSA_Q_W), BF16),
        compiler_params=_cparams(("parallel", "parallel", "parallel")),
        name="attn",
    )(lst, cnt, slopes, main, sel, kv, kv, kv, kv, main, yc, et)


def _pool_kernel(x_ref, halo_ref, w_ref, sc_ref, o_ref):
    s = pl.program_id(1)
    ts = x_ref.shape[0]
    x = x_ref[...]
    halo = jnp.where(s > 0, halo_ref[...], 0.0)
    t = s * ts + lax.broadcasted_iota(jnp.int32, (ts, 1), 0)
    for gi, win in enumerate(POOL_WINDOWS):
        lanes = slice(gi * POOL_GROUP_DIM, (gi + 1) * POOL_GROUP_DIM)
        ext = jnp.concatenate([halo[:, lanes], x[:, lanes]], axis=0)
        acc = ext
        span = 1
        while span < win:
            acc = acc + pltpu.roll(acc, span, 0)
            span *= 2
        cnt = jnp.minimum(t + 1, win).astype(F32)
        y = acc[POOL_HALO:] / cnt - x[:, lanes]
        o_ref[:, lanes] = (_mm(y, w_ref[gi]) * sc_ref[:, lanes]).astype(o_ref.dtype)


def _pool_call(main, w, scale, bsz, seq):
    ts = min(POOL_TS, seq)
    per_b = seq // ts
    halo_per_tile = ts // POOL_HALO
    return pl.pallas_call(
        _pool_kernel,
        grid=(bsz, per_b),
        in_specs=[pl.BlockSpec((ts, POOL_W), lambda b, s: (b * per_b + s, COL_POOL // POOL_W)),
                  pl.BlockSpec((POOL_HALO, POOL_W),
                               lambda b, s: (jnp.maximum((b * per_b + s) * halo_per_tile - 1, 0), COL_POOL // POOL_W)),
                  pl.BlockSpec(w.shape, lambda b, s: (0, 0, 0)),
                  pl.BlockSpec((1, POOL_W), lambda b, s: (0, 0))],
        out_specs=pl.BlockSpec((ts, POOL_W), lambda b, s: (b * per_b + s, 0)),
        out_shape=jax.ShapeDtypeStruct((bsz * seq, POOL_W), BF16),
        compiler_params=_cparams(("parallel", "parallel")),
        name="pool",
    )(main, main, w, scale.reshape(1, POOL_W))


def _unit_lower_inverses(nms, ri, ci):
    c = nms[0].shape[0]
    eye = jnp.where(ri == ci, 1.0, 0.0)
    in8 = jnp.right_shift(ri, 3) == jnp.right_shift(ci, 3)
    n8 = [jnp.where(in8, nm, 0.0) for nm in nms]
    ps = [eye - a for a in n8]
    ms = [_mm(a, a) for a in n8]
    ps = [p + _mm(p, m) for p, m in zip(ps, ms)]
    ms = [_mm(m, m) for m in ms]
    ps = [p + _mm(p, m) for p, m in zip(ps, ms)]
    shift = 3
    while (1 << shift) < c:
        rb = jnp.right_shift(ri, shift)
        cb = jnp.right_shift(ci, shift)
        lower_left = ((rb & 1) == 1) & (cb == rb - 1)
        ts = [_mm(p, jnp.where(lower_left, nm, 0.0)) for p, nm in zip(ps, nms)]
        ps = [p - _mm(t, p) for p, t in zip(ps, ts)]
        shift += 1
    return ps


def _row_cumsum(x, ri):
    span = 1
    while span < x.shape[0]:
        x = x + jnp.where(ri >= span, pltpu.roll(x, span, 0), 0.0)
        span *= 2
    return x


def _dn_kernel(qkv_ref, halo_ref, z_ref, gba_ref, cw_ref, alog_ref, dtb_ref, ng_ref, o_ref, st_scr):
    s = pl.program_id(1)
    nbatch, c = qkv_ref.shape[0], qkv_ref.shape[1]

    @pl.when(s == 0)
    def _():
        st_scr[...] = jnp.zeros(st_scr.shape, F32)

    ri = lax.broadcasted_iota(jnp.int32, (c, c), 0)
    ci = lax.broadcasted_iota(jnp.int32, (c, c), 1)
    tril = ri >= ci
    strict = ri > ci
    cw = cw_ref[...]
    ng = ng_ref[...]
    chains = [(bi, h) for bi in range(nbatch) for h in range(DN_HEADS)]
    qn, kn, kb, vb, gcb, lmask = [], [], [], [], [], []
    for bi in range(nbatch):
        halo = jnp.where(s > 0, halo_ref[bi], 0.0)
        ext = jnp.concatenate([halo, qkv_ref[bi]], axis=0)
        conv = cw[DN_CONV - 1:DN_CONV] * ext
        for tap in range(1, DN_CONV):
            conv = conv + cw[DN_CONV - 1 - tap:DN_CONV - tap] * pltpu.roll(ext, tap, 0)
        act = _silu(conv[DN_HALO:])

        gba = gba_ref[bi]
        beta_all = jax.nn.sigmoid(gba)
        xs = gba + dtb_ref[...]
        softplus = jnp.maximum(xs, 0.0) + jnp.log(1.0 + jnp.exp(-jnp.abs(xs)))
        gc_all = _row_cumsum(-jnp.exp(alog_ref[...]) * softplus, ri)
        gct_all = gc_all.T
        for h in range(DN_HEADS):
            q = act[:, h * DN_DK:(h + 1) * DN_DK]
            k = act[:, DN_W + h * DN_DK:DN_W + (h + 1) * DN_DK]
            v = act[:, 2 * DN_W + h * DN_DK:2 * DN_W + (h + 1) * DN_DK]
            qn.append(q * lax.rsqrt(jnp.sum(q * q, axis=-1, keepdims=True) + EPS) * (DN_DK ** -0.5))
            kn.append(k * lax.rsqrt(jnp.sum(k * k, axis=-1, keepdims=True) + EPS))
            beta = beta_all[:, LANE_BETA + h:LANE_BETA + h + 1]
            gcb.append(jnp.broadcast_to(gc_all[:, LANE_DECAY + h:LANE_DECAY + h + 1], (c, DN_DK)))
            gct = jnp.broadcast_to(gct_all[LANE_DECAY + h:LANE_DECAY + h + 1, :], (c, c))
            lmask.append(jnp.where(tril, jnp.exp(jnp.where(tril, gcb[-1] - gct, 0.0)), 0.0))
            kb.append(kn[-1] * beta)
            vb.append(v * beta)

    n = range(len(chains))
    kk = [_mm_nt(kb[i], kn[i]) for i in n]
    tinv = _unit_lower_inverses([jnp.where(strict, kk[i] * lmask[i], 0.0) for i in n], ri, ci)
    egc = [jnp.exp(gcb[i]) for i in n]
    sol = [_mm(tinv[i], jnp.concatenate([vb[i], kb[i] * egc[i]], axis=1)) for i in n]
    aqk = [_mm_nt(qn[i], kn[i]) * lmask[i] for i in n]
    state = [st_scr[i] for i in n]
    glast = [gcb[i][c - 1:c, :] for i in n]
    v_new = [sol[i][:, :DN_DK] - _mm(sol[i][:, DN_DK:], state[i]) for i in n]
    o_state = [_mm(qn[i] * egc[i], state[i]) for i in n]
    o = [o_state[i] + _mm(aqk[i], v_new[i]) for i in n]
    upd = [_mm_tn(kn[i] * jnp.exp(glast[i] - gcb[i]), v_new[i]) for i in n]
    for i, (bi, h) in enumerate(chains):
        st_scr[i] = state[i] * jnp.exp(glast[i]) + upd[i]
        on = o[i] * lax.rsqrt(jnp.mean(o[i] * o[i], axis=-1, keepdims=True) + EPS) * ng
        o_ref[bi, :, h * DN_DK:(h + 1) * DN_DK] = (
            on * _silu(z_ref[bi, :, h * DN_DK:(h + 1) * DN_DK])).astype(o_ref.dtype)


def _dn_call(main3, conv_w, alog_row, dtb_row, norm_g):
    bsz, seq, _ = main3.shape
    c = DN_CHUNK
    nbt = DN_BATCH if bsz % DN_BATCH == 0 else 1
    halo_per_chunk = c // DN_HALO
    w3 = 3 * DN_W
    return pl.pallas_call(
        _dn_kernel,
        grid=(bsz // nbt, seq // c),
        in_specs=[pl.BlockSpec((nbt, c, w3), lambda b, s: (b, s, COL_QKV // w3)),
                  pl.BlockSpec((nbt, DN_HALO, w3),
                               lambda b, s: (b, jnp.maximum(s * halo_per_chunk - 1, 0), COL_QKV // w3)),
                  pl.BlockSpec((nbt, c, DN_W), lambda b, s: (b, s, COL_Z // DN_W)),
                  pl.BlockSpec((nbt, c, LANES), lambda b, s: (b, s, COL_GATE // LANES)),
                  pl.BlockSpec((SUBLANES, w3), lambda b, s: (0, 0)),
                  pl.BlockSpec((1, LANES), lambda b, s: (0, 0)),
                  pl.BlockSpec((1, LANES), lambda b, s: (0, 0)),
                  pl.BlockSpec((1, DN_DK), lambda b, s: (0, 0))],
        out_specs=pl.BlockSpec((nbt, c, DN_W), lambda b, s: (b, s, 0)),
        out_shape=jax.ShapeDtypeStruct((bsz, seq, DN_W), BF16),
        scratch_shapes=[pltpu.VMEM((nbt * DN_HEADS, DN_DK, DN_DK), F32)],
        compiler_params=_cparams(("parallel", "arbitrary")),
        name="dn",
    )(main3, main3, main3, main3, conv_w, alog_row, dtb_row, norm_g.reshape(1, DN_DK))


def _merge_kernel(ya_ref, yb_ref, yc_ref, mg_ref, x_ref, gt_ref, wa_ref, wb_ref, wc_ref, wo_ref, o_ref):
    d = x_ref.shape[1]
    merged = None
    for i, (y_ref, w_ref) in enumerate(((ya_ref, wa_ref), (yb_ref, wb_ref), (yc_ref, wc_ref))):
        gate = jax.nn.sigmoid(mg_ref[:, i * d:(i + 1) * d].astype(F32))
        term = gate * jnp.dot(y_ref[...], w_ref[...], preferred_element_type=F32)
        merged = term if merged is None else merged + term
    y = jnp.dot(merged.astype(BF16), wo_ref[...], preferred_element_type=F32)
    o_ref[...] = x_ref[...] + gt_ref[0] * y


def _merge_call(ya, yb, yc, mg, x2, gt, wa, wb, wc, wo, seq):
    t, d = x2.shape
    tm = min(MERGE_TM, seq)
    per_b = seq // tm
    full = lambda a: pl.BlockSpec(a.shape, lambda i: (0, 0), pipeline_mode=pl.Buffered(1))
    return pl.pallas_call(
        _merge_kernel,
        grid=(t // tm,),
        in_specs=[pl.BlockSpec((tm, ya.shape[1]), lambda i: (i, 0)),
                  pl.BlockSpec((tm, yb.shape[1]), lambda i: (i, 0)),
                  pl.BlockSpec((tm, yc.shape[1]), lambda i: (i, 0)),
                  pl.BlockSpec((tm, N_MG), lambda i: (i, 0)),
                  pl.BlockSpec((tm, d), lambda i: (i, 0)),
                  pl.BlockSpec((1, 1, d), lambda i: (i // per_b, 0, 0)),
                  full(wa), full(wb), full(wc), full(wo)],
        out_specs=pl.BlockSpec((tm, d), lambda i: (i, 0)),
        out_shape=jax.ShapeDtypeStruct((t, d), F32),
        compiler_params=_cparams(("parallel",)),
        name="merge",
    )(ya, yb, yc, mg, x2, gt, wa, wb, wc, wo)


def _mlp_kernel(x_ref, g_ref, sc_ref, sh_ref, gt_ref, w1_ref, w2_ref, fg_ref, o_ref, *, final):
    x = x_ref[...]
    h = _norm_mod(x, g_ref[...], sc_ref[0], sh_ref[0]).astype(BF16)
    acc = None
    for c0 in range(0, w1_ref.shape[1], MLP_TF):
        a = jnp.maximum(jnp.dot(h, w1_ref[:, c0:c0 + MLP_TF], preferred_element_type=F32), 0.0)
        part = jnp.dot((a * a).astype(BF16), w2_ref[c0:c0 + MLP_TF, :], preferred_element_type=F32)
        acc = part if acc is None else acc + part
    y = x + gt_ref[0] * acc
    if final:
        ms = jnp.mean(y * y, axis=-1, keepdims=True)
        y = y * lax.rsqrt(ms + EPS) * fg_ref[...]
    o_ref[...] = y


def _mlp_call(x2, g, sc, sh, gt, w1, w2, layer, fg, seq, final):
    t, d = x2.shape
    f = w1.shape[2]
    assert f % MLP_TF == 0
    tm = min(MLP_TM, seq)
    per_b = seq // tm
    mod = lambda: pl.BlockSpec((1, 1, d), lambda i: (i // per_b, 0, 0))
    resident = pl.Buffered(1)
    return pl.pallas_call(
        functools.partial(_mlp_kernel, final=final),
        grid=(t // tm,),
        in_specs=[pl.BlockSpec((tm, d), lambda i: (i, 0)),
                  pl.BlockSpec((1, d), lambda i: (0, 0)),
                  mod(), mod(), mod(),
                  pl.BlockSpec((None, d, f), lambda i: (layer, 0, 0), pipeline_mode=resident),
                  pl.BlockSpec((None, f, d), lambda i: (layer, 0, 0), pipeline_mode=resident),
                  pl.BlockSpec((1, d), lambda i: (0, 0))],
        out_specs=pl.BlockSpec((tm, d), lambda i: (i, 0)),
        out_shape=jax.ShapeDtypeStruct((t, d), F32),
        compiler_params=_cparams(("parallel",)),
        name="mlp",
    )(x2, g.reshape(1, d), sc, sh, gt, w1, w2, fg.reshape(1, d))


def _split_w_in(w_in):
    offs = [0]
    for sz in IN_SIZES:
        offs.append(offs[-1] + sz)
    return [w_in[:, offs[i]:offs[i + 1]] for i in range(len(IN_SIZES))]


def _layout_w_in(w_in):
    nq, nkc, nvc, nks, nvs, nkw, nvw, ngate, pin, dqkv, dz, dbeta, da, mg = _split_w_in(w_in)
    d = w_in.shape[0]
    per_g = 3 * NSA_REP
    zeros = lambda n: jnp.zeros((d, n), w_in.dtype)
    gate0 = jnp.concatenate([ngate[:, :per_g], zeros(LANE_BETA - per_g), dbeta, da,
                             zeros(LANES - LANE_DECAY - DN_HEADS)], axis=1)
    gate1 = jnp.concatenate([ngate[:, per_g:], zeros(LANES - per_g)], axis=1)
    main = jnp.concatenate([nq, pin, dz, dqkv, gate0, gate1, nkc, nvc], axis=1)

    def per_group(w):
        pad = zeros(LANES - NSA_DH)
        return jnp.concatenate([w[:, :NSA_DH], pad, w[:, NSA_DH:], pad], axis=1)

    kv = jnp.concatenate([per_group(nks), per_group(nvs), per_group(nkw), per_group(nvw)], axis=1)
    return main.astype(BF16), jnp.concatenate([mg, kv], axis=1).astype(BF16)


def _group_lane_w1(w1):
    two, _, hid = w1.shape
    halves = w1.reshape(two, 2, CMP_STRIDE, NSA_DH, hid)
    out = []
    for g in range(NSA_GROUPS):
        padded = jnp.zeros((two, 2, CMP_STRIDE, LANES, hid), w1.dtype).at[:, :, :, g * NSA_DH:(g + 1) * NSA_DH].set(halves)
        out.append(padded.reshape(two, 2, CMP_STRIDE * LANES, hid))
    return jnp.stack(out, axis=1)


def _nsa_constants(seq):
    nb = seq // SEL_LEN
    ncp = seq // CMP_STRIDE
    j = jnp.arange(nb)[:, None] * SEL_LEN
    i = jnp.arange(ncp)[None, :] * CMP_STRIDE
    overlap = jnp.clip(jnp.minimum(j + SEL_LEN, i + CMP_LEN) - jnp.maximum(j, i), 0) // CMP_STRIDE
    block_onehot = (jnp.arange(seq)[:, None] // SEL_LEN) == jnp.arange(nb)[None, :]
    return overlap.astype(BF16), block_onehot.astype(BF16)


def kernel(x, c, ada_w, ada_b, norm1_g, norm2_g, w_in, phi_k1, phi_k2, phi_v1, phi_v2, pos_k, pos_v, pool_w, pool_scale, dn_conv_w, dn_A_log, dn_dt_bias, dn_norm_g, w_branch_nsa, w_branch_pool, w_branch_dn, w_out, mlp_w1, mlp_w2, final_g):
    bsz, seq, d = x.shape
    assert d == D_MODEL and seq % max(PROJ_TM, MLP_TM, POOL_TS) == 0 and seq % KV_CHUNK == 0
    assert seq % (ATTN_Q_TILE * ATTN_TILES * ATTN_GROUPS) == 0
    assert (seq // Q_TILE) % (SELECT_BUCKETS * SELECT_TILES) == 0
    assert DN_CHUNK == DN_DK == LANES
    t = bsz * seq
    depth = ada_w.shape[0]
    chunk = KV_CHUNK

    slopes = 2.0 ** (-(8.0 / NSA_HEADS) * (jnp.arange(NSA_HEADS, dtype=F32) + 1.0))
    overlap, block_onehot = _nsa_constants(seq)
    c_pad = jnp.zeros((SUBLANES, d), F32).at[:bsz].set(c)
    ada_b3 = ada_b.reshape(depth, 1, ada_b.shape[1])
    mlp_w1_bf, mlp_w2_bf = mlp_w1.astype(BF16), mlp_w2.astype(BF16)
    x2 = x.reshape(t, d)

    for l in range(depth):
        mod = _mod_call(c_pad, ada_w, ada_b3, l)[:bsz]
        sh1, sc1, gt1, sh2, sc2, gt2 = [m.reshape(bsz, 1, d) for m in jnp.split(mod, 6, axis=-1)]

        w_main, w_mgkv = _layout_w_in(w_in[l])
        main = _proj_call(x2, norm1_g[l], sc1, sh1, w_main, seq, PROJ_MAIN_TM, F32, "proj_main")
        kv = mg = _proj_kv_call(x2, norm1_g[l], sc1, sh1, w_mgkv, seq)

        main3 = main.reshape(bsz, seq, N_MAIN)
        pos = jnp.zeros((2, SUBLANES, CMP_LEN * NSA_DH), F32).at[:, 0].set(
            jnp.stack([pos_k[l].reshape(-1), pos_v[l].reshape(-1)])).astype(BF16)
        w1 = jnp.stack([phi_k1[l], phi_v1[l]]).astype(BF16)
        w2 = jnp.stack([phi_k2[l], phi_v2[l]])
        w2 = jnp.concatenate([w2, w2], axis=-1).astype(BF16)
        cmp_kv = _compress_call(main3, pos, w1, _group_lane_w1(w1), w2)

        sel, act, yc = _select_call(slopes, main, cmp_kv[0], cmp_kv[1], overlap, bsz, seq)
        lst, cnt = _chunk_lists(act, seq, chunk)
        y_a = _attn_call(lst, cnt, slopes, main, sel, kv, yc, block_onehot, bsz, seq, chunk)

        y_b = _pool_call(main, pool_w[l].astype(BF16), pool_scale[l], bsz, seq)

        conv_w = jnp.zeros((SUBLANES, 3 * DN_W), F32).at[:DN_CONV].set(dn_conv_w[l])
        lane_row = lambda v: jnp.zeros((1, LANES), F32).at[0, LANE_DECAY:LANE_DECAY + DN_HEADS].set(v)
        y_c = _dn_call(main3, conv_w, lane_row(dn_A_log[l]), lane_row(dn_dt_bias[l]),
                       dn_norm_g[l]).reshape(t, DN_W)

        x2 = _merge_call(y_a, y_b, y_c, mg, x2, gt1,
                         w_branch_nsa[l].astype(BF16), w_branch_pool[l].astype(BF16),
                         w_branch_dn[l].astype(BF16), w_out[l].astype(BF16), seq)
        x2 = _mlp_call(x2, norm2_g[l], sc2, sh2, gt2, mlp_w1_bf, mlp_w2_bf, l,
                       final_g, seq, final=(l == depth - 1))
    return x2.reshape(bsz, seq, d)
```
